```python
import jax, jax.numpy as jnp
from jax import lax
import numpy as np

D_MODEL = 1024
BATCH = 32
SEQ = 256
DEPTH = 1
DEC_BATCH = 4
DEC_SEQ = 1024
PAST_LEN = 256

GRID_W = 64
N_HEADS_M = 8
HEAD_DIM_M = 128
D_MLSTM = N_HEADS_M * HEAD_DIM_M
CHUNK = 64
POOL_WINDOWS = (2, 4, 8, 16)
N_POOL_GROUPS = 4
POOL_GROUP_DIM = 128
D_POOL = N_POOL_GROUPS * POOL_GROUP_DIM
N_EXPERT_GROUPS = 4
EXPERTS_PER_GROUP = 4
N_EXPERTS = N_EXPERT_GROUPS * EXPERTS_PER_GROUP
D_EXPERT = 512
TOP_K_IN_GROUP = 2
IN_SPLITS = (D_MLSTM, D_MLSTM, D_MLSTM, D_MLSTM, 4 * N_HEADS_M, D_POOL, D_MODEL, D_MODEL)
D_IN = 4 * D_MLSTM + 4 * N_HEADS_M + D_POOL + 2 * D_MODEL
GATE_OFFSET = 4 * D_MLSTM
EPS = 1e-6
NEG = -1e30

kernel_name = "hybrid_mlstm_pool_hiermoe_diffusion_step"


def _rmsnorm(x, g):
    xf = x.astype(jnp.float32)
    y = xf * lax.rsqrt(jnp.mean(xf * xf, axis=-1, keepdims=True) + EPS) * g.astype(jnp.float32)
    return y.astype(x.dtype)


def _mlstm_scan(q, k, v, log_i, log_f, C0, n0, m0):
    B, H, T, Dh = q.shape
    nc = T // CHUNK

    def to_chunks(a):
        return jnp.moveaxis(a.reshape(a.shape[:2] + (nc, CHUNK) + a.shape[3:]), 2, 0)

    xs = (to_chunks(q), to_chunks(k), to_chunks(v), to_chunks(log_i), to_chunks(log_f))
    lower = jnp.tril(jnp.ones((CHUNK, CHUNK), dtype=bool))

    def step(carry, inp):
        C, n, m = carry
        qc, kc, vc, ic, fc = inp
        b = jnp.cumsum(fc, axis=-1)
        dmat = jnp.where(lower, b[..., :, None] - b[..., None, :] + ic[..., None, :], NEG)
        inter = b + m[..., None]
        m_t = jnp.maximum(inter, jnp.max(dmat, axis=-1))
        w_intra = jnp.exp(dmat - m_t[..., None])
        w_inter = jnp.exp(inter - m_t)
        s = jnp.einsum('bhtd,bhsd->bhts', qc, kc) * w_intra
        num = w_inter[..., None] * jnp.einsum('bhtd,bhde->bhte', qc, C) + jnp.einsum('bhts,bhse->bhte', s, vc)
        den = w_inter * jnp.einsum('bhtd,bhd->bht', qc, n) + jnp.sum(s, axis=-1)
        h = num / jnp.maximum(jnp.abs(den), jnp.exp(-m_t))[..., None]
        b_last = b[..., -1]
        m_new = m_t[..., -1]
        w_state = jnp.exp(b_last[..., None] - b + ic - m_new[..., None])
        decay = jnp.exp(b_last + m - m_new)
        C_new = decay[..., None, None] * C + jnp.einsum('bhs,bhsd,bhse->bhde', w_state, kc, vc)
        n_new = decay[..., None] * n + jnp.einsum('bhs,bhsd->bhd', w_state, kc)
        return (C_new, n_new, m_new), h

    (C, n, m), hs = lax.scan(step, (C0, n0, m0), xs)
    h = jnp.moveaxis(hs, 0, 2).reshape(B, H, T, Dh)
    return h, C, n, m


def _mlstm_branch(q_raw, k_raw, v_raw, o_raw, gates, gn_gain, C0, n0, m0):
    B, T, _ = q_raw.shape

    def heads(a):
        return a.reshape(B, T, N_HEADS_M, HEAD_DIM_M).transpose(0, 2, 1, 3).astype(jnp.float32)

    q = heads(q_raw)
    k = heads(k_raw) * (HEAD_DIM_M ** -0.5)
    v = heads(v_raw)
    g = gates.astype(jnp.float32).transpose(0, 2, 1)
    i_f, f_f, i_b, f_b = jnp.split(g, 4, axis=1)
    C0 = C0.astype(jnp.float32); n0 = n0.astype(jnp.float32); m0 = m0.astype(jnp.float32)
    h_f, C_f, n_f, m_f = _mlstm_scan(q, k, v, i_f, jax.nn.log_sigmoid(f_f), C0[:, 0], n0[:, 0], m0[:, 0])

    def flip(a):
        return jnp.flip(a, axis=2)

    h_b, C_b, n_b, m_b = _mlstm_scan(flip(q), flip(k), flip(v), flip(i_b), flip(jax.nn.log_sigmoid(f_b)),
                                     C0[:, 1], n0[:, 1], m0[:, 1])
    h = h_f + flip(h_b)
    h = h * lax.rsqrt(jnp.mean(h * h, axis=-1, keepdims=True) + EPS)
    h = h.transpose(0, 2, 1, 3).reshape(B, T, D_MLSTM) * gn_gain.astype(jnp.float32)
    y = (h * jax.nn.sigmoid(o_raw.astype(jnp.float32))).astype(q_raw.dtype)
    C = jnp.stack([C_f, C_b], axis=1)
    n = jnp.stack([n_f, n_b], axis=1)
    m = jnp.stack([m_f, m_b], axis=1)
    return y, C, n, m


def _window_bounds(size, w):
    idx = jnp.arange(size)
    lo = jnp.clip(idx - w // 2, 0, size)
    hi = jnp.clip(idx - w // 2 + w, 0, size)
    return lo, hi


def _pool_1d(u, w):
    T = u.shape[1]
    cs = jnp.pad(jnp.cumsum(u, axis=1), ((0, 0), (1, 0), (0, 0)))
    lo, hi = _window_bounds(T, w)
    s = jnp.take(cs, hi, axis=1) - jnp.take(cs, lo, axis=1)
    return s / (hi - lo).astype(jnp.float32)[None, :, None]


def _pool_2d(g, w):
    R, W = g.shape[1], g.shape[2]
    integ = jnp.pad(jnp.cumsum(jnp.cumsum(g, axis=1), axis=2), ((0, 0), (1, 0), (1, 0), (0, 0)))
    rl, rh = _window_bounds(R, w)
    cl, ch = _window_bounds(W, w)

    def corner(ri, ci):
        return jnp.take(jnp.take(integ, ri, axis=1), ci, axis=2)

    s = corner(rh, ch) - corner(rl, ch) - corner(rh, cl) + corner(rl, cl)
    cnt = ((rh - rl)[:, None] * (ch - cl)[None, :]).astype(jnp.float32)
    return s / cnt[None, :, :, None]


def _pool_branch(u, w_pool_grp, pool_scale, rows):
    B, T, _ = u.shape
    uf = u.astype(jnp.float32)
    outs = []
    for gi, w in enumerate(POOL_WINDOWS):
        ug = uf[..., gi * POOL_GROUP_DIM:(gi + 1) * POOL_GROUP_DIM]
        if rows is None:
            pooled = _pool_1d(ug, w)
        else:
            pooled = _pool_2d(ug.reshape(B, rows, GRID_W, POOL_GROUP_DIM), w).reshape(B, T, POOL_GROUP_DIM)
        outs.append(pooled - ug)
    p = jnp.stack(outs, axis=2).astype(u.dtype)
    y = jnp.einsum('btgc,gcd->btgd', p, w_pool_grp).reshape(B, T, D_POOL)
    return y * pool_scale


def _hier_moe(h, w_router_grp, b_router_grp, w_router_exp, b_router_exp, w_exp_gate, w_exp_up, w_exp_down):
    B, T, D = h.shape
    ht = h.reshape(B * T, D)
    hf = ht.astype(jnp.float32)
    lg = hf @ w_router_grp.astype(jnp.float32) + b_router_grp.astype(jnp.float32)
    pg = jax.nn.softmax(lg, axis=-1)
    grp_oh = jax.nn.one_hot(jnp.argmax(lg, axis=-1), N_EXPERT_GROUPS, dtype=jnp.float32)
    p_top = jnp.sum(pg * grp_oh, axis=-1, keepdims=True)
    le = jnp.einsum('nd,dge->nge', hf, w_router_exp.astype(jnp.float32)) + b_router_exp.astype(jnp.float32)
    le_sel = jnp.einsum('nge,ng->ne', le, grp_oh)
    top_v, top_i = lax.top_k(le_sel, TOP_K_IN_GROUP)
    w2 = jax.nn.softmax(top_v, axis=-1) * p_top
    w_in_grp = jnp.sum(jax.nn.one_hot(top_i, EXPERTS_PER_GROUP, dtype=jnp.float32) * w2[..., None], axis=1)
    w_dense = (grp_oh[:, :, None] * w_in_grp[:, None, :]).reshape(B * T, N_EXPERTS)
    a = jnp.einsum('nd,edf->nef', ht, w_exp_gate)
    u = jnp.einsum('nd,edf->nef', ht, w_exp_up)
    act = jax.nn.silu(a) * u * w_dense.astype(ht.dtype)[..., None]
    out = jnp.einsum('nef,efd->nd', act, w_exp_down)
    return out.reshape(B, T, D)


def _layer(x, mod, C0, n0, m0, rows, g_norm1, w_in, b_in, gn_gain, w_pool_grp, pool_scale,
           w_proj_a, w_proj_b, w_out, g_norm2, w_router_grp, b_router_grp, w_router_exp,
           b_router_exp, w_exp_gate, w_exp_up, w_exp_down):
    shift1, scale1, gate1, shift2, scale2, gate2 = [m[:, None, :] for m in jnp.split(mod, 6, axis=-1)]
    h = _rmsnorm(x, g_norm1) * (1 + scale1) + shift1
    proj = h @ w_in + b_in
    split_at = [int(s) for s in np.cumsum(IN_SPLITS)[:-1]]
    q_raw, k_raw, v_raw, o_raw, gates, u_pool, g_a, g_b = jnp.split(proj, split_at, axis=-1)
    y_a, C, n, m = _mlstm_branch(q_raw, k_raw, v_raw, o_raw, gates, gn_gain, C0, n0, m0)
    y_b = _pool_branch(u_pool, w_pool_grp, pool_scale, rows)
    merged = jax.nn.sigmoid(g_a) * (y_a @ w_proj_a) + jax.nn.sigmoid(g_b) * (y_b @ w_proj_b)
    x = x + gate1 * (merged @ w_out)
    h2 = _rmsnorm(x, g_norm2) * (1 + scale2) + shift2
    x = x + gate2 * _hier_moe(h2, w_router_grp, b_router_grp, w_router_exp, b_router_exp,
                              w_exp_gate, w_exp_up, w_exp_down)
    return x, C, n, m


def setup_inputs(seed: int = 0) -> dict:
    key = jax.random.key(seed)
    ks = jax.random.split(key, 32)
    f32 = jnp.float32

    def nrm(k, shape, scale):
        return jax.random.normal(k, shape, f32) * scale

    b_in = nrm(ks[10], (DEPTH, D_IN), 0.02)
    fb = jax.random.uniform(ks[11], (DEPTH, 2, N_HEADS_M), f32, 3.0, 6.0)
    b_in = b_in.at[:, GATE_OFFSET + N_HEADS_M:GATE_OFFSET + 2 * N_HEADS_M].add(fb[:, 0])
    b_in = b_in.at[:, GATE_OFFSET + 3 * N_HEADS_M:GATE_OFFSET + 4 * N_HEADS_M].add(fb[:, 1])
    return {
        "x_prompt": nrm(ks[0], (BATCH, SEQ, D_MODEL), 1.0),
        "x_sample": nrm(ks[1], (DEC_BATCH, DEC_SEQ, D_MODEL), 1.0),
        "state_C": nrm(ks[2], (DEC_BATCH, DEPTH, 2, N_HEADS_M, HEAD_DIM_M, HEAD_DIM_M), 0.1),
        "state_n": nrm(ks[3], (DEC_BATCH, DEPTH, 2, N_HEADS_M, HEAD_DIM_M), 0.1),
        "state_m": nrm(ks[4], (DEC_BATCH, DEPTH, 2, N_HEADS_M), 1.0),
        "c": nrm(ks[5], (DEC_BATCH, D_MODEL), 1.0),
        "c_ctx": nrm(ks[6], (D_MODEL,), 1.0),
        "w_mod": nrm(ks[7], (DEPTH, D_MODEL, 6 * D_MODEL), 0.5 * D_MODEL ** -0.5),
        "b_mod": nrm(ks[8], (DEPTH, 6 * D_MODEL), 0.02),
        "g_norm1": 1.0 + nrm(ks[9], (DEPTH, D_MODEL), 0.02),
        "w_in": nrm(ks[12], (DEPTH, D_MODEL, D_IN), D_MODEL ** -0.5),
        "b_in": b_in,
        "gn_gain": 1.0 + nrm(ks[13], (DEPTH, D_MLSTM), 0.02),
        "w_pool_grp": nrm(ks[14], (DEPTH, N_POOL_GROUPS, POOL_GROUP_DIM, POOL_GROUP_DIM), POOL_GROUP_DIM ** -0.5),
        "pool_scale": 1.0 + nrm(ks[15], (DEPTH, D_POOL), 0.1),
        "w_proj_a": nrm(ks[16], (DEPTH, D_MLSTM, D_MODEL), D_MLSTM ** -0.5),
        "w_proj_b": nrm(ks[17], (DEPTH, D_POOL, D_MODEL), D_POOL ** -0.5),
        "w_out": nrm(ks[18], (DEPTH, D_MODEL, D_MODEL), D_MODEL ** -0.5),
        "g_norm2": 1.0 + nrm(ks[19], (DEPTH, D_MODEL), 0.02),
        "w_router_grp": nrm(ks[20], (DEPTH, D_MODEL, N_EXPERT_GROUPS), D_MODEL ** -0.5),
        "b_router_grp": nrm(ks[21], (DEPTH, N_EXPERT_GROUPS), 0.01),
        "w_router_exp": nrm(ks[22], (DEPTH, D_MODEL, N_EXPERT_GROUPS, EXPERTS_PER_GROUP), D_MODEL ** -0.5),
        "b_router_exp": nrm(ks[23], (DEPTH, N_EXPERT_GROUPS, EXPERTS_PER_GROUP), 0.01),
        "w_exp_gate": nrm(ks[24], (DEPTH, N_EXPERTS, D_MODEL, D_EXPERT), D_MODEL ** -0.5),
        "w_exp_up": nrm(ks[25], (DEPTH, N_EXPERTS, D_MODEL, D_EXPERT), D_MODEL ** -0.5),
        "w_exp_down": nrm(ks[26], (DEPTH, N_EXPERTS, D_EXPERT, D_MODEL), D_EXPERT ** -0.5),
        "g_final": 1.0 + nrm(ks[27], (D_MODEL,), 0.02),
    }


def reference(x_prompt, x_sample, state_C, state_n, state_m, c, c_ctx, w_mod, b_mod, g_norm1,
              w_in, b_in, gn_gain, w_pool_grp, pool_scale, w_proj_a, w_proj_b, w_out, g_norm2,
              w_router_grp, b_router_grp, w_router_exp, b_router_exp, w_exp_gate, w_exp_up,
              w_exp_down, g_final):
    rows = x_sample.shape[1] // GRID_W
    bp = x_prompt.shape[0]
    zC = jnp.zeros((bp, 2, N_HEADS_M, HEAD_DIM_M, HEAD_DIM_M), jnp.float32)
    zn = jnp.zeros((bp, 2, N_HEADS_M, HEAD_DIM_M), jnp.float32)
    zm = jnp.full((bp, 2, N_HEADS_M), NEG, jnp.float32)
    xp, xs = x_prompt, x_sample
    Cs, ns, ms = [], [], []
    for l in range(DEPTH):
        lw = (g_norm1[l], w_in[l], b_in[l], gn_gain[l], w_pool_grp[l], pool_scale[l], w_proj_a[l],
              w_proj_b[l], w_out[l], g_norm2[l], w_router_grp[l], b_router_grp[l], w_router_exp[l],
              b_router_exp[l], w_exp_gate[l], w_exp_up[l], w_exp_down[l])
        mod_ctx = jax.nn.silu(c_ctx)[None, :] @ w_mod[l] + b_mod[l]
        mod_lat = jax.nn.silu(c) @ w_mod[l] + b_mod[l]
        xp, C_l, n_l, m_l = _layer(xp, mod_ctx, zC, zn, zm, None, *lw)
        Cs.append(C_l); ns.append(n_l); ms.append(m_l)
        xs, _, _, _ = _layer(xs, mod_lat, state_C[:, l], state_n[:, l], state_m[:, l], rows, *lw)
    new_state_C = jnp.stack(Cs, axis=1)
    new_state_n = jnp.stack(ns, axis=1)
    new_state_m = jnp.stack(ms, axis=1)
    y_prompt = _rmsnorm(xp, g_final)
    y_sample = _rmsnorm(xs, g_final)
    return (y_prompt, y_sample, new_state_C, new_state_n, new_state_m)
```

```python
import functools

import numpy as np
import jax
import jax.numpy as jnp
from jax import lax
from jax.experimental import pallas as pl
from jax.experimental.pallas import tpu as pltpu

F32 = jnp.float32
BF16 = jnp.bfloat16

D_MODEL = 1024
N_HEADS = 8
HEAD_DIM = 128
GRID_W = 64
POOL_WINDOWS = (2, 4, 8, 16)
N_GROUPS = 4
EXPERTS_PER_GROUP = 4
N_EXPERTS = 16
D_EXPERT = 512
EPS = 1e-6
NEG = -1e30

LANES = 128
CHUNK = 256
N_SLABS = 52
SLAB_Q, SLAB_K, SLAB_V, SLAB_O, SLAB_GA, SLAB_GB, SLAB_POOL = 0, 8, 16, 24, 32, 40, 48
D_MAIN = N_SLABS * LANES
VMEM_LIMIT = 56 * 1024 * 1024


def _cparams(sem):
    return pltpu.CompilerParams(dimension_semantics=sem, vmem_limit_bytes=VMEM_LIMIT)


def _const_spec(shape):
    nd = len(shape)
    return pl.BlockSpec(shape, lambda *_: (0,) * nd, pipeline_mode=pl.Buffered(1))


def _split_bf16(x):
    hi = x.astype(BF16)
    lo = (x - hi.astype(F32)).astype(BF16)
    return hi, lo


def _dot(a, b):
    return jnp.dot(a, b, preferred_element_type=F32)


def _dot3(a, b_hi, b_lo):
    a_hi, a_lo = _split_bf16(a)
    return _dot(a_hi, b_hi) + _dot(a_hi, b_lo) + _dot(a_lo, b_hi)


def _rms_scale(x):
    return lax.rsqrt(jnp.mean(x * x, axis=-1, keepdims=True) + EPS)


def _log_sigmoid(x):
    return jnp.minimum(x, 0.0) - jnp.log1p(jnp.exp(-jnp.abs(x)))


def _mod_kernel(c_ref, whi_ref, wlo_ref, b_ref, o_ref):
    c = c_ref[...]
    s = c * jax.nn.sigmoid(c)
    o_ref[...] = _dot3(s, whi_ref[...], wlo_ref[...]) + b_ref[...]


def _modulation(cvec, w_mod, b_mod):
    tn = 512
    whi, wlo = _split_bf16(w_mod)
    out = pl.pallas_call(
        _mod_kernel,
        out_shape=jax.ShapeDtypeStruct((8, 6 * D_MODEL), F32),
        grid=(6 * D_MODEL // tn,),
        in_specs=[
            pl.BlockSpec((8, D_MODEL), lambda j: (0, 0)),
            pl.BlockSpec((D_MODEL, tn), lambda j: (0, j)),
            pl.BlockSpec((D_MODEL, tn), lambda j: (0, j)),
            pl.BlockSpec((1, tn), lambda j: (0, j)),
        ],
        out_specs=pl.BlockSpec((8, tn), lambda j: (0, j)),
        compiler_params=_cparams(("parallel",)),
        name="modulation",
    )(cvec, whi, wlo, b_mod.reshape(1, -1))
    return out.reshape(8, 6, D_MODEL)


IN_TM = 512
IN_TN = 512


def _inproj_kernel(x_ref, mod_ref, g_ref, w_ref, b_ref, cs_ref, wg_hi_ref, wg_lo_ref, bg_ref,
                   p3_ref, cols_ref):
    x = x_ref[...]
    h = x * _rms_scale(x) * g_ref[...]
    h = h * (1.0 + mod_ref[0, 1:2, :]) + mod_ref[0, 0:1, :]
    hb = h.astype(BF16)
    for n in range(D_MAIN // IN_TN):
        sl = slice(n * IN_TN, (n + 1) * IN_TN)
        acc = (_dot(hb, w_ref[:, sl]) + b_ref[:, sl]) * cs_ref[:, sl]
        accb = acc.astype(BF16)
        for j in range(IN_TN // LANES):
            p3_ref[n * (IN_TN // LANES) + j] = accb[:, j * LANES:(j + 1) * LANES]

    g2 = _dot3(h, wg_hi_ref[...], wg_lo_ref[...]) + bg_ref[...]
    ls = _log_sigmoid(g2[:, :LANES])
    gi = g2[:, LANES:]
    row = lax.broadcasted_iota(jnp.int32, (CHUNK, LANES), 0)
    lane = lax.broadcasted_iota(jnp.int32, (CHUNK, LANES), 1)
    for c in range(IN_TM // CHUNK):
        x0 = ls[c * CHUNK:(c + 1) * CHUNK]
        cf = x0
        cb = x0
        s = 1
        while s < CHUNK:
            cf = cf + jnp.where(row >= s, pltpu.roll(cf, s, 0), 0.0)
            cb = cb + jnp.where(row < CHUNK - s, pltpu.roll(cb, CHUNK - s, 0), 0.0)
            s *= 2
        b = jnp.where(lane < 2 * N_HEADS, cf, cb)
        is_imb = (lane & N_HEADS) != 0
        out = jnp.where(is_imb, gi[c * CHUNK:(c + 1) * CHUNK] - b, b)
        cols_ref[c * CHUNK:(c + 1) * CHUNK, :] = out


def _in_projection(x2d, mod3, mod_row_fn, g_norm1, wmain, bmain, colscale, wg_hi, wg_lo, bg):
    n = x2d.shape[0]
    return pl.pallas_call(
        _inproj_kernel,
        out_shape=(jax.ShapeDtypeStruct((N_SLABS, n, LANES), BF16),
                   jax.ShapeDtypeStruct((n, LANES), F32)),
        grid=(n // IN_TM,),
        in_specs=[
            pl.BlockSpec((IN_TM, D_MODEL), lambda i: (i, 0)),
            pl.BlockSpec((1, 6, D_MODEL), lambda i: (mod_row_fn(i * IN_TM), 0, 0)),
            _const_spec((1, D_MODEL)),
            _const_spec((D_MODEL, D_MAIN)),
            _const_spec((1, D_MAIN)),
            _const_spec((1, D_MAIN)),
            _const_spec((D_MODEL, 2 * LANES)),
            _const_spec((D_MODEL, 2 * LANES)),
            _const_spec((1, 2 * LANES)),
        ],
        out_specs=(pl.BlockSpec((N_SLABS, IN_TM, LANES), lambda i: (0, i, 0)),
                   pl.BlockSpec((IN_TM, LANES), lambda i: (i, 0))),
        compiler_params=_cparams(("parallel",)),
        name="in_projection",
    )(x2d, mod3, g_norm1, wmain, bmain, colscale, wg_hi, wg_lo, bg)


ML_ROWS = 1024
ML_NCH = ML_ROWS // CHUNK


def _pick_col(blk, lane_iota, idx):
    return jnp.sum(jnp.where(lane_iota == idx, blk, 0.0), axis=1, keepdims=True)


def _mlstm_kernel(*refs, carry):
    if carry:
        (q_ref, k_ref, v_ref, o_ref, cols_ref, rf_ref, rb_ref, gn_ref, c0_ref, n0_ref, m0_ref,
         ya_ref, s_scr, h_scr) = refs
    else:
        (q_ref, k_ref, v_ref, o_ref, cols_ref, rf_ref, rb_ref, gn_ref,
         ya_ref, cst_ref, nst_ref, mst_ref, s_scr, h_scr) = refs
    head = pl.program_id(1)
    lane = lax.broadcasted_iota(jnp.int32, (CHUNK, LANES), 1)
    r_idx = lax.broadcasted_iota(jnp.int32, (CHUNK, CHUNK), 0)
    c_idx = lax.broadcasted_iota(jnp.int32, (CHUNK, CHUNK), 1)

    for d in range(2):
        mask = (c_idx <= r_idx) if d == 0 else (c_idx >= r_idx)
        row_ref = rf_ref if d == 0 else rb_ref
        last = CHUNK - 1 if d == 0 else 0
        if carry:
            C = c0_ref[0, 0, d, 0]
            nst = n0_ref[0, d, 0]
            m = m0_ref[0, d, 0]
        order = range(ML_NCH) if d == 0 else range(ML_NCH - 1, -1, -1)
        for step, c in enumerate(order):
            rows = slice(c * CHUNK, (c + 1) * CHUNK)
            q = q_ref[0, rows, :]
            k = k_ref[0, rows, :]
            v = v_ref[0, rows, :]
            if d == 0:
                s = lax.dot_general(q, k, (((1,), (1,)), ((), ())), preferred_element_type=F32)
                s_scr[rows, :] = s
            else:
                s = s_scr[rows, :]
            blk = cols_ref[rows, :]
            bcol = _pick_col(blk, lane, head + 2 * N_HEADS * d)
            imbcol = _pick_col(blk, lane, head + 2 * N_HEADS * d + N_HEADS)
            bmirow = row_ref[0, :, rows]
            dm = jnp.where(mask, bcol - bmirow, NEG)
            a = jnp.max(dm, axis=1, keepdims=True)
            if carry:
                inter = bcol + m
            else:
                inter = bcol + NEG
            m_t = jnp.maximum(inter, a)
            sw = s * jnp.exp(dm - m_t)
            den = jnp.sum(sw, axis=1, keepdims=True)
            num = _dot(sw.astype(BF16), v)
            if carry:
                w_inter = jnp.exp(inter - m_t)
                qn = jnp.sum(q.astype(F32) * nst, axis=1, keepdims=True)
                den = den + w_inter * qn
                num = num + w_inter * _dot(q, C.astype(BF16))
            hdir = num / jnp.maximum(jnp.abs(den), jnp.exp(-m_t))
            if d == 0:
                h_scr[rows, :] = hdir
            else:
                h_scr[rows, :] = h_scr[rows, :] + hdir

            if carry and step == ML_NCH - 1:
                continue
            m_new = m_t[last:last + 1, :]
            b_last = bcol[last:last + 1, :]
            w_col = jnp.exp(b_last - m_new + imbcol)
            w_row = jnp.exp(b_last - m_new - bmirow)
            u = lax.dot_general(k, (w_col * v.astype(F32)).astype(BF16), (((0,), (0,)), ((), ())),
                                preferred_element_type=F32)
            w8 = jnp.broadcast_to(w_row, (8, CHUNK)).astype(BF16)
            nu = _dot(w8, k)[0:1, :]
            if carry:
                decay = jnp.exp(b_last + m - m_new)
                C = decay * C + u
                nst = decay * nst + nu
                m = m_new
            else:
                cst_ref[c, 0, d, 0] = u
                nst_ref[c, d, 0] = nu
                mst_ref[c, d, 0] = m_new

    hh = h_scr[...]
    hn = hh * _rms_scale(hh) * gn_ref[0]
    ya_ref[0] = (hn * jax.nn.sigmoid(o_ref[0].astype(F32))).astype(BF16)


def _mlstm(p3, cols, rows_f, rows_b, gn3, state=None):
    n = p3.shape[1]
    nblk = n // ML_ROWS
    carry = state is not None

    def slab(base):
        return pl.BlockSpec((1, ML_ROWS, LANES), lambda g, h: (base + h, g, 0))

    in_specs = [
        slab(SLAB_Q), slab(SLAB_K), slab(SLAB_V), slab(SLAB_O),
        pl.BlockSpec((ML_ROWS, LANES), lambda g, h: (g, 0)),
        pl.BlockSpec((1, 1, ML_ROWS), lambda g, h: (h, 0, g)),
        pl.BlockSpec((1, 1, ML_ROWS), lambda g, h: (h, 0, g)),
        pl.BlockSpec((1, 1, LANES), lambda g, h: (h, 0, 0)),
    ]
    args = [p3, p3, p3, p3, cols, rows_f, rows_b, gn3]
    ya_shape = jax.ShapeDtypeStruct((N_HEADS, n, LANES), BF16)
    ya_spec = pl.BlockSpec((1, ML_ROWS, LANES), lambda g, h: (h, g, 0))
    if carry:
        c0, n0, m0 = state
        in_specs += [
            pl.BlockSpec((1, 1, 2, 1, HEAD_DIM, HEAD_DIM), lambda g, h: (g, 0, 0, h, 0, 0)),
            pl.BlockSpec((1, 2, 1, 1, HEAD_DIM), lambda g, h: (g, 0, h, 0, 0)),
            pl.BlockSpec((1, 2, 1, 1, 1), lambda g, h: (g, 0, h, 0, 0)),
        ]
        args += [c0, n0, m0]
        out_shape = ya_shape
        out_specs = ya_spec
    else:
        nseq = n // CHUNK
        out_shape = (ya_shape,
                     jax.ShapeDtypeStruct((nseq, 1, 2, N_HEADS, HEAD_DIM, HEAD_DIM), F32),
                     jax.ShapeDtypeStruct((nseq, 2, N_HEADS, 1, HEAD_DIM), F32),
                     jax.ShapeDtypeStruct((nseq, 2, N_HEADS, 1, 1), F32))
        out_specs = (ya_spec,
                     pl.BlockSpec((ML_NCH, 1, 2, 1, HEAD_DIM, HEAD_DIM), lambda g, h: (g, 0, 0, h, 0, 0)),
                     pl.BlockSpec((ML_NCH, 2, 1, 1, HEAD_DIM), lambda g, h: (g, 0, h, 0, 0)),
                     pl.BlockSpec((ML_NCH, 2, 1, 1, 1), lambda g, h: (g, 0, h, 0, 0)))
    return pl.pallas_call(
        functools.partial(_mlstm_kernel, carry=carry),
        out_shape=out_shape,
        grid=(nblk, N_HEADS),
        in_specs=in_specs,
        out_specs=out_specs,
        scratch_shapes=[pltpu.VMEM((ML_ROWS, CHUNK), F32), pltpu.VMEM((ML_ROWS, HEAD_DIM), F32)],
        compiler_params=_cparams(("parallel", "parallel")),
        name="mlstm_latent" if carry else "mlstm_prompt",
    )(*args)


def _pool_kernel(u_ref, a_ref, ic_ref, wpg_ref, ps_ref, yb_ref, *, seq_len, n_seq):
    for sq in range(n_seq):
        rows = slice(sq * seq_len, (sq + 1) * seq_len)
        for g in range(N_GROUPS):
            u = u_ref[g, rows, :]
            pooled = _dot(a_ref[g], u) * ic_ref[g]
            p = (pooled - u.astype(F32)).astype(BF16)
            yb_ref[g, rows, :] = (_dot(p, wpg_ref[g]) * ps_ref[g]).astype(BF16)


def _pool(p3, amat, invc, wpg, pscale, seq_len, n_seq):
    n = p3.shape[1]
    rows = seq_len * n_seq
    return pl.pallas_call(
        functools.partial(_pool_kernel, seq_len=seq_len, n_seq=n_seq),
        out_shape=jax.ShapeDtypeStruct((N_GROUPS, n, LANES), BF16),
        grid=(n // rows,),
        in_specs=[
            pl.BlockSpec((N_GROUPS, rows, LANES), lambda i: (SLAB_POOL // N_GROUPS, i, 0)),
            _const_spec((N_GROUPS, seq_len, seq_len)),
            _const_spec((N_GROUPS, seq_len, LANES)),
            _const_spec((N_GROUPS, LANES, LANES)),
            _const_spec((N_GROUPS, 1, LANES)),
        ],
        out_specs=pl.BlockSpec((N_GROUPS, rows, LANES), lambda i: (0, i, 0)),
        compiler_params=_cparams(("parallel",)),
        name="pool",
    )(p3, amat, invc, wpg, pscale)


def _window_matrix(size, w):
    idx = np.arange(size)
    lo = np.clip(idx - w // 2, 0, size)
    hi = np.clip(idx - w // 2 + w, 0, size)
    s = np.arange(size)[None, :]
    a = ((s >= lo[:, None]) & (s < hi[:, None])).astype(np.float32)
    return a, (hi - lo).astype(np.float32)


def _pool_constants(seq_len, rows_2d):
    mats, invs = [], []
    for w in POOL_WINDOWS:
        if rows_2d is None:
            a, cnt = _window_matrix(seq_len, w)
        else:
            ar, cr = _window_matrix(rows_2d, w)
            ac, cc = _window_matrix(GRID_W, w)
            a = np.kron(ar, ac)
            cnt = np.kron(cr, cc)
        mats.append(a)
        invs.append(np.repeat((1.0 / cnt)[:, None], LANES, axis=1))
    return (jnp.asarray(np.stack(mats), dtype=BF16), jnp.asarray(np.stack(invs), dtype=F32))


MIX_TM = 512


def _mix_kernel(x_ref, mod_ref, ya_ref, yb_ref, ga_ref, gb_ref, wa_ref, wb_ref, wo_ref, x1_ref):
    ya = jnp.concatenate([ya_ref[i] for i in range(N_HEADS)], axis=1)
    yb = jnp.concatenate([yb_ref[i] for i in range(N_GROUPS)], axis=1)
    ga = jnp.concatenate([ga_ref[i] for i in range(8)], axis=1).astype(F32)
    gb = jnp.concatenate([gb_ref[i] for i in range(8)], axis=1).astype(F32)
    merged = jax.nn.sigmoid(ga) * _dot(ya, wa_ref[...]) + jax.nn.sigmoid(gb) * _dot(yb, wb_ref[...])
    out = _dot(merged.astype(BF16), wo_ref[...])
    x1_ref[...] = x_ref[...] + mod_ref[0, 2:3, :] * out


def _mix(x2d, mod3, mod_row_fn, ya, yb, p3, wa, wb, wo):
    n = x2d.shape[0]
    return pl.pallas_call(
        _mix_kernel,
        out_shape=jax.ShapeDtypeStruct((n, D_MODEL), F32),
        grid=(n // MIX_TM,),
        in_specs=[
            pl.BlockSpec((MIX_TM, D_MODEL), lambda i: (i, 0)),
            pl.BlockSpec((1, 6, D_MODEL), lambda i: (mod_row_fn(i * MIX_TM), 0, 0)),
            pl.BlockSpec((N_HEADS, MIX_TM, LANES), lambda i: (0, i, 0)),
            pl.BlockSpec((N_GROUPS, MIX_TM, LANES), lambda i: (0, i, 0)),
            pl.BlockSpec((8, MIX_TM, LANES), lambda i: (SLAB_GA // 8, i, 0)),
            pl.BlockSpec((8, MIX_TM, LANES), lambda i: (SLAB_GB // 8, i, 0)),
            _const_spec((D_MODEL, D_MODEL)),
            _const_spec((N_GROUPS * LANES, D_MODEL)),
            _const_spec((D_MODEL, D_MODEL)),
        ],
        out_specs=pl.BlockSpec((MIX_TM, D_MODEL), lambda i: (i, 0)),
        compiler_params=_cparams(("parallel",)),
        name="mix",
    )(x2d, mod3, ya, yb, p3, p3, wa, wb, wo)


MOE_TM = 1024
ROUTER_LANES = 4 + N_EXPERTS


def _router_weights(logits):
    lane = lax.broadcasted_iota(jnp.int32, logits.shape, 1)
    lanef = lane.astype(F32)
    big = float(LANES)
    lg = jnp.where(lane < N_GROUPS, logits, -jnp.inf)
    mx = jnp.max(lg, axis=1, keepdims=True)
    p_top = 1.0 / jnp.sum(jnp.exp(lg - mx), axis=1, keepdims=True)
    gidx = jnp.min(jnp.where(lg == mx, lanef, big), axis=1, keepdims=True)
    in_grp = (lane >= N_GROUPS) & (lane < ROUTER_LANES) & \
        (jnp.right_shift(lane - N_GROUPS, 2).astype(F32) == gidx)
    le = jnp.where(in_grp, logits, -jnp.inf)
    v1 = jnp.max(le, axis=1, keepdims=True)
    i1 = jnp.min(jnp.where(le == v1, lanef, big), axis=1, keepdims=True)
    le2 = jnp.where(lanef == i1, -jnp.inf, le)
    v2 = jnp.max(le2, axis=1, keepdims=True)
    i2 = jnp.min(jnp.where(le2 == v2, lanef, big), axis=1, keepdims=True)
    e2 = jnp.exp(v2 - v1)
    w1 = p_top / (1.0 + e2)
    w2 = p_top * e2 / (1.0 + e2)
    return jnp.where(lanef == i1, w1, 0.0) + jnp.where(lanef == i2, w2, 0.0)


def _moe_kernel(x1_ref, mod_ref, g2_ref, wr_hi_ref, wr_lo_ref, br_ref, wg_ref, wu_ref, wd_ref, gf_ref,
                y_ref, h2_scr, wd_scr, acc_scr):
    e = pl.program_id(1)

    @pl.when(e == 0)
    def _():
        x1 = x1_ref[...]
        h2 = x1 * _rms_scale(x1) * g2_ref[...]
        h2 = h2 * (1.0 + mod_ref[0, 4:5, :]) + mod_ref[0, 3:4, :]
        h2_scr[...] = h2.astype(BF16)
        logits = _dot3(h2, wr_hi_ref[...], wr_lo_ref[...]) + br_ref[...]
        wd_scr[...] = _router_weights(logits)
        acc_scr[...] = jnp.zeros_like(acc_scr)

    hb = h2_scr[...]
    a = _dot(hb, wg_ref[0])
    u = _dot(hb, wu_ref[0])
    wd = wd_scr[...]
    lane = lax.broadcasted_iota(jnp.int32, wd.shape, 1)
    w_e = jnp.sum(jnp.where(lane == e + N_GROUPS, wd, 0.0), axis=1, keepdims=True)
    act = a * jax.nn.sigmoid(a) * u * w_e
    acc_scr[...] += _dot(act.astype(BF16), wd_ref[0])

    @pl.when(e == N_EXPERTS - 1)
    def _():
        x2 = x1_ref[...] + mod_ref[0, 5:6, :] * acc_scr[...]
        y_ref[...] = x2 * _rms_scale(x2) * gf_ref[...]


def _moe(x1, mod3, mod_row_fn, g_norm2, wr_hi, wr_lo, br, wg, wu, wd, g_final):
    n = x1.shape[0]
    return pl.pallas_call(
        _moe_kernel,
        out_shape=jax.ShapeDtypeStruct((n, D_MODEL), F32),
        grid=(n // MOE_TM, N_EXPERTS),
        in_specs=[
            pl.BlockSpec((MOE_TM, D_MODEL), lambda i, e: (i, 0)),
            pl.BlockSpec((1, 6, D_MODEL), lambda i, e: (mod_row_fn(i * MOE_TM), 0, 0)),
            _const_spec((1, D_MODEL)),
            _const_spec((D_MODEL, LANES)),
            _const_spec((D_MODEL, LANES)),
            _const_spec((1, LANES)),
            pl.BlockSpec((1, D_MODEL, D_EXPERT), lambda i, e: (e, 0, 0)),
            pl.BlockSpec((1, D_MODEL, D_EXPERT), lambda i, e: (e, 0, 0)),
            pl.BlockSpec((1, D_EXPERT, D_MODEL), lambda i, e: (e, 0, 0)),
            _const_spec((1, D_MODEL)),
        ],
        out_specs=pl.BlockSpec((MOE_TM, D_MODEL), lambda i, e: (i, 0)),
        scratch_shapes=[pltpu.VMEM((MOE_TM, D_MODEL), BF16),
                        pltpu.VMEM((MOE_TM, LANES), F32),
                        pltpu.VMEM((MOE_TM, D_MODEL), F32)],
        compiler_params=_cparams(("parallel", "arbitrary")),
        name="moe",
    )(x1, mod3, g_norm2, wr_hi, wr_lo, br, wg, wu, wd, g_final)


def _stream(x, mod3, mod_row_fn, seq_len, rows_2d, state, w):
    bsz, t, _ = x.shape
    n = bsz * t
    x2d = x.reshape(n, D_MODEL)
    p3, cols = _in_projection(x2d, mod3, mod_row_fn, w["g_norm1"], w["wmain"], w["bmain"], w["colscale"],
                              w["wg_hi"], w["wg_lo"], w["bg"])
    rows_f = (-cols[:, N_HEADS:2 * N_HEADS]).T.reshape(N_HEADS, 1, n)
    rows_b = (-cols[:, 3 * N_HEADS:4 * N_HEADS]).T.reshape(N_HEADS, 1, n)
    ml = _mlstm(p3, cols, rows_f, rows_b, w["gn3"], state)
    amat, invc = _pool_constants(seq_len, rows_2d)
    n_seq = max(1, 1024 // seq_len)
    yb = _pool(p3, amat, invc, w["wpg"], w["pscale"], seq_len, n_seq)
    ya = ml if state is not None else ml[0]
    x1 = _mix(x2d, mod3, mod_row_fn, ya, yb, p3, w["wa"], w["wb"], w["wo"])
    y = _moe(x1, mod3, mod_row_fn, w["g_norm2"], w["wr_hi"], w["wr_lo"], w["br"],
             w["wg"], w["wu"], w["wd"], w["g_final"])
    return y.reshape(bsz, t, D_MODEL), ml


def kernel(x_prompt, x_sample, state_C, state_n, state_m, c, c_ctx, w_mod, b_mod, g_norm1, w_in, b_in, gn_gain, w_pool_grp, pool_scale, w_proj_a, w_proj_b, w_out, g_norm2, w_router_grp, b_router_grp, w_router_exp, b_router_exp, w_exp_gate, w_exp_up, w_exp_down, g_final):
    bp, tp, _ = x_prompt.shape
    bs, ts, _ = x_sample.shape
    assert w_mod.shape[0] == 1, "single trunk layer"
    assert tp == CHUNK and ts % CHUNK == 0 and ts == ML_ROWS

    cvec = jnp.zeros((8, D_MODEL), F32).at[0].set(c_ctx).at[1:1 + bs].set(c)
    mod3 = _modulation(cvec, w_mod[0], b_mod[0])

    wi, bi = w_in[0], b_in[0]
    d = D_MODEL
    o_q, o_k, o_v, o_o, o_g, o_p, o_ga, o_gb = 0, d, 2 * d, 3 * d, 4 * d, 4 * d + 32, 4 * d + 32 + 512, 4 * d + 32 + 512 + d
    order = [(o_q, d), (o_k, d), (o_v, d), (o_o, d), (o_ga, d), (o_gb, d), (o_p, 512)]
    wmain = jnp.concatenate([wi[:, s:s + l] for s, l in order], axis=1).astype(BF16)
    bmain = jnp.concatenate([bi[s:s + l] for s, l in order]).reshape(1, D_MAIN)
    colscale = jnp.ones((D_MAIN,), F32).at[SLAB_K * LANES:(SLAB_K + N_HEADS) * LANES].set(HEAD_DIM ** -0.5)
    colscale = colscale.reshape(1, D_MAIN)
    gw = wi[:, o_g:o_g + 32]
    gb = bi[o_g:o_g + 32]
    i_f, f_f, i_b, f_b = (slice(0, 8), slice(8, 16), slice(16, 24), slice(24, 32))
    zw = jnp.zeros((d, 8), F32)
    zb = jnp.zeros((8,), F32)
    wg2 = jnp.concatenate([gw[:, f_f], gw[:, f_f], gw[:, f_b], gw[:, f_b], jnp.zeros((d, LANES - 32), F32),
                           zw, gw[:, i_f], zw, gw[:, i_b], jnp.zeros((d, LANES - 32), F32)], axis=1)
    bg2 = jnp.concatenate([gb[f_f], gb[f_f], gb[f_b], gb[f_b], jnp.zeros((LANES - 32,), F32),
                           zb, gb[i_f], zb, gb[i_b], jnp.zeros((LANES - 32,), F32)]).reshape(1, 2 * LANES)
    wg_hi, wg_lo = _split_bf16(wg2)

    wr = jnp.concatenate([w_router_grp[0], w_router_exp[0].reshape(d, N_EXPERTS),
                          jnp.zeros((d, LANES - ROUTER_LANES), F32)], axis=1)
    br = jnp.concatenate([b_router_grp[0], b_router_exp[0].reshape(N_EXPERTS),
                          jnp.zeros((LANES - ROUTER_LANES,), F32)]).reshape(1, LANES)
    wr_hi, wr_lo = _split_bf16(wr)

    w = dict(
        g_norm1=g_norm1[0].reshape(1, d), wmain=wmain, bmain=bmain, colscale=colscale,
        wg_hi=wg_hi, wg_lo=wg_lo, bg=bg2,
        gn3=gn_gain[0].reshape(N_HEADS, 1, HEAD_DIM),
        wpg=w_pool_grp[0].astype(BF16), pscale=pool_scale[0].reshape(N_GROUPS, 1, LANES),
        wa=w_proj_a[0].astype(BF16), wb=w_proj_b[0].astype(BF16), wo=w_out[0].astype(BF16),
        g_norm2=g_norm2[0].reshape(1, d), wr_hi=wr_hi, wr_lo=wr_lo, br=br,
        wg=w_exp_gate[0].astype(BF16), wu=w_exp_up[0].astype(BF16), wd=w_exp_down[0].astype(BF16),
        g_final=g_final.reshape(1, d),
    )

    y_prompt, (_, cst, nst, mst) = _stream(x_prompt, mod3, lambda r: 0, tp, None, None, w)
    state = (state_C, state_n[:, 0].reshape(bs, 2, N_HEADS, 1, HEAD_DIM),
             state_m[:, 0].reshape(bs, 2, N_HEADS, 1, 1))
    y_sample, _ = _stream(x_sample, mod3, lambda r: 1 + r // ts, ts, ts // GRID_W, state, w)

    new_c = cst
    new_n = nst.reshape(bp, 1, 2, N_HEADS, HEAD_DIM)
    new_m = mst.reshape(bp, 1, 2, N_HEADS)
    return (y_prompt, y_sample, new_c, new_n, new_m)
```

```python
import functools

import numpy as np
import jax
import jax.numpy as jnp
from jax import lax
from jax.experimental import pallas as pl
from jax.experimental.pallas import tpu as pltpu

F32 = jnp.float32
BF16 = jnp.bfloat16

D_MODEL = 1024
N_HEADS = 8
HEAD_DIM = 128
GRID_W = 64
POOL_WINDOWS = (2, 4, 8, 16)
N_GROUPS = 4
EXPERTS_PER_GROUP = 4
N_EXPERTS = 16
D_EXPERT = 512
EPS = 1e-6
NEG = -1e30

LANES = 128
CHUNK = 256
N_SLABS = 52
SLAB_Q, SLAB_K, SLAB_V, SLAB_O, SLAB_GA, SLAB_GB, SLAB_POOL = 0, 8, 16, 24, 32, 40, 48
D_MAIN = N_SLABS * LANES
VMEM_LIMIT = 56 * 1024 * 1024


def _cparams(sem):
    return pltpu.CompilerParams(dimension_semantics=sem, vmem_limit_bytes=VMEM_LIMIT)


def _const_spec(shape):
    nd = len(shape)
    return pl.BlockSpec(shape, lambda *_: (0,) * nd, pipeline_mode=pl.Buffered(1))


def _split_bf16(x):
    hi = x.astype(BF16)
    lo = (x - hi.astype(F32)).astype(BF16)
    return hi, lo


def _dot(a, b):
    return jnp.dot(a, b, preferred_element_type=F32)


def _dot3(a, b_hi, b_lo):
    a_hi, a_lo = _split_bf16(a)
    return _dot(a_hi, b_hi) + _dot(a_hi, b_lo) + _dot(a_lo, b_hi)


def _rms_scale(x):
    return lax.rsqrt(jnp.mean(x * x, axis=-1, keepdims=True) + EPS)


def _log_sigmoid(x):
    return jnp.minimum(x, 0.0) - jnp.log1p(jnp.exp(-jnp.abs(x)))


def _mod_kernel(c_ref, whi_ref, wlo_ref, b_ref, o_ref):
    c = c_ref[...]
    s = c * jax.nn.sigmoid(c)
    o_ref[...] = _dot3(s, whi_ref[...], wlo_ref[...]) + b_ref[...]


def _modulation(cvec, w_mod, b_mod):
    tn = 512
    whi, wlo = _split_bf16(w_mod)
    out = pl.pallas_call(
        _mod_kernel,
        out_shape=jax.ShapeDtypeStruct((8, 6 * D_MODEL), F32),
        grid=(6 * D_MODEL // tn,),
        in_specs=[
            pl.BlockSpec((8, D_MODEL), lambda j: (0, 0)),
            pl.BlockSpec((D_MODEL, tn), lambda j: (0, j)),
            pl.BlockSpec((D_MODEL, tn), lambda j: (0, j)),
            pl.BlockSpec((1, tn), lambda j: (0, j)),
        ],
        out_specs=pl.BlockSpec((8, tn), lambda j: (0, j)),
        compiler_params=_cparams(("parallel",)),
        name="modulation",
    )(cvec, whi, wlo, b_mod.reshape(1, -1))
    return out.reshape(8, 6, D_MODEL)


IN_TM = 512
IN_TN = 512


def _inproj_kernel(x_ref, mod_ref, g_ref, w_ref, b_ref, cs_ref, wg_hi_ref, wg_lo_ref, bg_ref,
                   p3_ref, cols_ref):
    x = x_ref[...]
    h = x * _rms_scale(x) * g_ref[...]
    h = h * (1.0 + mod_ref[0, 1:2, :]) + mod_ref[0, 0:1, :]
    hb = h.astype(BF16)
    for n in range(D_MAIN // IN_TN):
        sl = slice(n * IN_TN, (n + 1) * IN_TN)
        acc = (_dot(hb, w_ref[:, sl]) + b_ref[:, sl]) * cs_ref[:, sl]
        accb = acc.astype(BF16)
        for j in range(IN_TN // LANES):
            p3_ref[n * (IN_TN // LANES) + j] = accb[:, j * LANES:(j + 1) * LANES]

    g2 = _dot3(h, wg_hi_ref[...], wg_lo_ref[...]) + bg_ref[...]
    ls = _log_sigmoid(g2[:, :LANES])
    gi = g2[:, LANES:]
    row = lax.broadcasted_iota(jnp.int32, (CHUNK, LANES), 0)
    lane = lax.broadcasted_iota(jnp.int32, (CHUNK, LANES), 1)
    for c in range(IN_TM // CHUNK):
        x0 = ls[c * CHUNK:(c + 1) * CHUNK]
        cf = x0
        cb = x0
        s = 1
        while s < CHUNK:
            cf = cf + jnp.where(row >= s, pltpu.roll(cf, s, 0), 0.0)
            cb = cb + jnp.where(row < CHUNK - s, pltpu.roll(cb, CHUNK - s, 0), 0.0)
            s *= 2
        b = jnp.where(lane < 2 * N_HEADS, cf, cb)
        is_imb = (lane & N_HEADS) != 0
        out = jnp.where(is_imb, gi[c * CHUNK:(c + 1) * CHUNK] - b, b)
        cols_ref[c * CHUNK:(c + 1) * CHUNK, :] = out


def _in_projection(x2d, mod3, mod_row_fn, g_norm1, wmain, bmain, colscale, wg_hi, wg_lo, bg):
    n = x2d.shape[0]
    return pl.pallas_call(
        _inproj_kernel,
        out_shape=(jax.ShapeDtypeStruct((N_SLABS, n, LANES), BF16),
                   jax.ShapeDtypeStruct((n, LANES), F32)),
        grid=(n // IN_TM,),
        in_specs=[
            pl.BlockSpec((IN_TM, D_MODEL), lambda i: (i, 0)),
            pl.BlockSpec((1, 6, D_MODEL), lambda i: (mod_row_fn(i * IN_TM), 0, 0)),
            _const_spec((1, D_MODEL)),
            _const_spec((D_MODEL, D_MAIN)),
            _const_spec((1, D_MAIN)),
            _const_spec((1, D_MAIN)),
            _const_spec((D_MODEL, 2 * LANES)),
            _const_spec((D_MODEL, 2 * LANES)),
            _const_spec((1, 2 * LANES)),
        ],
        out_specs=(pl.BlockSpec((N_SLABS, IN_TM, LANES), lambda i: (0, i, 0)),
                   pl.BlockSpec((IN_TM, LANES), lambda i: (i, 0))),
        compiler_params=_cparams(("parallel",)),
        name="in_projection",
    )(x2d, mod3, g_norm1, wmain, bmain, colscale, wg_hi, wg_lo, bg)


ML_ROWS = 1024
ML_NCH = ML_ROWS // CHUNK


def _pick_col(blk, lane_iota, idx):
    return jnp.sum(jnp.where(lane_iota == idx, blk, 0.0), axis=1, keepdims=True)


def _mlstm_kernel(*refs, carry):
    if carry:
        (q_ref, k_ref, v_ref, o_ref, cols_ref, rf_ref, rb_ref, gn_ref, c0_ref, n0_ref, m0_ref,
         ya_ref, s_scr, h_scr) = refs
    else:
        (q_ref, k_ref, v_ref, o_ref, cols_ref, rf_ref, rb_ref, gn_ref,
         ya_ref, cst_ref, nst_ref, mst_ref, s_scr, h_scr) = refs
    head = pl.program_id(1)
    lane = lax.broadcasted_iota(jnp.int32, (CHUNK, LANES), 1)
    r_idx = lax.broadcasted_iota(jnp.int32, (CHUNK, CHUNK), 0)
    c_idx = lax.broadcasted_iota(jnp.int32, (CHUNK, CHUNK), 1)

    for d in range(2):
        mask = (c_idx <= r_idx) if d == 0 else (c_idx >= r_idx)
        row_ref = rf_ref if d == 0 else rb_ref
        last = CHUNK - 1 if d == 0 else 0
        if carry:
            C = c0_ref[0, 0, d, 0]
            nst = n0_ref[0, d, 0]
            m = m0_ref[0, d, 0]
        order = range(ML_NCH) if d == 0 else range(ML_NCH - 1, -1, -1)
        for step, c in enumerate(order):
            rows = slice(c * CHUNK, (c + 1) * CHUNK)
            q = q_ref[0, rows, :]
            k = k_ref[0, rows, :]
            v = v_ref[0, rows, :]
            if d == 0:
                s = lax.dot_general(q, k, (((1,), (1,)), ((), ())), preferred_element_type=F32)
                s_scr[rows, :] = s
            else:
                s = s_scr[rows, :]
            blk = cols_ref[rows, :]
            bcol = _pick_col(blk, lane, head + 2 * N_HEADS * d)
            imbcol = _pick_col(blk, lane, head + 2 * N_HEADS * d + N_HEADS)
            bmirow = row_ref[0, :, rows]
            dm = jnp.where(mask, bcol - bmirow, NEG)
            a = jnp.max(dm, axis=1, keepdims=True)
            if carry:
                inter = bcol + m
            else:
                inter = bcol + NEG
            m_t = jnp.maximum(inter, a)
            sw = s * jnp.exp(dm - m_t)
            den = jnp.sum(sw, axis=1, keepdims=True)
            num = _dot(sw.astype(BF16), v)
            if carry:
                w_inter = jnp.exp(inter - m_t)
                qn = jnp.sum(q.astype(F32) * nst, axis=1, keepdims=True)
                den = den + w_inter * qn
                num = num + w_inter * _dot(q, C.astype(BF16))
            hdir = num / jnp.maximum(jnp.abs(den), jnp.exp(-m_t))
            if d == 0:
                h_scr[rows, :] = hdir
            else:
                h_scr[rows, :] = h_scr[rows, :] + hdir

            if carry and step == ML_NCH - 1:
                continue
            m_new = m_t[last:last + 1, :]
            b_last = bcol[last:last + 1, :]
            w_col = jnp.exp(b_last - m_new + imbcol)
            w_row = jnp.exp(b_last - m_new - bmirow)
            u = lax.dot_general(k, (w_col * v.astype(F32)).astype(BF16), (((0,), (0,)), ((), ())),
                                preferred_element_type=F32)
            w8 = jnp.broadcast_to(w_row, (8, CHUNK)).astype(BF16)
            nu = _dot(w8, k)[0:1, :]
            if carry:
                decay = jnp.exp(b_last + m - m_new)
                C = decay * C + u
                nst = decay * nst + nu
                m = m_new
            else:
                cst_ref[c, 0, d, 0] = u
                nst_ref[c, d, 0] = nu
                mst_ref[c, d, 0] = m_new

    hh = h_scr[...]
    hn = hh * _rms_scale(hh) * gn_ref[0]
    ya_ref[0] = (hn * jax.nn.sigmoid(o_ref[0].astype(F32))).astype(BF16)


def _mlstm(p3, cols, rows_f, rows_b, gn3, state=None):
    n = p3.shape[1]
    nblk = n // ML_ROWS
    carry = state is not None

    def slab(base):
        return pl.BlockSpec((1, ML_ROWS, LANES), lambda g, h: (base + h, g, 0))

    in_specs = [
        slab(SLAB_Q), slab(SLAB_K), slab(SLAB_V), slab(SLAB_O),
        pl.BlockSpec((ML_ROWS, LANES), lambda g, h: (g, 0)),
        pl.BlockSpec((1, 1, ML_ROWS), lambda g, h: (h, 0, g)),
        pl.BlockSpec((1, 1, ML_ROWS), lambda g, h: (h, 0, g)),
        pl.BlockSpec((1, 1, LANES), lambda g, h: (h, 0, 0)),
    ]
    args = [p3, p3, p3, p3, cols, rows_f, rows_b, gn3]
    ya_shape = jax.ShapeDtypeStruct((N_HEADS, n, LANES), BF16)
    ya_spec = pl.BlockSpec((1, ML_ROWS, LANES), lambda g, h: (h, g, 0))
    if carry:
        c0, n0, m0 = state
        in_specs += [
            pl.BlockSpec((1, 1, 2, 1, HEAD_DIM, HEAD_DIM), lambda g, h: (g, 0, 0, h, 0, 0)),
            pl.BlockSpec((1, 2, 1, 1, HEAD_DIM), lambda g, h: (g, 0, h, 0, 0)),
            pl.BlockSpec((1, 2, 1, 1, 1), lambda g, h: (g, 0, h, 0, 0)),
        ]
        args += [c0, n0, m0]
        out_shape = ya_shape
        out_specs = ya_spec
    else:
        nseq = n // CHUNK
        out_shape = (ya_shape,
                     jax.ShapeDtypeStruct((nseq, 1, 2, N_HEADS, HEAD_DIM, HEAD_DIM), F32),
                     jax.ShapeDtypeStruct((nseq, 2, N_HEADS, 1, HEAD_DIM), F32),
                     jax.ShapeDtypeStruct((nseq, 2, N_HEADS, 1, 1), F32))
        out_specs = (ya_spec,
                     pl.BlockSpec((ML_NCH, 1, 2, 1, HEAD_DIM, HEAD_DIM), lambda g, h: (g, 0, 0, h, 0, 0)),
                     pl.BlockSpec((ML_NCH, 2, 1, 1, HEAD_DIM), lambda g, h: (g, 0, h, 0, 0)),
                     pl.BlockSpec((ML_NCH, 2, 1, 1, 1), lambda g, h: (g, 0, h, 0, 0)))
    return pl.pallas_call(
        functools.partial(_mlstm_kernel, carry=carry),
        out_shape=out_shape,
        grid=(nblk, N_HEADS),
        in_specs=in_specs,
        out_specs=out_specs,
        scratch_shapes=[pltpu.VMEM((ML_ROWS, CHUNK), F32), pltpu.VMEM((ML_ROWS, HEAD_DIM), F32)],
        compiler_params=_cparams(("parallel", "parallel")),
        name="mlstm_latent" if carry else "mlstm_prompt",
    )(*args)


def _pool_kernel(u_ref, a_ref, ic_ref, wpg_ref, ps_ref, yb_ref, *, seq_len, n_seq):
    for sq in range(n_seq):
        rows = slice(sq * seq_len, (sq + 1) * seq_len)
        for g in range(N_GROUPS):
            u = u_ref[g, rows, :]
            pooled = _dot(a_ref[g], u) * ic_ref[g]
            p = (pooled - u.astype(F32)).astype(BF16)
            yb_ref[g, rows, :] = (_dot(p, wpg_ref[g]) * ps_ref[g]).astype(BF16)


def _pool(p3, amat, invc, wpg, pscale, seq_len, n_seq):
    n = p3.shape[1]
    rows = seq_len * n_seq
    return pl.pallas_call(
        functools.partial(_pool_kernel, seq_len=seq_len, n_seq=n_seq),
        out_shape=jax.ShapeDtypeStruct((N_GROUPS, n, LANES), BF16),
        grid=(n // rows,),
        in_specs=[
            pl.BlockSpec((N_GROUPS, rows, LANES), lambda i: (SLAB_POOL // N_GROUPS, i, 0)),
            _const_spec((N_GROUPS, seq_len, seq_len)),
            _const_spec((N_GROUPS, seq_len, LANES)),
            _const_spec((N_GROUPS, LANES, LANES)),
            _const_spec((N_GROUPS, 1, LANES)),
        ],
        out_specs=pl.BlockSpec((N_GROUPS, rows, LANES), lambda i: (0, i, 0)),
        compiler_params=_cparams(("parallel",)),
        name="pool",
    )(p3, amat, invc, wpg, pscale)


def _window_matrix(size, w):
    idx = np.arange(size)
    lo = np.clip(idx - w // 2, 0, size)
    hi = np.clip(idx - w // 2 + w, 0, size)
    s = np.arange(size)[None, :]
    a = ((s >= lo[:, None]) & (s < hi[:, None])).astype(np.float32)
    return a, (hi - lo).astype(np.float32)


def _pool_constants(seq_len, rows_2d):
    mats, invs = [], []
    for w in POOL_WINDOWS:
        if rows_2d is None:
            a, cnt = _window_matrix(seq_len, w)
        else:
            ar, cr = _window_matrix(rows_2d, w)
            ac, cc = _window_matrix(GRID_W, w)
            a = np.kron(ar, ac)
            cnt = np.kron(cr, cc)
        mats.append(a)
        invs.append(np.repeat((1.0 / cnt)[:, None], LANES, axis=1))
    return (jnp.asarray(np.stack(mats), dtype=BF16), jnp.asarray(np.stack(invs), dtype=F32))


MIX_TM = 512


def _mix_kernel(x_ref, mod_ref, ya_ref, yb_ref, ga_ref, gb_ref, wa_ref, wb_ref, wo_ref, x1_ref):
    ya = jnp.concatenate([ya_ref[i] for i in range(N_HEADS)], axis=1)
    yb = jnp.concatenate([yb_ref[i] for i in range(N_GROUPS)], axis=1)
    ga = jnp.concatenate([ga_ref[i] for i in range(8)], axis=1).astype(F32)
    gb = jnp.concatenate([gb_ref[i] for i in range(8)], axis=1).astype(F32)
    merged = jax.nn.sigmoid(ga) * _dot(ya, wa_ref[...]) + jax.nn.sigmoid(gb) * _dot(yb, wb_ref[...])
    out = _dot(merged.astype(BF16), wo_ref[...])
    x1_ref[...] = x_ref[...] + mod_ref[0, 2:3, :] * out


def _mix(x2d, mod3, mod_row_fn, ya, yb, p3, wa, wb, wo):
    n = x2d.shape[0]
    return pl.pallas_call(
        _mix_kernel,
        out_shape=jax.ShapeDtypeStruct((n, D_MODEL), F32),
        grid=(n // MIX_TM,),
        in_specs=[
            pl.BlockSpec((MIX_TM, D_MODEL), lambda i: (i, 0)),
            pl.BlockSpec((1, 6, D_MODEL), lambda i: (mod_row_fn(i * MIX_TM), 0, 0)),
            pl.BlockSpec((N_HEADS, MIX_TM, LANES), lambda i: (0, i, 0)),
            pl.BlockSpec((N_GROUPS, MIX_TM, LANES), lambda i: (0, i, 0)),
            pl.BlockSpec((8, MIX_TM, LANES), lambda i: (SLAB_GA // 8, i, 0)),
            pl.BlockSpec((8, MIX_TM, LANES), lambda i: (SLAB_GB // 8, i, 0)),
            _const_spec((D_MODEL, D_MODEL)),
            _const_spec((N_GROUPS * LANES, D_MODEL)),
            _const_spec((D_MODEL, D_MODEL)),
        ],
        out_specs=pl.BlockSpec((MIX_TM, D_MODEL), lambda i: (i, 0)),
        compiler_params=_cparams(("parallel",)),
        name="mix",
    )(x2d, mod3, ya, yb, p3, p3, wa, wb, wo)


MOE_TB = 1024
MOE_TILE = 160
ROUTER_ROWS = 4 + N_EXPERTS


def _nt_dot(a, b):
    return lax.dot_general(a, b, (((1,), (1,)), ((), ())), preferred_element_type=F32)


def _route_rows(lt):
    lg = [lt[i:i + 1] for i in range(N_GROUPS)]
    mx = jnp.maximum(jnp.maximum(lg[0], lg[1]), jnp.maximum(lg[2], lg[3]))
    p_top = 1.0 / (jnp.exp(lg[0] - mx) + jnp.exp(lg[1] - mx) + jnp.exp(lg[2] - mx) + jnp.exp(lg[3] - mx))
    gidx = jnp.zeros(lg[0].shape, jnp.int32)
    best = lg[0]
    for i in range(1, N_GROUPS):
        upd = lg[i] > best
        gidx = jnp.where(upd, i, gidx)
        best = jnp.where(upd, lg[i], best)

    def expert_row(g, e):
        r = N_GROUPS + g * EXPERTS_PER_GROUP + e
        return lt[r:r + 1]

    le = [jnp.where(gidx == 0, expert_row(0, e),
                    jnp.where(gidx == 1, expert_row(1, e),
                              jnp.where(gidx == 2, expert_row(2, e), expert_row(3, e))))
          for e in range(EXPERTS_PER_GROUP)]
    v1 = jnp.maximum(jnp.maximum(le[0], le[1]), jnp.maximum(le[2], le[3]))
    i1 = jnp.full(v1.shape, EXPERTS_PER_GROUP - 1, jnp.int32)
    for e in range(EXPERTS_PER_GROUP - 2, -1, -1):
        i1 = jnp.where(le[e] == v1, e, i1)
    le2 = [jnp.where(i1 == e, -jnp.inf, le[e]) for e in range(EXPERTS_PER_GROUP)]
    v2 = jnp.maximum(jnp.maximum(le2[0], le2[1]), jnp.maximum(le2[2], le2[3]))
    i2 = jnp.full(v2.shape, EXPERTS_PER_GROUP - 1, jnp.int32)
    for e in range(EXPERTS_PER_GROUP - 2, -1, -1):
        i2 = jnp.where(le2[e] == v2, e, i2)
    e2 = jnp.exp(v2 - v1)
    w1 = p_top / (1.0 + e2)
    w2 = p_top * e2 / (1.0 + e2)
    w_rows, sel_rows = [], []
    for g in range(N_GROUPS):
        for e in range(EXPERTS_PER_GROUP):
            first = jnp.where(gidx == g, jnp.where(i1 == e, 1.0, 0.0), 0.0)
            second = jnp.where(gidx == g, jnp.where(i2 == e, 1.0, 0.0), 0.0)
            w_rows.append(first * w1 + second * w2)
            sel_rows.append(first + second)
    return w_rows, sel_rows


def _moe_kernel(x1_ref, mod_ref, g2_ref, wrt_hi_ref, wrt_lo_ref, brt_ref, wg_ref, wu_ref, wd_ref, gf_ref,
                y_ref, h2_scr, rank_scr, w_scr, acc_scr):
    e = pl.program_id(1)

    @pl.when(e == 0)
    def _():
        x1 = x1_ref[...]
        h2 = x1 * _rms_scale(x1) * g2_ref[...]
        h2 = h2 * (1.0 + mod_ref[0, 4:5, :]) + mod_ref[0, 3:4, :]
        h_hi, h_lo = _split_bf16(h2)
        h2_scr[...] = h_hi
        wr_hi = wrt_hi_ref[...]
        lt = _nt_dot(wr_hi, h_hi) + _nt_dot(wr_hi, h_lo) + _nt_dot(wrt_lo_ref[...], h_hi) + brt_ref[...]
        w_rows, sel_rows = _route_rows(lt)
        sel = jnp.concatenate(sel_rows, axis=0)
        r_i = lax.broadcasted_iota(jnp.int32, (MOE_TB, MOE_TB), 0)
        c_i = lax.broadcasted_iota(jnp.int32, (MOE_TB, MOE_TB), 1)
        before = jnp.where(r_i < c_i, 1.0, 0.0).astype(BF16)
        rank = _dot(sel.astype(BF16), before)
        rank = jnp.where(sel > 0.0, rank, -1.0)
        for i in range(N_EXPERTS):
            rank_scr[i] = rank[i:i + 1]
            w_scr[i] = w_rows[i]
        acc_scr[...] = jnp.zeros_like(acc_scr)

    rank_row = rank_scr[e]
    w_row = w_scr[e]
    n_routed = jnp.max(rank_row).astype(jnp.int32) + 1
    n_pass = jnp.int32(0)
    for kk in range(-(-MOE_TB // MOE_TILE)):
        n_pass = n_pass + (n_routed > kk * MOE_TILE).astype(jnp.int32)
    slot = lax.broadcasted_iota(jnp.int32, (MOE_TILE, MOE_TB), 0).astype(F32)

    def one_pass(j, carry):
        hit = (slot + (j * MOE_TILE).astype(F32)) == rank_row
        pb = jnp.where(hit, 1.0, 0.0).astype(BF16)
        w_slot = jnp.sum(jnp.where(hit, w_row, 0.0), axis=1, keepdims=True)
        xg = _dot(pb, h2_scr[...]).astype(BF16)
        a = _dot(xg, wg_ref[0])
        u = _dot(xg, wu_ref[0])
        act = (a * jax.nn.sigmoid(a) * u * w_slot).astype(BF16)
        yv = _dot(act, wd_ref[0]).astype(BF16)
        acc_scr[...] += lax.dot_general(pb, yv, (((0,), (0,)), ((), ())), preferred_element_type=F32)
        return carry

    lax.fori_loop(0, n_pass, one_pass, 0)

    @pl.when(e == N_EXPERTS - 1)
    def _():
        x2 = x1_ref[...] + mod_ref[0, 5:6, :] * acc_scr[...]
        y_ref[...] = x2 * _rms_scale(x2) * gf_ref[...]


def _moe(x1, mod3, mod_row_fn, g_norm2, wrt_hi, wrt_lo, brt, wg, wu, wd, g_final):
    n = x1.shape[0]
    return pl.pallas_call(
        _moe_kernel,
        out_shape=jax.ShapeDtypeStruct((n, D_MODEL), F32),
        grid=(n // MOE_TB, N_EXPERTS),
        in_specs=[
            pl.BlockSpec((MOE_TB, D_MODEL), lambda i, e: (i, 0)),
            pl.BlockSpec((1, 6, D_MODEL), lambda i, e: (mod_row_fn(i * MOE_TB), 0, 0)),
            _const_spec((1, D_MODEL)),
            _const_spec((LANES, D_MODEL)),
            _const_spec((LANES, D_MODEL)),
            _const_spec((LANES, 1)),
            pl.BlockSpec((1, D_MODEL, D_EXPERT), lambda i, e: (e, 0, 0)),
            pl.BlockSpec((1, D_MODEL, D_EXPERT), lambda i, e: (e, 0, 0)),
            pl.BlockSpec((1, D_EXPERT, D_MODEL), lambda i, e: (e, 0, 0)),
            _const_spec((1, D_MODEL)),
        ],
        out_specs=pl.BlockSpec((MOE_TB, D_MODEL), lambda i, e: (i, 0)),
        scratch_shapes=[pltpu.VMEM((MOE_TB, D_MODEL), BF16),
                        pltpu.VMEM((N_EXPERTS, 1, MOE_TB), F32),
                        pltpu.VMEM((N_EXPERTS, 1, MOE_TB), F32),
                        pltpu.VMEM((MOE_TB, D_MODEL), F32)],
        compiler_params=_cparams(("parallel", "arbitrary")),
        name="moe",
    )(x1, mod3, g_norm2, wrt_hi, wrt_lo, brt, wg, wu, wd, g_final)


def _stream(x, mod3, mod_row_fn, seq_len, rows_2d, state, w):
    bsz, t, _ = x.shape
    n = bsz * t
    x2d = x.reshape(n, D_MODEL)
    p3, cols = _in_projection(x2d, mod3, mod_row_fn, w["g_norm1"], w["wmain"], w["bmain"], w["colscale"],
                              w["wg_hi"], w["wg_lo"], w["bg"])
    rows_f = (-cols[:, N_HEADS:2 * N_HEADS]).T.reshape(N_HEADS, 1, n)
    rows_b = (-cols[:, 3 * N_HEADS:4 * N_HEADS]).T.reshape(N_HEADS, 1, n)
    ml = _mlstm(p3, cols, rows_f, rows_b, w["gn3"], state)
    amat, invc = _pool_constants(seq_len, rows_2d)
    n_seq = max(1, 1024 // seq_len)
    yb = _pool(p3, amat, invc, w["wpg"], w["pscale"], seq_len, n_seq)
    ya = ml if state is not None else ml[0]
    x1 = _mix(x2d, mod3, mod_row_fn, ya, yb, p3, w["wa"], w["wb"], w["wo"])
    y = _moe(x1, mod3, mod_row_fn, w["g_norm2"], w["wrt_hi"], w["wrt_lo"], w["brt"],
             w["wg"], w["wu"], w["wd"], w["g_final"])
    return y.reshape(bsz, t, D_MODEL), ml


def kernel(x_prompt, x_sample, state_C, state_n, state_m, c, c_ctx, w_mod, b_mod, g_norm1, w_in, b_in, gn_gain, w_pool_grp, pool_scale, w_proj_a, w_proj_b, w_out, g_norm2, w_router_grp, b_router_grp, w_router_exp, b_router_exp, w_exp_gate, w_exp_up, w_exp_down, g_final):
    bp, tp, _ = x_prompt.shape
    bs, ts, _ = x_sample.shape
    assert w_mod.shape[0] == 1, "single trunk layer"
    assert tp == CHUNK and ts % CHUNK == 0 and ts == ML_ROWS

    cvec = jnp.zeros((8, D_MODEL), F32).at[0].set(c_ctx).at[1:1 + bs].set(c)
    mod3 = _modulation(cvec, w_mod[0], b_mod[0])

    wi, bi = w_in[0], b_in[0]
    d = D_MODEL
    o_q, o_k, o_v, o_o, o_g, o_p, o_ga, o_gb = 0, d, 2 * d, 3 * d, 4 * d, 4 * d + 32, 4 * d + 32 + 512, 4 * d + 32 + 512 + d
    order = [(o_q, d), (o_k, d), (o_v, d), (o_o, d), (o_ga, d), (o_gb, d), (o_p, 512)]
    wmain = jnp.concatenate([wi[:, s:s + l] for s, l in order], axis=1).astype(BF16)
    bmain = jnp.concatenate([bi[s:s + l] for s, l in order]).reshape(1, D_MAIN)
    colscale = jnp.ones((D_MAIN,), F32).at[SLAB_K * LANES:(SLAB_K + N_HEADS) * LANES].set(HEAD_DIM ** -0.5)
    colscale = colscale.reshape(1, D_MAIN)
    gw = wi[:, o_g:o_g + 32]
    gb = bi[o_g:o_g + 32]
    i_f, f_f, i_b, f_b = (slice(0, 8), slice(8, 16), slice(16, 24), slice(24, 32))
    zw = jnp.zeros((d, 8), F32)
    zb = jnp.zeros((8,), F32)
    wg2 = jnp.concatenate([gw[:, f_f], gw[:, f_f], gw[:, f_b], gw[:, f_b], jnp.zeros((d, LANES - 32), F32),
                           zw, gw[:, i_f], zw, gw[:, i_b], jnp.zeros((d, LANES - 32), F32)], axis=1)
    bg2 = jnp.concatenate([gb[f_f], gb[f_f], gb[f_b], gb[f_b], jnp.zeros((LANES - 32,), F32),
                           zb, gb[i_f], zb, gb[i_b], jnp.zeros((LANES - 32,), F32)]).reshape(1, 2 * LANES)
    wg_hi, wg_lo = _split_bf16(wg2)

    wrt = jnp.concatenate([w_router_grp[0], w_router_exp[0].reshape(d, N_EXPERTS),
                           jnp.zeros((d, LANES - ROUTER_ROWS), F32)], axis=1).T
    brt = jnp.concatenate([b_router_grp[0], b_router_exp[0].reshape(N_EXPERTS),
                           jnp.zeros((LANES - ROUTER_ROWS,), F32)]).reshape(LANES, 1)
    wrt_hi, wrt_lo = _split_bf16(wrt)

    w = dict(
        g_norm1=g_norm1[0].reshape(1, d), wmain=wmain, bmain=bmain, colscale=colscale,
        wg_hi=wg_hi, wg_lo=wg_lo, bg=bg2,
        gn3=gn_gain[0].reshape(N_HEADS, 1, HEAD_DIM),
        wpg=w_pool_grp[0].astype(BF16), pscale=pool_scale[0].reshape(N_GROUPS, 1, LANES),
        wa=w_proj_a[0].astype(BF16), wb=w_proj_b[0].astype(BF16), wo=w_out[0].astype(BF16),
        g_norm2=g_norm2[0].reshape(1, d), wrt_hi=wrt_hi, wrt_lo=wrt_lo, brt=brt,
        wg=w_exp_gate[0].astype(BF16), wu=w_exp_up[0].astype(BF16), wd=w_exp_down[0].astype(BF16),
        g_final=g_final.reshape(1, d),
    )

    y_prompt, (_, cst, nst, mst) = _stream(x_prompt, mod3, lambda r: 0, tp, None, None, w)
    state = (state_C, state_n[:, 0].reshape(bs, 2, N_HEADS, 1, HEAD_DIM),
             state_m[:, 0].reshape(bs, 2, N_HEADS, 1, 1))
    y_sample, _ = _stream(x_sample, mod3, lambda r: 1 + r // ts, ts, ts // GRID_W, state, w)

    new_c = cst
    new_n = nst.reshape(bp, 1, 2, N_HEADS, HEAD_DIM)
    new_m = mst.reshape(bp, 1, 2, N_HEADS)
    return (y_prompt, y_sample, new_c, new_n, new_m)
```

```python
import functools

import numpy as np
import jax
import jax.numpy as jnp
from jax import lax
from jax.experimental import pallas as pl
from jax.experimental.pallas import tpu as pltpu

F32 = jnp.float32
BF16 = jnp.bfloat16

D_MODEL = 1024
N_HEADS = 8
HEAD_DIM = 128
GRID_W = 64
POOL_WINDOWS = (2, 4, 8, 16)
N_GROUPS = 4
EXPERTS_PER_GROUP = 4
N_EXPERTS = 16
D_EXPERT = 512
EPS = 1e-6
NEG = -1e30

LANES = 128
CHUNK = 256
N_SLABS = 52
SLAB_Q, SLAB_K, SLAB_V, SLAB_O, SLAB_GA, SLAB_GB, SLAB_POOL = 0, 8, 16, 24, 32, 40, 48
D_MAIN = N_SLABS * LANES
VMEM_LIMIT = 56 * 1024 * 1024


def _cparams(sem):
    return pltpu.CompilerParams(dimension_semantics=sem, vmem_limit_bytes=VMEM_LIMIT)


def _const_spec(shape):
    nd = len(shape)
    return pl.BlockSpec(shape, lambda *_: (0,) * nd, pipeline_mode=pl.Buffered(1))


def _split_bf16(x):
    hi = x.astype(BF16)
    lo = (x - hi.astype(F32)).astype(BF16)
    return hi, lo


def _dot(a, b):
    return jnp.dot(a, b, preferred_element_type=F32)


def _dot3(a, b_hi, b_lo):
    a_hi, a_lo = _split_bf16(a)
    return _dot(a_hi, b_hi) + _dot(a_hi, b_lo) + _dot(a_lo, b_hi)


def _rms_scale(x):
    return lax.rsqrt(jnp.mean(x * x, axis=-1, keepdims=True) + EPS)


def _log_sigmoid(x):
    return jnp.minimum(x, 0.0) - jnp.log1p(jnp.exp(-jnp.abs(x)))


def _mod_kernel(c_ref, w_ref, b_ref, o_ref):
    c = c_ref[...]
    s = c * jax.nn.sigmoid(c)
    w_hi, w_lo = _split_bf16(w_ref[...])
    o_ref[...] = _dot3(s, w_hi, w_lo) + b_ref[...]


def _modulation(cvec, w_mod, b_mod):
    tn = 512
    out = pl.pallas_call(
        _mod_kernel,
        out_shape=jax.ShapeDtypeStruct((8, 6 * D_MODEL), F32),
        grid=(6 * D_MODEL // tn,),
        in_specs=[
            pl.BlockSpec((8, D_MODEL), lambda j: (0, 0)),
            pl.BlockSpec((D_MODEL, tn), lambda j: (0, j)),
            pl.BlockSpec((1, tn), lambda j: (0, j)),
        ],
        out_specs=pl.BlockSpec((8, tn), lambda j: (0, j)),
        compiler_params=_cparams(("parallel",)),
        name="modulation",
    )(cvec, w_mod, b_mod.reshape(1, -1))
    return out.reshape(8, 6, D_MODEL)


IN_TM = 512
IN_TN = 512
CMAX_LANE_SHIFT = 32
LOG2E = 1.4426950408889634


def _inproj_kernel(x_ref, mod_ref, g_ref, w_ref, b_ref, cs_ref, wg_hi_ref, wg_lo_ref, bg_ref,
                   p3_ref, cols_ref):
    x = x_ref[...]
    h = x * _rms_scale(x) * g_ref[...]
    h = h * (1.0 + mod_ref[0, 1:2, :]) + mod_ref[0, 0:1, :]
    hb = h.astype(BF16)
    for n in range(D_MAIN // IN_TN):
        sl = slice(n * IN_TN, (n + 1) * IN_TN)
        acc = (_dot(hb, w_ref[:, sl]) + b_ref[:, sl]) * cs_ref[:, sl]
        accb = acc.astype(BF16)
        for j in range(IN_TN // LANES):
            p3_ref[n * (IN_TN // LANES) + j] = accb[:, j * LANES:(j + 1) * LANES]

    g2 = _dot3(h, wg_hi_ref[...], wg_lo_ref[...]) + bg_ref[...]
    ls = _log_sigmoid(g2[:, :LANES])
    gi = g2[:, LANES:]
    row = lax.broadcasted_iota(jnp.int32, (CHUNK, LANES), 0)
    lane = lax.broadcasted_iota(jnp.int32, (CHUNK, LANES), 1)
    for c in range(IN_TM // CHUNK):
        x0 = ls[c * CHUNK:(c + 1) * CHUNK]
        cf = x0
        cb = x0
        s = 1
        while s < CHUNK:
            cf = cf + jnp.where(row >= s, pltpu.roll(cf, s, 0), 0.0)
            cb = cb + jnp.where(row < CHUNK - s, pltpu.roll(cb, CHUNK - s, 0), 0.0)
            s *= 2
        b = jnp.where(lane < 2 * N_HEADS, cf, cb)
        is_imb = (lane & N_HEADS) != 0
        out = jnp.where(is_imb, gi[c * CHUNK:(c + 1) * CHUNK] - b, b)
        mf = out
        mb = out
        s = 1
        while s < CHUNK:
            mf = jnp.maximum(mf, jnp.where(row >= s, pltpu.roll(mf, s, 0), -jnp.inf))
            mb = jnp.maximum(mb, jnp.where(row < CHUNK - s, pltpu.roll(mb, CHUNK - s, 0), -jnp.inf))
            s *= 2
        cmax = pltpu.roll(jnp.where(lane < 2 * N_HEADS, mf, mb), CMAX_LANE_SHIFT, 1)
        cols_ref[c * CHUNK:(c + 1) * CHUNK, :] = jnp.where(lane < CMAX_LANE_SHIFT, out, cmax)


def _in_projection(x2d, mod3, mod_row_fn, g_norm1, wmain, bmain, colscale, wg_hi, wg_lo, bg):
    n = x2d.shape[0]
    return pl.pallas_call(
        _inproj_kernel,
        out_shape=(jax.ShapeDtypeStruct((N_SLABS, n, LANES), BF16),
                   jax.ShapeDtypeStruct((n, LANES), F32)),
        grid=(n // IN_TM,),
        in_specs=[
            pl.BlockSpec((IN_TM, D_MODEL), lambda i: (i, 0)),
            pl.BlockSpec((1, 6, D_MODEL), lambda i: (mod_row_fn(i * IN_TM), 0, 0)),
            _const_spec((1, D_MODEL)),
            _const_spec((D_MODEL, D_MAIN)),
            _const_spec((1, D_MAIN)),
            _const_spec((1, D_MAIN)),
            _const_spec((D_MODEL, 2 * LANES)),
            _const_spec((D_MODEL, 2 * LANES)),
            _const_spec((1, 2 * LANES)),
        ],
        out_specs=(pl.BlockSpec((N_SLABS, IN_TM, LANES), lambda i: (0, i, 0)),
                   pl.BlockSpec((IN_TM, LANES), lambda i: (i, 0))),
        compiler_params=_cparams(("parallel",)),
        name="in_projection",
    )(x2d, mod3, g_norm1, wmain, bmain, colscale, wg_hi, wg_lo, bg)


ML_ROWS = 1024
ML_NCH = ML_ROWS // CHUNK


def _pick_col(blk, lane_iota, idx):
    return jnp.sum(jnp.where(lane_iota == idx, blk, 0.0), axis=1, keepdims=True)


def _mlstm_kernel(*refs, carry):
    if carry:
        (q_ref, k_ref, v_ref, o_ref, cols_ref, rf_ref, rb_ref, gn_ref, c0_ref, n0_ref, m0_ref,
         ya_ref, s_scr, h_scr) = refs
    else:
        (q_ref, k_ref, v_ref, o_ref, cols_ref, rf_ref, rb_ref, gn_ref,
         ya_ref, cst_ref, nst_ref, mst_ref, s_scr, h_scr) = refs
    head = pl.program_id(1)
    lane = lax.broadcasted_iota(jnp.int32, (CHUNK, LANES), 1)
    r_idx = lax.broadcasted_iota(jnp.int32, (CHUNK, CHUNK), 0)
    c_idx = lax.broadcasted_iota(jnp.int32, (CHUNK, CHUNK), 1)
    ones = jnp.ones((CHUNK, HEAD_DIM), BF16)
    tn_dims = (((0,), (0,)), ((), ()))

    for d in range(2):
        mask = (c_idx <= r_idx) if d == 0 else (c_idx >= r_idx)
        row_ref = rf_ref if d == 0 else rb_ref
        last = CHUNK - 1 if d == 0 else 0
        base = head + 2 * N_HEADS * d
        if carry:
            cn = jnp.concatenate([c0_ref[0, 0, d, 0], n0_ref[0, 0, d, 0]], axis=1)
            m = m0_ref[0, d, 0]
        order = range(ML_NCH) if d == 0 else range(ML_NCH - 1, -1, -1)
        for step, c in enumerate(order):
            rows = slice(c * CHUNK, (c + 1) * CHUNK)
            q = q_ref[0, rows, :]
            k = k_ref[0, rows, :]
            v = v_ref[0, rows, :]
            v1 = jnp.concatenate([v, ones], axis=1)
            if d == 0:
                s = lax.dot_general(q, k, (((1,), (1,)), ((), ())), preferred_element_type=F32)
                s_scr[rows, :] = s
            else:
                s = s_scr[rows, :]
            blk = cols_ref[rows, :]
            bcol = _pick_col(blk, lane, base)
            imbcol = _pick_col(blk, lane, base + N_HEADS)
            a = bcol + _pick_col(blk, lane, base + N_HEADS + CMAX_LANE_SHIFT)
            bmirow = row_ref[0, :, rows]
            inter = bcol + (m if carry else NEG)
            m_t = jnp.maximum(inter, a)
            arg = (bcol - m_t) * LOG2E - bmirow * LOG2E
            sw = s * jnp.exp2(jnp.where(mask, arg, NEG))
            nd = _dot(sw.astype(BF16), v1)
            if carry:
                nd = nd + jnp.exp(inter - m_t) * _dot(q, cn.astype(BF16))
            hdir = nd[:, :HEAD_DIM] / jnp.maximum(jnp.abs(nd[:, HEAD_DIM:]), jnp.exp(-m_t))
            if d == 0:
                h_scr[rows, :] = hdir
            else:
                h_scr[rows, :] = h_scr[rows, :] + hdir

            if carry and step == ML_NCH - 1:
                continue
            m_new = m_t[last:last + 1, :]
            b_last = bcol[last:last + 1, :]
            w_col = jnp.exp(b_last - m_new + imbcol)
            if carry:
                u = lax.dot_general(k, (w_col * v1.astype(F32)).astype(BF16), tn_dims,
                                    preferred_element_type=F32)
                cn = jnp.exp(b_last + m - m_new) * cn + u
                m = m_new
            else:
                u = lax.dot_general(k, (w_col * v.astype(F32)).astype(BF16), tn_dims,
                                    preferred_element_type=F32)
                w_row = jnp.exp(b_last - m_new - bmirow)
                w8 = jnp.broadcast_to(w_row, (8, CHUNK)).astype(BF16)
                cst_ref[c, 0, d, 0] = u
                nst_ref[c, d, 0] = _dot(w8, k)[0:1, :]
                mst_ref[c, d, 0] = m_new

    hh = h_scr[...]
    hn = hh * _rms_scale(hh) * gn_ref[0]
    ya_ref[0] = (hn * jax.nn.sigmoid(o_ref[0].astype(F32))).astype(BF16)


def _mlstm(p3, cols, rows_f, rows_b, gn3, state=None):
    n = p3.shape[1]
    nblk = n // ML_ROWS
    carry = state is not None

    def slab(base):
        return pl.BlockSpec((1, ML_ROWS, LANES), lambda g, h: (base + h, g, 0))

    in_specs = [
        slab(SLAB_Q), slab(SLAB_K), slab(SLAB_V), slab(SLAB_O),
        pl.BlockSpec((ML_ROWS, LANES), lambda g, h: (g, 0)),
        pl.BlockSpec((1, 1, ML_ROWS), lambda g, h: (h, 0, g)),
        pl.BlockSpec((1, 1, ML_ROWS), lambda g, h: (h, 0, g)),
        pl.BlockSpec((1, 1, LANES), lambda g, h: (h, 0, 0)),
    ]
    args = [p3, p3, p3, p3, cols, rows_f, rows_b, gn3]
    ya_shape = jax.ShapeDtypeStruct((N_HEADS, n, LANES), BF16)
    ya_spec = pl.BlockSpec((1, ML_ROWS, LANES), lambda g, h: (h, g, 0))
    if carry:
        c0, n0, m0 = state
        in_specs += [
            pl.BlockSpec((1, 1, 2, 1, HEAD_DIM, HEAD_DIM), lambda g, h: (g, 0, 0, h, 0, 0)),
            pl.BlockSpec((1, 1, 2, 1, HEAD_DIM, HEAD_DIM), lambda g, h: (g, 0, 0, h, 0, 0)),
            pl.BlockSpec((1, 2, 1, 1, 1), lambda g, h: (g, 0, h, 0, 0)),
        ]
        args += [c0, n0, m0]
        out_shape = ya_shape
        out_specs = ya_spec
    else:
        nseq = n // CHUNK
        out_shape = (ya_shape,
                     jax.ShapeDtypeStruct((nseq, 1, 2, N_HEADS, HEAD_DIM, HEAD_DIM), F32),
                     jax.ShapeDtypeStruct((nseq, 2, N_HEADS, 1, HEAD_DIM), F32),
                     jax.ShapeDtypeStruct((nseq, 2, N_HEADS, 1, 1), F32))
        out_specs = (ya_spec,
                     pl.BlockSpec((ML_NCH, 1, 2, 1, HEAD_DIM, HEAD_DIM), lambda g, h: (g, 0, 0, h, 0, 0)),
                     pl.BlockSpec((ML_NCH, 2, 1, 1, HEAD_DIM), lambda g, h: (g, 0, h, 0, 0)),
                     pl.BlockSpec((ML_NCH, 2, 1, 1, 1), lambda g, h: (g, 0, h, 0, 0)))
    return pl.pallas_call(
        functools.partial(_mlstm_kernel, carry=carry),
        out_shape=out_shape,
        grid=(nblk, N_HEADS),
        in_specs=in_specs,
        out_specs=out_specs,
        scratch_shapes=[pltpu.VMEM((ML_ROWS, CHUNK), F32), pltpu.VMEM((ML_ROWS, HEAD_DIM), F32)],
        compiler_params=_cparams(("parallel", "parallel")),
        name="mlstm_latent" if carry else "mlstm_prompt",
    )(*args)


def _pool_kernel(u_ref, a_ref, ic_ref, wpg_ref, ps_ref, yb_ref, *, seq_len, n_seq):
    for sq in range(n_seq):
        rows = slice(sq * seq_len, (sq + 1) * seq_len)
        for g in range(N_GROUPS):
            u = u_ref[g, rows, :]
            pooled = _dot(a_ref[g], u) * ic_ref[g]
            p = (pooled - u.astype(F32)).astype(BF16)
            yb_ref[g, rows, :] = (_dot(p, wpg_ref[g]) * ps_ref[g]).astype(BF16)


def _pool(p3, amat, invc, wpg, pscale, seq_len, n_seq):
    n = p3.shape[1]
    rows = seq_len * n_seq
    return pl.pallas_call(
        functools.partial(_pool_kernel, seq_len=seq_len, n_seq=n_seq),
        out_shape=jax.ShapeDtypeStruct((N_GROUPS, n, LANES), BF16),
        grid=(n // rows,),
        in_specs=[
            pl.BlockSpec((N_GROUPS, rows, LANES), lambda i: (SLAB_POOL // N_GROUPS, i, 0)),
            _const_spec((N_GROUPS, seq_len, seq_len)),
            _const_spec((N_GROUPS, seq_len, LANES)),
            _const_spec((N_GROUPS, LANES, LANES)),
            _const_spec((N_GROUPS, 1, LANES)),
        ],
        out_specs=pl.BlockSpec((N_GROUPS, rows, LANES), lambda i: (0, i, 0)),
        compiler_params=_cparams(("parallel",)),
        name="pool",
    )(p3, amat, invc, wpg, pscale)


def _window_matrix(size, w):
    idx = np.arange(size)
    lo = np.clip(idx - w // 2, 0, size)
    hi = np.clip(idx - w // 2 + w, 0, size)
    s = np.arange(size)[None, :]
    a = ((s >= lo[:, None]) & (s < hi[:, None])).astype(np.float32)
    return a, (hi - lo).astype(np.float32)


def _pool_constants(seq_len, rows_2d):
    mats, invs = [], []
    for w in POOL_WINDOWS:
        if rows_2d is None:
            a, cnt = _window_matrix(seq_len, w)
        else:
            ar, cr = _window_matrix(rows_2d, w)
            ac, cc = _window_matrix(GRID_W, w)
            a = np.kron(ar, ac)
            cnt = np.kron(cr, cc)
        mats.append(a)
        invs.append(np.repeat((1.0 / cnt)[:, None], LANES, axis=1))
    return (jnp.asarray(np.stack(mats), dtype=BF16), jnp.asarray(np.stack(invs), dtype=F32))


MIX_TM = 512


def _mix_kernel(x_ref, mod_ref, ya_ref, yb_ref, ga_ref, gb_ref, wa_ref, wb_ref, wo_ref, x1_ref):
    ya = jnp.concatenate([ya_ref[i] for i in range(N_HEADS)], axis=1)
    yb = jnp.concatenate([yb_ref[i] for i in range(N_GROUPS)], axis=1)
    ga = jnp.concatenate([ga_ref[i] for i in range(8)], axis=1).astype(F32)
    gb = jnp.concatenate([gb_ref[i] for i in range(8)], axis=1).astype(F32)
    merged = jax.nn.sigmoid(ga) * _dot(ya, wa_ref[...]) + jax.nn.sigmoid(gb) * _dot(yb, wb_ref[...])
    out = _dot(merged.astype(BF16), wo_ref[...])
    x1_ref[...] = x_ref[...] + mod_ref[0, 2:3, :] * out


def _mix(x2d, mod3, mod_row_fn, ya, yb, p3, wa, wb, wo):
    n = x2d.shape[0]
    return pl.pallas_call(
        _mix_kernel,
        out_shape=jax.ShapeDtypeStruct((n, D_MODEL), F32),
        grid=(n // MIX_TM,),
        in_specs=[
            pl.BlockSpec((MIX_TM, D_MODEL), lambda i: (i, 0)),
            pl.BlockSpec((1, 6, D_MODEL), lambda i: (mod_row_fn(i * MIX_TM), 0, 0)),
            pl.BlockSpec((N_HEADS, MIX_TM, LANES), lambda i: (0, i, 0)),
            pl.BlockSpec((N_GROUPS, MIX_TM, LANES), lambda i: (0, i, 0)),
            pl.BlockSpec((8, MIX_TM, LANES), lambda i: (SLAB_GA // 8, i, 0)),
            pl.BlockSpec((8, MIX_TM, LANES), lambda i: (SLAB_GB // 8, i, 0)),
            _const_spec((D_MODEL, D_MODEL)),
            _const_spec((N_GROUPS * LANES, D_MODEL)),
            _const_spec((D_MODEL, D_MODEL)),
        ],
        out_specs=pl.BlockSpec((MIX_TM, D_MODEL), lambda i: (i, 0)),
        compiler_params=_cparams(("parallel",)),
        name="mix",
    )(x2d, mod3, ya, yb, p3, p3, wa, wb, wo)


MOE_TB = 1024
MOE_TILE = 160
ROUTER_ROWS = 4 + N_EXPERTS


def _nt_dot(a, b):
    return lax.dot_general(a, b, (((1,), (1,)), ((), ())), preferred_element_type=F32)


def _route_rows(lt):
    lg = [lt[i:i + 1] for i in range(N_GROUPS)]
    mx = jnp.maximum(jnp.maximum(lg[0], lg[1]), jnp.maximum(lg[2], lg[3]))
    p_top = 1.0 / (jnp.exp(lg[0] - mx) + jnp.exp(lg[1] - mx) + jnp.exp(lg[2] - mx) + jnp.exp(lg[3] - mx))
    gidx = jnp.zeros(lg[0].shape, jnp.int32)
    best = lg[0]
    for i in range(1, N_GROUPS):
        upd = lg[i] > best
        gidx = jnp.where(upd, i, gidx)
        best = jnp.where(upd, lg[i], best)

    def expert_row(g, e):
        r = N_GROUPS + g * EXPERTS_PER_GROUP + e
        return lt[r:r + 1]

    le = [jnp.where(gidx == 0, expert_row(0, e),
                    jnp.where(gidx == 1, expert_row(1, e),
                              jnp.where(gidx == 2, expert_row(2, e), expert_row(3, e))))
          for e in range(EXPERTS_PER_GROUP)]
    v1 = jnp.maximum(jnp.maximum(le[0], le[1]), jnp.maximum(le[2], le[3]))
    i1 = jnp.full(v1.shape, EXPERTS_PER_GROUP - 1, jnp.int32)
    for e in range(EXPERTS_PER_GROUP - 2, -1, -1):
        i1 = jnp.where(le[e] == v1, e, i1)
    le2 = [jnp.where(i1 == e, -jnp.inf, le[e]) for e in range(EXPERTS_PER_GROUP)]
    v2 = jnp.maximum(jnp.maximum(le2[0], le2[1]), jnp.maximum(le2[2], le2[3]))
    i2 = jnp.full(v2.shape, EXPERTS_PER_GROUP - 1, jnp.int32)
    for e in range(EXPERTS_PER_GROUP - 2, -1, -1):
        i2 = jnp.where(le2[e] == v2, e, i2)
    e2 = jnp.exp(v2 - v1)
    w1 = p_top / (1.0 + e2)
    w2 = p_top * e2 / (1.0 + e2)
    w_rows, sel_rows = [], []
    for g in range(N_GROUPS):
        for e in range(EXPERTS_PER_GROUP):
            first = jnp.where(gidx == g, jnp.where(i1 == e, 1.0, 0.0), 0.0)
            second = jnp.where(gidx == g, jnp.where(i2 == e, 1.0, 0.0), 0.0)
            w_rows.append(first * w1 + second * w2)
            sel_rows.append(first + second)
    return w_rows, sel_rows


def _moe_kernel(x1_ref, mod_ref, g2_ref, wrt_hi_ref, wrt_lo_ref, brt_ref, wg_ref, wu_ref, wd_ref, gf_ref,
                y_ref, h2_scr, rank_scr, w_scr, acc_scr):
    e = pl.program_id(1)

    @pl.when(e == 0)
    def _():
        x1 = x1_ref[...]
        h2 = x1 * _rms_scale(x1) * g2_ref[...]
        h2 = h2 * (1.0 + mod_ref[0, 4:5, :]) + mod_ref[0, 3:4, :]
        h_hi, h_lo = _split_bf16(h2)
        h2_scr[...] = h_hi
        wr_hi = wrt_hi_ref[...]
        lt = _nt_dot(wr_hi, h_hi) + _nt_dot(wr_hi, h_lo) + _nt_dot(wrt_lo_ref[...], h_hi) + brt_ref[...]
        w_rows, sel_rows = _route_rows(lt)
        sel = jnp.concatenate(sel_rows, axis=0)
        r_i = lax.broadcasted_iota(jnp.int32, (MOE_TB, MOE_TB), 0)
        c_i = lax.broadcasted_iota(jnp.int32, (MOE_TB, MOE_TB), 1)
        before = jnp.where(r_i < c_i, 1.0, 0.0).astype(BF16)
        rank = _dot(sel.astype(BF16), before)
        rank = jnp.where(sel > 0.0, rank, -1.0)
        for i in range(N_EXPERTS):
            rank_scr[i] = rank[i:i + 1]
            w_scr[i] = w_rows[i]
        acc_scr[...] = jnp.zeros_like(acc_scr)

    rank_row = rank_scr[e]
    w_row = w_scr[e]
    n_routed = jnp.max(rank_row).astype(jnp.int32) + 1
    n_pass = jnp.int32(0)
    for kk in range(-(-MOE_TB // MOE_TILE)):
        n_pass = n_pass + (n_routed > kk * MOE_TILE).astype(jnp.int32)
    slot = lax.broadcasted_iota(jnp.int32, (MOE_TILE, MOE_TB), 0).astype(F32)

    def one_pass(j, carry):
        hit = (slot + (j * MOE_TILE).astype(F32)) == rank_row
        pb = jnp.where(hit, 1.0, 0.0).astype(BF16)
        w_slot = jnp.sum(jnp.where(hit, w_row, 0.0), axis=1, keepdims=True)
        xg = _dot(pb, h2_scr[...]).astype(BF16)
        a = _dot(xg, wg_ref[0])
        u = _dot(xg, wu_ref[0])
        act = (a * jax.nn.sigmoid(a) * u * w_slot).astype(BF16)
        yv = _dot(act, wd_ref[0]).astype(BF16)
        acc_scr[...] += lax.dot_general(pb, yv, (((0,), (0,)), ((), ())), preferred_element_type=F32)
        return carry

    lax.fori_loop(0, n_pass, one_pass, 0)

    @pl.when(e == N_EXPERTS - 1)
    def _():
        x2 = x1_ref[...] + mod_ref[0, 5:6, :] * acc_scr[...]
        y_ref[...] = x2 * _rms_scale(x2) * gf_ref[...]


def _moe(x1, mod3, mod_row_fn, g_norm2, wrt_hi, wrt_lo, brt, wg, wu, wd, g_final):
    n = x1.shape[0]
    return pl.pallas_call(
        _moe_kernel,
        out_shape=jax.ShapeDtypeStruct((n, D_MODEL), F32),
        grid=(n // MOE_TB, N_EXPERTS),
        in_specs=[
            pl.BlockSpec((MOE_TB, D_MODEL), lambda i, e: (i, 0)),
            pl.BlockSpec((1, 6, D_MODEL), lambda i, e: (mod_row_fn(i * MOE_TB), 0, 0)),
            _const_spec((1, D_MODEL)),
            _const_spec((LANES, D_MODEL)),
            _const_spec((LANES, D_MODEL)),
            _const_spec((LANES, 1)),
            pl.BlockSpec((1, D_MODEL, D_EXPERT), lambda i, e: (e, 0, 0)),
            pl.BlockSpec((1, D_MODEL, D_EXPERT), lambda i, e: (e, 0, 0)),
            pl.BlockSpec((1, D_EXPERT, D_MODEL), lambda i, e: (e, 0, 0)),
            _const_spec((1, D_MODEL)),
        ],
        out_specs=pl.BlockSpec((MOE_TB, D_MODEL), lambda i, e: (i, 0)),
        scratch_shapes=[pltpu.VMEM((MOE_TB, D_MODEL), BF16),
                        pltpu.VMEM((N_EXPERTS, 1, MOE_TB), F32),
                        pltpu.VMEM((N_EXPERTS, 1, MOE_TB), F32),
                        pltpu.VMEM((MOE_TB, D_MODEL), F32)],
        compiler_params=_cparams(("parallel", "arbitrary")),
        name="moe",
    )(x1, mod3, g_norm2, wrt_hi, wrt_lo, brt, wg, wu, wd, g_final)


def _stream(x, mod3, mod_row_fn, seq_len, rows_2d, state, w):
    bsz, t, _ = x.shape
    n = bsz * t
    x2d = x.reshape(n, D_MODEL)
    p3, cols = _in_projection(x2d, mod3, mod_row_fn, w["g_norm1"], w["wmain"], w["bmain"], w["colscale"],
                              w["wg_hi"], w["wg_lo"], w["bg"])
    rows_f = (-cols[:, N_HEADS:2 * N_HEADS]).T.reshape(N_HEADS, 1, n)
    rows_b = (-cols[:, 3 * N_HEADS:4 * N_HEADS]).T.reshape(N_HEADS, 1, n)
    ml = _mlstm(p3, cols, rows_f, rows_b, w["gn3"], state)
    amat, invc = _pool_constants(seq_len, rows_2d)
    n_seq = max(1, 1024 // seq_len)
    yb = _pool(p3, amat, invc, w["wpg"], w["pscale"], seq_len, n_seq)
    ya = ml if state is not None else ml[0]
    x1 = _mix(x2d, mod3, mod_row_fn, ya, yb, p3, w["wa"], w["wb"], w["wo"])
    y = _moe(x1, mod3, mod_row_fn, w["g_norm2"], w["wrt_hi"], w["wrt_lo"], w["brt"],
             w["wg"], w["wu"], w["wd"], w["g_final"])
    return y.reshape(bsz, t, D_MODEL), ml


def kernel(x_prompt, x_sample, state_C, state_n, state_m, c, c_ctx, w_mod, b_mod, g_norm1, w_in, b_in, gn_gain, w_pool_grp, pool_scale, w_proj_a, w_proj_b, w_out, g_norm2, w_router_grp, b_router_grp, w_router_exp, b_router_exp, w_exp_gate, w_exp_up, w_exp_down, g_final):
    bp, tp, _ = x_prompt.shape
    bs, ts, _ = x_sample.shape
    assert w_mod.shape[0] == 1, "single trunk layer"
    assert tp == CHUNK and ts % CHUNK == 0 and ts == ML_ROWS

    cvec = jnp.zeros((8, D_MODEL), F32).at[0].set(c_ctx).at[1:1 + bs].set(c)
    mod3 = _modulation(cvec, w_mod[0], b_mod[0])

    wi, bi = w_in[0], b_in[0]
    d = D_MODEL
    o_q, o_k, o_v, o_o, o_g, o_p, o_ga, o_gb = 0, d, 2 * d, 3 * d, 4 * d, 4 * d + 32, 4 * d + 32 + 512, 4 * d + 32 + 512 + d
    order = [(o_q, d), (o_k, d), (o_v, d), (o_o, d), (o_ga, d), (o_gb, d), (o_p, 512)]
    wmain = jnp.concatenate([wi[:, s:s + l] for s, l in order], axis=1).astype(BF16)
    bmain = jnp.concatenate([bi[s:s + l] for s, l in order]).reshape(1, D_MAIN)
    colscale = jnp.ones((D_MAIN,), F32).at[SLAB_K * LANES:(SLAB_K + N_HEADS) * LANES].set(HEAD_DIM ** -0.5)
    colscale = colscale.reshape(1, D_MAIN)
    gw = wi[:, o_g:o_g + 32]
    gb = bi[o_g:o_g + 32]
    i_f, f_f, i_b, f_b = (slice(0, 8), slice(8, 16), slice(16, 24), slice(24, 32))
    zw = jnp.zeros((d, 8), F32)
    zb = jnp.zeros((8,), F32)
    wg2 = jnp.concatenate([gw[:, f_f], gw[:, f_f], gw[:, f_b], gw[:, f_b], jnp.zeros((d, LANES - 32), F32),
                           zw, gw[:, i_f], zw, gw[:, i_b], jnp.zeros((d, LANES - 32), F32)], axis=1)
    bg2 = jnp.concatenate([gb[f_f], gb[f_f], gb[f_b], gb[f_b], jnp.zeros((LANES - 32,), F32),
                           zb, gb[i_f], zb, gb[i_b], jnp.zeros((LANES - 32,), F32)]).reshape(1, 2 * LANES)
    wg_hi, wg_lo = _split_bf16(wg2)

    wrt = jnp.concatenate([w_router_grp[0], w_router_exp[0].reshape(d, N_EXPERTS),
                           jnp.zeros((d, LANES - ROUTER_ROWS), F32)], axis=1).T
    brt = jnp.concatenate([b_router_grp[0], b_router_exp[0].reshape(N_EXPERTS),
                           jnp.zeros((LANES - ROUTER_ROWS,), F32)]).reshape(LANES, 1)
    wrt_hi, wrt_lo = _split_bf16(wrt)

    w = dict(
        g_norm1=g_norm1[0].reshape(1, d), wmain=wmain, bmain=bmain, colscale=colscale,
        wg_hi=wg_hi, wg_lo=wg_lo, bg=bg2,
        gn3=gn_gain[0].reshape(N_HEADS, 1, HEAD_DIM),
        wpg=w_pool_grp[0].astype(BF16), pscale=pool_scale[0].reshape(N_GROUPS, 1, LANES),
        wa=w_proj_a[0].astype(BF16), wb=w_proj_b[0].astype(BF16), wo=w_out[0].astype(BF16),
        g_norm2=g_norm2[0].reshape(1, d), wrt_hi=wrt_hi, wrt_lo=wrt_lo, brt=brt,
        wg=w_exp_gate[0].astype(BF16), wu=w_exp_up[0].astype(BF16), wd=w_exp_down[0].astype(BF16),
        g_final=g_final.reshape(1, d),
    )

    y_prompt, (_, cst, nst, mst) = _stream(x_prompt, mod3, lambda r: 0, tp, None, None, w)
    n_rep = jnp.broadcast_to(state_n[..., None], state_C.shape)
    state = (state_C, n_rep, state_m[:, 0].reshape(bs, 2, N_HEADS, 1, 1))
    y_sample, _ = _stream(x_sample, mod3, lambda r: 1 + r // ts, ts, ts // GRID_W, state, w)

    new_c = cst
    new_n = nst.reshape(bp, 1, 2, N_HEADS, HEAD_DIM)
    new_m = mst.reshape(bp, 1, 2, N_HEADS)
    return (y_prompt, y_sample, new_c, new_n, new_m)
```

```python
import functools

import numpy as np
import jax
import jax.numpy as jnp
from jax import lax
from jax.experimental import pallas as pl
from jax.experimental.pallas import tpu as pltpu

F32 = jnp.float32
BF16 = jnp.bfloat16

D_MODEL = 1024
N_HEADS = 8
HEAD_DIM = 128
GRID_W = 64
POOL_WINDOWS = (2, 4, 8, 16)
N_GROUPS = 4
EXPERTS_PER_GROUP = 4
N_EXPERTS = 16
D_EXPERT = 512
EPS = 1e-6
NEG = -1e30

LANES = 128
CHUNK = 256
N_SLABS = 52
SLAB_Q, SLAB_K, SLAB_V, SLAB_O, SLAB_GA, SLAB_GB, SLAB_POOL = 0, 8, 16, 24, 32, 40, 48
D_MAIN = N_SLABS * LANES
VMEM_LIMIT = 56 * 1024 * 1024


def _cparams(sem):
    return pltpu.CompilerParams(dimension_semantics=sem, vmem_limit_bytes=VMEM_LIMIT)


def _const_spec(shape):
    nd = len(shape)
    return pl.BlockSpec(shape, lambda *_: (0,) * nd, pipeline_mode=pl.Buffered(1))


def _split_bf16(x):
    hi = x.astype(BF16)
    lo = (x - hi.astype(F32)).astype(BF16)
    return hi, lo


def _dot(a, b):
    return jnp.dot(a, b, preferred_element_type=F32)


def _dot3(a, b_hi, b_lo):
    a_hi, a_lo = _split_bf16(a)
    return _dot(a_hi, b_hi) + _dot(a_hi, b_lo) + _dot(a_lo, b_hi)


def _rms_scale(x):
    return lax.rsqrt(jnp.mean(x * x, axis=-1, keepdims=True) + EPS)


def _log_sigmoid(x):
    return jnp.minimum(x, 0.0) - jnp.log1p(jnp.exp(-jnp.abs(x)))


def _mod_kernel(c_ref, w_ref, b_ref, o_ref):
    c = c_ref[...]
    s = c * jax.nn.sigmoid(c)
    w_hi, w_lo = _split_bf16(w_ref[...])
    o_ref[...] = _dot3(s, w_hi, w_lo) + b_ref[...]


def _modulation(cvec, w_mod, b_mod):
    tn = 512
    out = pl.pallas_call(
        _mod_kernel,
        out_shape=jax.ShapeDtypeStruct((8, 6 * D_MODEL), F32),
        grid=(6 * D_MODEL // tn,),
        in_specs=[
            pl.BlockSpec((8, D_MODEL), lambda j: (0, 0)),
            pl.BlockSpec((D_MODEL, tn), lambda j: (0, j)),
            pl.BlockSpec((1, tn), lambda j: (0, j)),
        ],
        out_specs=pl.BlockSpec((8, tn), lambda j: (0, j)),
        compiler_params=_cparams(("parallel",)),
        name="modulation",
    )(cvec, w_mod, b_mod.reshape(1, -1))
    return out.reshape(8, 6, D_MODEL)


IN_TM = 512
IN_TN = 512
CMAX_LANE_SHIFT = 32
LOG2E = 1.4426950408889634


def _inproj_kernel(x_ref, mod_ref, g_ref, w_ref, b_ref, cs_ref, wg_hi_ref, wg_lo_ref, bg_ref,
                   p3_ref, cols_ref):
    x = x_ref[...]
    h = x * _rms_scale(x) * g_ref[...]
    h = h * (1.0 + mod_ref[0, 1:2, :]) + mod_ref[0, 0:1, :]
    hb = h.astype(BF16)
    for n in range(D_MAIN // IN_TN):
        sl = slice(n * IN_TN, (n + 1) * IN_TN)
        acc = (_dot(hb, w_ref[:, sl]) + b_ref[:, sl]) * cs_ref[:, sl]
        accb = acc.astype(BF16)
        for j in range(IN_TN // LANES):
            p3_ref[n * (IN_TN // LANES) + j] = accb[:, j * LANES:(j + 1) * LANES]

    g2 = _dot3(h, wg_hi_ref[...], wg_lo_ref[...]) + bg_ref[...]
    ls = _log_sigmoid(g2[:, :LANES])
    gi = g2[:, LANES:]
    row = lax.broadcasted_iota(jnp.int32, (CHUNK, LANES), 0)
    lane = lax.broadcasted_iota(jnp.int32, (CHUNK, LANES), 1)
    for c in range(IN_TM // CHUNK):
        x0 = ls[c * CHUNK:(c + 1) * CHUNK]
        cf = x0
        cb = x0
        s = 1
        while s < CHUNK:
            cf = cf + jnp.where(row >= s, pltpu.roll(cf, s, 0), 0.0)
            cb = cb + jnp.where(row < CHUNK - s, pltpu.roll(cb, CHUNK - s, 0), 0.0)
            s *= 2
        b = jnp.where(lane < 2 * N_HEADS, cf, cb)
        is_imb = (lane & N_HEADS) != 0
        out = jnp.where(is_imb, gi[c * CHUNK:(c + 1) * CHUNK] - b, b)
        mf = out
        mb = out
        s = 1
        while s < CHUNK:
            mf = jnp.maximum(mf, jnp.where(row >= s, pltpu.roll(mf, s, 0), -jnp.inf))
            mb = jnp.maximum(mb, jnp.where(row < CHUNK - s, pltpu.roll(mb, CHUNK - s, 0), -jnp.inf))
            s *= 2
        cmax = pltpu.roll(jnp.where(lane < 2 * N_HEADS, mf, mb), CMAX_LANE_SHIFT, 1)
        cols_ref[c * CHUNK:(c + 1) * CHUNK, :] = jnp.where(lane < CMAX_LANE_SHIFT, out, cmax)


def _in_projection(x2d, mod3, mod_row_fn, g_norm1, wmain, bmain, colscale, wg_hi, wg_lo, bg):
    n = x2d.shape[0]
    return pl.pallas_call(
        _inproj_kernel,
        out_shape=(jax.ShapeDtypeStruct((N_SLABS, n, LANES), BF16),
                   jax.ShapeDtypeStruct((n, LANES), F32)),
        grid=(n // IN_TM,),
        in_specs=[
            pl.BlockSpec((IN_TM, D_MODEL), lambda i: (i, 0)),
            pl.BlockSpec((1, 6, D_MODEL), lambda i: (mod_row_fn(i * IN_TM), 0, 0)),
            _const_spec((1, D_MODEL)),
            _const_spec((D_MODEL, D_MAIN)),
            _const_spec((1, D_MAIN)),
            _const_spec((1, D_MAIN)),
            _const_spec((D_MODEL, 2 * LANES)),
            _const_spec((D_MODEL, 2 * LANES)),
            _const_spec((1, 2 * LANES)),
        ],
        out_specs=(pl.BlockSpec((N_SLABS, IN_TM, LANES), lambda i: (0, i, 0)),
                   pl.BlockSpec((IN_TM, LANES), lambda i: (i, 0))),
        compiler_params=_cparams(("parallel",)),
        name="in_projection",
    )(x2d, mod3, g_norm1, wmain, bmain, colscale, wg_hi, wg_lo, bg)


ML_ROWS = 1024
ML_NCH = ML_ROWS // CHUNK


def _pick_col(blk, lane_iota, idx):
    return jnp.sum(jnp.where(lane_iota == idx, blk, 0.0), axis=1, keepdims=True)


def _mlstm_kernel(*refs, carry):
    if carry:
        (q_ref, k_ref, v_ref, o_ref, cols_ref, rf_ref, rb_ref, gn_ref, c0_ref, n0_ref, m0_ref,
         ya_ref, s_scr, h_scr) = refs
    else:
        (q_ref, k_ref, v_ref, o_ref, cols_ref, rf_ref, rb_ref, gn_ref,
         ya_ref, cst_ref, nst_ref, mst_ref, s_scr, h_scr) = refs
    head = pl.program_id(1)
    lane = lax.broadcasted_iota(jnp.int32, (CHUNK, LANES), 1)
    r_idx = lax.broadcasted_iota(jnp.int32, (CHUNK, CHUNK), 0)
    c_idx = lax.broadcasted_iota(jnp.int32, (CHUNK, CHUNK), 1)
    ones = jnp.ones((CHUNK, HEAD_DIM), BF16)
    tn_dims = (((0,), (0,)), ((), ()))

    for d in range(2):
        mask = (c_idx <= r_idx) if d == 0 else (c_idx >= r_idx)
        row_ref = rf_ref if d == 0 else rb_ref
        last = CHUNK - 1 if d == 0 else 0
        base = head + 2 * N_HEADS * d
        if carry:
            cn = jnp.concatenate([c0_ref[0, 0, d, 0], n0_ref[0, 0, d, 0]], axis=1)
            m = m0_ref[0, d, 0]
        order = range(ML_NCH) if d == 0 else range(ML_NCH - 1, -1, -1)
        for step, c in enumerate(order):
            rows = slice(c * CHUNK, (c + 1) * CHUNK)
            q = q_ref[0, rows, :]
            k = k_ref[0, rows, :]
            v = v_ref[0, rows, :]
            v1 = jnp.concatenate([v, ones], axis=1)
            if d == 0:
                s = lax.dot_general(q, k, (((1,), (1,)), ((), ())), preferred_element_type=F32)
                s_scr[rows, :] = s
            else:
                s = s_scr[rows, :]
            blk = cols_ref[rows, :]
            bcol = _pick_col(blk, lane, base)
            imbcol = _pick_col(blk, lane, base + N_HEADS)
            a = bcol + _pick_col(blk, lane, base + N_HEADS + CMAX_LANE_SHIFT)
            bmirow = row_ref[0, :, rows]
            inter = bcol + (m if carry else NEG)
            m_t = jnp.maximum(inter, a)
            arg = (bcol - m_t) * LOG2E - bmirow * LOG2E
            sw = s * jnp.exp2(jnp.where(mask, arg, NEG))
            nd = _dot(sw.astype(BF16), v1)
            if carry:
                nd = nd + jnp.exp(inter - m_t) * _dot(q, cn.astype(BF16))
            hdir = nd[:, :HEAD_DIM] / jnp.maximum(jnp.abs(nd[:, HEAD_DIM:]), jnp.exp(-m_t))
            if d == 0:
                h_scr[rows, :] = hdir
            else:
                h_scr[rows, :] = h_scr[rows, :] + hdir

            if carry and step == ML_NCH - 1:
                continue
            m_new = m_t[last:last + 1, :]
            b_last = bcol[last:last + 1, :]
            w_col = jnp.exp(b_last - m_new + imbcol)
            if carry:
                u = lax.dot_general(k, (w_col * v1.astype(F32)).astype(BF16), tn_dims,
                                    preferred_element_type=F32)
                cn = jnp.exp(b_last + m - m_new) * cn + u
                m = m_new
            else:
                u = lax.dot_general(k, (w_col * v.astype(F32)).astype(BF16), tn_dims,
                                    preferred_element_type=F32)
                w_row = jnp.exp(b_last - m_new - bmirow)
                w8 = jnp.broadcast_to(w_row, (8, CHUNK)).astype(BF16)
                cst_ref[c, 0, d, 0] = u
                nst_ref[c, d, 0] = _dot(w8, k)[0:1, :]
                mst_ref[c, d, 0] = m_new

    hh = h_scr[...]
    hn = hh * _rms_scale(hh) * gn_ref[0]
    ya_ref[0] = (hn * jax.nn.sigmoid(o_ref[0].astype(F32))).astype(BF16)


def _mlstm(p3, cols, rows_f, rows_b, gn3, state=None):
    n = p3.shape[1]
    nblk = n // ML_ROWS
    carry = state is not None

    def slab(base):
        return pl.BlockSpec((1, ML_ROWS, LANES), lambda g, h: (base + h, g, 0))

    in_specs = [
        slab(SLAB_Q), slab(SLAB_K), slab(SLAB_V), slab(SLAB_O),
        pl.BlockSpec((ML_ROWS, LANES), lambda g, h: (g, 0)),
        pl.BlockSpec((1, 1, ML_ROWS), lambda g, h: (h, 0, g)),
        pl.BlockSpec((1, 1, ML_ROWS), lambda g, h: (h, 0, g)),
        pl.BlockSpec((1, 1, LANES), lambda g, h: (h, 0, 0)),
    ]
    args = [p3, p3, p3, p3, cols, rows_f, rows_b, gn3]
    ya_shape = jax.ShapeDtypeStruct((N_HEADS, n, LANES), BF16)
    ya_spec = pl.BlockSpec((1, ML_ROWS, LANES), lambda g, h: (h, g, 0))
    if carry:
        c0, n0, m0 = state
        in_specs += [
            pl.BlockSpec((1, 1, 2, 1, HEAD_DIM, HEAD_DIM), lambda g, h: (g, 0, 0, h, 0, 0)),
            pl.BlockSpec((1, 1, 2, 1, HEAD_DIM, HEAD_DIM), lambda g, h: (g, 0, 0, h, 0, 0)),
            pl.BlockSpec((1, 2, 1, 1, 1), lambda g, h: (g, 0, h, 0, 0)),
        ]
        args += [c0, n0, m0]
        out_shape = ya_shape
        out_specs = ya_spec
    else:
        nseq = n // CHUNK
        out_shape = (ya_shape,
                     jax.ShapeDtypeStruct((nseq, 1, 2, N_HEADS, HEAD_DIM, HEAD_DIM), F32),
                     jax.ShapeDtypeStruct((nseq, 2, N_HEADS, 1, HEAD_DIM), F32),
                     jax.ShapeDtypeStruct((nseq, 2, N_HEADS, 1, 1), F32))
        out_specs = (ya_spec,
                     pl.BlockSpec((ML_NCH, 1, 2, 1, HEAD_DIM, HEAD_DIM), lambda g, h: (g, 0, 0, h, 0, 0)),
                     pl.BlockSpec((ML_NCH, 2, 1, 1, HEAD_DIM), lambda g, h: (g, 0, h, 0, 0)),
                     pl.BlockSpec((ML_NCH, 2, 1, 1, 1), lambda g, h: (g, 0, h, 0, 0)))
    return pl.pallas_call(
        functools.partial(_mlstm_kernel, carry=carry),
        out_shape=out_shape,
        grid=(nblk, N_HEADS),
        in_specs=in_specs,
        out_specs=out_specs,
        scratch_shapes=[pltpu.VMEM((ML_ROWS, CHUNK), F32), pltpu.VMEM((ML_ROWS, HEAD_DIM), F32)],
        compiler_params=_cparams(("parallel", "parallel")),
        name="mlstm_latent" if carry else "mlstm_prompt",
    )(*args)


def _pool_kernel(u_ref, a_ref, ic_ref, wpg_ref, ps_ref, yb_ref, *, seq_len, n_seq):
    for sq in range(n_seq):
        rows = slice(sq * seq_len, (sq + 1) * seq_len)
        for g in range(N_GROUPS):
            u = u_ref[g, rows, :]
            pooled = _dot(a_ref[g], u) * ic_ref[g]
            p = (pooled - u.astype(F32)).astype(BF16)
            yb_ref[g, rows, :] = (_dot(p, wpg_ref[g]) * ps_ref[g]).astype(BF16)


def _pool(p3, amat, invc, wpg, pscale, seq_len, n_seq):
    n = p3.shape[1]
    rows = seq_len * n_seq
    return pl.pallas_call(
        functools.partial(_pool_kernel, seq_len=seq_len, n_seq=n_seq),
        out_shape=jax.ShapeDtypeStruct((N_GROUPS, n, LANES), BF16),
        grid=(n // rows,),
        in_specs=[
            pl.BlockSpec((N_GROUPS, rows, LANES), lambda i: (SLAB_POOL // N_GROUPS, i, 0)),
            _const_spec((N_GROUPS, seq_len, seq_len)),
            _const_spec((N_GROUPS, seq_len, LANES)),
            _const_spec((N_GROUPS, LANES, LANES)),
            _const_spec((N_GROUPS, 1, LANES)),
        ],
        out_specs=pl.BlockSpec((N_GROUPS, rows, LANES), lambda i: (0, i, 0)),
        compiler_params=_cparams(("parallel",)),
        name="pool",
    )(p3, amat, invc, wpg, pscale)


def _window_matrix(size, w):
    idx = np.arange(size)
    lo = np.clip(idx - w // 2, 0, size)
    hi = np.clip(idx - w // 2 + w, 0, size)
    s = np.arange(size)[None, :]
    a = ((s >= lo[:, None]) & (s < hi[:, None])).astype(np.float32)
    return a, (hi - lo).astype(np.float32)


def _pool_constants(seq_len, rows_2d):
    mats, invs = [], []
    for w in POOL_WINDOWS:
        if rows_2d is None:
            a, cnt = _window_matrix(seq_len, w)
        else:
            ar, cr = _window_matrix(rows_2d, w)
            ac, cc = _window_matrix(GRID_W, w)
            a = np.kron(ar, ac)
            cnt = np.kron(cr, cc)
        mats.append(a)
        invs.append(np.repeat((1.0 / cnt)[:, None], LANES, axis=1))
    return (jnp.asarray(np.stack(mats), dtype=BF16), jnp.asarray(np.stack(invs), dtype=F32))


MIX_TM = 512


def _mix_kernel(x_ref, mod_ref, ya_ref, yb_ref, ga_ref, gb_ref, wa_ref, wb_ref, wo_ref, x1_ref):
    ya = jnp.concatenate([ya_ref[i] for i in range(N_HEADS)], axis=1)
    yb = jnp.concatenate([yb_ref[i] for i in range(N_GROUPS)], axis=1)
    ga = jnp.concatenate([ga_ref[i] for i in range(8)], axis=1).astype(F32)
    gb = jnp.concatenate([gb_ref[i] for i in range(8)], axis=1).astype(F32)
    merged = jax.nn.sigmoid(ga) * _dot(ya, wa_ref[...]) + jax.nn.sigmoid(gb) * _dot(yb, wb_ref[...])
    out = _dot(merged.astype(BF16), wo_ref[...])
    x1_ref[...] = x_ref[...] + mod_ref[0, 2:3, :] * out


def _mix(x2d, mod3, mod_row_fn, ya, yb, p3, wa, wb, wo):
    n = x2d.shape[0]
    return pl.pallas_call(
        _mix_kernel,
        out_shape=jax.ShapeDtypeStruct((n, D_MODEL), F32),
        grid=(n // MIX_TM,),
        in_specs=[
            pl.BlockSpec((MIX_TM, D_MODEL), lambda i: (i, 0)),
            pl.BlockSpec((1, 6, D_MODEL), lambda i: (mod_row_fn(i * MIX_TM), 0, 0)),
            pl.BlockSpec((N_HEADS, MIX_TM, LANES), lambda i: (0, i, 0)),
            pl.BlockSpec((N_GROUPS, MIX_TM, LANES), lambda i: (0, i, 0)),
            pl.BlockSpec((8, MIX_TM, LANES), lambda i: (SLAB_GA // 8, i, 0)),
            pl.BlockSpec((8, MIX_TM, LANES), lambda i: (SLAB_GB // 8, i, 0)),
            _const_spec((D_MODEL, D_MODEL)),
            _const_spec((N_GROUPS * LANES, D_MODEL)),
            _const_spec((D_MODEL, D_MODEL)),
        ],
        out_specs=pl.BlockSpec((MIX_TM, D_MODEL), lambda i: (i, 0)),
        compiler_params=_cparams(("parallel",)),
        name="mix",
    )(x2d, mod3, ya, yb, p3, p3, wa, wb, wo)


MOE_TB = 1024
MOE_TILE = 160
MOE_EPS = 2
ROUTER_ROWS = 4 + N_EXPERTS


def _nt_dot(a, b):
    return lax.dot_general(a, b, (((1,), (1,)), ((), ())), preferred_element_type=F32)


def _route_rows(lt):
    lg = [lt[i:i + 1] for i in range(N_GROUPS)]
    mx = jnp.maximum(jnp.maximum(lg[0], lg[1]), jnp.maximum(lg[2], lg[3]))
    p_top = 1.0 / (jnp.exp(lg[0] - mx) + jnp.exp(lg[1] - mx) + jnp.exp(lg[2] - mx) + jnp.exp(lg[3] - mx))
    gidx = jnp.zeros(lg[0].shape, jnp.int32)
    best = lg[0]
    for i in range(1, N_GROUPS):
        upd = lg[i] > best
        gidx = jnp.where(upd, i, gidx)
        best = jnp.where(upd, lg[i], best)

    def expert_row(g, e):
        r = N_GROUPS + g * EXPERTS_PER_GROUP + e
        return lt[r:r + 1]

    le = [jnp.where(gidx == 0, expert_row(0, e),
                    jnp.where(gidx == 1, expert_row(1, e),
                              jnp.where(gidx == 2, expert_row(2, e), expert_row(3, e))))
          for e in range(EXPERTS_PER_GROUP)]
    v1 = jnp.maximum(jnp.maximum(le[0], le[1]), jnp.maximum(le[2], le[3]))
    i1 = jnp.full(v1.shape, EXPERTS_PER_GROUP - 1, jnp.int32)
    for e in range(EXPERTS_PER_GROUP - 2, -1, -1):
        i1 = jnp.where(le[e] == v1, e, i1)
    le2 = [jnp.where(i1 == e, -jnp.inf, le[e]) for e in range(EXPERTS_PER_GROUP)]
    v2 = jnp.maximum(jnp.maximum(le2[0], le2[1]), jnp.maximum(le2[2], le2[3]))
    i2 = jnp.full(v2.shape, EXPERTS_PER_GROUP - 1, jnp.int32)
    for e in range(EXPERTS_PER_GROUP - 2, -1, -1):
        i2 = jnp.where(le2[e] == v2, e, i2)
    e2 = jnp.exp(v2 - v1)
    w1 = p_top / (1.0 + e2)
    w2 = p_top * e2 / (1.0 + e2)
    w_rows, sel_rows = [], []
    for g in range(N_GROUPS):
        for e in range(EXPERTS_PER_GROUP):
            first = jnp.where(gidx == g, jnp.where(i1 == e, 1.0, 0.0), 0.0)
            second = jnp.where(gidx == g, jnp.where(i2 == e, 1.0, 0.0), 0.0)
            w_rows.append(first * w1 + second * w2)
            sel_rows.append(first + second)
    return w_rows, sel_rows


def _moe_kernel(x1_ref, mod_ref, g2_ref, wrt_hi_ref, wrt_lo_ref, brt_ref, before_ref, wg_ref, wu_ref, wd_ref,
                gf_ref, y_ref, h2_scr, rank_scr, w_scr, acc_scr, p_scr, y_scr):
    step = pl.program_id(1)

    @pl.when(step == 0)
    def _():
        x1 = x1_ref[...]
        h2 = x1 * _rms_scale(x1) * g2_ref[...]
        h2 = h2 * (1.0 + mod_ref[0, 4:5, :]) + mod_ref[0, 3:4, :]
        h_hi, h_lo = _split_bf16(h2)
        h2_scr[...] = h_hi
        wr_hi = wrt_hi_ref[...]
        lt = _nt_dot(wr_hi, h_hi) + _nt_dot(wr_hi, h_lo) + _nt_dot(wrt_lo_ref[...], h_hi) + brt_ref[...]
        w_rows, sel_rows = _route_rows(lt)
        sel = jnp.concatenate(sel_rows, axis=0)
        rank = _dot(sel.astype(BF16), before_ref[...])
        rank = jnp.where(sel > 0.0, rank, -1.0)
        for i in range(N_EXPERTS):
            rank_scr[i] = rank[i:i + 1]
            w_scr[i] = w_rows[i]
        acc_scr[...] = jnp.zeros_like(acc_scr)

    slot = lax.broadcasted_iota(jnp.int32, (MOE_TILE, MOE_TB), 0).astype(F32)
    tn_dims = (((0,), (0,)), ((), ()))

    for ei in range(MOE_EPS):
        e = step * MOE_EPS + ei
        rank_row = rank_scr[e]
        w_row = w_scr[e]

        def expert_rows(first_slot, rank_row=rank_row, w_row=w_row, ei=ei):
            hit = (slot + first_slot) == rank_row
            pb = jnp.where(hit, 1.0, 0.0).astype(BF16)
            w_slot = jnp.sum(jnp.where(hit, w_row, 0.0), axis=1, keepdims=True)
            xg = _dot(pb, h2_scr[...]).astype(BF16)
            a = _dot(xg, wg_ref[ei])
            u = _dot(xg, wu_ref[ei])
            act = (a * jax.nn.sigmoid(a) * u * w_slot).astype(BF16)
            return pb, _dot(act, wd_ref[ei]).astype(BF16)

        pb, yv = expert_rows(0.0)
        tile_rows = pl.ds(pl.multiple_of(e * MOE_TILE, MOE_TILE), MOE_TILE)
        p_scr[tile_rows, :] = pb
        y_scr[tile_rows, :] = yv

        n_routed = jnp.max(rank_row).astype(jnp.int32) + 1
        n_pass = jnp.int32(0)
        for kk in range(-(-MOE_TB // MOE_TILE)):
            n_pass = n_pass + (n_routed > kk * MOE_TILE).astype(jnp.int32)

        def extra_pass(j, carry, expert_rows=expert_rows):
            pb, yv = expert_rows((j * MOE_TILE).astype(F32))
            acc_scr[...] += lax.dot_general(pb, yv, tn_dims, preferred_element_type=F32)
            return carry

        lax.fori_loop(1, n_pass, extra_pass, 0)

    @pl.when(step == N_EXPERTS // MOE_EPS - 1)
    def _():
        moe = acc_scr[...] + lax.dot_general(p_scr[...], y_scr[...], tn_dims, preferred_element_type=F32)
        x2 = x1_ref[...] + mod_ref[0, 5:6, :] * moe
        y_ref[...] = x2 * _rms_scale(x2) * gf_ref[...]


def _moe(x1, mod3, mod_row_fn, g_norm2, wrt_hi, wrt_lo, brt, wg, wu, wd, g_final):
    n = x1.shape[0]
    t_i = lax.broadcasted_iota(jnp.int32, (MOE_TB, MOE_TB), 0)
    t_j = lax.broadcasted_iota(jnp.int32, (MOE_TB, MOE_TB), 1)
    before = (t_i < t_j).astype(BF16)
    return pl.pallas_call(
        _moe_kernel,
        out_shape=jax.ShapeDtypeStruct((n, D_MODEL), F32),
        grid=(n // MOE_TB, N_EXPERTS // MOE_EPS),
        in_specs=[
            pl.BlockSpec((MOE_TB, D_MODEL), lambda i, e: (i, 0)),
            pl.BlockSpec((1, 6, D_MODEL), lambda i, e: (mod_row_fn(i * MOE_TB), 0, 0)),
            _const_spec((1, D_MODEL)),
            _const_spec((LANES, D_MODEL)),
            _const_spec((LANES, D_MODEL)),
            _const_spec((LANES, 1)),
            _const_spec((MOE_TB, MOE_TB)),
            pl.BlockSpec((MOE_EPS, D_MODEL, D_EXPERT), lambda i, e: (e, 0, 0)),
            pl.BlockSpec((MOE_EPS, D_MODEL, D_EXPERT), lambda i, e: (e, 0, 0)),
            pl.BlockSpec((MOE_EPS, D_EXPERT, D_MODEL), lambda i, e: (e, 0, 0)),
            _const_spec((1, D_MODEL)),
        ],
        out_specs=pl.BlockSpec((MOE_TB, D_MODEL), lambda i, e: (i, 0)),
        scratch_shapes=[pltpu.VMEM((MOE_TB, D_MODEL), BF16),
                        pltpu.VMEM((N_EXPERTS, 1, MOE_TB), F32),
                        pltpu.VMEM((N_EXPERTS, 1, MOE_TB), F32),
                        pltpu.VMEM((MOE_TB, D_MODEL), F32),
                        pltpu.VMEM((N_EXPERTS * MOE_TILE, MOE_TB), BF16),
                        pltpu.VMEM((N_EXPERTS * MOE_TILE, D_MODEL), BF16)],
        compiler_params=_cparams(("parallel", "arbitrary")),
        name="moe",
    )(x1, mod3, g_norm2, wrt_hi, wrt_lo, brt, before, wg, wu, wd, g_final)


def _stream(x, mod3, mod_row_fn, seq_len, rows_2d, state, w):
    bsz, t, _ = x.shape
    n = bsz * t
    x2d = x.reshape(n, D_MODEL)
    p3, cols = _in_projection(x2d, mod3, mod_row_fn, w["g_norm1"], w["wmain"], w["bmain"], w["colscale"],
                              w["wg_hi"], w["wg_lo"], w["bg"])
    rows_f = (-cols[:, N_HEADS:2 * N_HEADS]).T.reshape(N_HEADS, 1, n)
    rows_b = (-cols[:, 3 * N_HEADS:4 * N_HEADS]).T.reshape(N_HEADS, 1, n)
    ml = _mlstm(p3, cols, rows_f, rows_b, w["gn3"], state)
    amat, invc = _pool_constants(seq_len, rows_2d)
    n_seq = max(1, 1024 // seq_len)
    yb = _pool(p3, amat, invc, w["wpg"], w["pscale"], seq_len, n_seq)
    ya = ml if state is not None else ml[0]
    x1 = _mix(x2d, mod3, mod_row_fn, ya, yb, p3, w["wa"], w["wb"], w["wo"])
    y = _moe(x1, mod3, mod_row_fn, w["g_norm2"], w["wrt_hi"], w["wrt_lo"], w["brt"],
             w["wg"], w["wu"], w["wd"], w["g_final"])
    return y.reshape(bsz, t, D_MODEL), ml


def kernel(x_prompt, x_sample, state_C, state_n, state_m, c, c_ctx, w_mod, b_mod, g_norm1, w_in, b_in, gn_gain, w_pool_grp, pool_scale, w_proj_a, w_proj_b, w_out, g_norm2, w_router_grp, b_router_grp, w_router_exp, b_router_exp, w_exp_gate, w_exp_up, w_exp_down, g_final):
    bp, tp, _ = x_prompt.shape
    bs, ts, _ = x_sample.shape
    assert w_mod.shape[0] == 1, "single trunk layer"
    assert tp == CHUNK and ts % CHUNK == 0 and ts == ML_ROWS

    cvec = jnp.zeros((8, D_MODEL), F32).at[0].set(c_ctx).at[1:1 + bs].set(c)
    mod3 = _modulation(cvec, w_mod[0], b_mod[0])

    wi, bi = w_in[0], b_in[0]
    d = D_MODEL
    o_q, o_k, o_v, o_o, o_g, o_p, o_ga, o_gb = 0, d, 2 * d, 3 * d, 4 * d, 4 * d + 32, 4 * d + 32 + 512, 4 * d + 32 + 512 + d
    order = [(o_q, d), (o_k, d), (o_v, d), (o_o, d), (o_ga, d), (o_gb, d), (o_p, 512)]
    wmain = jnp.concatenate([wi[:, s:s + l] for s, l in order], axis=1).astype(BF16)
    bmain = jnp.concatenate([bi[s:s + l] for s, l in order]).reshape(1, D_MAIN)
    colscale = jnp.ones((D_MAIN,), F32).at[SLAB_K * LANES:(SLAB_K + N_HEADS) * LANES].set(HEAD_DIM ** -0.5)
    colscale = colscale.reshape(1, D_MAIN)
    gw = wi[:, o_g:o_g + 32]
    gb = bi[o_g:o_g + 32]
    i_f, f_f, i_b, f_b = (slice(0, 8), slice(8, 16), slice(16, 24), slice(24, 32))
    zw = jnp.zeros((d, 8), F32)
    zb = jnp.zeros((8,), F32)
    wg2 = jnp.concatenate([gw[:, f_f], gw[:, f_f], gw[:, f_b], gw[:, f_b], jnp.zeros((d, LANES - 32), F32),
                           zw, gw[:, i_f], zw, gw[:, i_b], jnp.zeros((d, LANES - 32), F32)], axis=1)
    bg2 = jnp.concatenate([gb[f_f], gb[f_f], gb[f_b], gb[f_b], jnp.zeros((LANES - 32,), F32),
                           zb, gb[i_f], zb, gb[i_b], jnp.zeros((LANES - 32,), F32)]).reshape(1, 2 * LANES)
    wg_hi, wg_lo = _split_bf16(wg2)

    wrt = jnp.concatenate([w_router_grp[0], w_router_exp[0].reshape(d, N_EXPERTS),
                           jnp.zeros((d, LANES - ROUTER_ROWS), F32)], axis=1).T
    brt = jnp.concatenate([b_router_grp[0], b_router_exp[0].reshape(N_EXPERTS),
                           jnp.zeros((LANES - ROUTER_ROWS,), F32)]).reshape(LANES, 1)
    wrt_hi, wrt_lo = _split_bf16(wrt)

    w = dict(
        g_norm1=g_norm1[0].reshape(1, d), wmain=wmain, bmain=bmain, colscale=colscale,
        wg_hi=wg_hi, wg_lo=wg_lo, bg=bg2,
        gn3=gn_gain[0].reshape(N_HEADS, 1, HEAD_DIM),
        wpg=w_pool_grp[0].astype(BF16), pscale=pool_scale[0].reshape(N_GROUPS, 1, LANES),
        wa=w_proj_a[0].astype(BF16), wb=w_proj_b[0].astype(BF16), wo=w_out[0].astype(BF16),
        g_norm2=g_norm2[0].reshape(1, d), wrt_hi=wrt_hi, wrt_lo=wrt_lo, brt=brt,
        wg=w_exp_gate[0].astype(BF16), wu=w_exp_up[0].astype(BF16), wd=w_exp_down[0].astype(BF16),
        g_final=g_final.reshape(1, d),
    )

    y_prompt, (_, cst, nst, mst) = _stream(x_prompt, mod3, lambda r: 0, tp, None, None, w)
    n_rep = jnp.broadcast_to(state_n[..., None], state_C.shape)
    state = (state_C, n_rep, state_m[:, 0].reshape(bs, 2, N_HEADS, 1, 1))
    y_sample, _ = _stream(x_sample, mod3, lambda r: 1 + r // ts, ts, ts // GRID_W, state, w)

    new_c = cst
    new_n = nst.reshape(bp, 1, 2, N_HEADS, HEAD_DIM)
    new_m = mst.reshape(bp, 1, 2, N_HEADS)
    return (y_prompt, y_sample, new_c, new_n, new_m)
```

```python
import functools

import numpy as np
import jax
import jax.numpy as jnp
from jax import lax
from jax.experimental import pallas as pl
from jax.experimental.pallas import tpu as pltpu

F32 = jnp.float32
BF16 = jnp.bfloat16

D_MODEL = 1024
N_HEADS = 8
HEAD_DIM = 128
GRID_W = 64
POOL_WINDOWS = (2, 4, 8, 16)
N_GROUPS = 4
EXPERTS_PER_GROUP = 4
N_EXPERTS = 16
D_EXPERT = 512
EPS = 1e-6
NEG = -1e30

LANES = 128
CHUNK = 256
N_SLABS = 52
SLAB_Q, SLAB_K, SLAB_V, SLAB_O, SLAB_GA, SLAB_GB, SLAB_POOL = 0, 8, 16, 24, 32, 40, 48
D_MAIN = N_SLABS * LANES
VMEM_LIMIT = 56 * 1024 * 1024


def _cparams(sem):
    return pltpu.CompilerParams(dimension_semantics=sem, vmem_limit_bytes=VMEM_LIMIT)


def _const_spec(shape):
    nd = len(shape)
    return pl.BlockSpec(shape, lambda *_: (0,) * nd, pipeline_mode=pl.Buffered(1))


def _split_bf16(x):
    hi = x.astype(BF16)
    lo = (x - hi.astype(F32)).astype(BF16)
    return hi, lo


def _dot(a, b):
    return jnp.dot(a, b, preferred_element_type=F32)


def _dot3(a, b_hi, b_lo):
    a_hi, a_lo = _split_bf16(a)
    return _dot(a_hi, b_hi) + _dot(a_hi, b_lo) + _dot(a_lo, b_hi)


def _rms_scale(x):
    return lax.rsqrt(jnp.mean(x * x, axis=-1, keepdims=True) + EPS)


def _log_sigmoid(x):
    return jnp.minimum(x, 0.0) - jnp.log1p(jnp.exp(-jnp.abs(x)))


def _mod_kernel(c_ref, w_ref, b_ref, o_ref):
    c = c_ref[...]
    s = c * jax.nn.sigmoid(c)
    w_hi, w_lo = _split_bf16(w_ref[...])
    o_ref[...] = _dot3(s, w_hi, w_lo) + b_ref[...]


def _modulation(cvec, w_mod, b_mod):
    tn = 512
    out = pl.pallas_call(
        _mod_kernel,
        out_shape=jax.ShapeDtypeStruct((8, 6 * D_MODEL), F32),
        grid=(6 * D_MODEL // tn,),
        in_specs=[
            pl.BlockSpec((8, D_MODEL), lambda j: (0, 0)),
            pl.BlockSpec((D_MODEL, tn), lambda j: (0, j)),
            pl.BlockSpec((1, tn), lambda j: (0, j)),
        ],
        out_specs=pl.BlockSpec((8, tn), lambda j: (0, j)),
        compiler_params=_cparams(("parallel",)),
        name="modulation",
    )(cvec, w_mod, b_mod.reshape(1, -1))
    return out.reshape(8, 6, D_MODEL)


IN_TM = 512
IN_TN = 512
CMAX_LANE_SHIFT = 32
LOG2E = 1.4426950408889634


def _inproj_kernel(x_ref, mod_ref, g_ref, w_ref, b_ref, cs_ref, wg_hi_ref, wg_lo_ref, bg_ref,
                   p3_ref, cols_ref):
    x = x_ref[...]
    h = x * _rms_scale(x) * g_ref[...]
    h = h * (1.0 + mod_ref[0, 1:2, :]) + mod_ref[0, 0:1, :]
    hb = h.astype(BF16)
    for n in range(D_MAIN // IN_TN):
        sl = slice(n * IN_TN, (n + 1) * IN_TN)
        acc = (_dot(hb, w_ref[:, sl]) + b_ref[:, sl]) * cs_ref[:, sl]
        accb = acc.astype(BF16)
        for j in range(IN_TN // LANES):
            p3_ref[n * (IN_TN // LANES) + j] = accb[:, j * LANES:(j + 1) * LANES]

    g2 = _dot3(h, wg_hi_ref[...], wg_lo_ref[...]) + bg_ref[...]
    ls = _log_sigmoid(g2[:, :LANES])
    gi = g2[:, LANES:]
    row = lax.broadcasted_iota(jnp.int32, (CHUNK, LANES), 0)
    lane = lax.broadcasted_iota(jnp.int32, (CHUNK, LANES), 1)
    for c in range(IN_TM // CHUNK):
        x0 = ls[c * CHUNK:(c + 1) * CHUNK]
        cf = x0
        cb = x0
        s = 1
        while s < CHUNK:
            cf = cf + jnp.where(row >= s, pltpu.roll(cf, s, 0), 0.0)
            cb = cb + jnp.where(row < CHUNK - s, pltpu.roll(cb, CHUNK - s, 0), 0.0)
            s *= 2
        b = jnp.where(lane < 2 * N_HEADS, cf, cb)
        is_imb = (lane & N_HEADS) != 0
        out = jnp.where(is_imb, gi[c * CHUNK:(c + 1) * CHUNK] - b, b)
        mf = out
        mb = out
        s = 1
        while s < CHUNK:
            mf = jnp.maximum(mf, jnp.where(row >= s, pltpu.roll(mf, s, 0), -jnp.inf))
            mb = jnp.maximum(mb, jnp.where(row < CHUNK - s, pltpu.roll(mb, CHUNK - s, 0), -jnp.inf))
            s *= 2
        cmax = pltpu.roll(jnp.where(lane < 2 * N_HEADS, mf, mb), CMAX_LANE_SHIFT, 1)
        cols_ref[c * CHUNK:(c + 1) * CHUNK, :] = jnp.where(lane < CMAX_LANE_SHIFT, out, cmax)


def _in_projection(x2d, mod3, mod_row_fn, g_norm1, wmain, bmain, colscale, wg_hi, wg_lo, bg):
    n = x2d.shape[0]
    return pl.pallas_call(
        _inproj_kernel,
        out_shape=(jax.ShapeDtypeStruct((N_SLABS, n, LANES), BF16),
                   jax.ShapeDtypeStruct((n, LANES), F32)),
        grid=(n // IN_TM,),
        in_specs=[
            pl.BlockSpec((IN_TM, D_MODEL), lambda i: (i, 0)),
            pl.BlockSpec((1, 6, D_MODEL), lambda i: (mod_row_fn(i * IN_TM), 0, 0)),
            _const_spec((1, D_MODEL)),
            _const_spec((D_MODEL, D_MAIN)),
            _const_spec((1, D_MAIN)),
            _const_spec((1, D_MAIN)),
            _const_spec((D_MODEL, 2 * LANES)),
            _const_spec((D_MODEL, 2 * LANES)),
            _const_spec((1, 2 * LANES)),
        ],
        out_specs=(pl.BlockSpec((N_SLABS, IN_TM, LANES), lambda i: (0, i, 0)),
                   pl.BlockSpec((IN_TM, LANES), lambda i: (i, 0))),
        compiler_params=_cparams(("parallel",)),
        name="in_projection",
    )(x2d, mod3, g_norm1, wmain, bmain, colscale, wg_hi, wg_lo, bg)


ML_ROWS = 1024
ML_NCH = ML_ROWS // CHUNK


def _pick_col(blk, lane_iota, idx):
    return jnp.sum(jnp.where(lane_iota == idx, blk, 0.0), axis=1, keepdims=True)


def _mlstm_kernel(*refs, carry):
    if carry:
        (q_ref, k_ref, v_ref, o_ref, cols_ref, rf_ref, rb_ref, gn_ref, c0_ref, n0_ref, m0_ref,
         ya_ref, s_scr, h_scr) = refs
    else:
        (q_ref, k_ref, v_ref, o_ref, cols_ref, rf_ref, rb_ref, gn_ref,
         ya_ref, cst_ref, nst_ref, mst_ref, s_scr, h_scr) = refs
    head = pl.program_id(1)
    lane = lax.broadcasted_iota(jnp.int32, (CHUNK, LANES), 1)
    r_idx = lax.broadcasted_iota(jnp.int32, (CHUNK, CHUNK), 0)
    c_idx = lax.broadcasted_iota(jnp.int32, (CHUNK, CHUNK), 1)
    ones = jnp.ones((CHUNK, HEAD_DIM), BF16)
    tn_dims = (((0,), (0,)), ((), ()))

    for d in range(2):
        mask = (c_idx <= r_idx) if d == 0 else (c_idx >= r_idx)
        row_ref = rf_ref if d == 0 else rb_ref
        last = CHUNK - 1 if d == 0 else 0
        base = head + 2 * N_HEADS * d
        if carry:
            cn = jnp.concatenate([c0_ref[0, 0, d, 0], n0_ref[0, 0, d, 0]], axis=1)
            m = m0_ref[0, d, 0]
        order = range(ML_NCH) if d == 0 else range(ML_NCH - 1, -1, -1)
        for step, c in enumerate(order):
            rows = slice(c * CHUNK, (c + 1) * CHUNK)
            q = q_ref[0, rows, :]
            k = k_ref[0, rows, :]
            v = v_ref[0, rows, :]
            v1 = jnp.concatenate([v, ones], axis=1)
            if d == 0:
                s = lax.dot_general(q, k, (((1,), (1,)), ((), ())), preferred_element_type=F32)
                s_scr[rows, :] = s
            else:
                s = s_scr[rows, :]
            blk = cols_ref[rows, :]
            bcol = _pick_col(blk, lane, base)
            imbcol = _pick_col(blk, lane, base + N_HEADS)
            a = bcol + _pick_col(blk, lane, base + N_HEADS + CMAX_LANE_SHIFT)
            bmirow = row_ref[0, :, rows]
            inter = bcol + (m if carry else NEG)
            m_t = jnp.maximum(inter, a)
            arg = (bcol - m_t) * LOG2E - bmirow * LOG2E
            sw = s * jnp.exp2(jnp.where(mask, arg, NEG))
            nd = _dot(sw.astype(BF16), v1)
            if carry:
                nd = nd + jnp.exp(inter - m_t) * _dot(q, cn.astype(BF16))
            hdir = nd[:, :HEAD_DIM] / jnp.maximum(jnp.abs(nd[:, HEAD_DIM:]), jnp.exp(-m_t))
            if d == 0:
                h_scr[rows, :] = hdir
            else:
                h_scr[rows, :] = h_scr[rows, :] + hdir

            if carry and step == ML_NCH - 1:
                continue
            m_new = m_t[last:last + 1, :]
            b_last = bcol[last:last + 1, :]
            w_col = jnp.exp(b_last - m_new + imbcol)
            if carry:
                u = lax.dot_general(k, (w_col * v1.astype(F32)).astype(BF16), tn_dims,
                                    preferred_element_type=F32)
                cn = jnp.exp(b_last + m - m_new) * cn + u
                m = m_new
            else:
                u = lax.dot_general(k, (w_col * v.astype(F32)).astype(BF16), tn_dims,
                                    preferred_element_type=F32)
                w_row = jnp.exp(b_last - m_new - bmirow)
                w8 = jnp.broadcast_to(w_row, (8, CHUNK)).astype(BF16)
                cst_ref[c, 0, d, 0] = u
                nst_ref[c, d, 0] = _dot(w8, k)[0:1, :]
                mst_ref[c, d, 0] = m_new

    hh = h_scr[...]
    hn = hh * _rms_scale(hh) * gn_ref[0]
    ya_ref[0] = (hn * jax.nn.sigmoid(o_ref[0].astype(F32))).astype(BF16)


def _mlstm(p3, cols, rows_f, rows_b, gn3, state=None):
    n = p3.shape[1]
    nblk = n // ML_ROWS
    carry = state is not None

    def slab(base):
        return pl.BlockSpec((1, ML_ROWS, LANES), lambda g, h: (base + h, g, 0))

    in_specs = [
        slab(SLAB_Q), slab(SLAB_K), slab(SLAB_V), slab(SLAB_O),
        pl.BlockSpec((ML_ROWS, LANES), lambda g, h: (g, 0)),
        pl.BlockSpec((1, 1, ML_ROWS), lambda g, h: (h, 0, g)),
        pl.BlockSpec((1, 1, ML_ROWS), lambda g, h: (h, 0, g)),
        pl.BlockSpec((1, 1, LANES), lambda g, h: (h, 0, 0)),
    ]
    args = [p3, p3, p3, p3, cols, rows_f, rows_b, gn3]
    ya_shape = jax.ShapeDtypeStruct((N_HEADS, n, LANES), BF16)
    ya_spec = pl.BlockSpec((1, ML_ROWS, LANES), lambda g, h: (h, g, 0))
    if carry:
        c0, n0, m0 = state
        in_specs += [
            pl.BlockSpec((1, 1, 2, 1, HEAD_DIM, HEAD_DIM), lambda g, h: (g, 0, 0, h, 0, 0)),
            pl.BlockSpec((1, 1, 2, 1, HEAD_DIM, HEAD_DIM), lambda g, h: (g, 0, 0, h, 0, 0)),
            pl.BlockSpec((1, 2, 1, 1, 1), lambda g, h: (g, 0, h, 0, 0)),
        ]
        args += [c0, n0, m0]
        out_shape = ya_shape
        out_specs = ya_spec
    else:
        nseq = n // CHUNK
        out_shape = (ya_shape,
                     jax.ShapeDtypeStruct((nseq, 1, 2, N_HEADS, HEAD_DIM, HEAD_DIM), F32),
                     jax.ShapeDtypeStruct((nseq, 2, N_HEADS, 1, HEAD_DIM), F32),
                     jax.ShapeDtypeStruct((nseq, 2, N_HEADS, 1, 1), F32))
        out_specs = (ya_spec,
                     pl.BlockSpec((ML_NCH, 1, 2, 1, HEAD_DIM, HEAD_DIM), lambda g, h: (g, 0, 0, h, 0, 0)),
                     pl.BlockSpec((ML_NCH, 2, 1, 1, HEAD_DIM), lambda g, h: (g, 0, h, 0, 0)),
                     pl.BlockSpec((ML_NCH, 2, 1, 1, 1), lambda g, h: (g, 0, h, 0, 0)))
    return pl.pallas_call(
        functools.partial(_mlstm_kernel, carry=carry),
        out_shape=out_shape,
        grid=(nblk, N_HEADS),
        in_specs=in_specs,
        out_specs=out_specs,
        scratch_shapes=[pltpu.VMEM((ML_ROWS, CHUNK), F32), pltpu.VMEM((ML_ROWS, HEAD_DIM), F32)],
        compiler_params=_cparams(("parallel", "parallel")),
        name="mlstm_latent" if carry else "mlstm_prompt",
    )(*args)


def _pool_kernel(u_ref, a_ref, ic_ref, wpg_ref, ps_ref, yb_ref, *, seq_len, n_seq):
    for sq in range(n_seq):
        rows = slice(sq * seq_len, (sq + 1) * seq_len)
        for g in range(N_GROUPS):
            u = u_ref[g, rows, :]
            pooled = _dot(a_ref[g], u) * ic_ref[g]
            p = (pooled - u.astype(F32)).astype(BF16)
            yb_ref[g, rows, :] = (_dot(p, wpg_ref[g]) * ps_ref[g]).astype(BF16)


def _pool(p3, amat, invc, wpg, pscale, seq_len, n_seq):
    n = p3.shape[1]
    rows = seq_len * n_seq
    return pl.pallas_call(
        functools.partial(_pool_kernel, seq_len=seq_len, n_seq=n_seq),
        out_shape=jax.ShapeDtypeStruct((N_GROUPS, n, LANES), BF16),
        grid=(n // rows,),
        in_specs=[
            pl.BlockSpec((N_GROUPS, rows, LANES), lambda i: (SLAB_POOL // N_GROUPS, i, 0)),
            _const_spec((N_GROUPS, seq_len, seq_len)),
            _const_spec((N_GROUPS, seq_len, LANES)),
            _const_spec((N_GROUPS, LANES, LANES)),
            _const_spec((N_GROUPS, 1, LANES)),
        ],
        out_specs=pl.BlockSpec((N_GROUPS, rows, LANES), lambda i: (0, i, 0)),
        compiler_params=_cparams(("parallel",)),
        name="pool",
    )(p3, amat, invc, wpg, pscale)


def _window_matrix(size, w):
    idx = np.arange(size)
    lo = np.clip(idx - w // 2, 0, size)
    hi = np.clip(idx - w // 2 + w, 0, size)
    s = np.arange(size)[None, :]
    a = ((s >= lo[:, None]) & (s < hi[:, None])).astype(np.float32)
    return a, (hi - lo).astype(np.float32)


def _pool_constants(seq_len, rows_2d):
    mats, invs = [], []
    for w in POOL_WINDOWS:
        if rows_2d is None:
            a, cnt = _window_matrix(seq_len, w)
        else:
            ar, cr = _window_matrix(rows_2d, w)
            ac, cc = _window_matrix(GRID_W, w)
            a = np.kron(ar, ac)
            cnt = np.kron(cr, cc)
        mats.append(a)
        invs.append(np.repeat((1.0 / cnt)[:, None], LANES, axis=1))
    return (jnp.asarray(np.stack(mats), dtype=BF16), jnp.asarray(np.stack(invs), dtype=F32))


MIX_TM = 512


def _mix_kernel(x_ref, mod_ref, ya_ref, yb_ref, ga_ref, gb_ref, wa_ref, wb_ref, wo_ref, x1_ref):
    ya = jnp.concatenate([ya_ref[i] for i in range(N_HEADS)], axis=1)
    yb = jnp.concatenate([yb_ref[i] for i in range(N_GROUPS)], axis=1)
    ga = jnp.concatenate([ga_ref[i] for i in range(8)], axis=1).astype(F32)
    gb = jnp.concatenate([gb_ref[i] for i in range(8)], axis=1).astype(F32)
    merged = jax.nn.sigmoid(ga) * _dot(ya, wa_ref[...]) + jax.nn.sigmoid(gb) * _dot(yb, wb_ref[...])
    out = _dot(merged.astype(BF16), wo_ref[...])
    x1_ref[...] = x_ref[...] + mod_ref[0, 2:3, :] * out


def _mix(x2d, mod3, mod_row_fn, ya, yb, p3, wa, wb, wo):
    n = x2d.shape[0]
    return pl.pallas_call(
        _mix_kernel,
        out_shape=jax.ShapeDtypeStruct((n, D_MODEL), F32),
        grid=(n // MIX_TM,),
        in_specs=[
            pl.BlockSpec((MIX_TM, D_MODEL), lambda i: (i, 0)),
            pl.BlockSpec((1, 6, D_MODEL), lambda i: (mod_row_fn(i * MIX_TM), 0, 0)),
            pl.BlockSpec((N_HEADS, MIX_TM, LANES), lambda i: (0, i, 0)),
            pl.BlockSpec((N_GROUPS, MIX_TM, LANES), lambda i: (0, i, 0)),
            pl.BlockSpec((8, MIX_TM, LANES), lambda i: (SLAB_GA // 8, i, 0)),
            pl.BlockSpec((8, MIX_TM, LANES), lambda i: (SLAB_GB // 8, i, 0)),
            _const_spec((D_MODEL, D_MODEL)),
            _const_spec((N_GROUPS * LANES, D_MODEL)),
            _const_spec((D_MODEL, D_MODEL)),
        ],
        out_specs=pl.BlockSpec((MIX_TM, D_MODEL), lambda i: (i, 0)),
        compiler_params=_cparams(("parallel",)),
        name="mix",
    )(x2d, mod3, ya, yb, p3, p3, wa, wb, wo)


MOE_TB = 1024
MOE_TILE = 160
MOE_NSUB = 2
MOE_HALF = 8
ROUTER_ROWS = 4 + N_EXPERTS


def _nt_dot(a, b):
    return lax.dot_general(a, b, (((1,), (1,)), ((), ())), preferred_element_type=F32)


def _route_rows(lt):
    lg = [lt[i:i + 1] for i in range(N_GROUPS)]
    mx = jnp.maximum(jnp.maximum(lg[0], lg[1]), jnp.maximum(lg[2], lg[3]))
    p_top = 1.0 / (jnp.exp(lg[0] - mx) + jnp.exp(lg[1] - mx) + jnp.exp(lg[2] - mx) + jnp.exp(lg[3] - mx))
    gidx = jnp.zeros(lg[0].shape, jnp.int32)
    best = lg[0]
    for i in range(1, N_GROUPS):
        upd = lg[i] > best
        gidx = jnp.where(upd, i, gidx)
        best = jnp.where(upd, lg[i], best)

    def expert_row(g, e):
        r = N_GROUPS + g * EXPERTS_PER_GROUP + e
        return lt[r:r + 1]

    le = [jnp.where(gidx == 0, expert_row(0, e),
                    jnp.where(gidx == 1, expert_row(1, e),
                              jnp.where(gidx == 2, expert_row(2, e), expert_row(3, e))))
          for e in range(EXPERTS_PER_GROUP)]
    v1 = jnp.maximum(jnp.maximum(le[0], le[1]), jnp.maximum(le[2], le[3]))
    i1 = jnp.full(v1.shape, EXPERTS_PER_GROUP - 1, jnp.int32)
    for e in range(EXPERTS_PER_GROUP - 2, -1, -1):
        i1 = jnp.where(le[e] == v1, e, i1)
    le2 = [jnp.where(i1 == e, -jnp.inf, le[e]) for e in range(EXPERTS_PER_GROUP)]
    v2 = jnp.maximum(jnp.maximum(le2[0], le2[1]), jnp.maximum(le2[2], le2[3]))
    i2 = jnp.full(v2.shape, EXPERTS_PER_GROUP - 1, jnp.int32)
    for e in range(EXPERTS_PER_GROUP - 2, -1, -1):
        i2 = jnp.where(le2[e] == v2, e, i2)
    e2 = jnp.exp(v2 - v1)
    w1 = p_top / (1.0 + e2)
    w2 = p_top * e2 / (1.0 + e2)
    w_rows, sel_rows = [], []
    for g in range(N_GROUPS):
        for e in range(EXPERTS_PER_GROUP):
            first = jnp.where(gidx == g, jnp.where(i1 == e, 1.0, 0.0), 0.0)
            second = jnp.where(gidx == g, jnp.where(i2 == e, 1.0, 0.0), 0.0)
            w_rows.append(first * w1 + second * w2)
            sel_rows.append(first + second)
    return w_rows, sel_rows


def _moe_kernel(x1_ref, mod_a_ref, mod_b_ref, g2_ref, wrt_hi_ref, wrt_lo_ref, brt_ref, before_ref,
                wg_ref, wu_ref, wd_ref, gf_ref, y_ref, h2_scr, rank_scr, w_scr, p_scr, y_scr):
    e = pl.program_id(1)
    mod_refs = (mod_a_ref, mod_b_ref)
    tn_dims = (((0,), (0,)), ((), ()))
    slot = lax.broadcasted_iota(jnp.int32, (MOE_TILE, MOE_TB), 0).astype(F32)

    @pl.when(e == 0)
    def _():
        for sb in range(MOE_NSUB):
            x1 = x1_ref[sb * MOE_TB:(sb + 1) * MOE_TB, :]
            mod_ref = mod_refs[sb]
            h2 = x1 * _rms_scale(x1) * g2_ref[...]
            h2 = h2 * (1.0 + mod_ref[0, 4:5, :]) + mod_ref[0, 3:4, :]
            h_hi, h_lo = _split_bf16(h2)
            h2_scr[sb] = h_hi
            wr_hi = wrt_hi_ref[...]
            lt = _nt_dot(wr_hi, h_hi) + _nt_dot(wr_hi, h_lo) + _nt_dot(wrt_lo_ref[...], h_hi) + brt_ref[...]
            w_rows, sel_rows = _route_rows(lt)
            sel = jnp.concatenate(sel_rows, axis=0)
            rank = _dot(sel.astype(BF16), before_ref[...])
            rank = jnp.where(sel > 0.0, rank, -1.0)
            for i in range(N_EXPERTS):
                rank_scr[sb * N_EXPERTS + i] = rank[i:i + 1]
                w_scr[sb * N_EXPERTS + i] = w_rows[i]
        y_ref[...] = jnp.zeros_like(y_ref)

    half = lax.rem(e, MOE_HALF)
    for sb in range(MOE_NSUB):
        rank_row = rank_scr[sb * N_EXPERTS + e]
        w_row = w_scr[sb * N_EXPERTS + e]
        sub_rows = slice(sb * MOE_TB, (sb + 1) * MOE_TB)

        def expert_rows(first_slot, rank_row=rank_row, w_row=w_row, sb=sb):
            hit = (slot + first_slot) == rank_row
            pb = jnp.where(hit, 1.0, 0.0).astype(BF16)
            w_slot = jnp.sum(jnp.where(hit, w_row, 0.0), axis=1, keepdims=True)
            xg = _dot(pb, h2_scr[sb]).astype(BF16)
            a = _dot(xg, wg_ref[0])
            u = _dot(xg, wu_ref[0])
            act = (a * jax.nn.sigmoid(a) * u * w_slot).astype(BF16)
            return pb, _dot(act, wd_ref[0]).astype(BF16)

        _, yv = expert_rows(0.0)
        y_scr[sb, pl.ds(pl.multiple_of(half * MOE_TILE, MOE_TILE), MOE_TILE), :] = yv

        n_routed = jnp.max(rank_row).astype(jnp.int32) + 1
        n_pass = jnp.int32(0)
        for kk in range(-(-MOE_TB // MOE_TILE)):
            n_pass = n_pass + (n_routed > kk * MOE_TILE).astype(jnp.int32)

        def extra_pass(j, carry, expert_rows=expert_rows, sub_rows=sub_rows):
            pb, yv = expert_rows((j * MOE_TILE).astype(F32))
            y_ref[sub_rows, :] += lax.dot_general(pb, yv, tn_dims, preferred_element_type=F32)
            return carry

        lax.fori_loop(1, n_pass, extra_pass, 0)

    @pl.when(half == MOE_HALF - 1)
    def _():
        for sb in range(MOE_NSUB):
            for i in range(MOE_HALF):
                rank_row = rank_scr[sb * N_EXPERTS + (e - (MOE_HALF - 1)) + i]
                p_scr[i * MOE_TILE:(i + 1) * MOE_TILE, :] = jnp.where(slot == rank_row, 1.0, 0.0).astype(BF16)
            sub_rows = slice(sb * MOE_TB, (sb + 1) * MOE_TB)
            y_ref[sub_rows, :] += lax.dot_general(p_scr[...], y_scr[sb], tn_dims, preferred_element_type=F32)

    @pl.when(e == N_EXPERTS - 1)
    def _():
        for sb in range(MOE_NSUB):
            sub_rows = slice(sb * MOE_TB, (sb + 1) * MOE_TB)
            x2 = x1_ref[sub_rows, :] + mod_refs[sb][0, 5:6, :] * y_ref[sub_rows, :]
            y_ref[sub_rows, :] = x2 * _rms_scale(x2) * gf_ref[...]


def _moe(x1, mod3, mod_row_fn, g_norm2, wrt_hi, wrt_lo, brt, wg, wu, wd, g_final):
    n = x1.shape[0]
    rows = MOE_NSUB * MOE_TB
    t_i = lax.broadcasted_iota(jnp.int32, (MOE_TB, MOE_TB), 0)
    t_j = lax.broadcasted_iota(jnp.int32, (MOE_TB, MOE_TB), 1)
    before = (t_i < t_j).astype(BF16)
    return pl.pallas_call(
        _moe_kernel,
        out_shape=jax.ShapeDtypeStruct((n, D_MODEL), F32),
        grid=(n // rows, N_EXPERTS),
        in_specs=[
            pl.BlockSpec((rows, D_MODEL), lambda i, e: (i, 0), pipeline_mode=pl.Buffered(1)),
            pl.BlockSpec((1, 6, D_MODEL), lambda i, e: (mod_row_fn(i * rows), 0, 0)),
            pl.BlockSpec((1, 6, D_MODEL), lambda i, e: (mod_row_fn(i * rows + MOE_TB), 0, 0)),
            _const_spec((1, D_MODEL)),
            _const_spec((LANES, D_MODEL)),
            _const_spec((LANES, D_MODEL)),
            _const_spec((LANES, 1)),
            _const_spec((MOE_TB, MOE_TB)),
            pl.BlockSpec((1, D_MODEL, D_EXPERT), lambda i, e: (e, 0, 0)),
            pl.BlockSpec((1, D_MODEL, D_EXPERT), lambda i, e: (e, 0, 0)),
            pl.BlockSpec((1, D_EXPERT, D_MODEL), lambda i, e: (e, 0, 0)),
            _const_spec((1, D_MODEL)),
        ],
        out_specs=pl.BlockSpec((rows, D_MODEL), lambda i, e: (i, 0)),
        scratch_shapes=[pltpu.VMEM((MOE_NSUB, MOE_TB, D_MODEL), BF16),
                        pltpu.VMEM((MOE_NSUB * N_EXPERTS, 1, MOE_TB), F32),
                        pltpu.VMEM((MOE_NSUB * N_EXPERTS, 1, MOE_TB), F32),
                        pltpu.VMEM((MOE_HALF * MOE_TILE, MOE_TB), BF16),
                        pltpu.VMEM((MOE_NSUB, MOE_HALF * MOE_TILE, D_MODEL), BF16)],
        compiler_params=_cparams(("parallel", "arbitrary")),
        name="moe",
    )(x1, mod3, mod3, g_norm2, wrt_hi, wrt_lo, brt, before, wg, wu, wd, g_final)


def _stream(x, mod3, mod_row_fn, seq_len, rows_2d, state, w):
    bsz, t, _ = x.shape
    n = bsz * t
    x2d = x.reshape(n, D_MODEL)
    p3, cols = _in_projection(x2d, mod3, mod_row_fn, w["g_norm1"], w["wmain"], w["bmain"], w["colscale"],
                              w["wg_hi"], w["wg_lo"], w["bg"])
    rows_f = (-cols[:, N_HEADS:2 * N_HEADS]).T.reshape(N_HEADS, 1, n)
    rows_b = (-cols[:, 3 * N_HEADS:4 * N_HEADS]).T.reshape(N_HEADS, 1, n)
    ml = _mlstm(p3, cols, rows_f, rows_b, w["gn3"], state)
    amat, invc = _pool_constants(seq_len, rows_2d)
    n_seq = max(1, 1024 // seq_len)
    yb = _pool(p3, amat, invc, w["wpg"], w["pscale"], seq_len, n_seq)
    ya = ml if state is not None else ml[0]
    x1 = _mix(x2d, mod3, mod_row_fn, ya, yb, p3, w["wa"], w["wb"], w["wo"])
    y = _moe(x1, mod3, mod_row_fn, w["g_norm2"], w["wrt_hi"], w["wrt_lo"], w["brt"],
             w["wg"], w["wu"], w["wd"], w["g_final"])
    return y.reshape(bsz, t, D_MODEL), ml


def kernel(x_prompt, x_sample, state_C, state_n, state_m, c, c_ctx, w_mod, b_mod, g_norm1, w_in, b_in, gn_gain, w_pool_grp, pool_scale, w_proj_a, w_proj_b, w_out, g_norm2, w_router_grp, b_router_grp, w_router_exp, b_router_exp, w_exp_gate, w_exp_up, w_exp_down, g_final):
    bp, tp, _ = x_prompt.shape
    bs, ts, _ = x_sample.shape
    assert w_mod.shape[0] == 1, "single trunk layer"
    assert tp == CHUNK and ts % CHUNK == 0 and ts == ML_ROWS

    cvec = jnp.zeros((8, D_MODEL), F32).at[0].set(c_ctx).at[1:1 + bs].set(c)
    mod3 = _modulation(cvec, w_mod[0], b_mod[0])

    wi, bi = w_in[0], b_in[0]
    d = D_MODEL
    o_q, o_k, o_v, o_o, o_g, o_p, o_ga, o_gb = 0, d, 2 * d, 3 * d, 4 * d, 4 * d + 32, 4 * d + 32 + 512, 4 * d + 32 + 512 + d
    order = [(o_q, d), (o_k, d), (o_v, d), (o_o, d), (o_ga, d), (o_gb, d), (o_p, 512)]
    wmain = jnp.concatenate([wi[:, s:s + l] for s, l in order], axis=1).astype(BF16)
    bmain = jnp.concatenate([bi[s:s + l] for s, l in order]).reshape(1, D_MAIN)
    colscale = jnp.ones((D_MAIN,), F32).at[SLAB_K * LANES:(SLAB_K + N_HEADS) * LANES].set(HEAD_DIM ** -0.5)
    colscale = colscale.reshape(1, D_MAIN)
    gw = wi[:, o_g:o_g + 32]
    gb = bi[o_g:o_g + 32]
    i_f, f_f, i_b, f_b = (slice(0, 8), slice(8, 16), slice(16, 24), slice(24, 32))
    zw = jnp.zeros((d, 8), F32)
    zb = jnp.zeros((8,), F32)
    wg2 = jnp.concatenate([gw[:, f_f], gw[:, f_f], gw[:, f_b], gw[:, f_b], jnp.zeros((d, LANES - 32), F32),
                           zw, gw[:, i_f], zw, gw[:, i_b], jnp.zeros((d, LANES - 32), F32)], axis=1)
    bg2 = jnp.concatenate([gb[f_f], gb[f_f], gb[f_b], gb[f_b], jnp.zeros((LANES - 32,), F32),
                           zb, gb[i_f], zb, gb[i_b], jnp.zeros((LANES - 32,), F32)]).reshape(1, 2 * LANES)
    wg_hi, wg_lo = _split_bf16(wg2)

    wrt = jnp.concatenate([w_router_grp[0], w_router_exp[0].reshape(d, N_EXPERTS),
                           jnp.zeros((d, LANES - ROUTER_ROWS), F32)], axis=1).T
    brt = jnp.concatenate([b_router_grp[0], b_router_exp[0].reshape(N_EXPERTS),
                           jnp.zeros((LANES - ROUTER_ROWS,), F32)]).reshape(LANES, 1)
    wrt_hi, wrt_lo = _split_bf16(wrt)

    w = dict(
        g_norm1=g_norm1[0].reshape(1, d), wmain=wmain, bmain=bmain, colscale=colscale,
        wg_hi=wg_hi, wg_lo=wg_lo, bg=bg2,
        gn3=gn_gain[0].reshape(N_HEADS, 1, HEAD_DIM),
        wpg=w_pool_grp[0].astype(BF16), pscale=pool_scale[0].reshape(N_GROUPS, 1, LANES),
        wa=w_proj_a[0].astype(BF16), wb=w_proj_b[0].astype(BF16), wo=w_out[0].astype(BF16),
        g_norm2=g_norm2[0].reshape(1, d), wrt_hi=wrt_hi, wrt_lo=wrt_lo, brt=brt,
        wg=w_exp_gate[0].astype(BF16), wu=w_exp_up[0].astype(BF16), wd=w_exp_down[0].astype(BF16),
        g_final=g_final.reshape(1, d),
    )

    y_prompt, (_, cst, nst, mst) = _stream(x_prompt, mod3, lambda r: 0, tp, None, None, w)
    n_rep = jnp.broadcast_to(state_n[..., None], state_C.shape)
    state = (state_C, n_rep, state_m[:, 0].reshape(bs, 2, N_HEADS, 1, 1))
    y_sample, _ = _stream(x_sample, mod3, lambda r: 1 + r // ts, ts, ts // GRID_W, state, w)

    new_c = cst
    new_n = nst.reshape(bp, 1, 2, N_HEADS, HEAD_DIM)
    new_m = mst.reshape(bp, 1, 2, N_HEADS)
    return (y_prompt, y_sample, new_c, new_n, new_m)
```

```python
import functools

import numpy as np
import jax
import jax.numpy as jnp
from jax import lax
from jax.experimental import pallas as pl
from jax.experimental.pallas import tpu as pltpu

F32 = jnp.float32
BF16 = jnp.bfloat16

D_MODEL = 1024
N_HEADS = 8
HEAD_DIM = 128
GRID_W = 64
POOL_WINDOWS = (2, 4, 8, 16)
N_GROUPS = 4
EXPERTS_PER_GROUP = 4
N_EXPERTS = 16
D_EXPERT = 512
EPS = 1e-6
NEG = -1e30

LANES = 128
CHUNK = 256
N_SLABS = 52
SLAB_Q, SLAB_K, SLAB_V, SLAB_O, SLAB_GA, SLAB_GB, SLAB_POOL = 0, 8, 16, 24, 32, 40, 48
D_MAIN = N_SLABS * LANES
VMEM_LIMIT = 56 * 1024 * 1024


def _cparams(sem):
    return pltpu.CompilerParams(dimension_semantics=sem, vmem_limit_bytes=VMEM_LIMIT)


def _const_spec(shape):
    nd = len(shape)
    return pl.BlockSpec(shape, lambda *_: (0,) * nd, pipeline_mode=pl.Buffered(1))


def _split_bf16(x):
    hi = x.astype(BF16)
    lo = (x - hi.astype(F32)).astype(BF16)
    return hi, lo


def _dot(a, b):
    return jnp.dot(a, b, preferred_element_type=F32)


def _dot3(a, b_hi, b_lo):
    a_hi, a_lo = _split_bf16(a)
    return _dot(a_hi, b_hi) + _dot(a_hi, b_lo) + _dot(a_lo, b_hi)


def _rms_scale(x):
    return lax.rsqrt(jnp.mean(x * x, axis=-1, keepdims=True) + EPS)


def _log_sigmoid(x):
    return jnp.minimum(x, 0.0) - jnp.log1p(jnp.exp(-jnp.abs(x)))


def _mod_kernel(c_ref, w_ref, b_ref, o_ref):
    c = c_ref[...]
    s = c * jax.nn.sigmoid(c)
    w_hi, w_lo = _split_bf16(w_ref[...])
    o_ref[...] = _dot3(s, w_hi, w_lo) + b_ref[...]


def _modulation(cvec, w_mod, b_mod):
    tn = 512
    out = pl.pallas_call(
        _mod_kernel,
        out_shape=jax.ShapeDtypeStruct((8, 6 * D_MODEL), F32),
        grid=(6 * D_MODEL // tn,),
        in_specs=[
            pl.BlockSpec((8, D_MODEL), lambda j: (0, 0)),
            pl.BlockSpec((D_MODEL, tn), lambda j: (0, j)),
            pl.BlockSpec((1, tn), lambda j: (0, j)),
        ],
        out_specs=pl.BlockSpec((8, tn), lambda j: (0, j)),
        compiler_params=_cparams(("parallel",)),
        name="modulation",
    )(cvec, w_mod, b_mod.reshape(1, -1))
    return out.reshape(8, 6, D_MODEL)


IN_TM = 512
IN_TN = 512
CMAX_LANE_SHIFT = 32
LOG2E = 1.4426950408889634


def _inproj_kernel(x_ref, mod_ref, g_ref, w_ref, b_ref, cs_ref, wg_hi_ref, wg_lo_ref, bg_ref,
                   p3_ref, cols_ref):
    x = x_ref[...]
    h = x * _rms_scale(x) * g_ref[...]
    h = h * (1.0 + mod_ref[0, 1:2, :]) + mod_ref[0, 0:1, :]
    hb = h.astype(BF16)
    for n in range(D_MAIN // IN_TN):
        sl = slice(n * IN_TN, (n + 1) * IN_TN)
        acc = (_dot(hb, w_ref[:, sl]) + b_ref[:, sl]) * cs_ref[:, sl]
        accb = acc.astype(BF16)
        for j in range(IN_TN // LANES):
            p3_ref[n * (IN_TN // LANES) + j] = accb[:, j * LANES:(j + 1) * LANES]

    g2 = _dot3(h, wg_hi_ref[...], wg_lo_ref[...]) + bg_ref[...]
    ls = _log_sigmoid(g2[:, :LANES])
    gi = g2[:, LANES:]
    row = lax.broadcasted_iota(jnp.int32, (CHUNK, LANES), 0)
    lane = lax.broadcasted_iota(jnp.int32, (CHUNK, LANES), 1)
    for c in range(IN_TM // CHUNK):
        x0 = ls[c * CHUNK:(c + 1) * CHUNK]
        cf = x0
        cb = x0
        s = 1
        while s < CHUNK:
            cf = cf + jnp.where(row >= s, pltpu.roll(cf, s, 0), 0.0)
            cb = cb + jnp.where(row < CHUNK - s, pltpu.roll(cb, CHUNK - s, 0), 0.0)
            s *= 2
        b = jnp.where(lane < 2 * N_HEADS, cf, cb)
        is_imb = (lane & N_HEADS) != 0
        out = jnp.where(is_imb, gi[c * CHUNK:(c + 1) * CHUNK] - b, b)
        mf = out
        mb = out
        s = 1
        while s < CHUNK:
            mf = jnp.maximum(mf, jnp.where(row >= s, pltpu.roll(mf, s, 0), -jnp.inf))
            mb = jnp.maximum(mb, jnp.where(row < CHUNK - s, pltpu.roll(mb, CHUNK - s, 0), -jnp.inf))
            s *= 2
        cmax = pltpu.roll(jnp.where(lane < 2 * N_HEADS, mf, mb), CMAX_LANE_SHIFT, 1)
        cols_ref[c * CHUNK:(c + 1) * CHUNK, :] = jnp.where(lane < CMAX_LANE_SHIFT, out, cmax)


def _in_projection(x2d, mod3, mod_row_fn, g_norm1, wmain, bmain, colscale, wg_hi, wg_lo, bg):
    n = x2d.shape[0]
    return pl.pallas_call(
        _inproj_kernel,
        out_shape=(jax.ShapeDtypeStruct((N_SLABS, n, LANES), BF16),
                   jax.ShapeDtypeStruct((n, LANES), F32)),
        grid=(n // IN_TM,),
        in_specs=[
            pl.BlockSpec((IN_TM, D_MODEL), lambda i: (i, 0)),
            pl.BlockSpec((1, 6, D_MODEL), lambda i: (mod_row_fn(i * IN_TM), 0, 0)),
            _const_spec((1, D_MODEL)),
            _const_spec((D_MODEL, D_MAIN)),
            _const_spec((1, D_MAIN)),
            _const_spec((1, D_MAIN)),
            _const_spec((D_MODEL, 2 * LANES)),
            _const_spec((D_MODEL, 2 * LANES)),
            _const_spec((1, 2 * LANES)),
        ],
        out_specs=(pl.BlockSpec((N_SLABS, IN_TM, LANES), lambda i: (0, i, 0)),
                   pl.BlockSpec((IN_TM, LANES), lambda i: (i, 0))),
        compiler_params=_cparams(("parallel",)),
        name="in_projection",
    )(x2d, mod3, g_norm1, wmain, bmain, colscale, wg_hi, wg_lo, bg)


ML_ROWS = 1024
ML_NCH = ML_ROWS // CHUNK


def _pick_col(blk, lane_iota, idx):
    return jnp.sum(jnp.where(lane_iota == idx, blk, 0.0), axis=1, keepdims=True)


def _mlstm_kernel(*refs, carry):
    if carry:
        (q_ref, k_ref, v_ref, o_ref, cols_ref, rf_ref, rb_ref, gn_ref, c0_ref, n0_ref, m0_ref,
         ya_ref, s_scr, h_scr) = refs
    else:
        (q_ref, k_ref, v_ref, o_ref, cols_ref, rf_ref, rb_ref, gn_ref,
         ya_ref, cst_ref, nst_ref, mst_ref, s_scr, h_scr) = refs
    head = pl.program_id(1)
    lane = lax.broadcasted_iota(jnp.int32, (CHUNK, LANES), 1)
    r_idx = lax.broadcasted_iota(jnp.int32, (CHUNK, CHUNK), 0)
    c_idx = lax.broadcasted_iota(jnp.int32, (CHUNK, CHUNK), 1)
    ones = jnp.ones((CHUNK, HEAD_DIM), BF16)
    tn_dims = (((0,), (0,)), ((), ()))

    for d in range(2):
        mask = (c_idx <= r_idx) if d == 0 else (c_idx >= r_idx)
        row_ref = rf_ref if d == 0 else rb_ref
        last = CHUNK - 1 if d == 0 else 0
        base = head + 2 * N_HEADS * d
        if carry:
            cn = jnp.concatenate([c0_ref[0, 0, d, 0], n0_ref[0, 0, d, 0]], axis=1)
            m = m0_ref[0, d, 0]
        order = range(ML_NCH) if d == 0 else range(ML_NCH - 1, -1, -1)
        for step, c in enumerate(order):
            rows = slice(c * CHUNK, (c + 1) * CHUNK)
            q = q_ref[0, rows, :]
            k = k_ref[0, rows, :]
            v = v_ref[0, rows, :]
            v1 = jnp.concatenate([v, ones], axis=1)
            if d == 0:
                s = lax.dot_general(q, k, (((1,), (1,)), ((), ())), preferred_element_type=F32)
                s_scr[rows, :] = s
            else:
                s = s_scr[rows, :]
            blk = cols_ref[rows, :]
            bcol = _pick_col(blk, lane, base)
            imbcol = _pick_col(blk, lane, base + N_HEADS)
            a = bcol + _pick_col(blk, lane, base + N_HEADS + CMAX_LANE_SHIFT)
            bmirow = row_ref[0, :, rows]
            inter = bcol + (m if carry else NEG)
            m_t = jnp.maximum(inter, a)
            arg = (bcol - m_t) * LOG2E - bmirow * LOG2E
            sw = s * jnp.exp2(jnp.where(mask, arg, NEG))
            nd = _dot(sw.astype(BF16), v1)
            if carry:
                nd = nd + jnp.exp(inter - m_t) * _dot(q, cn.astype(BF16))
            hdir = nd[:, :HEAD_DIM] / jnp.maximum(jnp.abs(nd[:, HEAD_DIM:]), jnp.exp(-m_t))
            if d == 0:
                h_scr[rows, :] = hdir
            else:
                h_scr[rows, :] = h_scr[rows, :] + hdir

            if carry and step == ML_NCH - 1:
                continue
            m_new = m_t[last:last + 1, :]
            b_last = bcol[last:last + 1, :]
            w_col = jnp.exp(b_last - m_new + imbcol)
            if carry:
                u = lax.dot_general(k, (w_col * v1.astype(F32)).astype(BF16), tn_dims,
                                    preferred_element_type=F32)
                cn = jnp.exp(b_last + m - m_new) * cn + u
                m = m_new
            else:
                u = lax.dot_general(k, (w_col * v.astype(F32)).astype(BF16), tn_dims,
                                    preferred_element_type=F32)
                w_row = jnp.exp(b_last - m_new - bmirow)
                w8 = jnp.broadcast_to(w_row, (8, CHUNK)).astype(BF16)
                cst_ref[c, 0, d, 0] = u
                nst_ref[c, d, 0] = _dot(w8, k)[0:1, :]
                mst_ref[c, d, 0] = m_new

    hh = h_scr[...]
    hn = hh * _rms_scale(hh) * gn_ref[0]
    ya_ref[0] = (hn * jax.nn.sigmoid(o_ref[0].astype(F32))).astype(BF16)


def _mlstm(p3, cols, rows_f, rows_b, gn3, state=None):
    n = p3.shape[1]
    nblk = n // ML_ROWS
    carry = state is not None

    def slab(base):
        return pl.BlockSpec((1, ML_ROWS, LANES), lambda g, h: (base + h, g, 0))

    in_specs = [
        slab(SLAB_Q), slab(SLAB_K), slab(SLAB_V), slab(SLAB_O),
        pl.BlockSpec((ML_ROWS, LANES), lambda g, h: (g, 0)),
        pl.BlockSpec((1, 1, ML_ROWS), lambda g, h: (h, 0, g)),
        pl.BlockSpec((1, 1, ML_ROWS), lambda g, h: (h, 0, g)),
        pl.BlockSpec((1, 1, LANES), lambda g, h: (h, 0, 0)),
    ]
    args = [p3, p3, p3, p3, cols, rows_f, rows_b, gn3]
    ya_shape = jax.ShapeDtypeStruct((N_HEADS, n, LANES), BF16)
    ya_spec = pl.BlockSpec((1, ML_ROWS, LANES), lambda g, h: (h, g, 0))
    if carry:
        c0, n0, m0 = state
        in_specs += [
            pl.BlockSpec((1, 1, 2, 1, HEAD_DIM, HEAD_DIM), lambda g, h: (g, 0, 0, h, 0, 0)),
            pl.BlockSpec((1, 1, 2, 1, HEAD_DIM, HEAD_DIM), lambda g, h: (g, 0, 0, h, 0, 0)),
            pl.BlockSpec((1, 2, 1, 1, 1), lambda g, h: (g, 0, h, 0, 0)),
        ]
        args += [c0, n0, m0]
        out_shape = ya_shape
        out_specs = ya_spec
    else:
        nseq = n // CHUNK
        out_shape = (ya_shape,
                     jax.ShapeDtypeStruct((nseq, 1, 2, N_HEADS, HEAD_DIM, HEAD_DIM), F32),
                     jax.ShapeDtypeStruct((nseq, 2, N_HEADS, 1, HEAD_DIM), F32),
                     jax.ShapeDtypeStruct((nseq, 2, N_HEADS, 1, 1), F32))
        out_specs = (ya_spec,
                     pl.BlockSpec((ML_NCH, 1, 2, 1, HEAD_DIM, HEAD_DIM), lambda g, h: (g, 0, 0, h, 0, 0)),
                     pl.BlockSpec((ML_NCH, 2, 1, 1, HEAD_DIM), lambda g, h: (g, 0, h, 0, 0)),
                     pl.BlockSpec((ML_NCH, 2, 1, 1, 1), lambda g, h: (g, 0, h, 0, 0)))
    return pl.pallas_call(
        functools.partial(_mlstm_kernel, carry=carry),
        out_shape=out_shape,
        grid=(nblk, N_HEADS),
        in_specs=in_specs,
        out_specs=out_specs,
        scratch_shapes=[pltpu.VMEM((ML_ROWS, CHUNK), F32), pltpu.VMEM((ML_ROWS, HEAD_DIM), F32)],
        compiler_params=_cparams(("parallel", "parallel")),
        name="mlstm_latent" if carry else "mlstm_prompt",
    )(*args)


def _pool_kernel(u_ref, a_ref, ic_ref, wpg_ref, ps_ref, yb_ref, *, seq_len, n_seq):
    for sq in range(n_seq):
        rows = slice(sq * seq_len, (sq + 1) * seq_len)
        for g in range(N_GROUPS):
            u = u_ref[g, rows, :]
            pooled = _dot(a_ref[g], u) * ic_ref[g]
            p = (pooled - u.astype(F32)).astype(BF16)
            yb_ref[g, rows, :] = (_dot(p, wpg_ref[g]) * ps_ref[g]).astype(BF16)


def _pool(p3, amat, invc, wpg, pscale, seq_len, n_seq):
    n = p3.shape[1]
    rows = seq_len * n_seq
    return pl.pallas_call(
        functools.partial(_pool_kernel, seq_len=seq_len, n_seq=n_seq),
        out_shape=jax.ShapeDtypeStruct((N_GROUPS, n, LANES), BF16),
        grid=(n // rows,),
        in_specs=[
            pl.BlockSpec((N_GROUPS, rows, LANES), lambda i: (SLAB_POOL // N_GROUPS, i, 0)),
            _const_spec((N_GROUPS, seq_len, seq_len)),
            _const_spec((N_GROUPS, seq_len, LANES)),
            _const_spec((N_GROUPS, LANES, LANES)),
            _const_spec((N_GROUPS, 1, LANES)),
        ],
        out_specs=pl.BlockSpec((N_GROUPS, rows, LANES), lambda i: (0, i, 0)),
        compiler_params=_cparams(("parallel",)),
        name="pool",
    )(p3, amat, invc, wpg, pscale)


def _window_matrix(size, w):
    idx = np.arange(size)
    lo = np.clip(idx - w // 2, 0, size)
    hi = np.clip(idx - w // 2 + w, 0, size)
    s = np.arange(size)[None, :]
    a = ((s >= lo[:, None]) & (s < hi[:, None])).astype(np.float32)
    return a, (hi - lo).astype(np.float32)


def _pool_constants(seq_len, rows_2d):
    mats, invs = [], []
    for w in POOL_WINDOWS:
        if rows_2d is None:
            a, cnt = _window_matrix(seq_len, w)
        else:
            ar, cr = _window_matrix(rows_2d, w)
            ac, cc = _window_matrix(GRID_W, w)
            a = np.kron(ar, ac)
            cnt = np.kron(cr, cc)
        mats.append(a)
        invs.append(np.repeat((1.0 / cnt)[:, None], LANES, axis=1))
    return (jnp.asarray(np.stack(mats), dtype=BF16), jnp.asarray(np.stack(invs), dtype=F32))


MIX_TM = 512


def _mix_kernel(x_ref, mod_ref, ya_ref, yb_ref, ga_ref, gb_ref, wa_ref, wb_ref, wo_ref, x1_ref):
    ya = jnp.concatenate([ya_ref[i] for i in range(N_HEADS)], axis=1)
    yb = jnp.concatenate([yb_ref[i] for i in range(N_GROUPS)], axis=1)
    ga = jnp.concatenate([ga_ref[i] for i in range(8)], axis=1).astype(F32)
    gb = jnp.concatenate([gb_ref[i] for i in range(8)], axis=1).astype(F32)
    merged = jax.nn.sigmoid(ga) * _dot(ya, wa_ref[...]) + jax.nn.sigmoid(gb) * _dot(yb, wb_ref[...])
    out = _dot(merged.astype(BF16), wo_ref[...])
    x1_ref[...] = x_ref[...] + mod_ref[0, 2:3, :] * out


def _mix(x2d, mod3, mod_row_fn, ya, yb, p3, wa, wb, wo):
    n = x2d.shape[0]
    return pl.pallas_call(
        _mix_kernel,
        out_shape=jax.ShapeDtypeStruct((n, D_MODEL), F32),
        grid=(n // MIX_TM,),
        in_specs=[
            pl.BlockSpec((MIX_TM, D_MODEL), lambda i: (i, 0)),
            pl.BlockSpec((1, 6, D_MODEL), lambda i: (mod_row_fn(i * MIX_TM), 0, 0)),
            pl.BlockSpec((N_HEADS, MIX_TM, LANES), lambda i: (0, i, 0)),
            pl.BlockSpec((N_GROUPS, MIX_TM, LANES), lambda i: (0, i, 0)),
            pl.BlockSpec((8, MIX_TM, LANES), lambda i: (SLAB_GA // 8, i, 0)),
            pl.BlockSpec((8, MIX_TM, LANES), lambda i: (SLAB_GB // 8, i, 0)),
            _const_spec((D_MODEL, D_MODEL)),
            _const_spec((N_GROUPS * LANES, D_MODEL)),
            _const_spec((D_MODEL, D_MODEL)),
        ],
        out_specs=pl.BlockSpec((MIX_TM, D_MODEL), lambda i: (i, 0)),
        compiler_params=_cparams(("parallel",)),
        name="mix",
    )(x2d, mod3, ya, yb, p3, p3, wa, wb, wo)


MOE_TB = 1024
MOE_TILE = 160
MOE_NSUB = 2
MOE_HALF = 8
ROUTER_ROWS = 4 + N_EXPERTS


def _nt_dot(a, b):
    return lax.dot_general(a, b, (((1,), (1,)), ((), ())), preferred_element_type=F32)


def _route_rows(lt):
    lg = [lt[i:i + 1] for i in range(N_GROUPS)]
    mx = jnp.maximum(jnp.maximum(lg[0], lg[1]), jnp.maximum(lg[2], lg[3]))
    p_top = 1.0 / (jnp.exp(lg[0] - mx) + jnp.exp(lg[1] - mx) + jnp.exp(lg[2] - mx) + jnp.exp(lg[3] - mx))
    gidx = jnp.zeros(lg[0].shape, jnp.int32)
    best = lg[0]
    for i in range(1, N_GROUPS):
        upd = lg[i] > best
        gidx = jnp.where(upd, i, gidx)
        best = jnp.where(upd, lg[i], best)

    def expert_row(g, e):
        r = N_GROUPS + g * EXPERTS_PER_GROUP + e
        return lt[r:r + 1]

    le = [jnp.where(gidx == 0, expert_row(0, e),
                    jnp.where(gidx == 1, expert_row(1, e),
                              jnp.where(gidx == 2, expert_row(2, e), expert_row(3, e))))
          for e in range(EXPERTS_PER_GROUP)]
    v1 = jnp.maximum(jnp.maximum(le[0], le[1]), jnp.maximum(le[2], le[3]))
    i1 = jnp.full(v1.shape, EXPERTS_PER_GROUP - 1, jnp.int32)
    for e in range(EXPERTS_PER_GROUP - 2, -1, -1):
        i1 = jnp.where(le[e] == v1, e, i1)
    le2 = [jnp.where(i1 == e, -jnp.inf, le[e]) for e in range(EXPERTS_PER_GROUP)]
    v2 = jnp.maximum(jnp.maximum(le2[0], le2[1]), jnp.maximum(le2[2], le2[3]))
    i2 = jnp.full(v2.shape, EXPERTS_PER_GROUP - 1, jnp.int32)
    for e in range(EXPERTS_PER_GROUP - 2, -1, -1):
        i2 = jnp.where(le2[e] == v2, e, i2)
    e2 = jnp.exp(v2 - v1)
    w1 = p_top / (1.0 + e2)
    w2 = p_top * e2 / (1.0 + e2)
    w_rows, sel_rows = [], []
    for g in range(N_GROUPS):
        for e in range(EXPERTS_PER_GROUP):
            first = jnp.where(gidx == g, jnp.where(i1 == e, 1.0, 0.0), 0.0)
            second = jnp.where(gidx == g, jnp.where(i2 == e, 1.0, 0.0), 0.0)
            w_rows.append(first * w1 + second * w2)
            sel_rows.append(first + second)
    return w_rows, sel_rows


def _moe_kernel(x1_ref, mod_a_ref, mod_b_ref, g2_ref, wrt_hi_ref, wrt_lo_ref, brt_ref, before_ref,
                wg_ref, wu_ref, wd_ref, gf_ref, y_ref, h2_scr, rank_scr, w_scr, p_scr, y_scr):
    e = pl.program_id(1)
    mod_refs = (mod_a_ref, mod_b_ref)
    tn_dims = (((0,), (0,)), ((), ()))
    slot = lax.broadcasted_iota(jnp.int32, (MOE_TILE, MOE_TB), 0).astype(F32)

    @pl.when(e == 0)
    def _():
        for sb in range(MOE_NSUB):
            x1 = x1_ref[sb * MOE_TB:(sb + 1) * MOE_TB, :]
            mod_ref = mod_refs[sb]
            h2 = x1 * _rms_scale(x1) * g2_ref[...]
            h2 = h2 * (1.0 + mod_ref[0, 4:5, :]) + mod_ref[0, 3:4, :]
            h_hi, h_lo = _split_bf16(h2)
            h2_scr[sb] = h_hi
            wr_hi = wrt_hi_ref[...]
            lt = _nt_dot(wr_hi, h_hi) + _nt_dot(wr_hi, h_lo) + _nt_dot(wrt_lo_ref[...], h_hi) + brt_ref[...]
            w_rows, sel_rows = _route_rows(lt)
            sel = jnp.concatenate(sel_rows, axis=0)
            rank = _dot(sel.astype(BF16), before_ref[...])
            rank = jnp.where(sel > 0.0, rank, -1.0)
            for i in range(N_EXPERTS):
                rank_scr[sb * N_EXPERTS + i] = rank[i:i + 1]
                w_scr[sb * N_EXPERTS + i] = w_rows[i]
        y_ref[...] = jnp.zeros_like(y_ref)

    half = lax.rem(e, MOE_HALF)

    def gather_rows(sb, first_slot):
        rank_row = rank_scr[sb * N_EXPERTS + e]
        hit = (slot + first_slot) == rank_row
        pb = jnp.where(hit, 1.0, 0.0).astype(BF16)
        w_slot = jnp.sum(jnp.where(hit, w_scr[sb * N_EXPERTS + e], 0.0), axis=1, keepdims=True)
        return pb, w_slot, _dot(pb, h2_scr[sb]).astype(BF16)

    def expert(xg, w_slot):
        a = _dot(xg, wg_ref[0])
        u = _dot(xg, wu_ref[0])
        act = (a * jax.nn.sigmoid(a) * u * w_slot).astype(BF16)
        return _dot(act, wd_ref[0]).astype(BF16)

    first = [gather_rows(sb, 0.0) for sb in range(MOE_NSUB)]
    yv = expert(jnp.concatenate([f[2] for f in first], axis=0), jnp.concatenate([f[1] for f in first], axis=0))
    for sb in range(MOE_NSUB):
        y_scr[sb, pl.ds(pl.multiple_of(half * MOE_TILE, MOE_TILE), MOE_TILE), :] = \
            yv[sb * MOE_TILE:(sb + 1) * MOE_TILE]

    for sb in range(MOE_NSUB):
        n_routed = jnp.max(rank_scr[sb * N_EXPERTS + e]).astype(jnp.int32) + 1
        n_pass = jnp.int32(0)
        for kk in range(-(-MOE_TB // MOE_TILE)):
            n_pass = n_pass + (n_routed > kk * MOE_TILE).astype(jnp.int32)

        def extra_pass(j, carry, sb=sb):
            pb, w_slot, xg = gather_rows(sb, (j * MOE_TILE).astype(F32))
            y_ref[sb * MOE_TB:(sb + 1) * MOE_TB, :] += lax.dot_general(
                pb, expert(xg, w_slot), tn_dims, preferred_element_type=F32)
            return carry

        lax.fori_loop(1, n_pass, extra_pass, 0)

    @pl.when(half == MOE_HALF - 1)
    def _():
        for sb in range(MOE_NSUB):
            for i in range(MOE_HALF):
                rank_row = rank_scr[sb * N_EXPERTS + (e - (MOE_HALF - 1)) + i]
                p_scr[i * MOE_TILE:(i + 1) * MOE_TILE, :] = jnp.where(slot == rank_row, 1.0, 0.0).astype(BF16)
            sub_rows = slice(sb * MOE_TB, (sb + 1) * MOE_TB)
            y_ref[sub_rows, :] += lax.dot_general(p_scr[...], y_scr[sb], tn_dims, preferred_element_type=F32)

    @pl.when(e == N_EXPERTS - 1)
    def _():
        for sb in range(MOE_NSUB):
            sub_rows = slice(sb * MOE_TB, (sb + 1) * MOE_TB)
            x2 = x1_ref[sub_rows, :] + mod_refs[sb][0, 5:6, :] * y_ref[sub_rows, :]
            y_ref[sub_rows, :] = x2 * _rms_scale(x2) * gf_ref[...]


def _moe(x1, mod3, mod_row_fn, g_norm2, wrt_hi, wrt_lo, brt, wg, wu, wd, g_final):
    n = x1.shape[0]
    rows = MOE_NSUB * MOE_TB
    t_i = lax.broadcasted_iota(jnp.int32, (MOE_TB, MOE_TB), 0)
    t_j = lax.broadcasted_iota(jnp.int32, (MOE_TB, MOE_TB), 1)
    before = (t_i < t_j).astype(BF16)
    return pl.pallas_call(
        _moe_kernel,
        out_shape=jax.ShapeDtypeStruct((n, D_MODEL), F32),
        grid=(n // rows, N_EXPERTS),
        in_specs=[
            pl.BlockSpec((rows, D_MODEL), lambda i, e: (i, 0), pipeline_mode=pl.Buffered(1)),
            pl.BlockSpec((1, 6, D_MODEL), lambda i, e: (mod_row_fn(i * rows), 0, 0)),
            pl.BlockSpec((1, 6, D_MODEL), lambda i, e: (mod_row_fn(i * rows + MOE_TB), 0, 0)),
            _const_spec((1, D_MODEL)),
            _const_spec((LANES, D_MODEL)),
            _const_spec((LANES, D_MODEL)),
            _const_spec((LANES, 1)),
            _const_spec((MOE_TB, MOE_TB)),
            pl.BlockSpec((1, D_MODEL, D_EXPERT), lambda i, e: (e, 0, 0)),
            pl.BlockSpec((1, D_MODEL, D_EXPERT), lambda i, e: (e, 0, 0)),
            pl.BlockSpec((1, D_EXPERT, D_MODEL), lambda i, e: (e, 0, 0)),
            _const_spec((1, D_MODEL)),
        ],
        out_specs=pl.BlockSpec((rows, D_MODEL), lambda i, e: (i, 0)),
        scratch_shapes=[pltpu.VMEM((MOE_NSUB, MOE_TB, D_MODEL), BF16),
                        pltpu.VMEM((MOE_NSUB * N_EXPERTS, 1, MOE_TB), F32),
                        pltpu.VMEM((MOE_NSUB * N_EXPERTS, 1, MOE_TB), F32),
                        pltpu.VMEM((MOE_HALF * MOE_TILE, MOE_TB), BF16),
                        pltpu.VMEM((MOE_NSUB, MOE_HALF * MOE_TILE, D_MODEL), BF16)],
        compiler_params=_cparams(("parallel", "arbitrary")),
        name="moe",
    )(x1, mod3, mod3, g_norm2, wrt_hi, wrt_lo, brt, before, wg, wu, wd, g_final)


def _stream(x, mod3, mod_row_fn, seq_len, rows_2d, state, w):
    bsz, t, _ = x.shape
    n = bsz * t
    x2d = x.reshape(n, D_MODEL)
    p3, cols = _in_projection(x2d, mod3, mod_row_fn, w["g_norm1"], w["wmain"], w["bmain"], w["colscale"],
                              w["wg_hi"], w["wg_lo"], w["bg"])
    rows_f = (-cols[:, N_HEADS:2 * N_HEADS]).T.reshape(N_HEADS, 1, n)
    rows_b = (-cols[:, 3 * N_HEADS:4 * N_HEADS]).T.reshape(N_HEADS, 1, n)
    ml = _mlstm(p3, cols, rows_f, rows_b, w["gn3"], state)
    amat, invc = _pool_constants(seq_len, rows_2d)
    n_seq = max(1, 1024 // seq_len)
    yb = _pool(p3, amat, invc, w["wpg"], w["pscale"], seq_len, n_seq)
    ya = ml if state is not None else ml[0]
    x1 = _mix(x2d, mod3, mod_row_fn, ya, yb, p3, w["wa"], w["wb"], w["wo"])
    y = _moe(x1, mod3, mod_row_fn, w["g_norm2"], w["wrt_hi"], w["wrt_lo"], w["brt"],
             w["wg"], w["wu"], w["wd"], w["g_final"])
    return y.reshape(bsz, t, D_MODEL), ml


def kernel(x_prompt, x_sample, state_C, state_n, state_m, c, c_ctx, w_mod, b_mod, g_norm1, w_in, b_in, gn_gain, w_pool_grp, pool_scale, w_proj_a, w_proj_b, w_out, g_norm2, w_router_grp, b_router_grp, w_router_exp, b_router_exp, w_exp_gate, w_exp_up, w_exp_down, g_final):
    bp, tp, _ = x_prompt.shape
    bs, ts, _ = x_sample.shape
    assert w_mod.shape[0] == 1, "single trunk layer"
    assert tp == CHUNK and ts % CHUNK == 0 and ts == ML_ROWS

    cvec = jnp.zeros((8, D_MODEL), F32).at[0].set(c_ctx).at[1:1 + bs].set(c)
    mod3 = _modulation(cvec, w_mod[0], b_mod[0])

    wi, bi = w_in[0], b_in[0]
    d = D_MODEL
    o_q, o_k, o_v, o_o, o_g, o_p, o_ga, o_gb = 0, d, 2 * d, 3 * d, 4 * d, 4 * d + 32, 4 * d + 32 + 512, 4 * d + 32 + 512 + d
    order = [(o_q, d), (o_k, d), (o_v, d), (o_o, d), (o_ga, d), (o_gb, d), (o_p, 512)]
    wmain = jnp.concatenate([wi[:, s:s + l] for s, l in order], axis=1).astype(BF16)
    bmain = jnp.concatenate([bi[s:s + l] for s, l in order]).reshape(1, D_MAIN)
    colscale = jnp.ones((D_MAIN,), F32).at[SLAB_K * LANES:(SLAB_K + N_HEADS) * LANES].set(HEAD_DIM ** -0.5)
    colscale = colscale.reshape(1, D_MAIN)
    gw = wi[:, o_g:o_g + 32]
    gb = bi[o_g:o_g + 32]
    i_f, f_f, i_b, f_b = (slice(0, 8), slice(8, 16), slice(16, 24), slice(24, 32))
    zw = jnp.zeros((d, 8), F32)
    zb = jnp.zeros((8,), F32)
    wg2 = jnp.concatenate([gw[:, f_f], gw[:, f_f], gw[:, f_b], gw[:, f_b], jnp.zeros((d, LANES - 32), F32),
                           zw, gw[:, i_f], zw, gw[:, i_b], jnp.zeros((d, LANES - 32), F32)], axis=1)
    bg2 = jnp.concatenate([gb[f_f], gb[f_f], gb[f_b], gb[f_b], jnp.zeros((LANES - 32,), F32),
                           zb, gb[i_f], zb, gb[i_b], jnp.zeros((LANES - 32,), F32)]).reshape(1, 2 * LANES)
    wg_hi, wg_lo = _split_bf16(wg2)

    wrt = jnp.concatenate([w_router_grp[0], w_router_exp[0].reshape(d, N_EXPERTS),
                           jnp.zeros((d, LANES - ROUTER_ROWS), F32)], axis=1).T
    brt = jnp.concatenate([b_router_grp[0], b_router_exp[0].reshape(N_EXPERTS),
                           jnp.zeros((LANES - ROUTER_ROWS,), F32)]).reshape(LANES, 1)
    wrt_hi, wrt_lo = _split_bf16(wrt)

    w = dict(
        g_norm1=g_norm1[0].reshape(1, d), wmain=wmain, bmain=bmain, colscale=colscale,
        wg_hi=wg_hi, wg_lo=wg_lo, bg=bg2,
        gn3=gn_gain[0].reshape(N_HEADS, 1, HEAD_DIM),
        wpg=w_pool_grp[0].astype(BF16), pscale=pool_scale[0].reshape(N_GROUPS, 1, LANES),
        wa=w_proj_a[0].astype(BF16), wb=w_proj_b[0].astype(BF16), wo=w_out[0].astype(BF16),
        g_norm2=g_norm2[0].reshape(1, d), wrt_hi=wrt_hi, wrt_lo=wrt_lo, brt=brt,
        wg=w_exp_gate[0].astype(BF16), wu=w_exp_up[0].astype(BF16), wd=w_exp_down[0].astype(BF16),
        g_final=g_final.reshape(1, d),
    )

    y_prompt, (_, cst, nst, mst) = _stream(x_prompt, mod3, lambda r: 0, tp, None, None, w)
    n_rep = jnp.broadcast_to(state_n[..., None], state_C.shape)
    state = (state_C, n_rep, state_m[:, 0].reshape(bs, 2, N_HEADS, 1, 1))
    y_sample, _ = _stream(x_sample, mod3, lambda r: 1 + r // ts, ts, ts // GRID_W, state, w)

    new_c = cst
    new_n = nst.reshape(bp, 1, 2, N_HEADS, HEAD_DIM)
    new_m = mst.reshape(bp, 1, 2, N_HEADS)
    return (y_prompt, y_sample, new_c, new_n, new_m)
```

```python
import functools

import numpy as np
import jax
import jax.numpy as jnp
from jax import lax
from jax.experimental import pallas as pl
from jax.experimental.pallas import tpu as pltpu

F32 = jnp.float32
BF16 = jnp.bfloat16

D_MODEL = 1024
N_HEADS = 8
HEAD_DIM = 128
GRID_W = 64
POOL_WINDOWS = (2, 4, 8, 16)
N_GROUPS = 4
EXPERTS_PER_GROUP = 4
N_EXPERTS = 16
D_EXPERT = 512
EPS = 1e-6
NEG = -1e30

LANES = 128
CHUNK = 256
N_SLABS = 52
SLAB_Q, SLAB_K, SLAB_V, SLAB_O, SLAB_GA, SLAB_GB, SLAB_POOL = 0, 8, 16, 24, 32, 40, 48
D_MAIN = N_SLABS * LANES
VMEM_LIMIT = 56 * 1024 * 1024


def _cparams(sem):
    return pltpu.CompilerParams(dimension_semantics=sem, vmem_limit_bytes=VMEM_LIMIT)


def _const_spec(shape):
    nd = len(shape)
    return pl.BlockSpec(shape, lambda *_: (0,) * nd, pipeline_mode=pl.Buffered(1))


def _split_bf16(x):
    hi = x.astype(BF16)
    lo = (x - hi.astype(F32)).astype(BF16)
    return hi, lo


def _dot(a, b):
    return jnp.dot(a, b, preferred_element_type=F32)


def _dot3(a, b_hi, b_lo):
    a_hi, a_lo = _split_bf16(a)
    return _dot(a_hi, b_hi) + _dot(a_hi, b_lo) + _dot(a_lo, b_hi)


def _rms_scale(x):
    return lax.rsqrt(jnp.mean(x * x, axis=-1, keepdims=True) + EPS)


def _log_sigmoid(x):
    return jnp.minimum(x, 0.0) - jnp.log1p(jnp.exp(-jnp.abs(x)))


def _mod_kernel(c_ref, w_ref, b_ref, o_ref):
    c = c_ref[...]
    s = c * jax.nn.sigmoid(c)
    w_hi, w_lo = _split_bf16(w_ref[...])
    o_ref[...] = _dot3(s, w_hi, w_lo) + b_ref[...]


def _modulation(cvec, w_mod, b_mod):
    tn = 512
    out = pl.pallas_call(
        _mod_kernel,
        out_shape=jax.ShapeDtypeStruct((8, 6 * D_MODEL), F32),
        grid=(6 * D_MODEL // tn,),
        in_specs=[
            pl.BlockSpec((8, D_MODEL), lambda j: (0, 0)),
            pl.BlockSpec((D_MODEL, tn), lambda j: (0, j)),
            pl.BlockSpec((1, tn), lambda j: (0, j)),
        ],
        out_specs=pl.BlockSpec((8, tn), lambda j: (0, j)),
        compiler_params=_cparams(("parallel",)),
        name="modulation",
    )(cvec, w_mod, b_mod.reshape(1, -1))
    return out.reshape(8, 6, D_MODEL)


IN_TM = 512
IN_TN = 512
CMAX_LANE_SHIFT = 32
LOG2E = 1.4426950408889634


def _inproj_kernel(x_ref, mod_ref, g_ref, w_ref, b_ref, cs_ref, wg_hi_ref, wg_lo_ref, bg_ref,
                   p3_ref, cols_ref):
    x = x_ref[...]
    h = x * _rms_scale(x) * g_ref[...]
    h = h * (1.0 + mod_ref[0, 1:2, :]) + mod_ref[0, 0:1, :]
    hb = h.astype(BF16)
    for n in range(D_MAIN // IN_TN):
        sl = slice(n * IN_TN, (n + 1) * IN_TN)
        acc = (_dot(hb, w_ref[:, sl]) + b_ref[:, sl]) * cs_ref[:, sl]
        accb = acc.astype(BF16)
        for j in range(IN_TN // LANES):
            p3_ref[n * (IN_TN // LANES) + j] = accb[:, j * LANES:(j + 1) * LANES]

    g2 = _dot3(h, wg_hi_ref[...], wg_lo_ref[...]) + bg_ref[...]
    ls = _log_sigmoid(g2[:, :LANES])
    gi = g2[:, LANES:]
    row = lax.broadcasted_iota(jnp.int32, (CHUNK, LANES), 0)
    lane = lax.broadcasted_iota(jnp.int32, (CHUNK, LANES), 1)
    for c in range(IN_TM // CHUNK):
        x0 = ls[c * CHUNK:(c + 1) * CHUNK]
        cf = x0
        cb = x0
        s = 1
        while s < CHUNK:
            cf = cf + jnp.where(row >= s, pltpu.roll(cf, s, 0), 0.0)
            cb = cb + jnp.where(row < CHUNK - s, pltpu.roll(cb, CHUNK - s, 0), 0.0)
            s *= 2
        b = jnp.where(lane < 2 * N_HEADS, cf, cb)
        is_imb = (lane & N_HEADS) != 0
        out = jnp.where(is_imb, gi[c * CHUNK:(c + 1) * CHUNK] - b, b)
        mf = out
        mb = out
        s = 1
        while s < CHUNK:
            mf = jnp.maximum(mf, jnp.where(row >= s, pltpu.roll(mf, s, 0), -jnp.inf))
            mb = jnp.maximum(mb, jnp.where(row < CHUNK - s, pltpu.roll(mb, CHUNK - s, 0), -jnp.inf))
            s *= 2
        cmax = pltpu.roll(jnp.where(lane < 2 * N_HEADS, mf, mb), CMAX_LANE_SHIFT, 1)
        cols_ref[c * CHUNK:(c + 1) * CHUNK, :] = jnp.where(lane < CMAX_LANE_SHIFT, out, cmax)


def _in_projection(x2d, mod3, mod_row_fn, g_norm1, wmain, bmain, colscale, wg_hi, wg_lo, bg):
    n = x2d.shape[0]
    return pl.pallas_call(
        _inproj_kernel,
        out_shape=(jax.ShapeDtypeStruct((N_SLABS, n, LANES), BF16),
                   jax.ShapeDtypeStruct((n, LANES), F32)),
        grid=(n // IN_TM,),
        in_specs=[
            pl.BlockSpec((IN_TM, D_MODEL), lambda i: (i, 0)),
            pl.BlockSpec((1, 6, D_MODEL), lambda i: (mod_row_fn(i * IN_TM), 0, 0)),
            _const_spec((1, D_MODEL)),
            _const_spec((D_MODEL, D_MAIN)),
            _const_spec((1, D_MAIN)),
            _const_spec((1, D_MAIN)),
            _const_spec((D_MODEL, 2 * LANES)),
            _const_spec((D_MODEL, 2 * LANES)),
            _const_spec((1, 2 * LANES)),
        ],
        out_specs=(pl.BlockSpec((N_SLABS, IN_TM, LANES), lambda i: (0, i, 0)),
                   pl.BlockSpec((IN_TM, LANES), lambda i: (i, 0))),
        compiler_params=_cparams(("parallel",)),
        name="in_projection",
    )(x2d, mod3, g_norm1, wmain, bmain, colscale, wg_hi, wg_lo, bg)


ML_ROWS = 1024
ML_NCH = ML_ROWS // CHUNK


def _pick_col(blk, lane_iota, idx):
    return jnp.sum(jnp.where(lane_iota == idx, blk, 0.0), axis=1, keepdims=True)


def _mlstm_kernel(*refs, carry):
    if carry:
        (q_ref, k_ref, v_ref, o_ref, cols_ref, rf_ref, rb_ref, gn_ref, c0_ref, n0_ref, m0_ref,
         ya_ref, s_scr, h_scr) = refs
    else:
        (q_ref, k_ref, v_ref, o_ref, cols_ref, rf_ref, rb_ref, gn_ref,
         ya_ref, cst_ref, nst_ref, mst_ref, s_scr, h_scr) = refs
    head = pl.program_id(1)
    lane = lax.broadcasted_iota(jnp.int32, (CHUNK, LANES), 1)
    r_idx = lax.broadcasted_iota(jnp.int32, (CHUNK, CHUNK), 0)
    c_idx = lax.broadcasted_iota(jnp.int32, (CHUNK, CHUNK), 1)
    ones = jnp.ones((CHUNK, HEAD_DIM), BF16)
    tn_dims = (((0,), (0,)), ((), ()))

    for d in range(2):
        mask = (c_idx <= r_idx) if d == 0 else (c_idx >= r_idx)
        row_ref = rf_ref if d == 0 else rb_ref
        last = CHUNK - 1 if d == 0 else 0
        base = head + 2 * N_HEADS * d
        if carry:
            cn = jnp.concatenate([c0_ref[0, 0, d, 0], n0_ref[0, 0, d, 0]], axis=1)
            m = m0_ref[0, d, 0]
        order = range(ML_NCH) if d == 0 else range(ML_NCH - 1, -1, -1)
        for step, c in enumerate(order):
            rows = slice(c * CHUNK, (c + 1) * CHUNK)
            q = q_ref[0, rows, :]
            k = k_ref[0, rows, :]
            v = v_ref[0, rows, :]
            v1 = jnp.concatenate([v, ones], axis=1)
            if d == 0:
                s = lax.dot_general(q, k, (((1,), (1,)), ((), ())), preferred_element_type=F32)
                s_scr[rows, :] = s
            else:
                s = s_scr[rows, :]
            blk = cols_ref[rows, :]
            bcol = _pick_col(blk, lane, base)
            imbcol = _pick_col(blk, lane, base + N_HEADS)
            a = bcol + _pick_col(blk, lane, base + N_HEADS + CMAX_LANE_SHIFT)
            bmirow = row_ref[0, :, rows]
            inter = bcol + (m if carry else NEG)
            m_t = jnp.maximum(inter, a)
            arg = (bcol - m_t) * LOG2E - bmirow * LOG2E
            sw = s * jnp.exp2(jnp.where(mask, arg, NEG))
            nd = _dot(sw.astype(BF16), v1)
            if carry:
                nd = nd + jnp.exp(inter - m_t) * _dot(q, cn.astype(BF16))
            hdir = nd[:, :HEAD_DIM] / jnp.maximum(jnp.abs(nd[:, HEAD_DIM:]), jnp.exp(-m_t))
            if d == 0:
                h_scr[rows, :] = hdir
            else:
                h_scr[rows, :] = h_scr[rows, :] + hdir

            if carry and step == ML_NCH - 1:
                continue
            m_new = m_t[last:last + 1, :]
            b_last = bcol[last:last + 1, :]
            w_col = jnp.exp(b_last - m_new + imbcol)
            if carry:
                u = lax.dot_general(k, (w_col * v1.astype(F32)).astype(BF16), tn_dims,
                                    preferred_element_type=F32)
                cn = jnp.exp(b_last + m - m_new) * cn + u
                m = m_new
            else:
                u = lax.dot_general(k, (w_col * v.astype(F32)).astype(BF16), tn_dims,
                                    preferred_element_type=F32)
                w_row = jnp.exp(b_last - m_new - bmirow)
                w8 = jnp.broadcast_to(w_row, (8, CHUNK)).astype(BF16)
                cst_ref[c, 0, d, 0] = u
                nst_ref[c, d, 0] = _dot(w8, k)[0:1, :]
                mst_ref[c, d, 0] = m_new

    hh = h_scr[...]
    hn = hh * _rms_scale(hh) * gn_ref[0]
    ya_ref[0] = (hn * jax.nn.sigmoid(o_ref[0].astype(F32))).astype(BF16)


def _mlstm(p3, cols, rows_f, rows_b, gn3, state=None):
    n = p3.shape[1]
    nblk = n // ML_ROWS
    carry = state is not None

    def slab(base):
        return pl.BlockSpec((1, ML_ROWS, LANES), lambda g, h: (base + h, g, 0))

    in_specs = [
        slab(SLAB_Q), slab(SLAB_K), slab(SLAB_V), slab(SLAB_O),
        pl.BlockSpec((ML_ROWS, LANES), lambda g, h: (g, 0)),
        pl.BlockSpec((1, 1, ML_ROWS), lambda g, h: (h, 0, g)),
        pl.BlockSpec((1, 1, ML_ROWS), lambda g, h: (h, 0, g)),
        pl.BlockSpec((1, 1, LANES), lambda g, h: (h, 0, 0)),
    ]
    args = [p3, p3, p3, p3, cols, rows_f, rows_b, gn3]
    ya_shape = jax.ShapeDtypeStruct((N_HEADS, n, LANES), BF16)
    ya_spec = pl.BlockSpec((1, ML_ROWS, LANES), lambda g, h: (h, g, 0))
    if carry:
        c0, n0, m0 = state
        in_specs += [
            pl.BlockSpec((1, 1, 2, 1, HEAD_DIM, HEAD_DIM), lambda g, h: (g, 0, 0, h, 0, 0)),
            pl.BlockSpec((1, 1, 2, 1, HEAD_DIM, HEAD_DIM), lambda g, h: (g, 0, 0, h, 0, 0)),
            pl.BlockSpec((1, 2, 1, 1, 1), lambda g, h: (g, 0, h, 0, 0)),
        ]
        args += [c0, n0, m0]
        out_shape = ya_shape
        out_specs = ya_spec
    else:
        nseq = n // CHUNK
        out_shape = (ya_shape,
                     jax.ShapeDtypeStruct((nseq, 1, 2, N_HEADS, HEAD_DIM, HEAD_DIM), F32),
                     jax.ShapeDtypeStruct((nseq, 2, N_HEADS, 1, HEAD_DIM), F32),
                     jax.ShapeDtypeStruct((nseq, 2, N_HEADS, 1, 1), F32))
        out_specs = (ya_spec,
                     pl.BlockSpec((ML_NCH, 1, 2, 1, HEAD_DIM, HEAD_DIM), lambda g, h: (g, 0, 0, h, 0, 0)),
                     pl.BlockSpec((ML_NCH, 2, 1, 1, HEAD_DIM), lambda g, h: (g, 0, h, 0, 0)),
                     pl.BlockSpec((ML_NCH, 2, 1, 1, 1), lambda g, h: (g, 0, h, 0, 0)))
    return pl.pallas_call(
        functools.partial(_mlstm_kernel, carry=carry),
        out_shape=out_shape,
        grid=(nblk, N_HEADS),
        in_specs=in_specs,
        out_specs=out_specs,
        scratch_shapes=[pltpu.VMEM((ML_ROWS, CHUNK), F32), pltpu.VMEM((ML_ROWS, HEAD_DIM), F32)],
        compiler_params=_cparams(("parallel", "parallel")),
        name="mlstm_latent" if carry else "mlstm_prompt",
    )(*args)


def _pool_kernel(u_ref, a_ref, ic_ref, wpg_ref, ps_ref, yb_ref, *, seq_len, n_seq):
    for sq in range(n_seq):
        rows = slice(sq * seq_len, (sq + 1) * seq_len)
        for g in range(N_GROUPS):
            u = u_ref[g, rows, :]
            pooled = _dot(a_ref[g], u) * ic_ref[g]
            p = (pooled - u.astype(F32)).astype(BF16)
            yb_ref[g, rows, :] = (_dot(p, wpg_ref[g]) * ps_ref[g]).astype(BF16)


def _pool(p3, amat, invc, wpg, pscale, seq_len, n_seq):
    n = p3.shape[1]
    rows = seq_len * n_seq
    return pl.pallas_call(
        functools.partial(_pool_kernel, seq_len=seq_len, n_seq=n_seq),
        out_shape=jax.ShapeDtypeStruct((N_GROUPS, n, LANES), BF16),
        grid=(n // rows,),
        in_specs=[
            pl.BlockSpec((N_GROUPS, rows, LANES), lambda i: (SLAB_POOL // N_GROUPS, i, 0)),
            _const_spec((N_GROUPS, seq_len, seq_len)),
            _const_spec((N_GROUPS, seq_len, LANES)),
            _const_spec((N_GROUPS, LANES, LANES)),
            _const_spec((N_GROUPS, 1, LANES)),
        ],
        out_specs=pl.BlockSpec((N_GROUPS, rows, LANES), lambda i: (0, i, 0)),
        compiler_params=_cparams(("parallel",)),
        name="pool",
    )(p3, amat, invc, wpg, pscale)


def _window_matrix(size, w):
    idx = np.arange(size)
    lo = np.clip(idx - w // 2, 0, size)
    hi = np.clip(idx - w // 2 + w, 0, size)
    s = np.arange(size)[None, :]
    a = ((s >= lo[:, None]) & (s < hi[:, None])).astype(np.float32)
    return a, (hi - lo).astype(np.float32)


def _pool_constants(seq_len, rows_2d):
    mats, invs = [], []
    for w in POOL_WINDOWS:
        if rows_2d is None:
            a, cnt = _window_matrix(seq_len, w)
        else:
            ar, cr = _window_matrix(rows_2d, w)
            ac, cc = _window_matrix(GRID_W, w)
            a = np.kron(ar, ac)
            cnt = np.kron(cr, cc)
        mats.append(a)
        invs.append(np.repeat((1.0 / cnt)[:, None], LANES, axis=1))
    return (jnp.asarray(np.stack(mats), dtype=BF16), jnp.asarray(np.stack(invs), dtype=F32))


MIX_TM = 512


def _mix_kernel(x_ref, mod_ref, ya_ref, yb_ref, ga_ref, gb_ref, wa_ref, wb_ref, wo_ref, x1_ref):
    ya = jnp.concatenate([ya_ref[i] for i in range(N_HEADS)], axis=1)
    yb = jnp.concatenate([yb_ref[i] for i in range(N_GROUPS)], axis=1)
    ga = jnp.concatenate([ga_ref[i] for i in range(8)], axis=1).astype(F32)
    gb = jnp.concatenate([gb_ref[i] for i in range(8)], axis=1).astype(F32)
    merged = jax.nn.sigmoid(ga) * _dot(ya, wa_ref[...]) + jax.nn.sigmoid(gb) * _dot(yb, wb_ref[...])
    out = _dot(merged.astype(BF16), wo_ref[...])
    x1_ref[...] = x_ref[...] + mod_ref[0, 2:3, :] * out


def _mix(x2d, mod3, mod_row_fn, ya, yb, p3, wa, wb, wo):
    n = x2d.shape[0]
    return pl.pallas_call(
        _mix_kernel,
        out_shape=jax.ShapeDtypeStruct((n, D_MODEL), F32),
        grid=(n // MIX_TM,),
        in_specs=[
            pl.BlockSpec((MIX_TM, D_MODEL), lambda i: (i, 0)),
            pl.BlockSpec((1, 6, D_MODEL), lambda i: (mod_row_fn(i * MIX_TM), 0, 0)),
            pl.BlockSpec((N_HEADS, MIX_TM, LANES), lambda i: (0, i, 0)),
            pl.BlockSpec((N_GROUPS, MIX_TM, LANES), lambda i: (0, i, 0)),
            pl.BlockSpec((8, MIX_TM, LANES), lambda i: (SLAB_GA // 8, i, 0)),
            pl.BlockSpec((8, MIX_TM, LANES), lambda i: (SLAB_GB // 8, i, 0)),
            _const_spec((D_MODEL, D_MODEL)),
            _const_spec((N_GROUPS * LANES, D_MODEL)),
            _const_spec((D_MODEL, D_MODEL)),
        ],
        out_specs=pl.BlockSpec((MIX_TM, D_MODEL), lambda i: (i, 0)),
        compiler_params=_cparams(("parallel",)),
        name="mix",
    )(x2d, mod3, ya, yb, p3, p3, wa, wb, wo)


MOE_TB = 1024
MOE_TILE = 160
MOE_NSUB = 2
MOE_HALF = 8
ROUTER_ROWS = 4 + N_EXPERTS


def _route_rows(lt):
    lg = [lt[i:i + 1] for i in range(N_GROUPS)]
    mx = jnp.maximum(jnp.maximum(lg[0], lg[1]), jnp.maximum(lg[2], lg[3]))
    p_top = 1.0 / (jnp.exp(lg[0] - mx) + jnp.exp(lg[1] - mx) + jnp.exp(lg[2] - mx) + jnp.exp(lg[3] - mx))
    gidx = jnp.zeros(lg[0].shape, jnp.int32)
    best = lg[0]
    for i in range(1, N_GROUPS):
        upd = lg[i] > best
        gidx = jnp.where(upd, i, gidx)
        best = jnp.where(upd, lg[i], best)

    def expert_row(g, e):
        r = N_GROUPS + g * EXPERTS_PER_GROUP + e
        return lt[r:r + 1]

    le = [jnp.where(gidx == 0, expert_row(0, e),
                    jnp.where(gidx == 1, expert_row(1, e),
                              jnp.where(gidx == 2, expert_row(2, e), expert_row(3, e))))
          for e in range(EXPERTS_PER_GROUP)]
    v1 = jnp.maximum(jnp.maximum(le[0], le[1]), jnp.maximum(le[2], le[3]))
    i1 = jnp.full(v1.shape, EXPERTS_PER_GROUP - 1, jnp.int32)
    for e in range(EXPERTS_PER_GROUP - 2, -1, -1):
        i1 = jnp.where(le[e] == v1, e, i1)
    le2 = [jnp.where(i1 == e, -jnp.inf, le[e]) for e in range(EXPERTS_PER_GROUP)]
    v2 = jnp.maximum(jnp.maximum(le2[0], le2[1]), jnp.maximum(le2[2], le2[3]))
    i2 = jnp.full(v2.shape, EXPERTS_PER_GROUP - 1, jnp.int32)
    for e in range(EXPERTS_PER_GROUP - 2, -1, -1):
        i2 = jnp.where(le2[e] == v2, e, i2)
    e2 = jnp.exp(v2 - v1)
    w1 = p_top / (1.0 + e2)
    w2 = p_top * e2 / (1.0 + e2)
    w_rows, sel_rows = [], []
    for g in range(N_GROUPS):
        for e in range(EXPERTS_PER_GROUP):
            first = jnp.where(gidx == g, jnp.where(i1 == e, 1.0, 0.0), 0.0)
            second = jnp.where(gidx == g, jnp.where(i2 == e, 1.0, 0.0), 0.0)
            w_rows.append(first * w1 + second * w2)
            sel_rows.append(first + second)
    return w_rows, sel_rows


def _moe_kernel(x1_ref, mod_a_ref, mod_b_ref, g2_ref, wr_hi_ref, wr_lo_ref, br_ref, before_ref,
                wg_ref, wu_ref, wd_ref, gf_ref, y_ref, h2_scr, rank_scr, w_scr, p_scr, y_scr):
    e = pl.program_id(1)
    mod_refs = (mod_a_ref, mod_b_ref)
    tn_dims = (((0,), (0,)), ((), ()))
    slot = lax.broadcasted_iota(jnp.int32, (MOE_TILE, MOE_TB), 0).astype(F32)

    @pl.when(e == 0)
    def _():
        for sb in range(MOE_NSUB):
            x1 = x1_ref[sb * MOE_TB:(sb + 1) * MOE_TB, :]
            mod_ref = mod_refs[sb]
            h2 = x1 * _rms_scale(x1) * g2_ref[...]
            h2 = h2 * (1.0 + mod_ref[0, 4:5, :]) + mod_ref[0, 3:4, :]
            h_hi, h_lo = _split_bf16(h2)
            h2_scr[sb] = h_hi
            wr_hi = wr_hi_ref[...]
            logits = _dot(h_hi, wr_hi) + _dot(h_lo, wr_hi) + _dot(h_hi, wr_lo_ref[...]) + br_ref[...]
            w_rows, sel_rows = _route_rows(logits.T)
            sel = jnp.concatenate(sel_rows, axis=0)
            rank = _dot(sel.astype(BF16), before_ref[...])
            rank = jnp.where(sel > 0.0, rank, -1.0)
            for i in range(N_EXPERTS):
                rank_scr[sb * N_EXPERTS + i] = rank[i:i + 1]
                w_scr[sb * N_EXPERTS + i] = w_rows[i]
        y_ref[...] = jnp.zeros_like(y_ref)

    half = lax.rem(e, MOE_HALF)

    def gather_rows(sb, first_slot):
        rank_row = rank_scr[sb * N_EXPERTS + e]
        hit = (slot + first_slot) == rank_row
        pb = jnp.where(hit, 1.0, 0.0).astype(BF16)
        w_slot = jnp.sum(jnp.where(hit, w_scr[sb * N_EXPERTS + e], 0.0), axis=1, keepdims=True)
        return pb, w_slot, _dot(pb, h2_scr[sb]).astype(BF16)

    def expert(xg, w_slot):
        a = _dot(xg, wg_ref[0])
        u = _dot(xg, wu_ref[0])
        act = (a * jax.nn.sigmoid(a) * u * w_slot).astype(BF16)
        return _dot(act, wd_ref[0]).astype(BF16)

    first = [gather_rows(sb, 0.0) for sb in range(MOE_NSUB)]
    yv = expert(jnp.concatenate([f[2] for f in first], axis=0), jnp.concatenate([f[1] for f in first], axis=0))
    for sb in range(MOE_NSUB):
        y_scr[sb, pl.ds(pl.multiple_of(half * MOE_TILE, MOE_TILE), MOE_TILE), :] = \
            yv[sb * MOE_TILE:(sb + 1) * MOE_TILE]

    for sb in range(MOE_NSUB):
        n_routed = jnp.max(rank_scr[sb * N_EXPERTS + e]).astype(jnp.int32) + 1
        n_pass = jnp.int32(0)
        for kk in range(-(-MOE_TB // MOE_TILE)):
            n_pass = n_pass + (n_routed > kk * MOE_TILE).astype(jnp.int32)

        def extra_pass(j, carry, sb=sb):
            pb, w_slot, xg = gather_rows(sb, (j * MOE_TILE).astype(F32))
            y_ref[sb * MOE_TB:(sb + 1) * MOE_TB, :] += lax.dot_general(
                pb, expert(xg, w_slot), tn_dims, preferred_element_type=F32)
            return carry

        lax.fori_loop(1, n_pass, extra_pass, 0)

    @pl.when(half == MOE_HALF - 1)
    def _():
        for sb in range(MOE_NSUB):
            for i in range(MOE_HALF):
                rank_row = rank_scr[sb * N_EXPERTS + (e - (MOE_HALF - 1)) + i]
                p_scr[i * MOE_TILE:(i + 1) * MOE_TILE, :] = jnp.where(slot == rank_row, 1.0, 0.0).astype(BF16)
            sub_rows = slice(sb * MOE_TB, (sb + 1) * MOE_TB)
            y_ref[sub_rows, :] += lax.dot_general(p_scr[...], y_scr[sb], tn_dims, preferred_element_type=F32)

    @pl.when(e == N_EXPERTS - 1)
    def _():
        for sb in range(MOE_NSUB):
            sub_rows = slice(sb * MOE_TB, (sb + 1) * MOE_TB)
            x2 = x1_ref[sub_rows, :] + mod_refs[sb][0, 5:6, :] * y_ref[sub_rows, :]
            y_ref[sub_rows, :] = x2 * _rms_scale(x2) * gf_ref[...]


def _moe(x1, mod3, mod_row_fn, g_norm2, wr_hi, wr_lo, br, wg, wu, wd, g_final):
    n = x1.shape[0]
    rows = MOE_NSUB * MOE_TB
    t_i = lax.broadcasted_iota(jnp.int32, (MOE_TB, MOE_TB), 0)
    t_j = lax.broadcasted_iota(jnp.int32, (MOE_TB, MOE_TB), 1)
    before = (t_i < t_j).astype(BF16)
    return pl.pallas_call(
        _moe_kernel,
        out_shape=jax.ShapeDtypeStruct((n, D_MODEL), F32),
        grid=(n // rows, N_EXPERTS),
        in_specs=[
            pl.BlockSpec((rows, D_MODEL), lambda i, e: (i, 0), pipeline_mode=pl.Buffered(1)),
            pl.BlockSpec((1, 6, D_MODEL), lambda i, e: (mod_row_fn(i * rows), 0, 0)),
            pl.BlockSpec((1, 6, D_MODEL), lambda i, e: (mod_row_fn(i * rows + MOE_TB), 0, 0)),
            _const_spec((1, D_MODEL)),
            _const_spec((D_MODEL, LANES)),
            _const_spec((D_MODEL, LANES)),
            _const_spec((1, LANES)),
            _const_spec((MOE_TB, MOE_TB)),
            pl.BlockSpec((1, D_MODEL, D_EXPERT), lambda i, e: (e, 0, 0)),
            pl.BlockSpec((1, D_MODEL, D_EXPERT), lambda i, e: (e, 0, 0)),
            pl.BlockSpec((1, D_EXPERT, D_MODEL), lambda i, e: (e, 0, 0)),
            _const_spec((1, D_MODEL)),
        ],
        out_specs=pl.BlockSpec((rows, D_MODEL), lambda i, e: (i, 0)),
        scratch_shapes=[pltpu.VMEM((MOE_NSUB, MOE_TB, D_MODEL), BF16),
                        pltpu.VMEM((MOE_NSUB * N_EXPERTS, 1, MOE_TB), F32),
                        pltpu.VMEM((MOE_NSUB * N_EXPERTS, 1, MOE_TB), F32),
                        pltpu.VMEM((MOE_HALF * MOE_TILE, MOE_TB), BF16),
                        pltpu.VMEM((MOE_NSUB, MOE_HALF * MOE_TILE, D_MODEL), BF16)],
        compiler_params=_cparams(("parallel", "arbitrary")),
        name="moe",
    )(x1, mod3, mod3, g_norm2, wr_hi, wr_lo, br, before, wg, wu, wd, g_final)


def _stream(x, mod3, mod_row_fn, seq_len, rows_2d, state, w):
    bsz, t, _ = x.shape
    n = bsz * t
    x2d = x.reshape(n, D_MODEL)
    p3, cols = _in_projection(x2d, mod3, mod_row_fn, w["g_norm1"], w["wmain"], w["bmain"], w["colscale"],
                              w["wg_hi"], w["wg_lo"], w["bg"])
    rows_f = (-cols[:, N_HEADS:2 * N_HEADS]).T.reshape(N_HEADS, 1, n)
    rows_b = (-cols[:, 3 * N_HEADS:4 * N_HEADS]).T.reshape(N_HEADS, 1, n)
    ml = _mlstm(p3, cols, rows_f, rows_b, w["gn3"], state)
    amat, invc = _pool_constants(seq_len, rows_2d)
    n_seq = max(1, 1024 // seq_len)
    yb = _pool(p3, amat, invc, w["wpg"], w["pscale"], seq_len, n_seq)
    ya = ml if state is not None else ml[0]
    x1 = _mix(x2d, mod3, mod_row_fn, ya, yb, p3, w["wa"], w["wb"], w["wo"])
    y = _moe(x1, mod3, mod_row_fn, w["g_norm2"], w["wr_hi"], w["wr_lo"], w["br"],
             w["wg"], w["wu"], w["wd"], w["g_final"])
    return y.reshape(bsz, t, D_MODEL), ml


def kernel(x_prompt, x_sample, state_C, state_n, state_m, c, c_ctx, w_mod, b_mod, g_norm1, w_in, b_in, gn_gain, w_pool_grp, pool_scale, w_proj_a, w_proj_b, w_out, g_norm2, w_router_grp, b_router_grp, w_router_exp, b_router_exp, w_exp_gate, w_exp_up, w_exp_down, g_final):
    bp, tp, _ = x_prompt.shape
    bs, ts, _ = x_sample.shape
    assert w_mod.shape[0] == 1, "single trunk layer"
    assert tp == CHUNK and ts % CHUNK == 0 and ts == ML_ROWS

    cvec = jnp.zeros((8, D_MODEL), F32).at[0].set(c_ctx).at[1:1 + bs].set(c)
    mod3 = _modulation(cvec, w_mod[0], b_mod[0])

    wi, bi = w_in[0], b_in[0]
    d = D_MODEL
    o_q, o_k, o_v, o_o, o_g, o_p, o_ga, o_gb = 0, d, 2 * d, 3 * d, 4 * d, 4 * d + 32, 4 * d + 32 + 512, 4 * d + 32 + 512 + d
    order = [(o_q, d), (o_k, d), (o_v, d), (o_o, d), (o_ga, d), (o_gb, d), (o_p, 512)]
    wmain = jnp.concatenate([wi[:, s:s + l] for s, l in order], axis=1).astype(BF16)
    bmain = jnp.concatenate([bi[s:s + l] for s, l in order]).reshape(1, D_MAIN)
    colscale = jnp.ones((D_MAIN,), F32).at[SLAB_K * LANES:(SLAB_K + N_HEADS) * LANES].set(HEAD_DIM ** -0.5)
    colscale = colscale.reshape(1, D_MAIN)
    gw = wi[:, o_g:o_g + 32]
    gb = bi[o_g:o_g + 32]
    i_f, f_f, i_b, f_b = (slice(0, 8), slice(8, 16), slice(16, 24), slice(24, 32))
    zw = jnp.zeros((d, 8), F32)
    zb = jnp.zeros((8,), F32)
    wg2 = jnp.concatenate([gw[:, f_f], gw[:, f_f], gw[:, f_b], gw[:, f_b], jnp.zeros((d, LANES - 32), F32),
                           zw, gw[:, i_f], zw, gw[:, i_b], jnp.zeros((d, LANES - 32), F32)], axis=1)
    bg2 = jnp.concatenate([gb[f_f], gb[f_f], gb[f_b], gb[f_b], jnp.zeros((LANES - 32,), F32),
                           zb, gb[i_f], zb, gb[i_b], jnp.zeros((LANES - 32,), F32)]).reshape(1, 2 * LANES)
    wg_hi, wg_lo = _split_bf16(wg2)

    wr = jnp.concatenate([w_router_grp[0], w_router_exp[0].reshape(d, N_EXPERTS),
                           jnp.zeros((d, LANES - ROUTER_ROWS), F32)], axis=1)
    br = jnp.concatenate([b_router_grp[0], b_router_exp[0].reshape(N_EXPERTS),
                           jnp.zeros((LANES - ROUTER_ROWS,), F32)]).reshape(1, LANES)
    wr_hi, wr_lo = _split_bf16(wr)

    w = dict(
        g_norm1=g_norm1[0].reshape(1, d), wmain=wmain, bmain=bmain, colscale=colscale,
        wg_hi=wg_hi, wg_lo=wg_lo, bg=bg2,
        gn3=gn_gain[0].reshape(N_HEADS, 1, HEAD_DIM),
        wpg=w_pool_grp[0].astype(BF16), pscale=pool_scale[0].reshape(N_GROUPS, 1, LANES),
        wa=w_proj_a[0].astype(BF16), wb=w_proj_b[0].astype(BF16), wo=w_out[0].astype(BF16),
        g_norm2=g_norm2[0].reshape(1, d), wr_hi=wr_hi, wr_lo=wr_lo, br=br,
        wg=w_exp_gate[0].astype(BF16), wu=w_exp_up[0].astype(BF16), wd=w_exp_down[0].astype(BF16),
        g_final=g_final.reshape(1, d),
    )

    y_prompt, (_, cst, nst, mst) = _stream(x_prompt, mod3, lambda r: 0, tp, None, None, w)
    n_rep = jnp.broadcast_to(state_n[..., None], state_C.shape)
    state = (state_C, n_rep, state_m[:, 0].reshape(bs, 2, N_HEADS, 1, 1))
    y_sample, _ = _stream(x_sample, mod3, lambda r: 1 + r // ts, ts, ts // GRID_W, state, w)

    new_c = cst
    new_n = nst.reshape(bp, 1, 2, N_HEADS, HEAD_DIM)
    new_m = mst.reshape(bp, 1, 2, N_HEADS)
    return (y_prompt, y_sample, new_c, new_n, new_m)
```

```python
import functools

import numpy as np
import jax
import jax.numpy as jnp
from jax import lax
from jax.experimental import pallas as pl
from jax.experimental.pallas import tpu as pltpu

F32 = jnp.float32
BF16 = jnp.bfloat16

D_MODEL = 1024
N_HEADS = 8
HEAD_DIM = 128
GRID_W = 64
POOL_WINDOWS = (2, 4, 8, 16)
N_GROUPS = 4
EXPERTS_PER_GROUP = 4
N_EXPERTS = 16
D_EXPERT = 512
EPS = 1e-6
NEG = -1e30

LANES = 128
CHUNK = 256
N_SLABS = 52
SLAB_Q, SLAB_K, SLAB_V, SLAB_O, SLAB_GA, SLAB_GB, SLAB_POOL = 0, 8, 16, 24, 32, 40, 48
D_MAIN = N_SLABS * LANES
VMEM_LIMIT = 56 * 1024 * 1024


def _cparams(sem):
    return pltpu.CompilerParams(dimension_semantics=sem, vmem_limit_bytes=VMEM_LIMIT)


def _const_spec(shape):
    nd = len(shape)
    return pl.BlockSpec(shape, lambda *_: (0,) * nd, pipeline_mode=pl.Buffered(1))


def _split_bf16(x):
    hi = x.astype(BF16)
    lo = (x - hi.astype(F32)).astype(BF16)
    return hi, lo


def _dot(a, b):
    return jnp.dot(a, b, preferred_element_type=F32)


def _dot3(a, b_hi, b_lo):
    a_hi, a_lo = _split_bf16(a)
    return _dot(a_hi, b_hi) + _dot(a_hi, b_lo) + _dot(a_lo, b_hi)


def _rms_scale(x):
    return lax.rsqrt(jnp.mean(x * x, axis=-1, keepdims=True) + EPS)


def _log_sigmoid(x):
    return jnp.minimum(x, 0.0) - jnp.log1p(jnp.exp(-jnp.abs(x)))


def _mod_kernel(c_ref, w_ref, b_ref, o_ref):
    c = c_ref[...]
    s = c * jax.nn.sigmoid(c)
    w_hi, w_lo = _split_bf16(w_ref[...])
    o_ref[...] = _dot3(s, w_hi, w_lo) + b_ref[...]


def _modulation(cvec, w_mod, b_mod):
    tn = 512
    out = pl.pallas_call(
        _mod_kernel,
        out_shape=jax.ShapeDtypeStruct((8, 6 * D_MODEL), F32),
        grid=(6 * D_MODEL // tn,),
        in_specs=[
            pl.BlockSpec((8, D_MODEL), lambda j: (0, 0)),
            pl.BlockSpec((D_MODEL, tn), lambda j: (0, j)),
            pl.BlockSpec((1, tn), lambda j: (0, j)),
        ],
        out_specs=pl.BlockSpec((8, tn), lambda j: (0, j)),
        compiler_params=_cparams(("parallel",)),
        name="modulation",
    )(cvec, w_mod, b_mod.reshape(1, -1))
    return out.reshape(8, 6, D_MODEL)


IN_TM = 512
IN_TN = 512
CMAX_LANE_SHIFT = 32
LOG2E = 1.4426950408889634


def _inproj_kernel(x_ref, mod_ref, g_ref, w_ref, b_ref, cs_ref, wg_hi_ref, wg_lo_ref, bg_ref,
                   p3_ref, cols_ref):
    x = x_ref[...]
    h = x * _rms_scale(x) * g_ref[...]
    h = h * (1.0 + mod_ref[0, 1:2, :]) + mod_ref[0, 0:1, :]
    hb = h.astype(BF16)
    for n in range(D_MAIN // IN_TN):
        sl = slice(n * IN_TN, (n + 1) * IN_TN)
        acc = (_dot(hb, w_ref[:, sl]) + b_ref[:, sl]) * cs_ref[:, sl]
        accb = acc.astype(BF16)
        for j in range(IN_TN // LANES):
            p3_ref[n * (IN_TN // LANES) + j] = accb[:, j * LANES:(j + 1) * LANES]

    g2 = _dot3(h, wg_hi_ref[...], wg_lo_ref[...]) + bg_ref[...]
    ls = _log_sigmoid(g2[:, :LANES])
    gi = g2[:, LANES:]
    row = lax.broadcasted_iota(jnp.int32, (CHUNK, LANES), 0)
    lane = lax.broadcasted_iota(jnp.int32, (CHUNK, LANES), 1)
    for c in range(IN_TM // CHUNK):
        x0 = ls[c * CHUNK:(c + 1) * CHUNK]
        cf = x0
        cb = x0
        s = 1
        while s < CHUNK:
            cf = cf + jnp.where(row >= s, pltpu.roll(cf, s, 0), 0.0)
            cb = cb + jnp.where(row < CHUNK - s, pltpu.roll(cb, CHUNK - s, 0), 0.0)
            s *= 2
        b = jnp.where(lane < 2 * N_HEADS, cf, cb)
        is_imb = (lane & N_HEADS) != 0
        out = jnp.where(is_imb, gi[c * CHUNK:(c + 1) * CHUNK] - b, b)
        mf = out
        mb = out
        s = 1
        while s < CHUNK:
            mf = jnp.maximum(mf, jnp.where(row >= s, pltpu.roll(mf, s, 0), -jnp.inf))
            mb = jnp.maximum(mb, jnp.where(row < CHUNK - s, pltpu.roll(mb, CHUNK - s, 0), -jnp.inf))
            s *= 2
        cmax = pltpu.roll(jnp.where(lane < 2 * N_HEADS, mf, mb), CMAX_LANE_SHIFT, 1)
        cols_ref[c * CHUNK:(c + 1) * CHUNK, :] = jnp.where(lane < CMAX_LANE_SHIFT, out, cmax)


def _in_projection(x2d, mod3, mod_row_fn, g_norm1, wmain, bmain, colscale, wg_hi, wg_lo, bg):
    n = x2d.shape[0]
    return pl.pallas_call(
        _inproj_kernel,
        out_shape=(jax.ShapeDtypeStruct((N_SLABS, n, LANES), BF16),
                   jax.ShapeDtypeStruct((n, LANES), F32)),
        grid=(n // IN_TM,),
        in_specs=[
            pl.BlockSpec((IN_TM, D_MODEL), lambda i: (i, 0)),
            pl.BlockSpec((1, 6, D_MODEL), lambda i: (mod_row_fn(i * IN_TM), 0, 0)),
            _const_spec((1, D_MODEL)),
            _const_spec((D_MODEL, D_MAIN)),
            _const_spec((1, D_MAIN)),
            _const_spec((1, D_MAIN)),
            _const_spec((D_MODEL, 2 * LANES)),
            _const_spec((D_MODEL, 2 * LANES)),
            _const_spec((1, 2 * LANES)),
        ],
        out_specs=(pl.BlockSpec((N_SLABS, IN_TM, LANES), lambda i: (0, i, 0)),
                   pl.BlockSpec((IN_TM, LANES), lambda i: (i, 0))),
        compiler_params=_cparams(("parallel",)),
        name="in_projection",
    )(x2d, mod3, g_norm1, wmain, bmain, colscale, wg_hi, wg_lo, bg)


ML_ROWS = 1024
ML_NCH = ML_ROWS // CHUNK


def _pick_col(blk, lane_iota, idx):
    return jnp.sum(jnp.where(lane_iota == idx, blk, 0.0), axis=1, keepdims=True)


def _mlstm_kernel(*refs, carry):
    if carry:
        (q_ref, k_ref, v_ref, o_ref, cols_ref, rf_ref, rb_ref, gn_ref, c0_ref, n0_ref, m0_ref,
         ya_ref, s_scr, h_scr) = refs
    else:
        (q_ref, k_ref, v_ref, o_ref, cols_ref, rf_ref, rb_ref, gn_ref,
         ya_ref, cst_ref, nst_ref, mst_ref, s_scr, h_scr) = refs
    head = pl.program_id(1)
    lane = lax.broadcasted_iota(jnp.int32, (CHUNK, LANES), 1)
    r_idx = lax.broadcasted_iota(jnp.int32, (CHUNK, CHUNK), 0)
    c_idx = lax.broadcasted_iota(jnp.int32, (CHUNK, CHUNK), 1)
    ones = jnp.ones((CHUNK, HEAD_DIM), BF16)
    tn_dims = (((0,), (0,)), ((), ()))

    for d in range(2):
        mask = (c_idx <= r_idx) if d == 0 else (c_idx >= r_idx)
        row_ref = rf_ref if d == 0 else rb_ref
        last = CHUNK - 1 if d == 0 else 0
        base = head + 2 * N_HEADS * d
        if carry:
            cn = jnp.concatenate([c0_ref[0, 0, d, 0], n0_ref[0, 0, d, 0]], axis=1)
            m = m0_ref[0, d, 0]
        order = range(ML_NCH) if d == 0 else range(ML_NCH - 1, -1, -1)
        for step, c in enumerate(order):
            rows = slice(c * CHUNK, (c + 1) * CHUNK)
            q = q_ref[0, rows, :]
            k = k_ref[0, rows, :]
            v = v_ref[0, rows, :]
            v1 = jnp.concatenate([v, ones], axis=1)
            if d == 0:
                s = lax.dot_general(q, k, (((1,), (1,)), ((), ())), preferred_element_type=F32)
                s_scr[rows, :] = s
            else:
                s = s_scr[rows, :]
            blk = cols_ref[rows, :]
            bcol = _pick_col(blk, lane, base)
            imbcol = _pick_col(blk, lane, base + N_HEADS)
            a = bcol + _pick_col(blk, lane, base + N_HEADS + CMAX_LANE_SHIFT)
            bmirow = row_ref[0, :, rows]
            inter = bcol + (m if carry else NEG)
            m_t = jnp.maximum(inter, a)
            arg = (bcol - m_t) * LOG2E - bmirow * LOG2E
            sw = s * jnp.exp2(jnp.where(mask, arg, NEG))
            nd = _dot(sw.astype(BF16), v1)
            if carry:
                nd = nd + jnp.exp(inter - m_t) * _dot(q, cn.astype(BF16))
            hdir = nd[:, :HEAD_DIM] / jnp.maximum(jnp.abs(nd[:, HEAD_DIM:]), jnp.exp(-m_t))
            if d == 0:
                h_scr[rows, :] = hdir
            else:
                h_scr[rows, :] = h_scr[rows, :] + hdir

            if carry and step == ML_NCH - 1:
                continue
            m_new = m_t[last:last + 1, :]
            b_last = bcol[last:last + 1, :]
            w_col = jnp.exp(b_last - m_new + imbcol)
            if carry:
                u = lax.dot_general(k, (w_col * v1.astype(F32)).astype(BF16), tn_dims,
                                    preferred_element_type=F32)
                cn = jnp.exp(b_last + m - m_new) * cn + u
                m = m_new
            else:
                u = lax.dot_general(k, (w_col * v.astype(F32)).astype(BF16), tn_dims,
                                    preferred_element_type=F32)
                w_row = jnp.exp(b_last - m_new - bmirow)
                w8 = jnp.broadcast_to(w_row, (8, CHUNK)).astype(BF16)
                cst_ref[c, 0, d, 0] = u
                nst_ref[c, d, 0] = _dot(w8, k)[0:1, :]
                mst_ref[c, d, 0] = m_new

    hh = h_scr[...]
    hn = hh * _rms_scale(hh) * gn_ref[0]
    ya_ref[0] = (hn * jax.nn.sigmoid(o_ref[0].astype(F32))).astype(BF16)


def _mlstm(p3, cols, rows_f, rows_b, gn3, state=None):
    n = p3.shape[1]
    nblk = n // ML_ROWS
    carry = state is not None

    def slab(base):
        return pl.BlockSpec((1, ML_ROWS, LANES), lambda g, h: (base + h, g, 0))

    in_specs = [
        slab(SLAB_Q), slab(SLAB_K), slab(SLAB_V), slab(SLAB_O),
        pl.BlockSpec((ML_ROWS, LANES), lambda g, h: (g, 0)),
        pl.BlockSpec((1, 1, ML_ROWS), lambda g, h: (h, 0, g)),
        pl.BlockSpec((1, 1, ML_ROWS), lambda g, h: (h, 0, g)),
        pl.BlockSpec((1, 1, LANES), lambda g, h: (h, 0, 0)),
    ]
    args = [p3, p3, p3, p3, cols, rows_f, rows_b, gn3]
    ya_shape = jax.ShapeDtypeStruct((N_HEADS, n, LANES), BF16)
    ya_spec = pl.BlockSpec((1, ML_ROWS, LANES), lambda g, h: (h, g, 0))
    if carry:
        c0, n0, m0 = state
        in_specs += [
            pl.BlockSpec((1, 1, 2, 1, HEAD_DIM, HEAD_DIM), lambda g, h: (g, 0, 0, h, 0, 0)),
            pl.BlockSpec((1, 1, 2, 1, HEAD_DIM, HEAD_DIM), lambda g, h: (g, 0, 0, h, 0, 0)),
            pl.BlockSpec((1, 2, 1, 1, 1), lambda g, h: (g, 0, h, 0, 0)),
        ]
        args += [c0, n0, m0]
        out_shape = ya_shape
        out_specs = ya_spec
    else:
        nseq = n // CHUNK
        out_shape = (ya_shape,
                     jax.ShapeDtypeStruct((nseq, 1, 2, N_HEADS, HEAD_DIM, HEAD_DIM), F32),
                     jax.ShapeDtypeStruct((nseq, 2, N_HEADS, 1, HEAD_DIM), F32),
                     jax.ShapeDtypeStruct((nseq, 2, N_HEADS, 1, 1), F32))
        out_specs = (ya_spec,
                     pl.BlockSpec((ML_NCH, 1, 2, 1, HEAD_DIM, HEAD_DIM), lambda g, h: (g, 0, 0, h, 0, 0)),
                     pl.BlockSpec((ML_NCH, 2, 1, 1, HEAD_DIM), lambda g, h: (g, 0, h, 0, 0)),
                     pl.BlockSpec((ML_NCH, 2, 1, 1, 1), lambda g, h: (g, 0, h, 0, 0)))
    return pl.pallas_call(
        functools.partial(_mlstm_kernel, carry=carry),
        out_shape=out_shape,
        grid=(nblk, N_HEADS),
        in_specs=in_specs,
        out_specs=out_specs,
        scratch_shapes=[pltpu.VMEM((ML_ROWS, CHUNK), F32), pltpu.VMEM((ML_ROWS, HEAD_DIM), F32)],
        compiler_params=_cparams(("parallel", "parallel")),
        name="mlstm_latent" if carry else "mlstm_prompt",
    )(*args)


def _pool_kernel(u_ref, a_ref, ic_ref, wpg_ref, ps_ref, yb_ref, *, seq_len, n_seq):
    for sq in range(n_seq):
        rows = slice(sq * seq_len, (sq + 1) * seq_len)
        for g in range(N_GROUPS):
            u = u_ref[g, rows, :]
            pooled = _dot(a_ref[g], u) * ic_ref[g]
            p = (pooled - u.astype(F32)).astype(BF16)
            yb_ref[g, rows, :] = (_dot(p, wpg_ref[g]) * ps_ref[g]).astype(BF16)


def _pool(p3, amat, invc, wpg, pscale, seq_len, n_seq):
    n = p3.shape[1]
    rows = seq_len * n_seq
    return pl.pallas_call(
        functools.partial(_pool_kernel, seq_len=seq_len, n_seq=n_seq),
        out_shape=jax.ShapeDtypeStruct((N_GROUPS, n, LANES), BF16),
        grid=(n // rows,),
        in_specs=[
            pl.BlockSpec((N_GROUPS, rows, LANES), lambda i: (SLAB_POOL // N_GROUPS, i, 0)),
            _const_spec((N_GROUPS, seq_len, seq_len)),
            _const_spec((N_GROUPS, seq_len, LANES)),
            _const_spec((N_GROUPS, LANES, LANES)),
            _const_spec((N_GROUPS, 1, LANES)),
        ],
        out_specs=pl.BlockSpec((N_GROUPS, rows, LANES), lambda i: (0, i, 0)),
        compiler_params=_cparams(("parallel",)),
        name="pool",
    )(p3, amat, invc, wpg, pscale)


def _window_matrix(size, w):
    idx = np.arange(size)
    lo = np.clip(idx - w // 2, 0, size)
    hi = np.clip(idx - w // 2 + w, 0, size)
    s = np.arange(size)[None, :]
    a = ((s >= lo[:, None]) & (s < hi[:, None])).astype(np.float32)
    return a, (hi - lo).astype(np.float32)


def _pool_constants(seq_len, rows_2d):
    mats, invs = [], []
    for w in POOL_WINDOWS:
        if rows_2d is None:
            a, cnt = _window_matrix(seq_len, w)
        else:
            ar, cr = _window_matrix(rows_2d, w)
            ac, cc = _window_matrix(GRID_W, w)
            a = np.kron(ar, ac)
            cnt = np.kron(cr, cc)
        mats.append(a)
        invs.append(np.repeat((1.0 / cnt)[:, None], LANES, axis=1))
    return (jnp.asarray(np.stack(mats), dtype=BF16), jnp.asarray(np.stack(invs), dtype=F32))


MIX_TM = 512


def _mix_kernel(x_ref, mod_ref, ya_ref, yb_ref, ga_ref, gb_ref, wa_ref, wb_ref, wo_ref, x1_ref):
    ya = jnp.concatenate([ya_ref[i] for i in range(N_HEADS)], axis=1)
    yb = jnp.concatenate([yb_ref[i] for i in range(N_GROUPS)], axis=1)
    ga = jnp.concatenate([ga_ref[i] for i in range(8)], axis=1).astype(F32)
    gb = jnp.concatenate([gb_ref[i] for i in range(8)], axis=1).astype(F32)
    merged = jax.nn.sigmoid(ga) * _dot(ya, wa_ref[...]) + jax.nn.sigmoid(gb) * _dot(yb, wb_ref[...])
    out = _dot(merged.astype(BF16), wo_ref[...])
    x1_ref[...] = x_ref[...] + mod_ref[0, 2:3, :] * out


def _mix(x2d, mod3, mod_row_fn, ya, yb, p3, wa, wb, wo):
    n = x2d.shape[0]
    return pl.pallas_call(
        _mix_kernel,
        out_shape=jax.ShapeDtypeStruct((n, D_MODEL), F32),
        grid=(n // MIX_TM,),
        in_specs=[
            pl.BlockSpec((MIX_TM, D_MODEL), lambda i: (i, 0)),
            pl.BlockSpec((1, 6, D_MODEL), lambda i: (mod_row_fn(i * MIX_TM), 0, 0)),
            pl.BlockSpec((N_HEADS, MIX_TM, LANES), lambda i: (0, i, 0)),
            pl.BlockSpec((N_GROUPS, MIX_TM, LANES), lambda i: (0, i, 0)),
            pl.BlockSpec((8, MIX_TM, LANES), lambda i: (SLAB_GA // 8, i, 0)),
            pl.BlockSpec((8, MIX_TM, LANES), lambda i: (SLAB_GB // 8, i, 0)),
            _const_spec((D_MODEL, D_MODEL)),
            _const_spec((N_GROUPS * LANES, D_MODEL)),
            _const_spec((D_MODEL, D_MODEL)),
        ],
        out_specs=pl.BlockSpec((MIX_TM, D_MODEL), lambda i: (i, 0)),
        compiler_params=_cparams(("parallel",)),
        name="mix",
    )(x2d, mod3, ya, yb, p3, p3, wa, wb, wo)


MOE_TB = 1024
MOE_TILE = 160
MOE_NSUB = 2
MOE_HALF = 8
ROUTER_ROWS = 4 + N_EXPERTS


def _route_rows(lt):
    lg = [lt[i:i + 1] for i in range(N_GROUPS)]
    mx = jnp.maximum(jnp.maximum(lg[0], lg[1]), jnp.maximum(lg[2], lg[3]))
    p_top = 1.0 / (jnp.exp(lg[0] - mx) + jnp.exp(lg[1] - mx) + jnp.exp(lg[2] - mx) + jnp.exp(lg[3] - mx))
    gidx = jnp.zeros(lg[0].shape, jnp.int32)
    best = lg[0]
    for i in range(1, N_GROUPS):
        upd = lg[i] > best
        gidx = jnp.where(upd, i, gidx)
        best = jnp.where(upd, lg[i], best)

    def expert_row(g, e):
        r = N_GROUPS + g * EXPERTS_PER_GROUP + e
        return lt[r:r + 1]

    le = [jnp.where(gidx == 0, expert_row(0, e),
                    jnp.where(gidx == 1, expert_row(1, e),
                              jnp.where(gidx == 2, expert_row(2, e), expert_row(3, e))))
          for e in range(EXPERTS_PER_GROUP)]
    v1 = jnp.maximum(jnp.maximum(le[0], le[1]), jnp.maximum(le[2], le[3]))
    i1 = jnp.full(v1.shape, EXPERTS_PER_GROUP - 1, jnp.int32)
    for e in range(EXPERTS_PER_GROUP - 2, -1, -1):
        i1 = jnp.where(le[e] == v1, e, i1)
    le2 = [jnp.where(i1 == e, -jnp.inf, le[e]) for e in range(EXPERTS_PER_GROUP)]
    v2 = jnp.maximum(jnp.maximum(le2[0], le2[1]), jnp.maximum(le2[2], le2[3]))
    i2 = jnp.full(v2.shape, EXPERTS_PER_GROUP - 1, jnp.int32)
    for e in range(EXPERTS_PER_GROUP - 2, -1, -1):
        i2 = jnp.where(le2[e] == v2, e, i2)
    e2 = jnp.exp(v2 - v1)
    w1 = p_top / (1.0 + e2)
    w2 = p_top * e2 / (1.0 + e2)
    w_rows, sel_rows = [], []
    for g in range(N_GROUPS):
        for e in range(EXPERTS_PER_GROUP):
            first = jnp.where(gidx == g, jnp.where(i1 == e, 1.0, 0.0), 0.0)
            second = jnp.where(gidx == g, jnp.where(i2 == e, 1.0, 0.0), 0.0)
            w_rows.append(first * w1 + second * w2)
            sel_rows.append(first + second)
    return w_rows, sel_rows


def _moe_kernel(x1_ref, mod_a_ref, mod_b_ref, g2_ref, wr_hi_ref, wr_lo_ref, br_ref, before_ref,
                wg_ref, wu_ref, wd_ref, gf_ref, y_ref, h2_scr, rank_scr, w_scr, p_scr, y_scr, npass_scr):
    e = pl.program_id(1)
    mod_refs = (mod_a_ref, mod_b_ref)
    tn_dims = (((0,), (0,)), ((), ()))
    slot = lax.broadcasted_iota(jnp.int32, (MOE_TILE, MOE_TB), 0).astype(F32)

    @pl.when(e == 0)
    def _():
        for sb in range(MOE_NSUB):
            x1 = x1_ref[sb * MOE_TB:(sb + 1) * MOE_TB, :]
            mod_ref = mod_refs[sb]
            h2 = x1 * _rms_scale(x1) * g2_ref[...]
            h2 = h2 * (1.0 + mod_ref[0, 4:5, :]) + mod_ref[0, 3:4, :]
            h_hi, h_lo = _split_bf16(h2)
            h2_scr[sb] = h_hi
            wr_hi = wr_hi_ref[...]
            logits = _dot(h_hi, wr_hi) + _dot(h_lo, wr_hi) + _dot(h_hi, wr_lo_ref[...]) + br_ref[...]
            w_rows, sel_rows = _route_rows(logits.T)
            sel = jnp.concatenate(sel_rows, axis=0)
            rank = _dot(sel.astype(BF16), before_ref[...])
            rank = jnp.where(sel > 0.0, rank, -1.0)
            n_routed = _dot(sel.astype(BF16), jnp.ones((MOE_TB, LANES), BF16))
            for i in range(N_EXPERTS):
                rank_scr[sb * N_EXPERTS + i] = rank[i:i + 1]
                w_scr[sb * N_EXPERTS + i] = w_rows[i]
                n_i = n_routed[i, 0].astype(jnp.int32)
                passes = jnp.int32(0)
                for kk in range(-(-MOE_TB // MOE_TILE)):
                    passes = passes + (n_i > kk * MOE_TILE).astype(jnp.int32)
                npass_scr[sb * N_EXPERTS + i] = passes
        y_ref[...] = jnp.zeros_like(y_ref)

    half = lax.rem(e, MOE_HALF)

    def gather_rows(sb, first_slot):
        rank_row = rank_scr[sb * N_EXPERTS + e]
        hit = (slot + first_slot) == rank_row
        pb = jnp.where(hit, 1.0, 0.0).astype(BF16)
        w_slot = jnp.sum(jnp.where(hit, w_scr[sb * N_EXPERTS + e], 0.0), axis=1, keepdims=True)
        return pb, w_slot, _dot(pb, h2_scr[sb]).astype(BF16)

    def expert(xg, w_slot):
        a = _dot(xg, wg_ref[0])
        u = _dot(xg, wu_ref[0])
        act = (a * jax.nn.sigmoid(a) * u * w_slot).astype(BF16)
        return _dot(act, wd_ref[0]).astype(BF16)

    first = [gather_rows(sb, 0.0) for sb in range(MOE_NSUB)]
    yv = expert(jnp.concatenate([f[2] for f in first], axis=0), jnp.concatenate([f[1] for f in first], axis=0))
    for sb in range(MOE_NSUB):
        y_scr[sb, pl.ds(pl.multiple_of(half * MOE_TILE, MOE_TILE), MOE_TILE), :] = \
            yv[sb * MOE_TILE:(sb + 1) * MOE_TILE]

    def extra_pass(j, carry):
        for sb in range(MOE_NSUB):
            pb, w_slot, xg = gather_rows(sb, (j * MOE_TILE).astype(F32))
            y_ref[sb * MOE_TB:(sb + 1) * MOE_TB, :] += lax.dot_general(
                pb, expert(xg, w_slot), tn_dims, preferred_element_type=F32)
        return carry

    n_pass = npass_scr[e]
    for sb in range(1, MOE_NSUB):
        n_pass = jnp.maximum(n_pass, npass_scr[sb * N_EXPERTS + e])
    lax.fori_loop(1, n_pass, extra_pass, 0)

    @pl.when(half == MOE_HALF - 1)
    def _():
        for sb in range(MOE_NSUB):
            for i in range(MOE_HALF):
                rank_row = rank_scr[sb * N_EXPERTS + (e - (MOE_HALF - 1)) + i]
                p_scr[i * MOE_TILE:(i + 1) * MOE_TILE, :] = jnp.where(slot == rank_row, 1.0, 0.0).astype(BF16)
            sub_rows = slice(sb * MOE_TB, (sb + 1) * MOE_TB)
            y_ref[sub_rows, :] += lax.dot_general(p_scr[...], y_scr[sb], tn_dims, preferred_element_type=F32)

    @pl.when(e == N_EXPERTS - 1)
    def _():
        for sb in range(MOE_NSUB):
            sub_rows = slice(sb * MOE_TB, (sb + 1) * MOE_TB)
            x2 = x1_ref[sub_rows, :] + mod_refs[sb][0, 5:6, :] * y_ref[sub_rows, :]
            y_ref[sub_rows, :] = x2 * _rms_scale(x2) * gf_ref[...]


def _moe(x1, mod3, mod_row_fn, g_norm2, wr_hi, wr_lo, br, wg, wu, wd, g_final):
    n = x1.shape[0]
    rows = MOE_NSUB * MOE_TB
    t_i = lax.broadcasted_iota(jnp.int32, (MOE_TB, MOE_TB), 0)
    t_j = lax.broadcasted_iota(jnp.int32, (MOE_TB, MOE_TB), 1)
    before = (t_i < t_j).astype(BF16)
    return pl.pallas_call(
        _moe_kernel,
        out_shape=jax.ShapeDtypeStruct((n, D_MODEL), F32),
        grid=(n // rows, N_EXPERTS),
        in_specs=[
            pl.BlockSpec((rows, D_MODEL), lambda i, e: (i, 0), pipeline_mode=pl.Buffered(1)),
            pl.BlockSpec((1, 6, D_MODEL), lambda i, e: (mod_row_fn(i * rows), 0, 0)),
            pl.BlockSpec((1, 6, D_MODEL), lambda i, e: (mod_row_fn(i * rows + MOE_TB), 0, 0)),
            _const_spec((1, D_MODEL)),
            _const_spec((D_MODEL, LANES)),
            _const_spec((D_MODEL, LANES)),
            _const_spec((1, LANES)),
            _const_spec((MOE_TB, MOE_TB)),
            pl.BlockSpec((1, D_MODEL, D_EXPERT), lambda i, e: (e, 0, 0)),
            pl.BlockSpec((1, D_MODEL, D_EXPERT), lambda i, e: (e, 0, 0)),
            pl.BlockSpec((1, D_EXPERT, D_MODEL), lambda i, e: (e, 0, 0)),
            _const_spec((1, D_MODEL)),
        ],
        out_specs=pl.BlockSpec((rows, D_MODEL), lambda i, e: (i, 0)),
        scratch_shapes=[pltpu.VMEM((MOE_NSUB, MOE_TB, D_MODEL), BF16),
                        pltpu.VMEM((MOE_NSUB * N_EXPERTS, 1, MOE_TB), F32),
                        pltpu.VMEM((MOE_NSUB * N_EXPERTS, 1, MOE_TB), F32),
                        pltpu.VMEM((MOE_HALF * MOE_TILE, MOE_TB), BF16),
                        pltpu.VMEM((MOE_NSUB, MOE_HALF * MOE_TILE, D_MODEL), BF16),
                        pltpu.SMEM((MOE_NSUB * N_EXPERTS,), jnp.int32)],
        compiler_params=_cparams(("parallel", "arbitrary")),
        name="moe",
    )(x1, mod3, mod3, g_norm2, wr_hi, wr_lo, br, before, wg, wu, wd, g_final)


def _stream(x, mod3, mod_row_fn, seq_len, rows_2d, state, w):
    bsz, t, _ = x.shape
    n = bsz * t
    x2d = x.reshape(n, D_MODEL)
    p3, cols = _in_projection(x2d, mod3, mod_row_fn, w["g_norm1"], w["wmain"], w["bmain"], w["colscale"],
                              w["wg_hi"], w["wg_lo"], w["bg"])
    rows_f = (-cols[:, N_HEADS:2 * N_HEADS]).T.reshape(N_HEADS, 1, n)
    rows_b = (-cols[:, 3 * N_HEADS:4 * N_HEADS]).T.reshape(N_HEADS, 1, n)
    ml = _mlstm(p3, cols, rows_f, rows_b, w["gn3"], state)
    amat, invc = _pool_constants(seq_len, rows_2d)
    n_seq = max(1, 1024 // seq_len)
    yb = _pool(p3, amat, invc, w["wpg"], w["pscale"], seq_len, n_seq)
    ya = ml if state is not None else ml[0]
    x1 = _mix(x2d, mod3, mod_row_fn, ya, yb, p3, w["wa"], w["wb"], w["wo"])
    y = _moe(x1, mod3, mod_row_fn, w["g_norm2"], w["wr_hi"], w["wr_lo"], w["br"],
             w["wg"], w["wu"], w["wd"], w["g_final"])
    return y.reshape(bsz, t, D_MODEL), ml


def kernel(x_prompt, x_sample, state_C, state_n, state_m, c, c_ctx, w_mod, b_mod, g_norm1, w_in, b_in, gn_gain, w_pool_grp, pool_scale, w_proj_a, w_proj_b, w_out, g_norm2, w_router_grp, b_router_grp, w_router_exp, b_router_exp, w_exp_gate, w_exp_up, w_exp_down, g_final):
    bp, tp, _ = x_prompt.shape
    bs, ts, _ = x_sample.shape
    assert w_mod.shape[0] == 1, "single trunk layer"
    assert tp == CHUNK and ts % CHUNK == 0 and ts == ML_ROWS

    cvec = jnp.zeros((8, D_MODEL), F32).at[0].set(c_ctx).at[1:1 + bs].set(c)
    mod3 = _modulation(cvec, w_mod[0], b_mod[0])

    wi, bi = w_in[0], b_in[0]
    d = D_MODEL
    o_q, o_k, o_v, o_o, o_g, o_p, o_ga, o_gb = 0, d, 2 * d, 3 * d, 4 * d, 4 * d + 32, 4 * d + 32 + 512, 4 * d + 32 + 512 + d
    order = [(o_q, d), (o_k, d), (o_v, d), (o_o, d), (o_ga, d), (o_gb, d), (o_p, 512)]
    wmain = jnp.concatenate([wi[:, s:s + l] for s, l in order], axis=1).astype(BF16)
    bmain = jnp.concatenate([bi[s:s + l] for s, l in order]).reshape(1, D_MAIN)
    colscale = jnp.ones((D_MAIN,), F32).at[SLAB_K * LANES:(SLAB_K + N_HEADS) * LANES].set(HEAD_DIM ** -0.5)
    colscale = colscale.reshape(1, D_MAIN)
    gw = wi[:, o_g:o_g + 32]
    gb = bi[o_g:o_g + 32]
    i_f, f_f, i_b, f_b = (slice(0, 8), slice(8, 16), slice(16, 24), slice(24, 32))
    zw = jnp.zeros((d, 8), F32)
    zb = jnp.zeros((8,), F32)
    wg2 = jnp.concatenate([gw[:, f_f], gw[:, f_f], gw[:, f_b], gw[:, f_b], jnp.zeros((d, LANES - 32), F32),
                           zw, gw[:, i_f], zw, gw[:, i_b], jnp.zeros((d, LANES - 32), F32)], axis=1)
    bg2 = jnp.concatenate([gb[f_f], gb[f_f], gb[f_b], gb[f_b], jnp.zeros((LANES - 32,), F32),
                           zb, gb[i_f], zb, gb[i_b], jnp.zeros((LANES - 32,), F32)]).reshape(1, 2 * LANES)
    wg_hi, wg_lo = _split_bf16(wg2)

    wr = jnp.concatenate([w_router_grp[0], w_router_exp[0].reshape(d, N_EXPERTS),
                           jnp.zeros((d, LANES - ROUTER_ROWS), F32)], axis=1)
    br = jnp.concatenate([b_router_grp[0], b_router_exp[0].reshape(N_EXPERTS),
                           jnp.zeros((LANES - ROUTER_ROWS,), F32)]).reshape(1, LANES)
    wr_hi, wr_lo = _split_bf16(wr)

    w = dict(
        g_norm1=g_norm1[0].reshape(1, d), wmain=wmain, bmain=bmain, colscale=colscale,
        wg_hi=wg_hi, wg_lo=wg_lo, bg=bg2,
        gn3=gn_gain[0].reshape(N_HEADS, 1, HEAD_DIM),
        wpg=w_pool_grp[0].astype(BF16), pscale=pool_scale[0].reshape(N_GROUPS, 1, LANES),
        wa=w_proj_a[0].astype(BF16), wb=w_proj_b[0].astype(BF16), wo=w_out[0].astype(BF16),
        g_norm2=g_norm2[0].reshape(1, d), wr_hi=wr_hi, wr_lo=wr_lo, br=br,
        wg=w_exp_gate[0].astype(BF16), wu=w_exp_up[0].astype(BF16), wd=w_exp_down[0].astype(BF16),
        g_final=g_final.reshape(1, d),
    )

    y_prompt, (_, cst, nst, mst) = _stream(x_prompt, mod3, lambda r: 0, tp, None, None, w)
    n_rep = jnp.broadcast_to(state_n[..., None], state_C.shape)
    state = (state_C, n_rep, state_m[:, 0].reshape(bs, 2, N_HEADS, 1, 1))
    y_sample, _ = _stream(x_sample, mod3, lambda r: 1 + r // ts, ts, ts // GRID_W, state, w)

    new_c = cst
    new_n = nst.reshape(bp, 1, 2, N_HEADS, HEAD_DIM)
    new_m = mst.reshape(bp, 1, 2, N_HEADS)
    return (y_prompt, y_sample, new_c, new_n, new_m)
```

```python
import functools

import numpy as np
import jax
import jax.numpy as jnp
from jax import lax
from jax.experimental import pallas as pl
from jax.experimental.pallas import tpu as pltpu

F32 = jnp.float32
BF16 = jnp.bfloat16

D_MODEL = 1024
N_HEADS = 8
HEAD_DIM = 128
GRID_W = 64
POOL_WINDOWS = (2, 4, 8, 16)
N_GROUPS = 4
EXPERTS_PER_GROUP = 4
N_EXPERTS = 16
D_EXPERT = 512
EPS = 1e-6
NEG = -1e30

LANES = 128
CHUNK = 256
N_SLABS = 52
SLAB_Q, SLAB_K, SLAB_V, SLAB_O, SLAB_GA, SLAB_GB, SLAB_POOL = 0, 8, 16, 24, 32, 40, 48
D_MAIN = N_SLABS * LANES
VMEM_LIMIT = 56 * 1024 * 1024


def _cparams(sem):
    return pltpu.CompilerParams(dimension_semantics=sem, vmem_limit_bytes=VMEM_LIMIT)


def _const_spec(shape):
    nd = len(shape)
    return pl.BlockSpec(shape, lambda *_: (0,) * nd, pipeline_mode=pl.Buffered(1))


def _split_bf16(x):
    hi = x.astype(BF16)
    lo = (x - hi.astype(F32)).astype(BF16)
    return hi, lo


def _dot(a, b):
    return jnp.dot(a, b, preferred_element_type=F32)


def _dot3(a, b_hi, b_lo):
    a_hi, a_lo = _split_bf16(a)
    return _dot(a_hi, b_hi) + _dot(a_hi, b_lo) + _dot(a_lo, b_hi)


def _rms_scale(x):
    return lax.rsqrt(jnp.mean(x * x, axis=-1, keepdims=True) + EPS)


def _log_sigmoid(x):
    return jnp.minimum(x, 0.0) - jnp.log1p(jnp.exp(-jnp.abs(x)))


def _mod_kernel(c_ref, w_ref, b_ref, o_ref):
    c = c_ref[...]
    s = c * jax.nn.sigmoid(c)
    w_hi, w_lo = _split_bf16(w_ref[...])
    o_ref[...] = _dot3(s, w_hi, w_lo) + b_ref[...]


def _modulation(cvec, w_mod, b_mod):
    tn = 512
    out = pl.pallas_call(
        _mod_kernel,
        out_shape=jax.ShapeDtypeStruct((8, 6 * D_MODEL), F32),
        grid=(6 * D_MODEL // tn,),
        in_specs=[
            pl.BlockSpec((8, D_MODEL), lambda j: (0, 0)),
            pl.BlockSpec((D_MODEL, tn), lambda j: (0, j)),
            pl.BlockSpec((1, tn), lambda j: (0, j)),
        ],
        out_specs=pl.BlockSpec((8, tn), lambda j: (0, j)),
        compiler_params=_cparams(("parallel",)),
        name="modulation",
    )(cvec, w_mod, b_mod.reshape(1, -1))
    return out.reshape(8, 6, D_MODEL)


PACK_ROWS = 256


def _pack_kernel(w_ref, o_ref, *, order):
    dst = 0
    for src, width in order:
        o_ref[:, dst:dst + width] = w_ref[:, src:src + width].astype(BF16)
        dst += width


def _pack_columns(w, order):
    k, n = w.shape
    n_out = sum(width for _, width in order)
    return pl.pallas_call(
        functools.partial(_pack_kernel, order=tuple(order)),
        out_shape=jax.ShapeDtypeStruct((k, n_out), BF16),
        grid=(k // PACK_ROWS,),
        in_specs=[pl.BlockSpec((PACK_ROWS, n), lambda i: (i, 0))],
        out_specs=pl.BlockSpec((PACK_ROWS, n_out), lambda i: (i, 0)),
        compiler_params=_cparams(("parallel",)),
        name="pack_columns",
    )(w)


IN_TM = 512
IN_TN = 512
CMAX_LANE_SHIFT = 32
LOG2E = 1.4426950408889634


def _inproj_kernel(x_ref, mod_ref, g_ref, w_ref, b_ref, cs_ref, wg_hi_ref, wg_lo_ref, bg_ref,
                   p3_ref, cols_ref):
    x = x_ref[...]
    h = x * _rms_scale(x) * g_ref[...]
    h = h * (1.0 + mod_ref[0, 1:2, :]) + mod_ref[0, 0:1, :]
    hb = h.astype(BF16)
    for n in range(D_MAIN // IN_TN):
        sl = slice(n * IN_TN, (n + 1) * IN_TN)
        acc = (_dot(hb, w_ref[:, sl]) + b_ref[:, sl]) * cs_ref[:, sl]
        accb = acc.astype(BF16)
        for j in range(IN_TN // LANES):
            p3_ref[n * (IN_TN // LANES) + j] = accb[:, j * LANES:(j + 1) * LANES]

    g2 = _dot3(h, wg_hi_ref[...], wg_lo_ref[...]) + bg_ref[...]
    ls = _log_sigmoid(g2[:, :LANES])
    gi = g2[:, LANES:]
    row = lax.broadcasted_iota(jnp.int32, (CHUNK, LANES), 0)
    lane = lax.broadcasted_iota(jnp.int32, (CHUNK, LANES), 1)
    for c in range(IN_TM // CHUNK):
        x0 = ls[c * CHUNK:(c + 1) * CHUNK]
        cf = x0
        cb = x0
        s = 1
        while s < CHUNK:
            cf = cf + jnp.where(row >= s, pltpu.roll(cf, s, 0), 0.0)
            cb = cb + jnp.where(row < CHUNK - s, pltpu.roll(cb, CHUNK - s, 0), 0.0)
            s *= 2
        b = jnp.where(lane < 2 * N_HEADS, cf, cb)
        is_imb = (lane & N_HEADS) != 0
        out = jnp.where(is_imb, gi[c * CHUNK:(c + 1) * CHUNK] - b, b)
        mf = out
        mb = out
        s = 1
        while s < CHUNK:
            mf = jnp.maximum(mf, jnp.where(row >= s, pltpu.roll(mf, s, 0), -jnp.inf))
            mb = jnp.maximum(mb, jnp.where(row < CHUNK - s, pltpu.roll(mb, CHUNK - s, 0), -jnp.inf))
            s *= 2
        cmax = pltpu.roll(jnp.where(lane < 2 * N_HEADS, mf, mb), CMAX_LANE_SHIFT, 1)
        cols_ref[c * CHUNK:(c + 1) * CHUNK, :] = jnp.where(lane < CMAX_LANE_SHIFT, out, cmax)


def _in_projection(x2d, mod3, mod_row_fn, g_norm1, wmain, bmain, colscale, wg_hi, wg_lo, bg):
    n = x2d.shape[0]
    return pl.pallas_call(
        _inproj_kernel,
        out_shape=(jax.ShapeDtypeStruct((N_SLABS, n, LANES), BF16),
                   jax.ShapeDtypeStruct((n, LANES), F32)),
        grid=(n // IN_TM,),
        in_specs=[
            pl.BlockSpec((IN_TM, D_MODEL), lambda i: (i, 0)),
            pl.BlockSpec((1, 6, D_MODEL), lambda i: (mod_row_fn(i * IN_TM), 0, 0)),
            _const_spec((1, D_MODEL)),
            _const_spec((D_MODEL, D_MAIN)),
            _const_spec((1, D_MAIN)),
            _const_spec((1, D_MAIN)),
            _const_spec((D_MODEL, 2 * LANES)),
            _const_spec((D_MODEL, 2 * LANES)),
            _const_spec((1, 2 * LANES)),
        ],
        out_specs=(pl.BlockSpec((N_SLABS, IN_TM, LANES), lambda i: (0, i, 0)),
                   pl.BlockSpec((IN_TM, LANES), lambda i: (i, 0))),
        compiler_params=_cparams(("parallel",)),
        name="in_projection",
    )(x2d, mod3, g_norm1, wmain, bmain, colscale, wg_hi, wg_lo, bg)


ML_ROWS = 1024
ML_NCH = ML_ROWS // CHUNK


def _pick_col(blk, lane_iota, idx):
    return jnp.sum(jnp.where(lane_iota == idx, blk, 0.0), axis=1, keepdims=True)


def _mlstm_kernel(*refs, carry):
    if carry:
        (q_ref, k_ref, v_ref, o_ref, cols_ref, rf_ref, rb_ref, gn_ref, c0_ref, n0_ref, m0_ref,
         ya_ref, s_scr, h_scr) = refs
    else:
        (q_ref, k_ref, v_ref, o_ref, cols_ref, rf_ref, rb_ref, gn_ref,
         ya_ref, cst_ref, nst_ref, mst_ref, s_scr, h_scr) = refs
    head = pl.program_id(1)
    lane = lax.broadcasted_iota(jnp.int32, (CHUNK, LANES), 1)
    r_idx = lax.broadcasted_iota(jnp.int32, (CHUNK, CHUNK), 0)
    c_idx = lax.broadcasted_iota(jnp.int32, (CHUNK, CHUNK), 1)
    ones = jnp.ones((CHUNK, HEAD_DIM), BF16)
    tn_dims = (((0,), (0,)), ((), ()))

    for d in range(2):
        mask = (c_idx <= r_idx) if d == 0 else (c_idx >= r_idx)
        row_ref = rf_ref if d == 0 else rb_ref
        last = CHUNK - 1 if d == 0 else 0
        base = head + 2 * N_HEADS * d
        if carry:
            cn = jnp.concatenate([c0_ref[0, 0, d, 0], n0_ref[0, 0, d, 0]], axis=1)
            m = m0_ref[0, d, 0]
        order = range(ML_NCH) if d == 0 else range(ML_NCH - 1, -1, -1)
        for step, c in enumerate(order):
            rows = slice(c * CHUNK, (c + 1) * CHUNK)
            q = q_ref[0, rows, :]
            k = k_ref[0, rows, :]
            v = v_ref[0, rows, :]
            v1 = jnp.concatenate([v, ones], axis=1)
            if d == 0:
                s = lax.dot_general(q, k, (((1,), (1,)), ((), ())), preferred_element_type=F32)
                s_scr[rows, :] = s
            else:
                s = s_scr[rows, :]
            blk = cols_ref[rows, :]
            bcol = _pick_col(blk, lane, base)
            imbcol = _pick_col(blk, lane, base + N_HEADS)
            a = bcol + _pick_col(blk, lane, base + N_HEADS + CMAX_LANE_SHIFT)
            bmirow = row_ref[0, :, rows]
            inter = bcol + (m if carry else NEG)
            m_t = jnp.maximum(inter, a)
            arg = (bcol - m_t) * LOG2E - bmirow * LOG2E
            sw = s * jnp.exp2(jnp.where(mask, arg, NEG))
            nd = _dot(sw.astype(BF16), v1)
            if carry:
                nd = nd + jnp.exp(inter - m_t) * _dot(q, cn.astype(BF16))
            hdir = nd[:, :HEAD_DIM] / jnp.maximum(jnp.abs(nd[:, HEAD_DIM:]), jnp.exp(-m_t))
            if d == 0:
                h_scr[rows, :] = hdir
            else:
                h_scr[rows, :] = h_scr[rows, :] + hdir

            if carry and step == ML_NCH - 1:
                continue
            m_new = m_t[last:last + 1, :]
            b_last = bcol[last:last + 1, :]
            w_col = jnp.exp(b_last - m_new + imbcol)
            if carry:
                u = lax.dot_general(k, (w_col * v1.astype(F32)).astype(BF16), tn_dims,
                                    preferred_element_type=F32)
                cn = jnp.exp(b_last + m - m_new) * cn + u
                m = m_new
            else:
                u = lax.dot_general(k, (w_col * v.astype(F32)).astype(BF16), tn_dims,
                                    preferred_element_type=F32)
                w_row = jnp.exp(b_last - m_new - bmirow)
                w8 = jnp.broadcast_to(w_row, (8, CHUNK)).astype(BF16)
                cst_ref[c, 0, d, 0] = u
                nst_ref[c, d, 0] = _dot(w8, k)[0:1, :]
                mst_ref[c, d, 0] = m_new

    hh = h_scr[...]
    hn = hh * _rms_scale(hh) * gn_ref[0]
    ya_ref[0] = (hn * jax.nn.sigmoid(o_ref[0].astype(F32))).astype(BF16)


def _mlstm(p3, cols, rows_f, rows_b, gn3, state=None):
    n = p3.shape[1]
    nblk = n // ML_ROWS
    carry = state is not None

    def slab(base):
        return pl.BlockSpec((1, ML_ROWS, LANES), lambda g, h: (base + h, g, 0))

    in_specs = [
        slab(SLAB_Q), slab(SLAB_K), slab(SLAB_V), slab(SLAB_O),
        pl.BlockSpec((ML_ROWS, LANES), lambda g, h: (g, 0)),
        pl.BlockSpec((1, 1, ML_ROWS), lambda g, h: (h, 0, g)),
        pl.BlockSpec((1, 1, ML_ROWS), lambda g, h: (h, 0, g)),
        pl.BlockSpec((1, 1, LANES), lambda g, h: (h, 0, 0)),
    ]
    args = [p3, p3, p3, p3, cols, rows_f, rows_b, gn3]
    ya_shape = jax.ShapeDtypeStruct((N_HEADS, n, LANES), BF16)
    ya_spec = pl.BlockSpec((1, ML_ROWS, LANES), lambda g, h: (h, g, 0))
    if carry:
        c0, n0, m0 = state
        in_specs += [
            pl.BlockSpec((1, 1, 2, 1, HEAD_DIM, HEAD_DIM), lambda g, h: (g, 0, 0, h, 0, 0)),
            pl.BlockSpec((1, 1, 2, 1, HEAD_DIM, HEAD_DIM), lambda g, h: (g, 0, 0, h, 0, 0)),
            pl.BlockSpec((1, 2, 1, 1, 1), lambda g, h: (g, 0, h, 0, 0)),
        ]
        args += [c0, n0, m0]
        out_shape = ya_shape
        out_specs = ya_spec
    else:
        nseq = n // CHUNK
        out_shape = (ya_shape,
                     jax.ShapeDtypeStruct((nseq, 1, 2, N_HEADS, HEAD_DIM, HEAD_DIM), F32),
                     jax.ShapeDtypeStruct((nseq, 2, N_HEADS, 1, HEAD_DIM), F32),
                     jax.ShapeDtypeStruct((nseq, 2, N_HEADS, 1, 1), F32))
        out_specs = (ya_spec,
                     pl.BlockSpec((ML_NCH, 1, 2, 1, HEAD_DIM, HEAD_DIM), lambda g, h: (g, 0, 0, h, 0, 0)),
                     pl.BlockSpec((ML_NCH, 2, 1, 1, HEAD_DIM), lambda g, h: (g, 0, h, 0, 0)),
                     pl.BlockSpec((ML_NCH, 2, 1, 1, 1), lambda g, h: (g, 0, h, 0, 0)))
    return pl.pallas_call(
        functools.partial(_mlstm_kernel, carry=carry),
        out_shape=out_shape,
        grid=(nblk, N_HEADS),
        in_specs=in_specs,
        out_specs=out_specs,
        scratch_shapes=[pltpu.VMEM((ML_ROWS, CHUNK), F32), pltpu.VMEM((ML_ROWS, HEAD_DIM), F32)],
        compiler_params=_cparams(("parallel", "parallel")),
        name="mlstm_latent" if carry else "mlstm_prompt",
    )(*args)


def _pool_kernel(u_ref, a_ref, ic_ref, wpg_ref, ps_ref, yb_ref, *, seq_len, n_seq):
    for sq in range(n_seq):
        rows = slice(sq * seq_len, (sq + 1) * seq_len)
        for g in range(N_GROUPS):
            u = u_ref[g, rows, :]
            pooled = _dot(a_ref[g], u) * ic_ref[g]
            p = (pooled - u.astype(F32)).astype(BF16)
            yb_ref[g, rows, :] = (_dot(p, wpg_ref[g]) * ps_ref[g]).astype(BF16)


def _pool(p3, amat, invc, wpg, pscale, seq_len, n_seq):
    n = p3.shape[1]
    rows = seq_len * n_seq
    return pl.pallas_call(
        functools.partial(_pool_kernel, seq_len=seq_len, n_seq=n_seq),
        out_shape=jax.ShapeDtypeStruct((N_GROUPS, n, LANES), BF16),
        grid=(n // rows,),
        in_specs=[
            pl.BlockSpec((N_GROUPS, rows, LANES), lambda i: (SLAB_POOL // N_GROUPS, i, 0)),
            _const_spec((N_GROUPS, seq_len, seq_len)),
            _const_spec((N_GROUPS, seq_len, LANES)),
            _const_spec((N_GROUPS, LANES, LANES)),
            _const_spec((N_GROUPS, 1, LANES)),
        ],
        out_specs=pl.BlockSpec((N_GROUPS, rows, LANES), lambda i: (0, i, 0)),
        compiler_params=_cparams(("parallel",)),
        name="pool",
    )(p3, amat, invc, wpg, pscale)


def _window_matrix(size, w):
    idx = np.arange(size)
    lo = np.clip(idx - w // 2, 0, size)
    hi = np.clip(idx - w // 2 + w, 0, size)
    s = np.arange(size)[None, :]
    a = ((s >= lo[:, None]) & (s < hi[:, None])).astype(np.float32)
    return a, (hi - lo).astype(np.float32)


def _pool_constants(seq_len, rows_2d):
    mats, invs = [], []
    for w in POOL_WINDOWS:
        if rows_2d is None:
            a, cnt = _window_matrix(seq_len, w)
        else:
            ar, cr = _window_matrix(rows_2d, w)
            ac, cc = _window_matrix(GRID_W, w)
            a = np.kron(ar, ac)
            cnt = np.kron(cr, cc)
        mats.append(a)
        invs.append(np.repeat((1.0 / cnt)[:, None], LANES, axis=1))
    return (jnp.asarray(np.stack(mats), dtype=BF16), jnp.asarray(np.stack(invs), dtype=F32))


MIX_TM = 512


def _mix_kernel(x_ref, mod_ref, ya_ref, yb_ref, ga_ref, gb_ref, wa_ref, wb_ref, wo_ref, x1_ref):
    ya = jnp.concatenate([ya_ref[i] for i in range(N_HEADS)], axis=1)
    yb = jnp.concatenate([yb_ref[i] for i in range(N_GROUPS)], axis=1)
    ga = jnp.concatenate([ga_ref[i] for i in range(8)], axis=1).astype(F32)
    gb = jnp.concatenate([gb_ref[i] for i in range(8)], axis=1).astype(F32)
    merged = jax.nn.sigmoid(ga) * _dot(ya, wa_ref[...]) + jax.nn.sigmoid(gb) * _dot(yb, wb_ref[...])
    out = _dot(merged.astype(BF16), wo_ref[...])
    x1_ref[...] = x_ref[...] + mod_ref[0, 2:3, :] * out


def _mix(x2d, mod3, mod_row_fn, ya, yb, p3, wa, wb, wo):
    n = x2d.shape[0]
    return pl.pallas_call(
        _mix_kernel,
        out_shape=jax.ShapeDtypeStruct((n, D_MODEL), F32),
        grid=(n // MIX_TM,),
        in_specs=[
            pl.BlockSpec((MIX_TM, D_MODEL), lambda i: (i, 0)),
            pl.BlockSpec((1, 6, D_MODEL), lambda i: (mod_row_fn(i * MIX_TM), 0, 0)),
            pl.BlockSpec((N_HEADS, MIX_TM, LANES), lambda i: (0, i, 0)),
            pl.BlockSpec((N_GROUPS, MIX_TM, LANES), lambda i: (0, i, 0)),
            pl.BlockSpec((8, MIX_TM, LANES), lambda i: (SLAB_GA // 8, i, 0)),
            pl.BlockSpec((8, MIX_TM, LANES), lambda i: (SLAB_GB // 8, i, 0)),
            _const_spec((D_MODEL, D_MODEL)),
            _const_spec((N_GROUPS * LANES, D_MODEL)),
            _const_spec((D_MODEL, D_MODEL)),
        ],
        out_specs=pl.BlockSpec((MIX_TM, D_MODEL), lambda i: (i, 0)),
        compiler_params=_cparams(("parallel",)),
        name="mix",
    )(x2d, mod3, ya, yb, p3, p3, wa, wb, wo)


MOE_TB = 1024
MOE_TILE = 160
MOE_NSUB = 2
MOE_HALF = 8
ROUTER_ROWS = 4 + N_EXPERTS


def _route_rows(lt):
    lg = [lt[i:i + 1] for i in range(N_GROUPS)]
    mx = jnp.maximum(jnp.maximum(lg[0], lg[1]), jnp.maximum(lg[2], lg[3]))
    p_top = 1.0 / (jnp.exp(lg[0] - mx) + jnp.exp(lg[1] - mx) + jnp.exp(lg[2] - mx) + jnp.exp(lg[3] - mx))
    gidx = jnp.zeros(lg[0].shape, jnp.int32)
    best = lg[0]
    for i in range(1, N_GROUPS):
        upd = lg[i] > best
        gidx = jnp.where(upd, i, gidx)
        best = jnp.where(upd, lg[i], best)

    def expert_row(g, e):
        r = N_GROUPS + g * EXPERTS_PER_GROUP + e
        return lt[r:r + 1]

    le = [jnp.where(gidx == 0, expert_row(0, e),
                    jnp.where(gidx == 1, expert_row(1, e),
                              jnp.where(gidx == 2, expert_row(2, e), expert_row(3, e))))
          for e in range(EXPERTS_PER_GROUP)]
    v1 = jnp.maximum(jnp.maximum(le[0], le[1]), jnp.maximum(le[2], le[3]))
    i1 = jnp.full(v1.shape, EXPERTS_PER_GROUP - 1, jnp.int32)
    for e in range(EXPERTS_PER_GROUP - 2, -1, -1):
        i1 = jnp.where(le[e] == v1, e, i1)
    le2 = [jnp.where(i1 == e, -jnp.inf, le[e]) for e in range(EXPERTS_PER_GROUP)]
    v2 = jnp.maximum(jnp.maximum(le2[0], le2[1]), jnp.maximum(le2[2], le2[3]))
    i2 = jnp.full(v2.shape, EXPERTS_PER_GROUP - 1, jnp.int32)
    for e in range(EXPERTS_PER_GROUP - 2, -1, -1):
        i2 = jnp.where(le2[e] == v2, e, i2)
    e2 = jnp.exp(v2 - v1)
    w1 = p_top / (1.0 + e2)
    w2 = p_top * e2 / (1.0 + e2)
    w_rows, sel_rows = [], []
    for g in range(N_GROUPS):
        for e in range(EXPERTS_PER_GROUP):
            first = jnp.where(gidx == g, jnp.where(i1 == e, 1.0, 0.0), 0.0)
            second = jnp.where(gidx == g, jnp.where(i2 == e, 1.0, 0.0), 0.0)
            w_rows.append(first * w1 + second * w2)
            sel_rows.append(first + second)
    return w_rows, sel_rows


def _moe_kernel(x1_ref, mod_a_ref, mod_b_ref, g2_ref, wr_hi_ref, wr_lo_ref, br_ref, before_ref,
                wg_ref, wu_ref, wd_ref, gf_ref, y_ref, h2_scr, rank_scr, w_scr, p_scr, y_scr, npass_scr):
    e = pl.program_id(1)
    mod_refs = (mod_a_ref, mod_b_ref)
    tn_dims = (((0,), (0,)), ((), ()))
    slot = lax.broadcasted_iota(jnp.int32, (MOE_TILE, MOE_TB), 0).astype(F32)

    @pl.when(e == 0)
    def _():
        for sb in range(MOE_NSUB):
            x1 = x1_ref[sb * MOE_TB:(sb + 1) * MOE_TB, :]
            mod_ref = mod_refs[sb]
            h2 = x1 * _rms_scale(x1) * g2_ref[...]
            h2 = h2 * (1.0 + mod_ref[0, 4:5, :]) + mod_ref[0, 3:4, :]
            h_hi, h_lo = _split_bf16(h2)
            h2_scr[sb] = h_hi
            wr_hi = wr_hi_ref[...]
            logits = _dot(h_hi, wr_hi) + _dot(h_lo, wr_hi) + _dot(h_hi, wr_lo_ref[...]) + br_ref[...]
            w_rows, sel_rows = _route_rows(logits.T)
            sel = jnp.concatenate(sel_rows, axis=0)
            rank = _dot(sel.astype(BF16), before_ref[...])
            rank = jnp.where(sel > 0.0, rank, -1.0)
            n_routed = _dot(sel.astype(BF16), jnp.ones((MOE_TB, LANES), BF16))
            for i in range(N_EXPERTS):
                rank_scr[sb * N_EXPERTS + i] = rank[i:i + 1]
                w_scr[sb * N_EXPERTS + i] = w_rows[i]
                n_i = n_routed[i, 0].astype(jnp.int32)
                passes = jnp.int32(0)
                for kk in range(-(-MOE_TB // MOE_TILE)):
                    passes = passes + (n_i > kk * MOE_TILE).astype(jnp.int32)
                npass_scr[sb * N_EXPERTS + i] = passes
        y_ref[...] = jnp.zeros_like(y_ref)

    half = lax.rem(e, MOE_HALF)

    def gather_rows(sb, first_slot):
        rank_row = rank_scr[sb * N_EXPERTS + e]
        hit = (slot + first_slot) == rank_row
        pb = jnp.where(hit, 1.0, 0.0).astype(BF16)
        w_slot = jnp.sum(jnp.where(hit, w_scr[sb * N_EXPERTS + e], 0.0), axis=1, keepdims=True)
        return pb, w_slot, _dot(pb, h2_scr[sb]).astype(BF16)

    def expert(xg, w_slot):
        a = _dot(xg, wg_ref[0])
        u = _dot(xg, wu_ref[0])
        act = (a * jax.nn.sigmoid(a) * u * w_slot).astype(BF16)
        return _dot(act, wd_ref[0]).astype(BF16)

    first = [gather_rows(sb, 0.0) for sb in range(MOE_NSUB)]
    yv = expert(jnp.concatenate([f[2] for f in first], axis=0), jnp.concatenate([f[1] for f in first], axis=0))
    for sb in range(MOE_NSUB):
        y_scr[sb, pl.ds(pl.multiple_of(half * MOE_TILE, MOE_TILE), MOE_TILE), :] = \
            yv[sb * MOE_TILE:(sb + 1) * MOE_TILE]

    for sb in range(MOE_NSUB):
        def extra_pass(j, carry, sb=sb):
            pb, w_slot, xg = gather_rows(sb, (j * MOE_TILE).astype(F32))
            y_ref[sb * MOE_TB:(sb + 1) * MOE_TB, :] += lax.dot_general(
                pb, expert(xg, w_slot), tn_dims, preferred_element_type=F32)
            return carry

        lax.fori_loop(1, npass_scr[sb * N_EXPERTS + e], extra_pass, 0)

    @pl.when(half == MOE_HALF - 1)
    def _():
        for sb in range(MOE_NSUB):
            for i in range(MOE_HALF):
                rank_row = rank_scr[sb * N_EXPERTS + (e - (MOE_HALF - 1)) + i]
                p_scr[i * MOE_TILE:(i + 1) * MOE_TILE, :] = jnp.where(slot == rank_row, 1.0, 0.0).astype(BF16)
            sub_rows = slice(sb * MOE_TB, (sb + 1) * MOE_TB)
            y_ref[sub_rows, :] += lax.dot_general(p_scr[...], y_scr[sb], tn_dims, preferred_element_type=F32)

    @pl.when(e == N_EXPERTS - 1)
    def _():
        for sb in range(MOE_NSUB):
            sub_rows = slice(sb * MOE_TB, (sb + 1) * MOE_TB)
            x2 = x1_ref[sub_rows, :] + mod_refs[sb][0, 5:6, :] * y_ref[sub_rows, :]
            y_ref[sub_rows, :] = x2 * _rms_scale(x2) * gf_ref[...]


def _moe(x1, mod3, mod_row_fn, g_norm2, wr_hi, wr_lo, br, wg, wu, wd, g_final):
    n = x1.shape[0]
    rows = MOE_NSUB * MOE_TB
    t_i = lax.broadcasted_iota(jnp.int32, (MOE_TB, MOE_TB), 0)
    t_j = lax.broadcasted_iota(jnp.int32, (MOE_TB, MOE_TB), 1)
    before = (t_i < t_j).astype(BF16)
    return pl.pallas_call(
        _moe_kernel,
        out_shape=jax.ShapeDtypeStruct((n, D_MODEL), F32),
        grid=(n // rows, N_EXPERTS),
        in_specs=[
            pl.BlockSpec((rows, D_MODEL), lambda i, e: (i, 0), pipeline_mode=pl.Buffered(1)),
            pl.BlockSpec((1, 6, D_MODEL), lambda i, e: (mod_row_fn(i * rows), 0, 0)),
            pl.BlockSpec((1, 6, D_MODEL), lambda i, e: (mod_row_fn(i * rows + MOE_TB), 0, 0)),
            _const_spec((1, D_MODEL)),
            _const_spec((D_MODEL, LANES)),
            _const_spec((D_MODEL, LANES)),
            _const_spec((1, LANES)),
            _const_spec((MOE_TB, MOE_TB)),
            pl.BlockSpec((1, D_MODEL, D_EXPERT), lambda i, e: (e, 0, 0)),
            pl.BlockSpec((1, D_MODEL, D_EXPERT), lambda i, e: (e, 0, 0)),
            pl.BlockSpec((1, D_EXPERT, D_MODEL), lambda i, e: (e, 0, 0)),
            _const_spec((1, D_MODEL)),
        ],
        out_specs=pl.BlockSpec((rows, D_MODEL), lambda i, e: (i, 0)),
        scratch_shapes=[pltpu.VMEM((MOE_NSUB, MOE_TB, D_MODEL), BF16),
                        pltpu.VMEM((MOE_NSUB * N_EXPERTS, 1, MOE_TB), F32),
                        pltpu.VMEM((MOE_NSUB * N_EXPERTS, 1, MOE_TB), F32),
                        pltpu.VMEM((MOE_HALF * MOE_TILE, MOE_TB), BF16),
                        pltpu.VMEM((MOE_NSUB, MOE_HALF * MOE_TILE, D_MODEL), BF16),
                        pltpu.SMEM((MOE_NSUB * N_EXPERTS,), jnp.int32)],
        compiler_params=_cparams(("parallel", "arbitrary")),
        name="moe",
    )(x1, mod3, mod3, g_norm2, wr_hi, wr_lo, br, before, wg, wu, wd, g_final)


def _stream(x, mod3, mod_row_fn, seq_len, rows_2d, state, w):
    bsz, t, _ = x.shape
    n = bsz * t
    x2d = x.reshape(n, D_MODEL)
    p3, cols = _in_projection(x2d, mod3, mod_row_fn, w["g_norm1"], w["wmain"], w["bmain"], w["colscale"],
                              w["wg_hi"], w["wg_lo"], w["bg"])
    rows_f = (-cols[:, N_HEADS:2 * N_HEADS]).T.reshape(N_HEADS, 1, n)
    rows_b = (-cols[:, 3 * N_HEADS:4 * N_HEADS]).T.reshape(N_HEADS, 1, n)
    ml = _mlstm(p3, cols, rows_f, rows_b, w["gn3"], state)
    amat, invc = _pool_constants(seq_len, rows_2d)
    n_seq = max(1, 1024 // seq_len)
    yb = _pool(p3, amat, invc, w["wpg"], w["pscale"], seq_len, n_seq)
    ya = ml if state is not None else ml[0]
    x1 = _mix(x2d, mod3, mod_row_fn, ya, yb, p3, w["wa"], w["wb"], w["wo"])
    y = _moe(x1, mod3, mod_row_fn, w["g_norm2"], w["wr_hi"], w["wr_lo"], w["br"],
             w["wg"], w["wu"], w["wd"], w["g_final"])
    return y.reshape(bsz, t, D_MODEL), ml


def kernel(x_prompt, x_sample, state_C, state_n, state_m, c, c_ctx, w_mod, b_mod, g_norm1, w_in, b_in, gn_gain, w_pool_grp, pool_scale, w_proj_a, w_proj_b, w_out, g_norm2, w_router_grp, b_router_grp, w_router_exp, b_router_exp, w_exp_gate, w_exp_up, w_exp_down, g_final):
    bp, tp, _ = x_prompt.shape
    bs, ts, _ = x_sample.shape
    assert w_mod.shape[0] == 1, "single trunk layer"
    assert tp == CHUNK and ts % CHUNK == 0 and ts == ML_ROWS

    cvec = jnp.zeros((8, D_MODEL), F32).at[0].set(c_ctx).at[1:1 + bs].set(c)
    mod3 = _modulation(cvec, w_mod[0], b_mod[0])

    wi, bi = w_in[0], b_in[0]
    d = D_MODEL
    o_q, o_k, o_v, o_o, o_g, o_p, o_ga, o_gb = 0, d, 2 * d, 3 * d, 4 * d, 4 * d + 32, 4 * d + 32 + 512, 4 * d + 32 + 512 + d
    order = [(o_q, d), (o_k, d), (o_v, d), (o_o, d), (o_ga, d), (o_gb, d), (o_p, 512)]
    wmain = _pack_columns(wi, order)
    bmain = jnp.concatenate([bi[s:s + l] for s, l in order]).reshape(1, D_MAIN)
    colscale = jnp.ones((D_MAIN,), F32).at[SLAB_K * LANES:(SLAB_K + N_HEADS) * LANES].set(HEAD_DIM ** -0.5)
    colscale = colscale.reshape(1, D_MAIN)
    gw = wi[:, o_g:o_g + 32]
    gb = bi[o_g:o_g + 32]
    i_f, f_f, i_b, f_b = (slice(0, 8), slice(8, 16), slice(16, 24), slice(24, 32))
    zw = jnp.zeros((d, 8), F32)
    zb = jnp.zeros((8,), F32)
    wg2 = jnp.concatenate([gw[:, f_f], gw[:, f_f], gw[:, f_b], gw[:, f_b], jnp.zeros((d, LANES - 32), F32),
                           zw, gw[:, i_f], zw, gw[:, i_b], jnp.zeros((d, LANES - 32), F32)], axis=1)
    bg2 = jnp.concatenate([gb[f_f], gb[f_f], gb[f_b], gb[f_b], jnp.zeros((LANES - 32,), F32),
                           zb, gb[i_f], zb, gb[i_b], jnp.zeros((LANES - 32,), F32)]).reshape(1, 2 * LANES)
    wg_hi, wg_lo = _split_bf16(wg2)

    wr = jnp.concatenate([w_router_grp[0], w_router_exp[0].reshape(d, N_EXPERTS),
                           jnp.zeros((d, LANES - ROUTER_ROWS), F32)], axis=1)
    br = jnp.concatenate([b_router_grp[0], b_router_exp[0].reshape(N_EXPERTS),
                           jnp.zeros((LANES - ROUTER_ROWS,), F32)]).reshape(1, LANES)
    wr_hi, wr_lo = _split_bf16(wr)

    w = dict(
        g_norm1=g_norm1[0].reshape(1, d), wmain=wmain, bmain=bmain, colscale=colscale,
        wg_hi=wg_hi, wg_lo=wg_lo, bg=bg2,
        gn3=gn_gain[0].reshape(N_HEADS, 1, HEAD_DIM),
        wpg=w_pool_grp[0].astype(BF16), pscale=pool_scale[0].reshape(N_GROUPS, 1, LANES),
        wa=w_proj_a[0].astype(BF16), wb=w_proj_b[0].astype(BF16), wo=w_out[0].astype(BF16),
        g_norm2=g_norm2[0].reshape(1, d), wr_hi=wr_hi, wr_lo=wr_lo, br=br,
        wg=w_exp_gate[0].astype(BF16), wu=w_exp_up[0].astype(BF16), wd=w_exp_down[0].astype(BF16),
        g_final=g_final.reshape(1, d),
    )

    y_prompt, (_, cst, nst, mst) = _stream(x_prompt, mod3, lambda r: 0, tp, None, None, w)
    n_rep = jnp.broadcast_to(state_n[..., None], state_C.shape)
    state = (state_C, n_rep, state_m[:, 0].reshape(bs, 2, N_HEADS, 1, 1))
    y_sample, _ = _stream(x_sample, mod3, lambda r: 1 + r // ts, ts, ts // GRID_W, state, w)

    new_c = cst
    new_n = nst.reshape(bp, 1, 2, N_HEADS, HEAD_DIM)
    new_m = mst.reshape(bp, 1, 2, N_HEADS)
    return (y_prompt, y_sample, new_c, new_n, new_m)
```

```python
import functools

import numpy as np
import jax
import jax.numpy as jnp
from jax import lax
from jax.experimental import pallas as pl
from jax.experimental.pallas import tpu as pltpu

F32 = jnp.float32
BF16 = jnp.bfloat16

D_MODEL = 1024
N_HEADS = 8
HEAD_DIM = 128
GRID_W = 64
POOL_WINDOWS = (2, 4, 8, 16)
N_GROUPS = 4
EXPERTS_PER_GROUP = 4
N_EXPERTS = 16
D_EXPERT = 512
EPS = 1e-6
NEG = -1e30

LANES = 128
CHUNK = 256
N_SLABS = 52
SLAB_Q, SLAB_K, SLAB_V, SLAB_O, SLAB_GA, SLAB_GB, SLAB_POOL = 0, 8, 16, 24, 32, 40, 48
D_MAIN = N_SLABS * LANES
VMEM_LIMIT = 56 * 1024 * 1024


def _cparams(sem):
    return pltpu.CompilerParams(dimension_semantics=sem, vmem_limit_bytes=VMEM_LIMIT)


def _const_spec(shape):
    nd = len(shape)
    return pl.BlockSpec(shape, lambda *_: (0,) * nd, pipeline_mode=pl.Buffered(1))


def _split_bf16(x):
    hi = x.astype(BF16)
    lo = (x - hi.astype(F32)).astype(BF16)
    return hi, lo


def _dot(a, b):
    return jnp.dot(a, b, preferred_element_type=F32)


def _dot3(a, b_hi, b_lo):
    a_hi, a_lo = _split_bf16(a)
    return _dot(a_hi, b_hi) + _dot(a_hi, b_lo) + _dot(a_lo, b_hi)


def _rms_scale(x):
    return lax.rsqrt(jnp.mean(x * x, axis=-1, keepdims=True) + EPS)


def _log_sigmoid(x):
    return jnp.minimum(x, 0.0) - jnp.log1p(jnp.exp(-jnp.abs(x)))


def _mod_kernel(c_ref, w_ref, b_ref, o_ref):
    c = c_ref[...]
    s = c * jax.nn.sigmoid(c)
    w_hi, w_lo = _split_bf16(w_ref[...])
    o_ref[...] = _dot3(s, w_hi, w_lo) + b_ref[...]


def _modulation(cvec, w_mod, b_mod):
    tn = 512
    out = pl.pallas_call(
        _mod_kernel,
        out_shape=jax.ShapeDtypeStruct((8, 6 * D_MODEL), F32),
        grid=(6 * D_MODEL // tn,),
        in_specs=[
            pl.BlockSpec((8, D_MODEL), lambda j: (0, 0)),
            pl.BlockSpec((D_MODEL, tn), lambda j: (0, j)),
            pl.BlockSpec((1, tn), lambda j: (0, j)),
        ],
        out_specs=pl.BlockSpec((8, tn), lambda j: (0, j)),
        compiler_params=_cparams(("parallel",)),
        name="modulation",
    )(cvec, w_mod, b_mod.reshape(1, -1))
    return out.reshape(8, 6, D_MODEL)


PACK_COLS = 512


def _pack_kernel(wt_ref, o_ref):
    o_ref[...] = wt_ref[...].T.astype(BF16)


def _pack_columns(w, order):
    k, n = w.shape
    starts = []
    for src, width in order:
        assert width % PACK_COLS == 0
        starts += [src + j for j in range(0, width, PACK_COLS)]
    n_out = len(starts) * PACK_COLS

    unit = 32
    assert all(s % unit == 0 for s in starts)

    def src_row(j):
        row = jnp.int32(starts[-1] // unit)
        for idx in range(len(starts) - 2, -1, -1):
            row = jnp.where(j == idx, starts[idx] // unit, row)
        return row * unit

    return pl.pallas_call(
        _pack_kernel,
        out_shape=jax.ShapeDtypeStruct((k, n_out), BF16),
        grid=(len(starts),),
        in_specs=[pl.BlockSpec((pl.Element(PACK_COLS), pl.Element(k)), lambda j: (src_row(j), 0))],
        out_specs=pl.BlockSpec((k, PACK_COLS), lambda j: (0, j)),
        compiler_params=_cparams(("parallel",)),
        name="pack_columns",
    )(w.T)


IN_TM = 512
IN_TN = 512
LOG2E = 1.4426950408889634
N_HD = 2 * N_HEADS


def _fwd_lanes(shape):
    lane = lax.broadcasted_iota(jnp.int32, shape, len(shape) - 1)
    return (lane & N_HEADS) == 0


def _inproj_kernel(x_ref, mod_ref, g_ref, w_ref, b_ref, cs_ref, wg_hi_ref, wg_lo_ref, bg_ref,
                   p3_ref, bcum_ref, imb_ref, amax_ref):
    x = x_ref[...]
    h = x * _rms_scale(x) * g_ref[...]
    h = h * (1.0 + mod_ref[0, 1:2, :]) + mod_ref[0, 0:1, :]
    hb = h.astype(BF16)
    for n in range(D_MAIN // IN_TN):
        sl = slice(n * IN_TN, (n + 1) * IN_TN)
        acc = (_dot(hb, w_ref[:, sl]) + b_ref[:, sl]) * cs_ref[:, sl]
        accb = acc.astype(BF16)
        for j in range(IN_TN // LANES):
            p3_ref[n * (IN_TN // LANES) + j] = accb[:, j * LANES:(j + 1) * LANES]

    g2 = _dot3(h, wg_hi_ref[...], wg_lo_ref[...]) + bg_ref[...]
    ls = _log_sigmoid(g2[:, :LANES])
    gi = g2[:, LANES:]
    row = lax.broadcasted_iota(jnp.int32, (CHUNK, LANES), 0)
    fwd = _fwd_lanes((CHUNK, LANES))
    for c in range(IN_TM // CHUNK):
        rows = slice(c * CHUNK, (c + 1) * CHUNK)
        cf = ls[rows]
        cb = cf
        s = 1
        while s < CHUNK:
            cf = cf + jnp.where(row >= s, pltpu.roll(cf, s, 0), 0.0)
            cb = cb + jnp.where(row < CHUNK - s, pltpu.roll(cb, CHUNK - s, 0), 0.0)
            s *= 2
        b = jnp.where(fwd, cf, cb)
        imb = gi[rows] - b
        mf = imb
        mb = imb
        s = 1
        while s < CHUNK:
            mf = jnp.maximum(mf, jnp.where(row >= s, pltpu.roll(mf, s, 0), -jnp.inf))
            mb = jnp.maximum(mb, jnp.where(row < CHUNK - s, pltpu.roll(mb, CHUNK - s, 0), -jnp.inf))
            s *= 2
        bcum_ref[rows, :] = b
        imb_ref[rows, :] = imb
        amax_ref[rows, :] = b + jnp.where(fwd, mf, mb)


def _in_projection(x2d, mod3, mod_row_fn, g_norm1, wmain, bmain, colscale, wg_hi, wg_lo, bg):
    n = x2d.shape[0]
    return pl.pallas_call(
        _inproj_kernel,
        out_shape=(jax.ShapeDtypeStruct((N_SLABS, n, LANES), BF16),) + (jax.ShapeDtypeStruct((n, LANES), F32),) * 3,
        grid=(n // IN_TM,),
        in_specs=[
            pl.BlockSpec((IN_TM, D_MODEL), lambda i: (i, 0)),
            pl.BlockSpec((1, 6, D_MODEL), lambda i: (mod_row_fn(i * IN_TM), 0, 0)),
            _const_spec((1, D_MODEL)),
            _const_spec((D_MODEL, D_MAIN)),
            _const_spec((1, D_MAIN)),
            _const_spec((1, D_MAIN)),
            _const_spec((D_MODEL, 2 * LANES)),
            _const_spec((D_MODEL, 2 * LANES)),
            _const_spec((1, 2 * LANES)),
        ],
        out_specs=(pl.BlockSpec((N_SLABS, IN_TM, LANES), lambda i: (0, i, 0)),)
        + (pl.BlockSpec((IN_TM, LANES), lambda i: (i, 0)),) * 3,
        compiler_params=_cparams(("parallel",)),
        name="in_projection",
    )(x2d, mod3, g_norm1, wmain, bmain, colscale, wg_hi, wg_lo, bg)


GP_ROWS = 1024
GP_NCH = GP_ROWS // CHUNK
Q_C2, Q_EM, Q_WI, Q_WC = 0, 1, 2, 3


def _gate_prep_kernel(*refs, carry):
    if carry:
        b_ref, imb_ref, a_ref, m0_ref, cols_ref, rw_ref, cs_ref = refs
    else:
        b_ref, imb_ref, a_ref, cols_ref, rw_ref, cs_ref = refs
    fwd1 = _fwd_lanes((1, LANES))
    lane = lax.broadcasted_iota(jnp.int32, (CHUNK, LANES), 1)
    neg = jnp.full((1, LANES), NEG, F32)

    def at_scan_end(ref, c):
        return jnp.where(fwd1, ref[(c + 1) * CHUNK - 1:(c + 1) * CHUNK, :], ref[c * CHUNK:c * CHUNK + 1, :])

    b_last = [at_scan_end(b_ref, c) for c in range(GP_NCH)]
    a_last = [at_scan_end(a_ref, c) for c in range(GP_NCH)]
    m_start = m0_ref[0] if carry else neg
    m_prev = [[None] * GP_NCH, [None] * GP_NCH]
    m_new = [[None] * GP_NCH, [None] * GP_NCH]
    for d, order in enumerate((range(GP_NCH), range(GP_NCH - 1, -1, -1))):
        m = m_start
        for c in order:
            m_prev[d][c] = m
            m_new[d][c] = jnp.maximum(b_last[c] + m, a_last[c])
            m = m_new[d][c] if carry else neg

    for c in range(GP_NCH):
        rows = slice(c * CHUNK, (c + 1) * CHUNK)
        mp = jnp.where(fwd1, m_prev[0][c], m_prev[1][c])
        mn = jnp.where(fwd1, m_new[0][c], m_new[1][c])
        b = b_ref[rows, :]
        imb = imb_ref[rows, :]
        inter = b + mp
        m_t = jnp.maximum(inter, a_ref[rows, :])
        c2 = (b - m_t) * LOG2E
        em = jnp.exp(-m_t)
        wc = jnp.exp(b_last[c] - mn + imb)
        packed = jnp.where(lane < N_HD, c2, pltpu.roll(em, N_HD * Q_EM, 1))
        if carry:
            wi = jnp.exp(inter - m_t)
            packed = jnp.where(lane < N_HD * Q_WI, packed, pltpu.roll(wi, N_HD * Q_WI, 1))
        cols_ref[rows, :] = jnp.where(lane < N_HD * Q_WC, packed, pltpu.roll(wc, N_HD * Q_WC, 1))
        rw_ref[rows, :] = jnp.where(lane < N_HD, imb * (-LOG2E), pltpu.roll(wc, N_HD, 1))
        cs_ref[c, 0:1, :] = jnp.exp(b_last[c] + mp - mn)
        cs_ref[c, 1:2, :] = mn


def _gate_prep(bcum, imb, amax, m0=None):
    n = bcum.shape[0]
    carry = m0 is not None
    blk = pl.BlockSpec((GP_ROWS, LANES), lambda i: (i, 0))
    in_specs = [blk, blk, blk]
    args = [bcum, imb, amax]
    if carry:
        in_specs.append(pl.BlockSpec((1, 1, LANES), lambda i: (i, 0, 0)))
        args.append(m0)
    return pl.pallas_call(
        functools.partial(_gate_prep_kernel, carry=carry),
        out_shape=(jax.ShapeDtypeStruct((n, LANES), F32), jax.ShapeDtypeStruct((n, LANES), F32),
                   jax.ShapeDtypeStruct((n // CHUNK, 2, LANES), F32)),
        grid=(n // GP_ROWS,),
        in_specs=in_specs,
        out_specs=(blk, blk, pl.BlockSpec((GP_NCH, 2, LANES), lambda i: (i, 0, 0))),
        compiler_params=_cparams(("parallel",)),
        name="gate_prep",
    )(*args)


ML_ROWS = 1024
ML_NCH = ML_ROWS // CHUNK


def _pick_col(blk, lane_iota, idx):
    return jnp.sum(jnp.where(lane_iota == idx, blk, 0.0), axis=1, keepdims=True)


def _mlstm_kernel(*refs, carry):
    if carry:
        (q_ref, k_ref, v_ref, o_ref, cols_ref, r2f_ref, r2b_ref, gn_ref, cs_ref, c0_ref, n0_ref,
         ya_ref, s_scr, h_scr) = refs
    else:
        (q_ref, k_ref, v_ref, o_ref, cols_ref, r2f_ref, r2b_ref, gn_ref, wrf_ref, wrb_ref,
         ya_ref, cst_ref, nst_ref, s_scr, h_scr) = refs
    head = pl.program_id(1)
    lane = lax.broadcasted_iota(jnp.int32, (CHUNK, LANES), 1)
    lane1 = lax.broadcasted_iota(jnp.int32, (1, LANES), 1)
    r_idx = lax.broadcasted_iota(jnp.int32, (CHUNK, CHUNK), 0)
    c_idx = lax.broadcasted_iota(jnp.int32, (CHUNK, CHUNK), 1)
    ones = jnp.ones((CHUNK, HEAD_DIM), BF16)
    tn_dims = (((0,), (0,)), ((), ()))

    for d in range(2):
        mask = (c_idx <= r_idx) if d == 0 else (c_idx >= r_idx)
        r2_ref = r2f_ref if d == 0 else r2b_ref
        base = N_HEADS * d + head
        if carry:
            cn = jnp.concatenate([c0_ref[0, 0, d, 0], n0_ref[0, 0, d, 0]], axis=1)
        order = range(ML_NCH) if d == 0 else range(ML_NCH - 1, -1, -1)
        for step, c in enumerate(order):
            rows = slice(c * CHUNK, (c + 1) * CHUNK)
            q = q_ref[0, rows, :]
            k = k_ref[0, rows, :]
            v = v_ref[0, rows, :]
            v1 = jnp.concatenate([v, ones], axis=1)
            if d == 0:
                s = lax.dot_general(q, k, (((1,), (1,)), ((), ())), preferred_element_type=F32)
                s_scr[rows, :] = s
            else:
                s = s_scr[rows, :]
            blk = cols_ref[rows, :]
            arg = _pick_col(blk, lane, base + N_HD * Q_C2) - r2_ref[0, :, rows]
            sw = s * jnp.exp2(jnp.where(mask, arg, NEG))
            nd = _dot(sw.astype(BF16), v1)
            if carry:
                nd = nd + _pick_col(blk, lane, base + N_HD * Q_WI) * _dot(q, cn.astype(BF16))
            hdir = nd[:, :HEAD_DIM] / jnp.maximum(jnp.abs(nd[:, HEAD_DIM:]), _pick_col(blk, lane, base + N_HD * Q_EM))
            if d == 0:
                h_scr[rows, :] = hdir
            else:
                h_scr[rows, :] = h_scr[rows, :] + hdir

            if carry and step == ML_NCH - 1:
                continue
            w_col = _pick_col(blk, lane, base + N_HD * Q_WC)
            if carry:
                u = lax.dot_general(k, (w_col * v1.astype(F32)).astype(BF16), tn_dims,
                                    preferred_element_type=F32)
                decay = jnp.sum(jnp.where(lane1 == base, cs_ref[c, 0:1, :], 0.0), axis=1, keepdims=True)
                cn = decay * cn + u
            else:
                w_row = (wrf_ref if d == 0 else wrb_ref)[0, :, rows]
                w8 = jnp.broadcast_to(w_row, (8, CHUNK)).astype(BF16)
                cst_ref[c, 0, d, 0] = lax.dot_general(k, (w_col * v.astype(F32)).astype(BF16), tn_dims,
                                                      preferred_element_type=F32)
                nst_ref[c, d, 0] = _dot(w8, k)[0:1, :]

    hh = h_scr[...]
    hn = hh * _rms_scale(hh) * gn_ref[0]
    ya_ref[0] = (hn * jax.nn.sigmoid(o_ref[0].astype(F32))).astype(BF16)


def _mlstm(p3, cols, rows, cs, gn3, state=None):
    n = p3.shape[1]
    nblk = n // ML_ROWS
    carry = state is not None

    def slab(base):
        return pl.BlockSpec((1, ML_ROWS, LANES), lambda g, h: (base + h, g, 0))

    def row(base):
        return pl.BlockSpec((1, 1, ML_ROWS), lambda g, h: (base + h, 0, g))

    in_specs = [
        slab(SLAB_Q), slab(SLAB_K), slab(SLAB_V), slab(SLAB_O),
        pl.BlockSpec((ML_ROWS, LANES), lambda g, h: (g, 0)),
        row(0), row(N_HEADS),
        pl.BlockSpec((1, 1, LANES), lambda g, h: (h, 0, 0)),
    ]
    args = [p3, p3, p3, p3, cols, rows, rows, gn3]
    ya_shape = jax.ShapeDtypeStruct((N_HEADS, n, LANES), BF16)
    ya_spec = pl.BlockSpec((1, ML_ROWS, LANES), lambda g, h: (h, g, 0))
    state_spec = pl.BlockSpec((1, 1, 2, 1, HEAD_DIM, HEAD_DIM), lambda g, h: (g, 0, 0, h, 0, 0))
    if carry:
        c0, n0 = state
        in_specs += [pl.BlockSpec((ML_NCH, 2, LANES), lambda g, h: (g, 0, 0)), state_spec, state_spec]
        args += [cs, c0, n0]
        out_shape = ya_shape
        out_specs = ya_spec
    else:
        in_specs += [row(N_HD), row(N_HD + N_HEADS)]
        args += [rows, rows]
        nseq = n // CHUNK
        out_shape = (ya_shape,
                     jax.ShapeDtypeStruct((nseq, 1, 2, N_HEADS, HEAD_DIM, HEAD_DIM), F32),
                     jax.ShapeDtypeStruct((nseq, 2, N_HEADS, 1, HEAD_DIM), F32))
        out_specs = (ya_spec,
                     pl.BlockSpec((ML_NCH, 1, 2, 1, HEAD_DIM, HEAD_DIM), lambda g, h: (g, 0, 0, h, 0, 0)),
                     pl.BlockSpec((ML_NCH, 2, 1, 1, HEAD_DIM), lambda g, h: (g, 0, h, 0, 0)))
    return pl.pallas_call(
        functools.partial(_mlstm_kernel, carry=carry),
        out_shape=out_shape,
        grid=(nblk, N_HEADS),
        in_specs=in_specs,
        out_specs=out_specs,
        scratch_shapes=[pltpu.VMEM((ML_ROWS, CHUNK), F32), pltpu.VMEM((ML_ROWS, HEAD_DIM), F32)],
        compiler_params=_cparams(("parallel", "parallel")),
        name="mlstm_latent" if carry else "mlstm_prompt",
    )(*args)


def _pool_kernel(u_ref, a_ref, ic_ref, wpg_ref, ps_ref, yb_ref, *, seq_len, n_seq):
    for sq in range(n_seq):
        rows = slice(sq * seq_len, (sq + 1) * seq_len)
        for g in range(N_GROUPS):
            u = u_ref[g, rows, :]
            pooled = _dot(a_ref[g], u) * ic_ref[g]
            p = (pooled - u.astype(F32)).astype(BF16)
            yb_ref[g, rows, :] = (_dot(p, wpg_ref[g]) * ps_ref[g]).astype(BF16)


def _pool(p3, amat, invc, wpg, pscale, seq_len, n_seq):
    n = p3.shape[1]
    rows = seq_len * n_seq
    return pl.pallas_call(
        functools.partial(_pool_kernel, seq_len=seq_len, n_seq=n_seq),
        out_shape=jax.ShapeDtypeStruct((N_GROUPS, n, LANES), BF16),
        grid=(n // rows,),
        in_specs=[
            pl.BlockSpec((N_GROUPS, rows, LANES), lambda i: (SLAB_POOL // N_GROUPS, i, 0)),
            _const_spec((N_GROUPS, seq_len, seq_len)),
            _const_spec((N_GROUPS, seq_len, LANES)),
            _const_spec((N_GROUPS, LANES, LANES)),
            _const_spec((N_GROUPS, 1, LANES)),
        ],
        out_specs=pl.BlockSpec((N_GROUPS, rows, LANES), lambda i: (0, i, 0)),
        compiler_params=_cparams(("parallel",)),
        name="pool",
    )(p3, amat, invc, wpg, pscale)


def _window_matrix(size, w):
    idx = np.arange(size)
    lo = np.clip(idx - w // 2, 0, size)
    hi = np.clip(idx - w // 2 + w, 0, size)
    s = np.arange(size)[None, :]
    a = ((s >= lo[:, None]) & (s < hi[:, None])).astype(np.float32)
    return a, (hi - lo).astype(np.float32)


def _pool_constants(seq_len, rows_2d):
    mats, invs = [], []
    for w in POOL_WINDOWS:
        if rows_2d is None:
            a, cnt = _window_matrix(seq_len, w)
        else:
            ar, cr = _window_matrix(rows_2d, w)
            ac, cc = _window_matrix(GRID_W, w)
            a = np.kron(ar, ac)
            cnt = np.kron(cr, cc)
        mats.append(a)
        invs.append(np.repeat((1.0 / cnt)[:, None], LANES, axis=1))
    return (jnp.asarray(np.stack(mats), dtype=BF16), jnp.asarray(np.stack(invs), dtype=F32))


MIX_TM = 512


def _mix_kernel(x_ref, mod_ref, ya_ref, yb_ref, ga_ref, gb_ref, wa_ref, wb_ref, wo_ref, x1_ref):
    ya = jnp.concatenate([ya_ref[i] for i in range(N_HEADS)], axis=1)
    yb = jnp.concatenate([yb_ref[i] for i in range(N_GROUPS)], axis=1)
    ga = jnp.concatenate([ga_ref[i] for i in range(8)], axis=1).astype(F32)
    gb = jnp.concatenate([gb_ref[i] for i in range(8)], axis=1).astype(F32)
    merged = jax.nn.sigmoid(ga) * _dot(ya, wa_ref[...]) + jax.nn.sigmoid(gb) * _dot(yb, wb_ref[...])
    out = _dot(merged.astype(BF16), wo_ref[...])
    x1_ref[...] = x_ref[...] + mod_ref[0, 2:3, :] * out


def _mix(x2d, mod3, mod_row_fn, ya, yb, p3, wa, wb, wo):
    n = x2d.shape[0]
    return pl.pallas_call(
        _mix_kernel,
        out_shape=jax.ShapeDtypeStruct((n, D_MODEL), F32),
        grid=(n // MIX_TM,),
        in_specs=[
            pl.BlockSpec((MIX_TM, D_MODEL), lambda i: (i, 0)),
            pl.BlockSpec((1, 6, D_MODEL), lambda i: (mod_row_fn(i * MIX_TM), 0, 0)),
            pl.BlockSpec((N_HEADS, MIX_TM, LANES), lambda i: (0, i, 0)),
            pl.BlockSpec((N_GROUPS, MIX_TM, LANES), lambda i: (0, i, 0)),
            pl.BlockSpec((8, MIX_TM, LANES), lambda i: (SLAB_GA // 8, i, 0)),
            pl.BlockSpec((8, MIX_TM, LANES), lambda i: (SLAB_GB // 8, i, 0)),
            _const_spec((D_MODEL, D_MODEL)),
            _const_spec((N_GROUPS * LANES, D_MODEL)),
            _const_spec((D_MODEL, D_MODEL)),
        ],
        out_specs=pl.BlockSpec((MIX_TM, D_MODEL), lambda i: (i, 0)),
        compiler_params=_cparams(("parallel",)),
        name="mix",
    )(x2d, mod3, ya, yb, p3, p3, wa, wb, wo)


MOE_TB = 1024
MOE_TILE = 160
MOE_NSUB = 2
MOE_HALF = 8
ROUTER_ROWS = 4 + N_EXPERTS


def _route_rows(lt):
    lg = [lt[i:i + 1] for i in range(N_GROUPS)]
    mx = jnp.maximum(jnp.maximum(lg[0], lg[1]), jnp.maximum(lg[2], lg[3]))
    p_top = 1.0 / (jnp.exp(lg[0] - mx) + jnp.exp(lg[1] - mx) + jnp.exp(lg[2] - mx) + jnp.exp(lg[3] - mx))
    gidx = jnp.zeros(lg[0].shape, jnp.int32)
    best = lg[0]
    for i in range(1, N_GROUPS):
        upd = lg[i] > best
        gidx = jnp.where(upd, i, gidx)
        best = jnp.where(upd, lg[i], best)

    def expert_row(g, e):
        r = N_GROUPS + g * EXPERTS_PER_GROUP + e
        return lt[r:r + 1]

    le = [jnp.where(gidx == 0, expert_row(0, e),
                    jnp.where(gidx == 1, expert_row(1, e),
                              jnp.where(gidx == 2, expert_row(2, e), expert_row(3, e))))
          for e in range(EXPERTS_PER_GROUP)]
    v1 = jnp.maximum(jnp.maximum(le[0], le[1]), jnp.maximum(le[2], le[3]))
    i1 = jnp.full(v1.shape, EXPERTS_PER_GROUP - 1, jnp.int32)
    for e in range(EXPERTS_PER_GROUP - 2, -1, -1):
        i1 = jnp.where(le[e] == v1, e, i1)
    le2 = [jnp.where(i1 == e, -jnp.inf, le[e]) for e in range(EXPERTS_PER_GROUP)]
    v2 = jnp.maximum(jnp.maximum(le2[0], le2[1]), jnp.maximum(le2[2], le2[3]))
    i2 = jnp.full(v2.shape, EXPERTS_PER_GROUP - 1, jnp.int32)
    for e in range(EXPERTS_PER_GROUP - 2, -1, -1):
        i2 = jnp.where(le2[e] == v2, e, i2)
    e2 = jnp.exp(v2 - v1)
    w1 = p_top / (1.0 + e2)
    w2 = p_top * e2 / (1.0 + e2)
    w_rows, sel_rows = [], []
    for g in range(N_GROUPS):
        for e in range(EXPERTS_PER_GROUP):
            first = jnp.where(gidx == g, jnp.where(i1 == e, 1.0, 0.0), 0.0)
            second = jnp.where(gidx == g, jnp.where(i2 == e, 1.0, 0.0), 0.0)
            w_rows.append(first * w1 + second * w2)
            sel_rows.append(first + second)
    return w_rows, sel_rows


def _moe_kernel(x1_ref, mod_a_ref, mod_b_ref, g2_ref, wr_hi_ref, wr_lo_ref, br_ref, before_ref,
                wg_ref, wu_ref, wd_ref, gf_ref, y_ref, h2_scr, rank_scr, w_scr, p_scr, y_scr, npass_scr):
    e = pl.program_id(1)
    mod_refs = (mod_a_ref, mod_b_ref)
    tn_dims = (((0,), (0,)), ((), ()))
    slot = lax.broadcasted_iota(jnp.int32, (MOE_TILE, MOE_TB), 0).astype(F32)

    @pl.when(e == 0)
    def _():
        for sb in range(MOE_NSUB):
            x1 = x1_ref[sb * MOE_TB:(sb + 1) * MOE_TB, :]
            mod_ref = mod_refs[sb]
            h2 = x1 * _rms_scale(x1) * g2_ref[...]
            h2 = h2 * (1.0 + mod_ref[0, 4:5, :]) + mod_ref[0, 3:4, :]
            h_hi, h_lo = _split_bf16(h2)
            h2_scr[sb] = h_hi
            wr_hi = wr_hi_ref[...]
            logits = _dot(h_hi, wr_hi) + _dot(h_lo, wr_hi) + _dot(h_hi, wr_lo_ref[...]) + br_ref[...]
            w_rows, sel_rows = _route_rows(logits.T)
            sel = jnp.concatenate(sel_rows, axis=0)
            rank = _dot(sel.astype(BF16), before_ref[...])
            rank = jnp.where(sel > 0.0, rank, -1.0)
            n_routed = _dot(sel.astype(BF16), jnp.ones((MOE_TB, LANES), BF16))
            for i in range(N_EXPERTS):
                rank_scr[sb * N_EXPERTS + i] = rank[i:i + 1]
                w_scr[sb * N_EXPERTS + i] = w_rows[i]
                n_i = n_routed[i, 0].astype(jnp.int32)
                passes = jnp.int32(0)
                for kk in range(-(-MOE_TB // MOE_TILE)):
                    passes = passes + (n_i > kk * MOE_TILE).astype(jnp.int32)
                npass_scr[sb * N_EXPERTS + i] = passes
        y_ref[...] = jnp.zeros_like(y_ref)

    half = lax.rem(e, MOE_HALF)

    def gather_rows(sb, first_slot):
        rank_row = rank_scr[sb * N_EXPERTS + e]
        hit = (slot + first_slot) == rank_row
        pb = jnp.where(hit, 1.0, 0.0).astype(BF16)
        w_slot = jnp.sum(jnp.where(hit, w_scr[sb * N_EXPERTS + e], 0.0), axis=1, keepdims=True)
        return pb, w_slot, _dot(pb, h2_scr[sb]).astype(BF16)

    def expert(xg, w_slot):
        a = _dot(xg, wg_ref[0])
        u = _dot(xg, wu_ref[0])
        act = (a * jax.nn.sigmoid(a) * u * w_slot).astype(BF16)
        return _dot(act, wd_ref[0]).astype(BF16)

    first = [gather_rows(sb, 0.0) for sb in range(MOE_NSUB)]
    yv = expert(jnp.concatenate([f[2] for f in first], axis=0), jnp.concatenate([f[1] for f in first], axis=0))
    for sb in range(MOE_NSUB):
        y_scr[sb, pl.ds(pl.multiple_of(half * MOE_TILE, MOE_TILE), MOE_TILE), :] = \
            yv[sb * MOE_TILE:(sb + 1) * MOE_TILE]

    for sb in range(MOE_NSUB):
        def extra_pass(j, carry, sb=sb):
            pb, w_slot, xg = gather_rows(sb, (j * MOE_TILE).astype(F32))
            y_ref[sb * MOE_TB:(sb + 1) * MOE_TB, :] += lax.dot_general(
                pb, expert(xg, w_slot), tn_dims, preferred_element_type=F32)
            return carry

        lax.fori_loop(1, npass_scr[sb * N_EXPERTS + e], extra_pass, 0)

    @pl.when(half == MOE_HALF - 1)
    def _():
        for sb in range(MOE_NSUB):
            for i in range(MOE_HALF):
                rank_row = rank_scr[sb * N_EXPERTS + (e - (MOE_HALF - 1)) + i]
                p_scr[i * MOE_TILE:(i + 1) * MOE_TILE, :] = jnp.where(slot == rank_row, 1.0, 0.0).astype(BF16)
            sub_rows = slice(sb * MOE_TB, (sb + 1) * MOE_TB)
            y_ref[sub_rows, :] += lax.dot_general(p_scr[...], y_scr[sb], tn_dims, preferred_element_type=F32)

    @pl.when(e == N_EXPERTS - 1)
    def _():
        for sb in range(MOE_NSUB):
            sub_rows = slice(sb * MOE_TB, (sb + 1) * MOE_TB)
            x2 = x1_ref[sub_rows, :] + mod_refs[sb][0, 5:6, :] * y_ref[sub_rows, :]
            y_ref[sub_rows, :] = x2 * _rms_scale(x2) * gf_ref[...]


def _moe(x1, mod3, mod_row_fn, g_norm2, wr_hi, wr_lo, br, wg, wu, wd, g_final):
    n = x1.shape[0]
    rows = MOE_NSUB * MOE_TB
    t_i = lax.broadcasted_iota(jnp.int32, (MOE_TB, MOE_TB), 0)
    t_j = lax.broadcasted_iota(jnp.int32, (MOE_TB, MOE_TB), 1)
    before = (t_i < t_j).astype(BF16)
    return pl.pallas_call(
        _moe_kernel,
        out_shape=jax.ShapeDtypeStruct((n, D_MODEL), F32),
        grid=(n // rows, N_EXPERTS),
        in_specs=[
            pl.BlockSpec((rows, D_MODEL), lambda i, e: (i, 0), pipeline_mode=pl.Buffered(1)),
            pl.BlockSpec((1, 6, D_MODEL), lambda i, e: (mod_row_fn(i * rows), 0, 0)),
            pl.BlockSpec((1, 6, D_MODEL), lambda i, e: (mod_row_fn(i * rows + MOE_TB), 0, 0)),
            _const_spec((1, D_MODEL)),
            _const_spec((D_MODEL, LANES)),
            _const_spec((D_MODEL, LANES)),
            _const_spec((1, LANES)),
            _const_spec((MOE_TB, MOE_TB)),
            pl.BlockSpec((1, D_MODEL, D_EXPERT), lambda i, e: (e, 0, 0)),
            pl.BlockSpec((1, D_MODEL, D_EXPERT), lambda i, e: (e, 0, 0)),
            pl.BlockSpec((1, D_EXPERT, D_MODEL), lambda i, e: (e, 0, 0)),
            _const_spec((1, D_MODEL)),
        ],
        out_specs=pl.BlockSpec((rows, D_MODEL), lambda i, e: (i, 0)),
        scratch_shapes=[pltpu.VMEM((MOE_NSUB, MOE_TB, D_MODEL), BF16),
                        pltpu.VMEM((MOE_NSUB * N_EXPERTS, 1, MOE_TB), F32),
                        pltpu.VMEM((MOE_NSUB * N_EXPERTS, 1, MOE_TB), F32),
                        pltpu.VMEM((MOE_HALF * MOE_TILE, MOE_TB), BF16),
                        pltpu.VMEM((MOE_NSUB, MOE_HALF * MOE_TILE, D_MODEL), BF16),
                        pltpu.SMEM((MOE_NSUB * N_EXPERTS,), jnp.int32)],
        compiler_params=_cparams(("parallel", "arbitrary")),
        name="moe",
    )(x1, mod3, mod3, g_norm2, wr_hi, wr_lo, br, before, wg, wu, wd, g_final)


def _stream(x, mod3, mod_row_fn, seq_len, rows_2d, state, w):
    bsz, t, _ = x.shape
    n = bsz * t
    x2d = x.reshape(n, D_MODEL)
    p3, bcum, imb, amax = _in_projection(x2d, mod3, mod_row_fn, w["g_norm1"], w["wmain"], w["bmain"],
                                         w["colscale"], w["wg_hi"], w["wg_lo"], w["bg"])
    m0 = None if state is None else state[2]
    cols, rw, cs = _gate_prep(bcum, imb, amax, m0)
    rows = rw[:, :2 * N_HD].T.reshape(2 * N_HD, 1, n)
    ml = _mlstm(p3, cols, rows, cs, w["gn3"], None if state is None else state[:2])
    amat, invc = _pool_constants(seq_len, rows_2d)
    n_seq = max(1, 1024 // seq_len)
    yb = _pool(p3, amat, invc, w["wpg"], w["pscale"], seq_len, n_seq)
    ya = ml if state is not None else ml[0]
    x1 = _mix(x2d, mod3, mod_row_fn, ya, yb, p3, w["wa"], w["wb"], w["wo"])
    y = _moe(x1, mod3, mod_row_fn, w["g_norm2"], w["wr_hi"], w["wr_lo"], w["br"],
             w["wg"], w["wu"], w["wd"], w["g_final"])
    return y.reshape(bsz, t, D_MODEL), ml, cs


def kernel(x_prompt, x_sample, state_C, state_n, state_m, c, c_ctx, w_mod, b_mod, g_norm1, w_in, b_in, gn_gain, w_pool_grp, pool_scale, w_proj_a, w_proj_b, w_out, g_norm2, w_router_grp, b_router_grp, w_router_exp, b_router_exp, w_exp_gate, w_exp_up, w_exp_down, g_final):
    bp, tp, _ = x_prompt.shape
    bs, ts, _ = x_sample.shape
    assert w_mod.shape[0] == 1, "single trunk layer"
    assert tp == CHUNK and ts % CHUNK == 0 and ts == ML_ROWS

    cvec = jnp.zeros((8, D_MODEL), F32).at[0].set(c_ctx).at[1:1 + bs].set(c)
    mod3 = _modulation(cvec, w_mod[0], b_mod[0])

    wi, bi = w_in[0], b_in[0]
    d = D_MODEL
    o_q, o_k, o_v, o_o, o_g, o_p, o_ga, o_gb = 0, d, 2 * d, 3 * d, 4 * d, 4 * d + 32, 4 * d + 32 + 512, 4 * d + 32 + 512 + d
    order = [(o_q, d), (o_k, d), (o_v, d), (o_o, d), (o_ga, d), (o_gb, d), (o_p, 512)]
    wmain = _pack_columns(wi, order)
    bmain = jnp.concatenate([bi[s:s + l] for s, l in order]).reshape(1, D_MAIN)
    colscale = jnp.ones((D_MAIN,), F32).at[SLAB_K * LANES:(SLAB_K + N_HEADS) * LANES].set(HEAD_DIM ** -0.5)
    colscale = colscale.reshape(1, D_MAIN)
    gw = wi[:, o_g:o_g + 32]
    gb = bi[o_g:o_g + 32]
    i_f, f_f, i_b, f_b = (slice(0, 8), slice(8, 16), slice(16, 24), slice(24, 32))
    pad_w = jnp.zeros((d, LANES - N_HD), F32)
    pad_b = jnp.zeros((LANES - N_HD,), F32)
    wg2 = jnp.concatenate([gw[:, f_f], gw[:, f_b], pad_w, gw[:, i_f], gw[:, i_b], pad_w], axis=1)
    bg2 = jnp.concatenate([gb[f_f], gb[f_b], pad_b, gb[i_f], gb[i_b], pad_b]).reshape(1, 2 * LANES)
    wg_hi, wg_lo = _split_bf16(wg2)

    wr = jnp.concatenate([w_router_grp[0], w_router_exp[0].reshape(d, N_EXPERTS),
                          jnp.zeros((d, LANES - ROUTER_ROWS), F32)], axis=1)
    br = jnp.concatenate([b_router_grp[0], b_router_exp[0].reshape(N_EXPERTS),
                          jnp.zeros((LANES - ROUTER_ROWS,), F32)]).reshape(1, LANES)
    wr_hi, wr_lo = _split_bf16(wr)

    w = dict(
        g_norm1=g_norm1[0].reshape(1, d), wmain=wmain, bmain=bmain, colscale=colscale,
        wg_hi=wg_hi, wg_lo=wg_lo, bg=bg2,
        gn3=gn_gain[0].reshape(N_HEADS, 1, HEAD_DIM),
        wpg=w_pool_grp[0].astype(BF16), pscale=pool_scale[0].reshape(N_GROUPS, 1, LANES),
        wa=w_proj_a[0].astype(BF16), wb=w_proj_b[0].astype(BF16), wo=w_out[0].astype(BF16),
        g_norm2=g_norm2[0].reshape(1, d), wr_hi=wr_hi, wr_lo=wr_lo, br=br,
        wg=w_exp_gate[0].astype(BF16), wu=w_exp_up[0].astype(BF16), wd=w_exp_down[0].astype(BF16),
        g_final=g_final.reshape(1, d),
    )

    y_prompt, (_, cst, nst), cs_prompt = _stream(x_prompt, mod3, lambda r: 0, tp, None, None, w)
    n_rep = jnp.broadcast_to(state_n[..., None], state_C.shape)
    m0 = jnp.pad(state_m[:, 0].reshape(bs, N_HD), ((0, 0), (0, LANES - N_HD))).reshape(bs, 1, LANES)
    y_sample, _, _ = _stream(x_sample, mod3, lambda r: 1 + r // ts, ts, ts // GRID_W, (state_C, n_rep, m0), w)

    new_c = cst
    new_n = nst.reshape(bp, 1, 2, N_HEADS, HEAD_DIM)
    new_m = cs_prompt[:, 1, :N_HD].reshape(bp, 1, 2, N_HEADS)
    return (y_prompt, y_sample, new_c, new_n, new_m)
```

```python
import functools

import numpy as np
import jax
import jax.numpy as jnp
from jax import lax
from jax.experimental import pallas as pl
from jax.experimental.pallas import tpu as pltpu

F32 = jnp.float32
BF16 = jnp.bfloat16

D_MODEL = 1024
N_HEADS = 8
HEAD_DIM = 128
GRID_W = 64
POOL_WINDOWS = (2, 4, 8, 16)
N_GROUPS = 4
EXPERTS_PER_GROUP = 4
N_EXPERTS = 16
D_EXPERT = 512
EPS = 1e-6
NEG = -1e30

LANES = 128
CHUNK = 256
N_SLABS = 52
SLAB_Q, SLAB_K, SLAB_V, SLAB_O, SLAB_GA, SLAB_GB, SLAB_POOL = 0, 8, 16, 24, 32, 40, 48
D_MAIN = N_SLABS * LANES
VMEM_LIMIT = 56 * 1024 * 1024


def _cparams(sem):
    return pltpu.CompilerParams(dimension_semantics=sem, vmem_limit_bytes=VMEM_LIMIT)


def _const_spec(shape):
    nd = len(shape)
    return pl.BlockSpec(shape, lambda *_: (0,) * nd, pipeline_mode=pl.Buffered(1))


def _split_bf16(x):
    hi = x.astype(BF16)
    lo = (x - hi.astype(F32)).astype(BF16)
    return hi, lo


def _dot(a, b):
    return jnp.dot(a, b, preferred_element_type=F32)


def _dot3(a, b_hi, b_lo):
    a_hi, a_lo = _split_bf16(a)
    return _dot(a_hi, b_hi) + _dot(a_hi, b_lo) + _dot(a_lo, b_hi)


def _rms_scale(x):
    return lax.rsqrt(jnp.mean(x * x, axis=-1, keepdims=True) + EPS)


def _log_sigmoid(x):
    return jnp.minimum(x, 0.0) - jnp.log1p(jnp.exp(-jnp.abs(x)))


def _mod_kernel(c_ref, w_ref, b_ref, o_ref):
    c = c_ref[...]
    s = c * jax.nn.sigmoid(c)
    w_hi, w_lo = _split_bf16(w_ref[...])
    o_ref[...] = _dot3(s, w_hi, w_lo) + b_ref[...]


def _modulation(cvec, w_mod, b_mod):
    tn = 512
    out = pl.pallas_call(
        _mod_kernel,
        out_shape=jax.ShapeDtypeStruct((8, 6 * D_MODEL), F32),
        grid=(6 * D_MODEL // tn,),
        in_specs=[
            pl.BlockSpec((8, D_MODEL), lambda j: (0, 0)),
            pl.BlockSpec((D_MODEL, tn), lambda j: (0, j)),
            pl.BlockSpec((1, tn), lambda j: (0, j)),
        ],
        out_specs=pl.BlockSpec((8, tn), lambda j: (0, j)),
        compiler_params=_cparams(("parallel",)),
        name="modulation",
    )(cvec, w_mod, b_mod.reshape(1, -1))
    return out.reshape(8, 6, D_MODEL)


PACK_COLS = 512


def _pack_kernel(wt_ref, o_ref):
    o_ref[...] = wt_ref[...].T.astype(BF16)


def _pack_columns(w, order):
    k, n = w.shape
    starts = []
    for src, width in order:
        assert width % PACK_COLS == 0
        starts += [src + j for j in range(0, width, PACK_COLS)]
    n_out = len(starts) * PACK_COLS

    unit = 32
    assert all(s % unit == 0 for s in starts)

    def src_row(j):
        row = jnp.int32(starts[-1] // unit)
        for idx in range(len(starts) - 2, -1, -1):
            row = jnp.where(j == idx, starts[idx] // unit, row)
        return row * unit

    return pl.pallas_call(
        _pack_kernel,
        out_shape=jax.ShapeDtypeStruct((k, n_out), BF16),
        grid=(len(starts),),
        in_specs=[pl.BlockSpec((pl.Element(PACK_COLS), pl.Element(k)), lambda j: (src_row(j), 0))],
        out_specs=pl.BlockSpec((k, PACK_COLS), lambda j: (0, j)),
        compiler_params=_cparams(("parallel",)),
        name="pack_columns",
    )(w.T)


def _copy_kernel(w_ref, o_ref):
    o_ref[...] = w_ref[...]


def _take_columns_t(w, start, width):
    k = w.shape[0]
    return pl.pallas_call(
        _copy_kernel,
        out_shape=jax.ShapeDtypeStruct((width, k), w.dtype),
        grid=(1,),
        in_specs=[pl.BlockSpec((pl.Element(width), pl.Element(k)), lambda j: (start, 0))],
        out_specs=pl.BlockSpec((width, k), lambda j: (0, 0)),
        name="take_columns",
    )(w.T)


IN_TM = 512
IN_TN = 512
LOG2E = 1.4426950408889634
N_HD = 2 * N_HEADS


def _fwd_lanes(shape):
    lane = lax.broadcasted_iota(jnp.int32, shape, len(shape) - 1)
    return (lane & N_HEADS) == 0


def _inproj_kernel(x_ref, mod_ref, g_ref, w_ref, b_ref, cs_ref, wg_hi_ref, wg_lo_ref, bg_ref,
                   p3_ref, bcum_ref, imb_ref, amax_ref):
    x = x_ref[...]
    h = x * _rms_scale(x) * g_ref[...]
    h = h * (1.0 + mod_ref[0, 1:2, :]) + mod_ref[0, 0:1, :]
    hb = h.astype(BF16)
    for n in range(D_MAIN // IN_TN):
        sl = slice(n * IN_TN, (n + 1) * IN_TN)
        acc = (_dot(hb, w_ref[:, sl]) + b_ref[:, sl]) * cs_ref[:, sl]
        accb = acc.astype(BF16)
        for j in range(IN_TN // LANES):
            p3_ref[n * (IN_TN // LANES) + j] = accb[:, j * LANES:(j + 1) * LANES]

    g2 = _dot3(h, wg_hi_ref[...], wg_lo_ref[...]) + bg_ref[...]
    ls = _log_sigmoid(g2[:, :LANES])
    gi = g2[:, LANES:]
    row = lax.broadcasted_iota(jnp.int32, (CHUNK, LANES), 0)
    fwd = _fwd_lanes((CHUNK, LANES))
    for c in range(IN_TM // CHUNK):
        rows = slice(c * CHUNK, (c + 1) * CHUNK)
        cf = ls[rows]
        cb = cf
        s = 1
        while s < CHUNK:
            cf = cf + jnp.where(row >= s, pltpu.roll(cf, s, 0), 0.0)
            cb = cb + jnp.where(row < CHUNK - s, pltpu.roll(cb, CHUNK - s, 0), 0.0)
            s *= 2
        b = jnp.where(fwd, cf, cb)
        imb = gi[rows] - b
        mf = imb
        mb = imb
        s = 1
        while s < CHUNK:
            mf = jnp.maximum(mf, jnp.where(row >= s, pltpu.roll(mf, s, 0), -jnp.inf))
            mb = jnp.maximum(mb, jnp.where(row < CHUNK - s, pltpu.roll(mb, CHUNK - s, 0), -jnp.inf))
            s *= 2
        bcum_ref[rows, :] = b
        imb_ref[rows, :] = imb
        amax_ref[rows, :] = b + jnp.where(fwd, mf, mb)


def _in_projection(x2d, mod3, mod_row_fn, g_norm1, wmain, bmain, colscale, wg_hi, wg_lo, bg):
    n = x2d.shape[0]
    return pl.pallas_call(
        _inproj_kernel,
        out_shape=(jax.ShapeDtypeStruct((N_SLABS, n, LANES), BF16),) + (jax.ShapeDtypeStruct((n, LANES), F32),) * 3,
        grid=(n // IN_TM,),
        in_specs=[
            pl.BlockSpec((IN_TM, D_MODEL), lambda i: (i, 0)),
            pl.BlockSpec((1, 6, D_MODEL), lambda i: (mod_row_fn(i * IN_TM), 0, 0)),
            _const_spec((1, D_MODEL)),
            _const_spec((D_MODEL, D_MAIN)),
            _const_spec((1, D_MAIN)),
            _const_spec((1, D_MAIN)),
            _const_spec((D_MODEL, 2 * LANES)),
            _const_spec((D_MODEL, 2 * LANES)),
            _const_spec((1, 2 * LANES)),
        ],
        out_specs=(pl.BlockSpec((N_SLABS, IN_TM, LANES), lambda i: (0, i, 0)),)
        + (pl.BlockSpec((IN_TM, LANES), lambda i: (i, 0)),) * 3,
        compiler_params=_cparams(("parallel",)),
        name="in_projection",
    )(x2d, mod3, g_norm1, wmain, bmain, colscale, wg_hi, wg_lo, bg)


GP_ROWS = 1024
GP_NCH = GP_ROWS // CHUNK
Q_C2, Q_EM, Q_WI, Q_WC = 0, 1, 2, 3


def _gate_prep_kernel(*refs, carry):
    if carry:
        b_ref, imb_ref, a_ref, m0_ref, cols_ref, rw_ref, cs_ref = refs
    else:
        b_ref, imb_ref, a_ref, cols_ref, rw_ref, cs_ref = refs
    fwd1 = _fwd_lanes((1, LANES))
    lane = lax.broadcasted_iota(jnp.int32, (CHUNK, LANES), 1)
    neg = jnp.full((1, LANES), NEG, F32)

    def at_scan_end(ref, c):
        return jnp.where(fwd1, ref[(c + 1) * CHUNK - 1:(c + 1) * CHUNK, :], ref[c * CHUNK:c * CHUNK + 1, :])

    b_last = [at_scan_end(b_ref, c) for c in range(GP_NCH)]
    a_last = [at_scan_end(a_ref, c) for c in range(GP_NCH)]
    m_start = m0_ref[0] if carry else neg
    m_prev = [[None] * GP_NCH, [None] * GP_NCH]
    m_new = [[None] * GP_NCH, [None] * GP_NCH]
    for d, order in enumerate((range(GP_NCH), range(GP_NCH - 1, -1, -1))):
        m = m_start
        for c in order:
            m_prev[d][c] = m
            m_new[d][c] = jnp.maximum(b_last[c] + m, a_last[c])
            m = m_new[d][c] if carry else neg

    for c in range(GP_NCH):
        rows = slice(c * CHUNK, (c + 1) * CHUNK)
        mp = jnp.where(fwd1, m_prev[0][c], m_prev[1][c])
        mn = jnp.where(fwd1, m_new[0][c], m_new[1][c])
        b = b_ref[rows, :]
        imb = imb_ref[rows, :]
        inter = b + mp
        m_t = jnp.maximum(inter, a_ref[rows, :])
        c2 = (b - m_t) * LOG2E
        em = jnp.exp(-m_t)
        wc = jnp.exp(b_last[c] - mn + imb)
        packed = jnp.where(lane < N_HD, c2, pltpu.roll(em, N_HD * Q_EM, 1))
        if carry:
            wi = jnp.exp(inter - m_t)
            packed = jnp.where(lane < N_HD * Q_WI, packed, pltpu.roll(wi, N_HD * Q_WI, 1))
        cols_ref[rows, :] = jnp.where(lane < N_HD * Q_WC, packed, pltpu.roll(wc, N_HD * Q_WC, 1))
        rw_ref[rows, :] = jnp.where(lane < N_HD, imb * (-LOG2E), pltpu.roll(wc, N_HD, 1))
        cs_ref[c, 0:1, :] = jnp.exp(b_last[c] + mp - mn)
        cs_ref[c, 1:2, :] = mn


def _gate_prep(bcum, imb, amax, m0=None):
    n = bcum.shape[0]
    carry = m0 is not None
    blk = pl.BlockSpec((GP_ROWS, LANES), lambda i: (i, 0))
    in_specs = [blk, blk, blk]
    args = [bcum, imb, amax]
    if carry:
        in_specs.append(pl.BlockSpec((1, 1, LANES), lambda i: (i, 0, 0)))
        args.append(m0)
    return pl.pallas_call(
        functools.partial(_gate_prep_kernel, carry=carry),
        out_shape=(jax.ShapeDtypeStruct((n, LANES), F32), jax.ShapeDtypeStruct((n, LANES), F32),
                   jax.ShapeDtypeStruct((n // CHUNK, 2, LANES), F32)),
        grid=(n // GP_ROWS,),
        in_specs=in_specs,
        out_specs=(blk, blk, pl.BlockSpec((GP_NCH, 2, LANES), lambda i: (i, 0, 0))),
        compiler_params=_cparams(("parallel",)),
        name="gate_prep",
    )(*args)


ML_ROWS = 1024
ML_NCH = ML_ROWS // CHUNK


def _pick_col(blk, lane_iota, idx):
    return jnp.sum(jnp.where(lane_iota == idx, blk, 0.0), axis=1, keepdims=True)


def _mlstm_kernel(*refs, carry):
    if carry:
        (q_ref, k_ref, v_ref, o_ref, cols_ref, r2f_ref, r2b_ref, gn_ref, cs_ref, c0_ref, n0_ref,
         ya_ref, s_scr, h_scr) = refs
    else:
        (q_ref, k_ref, v_ref, o_ref, cols_ref, r2f_ref, r2b_ref, gn_ref, wrf_ref, wrb_ref,
         ya_ref, cst_ref, nst_ref, s_scr, h_scr) = refs
    head = pl.program_id(1)
    lane = lax.broadcasted_iota(jnp.int32, (CHUNK, LANES), 1)
    lane1 = lax.broadcasted_iota(jnp.int32, (1, LANES), 1)
    r_idx = lax.broadcasted_iota(jnp.int32, (CHUNK, CHUNK), 0)
    c_idx = lax.broadcasted_iota(jnp.int32, (CHUNK, CHUNK), 1)
    ones = jnp.ones((CHUNK, HEAD_DIM), BF16)
    tn_dims = (((0,), (0,)), ((), ()))

    for d in range(2):
        mask = (c_idx <= r_idx) if d == 0 else (c_idx >= r_idx)
        r2_ref = r2f_ref if d == 0 else r2b_ref
        base = N_HEADS * d + head
        if carry:
            cn = jnp.concatenate([c0_ref[0, 0, d, 0], n0_ref[0, 0, d, 0]], axis=1)
        order = range(ML_NCH) if d == 0 else range(ML_NCH - 1, -1, -1)
        for step, c in enumerate(order):
            rows = slice(c * CHUNK, (c + 1) * CHUNK)
            q = q_ref[0, rows, :]
            k = k_ref[0, rows, :]
            v = v_ref[0, rows, :]
            v1 = jnp.concatenate([v, ones], axis=1)
            if d == 0:
                s = lax.dot_general(q, k, (((1,), (1,)), ((), ())), preferred_element_type=F32)
                s_scr[rows, :] = s
            else:
                s = s_scr[rows, :]
            blk = cols_ref[rows, :]
            arg = _pick_col(blk, lane, base + N_HD * Q_C2) - r2_ref[0, :, rows]
            sw = s * jnp.exp2(jnp.where(mask, arg, NEG))
            nd = _dot(sw.astype(BF16), v1)
            if carry:
                nd = nd + _pick_col(blk, lane, base + N_HD * Q_WI) * _dot(q, cn.astype(BF16))
            hdir = nd[:, :HEAD_DIM] / jnp.maximum(jnp.abs(nd[:, HEAD_DIM:]), _pick_col(blk, lane, base + N_HD * Q_EM))
            if d == 0:
                h_scr[rows, :] = hdir
            else:
                h_scr[rows, :] = h_scr[rows, :] + hdir

            if carry and step == ML_NCH - 1:
                continue
            w_col = _pick_col(blk, lane, base + N_HD * Q_WC)
            if carry:
                u = lax.dot_general(k, (w_col * v1.astype(F32)).astype(BF16), tn_dims,
                                    preferred_element_type=F32)
                decay = jnp.sum(jnp.where(lane1 == base, cs_ref[c, 0:1, :], 0.0), axis=1, keepdims=True)
                cn = decay * cn + u
            else:
                w_row = (wrf_ref if d == 0 else wrb_ref)[0, :, rows]
                w8 = jnp.broadcast_to(w_row, (8, CHUNK)).astype(BF16)
                cst_ref[c, 0, d, 0] = lax.dot_general(k, (w_col * v.astype(F32)).astype(BF16), tn_dims,
                                                      preferred_element_type=F32)
                nst_ref[c, d, 0] = _dot(w8, k)[0:1, :]

    hh = h_scr[...]
    hn = hh * _rms_scale(hh) * gn_ref[0]
    ya_ref[0] = (hn * jax.nn.sigmoid(o_ref[0].astype(F32))).astype(BF16)


def _mlstm(p3, cols, rows, cs, gn3, state=None):
    n = p3.shape[1]
    nblk = n // ML_ROWS
    carry = state is not None

    def slab(base):
        return pl.BlockSpec((1, ML_ROWS, LANES), lambda g, h: (base + h, g, 0))

    def row(base):
        return pl.BlockSpec((1, 1, ML_ROWS), lambda g, h: (base + h, 0, g))

    in_specs = [
        slab(SLAB_Q), slab(SLAB_K), slab(SLAB_V), slab(SLAB_O),
        pl.BlockSpec((ML_ROWS, LANES), lambda g, h: (g, 0)),
        row(0), row(N_HEADS),
        pl.BlockSpec((1, 1, LANES), lambda g, h: (h, 0, 0)),
    ]
    args = [p3, p3, p3, p3, cols, rows, rows, gn3]
    ya_shape = jax.ShapeDtypeStruct((N_HEADS, n, LANES), BF16)
    ya_spec = pl.BlockSpec((1, ML_ROWS, LANES), lambda g, h: (h, g, 0))
    state_spec = pl.BlockSpec((1, 1, 2, 1, HEAD_DIM, HEAD_DIM), lambda g, h: (g, 0, 0, h, 0, 0))
    if carry:
        c0, n0 = state
        in_specs += [pl.BlockSpec((ML_NCH, 2, LANES), lambda g, h: (g, 0, 0)), state_spec, state_spec]
        args += [cs, c0, n0]
        out_shape = ya_shape
        out_specs = ya_spec
    else:
        in_specs += [row(N_HD), row(N_HD + N_HEADS)]
        args += [rows, rows]
        nseq = n // CHUNK
        out_shape = (ya_shape,
                     jax.ShapeDtypeStruct((nseq, 1, 2, N_HEADS, HEAD_DIM, HEAD_DIM), F32),
                     jax.ShapeDtypeStruct((nseq, 2, N_HEADS, 1, HEAD_DIM), F32))
        out_specs = (ya_spec,
                     pl.BlockSpec((ML_NCH, 1, 2, 1, HEAD_DIM, HEAD_DIM), lambda g, h: (g, 0, 0, h, 0, 0)),
                     pl.BlockSpec((ML_NCH, 2, 1, 1, HEAD_DIM), lambda g, h: (g, 0, h, 0, 0)))
    return pl.pallas_call(
        functools.partial(_mlstm_kernel, carry=carry),
        out_shape=out_shape,
        grid=(nblk, N_HEADS),
        in_specs=in_specs,
        out_specs=out_specs,
        scratch_shapes=[pltpu.VMEM((ML_ROWS, CHUNK), F32), pltpu.VMEM((ML_ROWS, HEAD_DIM), F32)],
        compiler_params=_cparams(("parallel", "parallel")),
        name="mlstm_latent" if carry else "mlstm_prompt",
    )(*args)


def _pool_kernel(u_ref, a_ref, ic_ref, wpg_ref, ps_ref, yb_ref, *, seq_len, n_seq, rows_2d):
    piece = seq_len if rows_2d is None else GRID_W
    n_piece = n_seq if rows_2d is None else rows_2d
    for g in range(N_GROUPS):
        u = jnp.concatenate([u_ref[g, i * piece:(i + 1) * piece, :] for i in range(n_piece)], axis=1)
        win = _dot(a_ref[g], u)
        if rows_2d is None:
            inv_cnt = jnp.concatenate([ic_ref[g]] * n_piece, axis=1)
        else:
            run = [jnp.zeros((piece, LANES), F32)]
            for r in range(rows_2d):
                run.append(run[-1] + win[:, r * LANES:(r + 1) * LANES])
            half = POOL_WINDOWS[g] // 2
            win = jnp.concatenate([run[min(r - half + POOL_WINDOWS[g], rows_2d)] - run[max(r - half, 0)]
                                   for r in range(rows_2d)], axis=1)
            inv_cnt = jnp.concatenate([ic_ref[g, r * piece:(r + 1) * piece, :] for r in range(rows_2d)], axis=1)
        p = (win * inv_cnt - u.astype(F32)).astype(BF16)
        p = jnp.concatenate([p[:, i * LANES:(i + 1) * LANES] for i in range(n_piece)], axis=0)
        yb_ref[g] = (_dot(p, wpg_ref[g]) * ps_ref[g]).astype(BF16)


def _pool(p3, amat, invc, wpg, pscale, seq_len, n_seq, rows_2d):
    n = p3.shape[1]
    rows = seq_len * n_seq
    return pl.pallas_call(
        functools.partial(_pool_kernel, seq_len=seq_len, n_seq=n_seq, rows_2d=rows_2d),
        out_shape=jax.ShapeDtypeStruct((N_GROUPS, n, LANES), BF16),
        grid=(n // rows,),
        in_specs=[
            pl.BlockSpec((N_GROUPS, rows, LANES), lambda i: (SLAB_POOL // N_GROUPS, i, 0)),
            _const_spec(amat.shape),
            _const_spec((N_GROUPS, seq_len, LANES)),
            _const_spec((N_GROUPS, LANES, LANES)),
            _const_spec((N_GROUPS, 1, LANES)),
        ],
        out_specs=pl.BlockSpec((N_GROUPS, rows, LANES), lambda i: (0, i, 0)),
        compiler_params=_cparams(("parallel",)),
        name="pool",
    )(p3, amat, invc, wpg, pscale)


def _window_matrix(size, w):
    idx = np.arange(size)
    lo = np.clip(idx - w // 2, 0, size)
    hi = np.clip(idx - w // 2 + w, 0, size)
    s = np.arange(size)[None, :]
    a = ((s >= lo[:, None]) & (s < hi[:, None])).astype(np.float32)
    return a, (hi - lo).astype(np.float32)


def _pool_constants(seq_len, rows_2d):
    mats, invs = [], []
    for w in POOL_WINDOWS:
        if rows_2d is None:
            a, cnt = _window_matrix(seq_len, w)
        else:
            _, cr = _window_matrix(rows_2d, w)
            a, cc = _window_matrix(GRID_W, w)
            cnt = np.kron(cr, cc)
        mats.append(a)
        invs.append(np.repeat((1.0 / cnt)[:, None], LANES, axis=1))
    return (jnp.asarray(np.stack(mats), dtype=BF16), jnp.asarray(np.stack(invs), dtype=F32))


MIX_TM = 512


def _mix_kernel(x_ref, mod_ref, ya_ref, yb_ref, ga_ref, gb_ref, wa_ref, wb_ref, wo_ref, x1_ref):
    ya = jnp.concatenate([ya_ref[i] for i in range(N_HEADS)], axis=1)
    yb = jnp.concatenate([yb_ref[i] for i in range(N_GROUPS)], axis=1)
    ga = jnp.concatenate([ga_ref[i] for i in range(8)], axis=1).astype(F32)
    gb = jnp.concatenate([gb_ref[i] for i in range(8)], axis=1).astype(F32)
    merged = jax.nn.sigmoid(ga) * _dot(ya, wa_ref[...]) + jax.nn.sigmoid(gb) * _dot(yb, wb_ref[...])
    out = _dot(merged.astype(BF16), wo_ref[...])
    x1_ref[...] = x_ref[...] + mod_ref[0, 2:3, :] * out


def _mix(x2d, mod3, mod_row_fn, ya, yb, p3, wa, wb, wo):
    n = x2d.shape[0]
    return pl.pallas_call(
        _mix_kernel,
        out_shape=jax.ShapeDtypeStruct((n, D_MODEL), F32),
        grid=(n // MIX_TM,),
        in_specs=[
            pl.BlockSpec((MIX_TM, D_MODEL), lambda i: (i, 0)),
            pl.BlockSpec((1, 6, D_MODEL), lambda i: (mod_row_fn(i * MIX_TM), 0, 0)),
            pl.BlockSpec((N_HEADS, MIX_TM, LANES), lambda i: (0, i, 0)),
            pl.BlockSpec((N_GROUPS, MIX_TM, LANES), lambda i: (0, i, 0)),
            pl.BlockSpec((8, MIX_TM, LANES), lambda i: (SLAB_GA // 8, i, 0)),
            pl.BlockSpec((8, MIX_TM, LANES), lambda i: (SLAB_GB // 8, i, 0)),
            _const_spec((D_MODEL, D_MODEL)),
            _const_spec((N_GROUPS * LANES, D_MODEL)),
            _const_spec((D_MODEL, D_MODEL)),
        ],
        out_specs=pl.BlockSpec((MIX_TM, D_MODEL), lambda i: (i, 0)),
        compiler_params=_cparams(("parallel",)),
        name="mix",
    )(x2d, mod3, ya, yb, p3, p3, wa, wb, wo)


MOE_TB = 1024
MOE_TILE = 160
MOE_NSUB = 2
MOE_HALF = 8
ROUTER_ROWS = 4 + N_EXPERTS


def _route_rows(lt):
    lg = [lt[i:i + 1] for i in range(N_GROUPS)]
    mx = jnp.maximum(jnp.maximum(lg[0], lg[1]), jnp.maximum(lg[2], lg[3]))
    p_top = 1.0 / (jnp.exp(lg[0] - mx) + jnp.exp(lg[1] - mx) + jnp.exp(lg[2] - mx) + jnp.exp(lg[3] - mx))
    gidx = jnp.zeros(lg[0].shape, jnp.int32)
    best = lg[0]
    for i in range(1, N_GROUPS):
        upd = lg[i] > best
        gidx = jnp.where(upd, i, gidx)
        best = jnp.where(upd, lg[i], best)

    def expert_row(g, e):
        r = N_GROUPS + g * EXPERTS_PER_GROUP + e
        return lt[r:r + 1]

    le = [jnp.where(gidx == 0, expert_row(0, e),
                    jnp.where(gidx == 1, expert_row(1, e),
                              jnp.where(gidx == 2, expert_row(2, e), expert_row(3, e))))
          for e in range(EXPERTS_PER_GROUP)]
    v1 = jnp.maximum(jnp.maximum(le[0], le[1]), jnp.maximum(le[2], le[3]))
    i1 = jnp.full(v1.shape, EXPERTS_PER_GROUP - 1, jnp.int32)
    for e in range(EXPERTS_PER_GROUP - 2, -1, -1):
        i1 = jnp.where(le[e] == v1, e, i1)
    le2 = [jnp.where(i1 == e, -jnp.inf, le[e]) for e in range(EXPERTS_PER_GROUP)]
    v2 = jnp.maximum(jnp.maximum(le2[0], le2[1]), jnp.maximum(le2[2], le2[3]))
    i2 = jnp.full(v2.shape, EXPERTS_PER_GROUP - 1, jnp.int32)
    for e in range(EXPERTS_PER_GROUP - 2, -1, -1):
        i2 = jnp.where(le2[e] == v2, e, i2)
    e2 = jnp.exp(v2 - v1)
    w1 = p_top / (1.0 + e2)
    w2 = p_top * e2 / (1.0 + e2)
    w_rows, sel_rows = [], []
    for g in range(N_GROUPS):
        for e in range(EXPERTS_PER_GROUP):
            first = jnp.where(gidx == g, jnp.where(i1 == e, 1.0, 0.0), 0.0)
            second = jnp.where(gidx == g, jnp.where(i2 == e, 1.0, 0.0), 0.0)
            w_rows.append(first * w1 + second * w2)
            sel_rows.append(first + second)
    return w_rows, sel_rows


def _moe_kernel(x1_ref, mod_a_ref, mod_b_ref, g2_ref, wr_hi_ref, wr_lo_ref, br_ref, before_ref,
                wg_ref, wu_ref, wd_ref, gf_ref, y_ref, h2_scr, rank_scr, w_scr, p_scr, y_scr, npass_scr):
    e = pl.program_id(1)
    mod_refs = (mod_a_ref, mod_b_ref)
    tn_dims = (((0,), (0,)), ((), ()))
    slot = lax.broadcasted_iota(jnp.int32, (MOE_TILE, MOE_TB), 0).astype(F32)

    @pl.when(e == 0)
    def _():
        for sb in range(MOE_NSUB):
            x1 = x1_ref[sb * MOE_TB:(sb + 1) * MOE_TB, :]
            mod_ref = mod_refs[sb]
            h2 = x1 * _rms_scale(x1) * g2_ref[...]
            h2 = h2 * (1.0 + mod_ref[0, 4:5, :]) + mod_ref[0, 3:4, :]
            h_hi, h_lo = _split_bf16(h2)
            h2_scr[sb] = h_hi
            wr_hi = wr_hi_ref[...]
            logits = _dot(h_hi, wr_hi) + _dot(h_lo, wr_hi) + _dot(h_hi, wr_lo_ref[...]) + br_ref[...]
            w_rows, sel_rows = _route_rows(logits.T)
            sel = jnp.concatenate(sel_rows, axis=0)
            rank = _dot(sel.astype(BF16), before_ref[...])
            rank = jnp.where(sel > 0.0, rank, -1.0)
            n_routed = _dot(sel.astype(BF16), jnp.ones((MOE_TB, LANES), BF16))
            for i in range(N_EXPERTS):
                rank_scr[sb * N_EXPERTS + i] = rank[i:i + 1]
                w_scr[sb * N_EXPERTS + i] = w_rows[i]
                n_i = n_routed[i, 0].astype(jnp.int32)
                passes = jnp.int32(0)
                for kk in range(-(-MOE_TB // MOE_TILE)):
                    passes = passes + (n_i > kk * MOE_TILE).astype(jnp.int32)
                npass_scr[sb * N_EXPERTS + i] = passes
        y_ref[...] = jnp.zeros_like(y_ref)

    half = lax.rem(e, MOE_HALF)

    def gather_rows(sb, first_slot):
        rank_row = rank_scr[sb * N_EXPERTS + e]
        hit = (slot + first_slot) == rank_row
        pb = jnp.where(hit, 1.0, 0.0).astype(BF16)
        w_slot = jnp.sum(jnp.where(hit, w_scr[sb * N_EXPERTS + e], 0.0), axis=1, keepdims=True)
        return pb, w_slot, _dot(pb, h2_scr[sb]).astype(BF16)

    def expert(xg, w_slot):
        a = _dot(xg, wg_ref[0].astype(BF16))
        u = _dot(xg, wu_ref[0].astype(BF16))
        act = (a * jax.nn.sigmoid(a) * u * w_slot).astype(BF16)
        return _dot(act, wd_ref[0].astype(BF16)).astype(BF16)

    first = [gather_rows(sb, 0.0) for sb in range(MOE_NSUB)]
    yv = expert(jnp.concatenate([f[2] for f in first], axis=0), jnp.concatenate([f[1] for f in first], axis=0))
    for sb in range(MOE_NSUB):
        y_scr[sb, pl.ds(pl.multiple_of(half * MOE_TILE, MOE_TILE), MOE_TILE), :] = \
            yv[sb * MOE_TILE:(sb + 1) * MOE_TILE]

    for sb in range(MOE_NSUB):
        def extra_pass(j, carry, sb=sb):
            pb, w_slot, xg = gather_rows(sb, (j * MOE_TILE).astype(F32))
            y_ref[sb * MOE_TB:(sb + 1) * MOE_TB, :] += lax.dot_general(
                pb, expert(xg, w_slot), tn_dims, preferred_element_type=F32)
            return carry

        lax.fori_loop(1, npass_scr[sb * N_EXPERTS + e], extra_pass, 0)

    @pl.when(half == MOE_HALF - 1)
    def _():
        for sb in range(MOE_NSUB):
            for i in range(MOE_HALF):
                rank_row = rank_scr[sb * N_EXPERTS + (e - (MOE_HALF - 1)) + i]
                p_scr[i * MOE_TILE:(i + 1) * MOE_TILE, :] = jnp.where(slot == rank_row, 1.0, 0.0).astype(BF16)
            sub_rows = slice(sb * MOE_TB, (sb + 1) * MOE_TB)
            y_ref[sub_rows, :] += lax.dot_general(p_scr[...], y_scr[sb], tn_dims, preferred_element_type=F32)

    @pl.when(e == N_EXPERTS - 1)
    def _():
        for sb in range(MOE_NSUB):
            sub_rows = slice(sb * MOE_TB, (sb + 1) * MOE_TB)
            x2 = x1_ref[sub_rows, :] + mod_refs[sb][0, 5:6, :] * y_ref[sub_rows, :]
            y_ref[sub_rows, :] = x2 * _rms_scale(x2) * gf_ref[...]


def _moe(x1, mod3, mod_row_fn, g_norm2, wr_hi, wr_lo, br, wg, wu, wd, g_final):
    n = x1.shape[0]
    rows = MOE_NSUB * MOE_TB
    t_i = lax.broadcasted_iota(jnp.int32, (MOE_TB, MOE_TB), 0)
    t_j = lax.broadcasted_iota(jnp.int32, (MOE_TB, MOE_TB), 1)
    before = (t_i < t_j).astype(BF16)
    return pl.pallas_call(
        _moe_kernel,
        out_shape=jax.ShapeDtypeStruct((n, D_MODEL), F32),
        grid=(n // rows, N_EXPERTS),
        in_specs=[
            pl.BlockSpec((rows, D_MODEL), lambda i, e: (i, 0), pipeline_mode=pl.Buffered(1)),
            pl.BlockSpec((1, 6, D_MODEL), lambda i, e: (mod_row_fn(i * rows), 0, 0)),
            pl.BlockSpec((1, 6, D_MODEL), lambda i, e: (mod_row_fn(i * rows + MOE_TB), 0, 0)),
            _const_spec((1, D_MODEL)),
            _const_spec((D_MODEL, LANES)),
            _const_spec((D_MODEL, LANES)),
            _const_spec((1, LANES)),
            _const_spec((MOE_TB, MOE_TB)),
            pl.BlockSpec((1, D_MODEL, D_EXPERT), lambda i, e: (e, 0, 0)),
            pl.BlockSpec((1, D_MODEL, D_EXPERT), lambda i, e: (e, 0, 0)),
            pl.BlockSpec((1, D_EXPERT, D_MODEL), lambda i, e: (e, 0, 0)),
            _const_spec((1, D_MODEL)),
        ],
        out_specs=pl.BlockSpec((rows, D_MODEL), lambda i, e: (i, 0)),
        scratch_shapes=[pltpu.VMEM((MOE_NSUB, MOE_TB, D_MODEL), BF16),
                        pltpu.VMEM((MOE_NSUB * N_EXPERTS, 1, MOE_TB), F32),
                        pltpu.VMEM((MOE_NSUB * N_EXPERTS, 1, MOE_TB), F32),
                        pltpu.VMEM((MOE_HALF * MOE_TILE, MOE_TB), BF16),
                        pltpu.VMEM((MOE_NSUB, MOE_HALF * MOE_TILE, D_MODEL), BF16),
                        pltpu.SMEM((MOE_NSUB * N_EXPERTS,), jnp.int32)],
        compiler_params=_cparams(("parallel", "arbitrary")),
        name="moe",
    )(x1, mod3, mod3, g_norm2, wr_hi, wr_lo, br, before, wg, wu, wd, g_final)


def _stream(x, mod3, mod_row_fn, seq_len, rows_2d, state, w):
    bsz, t, _ = x.shape
    n = bsz * t
    x2d = x.reshape(n, D_MODEL)
    p3, bcum, imb, amax = _in_projection(x2d, mod3, mod_row_fn, w["g_norm1"], w["wmain"], w["bmain"],
                                         w["colscale"], w["wg_hi"], w["wg_lo"], w["bg"])
    m0 = None if state is None else state[2]
    cols, rw, cs = _gate_prep(bcum, imb, amax, m0)
    rows = rw[:, :2 * N_HD].T.reshape(2 * N_HD, 1, n)
    ml = _mlstm(p3, cols, rows, cs, w["gn3"], None if state is None else state[:2])
    amat, invc = _pool_constants(seq_len, rows_2d)
    n_seq = max(1, 1024 // seq_len)
    yb = _pool(p3, amat, invc, w["wpg"], w["pscale"], seq_len, n_seq, rows_2d)
    ya = ml if state is not None else ml[0]
    x1 = _mix(x2d, mod3, mod_row_fn, ya, yb, p3, w["wa"], w["wb"], w["wo"])
    y = _moe(x1, mod3, mod_row_fn, w["g_norm2"], w["wr_hi"], w["wr_lo"], w["br"],
             w["wg"], w["wu"], w["wd"], w["g_final"])
    return y.reshape(bsz, t, D_MODEL), ml, cs


def kernel(x_prompt, x_sample, state_C, state_n, state_m, c, c_ctx, w_mod, b_mod, g_norm1, w_in, b_in, gn_gain, w_pool_grp, pool_scale, w_proj_a, w_proj_b, w_out, g_norm2, w_router_grp, b_router_grp, w_router_exp, b_router_exp, w_exp_gate, w_exp_up, w_exp_down, g_final):
    bp, tp, _ = x_prompt.shape
    bs, ts, _ = x_sample.shape
    assert w_mod.shape[0] == 1, "single trunk layer"
    assert tp == CHUNK and ts % CHUNK == 0 and ts == ML_ROWS

    cvec = jnp.zeros((8, D_MODEL), F32).at[0].set(c_ctx).at[1:1 + bs].set(c)
    mod3 = _modulation(cvec, w_mod[0], b_mod[0])

    wi, bi = w_in[0], b_in[0]
    d = D_MODEL
    o_q, o_k, o_v, o_o, o_g, o_p, o_ga, o_gb = 0, d, 2 * d, 3 * d, 4 * d, 4 * d + 32, 4 * d + 32 + 512, 4 * d + 32 + 512 + d
    order = [(o_q, d), (o_k, d), (o_v, d), (o_o, d), (o_ga, d), (o_gb, d), (o_p, 512)]
    wmain = _pack_columns(wi, order)
    bmain = jnp.concatenate([bi[s:s + l] for s, l in order]).reshape(1, D_MAIN)
    colscale = jnp.ones((D_MAIN,), F32).at[SLAB_K * LANES:(SLAB_K + N_HEADS) * LANES].set(HEAD_DIM ** -0.5)
    colscale = colscale.reshape(1, D_MAIN)
    gwt = _take_columns_t(wi, o_g, 32)
    gb = bi[o_g:o_g + 32]
    i_f, f_f, i_b, f_b = (slice(0, 8), slice(8, 16), slice(16, 24), slice(24, 32))
    pad_w = jnp.zeros((LANES - N_HD, d), F32)
    pad_b = jnp.zeros((LANES - N_HD,), F32)
    wg2 = jnp.concatenate([gwt[f_f], gwt[f_b], pad_w, gwt[i_f], gwt[i_b], pad_w], axis=0).T
    bg2 = jnp.concatenate([gb[f_f], gb[f_b], pad_b, gb[i_f], gb[i_b], pad_b]).reshape(1, 2 * LANES)
    wg_hi, wg_lo = _split_bf16(wg2)

    wr = jnp.concatenate([w_router_grp[0], w_router_exp[0].reshape(d, N_EXPERTS),
                          jnp.zeros((d, LANES - ROUTER_ROWS), F32)], axis=1)
    br = jnp.concatenate([b_router_grp[0], b_router_exp[0].reshape(N_EXPERTS),
                          jnp.zeros((LANES - ROUTER_ROWS,), F32)]).reshape(1, LANES)
    wr_hi, wr_lo = _split_bf16(wr)

    w = dict(
        g_norm1=g_norm1[0].reshape(1, d), wmain=wmain, bmain=bmain, colscale=colscale,
        wg_hi=wg_hi, wg_lo=wg_lo, bg=bg2,
        gn3=gn_gain[0].reshape(N_HEADS, 1, HEAD_DIM),
        wpg=w_pool_grp[0].astype(BF16), pscale=pool_scale[0].reshape(N_GROUPS, 1, LANES),
        wa=w_proj_a[0].astype(BF16), wb=w_proj_b[0].astype(BF16), wo=w_out[0].astype(BF16),
        g_norm2=g_norm2[0].reshape(1, d), wr_hi=wr_hi, wr_lo=wr_lo, br=br,
        wg=w_exp_gate[0], wu=w_exp_up[0], wd=w_exp_down[0],
        g_final=g_final.reshape(1, d),
    )

    y_prompt, (_, cst, nst), cs_prompt = _stream(x_prompt, mod3, lambda r: 0, tp, None, None, w)
    n_rep = jnp.broadcast_to(state_n[..., None], state_C.shape)
    m0 = jnp.pad(state_m[:, 0].reshape(bs, N_HD), ((0, 0), (0, LANES - N_HD))).reshape(bs, 1, LANES)
    y_sample, _, _ = _stream(x_sample, mod3, lambda r: 1 + r // ts, ts, ts // GRID_W, (state_C, n_rep, m0), w)

    new_c = cst
    new_n = nst.reshape(bp, 1, 2, N_HEADS, HEAD_DIM)
    new_m = cs_prompt[:, 1, :N_HD].reshape(bp, 1, 2, N_HEADS)
    return (y_prompt, y_sample, new_c, new_n, new_m)
```

```python
import functools

import numpy as np
import jax
import jax.numpy as jnp
from jax import lax
from jax.experimental import pallas as pl
from jax.experimental.pallas import tpu as pltpu
from jax.experimental.pallas import tpu_sc as plsc

F32 = jnp.float32
BF16 = jnp.bfloat16

D_MODEL = 1024
N_HEADS = 8
HEAD_DIM = 128
GRID_W = 64
POOL_WINDOWS = (2, 4, 8, 16)
N_GROUPS = 4
EXPERTS_PER_GROUP = 4
N_EXPERTS = 16
D_EXPERT = 512
EPS = 1e-6
NEG = -1e30

LANES = 128
CHUNK = 256
N_SLABS = 52
SLAB_Q, SLAB_K, SLAB_V, SLAB_O, SLAB_GA, SLAB_GB, SLAB_POOL = 0, 8, 16, 24, 32, 40, 48
D_MAIN = N_SLABS * LANES
VMEM_LIMIT = 56 * 1024 * 1024


def _cparams(sem):
    return pltpu.CompilerParams(dimension_semantics=sem, vmem_limit_bytes=VMEM_LIMIT)


def _const_spec(shape):
    nd = len(shape)
    return pl.BlockSpec(shape, lambda *_: (0,) * nd, pipeline_mode=pl.Buffered(1))


def _split_bf16(x):
    hi = x.astype(BF16)
    lo = (x - hi.astype(F32)).astype(BF16)
    return hi, lo


def _dot(a, b):
    return jnp.dot(a, b, preferred_element_type=F32)


def _dot3(a, b_hi, b_lo):
    a_hi, a_lo = _split_bf16(a)
    return _dot(a_hi, b_hi) + _dot(a_hi, b_lo) + _dot(a_lo, b_hi)


def _rms_scale(x):
    return lax.rsqrt(jnp.mean(x * x, axis=-1, keepdims=True) + EPS)


def _log_sigmoid(x):
    return jnp.minimum(x, 0.0) - jnp.log1p(jnp.exp(-jnp.abs(x)))


def _mod_kernel(c_ref, w_ref, b_ref, o_ref):
    c = c_ref[...]
    s = c * jax.nn.sigmoid(c)
    w_hi, w_lo = _split_bf16(w_ref[...])
    o_ref[...] = _dot3(s, w_hi, w_lo) + b_ref[...]


def _modulation(cvec, w_mod, b_mod):
    tn = 512
    out = pl.pallas_call(
        _mod_kernel,
        out_shape=jax.ShapeDtypeStruct((8, 6 * D_MODEL), F32),
        grid=(6 * D_MODEL // tn,),
        in_specs=[
            pl.BlockSpec((8, D_MODEL), lambda j: (0, 0)),
            pl.BlockSpec((D_MODEL, tn), lambda j: (0, j)),
            pl.BlockSpec((1, tn), lambda j: (0, j)),
        ],
        out_specs=pl.BlockSpec((8, tn), lambda j: (0, j)),
        compiler_params=_cparams(("parallel",)),
        name="modulation",
    )(cvec, w_mod, b_mod.reshape(1, -1))
    return out.reshape(8, 6, D_MODEL)


PACK_COLS = 512


def _pack_kernel(wt_ref, o_ref):
    o_ref[...] = wt_ref[...].T.astype(BF16)


def _pack_columns(w, order):
    k, n = w.shape
    starts = []
    for src, width in order:
        assert width % PACK_COLS == 0
        starts += [src + j for j in range(0, width, PACK_COLS)]
    n_out = len(starts) * PACK_COLS

    unit = 32
    assert all(s % unit == 0 for s in starts)

    def src_row(j):
        row = jnp.int32(starts[-1] // unit)
        for idx in range(len(starts) - 2, -1, -1):
            row = jnp.where(j == idx, starts[idx] // unit, row)
        return row * unit

    return pl.pallas_call(
        _pack_kernel,
        out_shape=jax.ShapeDtypeStruct((k, n_out), BF16),
        grid=(len(starts),),
        in_specs=[pl.BlockSpec((pl.Element(PACK_COLS), pl.Element(k)), lambda j: (src_row(j), 0))],
        out_specs=pl.BlockSpec((k, PACK_COLS), lambda j: (0, j)),
        compiler_params=_cparams(("parallel",)),
        name="pack_columns",
    )(w.T)


def _copy_kernel(w_ref, o_ref):
    o_ref[...] = w_ref[...]


def _take_columns_t(w, start, width):
    k = w.shape[0]
    return pl.pallas_call(
        _copy_kernel,
        out_shape=jax.ShapeDtypeStruct((width, k), w.dtype),
        grid=(1,),
        in_specs=[pl.BlockSpec((pl.Element(width), pl.Element(k)), lambda j: (start, 0))],
        out_specs=pl.BlockSpec((width, k), lambda j: (0, 0)),
        name="take_columns",
    )(w.T)


IN_TM = 512
IN_TN = 512
LOG2E = 1.4426950408889634
N_HD = 2 * N_HEADS


def _fwd_lanes(shape):
    lane = lax.broadcasted_iota(jnp.int32, shape, len(shape) - 1)
    return (lane & N_HEADS) == 0


def _inproj_kernel(x_ref, mod_ref, g_ref, w_ref, b_ref, cs_ref, wg_hi_ref, wg_lo_ref, bg_ref,
                   p3_ref, bcum_ref, imb_ref, amax_ref):
    x = x_ref[...]
    h = x * _rms_scale(x) * g_ref[...]
    h = h * (1.0 + mod_ref[0, 1:2, :]) + mod_ref[0, 0:1, :]
    hb = h.astype(BF16)
    for n in range(D_MAIN // IN_TN):
        sl = slice(n * IN_TN, (n + 1) * IN_TN)
        acc = (_dot(hb, w_ref[:, sl]) + b_ref[:, sl]) * cs_ref[:, sl]
        accb = acc.astype(BF16)
        for j in range(IN_TN // LANES):
            p3_ref[n * (IN_TN // LANES) + j] = accb[:, j * LANES:(j + 1) * LANES]

    g2 = _dot3(h, wg_hi_ref[...], wg_lo_ref[...]) + bg_ref[...]
    ls = _log_sigmoid(g2[:, :LANES])
    gi = g2[:, LANES:]
    row = lax.broadcasted_iota(jnp.int32, (CHUNK, LANES), 0)
    fwd = _fwd_lanes((CHUNK, LANES))
    for c in range(IN_TM // CHUNK):
        rows = slice(c * CHUNK, (c + 1) * CHUNK)
        cf = ls[rows]
        cb = cf
        s = 1
        while s < CHUNK:
            cf = cf + jnp.where(row >= s, pltpu.roll(cf, s, 0), 0.0)
            cb = cb + jnp.where(row < CHUNK - s, pltpu.roll(cb, CHUNK - s, 0), 0.0)
            s *= 2
        b = jnp.where(fwd, cf, cb)
        imb = gi[rows] - b
        mf = imb
        mb = imb
        s = 1
        while s < CHUNK:
            mf = jnp.maximum(mf, jnp.where(row >= s, pltpu.roll(mf, s, 0), -jnp.inf))
            mb = jnp.maximum(mb, jnp.where(row < CHUNK - s, pltpu.roll(mb, CHUNK - s, 0), -jnp.inf))
            s *= 2
        bcum_ref[rows, :] = b
        imb_ref[rows, :] = imb
        amax_ref[rows, :] = b + jnp.where(fwd, mf, mb)


def _in_projection(x2d, mod3, mod_row_fn, g_norm1, wmain, bmain, colscale, wg_hi, wg_lo, bg):
    n = x2d.shape[0]
    return pl.pallas_call(
        _inproj_kernel,
        out_shape=(jax.ShapeDtypeStruct((N_SLABS, n, LANES), BF16),) + (jax.ShapeDtypeStruct((n, LANES), F32),) * 3,
        grid=(n // IN_TM,),
        in_specs=[
            pl.BlockSpec((IN_TM, D_MODEL), lambda i: (i, 0)),
            pl.BlockSpec((1, 6, D_MODEL), lambda i: (mod_row_fn(i * IN_TM), 0, 0)),
            _const_spec((1, D_MODEL)),
            _const_spec((D_MODEL, D_MAIN)),
            _const_spec((1, D_MAIN)),
            _const_spec((1, D_MAIN)),
            _const_spec((D_MODEL, 2 * LANES)),
            _const_spec((D_MODEL, 2 * LANES)),
            _const_spec((1, 2 * LANES)),
        ],
        out_specs=(pl.BlockSpec((N_SLABS, IN_TM, LANES), lambda i: (0, i, 0)),)
        + (pl.BlockSpec((IN_TM, LANES), lambda i: (i, 0)),) * 3,
        compiler_params=_cparams(("parallel",)),
        name="in_projection",
    )(x2d, mod3, g_norm1, wmain, bmain, colscale, wg_hi, wg_lo, bg)


GP_ROWS = 1024
GP_NCH = GP_ROWS // CHUNK
Q_C2, Q_EM, Q_WI, Q_WC = 0, 1, 2, 3


def _gate_prep_kernel(*refs, carry):
    if carry:
        b_ref, imb_ref, a_ref, m0_ref, cols_ref, rw_ref, cs_ref = refs
    else:
        b_ref, imb_ref, a_ref, cols_ref, rw_ref, cs_ref = refs
    fwd1 = _fwd_lanes((1, LANES))
    lane = lax.broadcasted_iota(jnp.int32, (CHUNK, LANES), 1)
    neg = jnp.full((1, LANES), NEG, F32)

    def at_scan_end(ref, c):
        return jnp.where(fwd1, ref[(c + 1) * CHUNK - 1:(c + 1) * CHUNK, :], ref[c * CHUNK:c * CHUNK + 1, :])

    b_last = [at_scan_end(b_ref, c) for c in range(GP_NCH)]
    a_last = [at_scan_end(a_ref, c) for c in range(GP_NCH)]
    m_start = m0_ref[0] if carry else neg
    m_prev = [[None] * GP_NCH, [None] * GP_NCH]
    m_new = [[None] * GP_NCH, [None] * GP_NCH]
    for d, order in enumerate((range(GP_NCH), range(GP_NCH - 1, -1, -1))):
        m = m_start
        for c in order:
            m_prev[d][c] = m
            m_new[d][c] = jnp.maximum(b_last[c] + m, a_last[c])
            m = m_new[d][c] if carry else neg

    for c in range(GP_NCH):
        rows = slice(c * CHUNK, (c + 1) * CHUNK)
        mp = jnp.where(fwd1, m_prev[0][c], m_prev[1][c])
        mn = jnp.where(fwd1, m_new[0][c], m_new[1][c])
        b = b_ref[rows, :]
        imb = imb_ref[rows, :]
        inter = b + mp
        m_t = jnp.maximum(inter, a_ref[rows, :])
        c2 = (b - m_t) * LOG2E
        em = jnp.exp(-m_t)
        wc = jnp.exp(b_last[c] - mn + imb)
        packed = jnp.where(lane < N_HD, c2, pltpu.roll(em, N_HD * Q_EM, 1))
        if carry:
            wi = jnp.exp(inter - m_t)
            packed = jnp.where(lane < N_HD * Q_WI, packed, pltpu.roll(wi, N_HD * Q_WI, 1))
        cols_ref[rows, :] = jnp.where(lane < N_HD * Q_WC, packed, pltpu.roll(wc, N_HD * Q_WC, 1))
        rw_ref[rows, :] = jnp.where(lane < N_HD, imb * (-LOG2E), pltpu.roll(wc, N_HD, 1))
        cs_ref[c, 0:1, :] = jnp.exp(b_last[c] + mp - mn)
        cs_ref[c, 1:2, :] = mn


def _gate_prep(bcum, imb, amax, m0=None):
    n = bcum.shape[0]
    carry = m0 is not None
    blk = pl.BlockSpec((GP_ROWS, LANES), lambda i: (i, 0))
    in_specs = [blk, blk, blk]
    args = [bcum, imb, amax]
    if carry:
        in_specs.append(pl.BlockSpec((1, 1, LANES), lambda i: (i, 0, 0)))
        args.append(m0)
    return pl.pallas_call(
        functools.partial(_gate_prep_kernel, carry=carry),
        out_shape=(jax.ShapeDtypeStruct((n, LANES), F32), jax.ShapeDtypeStruct((n, LANES), F32),
                   jax.ShapeDtypeStruct((n // CHUNK, 2, LANES), F32)),
        grid=(n // GP_ROWS,),
        in_specs=in_specs,
        out_specs=(blk, blk, pl.BlockSpec((GP_NCH, 2, LANES), lambda i: (i, 0, 0))),
        compiler_params=_cparams(("parallel",)),
        name="gate_prep",
    )(*args)


ML_ROWS = 1024
ML_NCH = ML_ROWS // CHUNK


def _pick_col(blk, lane_iota, idx):
    return jnp.sum(jnp.where(lane_iota == idx, blk, 0.0), axis=1, keepdims=True)


def _mlstm_kernel(*refs, carry):
    if carry:
        (q_ref, k_ref, v_ref, o_ref, cols_ref, r2f_ref, r2b_ref, gn_ref, cs_ref, c0_ref, n0_ref,
         ya_ref, s_scr, h_scr) = refs
    else:
        (q_ref, k_ref, v_ref, o_ref, cols_ref, r2f_ref, r2b_ref, gn_ref, wrf_ref, wrb_ref,
         ya_ref, cst_ref, nst_ref, s_scr, h_scr) = refs
    head = pl.program_id(1)
    lane = lax.broadcasted_iota(jnp.int32, (CHUNK, LANES), 1)
    lane1 = lax.broadcasted_iota(jnp.int32, (1, LANES), 1)
    r_idx = lax.broadcasted_iota(jnp.int32, (CHUNK, CHUNK), 0)
    c_idx = lax.broadcasted_iota(jnp.int32, (CHUNK, CHUNK), 1)
    ones = jnp.ones((CHUNK, HEAD_DIM), BF16)
    tn_dims = (((0,), (0,)), ((), ()))

    for d in range(2):
        mask = (c_idx <= r_idx) if d == 0 else (c_idx >= r_idx)
        r2_ref = r2f_ref if d == 0 else r2b_ref
        base = N_HEADS * d + head
        if carry:
            cn = jnp.concatenate([c0_ref[0, 0, d, 0], n0_ref[0, 0, d, 0]], axis=1)
        order = range(ML_NCH) if d == 0 else range(ML_NCH - 1, -1, -1)
        for step, c in enumerate(order):
            rows = slice(c * CHUNK, (c + 1) * CHUNK)
            q = q_ref[0, rows, :]
            k = k_ref[0, rows, :]
            v = v_ref[0, rows, :]
            v1 = jnp.concatenate([v, ones], axis=1)
            if d == 0:
                s = lax.dot_general(q, k, (((1,), (1,)), ((), ())), preferred_element_type=F32)
                s_scr[rows, :] = s
            else:
                s = s_scr[rows, :]
            blk = cols_ref[rows, :]
            arg = _pick_col(blk, lane, base + N_HD * Q_C2) - r2_ref[0, :, rows]
            sw = s * jnp.exp2(jnp.where(mask, arg, NEG))
            nd = _dot(sw.astype(BF16), v1)
            if carry:
                nd = nd + _pick_col(blk, lane, base + N_HD * Q_WI) * _dot(q, cn.astype(BF16))
            hdir = nd[:, :HEAD_DIM] / jnp.maximum(jnp.abs(nd[:, HEAD_DIM:]), _pick_col(blk, lane, base + N_HD * Q_EM))
            if d == 0:
                h_scr[rows, :] = hdir
            else:
                h_scr[rows, :] = h_scr[rows, :] + hdir

            if carry and step == ML_NCH - 1:
                continue
            w_col = _pick_col(blk, lane, base + N_HD * Q_WC)
            if carry:
                u = lax.dot_general(k, (w_col * v1.astype(F32)).astype(BF16), tn_dims,
                                    preferred_element_type=F32)
                decay = jnp.sum(jnp.where(lane1 == base, cs_ref[c, 0:1, :], 0.0), axis=1, keepdims=True)
                cn = decay * cn + u
            else:
                w_row = (wrf_ref if d == 0 else wrb_ref)[0, :, rows]
                w8 = jnp.broadcast_to(w_row, (8, CHUNK)).astype(BF16)
                cst_ref[c, 0, d, 0] = lax.dot_general(k, (w_col * v.astype(F32)).astype(BF16), tn_dims,
                                                      preferred_element_type=F32)
                nst_ref[c, d, 0] = _dot(w8, k)[0:1, :]

    hh = h_scr[...]
    hn = hh * _rms_scale(hh) * gn_ref[0]
    ya_ref[0] = (hn * jax.nn.sigmoid(o_ref[0].astype(F32))).astype(BF16)


def _mlstm(p3, cols, rows, cs, gn3, state=None):
    n = p3.shape[1]
    nblk = n // ML_ROWS
    carry = state is not None

    def slab(base):
        return pl.BlockSpec((1, ML_ROWS, LANES), lambda g, h: (base + h, g, 0))

    def row(base):
        return pl.BlockSpec((1, 1, ML_ROWS), lambda g, h: (base + h, 0, g))

    in_specs = [
        slab(SLAB_Q), slab(SLAB_K), slab(SLAB_V), slab(SLAB_O),
        pl.BlockSpec((ML_ROWS, LANES), lambda g, h: (g, 0)),
        row(0), row(N_HEADS),
        pl.BlockSpec((1, 1, LANES), lambda g, h: (h, 0, 0)),
    ]
    args = [p3, p3, p3, p3, cols, rows, rows, gn3]
    ya_shape = jax.ShapeDtypeStruct((N_HEADS, n, LANES), BF16)
    ya_spec = pl.BlockSpec((1, ML_ROWS, LANES), lambda g, h: (h, g, 0))
    state_spec = pl.BlockSpec((1, 1, 2, 1, HEAD_DIM, HEAD_DIM), lambda g, h: (g, 0, 0, h, 0, 0))
    if carry:
        c0, n0 = state
        in_specs += [pl.BlockSpec((ML_NCH, 2, LANES), lambda g, h: (g, 0, 0)), state_spec, state_spec]
        args += [cs, c0, n0]
        out_shape = ya_shape
        out_specs = ya_spec
    else:
        in_specs += [row(N_HD), row(N_HD + N_HEADS)]
        args += [rows, rows]
        nseq = n // CHUNK
        out_shape = (ya_shape,
                     jax.ShapeDtypeStruct((nseq, 1, 2, N_HEADS, HEAD_DIM, HEAD_DIM), F32),
                     jax.ShapeDtypeStruct((nseq, 2, N_HEADS, 1, HEAD_DIM), F32))
        out_specs = (ya_spec,
                     pl.BlockSpec((ML_NCH, 1, 2, 1, HEAD_DIM, HEAD_DIM), lambda g, h: (g, 0, 0, h, 0, 0)),
                     pl.BlockSpec((ML_NCH, 2, 1, 1, HEAD_DIM), lambda g, h: (g, 0, h, 0, 0)))
    return pl.pallas_call(
        functools.partial(_mlstm_kernel, carry=carry),
        out_shape=out_shape,
        grid=(nblk, N_HEADS),
        in_specs=in_specs,
        out_specs=out_specs,
        scratch_shapes=[pltpu.VMEM((ML_ROWS, CHUNK), F32), pltpu.VMEM((ML_ROWS, HEAD_DIM), F32)],
        compiler_params=_cparams(("parallel", "parallel")),
        name="mlstm_latent" if carry else "mlstm_prompt",
    )(*args)


def _pool_kernel(u_ref, a_ref, ic_ref, wpg_ref, ps_ref, yb_ref, *, seq_len, n_seq, rows_2d):
    piece = seq_len if rows_2d is None else GRID_W
    n_piece = n_seq if rows_2d is None else rows_2d
    for g in range(N_GROUPS):
        u = jnp.concatenate([u_ref[g, i * piece:(i + 1) * piece, :] for i in range(n_piece)], axis=1)
        win = _dot(a_ref[g], u)
        if rows_2d is None:
            inv_cnt = jnp.concatenate([ic_ref[g]] * n_piece, axis=1)
        else:
            run = [jnp.zeros((piece, LANES), F32)]
            for r in range(rows_2d):
                run.append(run[-1] + win[:, r * LANES:(r + 1) * LANES])
            half = POOL_WINDOWS[g] // 2
            win = jnp.concatenate([run[min(r - half + POOL_WINDOWS[g], rows_2d)] - run[max(r - half, 0)]
                                   for r in range(rows_2d)], axis=1)
            inv_cnt = jnp.concatenate([ic_ref[g, r * piece:(r + 1) * piece, :] for r in range(rows_2d)], axis=1)
        p = (win * inv_cnt - u.astype(F32)).astype(BF16)
        p = jnp.concatenate([p[:, i * LANES:(i + 1) * LANES] for i in range(n_piece)], axis=0)
        yb_ref[g] = (_dot(p, wpg_ref[g]) * ps_ref[g]).astype(BF16)


def _pool(p3, amat, invc, wpg, pscale, seq_len, n_seq, rows_2d):
    n = p3.shape[1]
    rows = seq_len * n_seq
    return pl.pallas_call(
        functools.partial(_pool_kernel, seq_len=seq_len, n_seq=n_seq, rows_2d=rows_2d),
        out_shape=jax.ShapeDtypeStruct((N_GROUPS, n, LANES), BF16),
        grid=(n // rows,),
        in_specs=[
            pl.BlockSpec((N_GROUPS, rows, LANES), lambda i: (SLAB_POOL // N_GROUPS, i, 0)),
            _const_spec(amat.shape),
            _const_spec((N_GROUPS, seq_len, LANES)),
            _const_spec((N_GROUPS, LANES, LANES)),
            _const_spec((N_GROUPS, 1, LANES)),
        ],
        out_specs=pl.BlockSpec((N_GROUPS, rows, LANES), lambda i: (0, i, 0)),
        compiler_params=_cparams(("parallel",)),
        name="pool",
    )(p3, amat, invc, wpg, pscale)


def _window_matrix(size, w):
    idx = np.arange(size)
    lo = np.clip(idx - w // 2, 0, size)
    hi = np.clip(idx - w // 2 + w, 0, size)
    s = np.arange(size)[None, :]
    a = ((s >= lo[:, None]) & (s < hi[:, None])).astype(np.float32)
    return a, (hi - lo).astype(np.float32)


def _pool_constants(seq_len, rows_2d):
    mats, invs = [], []
    for w in POOL_WINDOWS:
        if rows_2d is None:
            a, cnt = _window_matrix(seq_len, w)
        else:
            _, cr = _window_matrix(rows_2d, w)
            a, cc = _window_matrix(GRID_W, w)
            cnt = np.kron(cr, cc)
        mats.append(a)
        invs.append(np.repeat((1.0 / cnt)[:, None], LANES, axis=1))
    return (jnp.asarray(np.stack(mats), dtype=BF16), jnp.asarray(np.stack(invs), dtype=F32))


MIX_TM = 512


def _mix_kernel(x_ref, mod_ref, ya_ref, yb_ref, ga_ref, gb_ref, wa_ref, wb_ref, wo_ref, x1_ref):
    ya = jnp.concatenate([ya_ref[i] for i in range(N_HEADS)], axis=1)
    yb = jnp.concatenate([yb_ref[i] for i in range(N_GROUPS)], axis=1)
    ga = jnp.concatenate([ga_ref[i] for i in range(8)], axis=1).astype(F32)
    gb = jnp.concatenate([gb_ref[i] for i in range(8)], axis=1).astype(F32)
    merged = jax.nn.sigmoid(ga) * _dot(ya, wa_ref[...]) + jax.nn.sigmoid(gb) * _dot(yb, wb_ref[...])
    out = _dot(merged.astype(BF16), wo_ref[...])
    x1_ref[...] = x_ref[...] + mod_ref[0, 2:3, :] * out


def _mix(x2d, mod3, mod_row_fn, ya, yb, p3, wa, wb, wo):
    n = x2d.shape[0]
    return pl.pallas_call(
        _mix_kernel,
        out_shape=jax.ShapeDtypeStruct((n, D_MODEL), F32),
        grid=(n // MIX_TM,),
        in_specs=[
            pl.BlockSpec((MIX_TM, D_MODEL), lambda i: (i, 0)),
            pl.BlockSpec((1, 6, D_MODEL), lambda i: (mod_row_fn(i * MIX_TM), 0, 0)),
            pl.BlockSpec((N_HEADS, MIX_TM, LANES), lambda i: (0, i, 0)),
            pl.BlockSpec((N_GROUPS, MIX_TM, LANES), lambda i: (0, i, 0)),
            pl.BlockSpec((8, MIX_TM, LANES), lambda i: (SLAB_GA // 8, i, 0)),
            pl.BlockSpec((8, MIX_TM, LANES), lambda i: (SLAB_GB // 8, i, 0)),
            _const_spec((D_MODEL, D_MODEL)),
            _const_spec((N_GROUPS * LANES, D_MODEL)),
            _const_spec((D_MODEL, D_MODEL)),
        ],
        out_specs=pl.BlockSpec((MIX_TM, D_MODEL), lambda i: (i, 0)),
        compiler_params=_cparams(("parallel",)),
        name="mix",
    )(x2d, mod3, ya, yb, p3, p3, wa, wb, wo)


ROUTE_TM = 1024
ROUTER_ROWS = 4 + N_EXPERTS
SLOT_TM = 256
SC_ROW = 256
SC_WIN = 128
HI_MASK = -65536


def _pack_bf16_pairs(x):
    bits = pltpu.bitcast(x, jnp.int32)
    half = D_MODEL // 2
    return jnp.bitwise_or(jnp.bitwise_and(lax.shift_right_logical(bits[:, :half], 16), 0xFFFF),
                          jnp.bitwise_and(bits[:, half:], HI_MASK))


def _unpack_bf16_pairs(lo_words, hi_words):
    def low(w):
        return pltpu.bitcast(lax.shift_left(w, 16), F32)

    def high(w):
        return pltpu.bitcast(jnp.bitwise_and(w, HI_MASK), F32)

    return jnp.concatenate([low(lo_words), low(hi_words), high(lo_words), high(hi_words)], axis=1).astype(BF16)


def _route_rows(lt):
    lg = [lt[i:i + 1] for i in range(N_GROUPS)]
    mx = jnp.maximum(jnp.maximum(lg[0], lg[1]), jnp.maximum(lg[2], lg[3]))
    p_top = 1.0 / (jnp.exp(lg[0] - mx) + jnp.exp(lg[1] - mx) + jnp.exp(lg[2] - mx) + jnp.exp(lg[3] - mx))
    gidx = jnp.zeros(lg[0].shape, jnp.int32)
    best = lg[0]
    for i in range(1, N_GROUPS):
        upd = lg[i] > best
        gidx = jnp.where(upd, i, gidx)
        best = jnp.where(upd, lg[i], best)

    def expert_row(g, e):
        r = N_GROUPS + g * EXPERTS_PER_GROUP + e
        return lt[r:r + 1]

    le = [jnp.where(gidx == 0, expert_row(0, e),
                    jnp.where(gidx == 1, expert_row(1, e),
                              jnp.where(gidx == 2, expert_row(2, e), expert_row(3, e))))
          for e in range(EXPERTS_PER_GROUP)]
    v1 = jnp.maximum(jnp.maximum(le[0], le[1]), jnp.maximum(le[2], le[3]))
    i1 = jnp.full(v1.shape, EXPERTS_PER_GROUP - 1, jnp.int32)
    for e in range(EXPERTS_PER_GROUP - 2, -1, -1):
        i1 = jnp.where(le[e] == v1, e, i1)
    le2 = [jnp.where(i1 == e, -jnp.inf, le[e]) for e in range(EXPERTS_PER_GROUP)]
    v2 = jnp.maximum(jnp.maximum(le2[0], le2[1]), jnp.maximum(le2[2], le2[3]))
    i2 = jnp.full(v2.shape, EXPERTS_PER_GROUP - 1, jnp.int32)
    for e in range(EXPERTS_PER_GROUP - 2, -1, -1):
        i2 = jnp.where(le2[e] == v2, e, i2)
    e2 = jnp.exp(v2 - v1)
    first = (gidx * EXPERTS_PER_GROUP + i1).astype(F32)
    second = (gidx * EXPERTS_PER_GROUP + i2).astype(F32)
    return first, second, p_top / (1.0 + e2), p_top * e2 / (1.0 + e2)


def _route_kernel(x1_ref, mod_ref, g2_ref, wr_hi_ref, wr_lo_ref, br_ref, h2_ref, info_ref):
    x1 = x1_ref[...]
    h2 = x1 * _rms_scale(x1) * g2_ref[...]
    h2 = h2 * (1.0 + mod_ref[0, 4:5, :]) + mod_ref[0, 3:4, :]
    h_hi, h_lo = _split_bf16(h2)
    packed = _pack_bf16_pairs(h_hi.astype(F32))
    h2_ref[0:ROUTE_TM, :] = packed[:, :SC_ROW]
    h2_ref[ROUTE_TM:2 * ROUTE_TM, :] = packed[:, SC_ROW:]
    wr_hi = wr_hi_ref[...]
    logits = _dot(h_hi, wr_hi) + _dot(h_lo, wr_hi) + _dot(h_hi, wr_lo_ref[...]) + br_ref[...]
    first, second, w1, w2 = _route_rows(logits.T)
    zero = jnp.zeros_like(w1)
    info_ref[...] = jnp.concatenate([first, second, w1, w2, zero, zero, zero, zero], axis=0)


def _route(x1, mod3, mod_row_fn, g_norm2, wr_hi, wr_lo, br):
    n = x1.shape[0]
    return pl.pallas_call(
        _route_kernel,
        out_shape=(jax.ShapeDtypeStruct((2 * n, SC_ROW), jnp.int32), jax.ShapeDtypeStruct((8, n), F32)),
        grid=(n // ROUTE_TM,),
        in_specs=[
            pl.BlockSpec((ROUTE_TM, D_MODEL), lambda i: (i, 0)),
            pl.BlockSpec((1, 6, D_MODEL), lambda i: (mod_row_fn(i * ROUTE_TM), 0, 0)),
            _const_spec((1, D_MODEL)),
            _const_spec((D_MODEL, LANES)),
            _const_spec((D_MODEL, LANES)),
            _const_spec((1, LANES)),
        ],
        out_specs=(pl.BlockSpec((2 * ROUTE_TM, SC_ROW), lambda i: (i, 0)),
                   pl.BlockSpec((8, ROUTE_TM), lambda i: (0, i))),
        compiler_params=_cparams(("parallel",)),
        name="route",
    )(x1, mod3, g_norm2, wr_hi, wr_lo, br)


def _plan_kernel(info_ref, before_ref, pos_ref, tmap_ref, rank_scr, count_scr, *, n_blocks):
    step = pl.program_id(0)
    expert_id = lax.broadcasted_iota(jnp.int32, (N_EXPERTS, ROUTE_TM), 0).astype(F32)

    @pl.when(step == 0)
    def _():
        count_scr[...] = jnp.zeros_like(count_scr)

    @pl.when(step < n_blocks)
    def _():
        cols = pl.ds(pl.multiple_of(step * ROUTE_TM, ROUTE_TM), ROUTE_TM)
        sel = jnp.where((info_ref[0:1, cols] == expert_id) | (info_ref[1:2, cols] == expert_id), 1.0, 0.0)
        selb = sel.astype(BF16)
        rank_scr[:, cols] = _dot(selb, before_ref[...]) + count_scr[:, 0:1]
        count_scr[...] += _dot(selb, jnp.ones((ROUTE_TM, LANES), BF16))

    @pl.when(step == n_blocks)
    def _():
        counts = count_scr[...]
        padded = jnp.floor((counts + (SLOT_TM - 1)) * (1.0 / SLOT_TM)) * SLOT_TM
        starts, ends, run = [], [], jnp.zeros((1, LANES), F32)
        for e in range(N_EXPERTS):
            starts.append(run)
            run = run + padded[e:e + 1]
            ends.append(run)
        n = pos_ref.shape[1]
        for k in range(2):
            chosen = info_ref[k:k + 1, :]
            slot = jnp.zeros((1, n), F32)
            for e in range(N_EXPERTS):
                slot = jnp.where(chosen == float(e), starts[e][:, 0:1] + rank_scr[e:e + 1, :], slot)
            pos_ref[k:k + 1, :] = slot.astype(jnp.int32)
        pos_ref[2:8, :] = jnp.zeros((6, n), jnp.int32)
        tile_start = lax.broadcasted_iota(jnp.int32, (1, LANES), 1).astype(F32) * SLOT_TM
        owner = jnp.zeros((1, LANES), F32)
        for e in range(N_EXPERTS - 1):
            owner = owner + jnp.where(ends[e] <= tile_start, 1.0, 0.0)
        used = jnp.where(tile_start < run, 1.0, 0.0)
        tmap_ref[...] = jnp.concatenate([owner, used] + [jnp.zeros((1, LANES), F32)] * 6, axis=0).astype(jnp.int32)


def _plan(info):
    n = info.shape[1]
    n_blocks = n // ROUTE_TM
    t_i = lax.broadcasted_iota(jnp.int32, (ROUTE_TM, ROUTE_TM), 0)
    t_j = lax.broadcasted_iota(jnp.int32, (ROUTE_TM, ROUTE_TM), 1)
    before = (t_i < t_j).astype(BF16)
    return pl.pallas_call(
        functools.partial(_plan_kernel, n_blocks=n_blocks),
        out_shape=(jax.ShapeDtypeStruct((8, n), jnp.int32), jax.ShapeDtypeStruct((8, LANES), jnp.int32)),
        grid=(n_blocks + 1,),
        in_specs=[_const_spec((8, n)), _const_spec((ROUTE_TM, ROUTE_TM))],
        out_specs=(pl.BlockSpec((8, n), lambda i: (0, 0)), pl.BlockSpec((8, LANES), lambda i: (0, 0))),
        scratch_shapes=[pltpu.VMEM((N_EXPERTS, n), F32), pltpu.VMEM((N_EXPERTS, LANES), F32)],
        compiler_params=_cparams(("arbitrary",)),
        name="plan",
    )(info, before)


def _sc_mesh():
    return plsc.VectorSubcoreMesh(core_axis_name="core", subcore_axis_name="subcore")


def _sc_scatter_rows(x, idx, n_out):
    n_idx = idx.shape[0]
    src_blocks = x.shape[0] // SC_WIN

    @pl.kernel(out_type=jax.ShapeDtypeStruct((n_out, SC_ROW), x.dtype), mesh=_sc_mesh(), scratch_types=[])
    def scatter_kernel(x_hbm, i_hbm, o_hbm):
        def body(x_vmem, i_vmem):
            pltpu.sync_copy(x_vmem, o_hbm.at[i_vmem.at[0]])

        pltpu.emit_pipeline(
            body,
            grid=(n_idx // SC_WIN,),
            in_specs=[pl.BlockSpec((SC_WIN, SC_ROW), index_map=lambda i: (i % src_blocks, 0)),
                      pl.BlockSpec((1, SC_WIN), index_map=lambda i: (0, i))],
            out_specs=[],
            core_axis_name=("core", "subcore"),
            dimension_semantics=(pltpu.PARALLEL,),
        )(x_hbm, i_hbm)

    return scatter_kernel(x, idx.reshape(1, n_idx))


def _sc_gather_rows(x, idx):
    n_idx = idx.shape[0]

    @pl.kernel(out_type=jax.ShapeDtypeStruct((n_idx, SC_ROW), x.dtype), mesh=_sc_mesh(), scratch_types=[])
    def gather_kernel(x_hbm, i_hbm, o_hbm):
        def body(i_vmem, o_vmem):
            pltpu.sync_copy(x_hbm.at[i_vmem.at[0]], o_vmem)

        pltpu.emit_pipeline(
            body,
            grid=(n_idx // SC_WIN,),
            in_specs=[pl.BlockSpec((1, SC_WIN), index_map=lambda i: (0, i))],
            out_specs=[pl.BlockSpec((SC_WIN, SC_ROW), index_map=lambda i: (i, 0))],
            core_axis_name=("core", "subcore"),
            dimension_semantics=(pltpu.PARALLEL,),
        )(i_hbm, o_hbm)

    return gather_kernel(x, idx.reshape(1, n_idx))


def _experts_kernel(tmap_ref, xs_ref, wg_ref, wu_ref, wd_ref, ys_ref, wg_scr, wu_scr, wd_scr):
    j = pl.program_id(0)

    @pl.when((j == 0) | (tmap_ref[0, j] != tmap_ref[0, jnp.maximum(j - 1, 0)]))
    def _():
        wg_scr[...] = wg_ref[0].astype(BF16)
        wu_scr[...] = wu_ref[0].astype(BF16)
        wd_scr[...] = wd_ref[0].astype(BF16)

    @pl.when(tmap_ref[1, j] > 0)
    def _():
        x = _unpack_bf16_pairs(xs_ref[0:SLOT_TM, :], xs_ref[SLOT_TM:2 * SLOT_TM, :])
        a = _dot(x, wg_scr[...])
        u = _dot(x, wu_scr[...])
        act = (a * jax.nn.sigmoid(a) * u).astype(BF16)
        y = _dot(act, wd_scr[...]).astype(BF16).astype(F32)
        packed = _pack_bf16_pairs(y)
        ys_ref[0:SLOT_TM, :] = packed[:, :SC_ROW]
        ys_ref[SLOT_TM:2 * SLOT_TM, :] = packed[:, SC_ROW:]


def _experts(xs, tmap, wg, wu, wd):
    n_tiles = xs.shape[0] // (2 * SLOT_TM)
    grid_spec = pltpu.PrefetchScalarGridSpec(
        num_scalar_prefetch=1,
        grid=(n_tiles,),
        in_specs=[
            pl.BlockSpec((2 * SLOT_TM, SC_ROW), lambda j, tm: (j, 0)),
            pl.BlockSpec((1, D_MODEL, D_EXPERT), lambda j, tm: (tm[0, j], 0, 0)),
            pl.BlockSpec((1, D_MODEL, D_EXPERT), lambda j, tm: (tm[0, j], 0, 0)),
            pl.BlockSpec((1, D_EXPERT, D_MODEL), lambda j, tm: (tm[0, j], 0, 0)),
        ],
        out_specs=pl.BlockSpec((2 * SLOT_TM, SC_ROW), lambda j, tm: (j, 0)),
        scratch_shapes=[pltpu.VMEM((D_MODEL, D_EXPERT), BF16), pltpu.VMEM((D_MODEL, D_EXPERT), BF16),
                        pltpu.VMEM((D_EXPERT, D_MODEL), BF16)],
    )
    return pl.pallas_call(
        _experts_kernel,
        out_shape=jax.ShapeDtypeStruct(xs.shape, jnp.int32),
        grid_spec=grid_spec,
        compiler_params=_cparams(("arbitrary",)),
        name="experts",
    )(tmap, xs, wg, wu, wd)


FIN_TM = 512


def _final_kernel(x1_ref, mod_ref, yg_ref, w_ref, gf_ref, y_ref):
    moe = None
    for k in range(2):
        base = 2 * k * FIN_TM
        yk = _unpack_bf16_pairs(yg_ref[base:base + FIN_TM, :], yg_ref[base + FIN_TM:base + 2 * FIN_TM, :])
        term = w_ref[:, k:k + 1] * yk.astype(F32)
        moe = term if moe is None else moe + term
    x2 = x1_ref[...] + mod_ref[0, 5:6, :] * moe
    y_ref[...] = x2 * _rms_scale(x2) * gf_ref[...]


def _final(x1, mod3, mod_row_fn, yg, w_cols, g_final):
    n = x1.shape[0]
    return pl.pallas_call(
        _final_kernel,
        out_shape=jax.ShapeDtypeStruct((n, D_MODEL), F32),
        grid=(n // FIN_TM,),
        in_specs=[
            pl.BlockSpec((FIN_TM, D_MODEL), lambda i: (i, 0)),
            pl.BlockSpec((1, 6, D_MODEL), lambda i: (mod_row_fn(i * FIN_TM), 0, 0)),
            pl.BlockSpec((4 * FIN_TM, SC_ROW), lambda i: (i, 0)),
            pl.BlockSpec((FIN_TM, 2), lambda i: (i, 0)),
            _const_spec((1, D_MODEL)),
        ],
        out_specs=pl.BlockSpec((FIN_TM, D_MODEL), lambda i: (i, 0)),
        compiler_params=_cparams(("parallel",)),
        name="final",
    )(x1, mod3, yg, w_cols, g_final)


def _half_row(pos, half):
    return (pos // SLOT_TM) * (2 * SLOT_TM) + half * SLOT_TM + pos % SLOT_TM


def _moe(x1, mod3, mod_row_fn, g_norm2, wr_hi, wr_lo, br, wg, wu, wd, g_final):
    n = x1.shape[0]
    h2p, info = _route(x1, mod3, mod_row_fn, g_norm2, wr_hi, wr_lo, br)
    pos, tmap = _plan(info)
    n_slots = 2 * n + N_EXPERTS * SLOT_TM
    pos1, pos2 = pos[0], pos[1]

    def route_order(p):
        return jnp.stack([_half_row(p, 0).reshape(n // ROUTE_TM, ROUTE_TM),
                          _half_row(p, 1).reshape(n // ROUTE_TM, ROUTE_TM)], axis=1).reshape(-1)

    xs = _sc_scatter_rows(h2p, jnp.concatenate([route_order(pos1), route_order(pos2)]), 2 * n_slots)
    ys = _experts(xs, tmap, wg, wu, wd)

    def fin_order(p):
        return jnp.stack([_half_row(p, 0).reshape(n // FIN_TM, FIN_TM),
                          _half_row(p, 1).reshape(n // FIN_TM, FIN_TM)], axis=1)

    gidx = jnp.stack([fin_order(pos1), fin_order(pos2)], axis=1).reshape(-1)
    yg = _sc_gather_rows(ys, gidx)
    w_cols = info[2:4].T
    return _final(x1, mod3, mod_row_fn, yg, w_cols, g_final)


def _stream(x, mod3, mod_row_fn, seq_len, rows_2d, state, w):
    bsz, t, _ = x.shape
    n = bsz * t
    x2d = x.reshape(n, D_MODEL)
    p3, bcum, imb, amax = _in_projection(x2d, mod3, mod_row_fn, w["g_norm1"], w["wmain"], w["bmain"],
                                         w["colscale"], w["wg_hi"], w["wg_lo"], w["bg"])
    m0 = None if state is None else state[2]
    cols, rw, cs = _gate_prep(bcum, imb, amax, m0)
    rows = rw[:, :2 * N_HD].T.reshape(2 * N_HD, 1, n)
    ml = _mlstm(p3, cols, rows, cs, w["gn3"], None if state is None else state[:2])
    amat, invc = _pool_constants(seq_len, rows_2d)
    n_seq = max(1, 1024 // seq_len)
    yb = _pool(p3, amat, invc, w["wpg"], w["pscale"], seq_len, n_seq, rows_2d)
    ya = ml if state is not None else ml[0]
    x1 = _mix(x2d, mod3, mod_row_fn, ya, yb, p3, w["wa"], w["wb"], w["wo"])
    y = _moe(x1, mod3, mod_row_fn, w["g_norm2"], w["wr_hi"], w["wr_lo"], w["br"],
             w["wg"], w["wu"], w["wd"], w["g_final"])
    return y.reshape(bsz, t, D_MODEL), ml, cs


def kernel(x_prompt, x_sample, state_C, state_n, state_m, c, c_ctx, w_mod, b_mod, g_norm1, w_in, b_in, gn_gain, w_pool_grp, pool_scale, w_proj_a, w_proj_b, w_out, g_norm2, w_router_grp, b_router_grp, w_router_exp, b_router_exp, w_exp_gate, w_exp_up, w_exp_down, g_final):
    bp, tp, _ = x_prompt.shape
    bs, ts, _ = x_sample.shape
    assert w_mod.shape[0] == 1, "single trunk layer"
    assert tp == CHUNK and ts % CHUNK == 0 and ts == ML_ROWS

    cvec = jnp.zeros((8, D_MODEL), F32).at[0].set(c_ctx).at[1:1 + bs].set(c)
    mod3 = _modulation(cvec, w_mod[0], b_mod[0])

    wi, bi = w_in[0], b_in[0]
    d = D_MODEL
    o_q, o_k, o_v, o_o, o_g, o_p, o_ga, o_gb = 0, d, 2 * d, 3 * d, 4 * d, 4 * d + 32, 4 * d + 32 + 512, 4 * d + 32 + 512 + d
    order = [(o_q, d), (o_k, d), (o_v, d), (o_o, d), (o_ga, d), (o_gb, d), (o_p, 512)]
    wmain = _pack_columns(wi, order)
    bmain = jnp.concatenate([bi[s:s + l] for s, l in order]).reshape(1, D_MAIN)
    colscale = jnp.ones((D_MAIN,), F32).at[SLAB_K * LANES:(SLAB_K + N_HEADS) * LANES].set(HEAD_DIM ** -0.5)
    colscale = colscale.reshape(1, D_MAIN)
    gwt = _take_columns_t(wi, o_g, 32)
    gb = bi[o_g:o_g + 32]
    i_f, f_f, i_b, f_b = (slice(0, 8), slice(8, 16), slice(16, 24), slice(24, 32))
    pad_w = jnp.zeros((LANES - N_HD, d), F32)
    pad_b = jnp.zeros((LANES - N_HD,), F32)
    wg2 = jnp.concatenate([gwt[f_f], gwt[f_b], pad_w, gwt[i_f], gwt[i_b], pad_w], axis=0).T
    bg2 = jnp.concatenate([gb[f_f], gb[f_b], pad_b, gb[i_f], gb[i_b], pad_b]).reshape(1, 2 * LANES)
    wg_hi, wg_lo = _split_bf16(wg2)

    wr = jnp.concatenate([w_router_grp[0], w_router_exp[0].reshape(d, N_EXPERTS),
                          jnp.zeros((d, LANES - ROUTER_ROWS), F32)], axis=1)
    br = jnp.concatenate([b_router_grp[0], b_router_exp[0].reshape(N_EXPERTS),
                          jnp.zeros((LANES - ROUTER_ROWS,), F32)]).reshape(1, LANES)
    wr_hi, wr_lo = _split_bf16(wr)

    w = dict(
        g_norm1=g_norm1[0].reshape(1, d), wmain=wmain, bmain=bmain, colscale=colscale,
        wg_hi=wg_hi, wg_lo=wg_lo, bg=bg2,
        gn3=gn_gain[0].reshape(N_HEADS, 1, HEAD_DIM),
        wpg=w_pool_grp[0].astype(BF16), pscale=pool_scale[0].reshape(N_GROUPS, 1, LANES),
        wa=w_proj_a[0].astype(BF16), wb=w_proj_b[0].astype(BF16), wo=w_out[0].astype(BF16),
        g_norm2=g_norm2[0].reshape(1, d), wr_hi=wr_hi, wr_lo=wr_lo, br=br,
        wg=w_exp_gate[0], wu=w_exp_up[0], wd=w_exp_down[0],
        g_final=g_final.reshape(1, d),
    )

    y_prompt, (_, cst, nst), cs_prompt = _stream(x_prompt, mod3, lambda r: 0, tp, None, None, w)
    n_rep = jnp.broadcast_to(state_n[..., None], state_C.shape)
    m0 = jnp.pad(state_m[:, 0].reshape(bs, N_HD), ((0, 0), (0, LANES - N_HD))).reshape(bs, 1, LANES)
    y_sample, _, _ = _stream(x_sample, mod3, lambda r: 1 + r // ts, ts, ts // GRID_W, (state_C, n_rep, m0), w)

    new_c = cst
    new_n = nst.reshape(bp, 1, 2, N_HEADS, HEAD_DIM)
    new_m = cs_prompt[:, 1, :N_HD].reshape(bp, 1, 2, N_HEADS)
    return (y_prompt, y_sample, new_c, new_n, new_m)
```

```python
import functools

import numpy as np
import jax
import jax.numpy as jnp
from jax import lax
from jax.experimental import pallas as pl
from jax.experimental.pallas import tpu as pltpu
from jax.experimental.pallas import tpu_sc as plsc

F32 = jnp.float32
BF16 = jnp.bfloat16

D_MODEL = 1024
N_HEADS = 8
HEAD_DIM = 128
GRID_W = 64
POOL_WINDOWS = (2, 4, 8, 16)
N_GROUPS = 4
EXPERTS_PER_GROUP = 4
N_EXPERTS = 16
D_EXPERT = 512
EPS = 1e-6
NEG = -1e30

LANES = 128
CHUNK = 256
N_SLABS = 52
SLAB_Q, SLAB_K, SLAB_V, SLAB_O, SLAB_GA, SLAB_GB, SLAB_POOL = 0, 8, 16, 24, 32, 40, 48
D_MAIN = N_SLABS * LANES
VMEM_LIMIT = 56 * 1024 * 1024


def _cparams(sem):
    return pltpu.CompilerParams(dimension_semantics=sem, vmem_limit_bytes=VMEM_LIMIT)


def _const_spec(shape):
    nd = len(shape)
    return pl.BlockSpec(shape, lambda *_: (0,) * nd, pipeline_mode=pl.Buffered(1))


def _split_bf16(x):
    hi = x.astype(BF16)
    lo = (x - hi.astype(F32)).astype(BF16)
    return hi, lo


def _dot(a, b):
    return jnp.dot(a, b, preferred_element_type=F32)


def _dot3(a, b_hi, b_lo):
    a_hi, a_lo = _split_bf16(a)
    return _dot(a_hi, b_hi) + _dot(a_hi, b_lo) + _dot(a_lo, b_hi)


def _rms_scale(x):
    return lax.rsqrt(jnp.mean(x * x, axis=-1, keepdims=True) + EPS)


def _log_sigmoid(x):
    return jnp.minimum(x, 0.0) - jnp.log1p(jnp.exp(-jnp.abs(x)))


def _mod_kernel(c_ref, w_ref, b_ref, o_ref):
    c = c_ref[...]
    s = c * jax.nn.sigmoid(c)
    w_hi, w_lo = _split_bf16(w_ref[...])
    o_ref[...] = _dot3(s, w_hi, w_lo) + b_ref[...]


def _modulation(cvec, w_mod, b_mod):
    tn = 512
    out = pl.pallas_call(
        _mod_kernel,
        out_shape=jax.ShapeDtypeStruct((8, 6 * D_MODEL), F32),
        grid=(6 * D_MODEL // tn,),
        in_specs=[
            pl.BlockSpec((8, D_MODEL), lambda j: (0, 0)),
            pl.BlockSpec((D_MODEL, tn), lambda j: (0, j)),
            pl.BlockSpec((1, tn), lambda j: (0, j)),
        ],
        out_specs=pl.BlockSpec((8, tn), lambda j: (0, j)),
        compiler_params=_cparams(("parallel",)),
        name="modulation",
    )(cvec, w_mod, b_mod.reshape(1, -1))
    return out.reshape(8, 6, D_MODEL)


PACK_COLS = 512


def _pack_kernel(wt_ref, o_ref):
    o_ref[...] = wt_ref[...].T.astype(BF16)


def _pack_columns(w, order):
    k, n = w.shape
    starts = []
    for src, width in order:
        assert width % PACK_COLS == 0
        starts += [src + j for j in range(0, width, PACK_COLS)]
    n_out = len(starts) * PACK_COLS

    unit = 32
    assert all(s % unit == 0 for s in starts)

    def src_row(j):
        row = jnp.int32(starts[-1] // unit)
        for idx in range(len(starts) - 2, -1, -1):
            row = jnp.where(j == idx, starts[idx] // unit, row)
        return row * unit

    return pl.pallas_call(
        _pack_kernel,
        out_shape=jax.ShapeDtypeStruct((k, n_out), BF16),
        grid=(len(starts),),
        in_specs=[pl.BlockSpec((pl.Element(PACK_COLS), pl.Element(k)), lambda j: (src_row(j), 0))],
        out_specs=pl.BlockSpec((k, PACK_COLS), lambda j: (0, j)),
        compiler_params=_cparams(("parallel",)),
        name="pack_columns",
    )(w.T)


def _copy_kernel(w_ref, o_ref):
    o_ref[...] = w_ref[...]


def _take_columns_t(w, start, width):
    k = w.shape[0]
    return pl.pallas_call(
        _copy_kernel,
        out_shape=jax.ShapeDtypeStruct((width, k), w.dtype),
        grid=(1,),
        in_specs=[pl.BlockSpec((pl.Element(width), pl.Element(k)), lambda j: (start, 0))],
        out_specs=pl.BlockSpec((width, k), lambda j: (0, 0)),
        name="take_columns",
    )(w.T)


IN_TM = 512
IN_TN = 512
LOG2E = 1.4426950408889634
N_HD = 2 * N_HEADS


def _fwd_lanes(shape):
    lane = lax.broadcasted_iota(jnp.int32, shape, len(shape) - 1)
    return (lane & N_HEADS) == 0


def _inproj_kernel(x_ref, mod_ref, g_ref, w_ref, b_ref, cs_ref, wg_hi_ref, wg_lo_ref, bg_ref,
                   p3_ref, bcum_ref, imb_ref, amax_ref):
    x = x_ref[...]
    h = x * _rms_scale(x) * g_ref[...]
    h = h * (1.0 + mod_ref[0, 1:2, :]) + mod_ref[0, 0:1, :]
    hb = h.astype(BF16)
    for n in range(D_MAIN // IN_TN):
        sl = slice(n * IN_TN, (n + 1) * IN_TN)
        acc = (_dot(hb, w_ref[:, sl]) + b_ref[:, sl]) * cs_ref[:, sl]
        accb = acc.astype(BF16)
        for j in range(IN_TN // LANES):
            p3_ref[n * (IN_TN // LANES) + j] = accb[:, j * LANES:(j + 1) * LANES]

    g2 = _dot3(h, wg_hi_ref[...], wg_lo_ref[...]) + bg_ref[...]
    ls = _log_sigmoid(g2[:, :LANES])
    gi = g2[:, LANES:]
    row = lax.broadcasted_iota(jnp.int32, (CHUNK, LANES), 0)
    fwd = _fwd_lanes((CHUNK, LANES))
    for c in range(IN_TM // CHUNK):
        rows = slice(c * CHUNK, (c + 1) * CHUNK)
        cf = ls[rows]
        cb = cf
        s = 1
        while s < CHUNK:
            cf = cf + jnp.where(row >= s, pltpu.roll(cf, s, 0), 0.0)
            cb = cb + jnp.where(row < CHUNK - s, pltpu.roll(cb, CHUNK - s, 0), 0.0)
            s *= 2
        b = jnp.where(fwd, cf, cb)
        imb = gi[rows] - b
        mf = imb
        mb = imb
        s = 1
        while s < CHUNK:
            mf = jnp.maximum(mf, jnp.where(row >= s, pltpu.roll(mf, s, 0), -jnp.inf))
            mb = jnp.maximum(mb, jnp.where(row < CHUNK - s, pltpu.roll(mb, CHUNK - s, 0), -jnp.inf))
            s *= 2
        bcum_ref[rows, :] = b
        imb_ref[rows, :] = imb
        amax_ref[rows, :] = b + jnp.where(fwd, mf, mb)


def _in_projection(x2d, mod3, mod_row_fn, g_norm1, wmain, bmain, colscale, wg_hi, wg_lo, bg):
    n = x2d.shape[0]
    return pl.pallas_call(
        _inproj_kernel,
        out_shape=(jax.ShapeDtypeStruct((N_SLABS, n, LANES), BF16),) + (jax.ShapeDtypeStruct((n, LANES), F32),) * 3,
        grid=(n // IN_TM,),
        in_specs=[
            pl.BlockSpec((IN_TM, D_MODEL), lambda i: (i, 0)),
            pl.BlockSpec((1, 6, D_MODEL), lambda i: (mod_row_fn(i * IN_TM), 0, 0)),
            _const_spec((1, D_MODEL)),
            _const_spec((D_MODEL, D_MAIN)),
            _const_spec((1, D_MAIN)),
            _const_spec((1, D_MAIN)),
            _const_spec((D_MODEL, 2 * LANES)),
            _const_spec((D_MODEL, 2 * LANES)),
            _const_spec((1, 2 * LANES)),
        ],
        out_specs=(pl.BlockSpec((N_SLABS, IN_TM, LANES), lambda i: (0, i, 0)),)
        + (pl.BlockSpec((IN_TM, LANES), lambda i: (i, 0)),) * 3,
        compiler_params=_cparams(("parallel",)),
        name="in_projection",
    )(x2d, mod3, g_norm1, wmain, bmain, colscale, wg_hi, wg_lo, bg)


GP_ROWS = 1024
GP_NCH = GP_ROWS // CHUNK
Q_C2, Q_EM, Q_WI, Q_WC = 0, 1, 2, 3


def _gate_prep_kernel(*refs, carry):
    if carry:
        b_ref, imb_ref, a_ref, m0_ref, cols_ref, rw_ref, cs_ref = refs
    else:
        b_ref, imb_ref, a_ref, cols_ref, rw_ref, cs_ref = refs
    fwd1 = _fwd_lanes((1, LANES))
    lane = lax.broadcasted_iota(jnp.int32, (CHUNK, LANES), 1)
    neg = jnp.full((1, LANES), NEG, F32)

    def at_scan_end(ref, c):
        return jnp.where(fwd1, ref[(c + 1) * CHUNK - 1:(c + 1) * CHUNK, :], ref[c * CHUNK:c * CHUNK + 1, :])

    b_last = [at_scan_end(b_ref, c) for c in range(GP_NCH)]
    a_last = [at_scan_end(a_ref, c) for c in range(GP_NCH)]
    m_start = m0_ref[0] if carry else neg
    m_prev = [[None] * GP_NCH, [None] * GP_NCH]
    m_new = [[None] * GP_NCH, [None] * GP_NCH]
    for d, order in enumerate((range(GP_NCH), range(GP_NCH - 1, -1, -1))):
        m = m_start
        for c in order:
            m_prev[d][c] = m
            m_new[d][c] = jnp.maximum(b_last[c] + m, a_last[c])
            m = m_new[d][c] if carry else neg

    for c in range(GP_NCH):
        rows = slice(c * CHUNK, (c + 1) * CHUNK)
        mp = jnp.where(fwd1, m_prev[0][c], m_prev[1][c])
        mn = jnp.where(fwd1, m_new[0][c], m_new[1][c])
        b = b_ref[rows, :]
        imb = imb_ref[rows, :]
        inter = b + mp
        m_t = jnp.maximum(inter, a_ref[rows, :])
        c2 = (b - m_t) * LOG2E
        em = jnp.exp(-m_t)
        wc = jnp.exp(b_last[c] - mn + imb)
        packed = jnp.where(lane < N_HD, c2, pltpu.roll(em, N_HD * Q_EM, 1))
        if carry:
            wi = jnp.exp(inter - m_t)
            packed = jnp.where(lane < N_HD * Q_WI, packed, pltpu.roll(wi, N_HD * Q_WI, 1))
        cols_ref[rows, :] = jnp.where(lane < N_HD * Q_WC, packed, pltpu.roll(wc, N_HD * Q_WC, 1))
        rw_ref[rows, :] = jnp.where(lane < N_HD, imb * (-LOG2E), pltpu.roll(wc, N_HD, 1))
        cs_ref[c, 0:1, :] = jnp.exp(b_last[c] + mp - mn)
        cs_ref[c, 1:2, :] = mn


def _gate_prep(bcum, imb, amax, m0=None):
    n = bcum.shape[0]
    carry = m0 is not None
    blk = pl.BlockSpec((GP_ROWS, LANES), lambda i: (i, 0))
    in_specs = [blk, blk, blk]
    args = [bcum, imb, amax]
    if carry:
        in_specs.append(pl.BlockSpec((1, 1, LANES), lambda i: (i, 0, 0)))
        args.append(m0)
    return pl.pallas_call(
        functools.partial(_gate_prep_kernel, carry=carry),
        out_shape=(jax.ShapeDtypeStruct((n, LANES), F32), jax.ShapeDtypeStruct((n, LANES), F32),
                   jax.ShapeDtypeStruct((n // CHUNK, 2, LANES), F32)),
        grid=(n // GP_ROWS,),
        in_specs=in_specs,
        out_specs=(blk, blk, pl.BlockSpec((GP_NCH, 2, LANES), lambda i: (i, 0, 0))),
        compiler_params=_cparams(("parallel",)),
        name="gate_prep",
    )(*args)


ML_ROWS = 1024
ML_NCH = ML_ROWS // CHUNK


def _pick_col(blk, lane_iota, idx):
    return jnp.sum(jnp.where(lane_iota == idx, blk, 0.0), axis=1, keepdims=True)


def _mlstm_kernel(*refs, carry):
    if carry:
        (q_ref, k_ref, v_ref, o_ref, cols_ref, r2f_ref, r2b_ref, gn_ref, cs_ref, c0_ref, n0_ref,
         ya_ref, s_scr, h_scr) = refs
    else:
        (q_ref, k_ref, v_ref, o_ref, cols_ref, r2f_ref, r2b_ref, gn_ref, wrf_ref, wrb_ref,
         ya_ref, cst_ref, nst_ref, s_scr, h_scr) = refs
    head = pl.program_id(1)
    lane = lax.broadcasted_iota(jnp.int32, (CHUNK, LANES), 1)
    lane1 = lax.broadcasted_iota(jnp.int32, (1, LANES), 1)
    r_idx = lax.broadcasted_iota(jnp.int32, (CHUNK, CHUNK), 0)
    c_idx = lax.broadcasted_iota(jnp.int32, (CHUNK, CHUNK), 1)
    ones = jnp.ones((CHUNK, HEAD_DIM), BF16)
    tn_dims = (((0,), (0,)), ((), ()))

    for d in range(2):
        mask = (c_idx <= r_idx) if d == 0 else (c_idx >= r_idx)
        r2_ref = r2f_ref if d == 0 else r2b_ref
        base = N_HEADS * d + head
        if carry:
            cn = jnp.concatenate([c0_ref[0, 0, d, 0], n0_ref[0, 0, d, 0]], axis=1)
        order = range(ML_NCH) if d == 0 else range(ML_NCH - 1, -1, -1)
        for step, c in enumerate(order):
            rows = slice(c * CHUNK, (c + 1) * CHUNK)
            q = q_ref[0, rows, :]
            k = k_ref[0, rows, :]
            v = v_ref[0, rows, :]
            v1 = jnp.concatenate([v, ones], axis=1)
            if d == 0:
                s = lax.dot_general(q, k, (((1,), (1,)), ((), ())), preferred_element_type=F32)
                s_scr[rows, :] = s
            else:
                s = s_scr[rows, :]
            blk = cols_ref[rows, :]
            arg = _pick_col(blk, lane, base + N_HD * Q_C2) - r2_ref[0, :, rows]
            sw = s * jnp.exp2(jnp.where(mask, arg, NEG))
            nd = _dot(sw.astype(BF16), v1)
            if carry:
                nd = nd + _pick_col(blk, lane, base + N_HD * Q_WI) * _dot(q, cn.astype(BF16))
            hdir = nd[:, :HEAD_DIM] / jnp.maximum(jnp.abs(nd[:, HEAD_DIM:]), _pick_col(blk, lane, base + N_HD * Q_EM))
            if d == 0:
                h_scr[rows, :] = hdir
            else:
                h_scr[rows, :] = h_scr[rows, :] + hdir

            if carry and step == ML_NCH - 1:
                continue
            w_col = _pick_col(blk, lane, base + N_HD * Q_WC)
            if carry:
                u = lax.dot_general(k, (w_col * v1.astype(F32)).astype(BF16), tn_dims,
                                    preferred_element_type=F32)
                decay = jnp.sum(jnp.where(lane1 == base, cs_ref[c, 0:1, :], 0.0), axis=1, keepdims=True)
                cn = decay * cn + u
            else:
                w_row = (wrf_ref if d == 0 else wrb_ref)[0, :, rows]
                w8 = jnp.broadcast_to(w_row, (8, CHUNK)).astype(BF16)
                cst_ref[c, 0, d, 0] = lax.dot_general(k, (w_col * v.astype(F32)).astype(BF16), tn_dims,
                                                      preferred_element_type=F32)
                nst_ref[c, d, 0] = _dot(w8, k)[0:1, :]

    hh = h_scr[...]
    hn = hh * _rms_scale(hh) * gn_ref[0]
    ya_ref[0] = (hn * jax.nn.sigmoid(o_ref[0].astype(F32))).astype(BF16)


def _mlstm(p3, cols, rows, cs, gn3, state=None):
    n = p3.shape[1]
    nblk = n // ML_ROWS
    carry = state is not None

    def slab(base):
        return pl.BlockSpec((1, ML_ROWS, LANES), lambda g, h: (base + h, g, 0))

    def row(base):
        return pl.BlockSpec((1, 1, ML_ROWS), lambda g, h: (base + h, 0, g))

    in_specs = [
        slab(SLAB_Q), slab(SLAB_K), slab(SLAB_V), slab(SLAB_O),
        pl.BlockSpec((ML_ROWS, LANES), lambda g, h: (g, 0)),
        row(0), row(N_HEADS),
        pl.BlockSpec((1, 1, LANES), lambda g, h: (h, 0, 0)),
    ]
    args = [p3, p3, p3, p3, cols, rows, rows, gn3]
    ya_shape = jax.ShapeDtypeStruct((N_HEADS, n, LANES), BF16)
    ya_spec = pl.BlockSpec((1, ML_ROWS, LANES), lambda g, h: (h, g, 0))
    state_spec = pl.BlockSpec((1, 1, 2, 1, HEAD_DIM, HEAD_DIM), lambda g, h: (g, 0, 0, h, 0, 0))
    if carry:
        c0, n0 = state
        in_specs += [pl.BlockSpec((ML_NCH, 2, LANES), lambda g, h: (g, 0, 0)), state_spec, state_spec]
        args += [cs, c0, n0]
        out_shape = ya_shape
        out_specs = ya_spec
    else:
        in_specs += [row(N_HD), row(N_HD + N_HEADS)]
        args += [rows, rows]
        nseq = n // CHUNK
        out_shape = (ya_shape,
                     jax.ShapeDtypeStruct((nseq, 1, 2, N_HEADS, HEAD_DIM, HEAD_DIM), F32),
                     jax.ShapeDtypeStruct((nseq, 2, N_HEADS, 1, HEAD_DIM), F32))
        out_specs = (ya_spec,
                     pl.BlockSpec((ML_NCH, 1, 2, 1, HEAD_DIM, HEAD_DIM), lambda g, h: (g, 0, 0, h, 0, 0)),
                     pl.BlockSpec((ML_NCH, 2, 1, 1, HEAD_DIM), lambda g, h: (g, 0, h, 0, 0)))
    return pl.pallas_call(
        functools.partial(_mlstm_kernel, carry=carry),
        out_shape=out_shape,
        grid=(nblk, N_HEADS),
        in_specs=in_specs,
        out_specs=out_specs,
        scratch_shapes=[pltpu.VMEM((ML_ROWS, CHUNK), F32), pltpu.VMEM((ML_ROWS, HEAD_DIM), F32)],
        compiler_params=_cparams(("parallel", "parallel")),
        name="mlstm_latent" if carry else "mlstm_prompt",
    )(*args)


def _pool_kernel(u_ref, a_ref, ic_ref, wpg_ref, ps_ref, yb_ref, *, seq_len, n_seq, rows_2d):
    piece = seq_len if rows_2d is None else GRID_W
    n_piece = n_seq if rows_2d is None else rows_2d
    for g in range(N_GROUPS):
        u = jnp.concatenate([u_ref[g, i * piece:(i + 1) * piece, :] for i in range(n_piece)], axis=1)
        win = _dot(a_ref[g], u)
        if rows_2d is None:
            inv_cnt = jnp.concatenate([ic_ref[g]] * n_piece, axis=1)
        else:
            run = [jnp.zeros((piece, LANES), F32)]
            for r in range(rows_2d):
                run.append(run[-1] + win[:, r * LANES:(r + 1) * LANES])
            half = POOL_WINDOWS[g] // 2
            win = jnp.concatenate([run[min(r - half + POOL_WINDOWS[g], rows_2d)] - run[max(r - half, 0)]
                                   for r in range(rows_2d)], axis=1)
            inv_cnt = jnp.concatenate([ic_ref[g, r * piece:(r + 1) * piece, :] for r in range(rows_2d)], axis=1)
        p = (win * inv_cnt - u.astype(F32)).astype(BF16)
        p = jnp.concatenate([p[:, i * LANES:(i + 1) * LANES] for i in range(n_piece)], axis=0)
        yb_ref[g] = (_dot(p, wpg_ref[g]) * ps_ref[g]).astype(BF16)


def _pool(p3, amat, invc, wpg, pscale, seq_len, n_seq, rows_2d):
    n = p3.shape[1]
    rows = seq_len * n_seq
    return pl.pallas_call(
        functools.partial(_pool_kernel, seq_len=seq_len, n_seq=n_seq, rows_2d=rows_2d),
        out_shape=jax.ShapeDtypeStruct((N_GROUPS, n, LANES), BF16),
        grid=(n // rows,),
        in_specs=[
            pl.BlockSpec((N_GROUPS, rows, LANES), lambda i: (SLAB_POOL // N_GROUPS, i, 0)),
            _const_spec(amat.shape),
            _const_spec((N_GROUPS, seq_len, LANES)),
            _const_spec((N_GROUPS, LANES, LANES)),
            _const_spec((N_GROUPS, 1, LANES)),
        ],
        out_specs=pl.BlockSpec((N_GROUPS, rows, LANES), lambda i: (0, i, 0)),
        compiler_params=_cparams(("parallel",)),
        name="pool",
    )(p3, amat, invc, wpg, pscale)


def _window_matrix(size, w):
    idx = np.arange(size)
    lo = np.clip(idx - w // 2, 0, size)
    hi = np.clip(idx - w // 2 + w, 0, size)
    s = np.arange(size)[None, :]
    a = ((s >= lo[:, None]) & (s < hi[:, None])).astype(np.float32)
    return a, (hi - lo).astype(np.float32)


def _pool_constants(seq_len, rows_2d):
    mats, invs = [], []
    for w in POOL_WINDOWS:
        if rows_2d is None:
            a, cnt = _window_matrix(seq_len, w)
        else:
            _, cr = _window_matrix(rows_2d, w)
            a, cc = _window_matrix(GRID_W, w)
            cnt = np.kron(cr, cc)
        mats.append(a)
        invs.append(np.repeat((1.0 / cnt)[:, None], LANES, axis=1))
    return (jnp.asarray(np.stack(mats), dtype=BF16), jnp.asarray(np.stack(invs), dtype=F32))


MIX_TM = 512


def _mix_kernel(x_ref, mod_ref, ya_ref, yb_ref, ga_ref, gb_ref, wa_ref, wb_ref, wo_ref, x1_ref):
    ya = jnp.concatenate([ya_ref[i] for i in range(N_HEADS)], axis=1)
    yb = jnp.concatenate([yb_ref[i] for i in range(N_GROUPS)], axis=1)
    ga = jnp.concatenate([ga_ref[i] for i in range(8)], axis=1).astype(F32)
    gb = jnp.concatenate([gb_ref[i] for i in range(8)], axis=1).astype(F32)
    merged = jax.nn.sigmoid(ga) * _dot(ya, wa_ref[...]) + jax.nn.sigmoid(gb) * _dot(yb, wb_ref[...])
    out = _dot(merged.astype(BF16), wo_ref[...])
    x1_ref[...] = x_ref[...] + mod_ref[0, 2:3, :] * out


def _mix(x2d, mod3, mod_row_fn, ya, yb, p3, wa, wb, wo):
    n = x2d.shape[0]
    return pl.pallas_call(
        _mix_kernel,
        out_shape=jax.ShapeDtypeStruct((n, D_MODEL), F32),
        grid=(n // MIX_TM,),
        in_specs=[
            pl.BlockSpec((MIX_TM, D_MODEL), lambda i: (i, 0)),
            pl.BlockSpec((1, 6, D_MODEL), lambda i: (mod_row_fn(i * MIX_TM), 0, 0)),
            pl.BlockSpec((N_HEADS, MIX_TM, LANES), lambda i: (0, i, 0)),
            pl.BlockSpec((N_GROUPS, MIX_TM, LANES), lambda i: (0, i, 0)),
            pl.BlockSpec((8, MIX_TM, LANES), lambda i: (SLAB_GA // 8, i, 0)),
            pl.BlockSpec((8, MIX_TM, LANES), lambda i: (SLAB_GB // 8, i, 0)),
            _const_spec((D_MODEL, D_MODEL)),
            _const_spec((N_GROUPS * LANES, D_MODEL)),
            _const_spec((D_MODEL, D_MODEL)),
        ],
        out_specs=pl.BlockSpec((MIX_TM, D_MODEL), lambda i: (i, 0)),
        compiler_params=_cparams(("parallel",)),
        name="mix",
    )(x2d, mod3, ya, yb, p3, p3, wa, wb, wo)


ROUTE_TM = 1024
ROUTER_ROWS = 4 + N_EXPERTS
SLOT_TM = 512
SC_ROW = 256
SC_WIN = 128
HI_MASK = -65536


def _pack_bf16_pairs(x):
    bits = pltpu.bitcast(x, jnp.int32)
    half = D_MODEL // 2
    return jnp.bitwise_or(jnp.bitwise_and(lax.shift_right_logical(bits[:, :half], 16), 0xFFFF),
                          jnp.bitwise_and(bits[:, half:], HI_MASK))


def _unpack_bf16_pairs(lo_words, hi_words):
    def low(w):
        return pltpu.bitcast(lax.shift_left(w, 16), F32)

    def high(w):
        return pltpu.bitcast(jnp.bitwise_and(w, HI_MASK), F32)

    return jnp.concatenate([low(lo_words), low(hi_words), high(lo_words), high(hi_words)], axis=1).astype(BF16)


def _route_rows(lt):
    lg = [lt[i:i + 1] for i in range(N_GROUPS)]
    mx = jnp.maximum(jnp.maximum(lg[0], lg[1]), jnp.maximum(lg[2], lg[3]))
    p_top = 1.0 / (jnp.exp(lg[0] - mx) + jnp.exp(lg[1] - mx) + jnp.exp(lg[2] - mx) + jnp.exp(lg[3] - mx))
    gidx = jnp.zeros(lg[0].shape, jnp.int32)
    best = lg[0]
    for i in range(1, N_GROUPS):
        upd = lg[i] > best
        gidx = jnp.where(upd, i, gidx)
        best = jnp.where(upd, lg[i], best)

    def expert_row(g, e):
        r = N_GROUPS + g * EXPERTS_PER_GROUP + e
        return lt[r:r + 1]

    le = [jnp.where(gidx == 0, expert_row(0, e),
                    jnp.where(gidx == 1, expert_row(1, e),
                              jnp.where(gidx == 2, expert_row(2, e), expert_row(3, e))))
          for e in range(EXPERTS_PER_GROUP)]
    v1 = jnp.maximum(jnp.maximum(le[0], le[1]), jnp.maximum(le[2], le[3]))
    i1 = jnp.full(v1.shape, EXPERTS_PER_GROUP - 1, jnp.int32)
    for e in range(EXPERTS_PER_GROUP - 2, -1, -1):
        i1 = jnp.where(le[e] == v1, e, i1)
    le2 = [jnp.where(i1 == e, -jnp.inf, le[e]) for e in range(EXPERTS_PER_GROUP)]
    v2 = jnp.maximum(jnp.maximum(le2[0], le2[1]), jnp.maximum(le2[2], le2[3]))
    i2 = jnp.full(v2.shape, EXPERTS_PER_GROUP - 1, jnp.int32)
    for e in range(EXPERTS_PER_GROUP - 2, -1, -1):
        i2 = jnp.where(le2[e] == v2, e, i2)
    e2 = jnp.exp(v2 - v1)
    first = (gidx * EXPERTS_PER_GROUP + i1).astype(F32)
    second = (gidx * EXPERTS_PER_GROUP + i2).astype(F32)
    return first, second, p_top / (1.0 + e2), p_top * e2 / (1.0 + e2)


def _route_kernel(x1_ref, mod_ref, g2_ref, wr_hi_ref, wr_lo_ref, br_ref, h2_ref, info_ref):
    x1 = x1_ref[...]
    h2 = x1 * _rms_scale(x1) * g2_ref[...]
    h2 = h2 * (1.0 + mod_ref[0, 4:5, :]) + mod_ref[0, 3:4, :]
    h_hi, h_lo = _split_bf16(h2)
    packed = _pack_bf16_pairs(h_hi.astype(F32))
    h2_ref[0:ROUTE_TM, :] = packed[:, :SC_ROW]
    h2_ref[ROUTE_TM:2 * ROUTE_TM, :] = packed[:, SC_ROW:]
    wr_hi = wr_hi_ref[...]
    logits = _dot(h_hi, wr_hi) + _dot(h_lo, wr_hi) + _dot(h_hi, wr_lo_ref[...]) + br_ref[...]
    first, second, w1, w2 = _route_rows(logits.T)
    zero = jnp.zeros_like(w1)
    info_ref[...] = jnp.concatenate([first, second, w1, w2, zero, zero, zero, zero], axis=0)


def _route(x1, mod3, mod_row_fn, g_norm2, wr_hi, wr_lo, br):
    n = x1.shape[0]
    return pl.pallas_call(
        _route_kernel,
        out_shape=(jax.ShapeDtypeStruct((2 * n, SC_ROW), jnp.int32), jax.ShapeDtypeStruct((8, n), F32)),
        grid=(n // ROUTE_TM,),
        in_specs=[
            pl.BlockSpec((ROUTE_TM, D_MODEL), lambda i: (i, 0)),
            pl.BlockSpec((1, 6, D_MODEL), lambda i: (mod_row_fn(i * ROUTE_TM), 0, 0)),
            _const_spec((1, D_MODEL)),
            _const_spec((D_MODEL, LANES)),
            _const_spec((D_MODEL, LANES)),
            _const_spec((1, LANES)),
        ],
        out_specs=(pl.BlockSpec((2 * ROUTE_TM, SC_ROW), lambda i: (i, 0)),
                   pl.BlockSpec((8, ROUTE_TM), lambda i: (0, i))),
        compiler_params=_cparams(("parallel",)),
        name="route",
    )(x1, mod3, g_norm2, wr_hi, wr_lo, br)


def _plan_kernel(info_ref, before_ref, pos_ref, tmap_ref, rank_scr, count_scr, *, n_blocks):
    step = pl.program_id(0)
    expert_id = lax.broadcasted_iota(jnp.int32, (N_EXPERTS, ROUTE_TM), 0).astype(F32)

    @pl.when(step == 0)
    def _():
        count_scr[...] = jnp.zeros_like(count_scr)

    @pl.when(step < n_blocks)
    def _():
        cols = pl.ds(pl.multiple_of(step * ROUTE_TM, ROUTE_TM), ROUTE_TM)
        sel = jnp.where((info_ref[0:1, cols] == expert_id) | (info_ref[1:2, cols] == expert_id), 1.0, 0.0)
        selb = sel.astype(BF16)
        rank_scr[:, cols] = _dot(selb, before_ref[...]) + count_scr[:, 0:1]
        count_scr[...] += _dot(selb, jnp.ones((ROUTE_TM, LANES), BF16))

    @pl.when(step == n_blocks)
    def _():
        counts = count_scr[...]
        padded = jnp.floor((counts + (SLOT_TM - 1)) * (1.0 / SLOT_TM)) * SLOT_TM
        starts, ends, run = [], [], jnp.zeros((1, LANES), F32)
        for e in range(N_EXPERTS):
            starts.append(run)
            run = run + padded[e:e + 1]
            ends.append(run)
        n = pos_ref.shape[1]
        for k in range(2):
            chosen = info_ref[k:k + 1, :]
            slot = jnp.zeros((1, n), F32)
            for e in range(N_EXPERTS):
                slot = jnp.where(chosen == float(e), starts[e][:, 0:1] + rank_scr[e:e + 1, :], slot)
            pos_ref[k:k + 1, :] = slot.astype(jnp.int32)
        pos_ref[2:8, :] = jnp.zeros((6, n), jnp.int32)
        tile_start = lax.broadcasted_iota(jnp.int32, (1, LANES), 1).astype(F32) * SLOT_TM
        owner = jnp.zeros((1, LANES), F32)
        for e in range(N_EXPERTS - 1):
            owner = owner + jnp.where(ends[e] <= tile_start, 1.0, 0.0)
        used = jnp.where(tile_start < run, 1.0, 0.0)
        tmap_ref[...] = jnp.concatenate([owner, used] + [jnp.zeros((1, LANES), F32)] * 6, axis=0).astype(jnp.int32)


def _plan(info):
    n = info.shape[1]
    n_blocks = n // ROUTE_TM
    t_i = lax.broadcasted_iota(jnp.int32, (ROUTE_TM, ROUTE_TM), 0)
    t_j = lax.broadcasted_iota(jnp.int32, (ROUTE_TM, ROUTE_TM), 1)
    before = (t_i < t_j).astype(BF16)
    return pl.pallas_call(
        functools.partial(_plan_kernel, n_blocks=n_blocks),
        out_shape=(jax.ShapeDtypeStruct((8, n), jnp.int32), jax.ShapeDtypeStruct((8, LANES), jnp.int32)),
        grid=(n_blocks + 1,),
        in_specs=[_const_spec((8, n)), _const_spec((ROUTE_TM, ROUTE_TM))],
        out_specs=(pl.BlockSpec((8, n), lambda i: (0, 0)), pl.BlockSpec((8, LANES), lambda i: (0, 0))),
        scratch_shapes=[pltpu.VMEM((N_EXPERTS, n), F32), pltpu.VMEM((N_EXPERTS, LANES), F32)],
        compiler_params=_cparams(("arbitrary",)),
        name="plan",
    )(info, before)


def _sc_mesh():
    return plsc.VectorSubcoreMesh(core_axis_name="core", subcore_axis_name="subcore")


def _sc_scatter_rows(x, idx, n_out):
    n_idx = idx.shape[0]
    src_blocks = x.shape[0] // SC_WIN

    @pl.kernel(out_type=jax.ShapeDtypeStruct((n_out, SC_ROW), x.dtype), mesh=_sc_mesh(), scratch_types=[])
    def scatter_kernel(x_hbm, i_hbm, o_hbm):
        def body(x_vmem, i_vmem):
            pltpu.sync_copy(x_vmem, o_hbm.at[i_vmem.at[0]])

        pltpu.emit_pipeline(
            body,
            grid=(n_idx // SC_WIN,),
            in_specs=[pl.BlockSpec((SC_WIN, SC_ROW), index_map=lambda i: (i % src_blocks, 0)),
                      pl.BlockSpec((1, SC_WIN), index_map=lambda i: (0, i))],
            out_specs=[],
            core_axis_name=("core", "subcore"),
            dimension_semantics=(pltpu.PARALLEL,),
        )(x_hbm, i_hbm)

    return scatter_kernel(x, idx.reshape(1, n_idx))


def _sc_gather_rows(x, idx):
    n_idx = idx.shape[0]

    @pl.kernel(out_type=jax.ShapeDtypeStruct((n_idx, SC_ROW), x.dtype), mesh=_sc_mesh(), scratch_types=[])
    def gather_kernel(x_hbm, i_hbm, o_hbm):
        def body(i_vmem, o_vmem):
            pltpu.sync_copy(x_hbm.at[i_vmem.at[0]], o_vmem)

        pltpu.emit_pipeline(
            body,
            grid=(n_idx // SC_WIN,),
            in_specs=[pl.BlockSpec((1, SC_WIN), index_map=lambda i: (0, i))],
            out_specs=[pl.BlockSpec((SC_WIN, SC_ROW), index_map=lambda i: (i, 0))],
            core_axis_name=("core", "subcore"),
            dimension_semantics=(pltpu.PARALLEL,),
        )(i_hbm, o_hbm)

    return gather_kernel(x, idx.reshape(1, n_idx))


def _experts_kernel(tmap_ref, xs_ref, wg_ref, wu_ref, wd_ref, ys_ref, wg_scr, wu_scr, wd_scr):
    j = pl.program_id(0)

    @pl.when((j == 0) | (tmap_ref[0, j] != tmap_ref[0, jnp.maximum(j - 1, 0)]))
    def _():
        wg_scr[...] = wg_ref[0].astype(BF16)
        wu_scr[...] = wu_ref[0].astype(BF16)
        wd_scr[...] = wd_ref[0].astype(BF16)

    @pl.when(tmap_ref[1, j] > 0)
    def _():
        x = _unpack_bf16_pairs(xs_ref[0:SLOT_TM, :], xs_ref[SLOT_TM:2 * SLOT_TM, :])
        a = _dot(x, wg_scr[...])
        u = _dot(x, wu_scr[...])
        act = (a * jax.nn.sigmoid(a) * u).astype(BF16)
        y = _dot(act, wd_scr[...]).astype(BF16).astype(F32)
        packed = _pack_bf16_pairs(y)
        ys_ref[0:SLOT_TM, :] = packed[:, :SC_ROW]
        ys_ref[SLOT_TM:2 * SLOT_TM, :] = packed[:, SC_ROW:]


def _experts(xs, tmap, wg, wu, wd):
    n_tiles = xs.shape[0] // (2 * SLOT_TM)
    grid_spec = pltpu.PrefetchScalarGridSpec(
        num_scalar_prefetch=1,
        grid=(n_tiles,),
        in_specs=[
            pl.BlockSpec((2 * SLOT_TM, SC_ROW), lambda j, tm: (j, 0)),
            pl.BlockSpec((1, D_MODEL, D_EXPERT), lambda j, tm: (tm[0, j], 0, 0)),
            pl.BlockSpec((1, D_MODEL, D_EXPERT), lambda j, tm: (tm[0, j], 0, 0)),
            pl.BlockSpec((1, D_EXPERT, D_MODEL), lambda j, tm: (tm[0, j], 0, 0)),
        ],
        out_specs=pl.BlockSpec((2 * SLOT_TM, SC_ROW), lambda j, tm: (j, 0)),
        scratch_shapes=[pltpu.VMEM((D_MODEL, D_EXPERT), BF16), pltpu.VMEM((D_MODEL, D_EXPERT), BF16),
                        pltpu.VMEM((D_EXPERT, D_MODEL), BF16)],
    )
    return pl.pallas_call(
        _experts_kernel,
        out_shape=jax.ShapeDtypeStruct(xs.shape, jnp.int32),
        grid_spec=grid_spec,
        compiler_params=_cparams(("arbitrary",)),
        name="experts",
    )(tmap, xs, wg, wu, wd)


FIN_TM = 512


def _final_kernel(x1_ref, mod_ref, yg_ref, w_ref, gf_ref, y_ref):
    moe = None
    for k in range(2):
        base = 2 * k * FIN_TM
        yk = _unpack_bf16_pairs(yg_ref[base:base + FIN_TM, :], yg_ref[base + FIN_TM:base + 2 * FIN_TM, :])
        term = w_ref[:, k:k + 1] * yk.astype(F32)
        moe = term if moe is None else moe + term
    x2 = x1_ref[...] + mod_ref[0, 5:6, :] * moe
    y_ref[...] = x2 * _rms_scale(x2) * gf_ref[...]


def _final(x1, mod3, mod_row_fn, yg, w_cols, g_final):
    n = x1.shape[0]
    return pl.pallas_call(
        _final_kernel,
        out_shape=jax.ShapeDtypeStruct((n, D_MODEL), F32),
        grid=(n // FIN_TM,),
        in_specs=[
            pl.BlockSpec((FIN_TM, D_MODEL), lambda i: (i, 0)),
            pl.BlockSpec((1, 6, D_MODEL), lambda i: (mod_row_fn(i * FIN_TM), 0, 0)),
            pl.BlockSpec((4 * FIN_TM, SC_ROW), lambda i: (i, 0)),
            pl.BlockSpec((FIN_TM, 2), lambda i: (i, 0)),
            _const_spec((1, D_MODEL)),
        ],
        out_specs=pl.BlockSpec((FIN_TM, D_MODEL), lambda i: (i, 0)),
        compiler_params=_cparams(("parallel",)),
        name="final",
    )(x1, mod3, yg, w_cols, g_final)


def _half_row(pos, half):
    return (pos // SLOT_TM) * (2 * SLOT_TM) + half * SLOT_TM + pos % SLOT_TM


def _moe(x1, mod3, mod_row_fn, g_norm2, wr_hi, wr_lo, br, wg, wu, wd, g_final):
    n = x1.shape[0]
    h2p, info = _route(x1, mod3, mod_row_fn, g_norm2, wr_hi, wr_lo, br)
    pos, tmap = _plan(info)
    n_slots = 2 * n + N_EXPERTS * SLOT_TM
    pos1, pos2 = pos[0], pos[1]

    def route_order(p):
        return jnp.stack([_half_row(p, 0).reshape(n // ROUTE_TM, ROUTE_TM),
                          _half_row(p, 1).reshape(n // ROUTE_TM, ROUTE_TM)], axis=1).reshape(-1)

    xs = _sc_scatter_rows(h2p, jnp.concatenate([route_order(pos1), route_order(pos2)]), 2 * n_slots)
    ys = _experts(xs, tmap, wg, wu, wd)

    def fin_order(p):
        return jnp.stack([_half_row(p, 0).reshape(n // FIN_TM, FIN_TM),
                          _half_row(p, 1).reshape(n // FIN_TM, FIN_TM)], axis=1)

    gidx = jnp.stack([fin_order(pos1), fin_order(pos2)], axis=1).reshape(-1)
    yg = _sc_gather_rows(ys, gidx)
    w_cols = info[2:4].T
    return _final(x1, mod3, mod_row_fn, yg, w_cols, g_final)


def _stream(x, mod3, mod_row_fn, seq_len, rows_2d, state, w):
    bsz, t, _ = x.shape
    n = bsz * t
    x2d = x.reshape(n, D_MODEL)
    p3, bcum, imb, amax = _in_projection(x2d, mod3, mod_row_fn, w["g_norm1"], w["wmain"], w["bmain"],
                                         w["colscale"], w["wg_hi"], w["wg_lo"], w["bg"])
    m0 = None if state is None else state[2]
    cols, rw, cs = _gate_prep(bcum, imb, amax, m0)
    rows = rw[:, :2 * N_HD].T.reshape(2 * N_HD, 1, n)
    ml = _mlstm(p3, cols, rows, cs, w["gn3"], None if state is None else state[:2])
    amat, invc = _pool_constants(seq_len, rows_2d)
    n_seq = max(1, 1024 // seq_len)
    yb = _pool(p3, amat, invc, w["wpg"], w["pscale"], seq_len, n_seq, rows_2d)
    ya = ml if state is not None else ml[0]
    x1 = _mix(x2d, mod3, mod_row_fn, ya, yb, p3, w["wa"], w["wb"], w["wo"])
    y = _moe(x1, mod3, mod_row_fn, w["g_norm2"], w["wr_hi"], w["wr_lo"], w["br"],
             w["wg"], w["wu"], w["wd"], w["g_final"])
    return y.reshape(bsz, t, D_MODEL), ml, cs


def kernel(x_prompt, x_sample, state_C, state_n, state_m, c, c_ctx, w_mod, b_mod, g_norm1, w_in, b_in, gn_gain, w_pool_grp, pool_scale, w_proj_a, w_proj_b, w_out, g_norm2, w_router_grp, b_router_grp, w_router_exp, b_router_exp, w_exp_gate, w_exp_up, w_exp_down, g_final):
    bp, tp, _ = x_prompt.shape
    bs, ts, _ = x_sample.shape
    assert w_mod.shape[0] == 1, "single trunk layer"
    assert tp == CHUNK and ts % CHUNK == 0 and ts == ML_ROWS

    cvec = jnp.zeros((8, D_MODEL), F32).at[0].set(c_ctx).at[1:1 + bs].set(c)
    mod3 = _modulation(cvec, w_mod[0], b_mod[0])

    wi, bi = w_in[0], b_in[0]
    d = D_MODEL
    o_q, o_k, o_v, o_o, o_g, o_p, o_ga, o_gb = 0, d, 2 * d, 3 * d, 4 * d, 4 * d + 32, 4 * d + 32 + 512, 4 * d + 32 + 512 + d
    order = [(o_q, d), (o_k, d), (o_v, d), (o_o, d), (o_ga, d), (o_gb, d), (o_p, 512)]
    wmain = _pack_columns(wi, order)
    bmain = jnp.concatenate([bi[s:s + l] for s, l in order]).reshape(1, D_MAIN)
    colscale = jnp.ones((D_MAIN,), F32).at[SLAB_K * LANES:(SLAB_K + N_HEADS) * LANES].set(HEAD_DIM ** -0.5)
    colscale = colscale.reshape(1, D_MAIN)
    gwt = _take_columns_t(wi, o_g, 32)
    gb = bi[o_g:o_g + 32]
    i_f, f_f, i_b, f_b = (slice(0, 8), slice(8, 16), slice(16, 24), slice(24, 32))
    pad_w = jnp.zeros((LANES - N_HD, d), F32)
    pad_b = jnp.zeros((LANES - N_HD,), F32)
    wg2 = jnp.concatenate([gwt[f_f], gwt[f_b], pad_w, gwt[i_f], gwt[i_b], pad_w], axis=0).T
    bg2 = jnp.concatenate([gb[f_f], gb[f_b], pad_b, gb[i_f], gb[i_b], pad_b]).reshape(1, 2 * LANES)
    wg_hi, wg_lo = _split_bf16(wg2)

    wr = jnp.concatenate([w_router_grp[0], w_router_exp[0].reshape(d, N_EXPERTS),
                          jnp.zeros((d, LANES - ROUTER_ROWS), F32)], axis=1)
    br = jnp.concatenate([b_router_grp[0], b_router_exp[0].reshape(N_EXPERTS),
                          jnp.zeros((LANES - ROUTER_ROWS,), F32)]).reshape(1, LANES)
    wr_hi, wr_lo = _split_bf16(wr)

    w = dict(
        g_norm1=g_norm1[0].reshape(1, d), wmain=wmain, bmain=bmain, colscale=colscale,
        wg_hi=wg_hi, wg_lo=wg_lo, bg=bg2,
        gn3=gn_gain[0].reshape(N_HEADS, 1, HEAD_DIM),
        wpg=w_pool_grp[0].astype(BF16), pscale=pool_scale[0].reshape(N_GROUPS, 1, LANES),
        wa=w_proj_a[0].astype(BF16), wb=w_proj_b[0].astype(BF16), wo=w_out[0].astype(BF16),
        g_norm2=g_norm2[0].reshape(1, d), wr_hi=wr_hi, wr_lo=wr_lo, br=br,
        wg=w_exp_gate[0], wu=w_exp_up[0], wd=w_exp_down[0],
        g_final=g_final.reshape(1, d),
    )

    y_prompt, (_, cst, nst), cs_prompt = _stream(x_prompt, mod3, lambda r: 0, tp, None, None, w)
    n_rep = jnp.broadcast_to(state_n[..., None], state_C.shape)
    m0 = jnp.pad(state_m[:, 0].reshape(bs, N_HD), ((0, 0), (0, LANES - N_HD))).reshape(bs, 1, LANES)
    y_sample, _, _ = _stream(x_sample, mod3, lambda r: 1 + r // ts, ts, ts // GRID_W, (state_C, n_rep, m0), w)

    new_c = cst
    new_n = nst.reshape(bp, 1, 2, N_HEADS, HEAD_DIM)
    new_m = cs_prompt[:, 1, :N_HD].reshape(bp, 1, 2, N_HEADS)
    return (y_prompt, y_sample, new_c, new_n, new_m)
```

```python
import functools

import numpy as np
import jax
import jax.numpy as jnp
from jax import lax
from jax.experimental import pallas as pl
from jax.experimental.pallas import tpu as pltpu
from jax.experimental.pallas import tpu_sc as plsc

F32 = jnp.float32
BF16 = jnp.bfloat16

D_MODEL = 1024
N_HEADS = 8
HEAD_DIM = 128
GRID_W = 64
POOL_WINDOWS = (2, 4, 8, 16)
N_GROUPS = 4
EXPERTS_PER_GROUP = 4
N_EXPERTS = 16
D_EXPERT = 512
EPS = 1e-6
NEG = -1e30

LANES = 128
CHUNK = 256
N_SLABS = 52
SLAB_Q, SLAB_K, SLAB_V, SLAB_O, SLAB_GA, SLAB_GB, SLAB_POOL = 0, 8, 16, 24, 32, 40, 48
D_MAIN = N_SLABS * LANES
VMEM_LIMIT = 56 * 1024 * 1024


def _cparams(sem):
    return pltpu.CompilerParams(dimension_semantics=sem, vmem_limit_bytes=VMEM_LIMIT)


def _const_spec(shape):
    nd = len(shape)
    return pl.BlockSpec(shape, lambda *_: (0,) * nd, pipeline_mode=pl.Buffered(1))


def _split_bf16(x):
    hi = x.astype(BF16)
    lo = (x - hi.astype(F32)).astype(BF16)
    return hi, lo


def _dot(a, b):
    return jnp.dot(a, b, preferred_element_type=F32)


def _dot3(a, b_hi, b_lo):
    a_hi, a_lo = _split_bf16(a)
    return _dot(a_hi, b_hi) + _dot(a_hi, b_lo) + _dot(a_lo, b_hi)


def _rms_scale(x):
    return lax.rsqrt(jnp.mean(x * x, axis=-1, keepdims=True) + EPS)


def _log_sigmoid(x):
    return jnp.minimum(x, 0.0) - jnp.log1p(jnp.exp(-jnp.abs(x)))


def _mod_kernel(c_ref, w_ref, b_ref, o_ref):
    c = c_ref[...]
    s = c * jax.nn.sigmoid(c)
    w_hi, w_lo = _split_bf16(w_ref[...])
    o_ref[...] = _dot3(s, w_hi, w_lo) + b_ref[...]


def _modulation(cvec, w_mod, b_mod):
    tn = 512
    out = pl.pallas_call(
        _mod_kernel,
        out_shape=jax.ShapeDtypeStruct((8, 6 * D_MODEL), F32),
        grid=(6 * D_MODEL // tn,),
        in_specs=[
            pl.BlockSpec((8, D_MODEL), lambda j: (0, 0)),
            pl.BlockSpec((D_MODEL, tn), lambda j: (0, j)),
            pl.BlockSpec((1, tn), lambda j: (0, j)),
        ],
        out_specs=pl.BlockSpec((8, tn), lambda j: (0, j)),
        compiler_params=_cparams(("parallel",)),
        name="modulation",
    )(cvec, w_mod, b_mod.reshape(1, -1))
    return out.reshape(8, 6, D_MODEL)


PACK_COLS = 512


def _pack_kernel(wt_ref, o_ref):
    o_ref[...] = wt_ref[...].T.astype(BF16)


def _pack_columns(w, order):
    k, n = w.shape
    starts = []
    for src, width in order:
        assert width % PACK_COLS == 0
        starts += [src + j for j in range(0, width, PACK_COLS)]
    n_out = len(starts) * PACK_COLS

    unit = 32
    assert all(s % unit == 0 for s in starts)

    def src_row(j):
        row = jnp.int32(starts[-1] // unit)
        for idx in range(len(starts) - 2, -1, -1):
            row = jnp.where(j == idx, starts[idx] // unit, row)
        return row * unit

    return pl.pallas_call(
        _pack_kernel,
        out_shape=jax.ShapeDtypeStruct((k, n_out), BF16),
        grid=(len(starts),),
        in_specs=[pl.BlockSpec((pl.Element(PACK_COLS), pl.Element(k)), lambda j: (src_row(j), 0))],
        out_specs=pl.BlockSpec((k, PACK_COLS), lambda j: (0, j)),
        compiler_params=_cparams(("parallel",)),
        name="pack_columns",
    )(w.T)


def _copy_kernel(w_ref, o_ref):
    o_ref[...] = w_ref[...]


def _take_columns_t(w, start, width):
    k = w.shape[0]
    return pl.pallas_call(
        _copy_kernel,
        out_shape=jax.ShapeDtypeStruct((width, k), w.dtype),
        grid=(1,),
        in_specs=[pl.BlockSpec((pl.Element(width), pl.Element(k)), lambda j: (start, 0))],
        out_specs=pl.BlockSpec((width, k), lambda j: (0, 0)),
        name="take_columns",
    )(w.T)


IN_TM = 512
IN_TN = 512
LOG2E = 1.4426950408889634
N_HD = 2 * N_HEADS


def _fwd_lanes(shape):
    lane = lax.broadcasted_iota(jnp.int32, shape, len(shape) - 1)
    return (lane & N_HEADS) == 0


def _inproj_kernel(x_ref, mod_ref, g_ref, w_ref, b_ref, cs_ref, wg_hi_ref, wg_lo_ref, bg_ref,
                   p3_ref, bcum_ref, imb_ref, amax_ref):
    x = x_ref[...]
    h = x * _rms_scale(x) * g_ref[...]
    h = h * (1.0 + mod_ref[0, 1:2, :]) + mod_ref[0, 0:1, :]
    hb = h.astype(BF16)
    for n in range(D_MAIN // IN_TN):
        sl = slice(n * IN_TN, (n + 1) * IN_TN)
        acc = (_dot(hb, w_ref[:, sl]) + b_ref[:, sl]) * cs_ref[:, sl]
        accb = acc.astype(BF16)
        for j in range(IN_TN // LANES):
            p3_ref[n * (IN_TN // LANES) + j] = accb[:, j * LANES:(j + 1) * LANES]

    g2 = _dot3(h, wg_hi_ref[...], wg_lo_ref[...]) + bg_ref[...]
    ls = _log_sigmoid(g2[:, :LANES])
    gi = g2[:, LANES:]
    row = lax.broadcasted_iota(jnp.int32, (CHUNK, LANES), 0)
    fwd = _fwd_lanes((CHUNK, LANES))
    for c in range(IN_TM // CHUNK):
        rows = slice(c * CHUNK, (c + 1) * CHUNK)
        cf = ls[rows]
        cb = cf
        s = 1
        while s < CHUNK:
            cf = cf + jnp.where(row >= s, pltpu.roll(cf, s, 0), 0.0)
            cb = cb + jnp.where(row < CHUNK - s, pltpu.roll(cb, CHUNK - s, 0), 0.0)
            s *= 2
        b = jnp.where(fwd, cf, cb)
        imb = gi[rows] - b
        mf = imb
        mb = imb
        s = 1
        while s < CHUNK:
            mf = jnp.maximum(mf, jnp.where(row >= s, pltpu.roll(mf, s, 0), -jnp.inf))
            mb = jnp.maximum(mb, jnp.where(row < CHUNK - s, pltpu.roll(mb, CHUNK - s, 0), -jnp.inf))
            s *= 2
        bcum_ref[rows, :] = b
        imb_ref[rows, :] = imb
        amax_ref[rows, :] = b + jnp.where(fwd, mf, mb)


def _in_projection(x2d, mod3, mod_row_fn, g_norm1, wmain, bmain, colscale, wg_hi, wg_lo, bg):
    n = x2d.shape[0]
    return pl.pallas_call(
        _inproj_kernel,
        out_shape=(jax.ShapeDtypeStruct((N_SLABS, n, LANES), BF16),) + (jax.ShapeDtypeStruct((n, LANES), F32),) * 3,
        grid=(n // IN_TM,),
        in_specs=[
            pl.BlockSpec((IN_TM, D_MODEL), lambda i: (i, 0)),
            pl.BlockSpec((1, 6, D_MODEL), lambda i: (mod_row_fn(i * IN_TM), 0, 0)),
            _const_spec((1, D_MODEL)),
            _const_spec((D_MODEL, D_MAIN)),
            _const_spec((1, D_MAIN)),
            _const_spec((1, D_MAIN)),
            _const_spec((D_MODEL, 2 * LANES)),
            _const_spec((D_MODEL, 2 * LANES)),
            _const_spec((1, 2 * LANES)),
        ],
        out_specs=(pl.BlockSpec((N_SLABS, IN_TM, LANES), lambda i: (0, i, 0)),)
        + (pl.BlockSpec((IN_TM, LANES), lambda i: (i, 0)),) * 3,
        compiler_params=_cparams(("parallel",)),
        name="in_projection",
    )(x2d, mod3, g_norm1, wmain, bmain, colscale, wg_hi, wg_lo, bg)


GP_ROWS = 1024
GP_NCH = GP_ROWS // CHUNK
Q_C2, Q_EM, Q_WI, Q_WC = 0, 1, 2, 3


def _gate_prep_kernel(*refs, carry):
    if carry:
        b_ref, imb_ref, a_ref, m0_ref, cols_ref, rw_ref, cs_ref = refs
    else:
        b_ref, imb_ref, a_ref, cols_ref, rw_ref, cs_ref = refs
    fwd1 = _fwd_lanes((1, LANES))
    lane = lax.broadcasted_iota(jnp.int32, (CHUNK, LANES), 1)
    neg = jnp.full((1, LANES), NEG, F32)

    def at_scan_end(ref, c):
        return jnp.where(fwd1, ref[(c + 1) * CHUNK - 1:(c + 1) * CHUNK, :], ref[c * CHUNK:c * CHUNK + 1, :])

    b_last = [at_scan_end(b_ref, c) for c in range(GP_NCH)]
    a_last = [at_scan_end(a_ref, c) for c in range(GP_NCH)]
    m_start = m0_ref[0] if carry else neg
    m_prev = [[None] * GP_NCH, [None] * GP_NCH]
    m_new = [[None] * GP_NCH, [None] * GP_NCH]
    for d, order in enumerate((range(GP_NCH), range(GP_NCH - 1, -1, -1))):
        m = m_start
        for c in order:
            m_prev[d][c] = m
            m_new[d][c] = jnp.maximum(b_last[c] + m, a_last[c])
            m = m_new[d][c] if carry else neg

    for c in range(GP_NCH):
        rows = slice(c * CHUNK, (c + 1) * CHUNK)
        mp = jnp.where(fwd1, m_prev[0][c], m_prev[1][c])
        mn = jnp.where(fwd1, m_new[0][c], m_new[1][c])
        b = b_ref[rows, :]
        imb = imb_ref[rows, :]
        inter = b + mp
        m_t = jnp.maximum(inter, a_ref[rows, :])
        c2 = (b - m_t) * LOG2E
        em = jnp.exp(-m_t)
        wc = jnp.exp(b_last[c] - mn + imb)
        packed = jnp.where(lane < N_HD, c2, pltpu.roll(em, N_HD * Q_EM, 1))
        if carry:
            wi = jnp.exp(inter - m_t)
            packed = jnp.where(lane < N_HD * Q_WI, packed, pltpu.roll(wi, N_HD * Q_WI, 1))
        cols_ref[rows, :] = jnp.where(lane < N_HD * Q_WC, packed, pltpu.roll(wc, N_HD * Q_WC, 1))
        rw_ref[rows, :] = jnp.where(lane < N_HD, imb * (-LOG2E), pltpu.roll(wc, N_HD, 1))
        cs_ref[c, 0:1, :] = jnp.exp(b_last[c] + mp - mn)
        cs_ref[c, 1:2, :] = mn


def _gate_prep(bcum, imb, amax, m0=None):
    n = bcum.shape[0]
    carry = m0 is not None
    blk = pl.BlockSpec((GP_ROWS, LANES), lambda i: (i, 0))
    in_specs = [blk, blk, blk]
    args = [bcum, imb, amax]
    if carry:
        in_specs.append(pl.BlockSpec((1, 1, LANES), lambda i: (i, 0, 0)))
        args.append(m0)
    return pl.pallas_call(
        functools.partial(_gate_prep_kernel, carry=carry),
        out_shape=(jax.ShapeDtypeStruct((n, LANES), F32), jax.ShapeDtypeStruct((n, LANES), F32),
                   jax.ShapeDtypeStruct((n // CHUNK, 2, LANES), F32)),
        grid=(n // GP_ROWS,),
        in_specs=in_specs,
        out_specs=(blk, blk, pl.BlockSpec((GP_NCH, 2, LANES), lambda i: (i, 0, 0))),
        compiler_params=_cparams(("parallel",)),
        name="gate_prep",
    )(*args)


ML_ROWS = 1024
ML_NCH = ML_ROWS // CHUNK


def _pick_col(blk, lane_iota, idx):
    return jnp.sum(jnp.where(lane_iota == idx, blk, 0.0), axis=1, keepdims=True)


def _mlstm_kernel(*refs, carry):
    if carry:
        (q_ref, k_ref, v_ref, o_ref, cols_ref, r2f_ref, r2b_ref, gn_ref, cs_ref, c0_ref, n0_ref,
         ya_ref, s_scr, h_scr) = refs
    else:
        (q_ref, k_ref, v_ref, o_ref, cols_ref, r2f_ref, r2b_ref, gn_ref, wrf_ref, wrb_ref,
         ya_ref, cst_ref, nst_ref, s_scr, h_scr) = refs
    head = pl.program_id(1)
    lane = lax.broadcasted_iota(jnp.int32, (CHUNK, LANES), 1)
    lane1 = lax.broadcasted_iota(jnp.int32, (1, LANES), 1)
    r_idx = lax.broadcasted_iota(jnp.int32, (CHUNK, CHUNK), 0)
    c_idx = lax.broadcasted_iota(jnp.int32, (CHUNK, CHUNK), 1)
    ones = jnp.ones((CHUNK, HEAD_DIM), BF16)
    tn_dims = (((0,), (0,)), ((), ()))

    for d in range(2):
        mask = (c_idx <= r_idx) if d == 0 else (c_idx >= r_idx)
        r2_ref = r2f_ref if d == 0 else r2b_ref
        base = N_HEADS * d + head
        if carry:
            cn = jnp.concatenate([c0_ref[0, 0, d, 0], n0_ref[0, 0, d, 0]], axis=1)
        order = range(ML_NCH) if d == 0 else range(ML_NCH - 1, -1, -1)
        for step, c in enumerate(order):
            rows = slice(c * CHUNK, (c + 1) * CHUNK)
            q = q_ref[0, rows, :]
            k = k_ref[0, rows, :]
            v = v_ref[0, rows, :]
            v1 = jnp.concatenate([v, ones], axis=1)
            if d == 0:
                s = lax.dot_general(q, k, (((1,), (1,)), ((), ())), preferred_element_type=F32)
                s_scr[rows, :] = s
            else:
                s = s_scr[rows, :]
            blk = cols_ref[rows, :]
            arg = _pick_col(blk, lane, base + N_HD * Q_C2) - r2_ref[0, :, rows]
            sw = s * jnp.exp2(jnp.where(mask, arg, NEG))
            nd = _dot(sw.astype(BF16), v1)
            if carry:
                nd = nd + _pick_col(blk, lane, base + N_HD * Q_WI) * _dot(q, cn.astype(BF16))
            hdir = nd[:, :HEAD_DIM] / jnp.maximum(jnp.abs(nd[:, HEAD_DIM:]), _pick_col(blk, lane, base + N_HD * Q_EM))
            if d == 0:
                h_scr[rows, :] = hdir
            else:
                h_scr[rows, :] = h_scr[rows, :] + hdir

            if carry and step == ML_NCH - 1:
                continue
            w_col = _pick_col(blk, lane, base + N_HD * Q_WC)
            if carry:
                u = lax.dot_general(k, (w_col * v1.astype(F32)).astype(BF16), tn_dims,
                                    preferred_element_type=F32)
                decay = jnp.sum(jnp.where(lane1 == base, cs_ref[c, 0:1, :], 0.0), axis=1, keepdims=True)
                cn = decay * cn + u
            else:
                w_row = (wrf_ref if d == 0 else wrb_ref)[0, :, rows]
                w8 = jnp.broadcast_to(w_row, (8, CHUNK)).astype(BF16)
                cst_ref[c, 0, d, 0] = lax.dot_general(k, (w_col * v.astype(F32)).astype(BF16), tn_dims,
                                                      preferred_element_type=F32)
                nst_ref[c, d, 0] = _dot(w8, k)[0:1, :]

    hh = h_scr[...]
    hn = hh * _rms_scale(hh) * gn_ref[0]
    ya_ref[0] = (hn * jax.nn.sigmoid(o_ref[0].astype(F32))).astype(BF16)


def _mlstm(p3, cols, rows, cs, gn3, state=None):
    n = p3.shape[1]
    nblk = n // ML_ROWS
    carry = state is not None

    def slab(base):
        return pl.BlockSpec((1, ML_ROWS, LANES), lambda g, h: (base + h, g, 0))

    def row(base):
        return pl.BlockSpec((1, 1, ML_ROWS), lambda g, h: (base + h, 0, g))

    in_specs = [
        slab(SLAB_Q), slab(SLAB_K), slab(SLAB_V), slab(SLAB_O),
        pl.BlockSpec((ML_ROWS, LANES), lambda g, h: (g, 0)),
        row(0), row(N_HEADS),
        pl.BlockSpec((1, 1, LANES), lambda g, h: (h, 0, 0)),
    ]
    args = [p3, p3, p3, p3, cols, rows, rows, gn3]
    ya_shape = jax.ShapeDtypeStruct((N_HEADS, n, LANES), BF16)
    ya_spec = pl.BlockSpec((1, ML_ROWS, LANES), lambda g, h: (h, g, 0))
    state_spec = pl.BlockSpec((1, 1, 2, 1, HEAD_DIM, HEAD_DIM), lambda g, h: (g, 0, 0, h, 0, 0))
    if carry:
        c0, n0 = state
        in_specs += [pl.BlockSpec((ML_NCH, 2, LANES), lambda g, h: (g, 0, 0)), state_spec, state_spec]
        args += [cs, c0, n0]
        out_shape = ya_shape
        out_specs = ya_spec
    else:
        in_specs += [row(N_HD), row(N_HD + N_HEADS)]
        args += [rows, rows]
        nseq = n // CHUNK
        out_shape = (ya_shape,
                     jax.ShapeDtypeStruct((nseq, 1, 2, N_HEADS, HEAD_DIM, HEAD_DIM), F32),
                     jax.ShapeDtypeStruct((nseq, 2, N_HEADS, 1, HEAD_DIM), F32))
        out_specs = (ya_spec,
                     pl.BlockSpec((ML_NCH, 1, 2, 1, HEAD_DIM, HEAD_DIM), lambda g, h: (g, 0, 0, h, 0, 0)),
                     pl.BlockSpec((ML_NCH, 2, 1, 1, HEAD_DIM), lambda g, h: (g, 0, h, 0, 0)))
    return pl.pallas_call(
        functools.partial(_mlstm_kernel, carry=carry),
        out_shape=out_shape,
        grid=(nblk, N_HEADS),
        in_specs=in_specs,
        out_specs=out_specs,
        scratch_shapes=[pltpu.VMEM((ML_ROWS, CHUNK), F32), pltpu.VMEM((ML_ROWS, HEAD_DIM), F32)],
        compiler_params=_cparams(("parallel", "parallel")),
        name="mlstm_latent" if carry else "mlstm_prompt",
    )(*args)


def _pool_kernel(u_ref, a_ref, ic_ref, wpg_ref, ps_ref, yb_ref, *, seq_len, n_seq, rows_2d):
    piece = seq_len if rows_2d is None else GRID_W
    n_piece = n_seq if rows_2d is None else rows_2d
    for g in range(N_GROUPS):
        u = jnp.concatenate([u_ref[g, i * piece:(i + 1) * piece, :] for i in range(n_piece)], axis=1)
        win = _dot(a_ref[g], u)
        if rows_2d is None:
            inv_cnt = jnp.concatenate([ic_ref[g]] * n_piece, axis=1)
        else:
            run = [jnp.zeros((piece, LANES), F32)]
            for r in range(rows_2d):
                run.append(run[-1] + win[:, r * LANES:(r + 1) * LANES])
            half = POOL_WINDOWS[g] // 2
            win = jnp.concatenate([run[min(r - half + POOL_WINDOWS[g], rows_2d)] - run[max(r - half, 0)]
                                   for r in range(rows_2d)], axis=1)
            inv_cnt = jnp.concatenate([ic_ref[g, r * piece:(r + 1) * piece, :] for r in range(rows_2d)], axis=1)
        p = (win * inv_cnt - u.astype(F32)).astype(BF16)
        p = jnp.concatenate([p[:, i * LANES:(i + 1) * LANES] for i in range(n_piece)], axis=0)
        yb_ref[g] = (_dot(p, wpg_ref[g]) * ps_ref[g]).astype(BF16)


def _pool(p3, amat, invc, wpg, pscale, seq_len, n_seq, rows_2d):
    n = p3.shape[1]
    rows = seq_len * n_seq
    return pl.pallas_call(
        functools.partial(_pool_kernel, seq_len=seq_len, n_seq=n_seq, rows_2d=rows_2d),
        out_shape=jax.ShapeDtypeStruct((N_GROUPS, n, LANES), BF16),
        grid=(n // rows,),
        in_specs=[
            pl.BlockSpec((N_GROUPS, rows, LANES), lambda i: (SLAB_POOL // N_GROUPS, i, 0)),
            _const_spec(amat.shape),
            _const_spec((N_GROUPS, seq_len, LANES)),
            _const_spec((N_GROUPS, LANES, LANES)),
            _const_spec((N_GROUPS, 1, LANES)),
        ],
        out_specs=pl.BlockSpec((N_GROUPS, rows, LANES), lambda i: (0, i, 0)),
        compiler_params=_cparams(("parallel",)),
        name="pool",
    )(p3, amat, invc, wpg, pscale)


def _window_matrix(size, w):
    idx = np.arange(size)
    lo = np.clip(idx - w // 2, 0, size)
    hi = np.clip(idx - w // 2 + w, 0, size)
    s = np.arange(size)[None, :]
    a = ((s >= lo[:, None]) & (s < hi[:, None])).astype(np.float32)
    return a, (hi - lo).astype(np.float32)


def _pool_constants(seq_len, rows_2d):
    mats, invs = [], []
    for w in POOL_WINDOWS:
        if rows_2d is None:
            a, cnt = _window_matrix(seq_len, w)
        else:
            _, cr = _window_matrix(rows_2d, w)
            a, cc = _window_matrix(GRID_W, w)
            cnt = np.kron(cr, cc)
        mats.append(a)
        invs.append(np.repeat((1.0 / cnt)[:, None], LANES, axis=1))
    return (jnp.asarray(np.stack(mats), dtype=BF16), jnp.asarray(np.stack(invs), dtype=F32))


MIX_TM = 512


def _mix_kernel(x_ref, mod_ref, ya_ref, yb_ref, ga_ref, gb_ref, wa_ref, wb_ref, wo_ref, x1_ref):
    ya = jnp.concatenate([ya_ref[i] for i in range(N_HEADS)], axis=1)
    yb = jnp.concatenate([yb_ref[i] for i in range(N_GROUPS)], axis=1)
    ga = jnp.concatenate([ga_ref[i] for i in range(8)], axis=1).astype(F32)
    gb = jnp.concatenate([gb_ref[i] for i in range(8)], axis=1).astype(F32)
    merged = jax.nn.sigmoid(ga) * _dot(ya, wa_ref[...]) + jax.nn.sigmoid(gb) * _dot(yb, wb_ref[...])
    out = _dot(merged.astype(BF16), wo_ref[...])
    x1_ref[...] = x_ref[...] + mod_ref[0, 2:3, :] * out


def _mix(x2d, mod3, mod_row_fn, ya, yb, p3, wa, wb, wo):
    n = x2d.shape[0]
    return pl.pallas_call(
        _mix_kernel,
        out_shape=jax.ShapeDtypeStruct((n, D_MODEL), F32),
        grid=(n // MIX_TM,),
        in_specs=[
            pl.BlockSpec((MIX_TM, D_MODEL), lambda i: (i, 0)),
            pl.BlockSpec((1, 6, D_MODEL), lambda i: (mod_row_fn(i * MIX_TM), 0, 0)),
            pl.BlockSpec((N_HEADS, MIX_TM, LANES), lambda i: (0, i, 0)),
            pl.BlockSpec((N_GROUPS, MIX_TM, LANES), lambda i: (0, i, 0)),
            pl.BlockSpec((8, MIX_TM, LANES), lambda i: (SLAB_GA // 8, i, 0)),
            pl.BlockSpec((8, MIX_TM, LANES), lambda i: (SLAB_GB // 8, i, 0)),
            _const_spec((D_MODEL, D_MODEL)),
            _const_spec((N_GROUPS * LANES, D_MODEL)),
            _const_spec((D_MODEL, D_MODEL)),
        ],
        out_specs=pl.BlockSpec((MIX_TM, D_MODEL), lambda i: (i, 0)),
        compiler_params=_cparams(("parallel",)),
        name="mix",
    )(x2d, mod3, ya, yb, p3, p3, wa, wb, wo)


ROUTE_TM = 1024
ROUTER_ROWS = 4 + N_EXPERTS
SLOT_TM = 512
SC_ROW = 256
SC_WIN = 128
HI_MASK = -65536


def _pack_bf16_pairs(x):
    bits = pltpu.bitcast(x, jnp.int32)
    half = D_MODEL // 2
    return jnp.bitwise_or(jnp.bitwise_and(lax.shift_right_logical(bits[:, :half], 16), 0xFFFF),
                          jnp.bitwise_and(bits[:, half:], HI_MASK))


def _unpack_bf16_pairs(lo_words, hi_words):
    def low(w):
        return pltpu.bitcast(lax.shift_left(w, 16), F32)

    def high(w):
        return pltpu.bitcast(jnp.bitwise_and(w, HI_MASK), F32)

    return jnp.concatenate([low(lo_words), low(hi_words), high(lo_words), high(hi_words)], axis=1).astype(BF16)


def _route_rows(lt):
    lg = [lt[i:i + 1] for i in range(N_GROUPS)]
    mx = jnp.maximum(jnp.maximum(lg[0], lg[1]), jnp.maximum(lg[2], lg[3]))
    p_top = 1.0 / (jnp.exp(lg[0] - mx) + jnp.exp(lg[1] - mx) + jnp.exp(lg[2] - mx) + jnp.exp(lg[3] - mx))
    gidx = jnp.zeros(lg[0].shape, jnp.int32)
    best = lg[0]
    for i in range(1, N_GROUPS):
        upd = lg[i] > best
        gidx = jnp.where(upd, i, gidx)
        best = jnp.where(upd, lg[i], best)

    def expert_row(g, e):
        r = N_GROUPS + g * EXPERTS_PER_GROUP + e
        return lt[r:r + 1]

    le = [jnp.where(gidx == 0, expert_row(0, e),
                    jnp.where(gidx == 1, expert_row(1, e),
                              jnp.where(gidx == 2, expert_row(2, e), expert_row(3, e))))
          for e in range(EXPERTS_PER_GROUP)]
    v1 = jnp.maximum(jnp.maximum(le[0], le[1]), jnp.maximum(le[2], le[3]))
    i1 = jnp.full(v1.shape, EXPERTS_PER_GROUP - 1, jnp.int32)
    for e in range(EXPERTS_PER_GROUP - 2, -1, -1):
        i1 = jnp.where(le[e] == v1, e, i1)
    le2 = [jnp.where(i1 == e, -jnp.inf, le[e]) for e in range(EXPERTS_PER_GROUP)]
    v2 = jnp.maximum(jnp.maximum(le2[0], le2[1]), jnp.maximum(le2[2], le2[3]))
    i2 = jnp.full(v2.shape, EXPERTS_PER_GROUP - 1, jnp.int32)
    for e in range(EXPERTS_PER_GROUP - 2, -1, -1):
        i2 = jnp.where(le2[e] == v2, e, i2)
    e2 = jnp.exp(v2 - v1)
    first = (gidx * EXPERTS_PER_GROUP + i1).astype(F32)
    second = (gidx * EXPERTS_PER_GROUP + i2).astype(F32)
    return first, second, p_top / (1.0 + e2), p_top * e2 / (1.0 + e2)


def _route_kernel(x1_ref, mod_ref, g2_ref, wr_hi_ref, wr_lo_ref, br_ref, h2_ref, info_ref):
    x1 = x1_ref[...]
    h2 = x1 * _rms_scale(x1) * g2_ref[...]
    h2 = h2 * (1.0 + mod_ref[0, 4:5, :]) + mod_ref[0, 3:4, :]
    h_hi, h_lo = _split_bf16(h2)
    packed = _pack_bf16_pairs(h_hi.astype(F32))
    h2_ref[0:ROUTE_TM, :] = packed[:, :SC_ROW]
    h2_ref[ROUTE_TM:2 * ROUTE_TM, :] = packed[:, SC_ROW:]
    wr_hi = wr_hi_ref[...]
    logits = _dot(h_hi, wr_hi) + _dot(h_lo, wr_hi) + _dot(h_hi, wr_lo_ref[...]) + br_ref[...]
    first, second, w1, w2 = _route_rows(logits.T)
    zero = jnp.zeros_like(w1)
    info_ref[...] = jnp.concatenate([first, second, w1, w2, zero, zero, zero, zero], axis=0)


def _route(x1, mod3, mod_row_fn, g_norm2, wr_hi, wr_lo, br):
    n = x1.shape[0]
    return pl.pallas_call(
        _route_kernel,
        out_shape=(jax.ShapeDtypeStruct((2 * n, SC_ROW), jnp.int32), jax.ShapeDtypeStruct((8, n), F32)),
        grid=(n // ROUTE_TM,),
        in_specs=[
            pl.BlockSpec((ROUTE_TM, D_MODEL), lambda i: (i, 0)),
            pl.BlockSpec((1, 6, D_MODEL), lambda i: (mod_row_fn(i * ROUTE_TM), 0, 0)),
            _const_spec((1, D_MODEL)),
            _const_spec((D_MODEL, LANES)),
            _const_spec((D_MODEL, LANES)),
            _const_spec((1, LANES)),
        ],
        out_specs=(pl.BlockSpec((2 * ROUTE_TM, SC_ROW), lambda i: (i, 0)),
                   pl.BlockSpec((8, ROUTE_TM), lambda i: (0, i))),
        compiler_params=_cparams(("parallel",)),
        name="route",
    )(x1, mod3, g_norm2, wr_hi, wr_lo, br)


def _plan_kernel(info_ref, before_ref, pos_ref, tmap_ref, rank_scr, count_scr, *, n_blocks):
    step = pl.program_id(0)
    expert_id = lax.broadcasted_iota(jnp.int32, (N_EXPERTS, ROUTE_TM), 0).astype(F32)

    @pl.when(step == 0)
    def _():
        count_scr[...] = jnp.zeros_like(count_scr)

    @pl.when(step < n_blocks)
    def _():
        cols = pl.ds(pl.multiple_of(step * ROUTE_TM, ROUTE_TM), ROUTE_TM)
        sel = jnp.where((info_ref[0:1, cols] == expert_id) | (info_ref[1:2, cols] == expert_id), 1.0, 0.0)
        selb = sel.astype(BF16)
        rank_scr[:, cols] = _dot(selb, before_ref[...]) + count_scr[:, 0:1]
        count_scr[...] += _dot(selb, jnp.ones((ROUTE_TM, LANES), BF16))

    @pl.when(step == n_blocks)
    def _():
        counts = count_scr[...]
        padded = jnp.floor((counts + (SLOT_TM - 1)) * (1.0 / SLOT_TM)) * SLOT_TM
        starts, ends, run = [], [], jnp.zeros((1, LANES), F32)
        for e in range(N_EXPERTS):
            starts.append(run)
            run = run + padded[e:e + 1]
            ends.append(run)
        n = pos_ref.shape[1]
        for k in range(2):
            chosen = info_ref[k:k + 1, :]
            slot = jnp.zeros((1, n), F32)
            for e in range(N_EXPERTS):
                slot = jnp.where(chosen == float(e), starts[e][:, 0:1] + rank_scr[e:e + 1, :], slot)
            pos_ref[k:k + 1, :] = slot.astype(jnp.int32)
        pos_ref[2:8, :] = jnp.zeros((6, n), jnp.int32)
        tile_start = lax.broadcasted_iota(jnp.int32, (1, LANES), 1).astype(F32) * SLOT_TM
        owner = jnp.zeros((1, LANES), F32)
        for e in range(N_EXPERTS - 1):
            owner = owner + jnp.where(ends[e] <= tile_start, 1.0, 0.0)
        used = jnp.where(tile_start < run, 1.0, 0.0)
        is_odd = owner - 2.0 * jnp.floor(owner * 0.5)
        even_next = jnp.minimum(owner + is_odd, float(N_EXPERTS - 2))
        odd_next = owner + (1.0 - is_odd)
        tmap_ref[...] = jnp.concatenate([owner, used, even_next, odd_next] + [jnp.zeros((1, LANES), F32)] * 4,
                                        axis=0).astype(jnp.int32)


def _plan(info):
    n = info.shape[1]
    n_blocks = n // ROUTE_TM
    t_i = lax.broadcasted_iota(jnp.int32, (ROUTE_TM, ROUTE_TM), 0)
    t_j = lax.broadcasted_iota(jnp.int32, (ROUTE_TM, ROUTE_TM), 1)
    before = (t_i < t_j).astype(BF16)
    return pl.pallas_call(
        functools.partial(_plan_kernel, n_blocks=n_blocks),
        out_shape=(jax.ShapeDtypeStruct((8, n), jnp.int32), jax.ShapeDtypeStruct((8, LANES), jnp.int32)),
        grid=(n_blocks + 1,),
        in_specs=[_const_spec((8, n)), _const_spec((ROUTE_TM, ROUTE_TM))],
        out_specs=(pl.BlockSpec((8, n), lambda i: (0, 0)), pl.BlockSpec((8, LANES), lambda i: (0, 0))),
        scratch_shapes=[pltpu.VMEM((N_EXPERTS, n), F32), pltpu.VMEM((N_EXPERTS, LANES), F32)],
        compiler_params=_cparams(("arbitrary",)),
        name="plan",
    )(info, before)


def _sc_mesh():
    return plsc.VectorSubcoreMesh(core_axis_name="core", subcore_axis_name="subcore")


def _sc_scatter_rows(x, idx, n_out):
    n_idx = idx.shape[0]
    src_blocks = x.shape[0] // SC_WIN

    @pl.kernel(out_type=jax.ShapeDtypeStruct((n_out, SC_ROW), x.dtype), mesh=_sc_mesh(), scratch_types=[])
    def scatter_kernel(x_hbm, i_hbm, o_hbm):
        def body(x_vmem, i_vmem):
            pltpu.sync_copy(x_vmem, o_hbm.at[i_vmem.at[0]])

        pltpu.emit_pipeline(
            body,
            grid=(n_idx // SC_WIN,),
            in_specs=[pl.BlockSpec((SC_WIN, SC_ROW), index_map=lambda i: (i % src_blocks, 0)),
                      pl.BlockSpec((1, SC_WIN), index_map=lambda i: (0, i))],
            out_specs=[],
            core_axis_name=("core", "subcore"),
            dimension_semantics=(pltpu.PARALLEL,),
        )(x_hbm, i_hbm)

    return scatter_kernel(x, idx.reshape(1, n_idx))


def _sc_gather_rows(x, idx):
    n_idx = idx.shape[0]

    @pl.kernel(out_type=jax.ShapeDtypeStruct((n_idx, SC_ROW), x.dtype), mesh=_sc_mesh(), scratch_types=[])
    def gather_kernel(x_hbm, i_hbm, o_hbm):
        def body(i_vmem, o_vmem):
            pltpu.sync_copy(x_hbm.at[i_vmem.at[0]], o_vmem)

        pltpu.emit_pipeline(
            body,
            grid=(n_idx // SC_WIN,),
            in_specs=[pl.BlockSpec((1, SC_WIN), index_map=lambda i: (0, i))],
            out_specs=[pl.BlockSpec((SC_WIN, SC_ROW), index_map=lambda i: (i, 0))],
            core_axis_name=("core", "subcore"),
            dimension_semantics=(pltpu.PARALLEL,),
        )(i_hbm, o_hbm)

    return gather_kernel(x, idx.reshape(1, n_idx))


def _experts_kernel(tmap_ref, xs_ref, wg_e_ref, wu_e_ref, wd_e_ref, wg_o_ref, wu_o_ref, wd_o_ref, ys_ref,
                    wg_scr, wu_scr, wd_scr):
    j = pl.program_id(0)
    owner = tmap_ref[0, j]
    new_expert = (j == 0) | (owner != tmap_ref[0, jnp.maximum(j - 1, 0)])
    odd = lax.rem(owner, 2) == 1

    @pl.when(new_expert & jnp.logical_not(odd))
    def _():
        wg_scr[...] = wg_e_ref[0].astype(BF16)
        wu_scr[...] = wu_e_ref[0].astype(BF16)
        wd_scr[...] = wd_e_ref[0].astype(BF16)

    @pl.when(new_expert & odd)
    def _():
        wg_scr[...] = wg_o_ref[0].astype(BF16)
        wu_scr[...] = wu_o_ref[0].astype(BF16)
        wd_scr[...] = wd_o_ref[0].astype(BF16)

    @pl.when(tmap_ref[1, j] > 0)
    def _():
        x = _unpack_bf16_pairs(xs_ref[0:SLOT_TM, :], xs_ref[SLOT_TM:2 * SLOT_TM, :])
        a = _dot(x, wg_scr[...])
        u = _dot(x, wu_scr[...])
        act = (a * jax.nn.sigmoid(a) * u).astype(BF16)
        y = _dot(act, wd_scr[...]).astype(BF16).astype(F32)
        packed = _pack_bf16_pairs(y)
        ys_ref[0:SLOT_TM, :] = packed[:, :SC_ROW]
        ys_ref[SLOT_TM:2 * SLOT_TM, :] = packed[:, SC_ROW:]


def _experts(xs, tmap, wg, wu, wd):
    n_tiles = xs.shape[0] // (2 * SLOT_TM)
    grid_spec = pltpu.PrefetchScalarGridSpec(
        num_scalar_prefetch=1,
        grid=(n_tiles,),
        in_specs=[
            pl.BlockSpec((2 * SLOT_TM, SC_ROW), lambda j, tm: (j, 0)),
            pl.BlockSpec((1, D_MODEL, D_EXPERT), lambda j, tm: (tm[2, j], 0, 0)),
            pl.BlockSpec((1, D_MODEL, D_EXPERT), lambda j, tm: (tm[2, j], 0, 0)),
            pl.BlockSpec((1, D_EXPERT, D_MODEL), lambda j, tm: (tm[2, j], 0, 0)),
            pl.BlockSpec((1, D_MODEL, D_EXPERT), lambda j, tm: (tm[3, j], 0, 0)),
            pl.BlockSpec((1, D_MODEL, D_EXPERT), lambda j, tm: (tm[3, j], 0, 0)),
            pl.BlockSpec((1, D_EXPERT, D_MODEL), lambda j, tm: (tm[3, j], 0, 0)),
        ],
        out_specs=pl.BlockSpec((2 * SLOT_TM, SC_ROW), lambda j, tm: (j, 0)),
        scratch_shapes=[pltpu.VMEM((D_MODEL, D_EXPERT), BF16), pltpu.VMEM((D_MODEL, D_EXPERT), BF16),
                        pltpu.VMEM((D_EXPERT, D_MODEL), BF16)],
    )
    return pl.pallas_call(
        _experts_kernel,
        out_shape=jax.ShapeDtypeStruct(xs.shape, jnp.int32),
        grid_spec=grid_spec,
        compiler_params=_cparams(("arbitrary",)),
        name="experts",
    )(tmap, xs, wg, wu, wd, wg, wu, wd)


FIN_TM = 512


def _final_kernel(x1_ref, mod_ref, yg_ref, w_ref, gf_ref, y_ref):
    moe = None
    for k in range(2):
        base = 2 * k * FIN_TM
        yk = _unpack_bf16_pairs(yg_ref[base:base + FIN_TM, :], yg_ref[base + FIN_TM:base + 2 * FIN_TM, :])
        term = w_ref[:, k:k + 1] * yk.astype(F32)
        moe = term if moe is None else moe + term
    x2 = x1_ref[...] + mod_ref[0, 5:6, :] * moe
    y_ref[...] = x2 * _rms_scale(x2) * gf_ref[...]


def _final(x1, mod3, mod_row_fn, yg, w_cols, g_final):
    n = x1.shape[0]
    return pl.pallas_call(
        _final_kernel,
        out_shape=jax.ShapeDtypeStruct((n, D_MODEL), F32),
        grid=(n // FIN_TM,),
        in_specs=[
            pl.BlockSpec((FIN_TM, D_MODEL), lambda i: (i, 0)),
            pl.BlockSpec((1, 6, D_MODEL), lambda i: (mod_row_fn(i * FIN_TM), 0, 0)),
            pl.BlockSpec((4 * FIN_TM, SC_ROW), lambda i: (i, 0)),
            pl.BlockSpec((FIN_TM, 2), lambda i: (i, 0)),
            _const_spec((1, D_MODEL)),
        ],
        out_specs=pl.BlockSpec((FIN_TM, D_MODEL), lambda i: (i, 0)),
        compiler_params=_cparams(("parallel",)),
        name="final",
    )(x1, mod3, yg, w_cols, g_final)


def _half_row(pos, half):
    return (pos // SLOT_TM) * (2 * SLOT_TM) + half * SLOT_TM + pos % SLOT_TM


def _moe(x1, mod3, mod_row_fn, g_norm2, wr_hi, wr_lo, br, wg, wu, wd, g_final):
    n = x1.shape[0]
    h2p, info = _route(x1, mod3, mod_row_fn, g_norm2, wr_hi, wr_lo, br)
    pos, tmap = _plan(info)
    n_slots = 2 * n + N_EXPERTS * SLOT_TM
    pos1, pos2 = pos[0], pos[1]

    def route_order(p):
        return jnp.stack([_half_row(p, 0).reshape(n // ROUTE_TM, ROUTE_TM),
                          _half_row(p, 1).reshape(n // ROUTE_TM, ROUTE_TM)], axis=1).reshape(-1)

    xs = _sc_scatter_rows(h2p, jnp.concatenate([route_order(pos1), route_order(pos2)]), 2 * n_slots)
    ys = _experts(xs, tmap, wg, wu, wd)

    def fin_order(p):
        return jnp.stack([_half_row(p, 0).reshape(n // FIN_TM, FIN_TM),
                          _half_row(p, 1).reshape(n // FIN_TM, FIN_TM)], axis=1)

    gidx = jnp.stack([fin_order(pos1), fin_order(pos2)], axis=1).reshape(-1)
    yg = _sc_gather_rows(ys, gidx)
    w_cols = info[2:4].T
    return _final(x1, mod3, mod_row_fn, yg, w_cols, g_final)


def _stream(x, mod3, mod_row_fn, seq_len, rows_2d, state, w):
    bsz, t, _ = x.shape
    n = bsz * t
    x2d = x.reshape(n, D_MODEL)
    p3, bcum, imb, amax = _in_projection(x2d, mod3, mod_row_fn, w["g_norm1"], w["wmain"], w["bmain"],
                                         w["colscale"], w["wg_hi"], w["wg_lo"], w["bg"])
    m0 = None if state is None else state[2]
    cols, rw, cs = _gate_prep(bcum, imb, amax, m0)
    rows = rw[:, :2 * N_HD].T.reshape(2 * N_HD, 1, n)
    ml = _mlstm(p3, cols, rows, cs, w["gn3"], None if state is None else state[:2])
    amat, invc = _pool_constants(seq_len, rows_2d)
    n_seq = max(1, 1024 // seq_len)
    yb = _pool(p3, amat, invc, w["wpg"], w["pscale"], seq_len, n_seq, rows_2d)
    ya = ml if state is not None else ml[0]
    x1 = _mix(x2d, mod3, mod_row_fn, ya, yb, p3, w["wa"], w["wb"], w["wo"])
    y = _moe(x1, mod3, mod_row_fn, w["g_norm2"], w["wr_hi"], w["wr_lo"], w["br"],
             w["wg"], w["wu"], w["wd"], w["g_final"])
    return y.reshape(bsz, t, D_MODEL), ml, cs


def kernel(x_prompt, x_sample, state_C, state_n, state_m, c, c_ctx, w_mod, b_mod, g_norm1, w_in, b_in, gn_gain, w_pool_grp, pool_scale, w_proj_a, w_proj_b, w_out, g_norm2, w_router_grp, b_router_grp, w_router_exp, b_router_exp, w_exp_gate, w_exp_up, w_exp_down, g_final):
    bp, tp, _ = x_prompt.shape
    bs, ts, _ = x_sample.shape
    assert w_mod.shape[0] == 1, "single trunk layer"
    assert tp == CHUNK and ts % CHUNK == 0 and ts == ML_ROWS

    cvec = jnp.zeros((8, D_MODEL), F32).at[0].set(c_ctx).at[1:1 + bs].set(c)
    mod3 = _modulation(cvec, w_mod[0], b_mod[0])

    wi, bi = w_in[0], b_in[0]
    d = D_MODEL
    o_q, o_k, o_v, o_o, o_g, o_p, o_ga, o_gb = 0, d, 2 * d, 3 * d, 4 * d, 4 * d + 32, 4 * d + 32 + 512, 4 * d + 32 + 512 + d
    order = [(o_q, d), (o_k, d), (o_v, d), (o_o, d), (o_ga, d), (o_gb, d), (o_p, 512)]
    wmain = _pack_columns(wi, order)
    bmain = jnp.concatenate([bi[s:s + l] for s, l in order]).reshape(1, D_MAIN)
    colscale = jnp.ones((D_MAIN,), F32).at[SLAB_K * LANES:(SLAB_K + N_HEADS) * LANES].set(HEAD_DIM ** -0.5)
    colscale = colscale.reshape(1, D_MAIN)
    gwt = _take_columns_t(wi, o_g, 32)
    gb = bi[o_g:o_g + 32]
    i_f, f_f, i_b, f_b = (slice(0, 8), slice(8, 16), slice(16, 24), slice(24, 32))
    pad_w = jnp.zeros((LANES - N_HD, d), F32)
    pad_b = jnp.zeros((LANES - N_HD,), F32)
    wg2 = jnp.concatenate([gwt[f_f], gwt[f_b], pad_w, gwt[i_f], gwt[i_b], pad_w], axis=0).T
    bg2 = jnp.concatenate([gb[f_f], gb[f_b], pad_b, gb[i_f], gb[i_b], pad_b]).reshape(1, 2 * LANES)
    wg_hi, wg_lo = _split_bf16(wg2)

    wr = jnp.concatenate([w_router_grp[0], w_router_exp[0].reshape(d, N_EXPERTS),
                          jnp.zeros((d, LANES - ROUTER_ROWS), F32)], axis=1)
    br = jnp.concatenate([b_router_grp[0], b_router_exp[0].reshape(N_EXPERTS),
                          jnp.zeros((LANES - ROUTER_ROWS,), F32)]).reshape(1, LANES)
    wr_hi, wr_lo = _split_bf16(wr)

    w = dict(
        g_norm1=g_norm1[0].reshape(1, d), wmain=wmain, bmain=bmain, colscale=colscale,
        wg_hi=wg_hi, wg_lo=wg_lo, bg=bg2,
        gn3=gn_gain[0].reshape(N_HEADS, 1, HEAD_DIM),
        wpg=w_pool_grp[0].astype(BF16), pscale=pool_scale[0].reshape(N_GROUPS, 1, LANES),
        wa=w_proj_a[0].astype(BF16), wb=w_proj_b[0].astype(BF16), wo=w_out[0].astype(BF16),
        g_norm2=g_norm2[0].reshape(1, d), wr_hi=wr_hi, wr_lo=wr_lo, br=br,
        wg=w_exp_gate[0], wu=w_exp_up[0], wd=w_exp_down[0],
        g_final=g_final.reshape(1, d),
    )

    y_prompt, (_, cst, nst), cs_prompt = _stream(x_prompt, mod3, lambda r: 0, tp, None, None, w)
    n_rep = jnp.broadcast_to(state_n[..., None], state_C.shape)
    m0 = jnp.pad(state_m[:, 0].reshape(bs, N_HD), ((0, 0), (0, LANES - N_HD))).reshape(bs, 1, LANES)
    y_sample, _, _ = _stream(x_sample, mod3, lambda r: 1 + r // ts, ts, ts // GRID_W, (state_C, n_rep, m0), w)

    new_c = cst
    new_n = nst.reshape(bp, 1, 2, N_HEADS, HEAD_DIM)
    new_m = cs_prompt[:, 1, :N_HD].reshape(bp, 1, 2, N_HEADS)
    return (y_prompt, y_sample, new_c, new_n, new_m)
```

```python
import functools

import numpy as np
import jax
import jax.numpy as jnp
from jax import lax
from jax.experimental import pallas as pl
from jax.experimental.pallas import tpu as pltpu
from jax.experimental.pallas import tpu_sc as plsc

F32 = jnp.float32
BF16 = jnp.bfloat16

D_MODEL = 1024
N_HEADS = 8
HEAD_DIM = 128
GRID_W = 64
POOL_WINDOWS = (2, 4, 8, 16)
N_GROUPS = 4
EXPERTS_PER_GROUP = 4
N_EXPERTS = 16
D_EXPERT = 512
EPS = 1e-6
NEG = -1e30

LANES = 128
CHUNK = 256
N_SLABS = 52
SLAB_Q, SLAB_K, SLAB_V, SLAB_O, SLAB_GA, SLAB_GB, SLAB_POOL = 0, 8, 16, 24, 32, 40, 48
D_MAIN = N_SLABS * LANES
VMEM_LIMIT = 56 * 1024 * 1024


def _cparams(sem):
    return pltpu.CompilerParams(dimension_semantics=sem, vmem_limit_bytes=VMEM_LIMIT)


def _const_spec(shape):
    nd = len(shape)
    return pl.BlockSpec(shape, lambda *_: (0,) * nd, pipeline_mode=pl.Buffered(1))


def _split_bf16(x):
    hi = x.astype(BF16)
    lo = (x - hi.astype(F32)).astype(BF16)
    return hi, lo


def _dot(a, b):
    return jnp.dot(a, b, preferred_element_type=F32)


def _dot3(a, b_hi, b_lo):
    a_hi, a_lo = _split_bf16(a)
    return _dot(a_hi, b_hi) + _dot(a_hi, b_lo) + _dot(a_lo, b_hi)


def _rms_scale(x):
    return lax.rsqrt(jnp.mean(x * x, axis=-1, keepdims=True) + EPS)


def _log_sigmoid(x):
    return jnp.minimum(x, 0.0) - jnp.log1p(jnp.exp(-jnp.abs(x)))


def _mod_kernel(c_ref, w_ref, b_ref, o_ref):
    c = c_ref[...]
    s = c * jax.nn.sigmoid(c)
    w_hi, w_lo = _split_bf16(w_ref[...])
    o_ref[...] = _dot3(s, w_hi, w_lo) + b_ref[...]


def _modulation(cvec, w_mod, b_mod):
    tn = 512
    out = pl.pallas_call(
        _mod_kernel,
        out_shape=jax.ShapeDtypeStruct((8, 6 * D_MODEL), F32),
        grid=(6 * D_MODEL // tn,),
        in_specs=[
            pl.BlockSpec((8, D_MODEL), lambda j: (0, 0)),
            pl.BlockSpec((D_MODEL, tn), lambda j: (0, j)),
            pl.BlockSpec((1, tn), lambda j: (0, j)),
        ],
        out_specs=pl.BlockSpec((8, tn), lambda j: (0, j)),
        compiler_params=_cparams(("parallel",)),
        name="modulation",
    )(cvec, w_mod, b_mod.reshape(1, -1))
    return out.reshape(8, 6, D_MODEL)


PACK_COLS = 512


def _pack_kernel(wt_ref, o_ref):
    o_ref[...] = wt_ref[...].T.astype(BF16)


def _pack_columns(w, order):
    k, n = w.shape
    starts = []
    for src, width in order:
        assert width % PACK_COLS == 0
        starts += [src + j for j in range(0, width, PACK_COLS)]
    n_out = len(starts) * PACK_COLS

    unit = 32
    assert all(s % unit == 0 for s in starts)

    def src_row(j):
        row = jnp.int32(starts[-1] // unit)
        for idx in range(len(starts) - 2, -1, -1):
            row = jnp.where(j == idx, starts[idx] // unit, row)
        return row * unit

    return pl.pallas_call(
        _pack_kernel,
        out_shape=jax.ShapeDtypeStruct((k, n_out), BF16),
        grid=(len(starts),),
        in_specs=[pl.BlockSpec((pl.Element(PACK_COLS), pl.Element(k)), lambda j: (src_row(j), 0))],
        out_specs=pl.BlockSpec((k, PACK_COLS), lambda j: (0, j)),
        compiler_params=_cparams(("parallel",)),
        name="pack_columns",
    )(w.T)


def _copy_kernel(w_ref, o_ref):
    o_ref[...] = w_ref[...]


def _take_columns_t(w, start, width):
    k = w.shape[0]
    return pl.pallas_call(
        _copy_kernel,
        out_shape=jax.ShapeDtypeStruct((width, k), w.dtype),
        grid=(1,),
        in_specs=[pl.BlockSpec((pl.Element(width), pl.Element(k)), lambda j: (start, 0))],
        out_specs=pl.BlockSpec((width, k), lambda j: (0, 0)),
        name="take_columns",
    )(w.T)


IN_TM = 512
IN_TN = 512
LOG2E = 1.4426950408889634
N_HD = 2 * N_HEADS


def _fwd_lanes(shape):
    lane = lax.broadcasted_iota(jnp.int32, shape, len(shape) - 1)
    return (lane & N_HEADS) == 0


def _inproj_kernel(x_ref, mod_ref, g_ref, w_ref, b_ref, cs_ref, wg_hi_ref, wg_lo_ref, bg_ref,
                   p3_ref, bcum_ref, imb_ref, amax_ref):
    x = x_ref[...]
    h = x * _rms_scale(x) * g_ref[...]
    h = h * (1.0 + mod_ref[0, 1:2, :]) + mod_ref[0, 0:1, :]
    hb = h.astype(BF16)
    for n in range(D_MAIN // IN_TN):
        sl = slice(n * IN_TN, (n + 1) * IN_TN)
        acc = (_dot(hb, w_ref[:, sl]) + b_ref[:, sl]) * cs_ref[:, sl]
        accb = acc.astype(BF16)
        for j in range(IN_TN // LANES):
            p3_ref[n * (IN_TN // LANES) + j] = accb[:, j * LANES:(j + 1) * LANES]

    g2 = _dot3(h, wg_hi_ref[...], wg_lo_ref[...]) + bg_ref[...]
    ls = _log_sigmoid(g2[:, :LANES])
    gi = g2[:, LANES:]
    row = lax.broadcasted_iota(jnp.int32, (CHUNK, LANES), 0)
    fwd = _fwd_lanes((CHUNK, LANES))
    for c in range(IN_TM // CHUNK):
        rows = slice(c * CHUNK, (c + 1) * CHUNK)
        cf = ls[rows]
        cb = cf
        s = 1
        while s < CHUNK:
            cf = cf + jnp.where(row >= s, pltpu.roll(cf, s, 0), 0.0)
            cb = cb + jnp.where(row < CHUNK - s, pltpu.roll(cb, CHUNK - s, 0), 0.0)
            s *= 2
        b = jnp.where(fwd, cf, cb)
        imb = gi[rows] - b
        mf = imb
        mb = imb
        s = 1
        while s < CHUNK:
            mf = jnp.maximum(mf, jnp.where(row >= s, pltpu.roll(mf, s, 0), -jnp.inf))
            mb = jnp.maximum(mb, jnp.where(row < CHUNK - s, pltpu.roll(mb, CHUNK - s, 0), -jnp.inf))
            s *= 2
        bcum_ref[rows, :] = b
        imb_ref[rows, :] = imb
        amax_ref[rows, :] = b + jnp.where(fwd, mf, mb)


def _in_projection(x2d, mod3, mod_row_fn, g_norm1, wmain, bmain, colscale, wg_hi, wg_lo, bg):
    n = x2d.shape[0]
    return pl.pallas_call(
        _inproj_kernel,
        out_shape=(jax.ShapeDtypeStruct((N_SLABS, n, LANES), BF16),) + (jax.ShapeDtypeStruct((n, LANES), F32),) * 3,
        grid=(n // IN_TM,),
        in_specs=[
            pl.BlockSpec((IN_TM, D_MODEL), lambda i: (i, 0)),
            pl.BlockSpec((1, 6, D_MODEL), lambda i: (mod_row_fn(i * IN_TM), 0, 0)),
            _const_spec((1, D_MODEL)),
            _const_spec((D_MODEL, D_MAIN)),
            _const_spec((1, D_MAIN)),
            _const_spec((1, D_MAIN)),
            _const_spec((D_MODEL, 2 * LANES)),
            _const_spec((D_MODEL, 2 * LANES)),
            _const_spec((1, 2 * LANES)),
        ],
        out_specs=(pl.BlockSpec((N_SLABS, IN_TM, LANES), lambda i: (0, i, 0)),)
        + (pl.BlockSpec((IN_TM, LANES), lambda i: (i, 0)),) * 3,
        compiler_params=_cparams(("parallel",)),
        name="in_projection",
    )(x2d, mod3, g_norm1, wmain, bmain, colscale, wg_hi, wg_lo, bg)


GP_ROWS = 1024
GP_NCH = GP_ROWS // CHUNK
Q_C2, Q_EM, Q_WI, Q_WC = 0, 1, 2, 3


def _gate_prep_kernel(*refs, carry):
    if carry:
        b_ref, imb_ref, a_ref, m0_ref, cols_ref, rw_ref, cs_ref = refs
    else:
        b_ref, imb_ref, a_ref, cols_ref, rw_ref, cs_ref = refs
    fwd1 = _fwd_lanes((1, LANES))
    lane = lax.broadcasted_iota(jnp.int32, (CHUNK, LANES), 1)
    neg = jnp.full((1, LANES), NEG, F32)

    def at_scan_end(ref, c):
        return jnp.where(fwd1, ref[(c + 1) * CHUNK - 1:(c + 1) * CHUNK, :], ref[c * CHUNK:c * CHUNK + 1, :])

    b_last = [at_scan_end(b_ref, c) for c in range(GP_NCH)]
    a_last = [at_scan_end(a_ref, c) for c in range(GP_NCH)]
    m_start = m0_ref[0] if carry else neg
    m_prev = [[None] * GP_NCH, [None] * GP_NCH]
    m_new = [[None] * GP_NCH, [None] * GP_NCH]
    for d, order in enumerate((range(GP_NCH), range(GP_NCH - 1, -1, -1))):
        m = m_start
        for c in order:
            m_prev[d][c] = m
            m_new[d][c] = jnp.maximum(b_last[c] + m, a_last[c])
            m = m_new[d][c] if carry else neg

    for c in range(GP_NCH):
        rows = slice(c * CHUNK, (c + 1) * CHUNK)
        mp = jnp.where(fwd1, m_prev[0][c], m_prev[1][c])
        mn = jnp.where(fwd1, m_new[0][c], m_new[1][c])
        b = b_ref[rows, :]
        imb = imb_ref[rows, :]
        inter = b + mp
        m_t = jnp.maximum(inter, a_ref[rows, :])
        c2 = (b - m_t) * LOG2E
        em = jnp.exp(-m_t)
        wc = jnp.exp(b_last[c] - mn + imb)
        packed = jnp.where(lane < N_HD, c2, pltpu.roll(em, N_HD * Q_EM, 1))
        if carry:
            wi = jnp.exp(inter - m_t)
            packed = jnp.where(lane < N_HD * Q_WI, packed, pltpu.roll(wi, N_HD * Q_WI, 1))
        cols_ref[rows, :] = jnp.where(lane < N_HD * Q_WC, packed, pltpu.roll(wc, N_HD * Q_WC, 1))
        rw_ref[rows, :] = jnp.where(lane < N_HD, imb * (-LOG2E), pltpu.roll(wc, N_HD, 1))
        cs_ref[c, 0:1, :] = jnp.exp(b_last[c] + mp - mn)
        cs_ref[c, 1:2, :] = mn


def _gate_prep(bcum, imb, amax, m0=None):
    n = bcum.shape[0]
    carry = m0 is not None
    blk = pl.BlockSpec((GP_ROWS, LANES), lambda i: (i, 0))
    in_specs = [blk, blk, blk]
    args = [bcum, imb, amax]
    if carry:
        in_specs.append(pl.BlockSpec((1, 1, LANES), lambda i: (i, 0, 0)))
        args.append(m0)
    return pl.pallas_call(
        functools.partial(_gate_prep_kernel, carry=carry),
        out_shape=(jax.ShapeDtypeStruct((n, LANES), F32), jax.ShapeDtypeStruct((n, LANES), F32),
                   jax.ShapeDtypeStruct((n // CHUNK, 2, LANES), F32)),
        grid=(n // GP_ROWS,),
        in_specs=in_specs,
        out_specs=(blk, blk, pl.BlockSpec((GP_NCH, 2, LANES), lambda i: (i, 0, 0))),
        compiler_params=_cparams(("parallel",)),
        name="gate_prep",
    )(*args)


ML_ROWS = 1024
ML_NCH = ML_ROWS // CHUNK


def _pick_col(blk, lane_iota, idx):
    return jnp.sum(jnp.where(lane_iota == idx, blk, 0.0), axis=1, keepdims=True)


def _mlstm_kernel(*refs, carry):
    if carry:
        (q_ref, k_ref, v_ref, o_ref, cols_ref, r2f_ref, r2b_ref, gn_ref, cs_ref, c0_ref, n0_ref,
         ya_ref, s_scr, h_scr) = refs
    else:
        (q_ref, k_ref, v_ref, o_ref, cols_ref, r2f_ref, r2b_ref, gn_ref, wrf_ref, wrb_ref,
         ya_ref, cst_ref, nst_ref, s_scr, h_scr) = refs
    head = pl.program_id(1)
    lane = lax.broadcasted_iota(jnp.int32, (CHUNK, LANES), 1)
    lane1 = lax.broadcasted_iota(jnp.int32, (1, LANES), 1)
    r_idx = lax.broadcasted_iota(jnp.int32, (CHUNK, CHUNK), 0)
    c_idx = lax.broadcasted_iota(jnp.int32, (CHUNK, CHUNK), 1)
    ones = jnp.ones((CHUNK, HEAD_DIM), BF16)
    tn_dims = (((0,), (0,)), ((), ()))

    for d in range(2):
        mask = (c_idx <= r_idx) if d == 0 else (c_idx >= r_idx)
        r2_ref = r2f_ref if d == 0 else r2b_ref
        base = N_HEADS * d + head
        if carry:
            cn = jnp.concatenate([c0_ref[0, 0, d, 0], n0_ref[0, 0, d, 0]], axis=1)
        order = range(ML_NCH) if d == 0 else range(ML_NCH - 1, -1, -1)
        for step, c in enumerate(order):
            rows = slice(c * CHUNK, (c + 1) * CHUNK)
            q = q_ref[0, rows, :]
            k = k_ref[0, rows, :]
            v = v_ref[0, rows, :]
            v1 = jnp.concatenate([v, ones], axis=1)
            if d == 0:
                s = lax.dot_general(q, k, (((1,), (1,)), ((), ())), preferred_element_type=F32)
                s_scr[rows, :] = s
            else:
                s = s_scr[rows, :]
            blk = cols_ref[rows, :]
            arg = _pick_col(blk, lane, base + N_HD * Q_C2) - r2_ref[0, :, rows]
            sw = s * jnp.exp2(jnp.where(mask, arg, NEG))
            nd = _dot(sw.astype(BF16), v1)
            if carry:
                nd = nd + _pick_col(blk, lane, base + N_HD * Q_WI) * _dot(q, cn.astype(BF16))
            hdir = nd[:, :HEAD_DIM] / jnp.maximum(jnp.abs(nd[:, HEAD_DIM:]), _pick_col(blk, lane, base + N_HD * Q_EM))
            if d == 0:
                h_scr[rows, :] = hdir
            else:
                h_scr[rows, :] = h_scr[rows, :] + hdir

            if carry and step == ML_NCH - 1:
                continue
            w_col = _pick_col(blk, lane, base + N_HD * Q_WC)
            if carry:
                u = lax.dot_general(k, (w_col * v1.astype(F32)).astype(BF16), tn_dims,
                                    preferred_element_type=F32)
                decay = jnp.sum(jnp.where(lane1 == base, cs_ref[c, 0:1, :], 0.0), axis=1, keepdims=True)
                cn = decay * cn + u
            else:
                w_row = (wrf_ref if d == 0 else wrb_ref)[0, :, rows]
                w8 = jnp.broadcast_to(w_row, (8, CHUNK)).astype(BF16)
                cst_ref[c, 0, d, 0] = lax.dot_general(k, (w_col * v.astype(F32)).astype(BF16), tn_dims,
                                                      preferred_element_type=F32)
                nst_ref[c, d, 0] = _dot(w8, k)[0:1, :]

    hh = h_scr[...]
    hn = hh * _rms_scale(hh) * gn_ref[0]
    ya_ref[0] = (hn * jax.nn.sigmoid(o_ref[0].astype(F32))).astype(BF16)


def _mlstm(p3, cols, rows, cs, gn3, state=None):
    n = p3.shape[1]
    nblk = n // ML_ROWS
    carry = state is not None

    def slab(base):
        return pl.BlockSpec((1, ML_ROWS, LANES), lambda g, h: (base + h, g, 0))

    def row(base):
        return pl.BlockSpec((1, 1, ML_ROWS), lambda g, h: (base + h, 0, g))

    in_specs = [
        slab(SLAB_Q), slab(SLAB_K), slab(SLAB_V), slab(SLAB_O),
        pl.BlockSpec((ML_ROWS, LANES), lambda g, h: (g, 0)),
        row(0), row(N_HEADS),
        pl.BlockSpec((1, 1, LANES), lambda g, h: (h, 0, 0)),
    ]
    args = [p3, p3, p3, p3, cols, rows, rows, gn3]
    ya_shape = jax.ShapeDtypeStruct((N_HEADS, n, LANES), BF16)
    ya_spec = pl.BlockSpec((1, ML_ROWS, LANES), lambda g, h: (h, g, 0))
    state_spec = pl.BlockSpec((1, 1, 2, 1, HEAD_DIM, HEAD_DIM), lambda g, h: (g, 0, 0, h, 0, 0))
    if carry:
        c0, n0 = state
        in_specs += [pl.BlockSpec((ML_NCH, 2, LANES), lambda g, h: (g, 0, 0)), state_spec, state_spec]
        args += [cs, c0, n0]
        out_shape = ya_shape
        out_specs = ya_spec
    else:
        in_specs += [row(N_HD), row(N_HD + N_HEADS)]
        args += [rows, rows]
        nseq = n // CHUNK
        out_shape = (ya_shape,
                     jax.ShapeDtypeStruct((nseq, 1, 2, N_HEADS, HEAD_DIM, HEAD_DIM), F32),
                     jax.ShapeDtypeStruct((nseq, 2, N_HEADS, 1, HEAD_DIM), F32))
        out_specs = (ya_spec,
                     pl.BlockSpec((ML_NCH, 1, 2, 1, HEAD_DIM, HEAD_DIM), lambda g, h: (g, 0, 0, h, 0, 0)),
                     pl.BlockSpec((ML_NCH, 2, 1, 1, HEAD_DIM), lambda g, h: (g, 0, h, 0, 0)))
    return pl.pallas_call(
        functools.partial(_mlstm_kernel, carry=carry),
        out_shape=out_shape,
        grid=(nblk, N_HEADS),
        in_specs=in_specs,
        out_specs=out_specs,
        scratch_shapes=[pltpu.VMEM((ML_ROWS, CHUNK), F32), pltpu.VMEM((ML_ROWS, HEAD_DIM), F32)],
        compiler_params=_cparams(("parallel", "parallel")),
        name="mlstm_latent" if carry else "mlstm_prompt",
    )(*args)


def _pool_kernel(u_ref, a_ref, ic_ref, wpg_ref, ps_ref, yb_ref, *, seq_len, n_seq, rows_2d):
    piece = seq_len if rows_2d is None else GRID_W
    n_piece = n_seq if rows_2d is None else rows_2d
    for g in range(N_GROUPS):
        u = jnp.concatenate([u_ref[g, i * piece:(i + 1) * piece, :] for i in range(n_piece)], axis=1)
        win = _dot(a_ref[g], u)
        if rows_2d is None:
            inv_cnt = jnp.concatenate([ic_ref[g]] * n_piece, axis=1)
        else:
            run = [jnp.zeros((piece, LANES), F32)]
            for r in range(rows_2d):
                run.append(run[-1] + win[:, r * LANES:(r + 1) * LANES])
            half = POOL_WINDOWS[g] // 2
            win = jnp.concatenate([run[min(r - half + POOL_WINDOWS[g], rows_2d)] - run[max(r - half, 0)]
                                   for r in range(rows_2d)], axis=1)
            inv_cnt = jnp.concatenate([ic_ref[g, r * piece:(r + 1) * piece, :] for r in range(rows_2d)], axis=1)
        p = (win * inv_cnt - u.astype(F32)).astype(BF16)
        p = jnp.concatenate([p[:, i * LANES:(i + 1) * LANES] for i in range(n_piece)], axis=0)
        yb_ref[g] = (_dot(p, wpg_ref[g]) * ps_ref[g]).astype(BF16)


def _pool(p3, amat, invc, wpg, pscale, seq_len, n_seq, rows_2d):
    n = p3.shape[1]
    rows = seq_len * n_seq
    return pl.pallas_call(
        functools.partial(_pool_kernel, seq_len=seq_len, n_seq=n_seq, rows_2d=rows_2d),
        out_shape=jax.ShapeDtypeStruct((N_GROUPS, n, LANES), BF16),
        grid=(n // rows,),
        in_specs=[
            pl.BlockSpec((N_GROUPS, rows, LANES), lambda i: (SLAB_POOL // N_GROUPS, i, 0)),
            _const_spec(amat.shape),
            _const_spec((N_GROUPS, seq_len, LANES)),
            _const_spec((N_GROUPS, LANES, LANES)),
            _const_spec((N_GROUPS, 1, LANES)),
        ],
        out_specs=pl.BlockSpec((N_GROUPS, rows, LANES), lambda i: (0, i, 0)),
        compiler_params=_cparams(("parallel",)),
        name="pool",
    )(p3, amat, invc, wpg, pscale)


def _window_matrix(size, w):
    idx = np.arange(size)
    lo = np.clip(idx - w // 2, 0, size)
    hi = np.clip(idx - w // 2 + w, 0, size)
    s = np.arange(size)[None, :]
    a = ((s >= lo[:, None]) & (s < hi[:, None])).astype(np.float32)
    return a, (hi - lo).astype(np.float32)


def _pool_constants(seq_len, rows_2d):
    mats, invs = [], []
    for w in POOL_WINDOWS:
        if rows_2d is None:
            a, cnt = _window_matrix(seq_len, w)
        else:
            _, cr = _window_matrix(rows_2d, w)
            a, cc = _window_matrix(GRID_W, w)
            cnt = np.kron(cr, cc)
        mats.append(a)
        invs.append(np.repeat((1.0 / cnt)[:, None], LANES, axis=1))
    return (jnp.asarray(np.stack(mats), dtype=BF16), jnp.asarray(np.stack(invs), dtype=F32))


MIX_TM = 512


def _mix_kernel(x_ref, mod_ref, ya_ref, yb_ref, ga_ref, gb_ref, wa_ref, wb_ref, wo_ref,
                g2_ref, wr_hi_ref, wr_lo_ref, br_ref, x1_ref, h2_ref, info_ref):
    ya = jnp.concatenate([ya_ref[i] for i in range(N_HEADS)], axis=1)
    yb = jnp.concatenate([yb_ref[i] for i in range(N_GROUPS)], axis=1)
    ga = jnp.concatenate([ga_ref[i] for i in range(8)], axis=1).astype(F32)
    gb = jnp.concatenate([gb_ref[i] for i in range(8)], axis=1).astype(F32)
    merged = jax.nn.sigmoid(ga) * _dot(ya, wa_ref[...]) + jax.nn.sigmoid(gb) * _dot(yb, wb_ref[...])
    out = _dot(merged.astype(BF16), wo_ref[...])
    x1 = x_ref[...] + mod_ref[0, 2:3, :] * out
    x1_ref[...] = x1

    h2 = x1 * _rms_scale(x1) * g2_ref[...]
    h2 = h2 * (1.0 + mod_ref[0, 4:5, :]) + mod_ref[0, 3:4, :]
    h_hi, h_lo = _split_bf16(h2)
    packed = _pack_bf16_pairs(h_hi.astype(F32))
    h2_ref[0:MIX_TM, :] = packed[:, :SC_ROW]
    h2_ref[MIX_TM:2 * MIX_TM, :] = packed[:, SC_ROW:]
    wr_hi = wr_hi_ref[...]
    logits = _dot(h_hi, wr_hi) + _dot(h_lo, wr_hi) + _dot(h_hi, wr_lo_ref[...]) + br_ref[...]
    first, second, w1, w2 = _route_rows(logits.T)
    zero = jnp.zeros_like(w1)
    info_ref[...] = jnp.concatenate([first, second, w1, w2, zero, zero, zero, zero], axis=0)


def _mix(x2d, mod3, mod_row_fn, ya, yb, p3, wa, wb, wo, g_norm2, wr_hi, wr_lo, br):
    n = x2d.shape[0]
    return pl.pallas_call(
        _mix_kernel,
        out_shape=(jax.ShapeDtypeStruct((n, D_MODEL), F32),
                   jax.ShapeDtypeStruct((2 * n, SC_ROW), jnp.int32), jax.ShapeDtypeStruct((8, n), F32)),
        grid=(n // MIX_TM,),
        in_specs=[
            pl.BlockSpec((MIX_TM, D_MODEL), lambda i: (i, 0)),
            pl.BlockSpec((1, 6, D_MODEL), lambda i: (mod_row_fn(i * MIX_TM), 0, 0)),
            pl.BlockSpec((N_HEADS, MIX_TM, LANES), lambda i: (0, i, 0)),
            pl.BlockSpec((N_GROUPS, MIX_TM, LANES), lambda i: (0, i, 0)),
            pl.BlockSpec((8, MIX_TM, LANES), lambda i: (SLAB_GA // 8, i, 0)),
            pl.BlockSpec((8, MIX_TM, LANES), lambda i: (SLAB_GB // 8, i, 0)),
            _const_spec((D_MODEL, D_MODEL)),
            _const_spec((N_GROUPS * LANES, D_MODEL)),
            _const_spec((D_MODEL, D_MODEL)),
            _const_spec((1, D_MODEL)),
            _const_spec((D_MODEL, LANES)),
            _const_spec((D_MODEL, LANES)),
            _const_spec((1, LANES)),
        ],
        out_specs=(pl.BlockSpec((MIX_TM, D_MODEL), lambda i: (i, 0)),
                   pl.BlockSpec((2 * MIX_TM, SC_ROW), lambda i: (i, 0)),
                   pl.BlockSpec((8, MIX_TM), lambda i: (0, i))),
        compiler_params=_cparams(("parallel",)),
        name="mix",
    )(x2d, mod3, ya, yb, p3, p3, wa, wb, wo, g_norm2, wr_hi, wr_lo, br)


PLAN_TB = 1024
ROUTER_ROWS = 4 + N_EXPERTS
SLOT_TM = 512
SC_ROW = 256
SC_WIN = 128
HI_MASK = -65536


def _pack_bf16_pairs(x):
    bits = pltpu.bitcast(x, jnp.int32)
    half = D_MODEL // 2
    return jnp.bitwise_or(jnp.bitwise_and(lax.shift_right_logical(bits[:, :half], 16), 0xFFFF),
                          jnp.bitwise_and(bits[:, half:], HI_MASK))


def _unpack_bf16_pairs(lo_words, hi_words):
    def low(w):
        return pltpu.bitcast(lax.shift_left(w, 16), F32)

    def high(w):
        return pltpu.bitcast(jnp.bitwise_and(w, HI_MASK), F32)

    return jnp.concatenate([low(lo_words), low(hi_words), high(lo_words), high(hi_words)], axis=1).astype(BF16)


def _route_rows(lt):
    lg = [lt[i:i + 1] for i in range(N_GROUPS)]
    mx = jnp.maximum(jnp.maximum(lg[0], lg[1]), jnp.maximum(lg[2], lg[3]))
    p_top = 1.0 / (jnp.exp(lg[0] - mx) + jnp.exp(lg[1] - mx) + jnp.exp(lg[2] - mx) + jnp.exp(lg[3] - mx))
    gidx = jnp.zeros(lg[0].shape, jnp.int32)
    best = lg[0]
    for i in range(1, N_GROUPS):
        upd = lg[i] > best
        gidx = jnp.where(upd, i, gidx)
        best = jnp.where(upd, lg[i], best)

    def expert_row(g, e):
        r = N_GROUPS + g * EXPERTS_PER_GROUP + e
        return lt[r:r + 1]

    le = [jnp.where(gidx == 0, expert_row(0, e),
                    jnp.where(gidx == 1, expert_row(1, e),
                              jnp.where(gidx == 2, expert_row(2, e), expert_row(3, e))))
          for e in range(EXPERTS_PER_GROUP)]
    v1 = jnp.maximum(jnp.maximum(le[0], le[1]), jnp.maximum(le[2], le[3]))
    i1 = jnp.full(v1.shape, EXPERTS_PER_GROUP - 1, jnp.int32)
    for e in range(EXPERTS_PER_GROUP - 2, -1, -1):
        i1 = jnp.where(le[e] == v1, e, i1)
    le2 = [jnp.where(i1 == e, -jnp.inf, le[e]) for e in range(EXPERTS_PER_GROUP)]
    v2 = jnp.maximum(jnp.maximum(le2[0], le2[1]), jnp.maximum(le2[2], le2[3]))
    i2 = jnp.full(v2.shape, EXPERTS_PER_GROUP - 1, jnp.int32)
    for e in range(EXPERTS_PER_GROUP - 2, -1, -1):
        i2 = jnp.where(le2[e] == v2, e, i2)
    e2 = jnp.exp(v2 - v1)
    first = (gidx * EXPERTS_PER_GROUP + i1).astype(F32)
    second = (gidx * EXPERTS_PER_GROUP + i2).astype(F32)
    return first, second, p_top / (1.0 + e2), p_top * e2 / (1.0 + e2)


def _plan_kernel(info_ref, before_ref, pos_ref, tmap_ref, rank_scr, count_scr, *, n_blocks):
    step = pl.program_id(0)
    expert_id = lax.broadcasted_iota(jnp.int32, (N_EXPERTS, PLAN_TB), 0).astype(F32)

    @pl.when(step == 0)
    def _():
        count_scr[...] = jnp.zeros_like(count_scr)

    @pl.when(step < n_blocks)
    def _():
        cols = pl.ds(pl.multiple_of(step * PLAN_TB, PLAN_TB), PLAN_TB)
        sel = jnp.where((info_ref[0:1, cols] == expert_id) | (info_ref[1:2, cols] == expert_id), 1.0, 0.0)
        selb = sel.astype(BF16)
        rank_scr[:, cols] = _dot(selb, before_ref[...]) + count_scr[:, 0:1]
        count_scr[...] += _dot(selb, jnp.ones((PLAN_TB, LANES), BF16))

    @pl.when(step == n_blocks)
    def _():
        counts = count_scr[...]
        padded = jnp.floor((counts + (SLOT_TM - 1)) * (1.0 / SLOT_TM)) * SLOT_TM
        starts, ends, run = [], [], jnp.zeros((1, LANES), F32)
        for e in range(N_EXPERTS):
            starts.append(run)
            run = run + padded[e:e + 1]
            ends.append(run)
        n = pos_ref.shape[1]
        for k in range(2):
            chosen = info_ref[k:k + 1, :]
            slot = jnp.zeros((1, n), F32)
            for e in range(N_EXPERTS):
                slot = jnp.where(chosen == float(e), starts[e][:, 0:1] + rank_scr[e:e + 1, :], slot)
            pos_ref[k:k + 1, :] = slot.astype(jnp.int32)
        pos_ref[2:8, :] = jnp.zeros((6, n), jnp.int32)
        tile_start = lax.broadcasted_iota(jnp.int32, (1, LANES), 1).astype(F32) * SLOT_TM
        used = jnp.where(tile_start < run, 1.0, 0.0)
        tile_start = jnp.minimum(tile_start, run - SLOT_TM)
        owner = jnp.zeros((1, LANES), F32)
        for e in range(N_EXPERTS - 1):
            owner = owner + jnp.where(ends[e] <= tile_start, 1.0, 0.0)
        tile = tile_start * (1.0 / SLOT_TM)
        tmap_ref[...] = jnp.concatenate([owner, used, tile] + [jnp.zeros((1, LANES), F32)] * 5,
                                        axis=0).astype(jnp.int32)


def _plan(info):
    n = info.shape[1]
    n_blocks = n // PLAN_TB
    t_i = lax.broadcasted_iota(jnp.int32, (PLAN_TB, PLAN_TB), 0)
    t_j = lax.broadcasted_iota(jnp.int32, (PLAN_TB, PLAN_TB), 1)
    before = (t_i < t_j).astype(BF16)
    return pl.pallas_call(
        functools.partial(_plan_kernel, n_blocks=n_blocks),
        out_shape=(jax.ShapeDtypeStruct((8, n), jnp.int32), jax.ShapeDtypeStruct((8, LANES), jnp.int32)),
        grid=(n_blocks + 1,),
        in_specs=[_const_spec((8, n)), _const_spec((PLAN_TB, PLAN_TB))],
        out_specs=(pl.BlockSpec((8, n), lambda i: (0, 0)), pl.BlockSpec((8, LANES), lambda i: (0, 0))),
        scratch_shapes=[pltpu.VMEM((N_EXPERTS, n), F32), pltpu.VMEM((N_EXPERTS, LANES), F32)],
        compiler_params=_cparams(("arbitrary",)),
        name="plan",
    )(info, before)


def _sc_mesh():
    return plsc.VectorSubcoreMesh(core_axis_name="core", subcore_axis_name="subcore")


def _sc_scatter_rows(x, idx, n_out):
    n_idx = idx.shape[0]
    src_blocks = x.shape[0] // SC_WIN

    @pl.kernel(out_type=jax.ShapeDtypeStruct((n_out, SC_ROW), x.dtype), mesh=_sc_mesh(), scratch_types=[])
    def scatter_kernel(x_hbm, i_hbm, o_hbm):
        def body(x_vmem, i_vmem):
            pltpu.sync_copy(x_vmem, o_hbm.at[i_vmem.at[0]])

        pltpu.emit_pipeline(
            body,
            grid=(n_idx // SC_WIN,),
            in_specs=[pl.BlockSpec((SC_WIN, SC_ROW), index_map=lambda i: (i % src_blocks, 0)),
                      pl.BlockSpec((1, SC_WIN), index_map=lambda i: (0, i))],
            out_specs=[],
            core_axis_name=("core", "subcore"),
            dimension_semantics=(pltpu.PARALLEL,),
        )(x_hbm, i_hbm)

    return scatter_kernel(x, idx.reshape(1, n_idx))


def _sc_gather_rows(x, idx):
    n_idx = idx.shape[0]

    @pl.kernel(out_type=jax.ShapeDtypeStruct((n_idx, SC_ROW), x.dtype), mesh=_sc_mesh(), scratch_types=[])
    def gather_kernel(x_hbm, i_hbm, o_hbm):
        def body(i_vmem, o_vmem):
            pltpu.sync_copy(x_hbm.at[i_vmem.at[0]], o_vmem)

        pltpu.emit_pipeline(
            body,
            grid=(n_idx // SC_WIN,),
            in_specs=[pl.BlockSpec((1, SC_WIN), index_map=lambda i: (0, i))],
            out_specs=[pl.BlockSpec((SC_WIN, SC_ROW), index_map=lambda i: (i, 0))],
            core_axis_name=("core", "subcore"),
            dimension_semantics=(pltpu.PARALLEL,),
        )(i_hbm, o_hbm)

    return gather_kernel(x, idx.reshape(1, n_idx))


def _experts_kernel(tmap_ref, xs_ref, wg_ref, wu_ref, wd_ref, ys_ref, wg_scr, wu_scr, wd_scr):
    j = pl.program_id(0)

    @pl.when((j == 0) | (tmap_ref[0, j] != tmap_ref[0, jnp.maximum(j - 1, 0)]))
    def _():
        wg_scr[...] = wg_ref[0].astype(BF16)
        wu_scr[...] = wu_ref[0].astype(BF16)
        wd_scr[...] = wd_ref[0].astype(BF16)

    @pl.when(tmap_ref[1, j] > 0)
    def _():
        x = _unpack_bf16_pairs(xs_ref[0:SLOT_TM, :], xs_ref[SLOT_TM:2 * SLOT_TM, :])
        a = _dot(x, wg_scr[...])
        u = _dot(x, wu_scr[...])
        act = (a * jax.nn.sigmoid(a) * u).astype(BF16)
        y = _dot(act, wd_scr[...]).astype(BF16).astype(F32)
        packed = _pack_bf16_pairs(y)
        ys_ref[0:SLOT_TM, :] = packed[:, :SC_ROW]
        ys_ref[SLOT_TM:2 * SLOT_TM, :] = packed[:, SC_ROW:]


def _experts(xs, tmap, wg, wu, wd):
    n_tiles = xs.shape[0] // (2 * SLOT_TM)
    grid_spec = pltpu.PrefetchScalarGridSpec(
        num_scalar_prefetch=1,
        grid=(n_tiles,),
        in_specs=[
            pl.BlockSpec((2 * SLOT_TM, SC_ROW), lambda j, tm: (tm[2, j], 0)),
            pl.BlockSpec((1, D_MODEL, D_EXPERT), lambda j, tm: (tm[0, j], 0, 0)),
            pl.BlockSpec((1, D_MODEL, D_EXPERT), lambda j, tm: (tm[0, j], 0, 0)),
            pl.BlockSpec((1, D_EXPERT, D_MODEL), lambda j, tm: (tm[0, j], 0, 0)),
        ],
        out_specs=pl.BlockSpec((2 * SLOT_TM, SC_ROW), lambda j, tm: (tm[2, j], 0)),
        scratch_shapes=[pltpu.VMEM((D_MODEL, D_EXPERT), BF16), pltpu.VMEM((D_MODEL, D_EXPERT), BF16),
                        pltpu.VMEM((D_EXPERT, D_MODEL), BF16)],
    )
    return pl.pallas_call(
        _experts_kernel,
        out_shape=jax.ShapeDtypeStruct(xs.shape, jnp.int32),
        grid_spec=grid_spec,
        compiler_params=_cparams(("arbitrary",)),
        name="experts",
    )(tmap, xs, wg, wu, wd)


FIN_TM = 512


def _final_kernel(x1_ref, mod_ref, yg_ref, w_ref, gf_ref, y_ref):
    moe = None
    for k in range(2):
        base = 2 * k * FIN_TM
        yk = _unpack_bf16_pairs(yg_ref[base:base + FIN_TM, :], yg_ref[base + FIN_TM:base + 2 * FIN_TM, :])
        term = w_ref[:, k:k + 1] * yk.astype(F32)
        moe = term if moe is None else moe + term
    x2 = x1_ref[...] + mod_ref[0, 5:6, :] * moe
    y_ref[...] = x2 * _rms_scale(x2) * gf_ref[...]


def _final(x1, mod3, mod_row_fn, yg, w_cols, g_final):
    n = x1.shape[0]
    return pl.pallas_call(
        _final_kernel,
        out_shape=jax.ShapeDtypeStruct((n, D_MODEL), F32),
        grid=(n // FIN_TM,),
        in_specs=[
            pl.BlockSpec((FIN_TM, D_MODEL), lambda i: (i, 0)),
            pl.BlockSpec((1, 6, D_MODEL), lambda i: (mod_row_fn(i * FIN_TM), 0, 0)),
            pl.BlockSpec((4 * FIN_TM, SC_ROW), lambda i: (i, 0)),
            pl.BlockSpec((FIN_TM, 2), lambda i: (i, 0)),
            _const_spec((1, D_MODEL)),
        ],
        out_specs=pl.BlockSpec((FIN_TM, D_MODEL), lambda i: (i, 0)),
        compiler_params=_cparams(("parallel",)),
        name="final",
    )(x1, mod3, yg, w_cols, g_final)


def _half_row(pos, half):
    return (pos // SLOT_TM) * (2 * SLOT_TM) + half * SLOT_TM + pos % SLOT_TM


def _moe(x1, h2p, info, mod3, mod_row_fn, wg, wu, wd, g_final):
    n = x1.shape[0]
    pos, tmap = _plan(info)
    n_slots = 2 * n + N_EXPERTS * SLOT_TM
    pos1, pos2 = pos[0], pos[1]

    def route_order(p):
        return jnp.stack([_half_row(p, 0).reshape(n // MIX_TM, MIX_TM),
                          _half_row(p, 1).reshape(n // MIX_TM, MIX_TM)], axis=1).reshape(-1)

    xs = _sc_scatter_rows(h2p, jnp.concatenate([route_order(pos1), route_order(pos2)]), 2 * n_slots)
    ys = _experts(xs, tmap, wg, wu, wd)

    def fin_order(p):
        return jnp.stack([_half_row(p, 0).reshape(n // FIN_TM, FIN_TM),
                          _half_row(p, 1).reshape(n // FIN_TM, FIN_TM)], axis=1)

    gidx = jnp.stack([fin_order(pos1), fin_order(pos2)], axis=1).reshape(-1)
    yg = _sc_gather_rows(ys, gidx)
    w_cols = info[2:4].T
    return _final(x1, mod3, mod_row_fn, yg, w_cols, g_final)


def _stream(x, mod3, mod_row_fn, seq_len, rows_2d, state, w):
    bsz, t, _ = x.shape
    n = bsz * t
    x2d = x.reshape(n, D_MODEL)
    p3, bcum, imb, amax = _in_projection(x2d, mod3, mod_row_fn, w["g_norm1"], w["wmain"], w["bmain"],
                                         w["colscale"], w["wg_hi"], w["wg_lo"], w["bg"])
    m0 = None if state is None else state[2]
    cols, rw, cs = _gate_prep(bcum, imb, amax, m0)
    rows = rw[:, :2 * N_HD].T.reshape(2 * N_HD, 1, n)
    ml = _mlstm(p3, cols, rows, cs, w["gn3"], None if state is None else state[:2])
    amat, invc = _pool_constants(seq_len, rows_2d)
    n_seq = max(1, 1024 // seq_len)
    yb = _pool(p3, amat, invc, w["wpg"], w["pscale"], seq_len, n_seq, rows_2d)
    ya = ml if state is not None else ml[0]
    x1, h2p, info = _mix(x2d, mod3, mod_row_fn, ya, yb, p3, w["wa"], w["wb"], w["wo"],
                         w["g_norm2"], w["wr_hi"], w["wr_lo"], w["br"])
    y = _moe(x1, h2p, info, mod3, mod_row_fn, w["wg"], w["wu"], w["wd"], w["g_final"])
    return y.reshape(bsz, t, D_MODEL), ml, cs


def kernel(x_prompt, x_sample, state_C, state_n, state_m, c, c_ctx, w_mod, b_mod, g_norm1, w_in, b_in, gn_gain, w_pool_grp, pool_scale, w_proj_a, w_proj_b, w_out, g_norm2, w_router_grp, b_router_grp, w_router_exp, b_router_exp, w_exp_gate, w_exp_up, w_exp_down, g_final):
    bp, tp, _ = x_prompt.shape
    bs, ts, _ = x_sample.shape
    assert w_mod.shape[0] == 1, "single trunk layer"
    assert tp == CHUNK and ts % CHUNK == 0 and ts == ML_ROWS

    cvec = jnp.zeros((8, D_MODEL), F32).at[0].set(c_ctx).at[1:1 + bs].set(c)
    mod3 = _modulation(cvec, w_mod[0], b_mod[0])

    wi, bi = w_in[0], b_in[0]
    d = D_MODEL
    o_q, o_k, o_v, o_o, o_g, o_p, o_ga, o_gb = 0, d, 2 * d, 3 * d, 4 * d, 4 * d + 32, 4 * d + 32 + 512, 4 * d + 32 + 512 + d
    order = [(o_q, d), (o_k, d), (o_v, d), (o_o, d), (o_ga, d), (o_gb, d), (o_p, 512)]
    wmain = _pack_columns(wi, order)
    bmain = jnp.concatenate([bi[s:s + l] for s, l in order]).reshape(1, D_MAIN)
    colscale = jnp.ones((D_MAIN,), F32).at[SLAB_K * LANES:(SLAB_K + N_HEADS) * LANES].set(HEAD_DIM ** -0.5)
    colscale = colscale.reshape(1, D_MAIN)
    gwt = _take_columns_t(wi, o_g, 32)
    gb = bi[o_g:o_g + 32]
    i_f, f_f, i_b, f_b = (slice(0, 8), slice(8, 16), slice(16, 24), slice(24, 32))
    pad_w = jnp.zeros((LANES - N_HD, d), F32)
    pad_b = jnp.zeros((LANES - N_HD,), F32)
    wg2 = jnp.concatenate([gwt[f_f], gwt[f_b], pad_w, gwt[i_f], gwt[i_b], pad_w], axis=0).T
    bg2 = jnp.concatenate([gb[f_f], gb[f_b], pad_b, gb[i_f], gb[i_b], pad_b]).reshape(1, 2 * LANES)
    wg_hi, wg_lo = _split_bf16(wg2)

    wr = jnp.concatenate([w_router_grp[0], w_router_exp[0].reshape(d, N_EXPERTS),
                          jnp.zeros((d, LANES - ROUTER_ROWS), F32)], axis=1)
    br = jnp.concatenate([b_router_grp[0], b_router_exp[0].reshape(N_EXPERTS),
                          jnp.zeros((LANES - ROUTER_ROWS,), F32)]).reshape(1, LANES)
    wr_hi, wr_lo = _split_bf16(wr)

    w = dict(
        g_norm1=g_norm1[0].reshape(1, d), wmain=wmain, bmain=bmain, colscale=colscale,
        wg_hi=wg_hi, wg_lo=wg_lo, bg=bg2,
        gn3=gn_gain[0].reshape(N_HEADS, 1, HEAD_DIM),
        wpg=w_pool_grp[0].astype(BF16), pscale=pool_scale[0].reshape(N_GROUPS, 1, LANES),
        wa=w_proj_a[0].astype(BF16), wb=w_proj_b[0].astype(BF16), wo=w_out[0].astype(BF16),
        g_norm2=g_norm2[0].reshape(1, d), wr_hi=wr_hi, wr_lo=wr_lo, br=br,
        wg=w_exp_gate[0], wu=w_exp_up[0], wd=w_exp_down[0],
        g_final=g_final.reshape(1, d),
    )

    y_prompt, (_, cst, nst), cs_prompt = _stream(x_prompt, mod3, lambda r: 0, tp, None, None, w)
    n_rep = jnp.broadcast_to(state_n[..., None], state_C.shape)
    m0 = jnp.pad(state_m[:, 0].reshape(bs, N_HD), ((0, 0), (0, LANES - N_HD))).reshape(bs, 1, LANES)
    y_sample, _, _ = _stream(x_sample, mod3, lambda r: 1 + r // ts, ts, ts // GRID_W, (state_C, n_rep, m0), w)

    new_c = cst
    new_n = nst.reshape(bp, 1, 2, N_HEADS, HEAD_DIM)
    new_m = cs_prompt[:, 1, :N_HD].reshape(bp, 1, 2, N_HEADS)
    return (y_prompt, y_sample, new_c, new_n, new_m)
```

```python
import functools

import numpy as np
import jax
import jax.numpy as jnp
from jax import lax
from jax.experimental import pallas as pl
from jax.experimental.pallas import tpu as pltpu
from jax.experimental.pallas import tpu_sc as plsc

F32 = jnp.float32
BF16 = jnp.bfloat16

D_MODEL = 1024
N_HEADS = 8
HEAD_DIM = 128
GRID_W = 64
POOL_WINDOWS = (2, 4, 8, 16)
N_GROUPS = 4
EXPERTS_PER_GROUP = 4
N_EXPERTS = 16
D_EXPERT = 512
EPS = 1e-6
NEG = -1e30

LANES = 128
CHUNK = 256
N_SLABS = 52
SLAB_Q, SLAB_K, SLAB_V, SLAB_O, SLAB_GA, SLAB_GB, SLAB_POOL = 0, 8, 16, 24, 32, 40, 48
D_MAIN = N_SLABS * LANES
VMEM_LIMIT = 56 * 1024 * 1024


def _cparams(sem):
    return pltpu.CompilerParams(dimension_semantics=sem, vmem_limit_bytes=VMEM_LIMIT)


def _const_spec(shape):
    nd = len(shape)
    return pl.BlockSpec(shape, lambda *_: (0,) * nd, pipeline_mode=pl.Buffered(1))


def _split_bf16(x):
    hi = x.astype(BF16)
    lo = (x - hi.astype(F32)).astype(BF16)
    return hi, lo


def _dot(a, b):
    return jnp.dot(a, b, preferred_element_type=F32)


def _dot3(a, b_hi, b_lo):
    a_hi, a_lo = _split_bf16(a)
    return _dot(a_hi, b_hi) + _dot(a_hi, b_lo) + _dot(a_lo, b_hi)


def _rms_scale(x):
    return lax.rsqrt(jnp.mean(x * x, axis=-1, keepdims=True) + EPS)


def _log_sigmoid(x):
    return jnp.minimum(x, 0.0) - jnp.log1p(jnp.exp(-jnp.abs(x)))


def _mod_kernel(c_ref, w_ref, b_ref, o_ref):
    c = c_ref[...]
    s = c * jax.nn.sigmoid(c)
    w_hi, w_lo = _split_bf16(w_ref[...])
    o_ref[...] = _dot3(s, w_hi, w_lo) + b_ref[...]


def _modulation(cvec, w_mod, b_mod):
    tn = 512
    out = pl.pallas_call(
        _mod_kernel,
        out_shape=jax.ShapeDtypeStruct((8, 6 * D_MODEL), F32),
        grid=(6 * D_MODEL // tn,),
        in_specs=[
            pl.BlockSpec((8, D_MODEL), lambda j: (0, 0)),
            pl.BlockSpec((D_MODEL, tn), lambda j: (0, j)),
            pl.BlockSpec((1, tn), lambda j: (0, j)),
        ],
        out_specs=pl.BlockSpec((8, tn), lambda j: (0, j)),
        compiler_params=_cparams(("parallel",)),
        name="modulation",
    )(cvec, w_mod, b_mod.reshape(1, -1))
    return out.reshape(8, 6, D_MODEL)


PACK_COLS = 512


def _pack_kernel(wt_ref, o_ref):
    o_ref[...] = wt_ref[...].T.astype(BF16)


def _pack_columns(w, order):
    k, n = w.shape
    starts = []
    for src, width in order:
        assert width % PACK_COLS == 0
        starts += [src + j for j in range(0, width, PACK_COLS)]
    n_out = len(starts) * PACK_COLS

    unit = 32
    assert all(s % unit == 0 for s in starts)

    def src_row(j):
        row = jnp.int32(starts[-1] // unit)
        for idx in range(len(starts) - 2, -1, -1):
            row = jnp.where(j == idx, starts[idx] // unit, row)
        return row * unit

    return pl.pallas_call(
        _pack_kernel,
        out_shape=jax.ShapeDtypeStruct((k, n_out), BF16),
        grid=(len(starts),),
        in_specs=[pl.BlockSpec((pl.Element(PACK_COLS), pl.Element(k)), lambda j: (src_row(j), 0))],
        out_specs=pl.BlockSpec((k, PACK_COLS), lambda j: (0, j)),
        compiler_params=_cparams(("parallel",)),
        name="pack_columns",
    )(w.T)


def _copy_kernel(w_ref, o_ref):
    o_ref[...] = w_ref[...]


def _take_columns_t(w, start, width):
    k = w.shape[0]
    return pl.pallas_call(
        _copy_kernel,
        out_shape=jax.ShapeDtypeStruct((width, k), w.dtype),
        grid=(1,),
        in_specs=[pl.BlockSpec((pl.Element(width), pl.Element(k)), lambda j: (start, 0))],
        out_specs=pl.BlockSpec((width, k), lambda j: (0, 0)),
        name="take_columns",
    )(w.T)


IN_TM = 512
IN_TN = 512
LOG2E = 1.4426950408889634
N_HD = 2 * N_HEADS


def _fwd_lanes(shape):
    lane = lax.broadcasted_iota(jnp.int32, shape, len(shape) - 1)
    return (lane & N_HEADS) == 0


def _inproj_kernel(x_ref, mod_ref, g_ref, w_ref, b_ref, cs_ref, wg_hi_ref, wg_lo_ref, bg_ref,
                   p3_ref, bcum_ref, imb_ref, amax_ref):
    x = x_ref[...]
    h = x * _rms_scale(x) * g_ref[...]
    h = h * (1.0 + mod_ref[0, 1:2, :]) + mod_ref[0, 0:1, :]
    hb = h.astype(BF16)
    for n in range(D_MAIN // IN_TN):
        sl = slice(n * IN_TN, (n + 1) * IN_TN)
        acc = (_dot(hb, w_ref[:, sl]) + b_ref[:, sl]) * cs_ref[:, sl]
        accb = acc.astype(BF16)
        for j in range(IN_TN // LANES):
            p3_ref[n * (IN_TN // LANES) + j] = accb[:, j * LANES:(j + 1) * LANES]

    g2 = _dot3(h, wg_hi_ref[...], wg_lo_ref[...]) + bg_ref[...]
    ls = _log_sigmoid(g2[:, :LANES])
    gi = g2[:, LANES:]
    row = lax.broadcasted_iota(jnp.int32, (CHUNK, LANES), 0)
    fwd = _fwd_lanes((CHUNK, LANES))
    for c in range(IN_TM // CHUNK):
        rows = slice(c * CHUNK, (c + 1) * CHUNK)
        cf = ls[rows]
        cb = cf
        s = 1
        while s < CHUNK:
            cf = cf + jnp.where(row >= s, pltpu.roll(cf, s, 0), 0.0)
            cb = cb + jnp.where(row < CHUNK - s, pltpu.roll(cb, CHUNK - s, 0), 0.0)
            s *= 2
        b = jnp.where(fwd, cf, cb)
        imb = gi[rows] - b
        mf = imb
        mb = imb
        s = 1
        while s < CHUNK:
            mf = jnp.maximum(mf, jnp.where(row >= s, pltpu.roll(mf, s, 0), -jnp.inf))
            mb = jnp.maximum(mb, jnp.where(row < CHUNK - s, pltpu.roll(mb, CHUNK - s, 0), -jnp.inf))
            s *= 2
        bcum_ref[rows, :] = b
        imb_ref[rows, :] = imb
        amax_ref[rows, :] = b + jnp.where(fwd, mf, mb)


def _in_projection(x2d, mod3, mod_row_fn, g_norm1, wmain, bmain, colscale, wg_hi, wg_lo, bg):
    n = x2d.shape[0]
    return pl.pallas_call(
        _inproj_kernel,
        out_shape=(jax.ShapeDtypeStruct((N_SLABS, n, LANES), BF16),) + (jax.ShapeDtypeStruct((n, LANES), F32),) * 3,
        grid=(n // IN_TM,),
        in_specs=[
            pl.BlockSpec((IN_TM, D_MODEL), lambda i: (i, 0)),
            pl.BlockSpec((1, 6, D_MODEL), lambda i: (mod_row_fn(i * IN_TM), 0, 0)),
            _const_spec((1, D_MODEL)),
            _const_spec((D_MODEL, D_MAIN)),
            _const_spec((1, D_MAIN)),
            _const_spec((1, D_MAIN)),
            _const_spec((D_MODEL, 2 * LANES)),
            _const_spec((D_MODEL, 2 * LANES)),
            _const_spec((1, 2 * LANES)),
        ],
        out_specs=(pl.BlockSpec((N_SLABS, IN_TM, LANES), lambda i: (0, i, 0)),)
        + (pl.BlockSpec((IN_TM, LANES), lambda i: (i, 0)),) * 3,
        compiler_params=_cparams(("parallel",)),
        name="in_projection",
    )(x2d, mod3, g_norm1, wmain, bmain, colscale, wg_hi, wg_lo, bg)


GP_ROWS = 1024
GP_NCH = GP_ROWS // CHUNK
Q_C2, Q_EM, Q_WI, Q_WC = 0, 1, 2, 3


def _gate_prep_kernel(*refs, carry):
    if carry:
        b_ref, imb_ref, a_ref, m0_ref, cols_ref, rw_ref, cs_ref = refs
    else:
        b_ref, imb_ref, a_ref, cols_ref, rw_ref, cs_ref = refs
    fwd1 = _fwd_lanes((1, LANES))
    lane = lax.broadcasted_iota(jnp.int32, (CHUNK, LANES), 1)
    neg = jnp.full((1, LANES), NEG, F32)

    def at_scan_end(ref, c):
        return jnp.where(fwd1, ref[(c + 1) * CHUNK - 1:(c + 1) * CHUNK, :], ref[c * CHUNK:c * CHUNK + 1, :])

    b_last = [at_scan_end(b_ref, c) for c in range(GP_NCH)]
    a_last = [at_scan_end(a_ref, c) for c in range(GP_NCH)]
    m_start = m0_ref[0] if carry else neg
    m_prev = [[None] * GP_NCH, [None] * GP_NCH]
    m_new = [[None] * GP_NCH, [None] * GP_NCH]
    for d, order in enumerate((range(GP_NCH), range(GP_NCH - 1, -1, -1))):
        m = m_start
        for c in order:
            m_prev[d][c] = m
            m_new[d][c] = jnp.maximum(b_last[c] + m, a_last[c])
            m = m_new[d][c] if carry else neg

    row_src = []
    for c in range(GP_NCH):
        rows = slice(c * CHUNK, (c + 1) * CHUNK)
        mp = jnp.where(fwd1, m_prev[0][c], m_prev[1][c])
        mn = jnp.where(fwd1, m_new[0][c], m_new[1][c])
        b = b_ref[rows, :]
        imb = imb_ref[rows, :]
        inter = b + mp
        m_t = jnp.maximum(inter, a_ref[rows, :])
        c2 = (b - m_t) * LOG2E
        em = jnp.exp(-m_t)
        wc = jnp.exp(b_last[c] - mn + imb)
        packed = jnp.where(lane < N_HD, c2, pltpu.roll(em, N_HD * Q_EM, 1))
        if carry:
            wi = jnp.exp(inter - m_t)
            packed = jnp.where(lane < N_HD * Q_WI, packed, pltpu.roll(wi, N_HD * Q_WI, 1))
        cols_ref[rows, :] = jnp.where(lane < N_HD * Q_WC, packed, pltpu.roll(wc, N_HD * Q_WC, 1))
        row_src.append(jnp.where(lane < N_HD, imb * (-LOG2E), pltpu.roll(wc, N_HD, 1)))
        cs_ref[c, 0:1, :] = jnp.exp(b_last[c] + mp - mn)
        cs_ref[c, 1:2, :] = mn

    rows_t = jnp.concatenate(row_src, axis=0).T
    for r in range(2 * N_HD):
        rw_ref[r] = rows_t[r:r + 1]


def _gate_prep(bcum, imb, amax, m0=None):
    n = bcum.shape[0]
    carry = m0 is not None
    blk = pl.BlockSpec((GP_ROWS, LANES), lambda i: (i, 0))
    in_specs = [blk, blk, blk]
    args = [bcum, imb, amax]
    if carry:
        in_specs.append(pl.BlockSpec((1, 1, LANES), lambda i: (i, 0, 0)))
        args.append(m0)
    return pl.pallas_call(
        functools.partial(_gate_prep_kernel, carry=carry),
        out_shape=(jax.ShapeDtypeStruct((n, LANES), F32), jax.ShapeDtypeStruct((2 * N_HD, 1, n), F32),
                   jax.ShapeDtypeStruct((n // CHUNK, 2, LANES), F32)),
        grid=(n // GP_ROWS,),
        in_specs=in_specs,
        out_specs=(blk, pl.BlockSpec((2 * N_HD, 1, GP_ROWS), lambda i: (0, 0, i)),
                   pl.BlockSpec((GP_NCH, 2, LANES), lambda i: (i, 0, 0))),
        compiler_params=_cparams(("parallel",)),
        name="gate_prep",
    )(*args)


ML_ROWS = 1024
ML_ROWS_FREE = 2048


def _pick_col(blk, lane_iota, idx):
    return jnp.sum(jnp.where(lane_iota == idx, blk, 0.0), axis=1, keepdims=True)


def _mlstm_kernel(*refs, carry, n_chunks):
    if carry:
        (q_ref, k_ref, v_ref, o_ref, cols_ref, r2f_ref, r2b_ref, gn_ref, cs_ref, c0_ref, n0_ref,
         ya_ref, s_scr, h_scr) = refs
    else:
        (q_ref, k_ref, v_ref, o_ref, cols_ref, r2f_ref, r2b_ref, gn_ref, wrf_ref, wrb_ref,
         ya_ref, cst_ref, nst_ref, s_scr, h_scr) = refs
    head = pl.program_id(1)
    lane = lax.broadcasted_iota(jnp.int32, (CHUNK, LANES), 1)
    lane1 = lax.broadcasted_iota(jnp.int32, (1, LANES), 1)
    r_idx = lax.broadcasted_iota(jnp.int32, (CHUNK, CHUNK), 0)
    c_idx = lax.broadcasted_iota(jnp.int32, (CHUNK, CHUNK), 1)
    ones = jnp.ones((CHUNK, HEAD_DIM), BF16)
    tn_dims = (((0,), (0,)), ((), ()))

    for d in range(2):
        mask = (c_idx <= r_idx) if d == 0 else (c_idx >= r_idx)
        r2_ref = r2f_ref if d == 0 else r2b_ref
        base = N_HEADS * d + head
        if carry:
            cn = jnp.concatenate([c0_ref[0, 0, d, 0], n0_ref[0, 0, d, 0]], axis=1)
        order = range(n_chunks) if d == 0 else range(n_chunks - 1, -1, -1)
        for step, c in enumerate(order):
            rows = slice(c * CHUNK, (c + 1) * CHUNK)
            q = q_ref[0, rows, :]
            k = k_ref[0, rows, :]
            v = v_ref[0, rows, :]
            v1 = jnp.concatenate([v, ones], axis=1)
            if d == 0:
                s = lax.dot_general(q, k, (((1,), (1,)), ((), ())), preferred_element_type=F32)
                s_scr[rows, :] = s
            else:
                s = s_scr[rows, :]
            blk = cols_ref[rows, :]
            arg = _pick_col(blk, lane, base + N_HD * Q_C2) - r2_ref[0, :, rows]
            sw = s * jnp.exp2(jnp.where(mask, arg, NEG))
            nd = _dot(sw.astype(BF16), v1)
            if carry:
                nd = nd + _pick_col(blk, lane, base + N_HD * Q_WI) * _dot(q, cn.astype(BF16))
            hdir = nd[:, :HEAD_DIM] / jnp.maximum(jnp.abs(nd[:, HEAD_DIM:]), _pick_col(blk, lane, base + N_HD * Q_EM))
            if d == 0:
                h_scr[rows, :] = hdir
            else:
                h_scr[rows, :] = h_scr[rows, :] + hdir

            if carry and step == n_chunks - 1:
                continue
            w_col = _pick_col(blk, lane, base + N_HD * Q_WC)
            if carry:
                u = lax.dot_general(k, (w_col * v1.astype(F32)).astype(BF16), tn_dims,
                                    preferred_element_type=F32)
                decay = jnp.sum(jnp.where(lane1 == base, cs_ref[c, 0:1, :], 0.0), axis=1, keepdims=True)
                cn = decay * cn + u
            else:
                w_row = (wrf_ref if d == 0 else wrb_ref)[0, :, rows]
                w8 = jnp.broadcast_to(w_row, (8, CHUNK)).astype(BF16)
                cst_ref[c, 0, d, 0] = lax.dot_general(k, (w_col * v.astype(F32)).astype(BF16), tn_dims,
                                                      preferred_element_type=F32)
                nst_ref[c, d, 0] = _dot(w8, k)[0:1, :]

    hh = h_scr[...]
    hn = hh * _rms_scale(hh) * gn_ref[0]
    ya_ref[0] = (hn * jax.nn.sigmoid(o_ref[0].astype(F32))).astype(BF16)


def _mlstm(p3, cols, rows, cs, gn3, state=None):
    n = p3.shape[1]
    carry = state is not None
    rows_step = ML_ROWS if carry else ML_ROWS_FREE
    n_chunks = rows_step // CHUNK
    nblk = n // rows_step

    def slab(base):
        return pl.BlockSpec((1, rows_step, LANES), lambda g, h: (base + h, g, 0))

    def row(base):
        return pl.BlockSpec((1, 1, rows_step), lambda g, h: (base + h, 0, g))

    in_specs = [
        slab(SLAB_Q), slab(SLAB_K), slab(SLAB_V), slab(SLAB_O),
        pl.BlockSpec((rows_step, LANES), lambda g, h: (g, 0)),
        row(0), row(N_HEADS),
        pl.BlockSpec((1, 1, LANES), lambda g, h: (h, 0, 0)),
    ]
    args = [p3, p3, p3, p3, cols, rows, rows, gn3]
    ya_shape = jax.ShapeDtypeStruct((N_HEADS, n, LANES), BF16)
    ya_spec = pl.BlockSpec((1, rows_step, LANES), lambda g, h: (h, g, 0))
    state_spec = pl.BlockSpec((1, 1, 2, 1, HEAD_DIM, HEAD_DIM), lambda g, h: (g, 0, 0, h, 0, 0))
    if carry:
        c0, n0 = state
        in_specs += [pl.BlockSpec((n_chunks, 2, LANES), lambda g, h: (g, 0, 0)), state_spec, state_spec]
        args += [cs, c0, n0]
        out_shape = ya_shape
        out_specs = ya_spec
    else:
        in_specs += [row(N_HD), row(N_HD + N_HEADS)]
        args += [rows, rows]
        nseq = n // CHUNK
        out_shape = (ya_shape,
                     jax.ShapeDtypeStruct((nseq, 1, 2, N_HEADS, HEAD_DIM, HEAD_DIM), F32),
                     jax.ShapeDtypeStruct((nseq, 2, N_HEADS, 1, HEAD_DIM), F32))
        out_specs = (ya_spec,
                     pl.BlockSpec((n_chunks, 1, 2, 1, HEAD_DIM, HEAD_DIM), lambda g, h: (g, 0, 0, h, 0, 0)),
                     pl.BlockSpec((n_chunks, 2, 1, 1, HEAD_DIM), lambda g, h: (g, 0, h, 0, 0)))
    return pl.pallas_call(
        functools.partial(_mlstm_kernel, carry=carry, n_chunks=n_chunks),
        out_shape=out_shape,
        grid=(nblk, N_HEADS),
        in_specs=in_specs,
        out_specs=out_specs,
        scratch_shapes=[pltpu.VMEM((rows_step, CHUNK), F32), pltpu.VMEM((rows_step, HEAD_DIM), F32)],
        compiler_params=_cparams(("parallel", "parallel")),
        name="mlstm_latent" if carry else "mlstm_prompt",
    )(*args)


def _pool_kernel(u_ref, a_ref, ic_ref, wpg_ref, ps_ref, yb_ref, *, seq_len, n_seq, rows_2d):
    piece = seq_len if rows_2d is None else GRID_W
    n_piece = n_seq if rows_2d is None else rows_2d
    for g in range(N_GROUPS):
        u = jnp.concatenate([u_ref[g, i * piece:(i + 1) * piece, :] for i in range(n_piece)], axis=1)
        win = _dot(a_ref[g], u)
        if rows_2d is None:
            inv_cnt = jnp.concatenate([ic_ref[g]] * n_piece, axis=1)
        else:
            run = [jnp.zeros((piece, LANES), F32)]
            for r in range(rows_2d):
                run.append(run[-1] + win[:, r * LANES:(r + 1) * LANES])
            half = POOL_WINDOWS[g] // 2
            win = jnp.concatenate([run[min(r - half + POOL_WINDOWS[g], rows_2d)] - run[max(r - half, 0)]
                                   for r in range(rows_2d)], axis=1)
            inv_cnt = jnp.concatenate([ic_ref[g, r * piece:(r + 1) * piece, :] for r in range(rows_2d)], axis=1)
        p = (win * inv_cnt - u.astype(F32)).astype(BF16)
        p = jnp.concatenate([p[:, i * LANES:(i + 1) * LANES] for i in range(n_piece)], axis=0)
        yb_ref[g] = (_dot(p, wpg_ref[g]) * ps_ref[g]).astype(BF16)


def _pool(p3, amat, invc, wpg, pscale, seq_len, n_seq, rows_2d):
    n = p3.shape[1]
    rows = seq_len * n_seq
    return pl.pallas_call(
        functools.partial(_pool_kernel, seq_len=seq_len, n_seq=n_seq, rows_2d=rows_2d),
        out_shape=jax.ShapeDtypeStruct((N_GROUPS, n, LANES), BF16),
        grid=(n // rows,),
        in_specs=[
            pl.BlockSpec((N_GROUPS, rows, LANES), lambda i: (SLAB_POOL // N_GROUPS, i, 0)),
            _const_spec(amat.shape),
            _const_spec((N_GROUPS, seq_len, LANES)),
            _const_spec((N_GROUPS, LANES, LANES)),
            _const_spec((N_GROUPS, 1, LANES)),
        ],
        out_specs=pl.BlockSpec((N_GROUPS, rows, LANES), lambda i: (0, i, 0)),
        compiler_params=_cparams(("parallel",)),
        name="pool",
    )(p3, amat, invc, wpg, pscale)


def _window_matrix(size, w):
    idx = np.arange(size)
    lo = np.clip(idx - w // 2, 0, size)
    hi = np.clip(idx - w // 2 + w, 0, size)
    s = np.arange(size)[None, :]
    a = ((s >= lo[:, None]) & (s < hi[:, None])).astype(np.float32)
    return a, (hi - lo).astype(np.float32)


def _pool_constants(seq_len, rows_2d):
    mats, invs = [], []
    for w in POOL_WINDOWS:
        if rows_2d is None:
            a, cnt = _window_matrix(seq_len, w)
        else:
            _, cr = _window_matrix(rows_2d, w)
            a, cc = _window_matrix(GRID_W, w)
            cnt = np.kron(cr, cc)
        mats.append(a)
        invs.append(np.repeat((1.0 / cnt)[:, None], LANES, axis=1))
    return (jnp.asarray(np.stack(mats), dtype=BF16), jnp.asarray(np.stack(invs), dtype=F32))


MIX_TM = 512


def _mix_kernel(x_ref, mod_ref, ya_ref, yb_ref, ga_ref, gb_ref, wa_ref, wb_ref, wo_ref,
                g2_ref, wr_hi_ref, wr_lo_ref, br_ref, x1_ref, h2_ref, info_ref):
    ya = jnp.concatenate([ya_ref[i] for i in range(N_HEADS)], axis=1)
    yb = jnp.concatenate([yb_ref[i] for i in range(N_GROUPS)], axis=1)
    ga = jnp.concatenate([ga_ref[i] for i in range(8)], axis=1).astype(F32)
    gb = jnp.concatenate([gb_ref[i] for i in range(8)], axis=1).astype(F32)
    merged = jax.nn.sigmoid(ga) * _dot(ya, wa_ref[...]) + jax.nn.sigmoid(gb) * _dot(yb, wb_ref[...])
    out = _dot(merged.astype(BF16), wo_ref[...])
    x1 = x_ref[...] + mod_ref[0, 2:3, :] * out
    x1_ref[...] = x1

    h2 = x1 * _rms_scale(x1) * g2_ref[...]
    h2 = h2 * (1.0 + mod_ref[0, 4:5, :]) + mod_ref[0, 3:4, :]
    h_hi, h_lo = _split_bf16(h2)
    packed = _pack_bf16_pairs(h_hi.astype(F32))
    h2_ref[0:MIX_TM, :] = packed[:, :SC_ROW]
    h2_ref[MIX_TM:2 * MIX_TM, :] = packed[:, SC_ROW:]
    wr_hi = wr_hi_ref[...]
    logits = _dot(h_hi, wr_hi) + _dot(h_lo, wr_hi) + _dot(h_hi, wr_lo_ref[...]) + br_ref[...]
    first, second, w1, w2 = _route_rows(logits.T)
    zero = jnp.zeros_like(w1)
    info_ref[...] = jnp.concatenate([first, second, w1, w2, zero, zero, zero, zero], axis=0)


def _mix(x2d, mod3, mod_row_fn, ya, yb, p3, wa, wb, wo, g_norm2, wr_hi, wr_lo, br):
    n = x2d.shape[0]
    return pl.pallas_call(
        _mix_kernel,
        out_shape=(jax.ShapeDtypeStruct((n, D_MODEL), F32),
                   jax.ShapeDtypeStruct((2 * n, SC_ROW), jnp.int32), jax.ShapeDtypeStruct((8, n), F32)),
        grid=(n // MIX_TM,),
        in_specs=[
            pl.BlockSpec((MIX_TM, D_MODEL), lambda i: (i, 0)),
            pl.BlockSpec((1, 6, D_MODEL), lambda i: (mod_row_fn(i * MIX_TM), 0, 0)),
            pl.BlockSpec((N_HEADS, MIX_TM, LANES), lambda i: (0, i, 0)),
            pl.BlockSpec((N_GROUPS, MIX_TM, LANES), lambda i: (0, i, 0)),
            pl.BlockSpec((8, MIX_TM, LANES), lambda i: (SLAB_GA // 8, i, 0)),
            pl.BlockSpec((8, MIX_TM, LANES), lambda i: (SLAB_GB // 8, i, 0)),
            _const_spec((D_MODEL, D_MODEL)),
            _const_spec((N_GROUPS * LANES, D_MODEL)),
            _const_spec((D_MODEL, D_MODEL)),
            _const_spec((1, D_MODEL)),
            _const_spec((D_MODEL, LANES)),
            _const_spec((D_MODEL, LANES)),
            _const_spec((1, LANES)),
        ],
        out_specs=(pl.BlockSpec((MIX_TM, D_MODEL), lambda i: (i, 0)),
                   pl.BlockSpec((2 * MIX_TM, SC_ROW), lambda i: (i, 0)),
                   pl.BlockSpec((8, MIX_TM), lambda i: (0, i))),
        compiler_params=_cparams(("parallel",)),
        name="mix",
    )(x2d, mod3, ya, yb, p3, p3, wa, wb, wo, g_norm2, wr_hi, wr_lo, br)


PLAN_TB = 1024
ROUTER_ROWS = 4 + N_EXPERTS
SLOT_TM = 512
SC_ROW = 256
SC_WIN = 128
HI_MASK = -65536


def _pack_bf16_pairs(x):
    bits = pltpu.bitcast(x, jnp.int32)
    half = D_MODEL // 2
    return jnp.bitwise_or(jnp.bitwise_and(lax.shift_right_logical(bits[:, :half], 16), 0xFFFF),
                          jnp.bitwise_and(bits[:, half:], HI_MASK))


def _unpack_bf16_pairs(lo_words, hi_words):
    def low(w):
        return pltpu.bitcast(lax.shift_left(w, 16), F32)

    def high(w):
        return pltpu.bitcast(jnp.bitwise_and(w, HI_MASK), F32)

    return jnp.concatenate([low(lo_words), low(hi_words), high(lo_words), high(hi_words)], axis=1).astype(BF16)


def _route_rows(lt):
    lg = [lt[i:i + 1] for i in range(N_GROUPS)]
    mx = jnp.maximum(jnp.maximum(lg[0], lg[1]), jnp.maximum(lg[2], lg[3]))
    p_top = 1.0 / (jnp.exp(lg[0] - mx) + jnp.exp(lg[1] - mx) + jnp.exp(lg[2] - mx) + jnp.exp(lg[3] - mx))
    gidx = jnp.zeros(lg[0].shape, jnp.int32)
    best = lg[0]
    for i in range(1, N_GROUPS):
        upd = lg[i] > best
        gidx = jnp.where(upd, i, gidx)
        best = jnp.where(upd, lg[i], best)

    def expert_row(g, e):
        r = N_GROUPS + g * EXPERTS_PER_GROUP + e
        return lt[r:r + 1]

    le = [jnp.where(gidx == 0, expert_row(0, e),
                    jnp.where(gidx == 1, expert_row(1, e),
                              jnp.where(gidx == 2, expert_row(2, e), expert_row(3, e))))
          for e in range(EXPERTS_PER_GROUP)]
    v1 = jnp.maximum(jnp.maximum(le[0], le[1]), jnp.maximum(le[2], le[3]))
    i1 = jnp.full(v1.shape, EXPERTS_PER_GROUP - 1, jnp.int32)
    for e in range(EXPERTS_PER_GROUP - 2, -1, -1):
        i1 = jnp.where(le[e] == v1, e, i1)
    le2 = [jnp.where(i1 == e, -jnp.inf, le[e]) for e in range(EXPERTS_PER_GROUP)]
    v2 = jnp.maximum(jnp.maximum(le2[0], le2[1]), jnp.maximum(le2[2], le2[3]))
    i2 = jnp.full(v2.shape, EXPERTS_PER_GROUP - 1, jnp.int32)
    for e in range(EXPERTS_PER_GROUP - 2, -1, -1):
        i2 = jnp.where(le2[e] == v2, e, i2)
    e2 = jnp.exp(v2 - v1)
    first = (gidx * EXPERTS_PER_GROUP + i1).astype(F32)
    second = (gidx * EXPERTS_PER_GROUP + i2).astype(F32)
    return first, second, p_top / (1.0 + e2), p_top * e2 / (1.0 + e2)


def _plan_kernel(info_ref, before_ref, pos_ref, tmap_ref, rank_scr, count_scr, *, n_blocks):
    step = pl.program_id(0)
    expert_id = lax.broadcasted_iota(jnp.int32, (N_EXPERTS, PLAN_TB), 0).astype(F32)

    @pl.when(step == 0)
    def _():
        count_scr[...] = jnp.zeros_like(count_scr)

    @pl.when(step < n_blocks)
    def _():
        cols = pl.ds(pl.multiple_of(step * PLAN_TB, PLAN_TB), PLAN_TB)
        sel = jnp.where((info_ref[0:1, cols] == expert_id) | (info_ref[1:2, cols] == expert_id), 1.0, 0.0)
        selb = sel.astype(BF16)
        rank_scr[:, cols] = _dot(selb, before_ref[...]) + count_scr[:, 0:1]
        count_scr[...] += _dot(selb, jnp.ones((PLAN_TB, LANES), BF16))

    @pl.when(step == n_blocks)
    def _():
        counts = count_scr[...]
        padded = jnp.floor((counts + (SLOT_TM - 1)) * (1.0 / SLOT_TM)) * SLOT_TM
        starts, ends, run = [], [], jnp.zeros((1, LANES), F32)
        for e in range(N_EXPERTS):
            starts.append(run)
            run = run + padded[e:e + 1]
            ends.append(run)
        n = pos_ref.shape[1]
        for k in range(2):
            chosen = info_ref[k:k + 1, :]
            slot = jnp.zeros((1, n), F32)
            for e in range(N_EXPERTS):
                slot = jnp.where(chosen == float(e), starts[e][:, 0:1] + rank_scr[e:e + 1, :], slot)
            pos_ref[k:k + 1, :] = slot.astype(jnp.int32)
        pos_ref[2:8, :] = jnp.zeros((6, n), jnp.int32)
        tile_start = lax.broadcasted_iota(jnp.int32, (1, LANES), 1).astype(F32) * SLOT_TM
        used = jnp.where(tile_start < run, 1.0, 0.0)
        tile_start = jnp.minimum(tile_start, run - SLOT_TM)
        owner = jnp.zeros((1, LANES), F32)
        for e in range(N_EXPERTS - 1):
            owner = owner + jnp.where(ends[e] <= tile_start, 1.0, 0.0)
        tile = tile_start * (1.0 / SLOT_TM)
        tmap_ref[...] = jnp.concatenate([owner, used, tile] + [jnp.zeros((1, LANES), F32)] * 5,
                                        axis=0).astype(jnp.int32)


def _plan(info):
    n = info.shape[1]
    n_blocks = n // PLAN_TB
    t_i = lax.broadcasted_iota(jnp.int32, (PLAN_TB, PLAN_TB), 0)
    t_j = lax.broadcasted_iota(jnp.int32, (PLAN_TB, PLAN_TB), 1)
    before = (t_i < t_j).astype(BF16)
    return pl.pallas_call(
        functools.partial(_plan_kernel, n_blocks=n_blocks),
        out_shape=(jax.ShapeDtypeStruct((8, n), jnp.int32), jax.ShapeDtypeStruct((8, LANES), jnp.int32)),
        grid=(n_blocks + 1,),
        in_specs=[_const_spec((8, n)), _const_spec((PLAN_TB, PLAN_TB))],
        out_specs=(pl.BlockSpec((8, n), lambda i: (0, 0)), pl.BlockSpec((8, LANES), lambda i: (0, 0))),
        scratch_shapes=[pltpu.VMEM((N_EXPERTS, n), F32), pltpu.VMEM((N_EXPERTS, LANES), F32)],
        compiler_params=_cparams(("arbitrary",)),
        name="plan",
    )(info, before)


def _sc_mesh():
    return plsc.VectorSubcoreMesh(core_axis_name="core", subcore_axis_name="subcore")


def _sc_scatter_rows(x, idx, n_out):
    n_idx = idx.shape[0]
    src_blocks = x.shape[0] // SC_WIN

    @pl.kernel(out_type=jax.ShapeDtypeStruct((n_out, SC_ROW), x.dtype), mesh=_sc_mesh(), scratch_types=[])
    def scatter_kernel(x_hbm, i_hbm, o_hbm):
        def body(x_vmem, i_vmem):
            pltpu.sync_copy(x_vmem, o_hbm.at[i_vmem.at[0]])

        pltpu.emit_pipeline(
            body,
            grid=(n_idx // SC_WIN,),
            in_specs=[pl.BlockSpec((SC_WIN, SC_ROW), index_map=lambda i: (i % src_blocks, 0)),
                      pl.BlockSpec((1, SC_WIN), index_map=lambda i: (0, i))],
            out_specs=[],
            core_axis_name=("core", "subcore"),
            dimension_semantics=(pltpu.PARALLEL,),
        )(x_hbm, i_hbm)

    return scatter_kernel(x, idx.reshape(1, n_idx))


def _sc_gather_rows(x, idx):
    n_idx = idx.shape[0]

    @pl.kernel(out_type=jax.ShapeDtypeStruct((n_idx, SC_ROW), x.dtype), mesh=_sc_mesh(), scratch_types=[])
    def gather_kernel(x_hbm, i_hbm, o_hbm):
        def body(i_vmem, o_vmem):
            pltpu.sync_copy(x_hbm.at[i_vmem.at[0]], o_vmem)

        pltpu.emit_pipeline(
            body,
            grid=(n_idx // SC_WIN,),
            in_specs=[pl.BlockSpec((1, SC_WIN), index_map=lambda i: (0, i))],
            out_specs=[pl.BlockSpec((SC_WIN, SC_ROW), index_map=lambda i: (i, 0))],
            core_axis_name=("core", "subcore"),
            dimension_semantics=(pltpu.PARALLEL,),
        )(i_hbm, o_hbm)

    return gather_kernel(x, idx.reshape(1, n_idx))


def _experts_kernel(tmap_ref, xs_ref, wg_ref, wu_ref, wd_ref, ys_ref, wg_scr, wu_scr, wd_scr):
    j = pl.program_id(0)

    @pl.when((j == 0) | (tmap_ref[0, j] != tmap_ref[0, jnp.maximum(j - 1, 0)]))
    def _():
        wg_scr[...] = wg_ref[0].astype(BF16)
        wu_scr[...] = wu_ref[0].astype(BF16)
        wd_scr[...] = wd_ref[0].astype(BF16)

    @pl.when(tmap_ref[1, j] > 0)
    def _():
        x = _unpack_bf16_pairs(xs_ref[0:SLOT_TM, :], xs_ref[SLOT_TM:2 * SLOT_TM, :])
        a = _dot(x, wg_scr[...])
        u = _dot(x, wu_scr[...])
        act = (a * jax.nn.sigmoid(a) * u).astype(BF16)
        y = _dot(act, wd_scr[...]).astype(BF16).astype(F32)
        packed = _pack_bf16_pairs(y)
        ys_ref[0:SLOT_TM, :] = packed[:, :SC_ROW]
        ys_ref[SLOT_TM:2 * SLOT_TM, :] = packed[:, SC_ROW:]


def _experts(xs, tmap, wg, wu, wd):
    n_tiles = xs.shape[0] // (2 * SLOT_TM)
    grid_spec = pltpu.PrefetchScalarGridSpec(
        num_scalar_prefetch=1,
        grid=(n_tiles,),
        in_specs=[
            pl.BlockSpec((2 * SLOT_TM, SC_ROW), lambda j, tm: (tm[2, j], 0)),
            pl.BlockSpec((1, D_MODEL, D_EXPERT), lambda j, tm: (tm[0, j], 0, 0)),
            pl.BlockSpec((1, D_MODEL, D_EXPERT), lambda j, tm: (tm[0, j], 0, 0)),
            pl.BlockSpec((1, D_EXPERT, D_MODEL), lambda j, tm: (tm[0, j], 0, 0)),
        ],
        out_specs=pl.BlockSpec((2 * SLOT_TM, SC_ROW), lambda j, tm: (tm[2, j], 0)),
        scratch_shapes=[pltpu.VMEM((D_MODEL, D_EXPERT), BF16), pltpu.VMEM((D_MODEL, D_EXPERT), BF16),
                        pltpu.VMEM((D_EXPERT, D_MODEL), BF16)],
    )
    return pl.pallas_call(
        _experts_kernel,
        out_shape=jax.ShapeDtypeStruct(xs.shape, jnp.int32),
        grid_spec=grid_spec,
        compiler_params=_cparams(("arbitrary",)),
        name="experts",
    )(tmap, xs, wg, wu, wd)


FIN_TM = 512


def _final_kernel(x1_ref, mod_ref, yg_ref, w_ref, gf_ref, y_ref):
    moe = None
    for k in range(2):
        base = 2 * k * FIN_TM
        yk = _unpack_bf16_pairs(yg_ref[base:base + FIN_TM, :], yg_ref[base + FIN_TM:base + 2 * FIN_TM, :])
        term = w_ref[:, k:k + 1] * yk.astype(F32)
        moe = term if moe is None else moe + term
    x2 = x1_ref[...] + mod_ref[0, 5:6, :] * moe
    y_ref[...] = x2 * _rms_scale(x2) * gf_ref[...]


def _final(x1, mod3, mod_row_fn, yg, w_cols, g_final):
    n = x1.shape[0]
    return pl.pallas_call(
        _final_kernel,
        out_shape=jax.ShapeDtypeStruct((n, D_MODEL), F32),
        grid=(n // FIN_TM,),
        in_specs=[
            pl.BlockSpec((FIN_TM, D_MODEL), lambda i: (i, 0)),
            pl.BlockSpec((1, 6, D_MODEL), lambda i: (mod_row_fn(i * FIN_TM), 0, 0)),
            pl.BlockSpec((4 * FIN_TM, SC_ROW), lambda i: (i, 0)),
            pl.BlockSpec((FIN_TM, 2), lambda i: (i, 0)),
            _const_spec((1, D_MODEL)),
        ],
        out_specs=pl.BlockSpec((FIN_TM, D_MODEL), lambda i: (i, 0)),
        compiler_params=_cparams(("parallel",)),
        name="final",
    )(x1, mod3, yg, w_cols, g_final)


def _half_row(pos, half):
    return (pos // SLOT_TM) * (2 * SLOT_TM) + half * SLOT_TM + pos % SLOT_TM


def _moe(x1, h2p, info, mod3, mod_row_fn, wg, wu, wd, g_final):
    n = x1.shape[0]
    pos, tmap = _plan(info)
    n_slots = 2 * n + N_EXPERTS * SLOT_TM
    pos1, pos2 = pos[0], pos[1]

    def route_order(p):
        return jnp.stack([_half_row(p, 0).reshape(n // MIX_TM, MIX_TM),
                          _half_row(p, 1).reshape(n // MIX_TM, MIX_TM)], axis=1).reshape(-1)

    xs = _sc_scatter_rows(h2p, jnp.concatenate([route_order(pos1), route_order(pos2)]), 2 * n_slots)
    ys = _experts(xs, tmap, wg, wu, wd)

    def fin_order(p):
        return jnp.stack([_half_row(p, 0).reshape(n // FIN_TM, FIN_TM),
                          _half_row(p, 1).reshape(n // FIN_TM, FIN_TM)], axis=1)

    gidx = jnp.stack([fin_order(pos1), fin_order(pos2)], axis=1).reshape(-1)
    yg = _sc_gather_rows(ys, gidx)
    w_cols = info[2:4].T
    return _final(x1, mod3, mod_row_fn, yg, w_cols, g_final)


def _stream(x, mod3, mod_row_fn, seq_len, rows_2d, state, w):
    bsz, t, _ = x.shape
    n = bsz * t
    x2d = x.reshape(n, D_MODEL)
    p3, bcum, imb, amax = _in_projection(x2d, mod3, mod_row_fn, w["g_norm1"], w["wmain"], w["bmain"],
                                         w["colscale"], w["wg_hi"], w["wg_lo"], w["bg"])
    m0 = None if state is None else state[2]
    cols, rows, cs = _gate_prep(bcum, imb, amax, m0)
    ml = _mlstm(p3, cols, rows, cs, w["gn3"], None if state is None else state[:2])
    amat, invc = _pool_constants(seq_len, rows_2d)
    n_seq = max(1, 1024 // seq_len)
    yb = _pool(p3, amat, invc, w["wpg"], w["pscale"], seq_len, n_seq, rows_2d)
    ya = ml if state is not None else ml[0]
    x1, h2p, info = _mix(x2d, mod3, mod_row_fn, ya, yb, p3, w["wa"], w["wb"], w["wo"],
                         w["g_norm2"], w["wr_hi"], w["wr_lo"], w["br"])
    y = _moe(x1, h2p, info, mod3, mod_row_fn, w["wg"], w["wu"], w["wd"], w["g_final"])
    return y.reshape(bsz, t, D_MODEL), ml, cs


def kernel(x_prompt, x_sample, state_C, state_n, state_m, c, c_ctx, w_mod, b_mod, g_norm1, w_in, b_in, gn_gain, w_pool_grp, pool_scale, w_proj_a, w_proj_b, w_out, g_norm2, w_router_grp, b_router_grp, w_router_exp, b_router_exp, w_exp_gate, w_exp_up, w_exp_down, g_final):
    bp, tp, _ = x_prompt.shape
    bs, ts, _ = x_sample.shape
    assert w_mod.shape[0] == 1, "single trunk layer"
    assert tp == CHUNK and ts % CHUNK == 0 and ts == ML_ROWS

    cvec = jnp.zeros((8, D_MODEL), F32).at[0].set(c_ctx).at[1:1 + bs].set(c)
    mod3 = _modulation(cvec, w_mod[0], b_mod[0])

    wi, bi = w_in[0], b_in[0]
    d = D_MODEL
    o_q, o_k, o_v, o_o, o_g, o_p, o_ga, o_gb = 0, d, 2 * d, 3 * d, 4 * d, 4 * d + 32, 4 * d + 32 + 512, 4 * d + 32 + 512 + d
    order = [(o_q, d), (o_k, d), (o_v, d), (o_o, d), (o_ga, d), (o_gb, d), (o_p, 512)]
    wmain = _pack_columns(wi, order)
    bmain = jnp.concatenate([bi[s:s + l] for s, l in order]).reshape(1, D_MAIN)
    colscale = jnp.ones((D_MAIN,), F32).at[SLAB_K * LANES:(SLAB_K + N_HEADS) * LANES].set(HEAD_DIM ** -0.5)
    colscale = colscale.reshape(1, D_MAIN)
    gwt = _take_columns_t(wi, o_g, 32)
    gb = bi[o_g:o_g + 32]
    i_f, f_f, i_b, f_b = (slice(0, 8), slice(8, 16), slice(16, 24), slice(24, 32))
    pad_w = jnp.zeros((LANES - N_HD, d), F32)
    pad_b = jnp.zeros((LANES - N_HD,), F32)
    wg2 = jnp.concatenate([gwt[f_f], gwt[f_b], pad_w, gwt[i_f], gwt[i_b], pad_w], axis=0).T
    bg2 = jnp.concatenate([gb[f_f], gb[f_b], pad_b, gb[i_f], gb[i_b], pad_b]).reshape(1, 2 * LANES)
    wg_hi, wg_lo = _split_bf16(wg2)

    wr = jnp.concatenate([w_router_grp[0], w_router_exp[0].reshape(d, N_EXPERTS),
                          jnp.zeros((d, LANES - ROUTER_ROWS), F32)], axis=1)
    br = jnp.concatenate([b_router_grp[0], b_router_exp[0].reshape(N_EXPERTS),
                          jnp.zeros((LANES - ROUTER_ROWS,), F32)]).reshape(1, LANES)
    wr_hi, wr_lo = _split_bf16(wr)

    w = dict(
        g_norm1=g_norm1[0].reshape(1, d), wmain=wmain, bmain=bmain, colscale=colscale,
        wg_hi=wg_hi, wg_lo=wg_lo, bg=bg2,
        gn3=gn_gain[0].reshape(N_HEADS, 1, HEAD_DIM),
        wpg=w_pool_grp[0].astype(BF16), pscale=pool_scale[0].reshape(N_GROUPS, 1, LANES),
        wa=w_proj_a[0].astype(BF16), wb=w_proj_b[0].astype(BF16), wo=w_out[0].astype(BF16),
        g_norm2=g_norm2[0].reshape(1, d), wr_hi=wr_hi, wr_lo=wr_lo, br=br,
        wg=w_exp_gate[0], wu=w_exp_up[0], wd=w_exp_down[0],
        g_final=g_final.reshape(1, d),
    )

    y_prompt, (_, cst, nst), cs_prompt = _stream(x_prompt, mod3, lambda r: 0, tp, None, None, w)
    n_rep = jnp.broadcast_to(state_n[..., None], state_C.shape)
    m0 = jnp.pad(state_m[:, 0].reshape(bs, N_HD), ((0, 0), (0, LANES - N_HD))).reshape(bs, 1, LANES)
    y_sample, _, _ = _stream(x_sample, mod3, lambda r: 1 + r // ts, ts, ts // GRID_W, (state_C, n_rep, m0), w)

    new_c = cst
    new_n = nst.reshape(bp, 1, 2, N_HEADS, HEAD_DIM)
    new_m = cs_prompt[:, 1, :N_HD].reshape(bp, 1, 2, N_HEADS)
    return (y_prompt, y_sample, new_c, new_n, new_m)
```

```python
import functools

import numpy as np
import jax
import jax.numpy as jnp
from jax import lax
from jax.experimental import pallas as pl
from jax.experimental.pallas import tpu as pltpu
from jax.experimental.pallas import tpu_sc as plsc

F32 = jnp.float32
BF16 = jnp.bfloat16

D_MODEL = 1024
N_HEADS = 8
HEAD_DIM = 128
GRID_W = 64
POOL_WINDOWS = (2, 4, 8, 16)
N_GROUPS = 4
EXPERTS_PER_GROUP = 4
N_EXPERTS = 16
D_EXPERT = 512
EPS = 1e-6
NEG = -1e30

LANES = 128
CHUNK = 256
N_SLABS = 52
SLAB_Q, SLAB_K, SLAB_V, SLAB_O, SLAB_GA, SLAB_GB, SLAB_POOL = 0, 8, 16, 24, 32, 40, 48
D_MAIN = N_SLABS * LANES
VMEM_LIMIT = 56 * 1024 * 1024


def _cparams(sem):
    return pltpu.CompilerParams(dimension_semantics=sem, vmem_limit_bytes=VMEM_LIMIT)


def _const_spec(shape):
    nd = len(shape)
    return pl.BlockSpec(shape, lambda *_: (0,) * nd, pipeline_mode=pl.Buffered(1))


def _split_bf16(x):
    hi = x.astype(BF16)
    lo = (x - hi.astype(F32)).astype(BF16)
    return hi, lo


def _dot(a, b):
    return jnp.dot(a, b, preferred_element_type=F32)


def _dot3(a, b_hi, b_lo):
    a_hi, a_lo = _split_bf16(a)
    return _dot(a_hi, b_hi) + _dot(a_hi, b_lo) + _dot(a_lo, b_hi)


def _rms_scale(x):
    return lax.rsqrt(jnp.mean(x * x, axis=-1, keepdims=True) + EPS)


def _log_sigmoid(x):
    return jnp.minimum(x, 0.0) - jnp.log1p(jnp.exp(-jnp.abs(x)))


def _mod_kernel(c_ref, w_ref, b_ref, o_ref):
    c = c_ref[...]
    s = c * jax.nn.sigmoid(c)
    w_hi, w_lo = _split_bf16(w_ref[...])
    o_ref[...] = _dot3(s, w_hi, w_lo) + b_ref[...]


def _modulation(cvec, w_mod, b_mod):
    tn = 512
    out = pl.pallas_call(
        _mod_kernel,
        out_shape=jax.ShapeDtypeStruct((8, 6 * D_MODEL), F32),
        grid=(6 * D_MODEL // tn,),
        in_specs=[
            pl.BlockSpec((8, D_MODEL), lambda j: (0, 0)),
            pl.BlockSpec((D_MODEL, tn), lambda j: (0, j)),
            pl.BlockSpec((1, tn), lambda j: (0, j)),
        ],
        out_specs=pl.BlockSpec((8, tn), lambda j: (0, j)),
        compiler_params=_cparams(("parallel",)),
        name="modulation",
    )(cvec, w_mod, b_mod.reshape(1, -1))
    return out.reshape(8, 6, D_MODEL)


PACK_COLS = 512


def _pack_kernel(wt_ref, o_ref):
    o_ref[...] = wt_ref[...].T.astype(BF16)


def _pack_columns(w, order):
    k, n = w.shape
    starts = []
    for src, width in order:
        assert width % PACK_COLS == 0
        starts += [src + j for j in range(0, width, PACK_COLS)]
    n_out = len(starts) * PACK_COLS

    unit = 32
    assert all(s % unit == 0 for s in starts)

    def src_row(j):
        row = jnp.int32(starts[-1] // unit)
        for idx in range(len(starts) - 2, -1, -1):
            row = jnp.where(j == idx, starts[idx] // unit, row)
        return row * unit

    return pl.pallas_call(
        _pack_kernel,
        out_shape=jax.ShapeDtypeStruct((k, n_out), BF16),
        grid=(len(starts),),
        in_specs=[pl.BlockSpec((pl.Element(PACK_COLS), pl.Element(k)), lambda j: (src_row(j), 0))],
        out_specs=pl.BlockSpec((k, PACK_COLS), lambda j: (0, j)),
        compiler_params=_cparams(("parallel",)),
        name="pack_columns",
    )(w.T)


def _copy_kernel(w_ref, o_ref):
    o_ref[...] = w_ref[...]


def _take_columns_t(w, start, width):
    k = w.shape[0]
    return pl.pallas_call(
        _copy_kernel,
        out_shape=jax.ShapeDtypeStruct((width, k), w.dtype),
        grid=(1,),
        in_specs=[pl.BlockSpec((pl.Element(width), pl.Element(k)), lambda j: (start, 0))],
        out_specs=pl.BlockSpec((width, k), lambda j: (0, 0)),
        name="take_columns",
    )(w.T)


IN_TM = 512
IN_TN = 512
LOG2E = 1.4426950408889634
N_HD = 2 * N_HEADS


def _fwd_lanes(shape):
    lane = lax.broadcasted_iota(jnp.int32, shape, len(shape) - 1)
    return (lane & N_HEADS) == 0


def _inproj_kernel(x_ref, mod_ref, g_ref, w_ref, b_ref, cs_ref, wg_hi_ref, wg_lo_ref, bg_ref,
                   p3_ref, bcum_ref, imb_ref, amax_ref):
    x = x_ref[...]
    h = x * _rms_scale(x) * g_ref[...]
    h = h * (1.0 + mod_ref[0, 1:2, :]) + mod_ref[0, 0:1, :]
    hb = h.astype(BF16)
    for n in range(D_MAIN // IN_TN):
        sl = slice(n * IN_TN, (n + 1) * IN_TN)
        acc = (_dot(hb, w_ref[:, sl]) + b_ref[:, sl]) * cs_ref[:, sl]
        accb = acc.astype(BF16)
        for j in range(IN_TN // LANES):
            p3_ref[n * (IN_TN // LANES) + j] = accb[:, j * LANES:(j + 1) * LANES]

    g2 = _dot3(h, wg_hi_ref[...], wg_lo_ref[...]) + bg_ref[...]
    ls = _log_sigmoid(g2[:, :LANES])
    gi = g2[:, LANES:]
    row = lax.broadcasted_iota(jnp.int32, (CHUNK, LANES), 0)
    fwd = _fwd_lanes((CHUNK, LANES))
    for c in range(IN_TM // CHUNK):
        rows = slice(c * CHUNK, (c + 1) * CHUNK)
        cf = ls[rows]
        cb = cf
        s = 1
        while s < CHUNK:
            cf = cf + jnp.where(row >= s, pltpu.roll(cf, s, 0), 0.0)
            cb = cb + jnp.where(row < CHUNK - s, pltpu.roll(cb, CHUNK - s, 0), 0.0)
            s *= 2
        b = jnp.where(fwd, cf, cb)
        imb = gi[rows] - b
        mf = imb
        mb = imb
        s = 1
        while s < CHUNK:
            mf = jnp.maximum(mf, jnp.where(row >= s, pltpu.roll(mf, s, 0), -jnp.inf))
            mb = jnp.maximum(mb, jnp.where(row < CHUNK - s, pltpu.roll(mb, CHUNK - s, 0), -jnp.inf))
            s *= 2
        bcum_ref[rows, :] = b
        imb_ref[rows, :] = imb
        amax_ref[rows, :] = b + jnp.where(fwd, mf, mb)


def _in_projection(x2d, mod3, mod_row_fn, g_norm1, wmain, bmain, colscale, wg_hi, wg_lo, bg):
    n = x2d.shape[0]
    return pl.pallas_call(
        _inproj_kernel,
        out_shape=(jax.ShapeDtypeStruct((N_SLABS, n, LANES), BF16),) + (jax.ShapeDtypeStruct((n, LANES), F32),) * 3,
        grid=(n // IN_TM,),
        in_specs=[
            pl.BlockSpec((IN_TM, D_MODEL), lambda i: (i, 0)),
            pl.BlockSpec((1, 6, D_MODEL), lambda i: (mod_row_fn(i * IN_TM), 0, 0)),
            _const_spec((1, D_MODEL)),
            _const_spec((D_MODEL, D_MAIN)),
            _const_spec((1, D_MAIN)),
            _const_spec((1, D_MAIN)),
            _const_spec((D_MODEL, 2 * LANES)),
            _const_spec((D_MODEL, 2 * LANES)),
            _const_spec((1, 2 * LANES)),
        ],
        out_specs=(pl.BlockSpec((N_SLABS, IN_TM, LANES), lambda i: (0, i, 0)),)
        + (pl.BlockSpec((IN_TM, LANES), lambda i: (i, 0)),) * 3,
        compiler_params=_cparams(("parallel",)),
        name="in_projection",
    )(x2d, mod3, g_norm1, wmain, bmain, colscale, wg_hi, wg_lo, bg)


GP_ROWS = 1024
GP_NCH = GP_ROWS // CHUNK
Q_C2, Q_EM, Q_WI, Q_WC = 0, 1, 2, 3


def _gate_prep_kernel(*refs, carry):
    if carry:
        b_ref, imb_ref, a_ref, m0_ref, cols_ref, rw_ref, cs_ref = refs
    else:
        b_ref, imb_ref, a_ref, cols_ref, rw_ref, cs_ref = refs
    fwd1 = _fwd_lanes((1, LANES))
    lane = lax.broadcasted_iota(jnp.int32, (CHUNK, LANES), 1)
    neg = jnp.full((1, LANES), NEG, F32)

    def at_scan_end(ref, c):
        return jnp.where(fwd1, ref[(c + 1) * CHUNK - 1:(c + 1) * CHUNK, :], ref[c * CHUNK:c * CHUNK + 1, :])

    b_last = [at_scan_end(b_ref, c) for c in range(GP_NCH)]
    a_last = [at_scan_end(a_ref, c) for c in range(GP_NCH)]
    m_start = m0_ref[0] if carry else neg
    m_prev = [[None] * GP_NCH, [None] * GP_NCH]
    m_new = [[None] * GP_NCH, [None] * GP_NCH]
    for d, order in enumerate((range(GP_NCH), range(GP_NCH - 1, -1, -1))):
        m = m_start
        for c in order:
            m_prev[d][c] = m
            m_new[d][c] = jnp.maximum(b_last[c] + m, a_last[c])
            m = m_new[d][c] if carry else neg

    row_src = []
    for c in range(GP_NCH):
        rows = slice(c * CHUNK, (c + 1) * CHUNK)
        mp = jnp.where(fwd1, m_prev[0][c], m_prev[1][c])
        mn = jnp.where(fwd1, m_new[0][c], m_new[1][c])
        b = b_ref[rows, :]
        imb = imb_ref[rows, :]
        inter = b + mp
        m_t = jnp.maximum(inter, a_ref[rows, :])
        c2 = (b - m_t) * LOG2E
        em = jnp.exp(-m_t)
        wc = jnp.exp(b_last[c] - mn + imb)
        packed = jnp.where(lane < N_HD, c2, pltpu.roll(em, N_HD * Q_EM, 1))
        if carry:
            wi = jnp.exp(inter - m_t)
            packed = jnp.where(lane < N_HD * Q_WI, packed, pltpu.roll(wi, N_HD * Q_WI, 1))
        cols_ref[rows, :] = jnp.where(lane < N_HD * Q_WC, packed, pltpu.roll(wc, N_HD * Q_WC, 1))
        row_src.append(jnp.where(lane < N_HD, imb * (-LOG2E), pltpu.roll(wc, N_HD, 1)))
        cs_ref[c, 0:1, :] = jnp.exp(b_last[c] + mp - mn)
        cs_ref[c, 1:2, :] = mn

    rows_t = jnp.concatenate(row_src, axis=0).T
    for r in range(2 * N_HD):
        rw_ref[r] = rows_t[r:r + 1]


def _gate_prep(bcum, imb, amax, m0=None):
    n = bcum.shape[0]
    carry = m0 is not None
    blk = pl.BlockSpec((GP_ROWS, LANES), lambda i: (i, 0))
    in_specs = [blk, blk, blk]
    args = [bcum, imb, amax]
    if carry:
        in_specs.append(pl.BlockSpec((1, 1, LANES), lambda i: (i, 0, 0)))
        args.append(m0)
    return pl.pallas_call(
        functools.partial(_gate_prep_kernel, carry=carry),
        out_shape=(jax.ShapeDtypeStruct((n, LANES), F32), jax.ShapeDtypeStruct((2 * N_HD, 1, n), F32),
                   jax.ShapeDtypeStruct((n // CHUNK, 2, LANES), F32)),
        grid=(n // GP_ROWS,),
        in_specs=in_specs,
        out_specs=(blk, pl.BlockSpec((2 * N_HD, 1, GP_ROWS), lambda i: (0, 0, i)),
                   pl.BlockSpec((GP_NCH, 2, LANES), lambda i: (i, 0, 0))),
        compiler_params=_cparams(("parallel",)),
        name="gate_prep",
    )(*args)


ML_ROWS = 1024
ML_ROWS_FREE = 2048


def _pick_col(blk, lane_iota, idx):
    return jnp.sum(jnp.where(lane_iota == idx, blk, 0.0), axis=1, keepdims=True)


def _mlstm_kernel(*refs, carry, n_chunks):
    if carry:
        (q_ref, k_ref, v_ref, o_ref, cols_ref, r2f_ref, r2b_ref, gn_ref, cs_ref, c0_ref, n0_ref,
         ya_ref, s_scr, h_scr) = refs
    else:
        (q_ref, k_ref, v_ref, o_ref, cols_ref, r2f_ref, r2b_ref, gn_ref, wrf_ref, wrb_ref,
         ya_ref, cst_ref, nst_ref, s_scr, h_scr) = refs
    head = pl.program_id(1)
    lane = lax.broadcasted_iota(jnp.int32, (CHUNK, LANES), 1)
    lane1 = lax.broadcasted_iota(jnp.int32, (1, LANES), 1)
    r_idx = lax.broadcasted_iota(jnp.int32, (CHUNK, CHUNK), 0)
    c_idx = lax.broadcasted_iota(jnp.int32, (CHUNK, CHUNK), 1)
    ones = jnp.ones((CHUNK, HEAD_DIM), BF16)
    tn_dims = (((0,), (0,)), ((), ()))

    for d in range(2):
        mask = (c_idx <= r_idx) if d == 0 else (c_idx >= r_idx)
        r2_ref = r2f_ref if d == 0 else r2b_ref
        base = N_HEADS * d + head
        if carry:
            cn = jnp.concatenate([c0_ref[0, 0, d, 0], n0_ref[0, 0, d, 0]], axis=1)
        order = range(n_chunks) if d == 0 else range(n_chunks - 1, -1, -1)
        for step, c in enumerate(order):
            rows = slice(c * CHUNK, (c + 1) * CHUNK)
            q = q_ref[0, rows, :]
            k = k_ref[0, rows, :]
            v = v_ref[0, rows, :]
            v1 = jnp.concatenate([v, ones], axis=1)
            if d == 0:
                s = lax.dot_general(q, k, (((1,), (1,)), ((), ())), preferred_element_type=F32)
                s_scr[rows, :] = s
            else:
                s = s_scr[rows, :]
            blk = cols_ref[rows, :]
            arg = _pick_col(blk, lane, base + N_HD * Q_C2) - r2_ref[0, :, rows]
            sw = s * jnp.exp2(jnp.where(mask, arg, NEG))
            nd = _dot(sw.astype(BF16), v1)
            if carry:
                nd = nd + _pick_col(blk, lane, base + N_HD * Q_WI) * _dot(q, cn.astype(BF16))
            hdir = nd[:, :HEAD_DIM] / jnp.maximum(jnp.abs(nd[:, HEAD_DIM:]), _pick_col(blk, lane, base + N_HD * Q_EM))
            if d == 0:
                h_scr[rows, :] = hdir
            else:
                h_scr[rows, :] = h_scr[rows, :] + hdir

            if carry and step == n_chunks - 1:
                continue
            w_col = _pick_col(blk, lane, base + N_HD * Q_WC)
            if carry:
                u = lax.dot_general(k, (w_col * v1.astype(F32)).astype(BF16), tn_dims,
                                    preferred_element_type=F32)
                decay = jnp.sum(jnp.where(lane1 == base, cs_ref[c, 0:1, :], 0.0), axis=1, keepdims=True)
                cn = decay * cn + u
            else:
                w_row = (wrf_ref if d == 0 else wrb_ref)[0, :, rows]
                w8 = jnp.broadcast_to(w_row, (8, CHUNK)).astype(BF16)
                cst_ref[c, 0, d, 0] = lax.dot_general(k, (w_col * v.astype(F32)).astype(BF16), tn_dims,
                                                      preferred_element_type=F32)
                nst_ref[c, d, 0] = _dot(w8, k)[0:1, :]

    hh = h_scr[...]
    hn = hh * _rms_scale(hh) * gn_ref[0]
    ya_ref[0] = (hn * jax.nn.sigmoid(o_ref[0].astype(F32))).astype(BF16)


def _mlstm(p3, cols, rows, cs, gn3, state=None):
    n = p3.shape[1]
    carry = state is not None
    rows_step = ML_ROWS if carry else ML_ROWS_FREE
    n_chunks = rows_step // CHUNK
    nblk = n // rows_step

    def slab(base):
        return pl.BlockSpec((1, rows_step, LANES), lambda g, h: (base + h, g, 0))

    def row(base):
        return pl.BlockSpec((1, 1, rows_step), lambda g, h: (base + h, 0, g))

    in_specs = [
        slab(SLAB_Q), slab(SLAB_K), slab(SLAB_V), slab(SLAB_O),
        pl.BlockSpec((rows_step, LANES), lambda g, h: (g, 0)),
        row(0), row(N_HEADS),
        pl.BlockSpec((1, 1, LANES), lambda g, h: (h, 0, 0)),
    ]
    args = [p3, p3, p3, p3, cols, rows, rows, gn3]
    ya_shape = jax.ShapeDtypeStruct((N_HEADS, n, LANES), BF16)
    ya_spec = pl.BlockSpec((1, rows_step, LANES), lambda g, h: (h, g, 0))
    state_spec = pl.BlockSpec((1, 1, 2, 1, HEAD_DIM, HEAD_DIM), lambda g, h: (g, 0, 0, h, 0, 0))
    if carry:
        c0, n0 = state
        in_specs += [pl.BlockSpec((n_chunks, 2, LANES), lambda g, h: (g, 0, 0)), state_spec, state_spec]
        args += [cs, c0, n0]
        out_shape = ya_shape
        out_specs = ya_spec
    else:
        in_specs += [row(N_HD), row(N_HD + N_HEADS)]
        args += [rows, rows]
        nseq = n // CHUNK
        out_shape = (ya_shape,
                     jax.ShapeDtypeStruct((nseq, 1, 2, N_HEADS, HEAD_DIM, HEAD_DIM), F32),
                     jax.ShapeDtypeStruct((nseq, 2, N_HEADS, 1, HEAD_DIM), F32))
        out_specs = (ya_spec,
                     pl.BlockSpec((n_chunks, 1, 2, 1, HEAD_DIM, HEAD_DIM), lambda g, h: (g, 0, 0, h, 0, 0)),
                     pl.BlockSpec((n_chunks, 2, 1, 1, HEAD_DIM), lambda g, h: (g, 0, h, 0, 0)))
    return pl.pallas_call(
        functools.partial(_mlstm_kernel, carry=carry, n_chunks=n_chunks),
        out_shape=out_shape,
        grid=(nblk, N_HEADS),
        in_specs=in_specs,
        out_specs=out_specs,
        scratch_shapes=[pltpu.VMEM((rows_step, CHUNK), F32), pltpu.VMEM((rows_step, HEAD_DIM), F32)],
        compiler_params=_cparams(("parallel", "parallel")),
        name="mlstm_latent" if carry else "mlstm_prompt",
    )(*args)


def _pool_kernel(u_ref, a_ref, ic_ref, wpg_ref, ps_ref, yb_ref, *, seq_len, n_seq, rows_2d):
    piece = seq_len if rows_2d is None else GRID_W
    n_piece = n_seq if rows_2d is None else rows_2d
    for g in range(N_GROUPS):
        u = jnp.concatenate([u_ref[g, i * piece:(i + 1) * piece, :] for i in range(n_piece)], axis=1)
        win = _dot(a_ref[g], u)
        if rows_2d is None:
            inv_cnt = jnp.concatenate([ic_ref[g]] * n_piece, axis=1)
        else:
            run = [jnp.zeros((piece, LANES), F32)]
            for r in range(rows_2d):
                run.append(run[-1] + win[:, r * LANES:(r + 1) * LANES])
            half = POOL_WINDOWS[g] // 2
            win = jnp.concatenate([run[min(r - half + POOL_WINDOWS[g], rows_2d)] - run[max(r - half, 0)]
                                   for r in range(rows_2d)], axis=1)
            inv_cnt = jnp.concatenate([ic_ref[g, r * piece:(r + 1) * piece, :] for r in range(rows_2d)], axis=1)
        p = (win * inv_cnt - u.astype(F32)).astype(BF16)
        p = jnp.concatenate([p[:, i * LANES:(i + 1) * LANES] for i in range(n_piece)], axis=0)
        yb_ref[g] = (_dot(p, wpg_ref[g]) * ps_ref[g]).astype(BF16)


def _pool(p3, amat, invc, wpg, pscale, seq_len, n_seq, rows_2d):
    n = p3.shape[1]
    rows = seq_len * n_seq
    return pl.pallas_call(
        functools.partial(_pool_kernel, seq_len=seq_len, n_seq=n_seq, rows_2d=rows_2d),
        out_shape=jax.ShapeDtypeStruct((N_GROUPS, n, LANES), BF16),
        grid=(n // rows,),
        in_specs=[
            pl.BlockSpec((N_GROUPS, rows, LANES), lambda i: (SLAB_POOL // N_GROUPS, i, 0)),
            _const_spec(amat.shape),
            _const_spec((N_GROUPS, seq_len, LANES)),
            _const_spec((N_GROUPS, LANES, LANES)),
            _const_spec((N_GROUPS, 1, LANES)),
        ],
        out_specs=pl.BlockSpec((N_GROUPS, rows, LANES), lambda i: (0, i, 0)),
        compiler_params=_cparams(("parallel",)),
        name="pool",
    )(p3, amat, invc, wpg, pscale)


def _window_matrix(size, w):
    idx = np.arange(size)
    lo = np.clip(idx - w // 2, 0, size)
    hi = np.clip(idx - w // 2 + w, 0, size)
    s = np.arange(size)[None, :]
    a = ((s >= lo[:, None]) & (s < hi[:, None])).astype(np.float32)
    return a, (hi - lo).astype(np.float32)


def _pool_constants(seq_len, rows_2d):
    mats, invs = [], []
    for w in POOL_WINDOWS:
        if rows_2d is None:
            a, cnt = _window_matrix(seq_len, w)
        else:
            _, cr = _window_matrix(rows_2d, w)
            a, cc = _window_matrix(GRID_W, w)
            cnt = np.kron(cr, cc)
        mats.append(a)
        invs.append(np.repeat((1.0 / cnt)[:, None], LANES, axis=1))
    return (jnp.asarray(np.stack(mats), dtype=BF16), jnp.asarray(np.stack(invs), dtype=F32))


MIX_TM = 512


def _mix_kernel(x_ref, mod_ref, ya_ref, yb_ref, ga_ref, gb_ref, wa_ref, wb_ref, wo_ref,
                g2_ref, wr_hi_ref, wr_lo_ref, br_ref, x1_ref, h2_ref, info_ref):
    ya = jnp.concatenate([ya_ref[i] for i in range(N_HEADS)], axis=1)
    yb = jnp.concatenate([yb_ref[i] for i in range(N_GROUPS)], axis=1)
    ga = jnp.concatenate([ga_ref[i] for i in range(8)], axis=1).astype(F32)
    gb = jnp.concatenate([gb_ref[i] for i in range(8)], axis=1).astype(F32)
    merged = jax.nn.sigmoid(ga) * _dot(ya, wa_ref[...]) + jax.nn.sigmoid(gb) * _dot(yb, wb_ref[...])
    out = _dot(merged.astype(BF16), wo_ref[...])
    x1 = x_ref[...] + mod_ref[0, 2:3, :] * out
    x1_ref[...] = x1

    h2 = x1 * _rms_scale(x1) * g2_ref[...]
    h2 = h2 * (1.0 + mod_ref[0, 4:5, :]) + mod_ref[0, 3:4, :]
    h_hi, h_lo = _split_bf16(h2)
    packed = _pack_bf16_pairs(h_hi.astype(F32))
    h2_ref[0:MIX_TM, :] = packed[:, :SC_ROW]
    h2_ref[MIX_TM:2 * MIX_TM, :] = packed[:, SC_ROW:]
    wr_hi = wr_hi_ref[...]
    logits = _dot(h_hi, wr_hi) + _dot(h_lo, wr_hi) + _dot(h_hi, wr_lo_ref[...]) + br_ref[...]
    first, second, w1, w2 = _route_rows(logits.T)
    zero = jnp.zeros_like(w1)
    info_ref[...] = jnp.concatenate([first, second, w1, w2, zero, zero, zero, zero], axis=0)


def _mix(x2d, mod3, mod_row_fn, ya, yb, p3, wa, wb, wo, g_norm2, wr_hi, wr_lo, br):
    n = x2d.shape[0]
    return pl.pallas_call(
        _mix_kernel,
        out_shape=(jax.ShapeDtypeStruct((n, D_MODEL), F32),
                   jax.ShapeDtypeStruct((2 * n, SC_ROW), jnp.int32), jax.ShapeDtypeStruct((8, n), F32)),
        grid=(n // MIX_TM,),
        in_specs=[
            pl.BlockSpec((MIX_TM, D_MODEL), lambda i: (i, 0)),
            pl.BlockSpec((1, 6, D_MODEL), lambda i: (mod_row_fn(i * MIX_TM), 0, 0)),
            pl.BlockSpec((N_HEADS, MIX_TM, LANES), lambda i: (0, i, 0)),
            pl.BlockSpec((N_GROUPS, MIX_TM, LANES), lambda i: (0, i, 0)),
            pl.BlockSpec((8, MIX_TM, LANES), lambda i: (SLAB_GA // 8, i, 0)),
            pl.BlockSpec((8, MIX_TM, LANES), lambda i: (SLAB_GB // 8, i, 0)),
            _const_spec((D_MODEL, D_MODEL)),
            _const_spec((N_GROUPS * LANES, D_MODEL)),
            _const_spec((D_MODEL, D_MODEL)),
            _const_spec((1, D_MODEL)),
            _const_spec((D_MODEL, LANES)),
            _const_spec((D_MODEL, LANES)),
            _const_spec((1, LANES)),
        ],
        out_specs=(pl.BlockSpec((MIX_TM, D_MODEL), lambda i: (i, 0)),
                   pl.BlockSpec((2 * MIX_TM, SC_ROW), lambda i: (i, 0)),
                   pl.BlockSpec((8, MIX_TM), lambda i: (0, i))),
        compiler_params=_cparams(("parallel",)),
        name="mix",
    )(x2d, mod3, ya, yb, p3, p3, wa, wb, wo, g_norm2, wr_hi, wr_lo, br)


PLAN_TB = 1024
ROUTER_ROWS = 4 + N_EXPERTS
SLOT_TM = 1024
SC_ROW = 256
SC_WIN = 128
HI_MASK = -65536


def _pack_bf16_pairs(x):
    bits = pltpu.bitcast(x, jnp.int32)
    half = D_MODEL // 2
    return jnp.bitwise_or(jnp.bitwise_and(lax.shift_right_logical(bits[:, :half], 16), 0xFFFF),
                          jnp.bitwise_and(bits[:, half:], HI_MASK))


def _unpack_bf16_pairs(lo_words, hi_words):
    def low(w):
        return pltpu.bitcast(lax.shift_left(w, 16), F32)

    def high(w):
        return pltpu.bitcast(jnp.bitwise_and(w, HI_MASK), F32)

    return jnp.concatenate([low(lo_words), low(hi_words), high(lo_words), high(hi_words)], axis=1).astype(BF16)


def _route_rows(lt):
    lg = [lt[i:i + 1] for i in range(N_GROUPS)]
    mx = jnp.maximum(jnp.maximum(lg[0], lg[1]), jnp.maximum(lg[2], lg[3]))
    p_top = 1.0 / (jnp.exp(lg[0] - mx) + jnp.exp(lg[1] - mx) + jnp.exp(lg[2] - mx) + jnp.exp(lg[3] - mx))
    gidx = jnp.zeros(lg[0].shape, jnp.int32)
    best = lg[0]
    for i in range(1, N_GROUPS):
        upd = lg[i] > best
        gidx = jnp.where(upd, i, gidx)
        best = jnp.where(upd, lg[i], best)

    def expert_row(g, e):
        r = N_GROUPS + g * EXPERTS_PER_GROUP + e
        return lt[r:r + 1]

    le = [jnp.where(gidx == 0, expert_row(0, e),
                    jnp.where(gidx == 1, expert_row(1, e),
                              jnp.where(gidx == 2, expert_row(2, e), expert_row(3, e))))
          for e in range(EXPERTS_PER_GROUP)]
    v1 = jnp.maximum(jnp.maximum(le[0], le[1]), jnp.maximum(le[2], le[3]))
    i1 = jnp.full(v1.shape, EXPERTS_PER_GROUP - 1, jnp.int32)
    for e in range(EXPERTS_PER_GROUP - 2, -1, -1):
        i1 = jnp.where(le[e] == v1, e, i1)
    le2 = [jnp.where(i1 == e, -jnp.inf, le[e]) for e in range(EXPERTS_PER_GROUP)]
    v2 = jnp.maximum(jnp.maximum(le2[0], le2[1]), jnp.maximum(le2[2], le2[3]))
    i2 = jnp.full(v2.shape, EXPERTS_PER_GROUP - 1, jnp.int32)
    for e in range(EXPERTS_PER_GROUP - 2, -1, -1):
        i2 = jnp.where(le2[e] == v2, e, i2)
    e2 = jnp.exp(v2 - v1)
    first = (gidx * EXPERTS_PER_GROUP + i1).astype(F32)
    second = (gidx * EXPERTS_PER_GROUP + i2).astype(F32)
    return first, second, p_top / (1.0 + e2), p_top * e2 / (1.0 + e2)


def _plan_kernel(info_ref, before_ref, pos_ref, tmap_ref, rank_scr, count_scr, *, n_blocks):
    step = pl.program_id(0)
    expert_id = lax.broadcasted_iota(jnp.int32, (N_EXPERTS, PLAN_TB), 0).astype(F32)

    @pl.when(step == 0)
    def _():
        count_scr[...] = jnp.zeros_like(count_scr)

    @pl.when(step < n_blocks)
    def _():
        cols = pl.ds(pl.multiple_of(step * PLAN_TB, PLAN_TB), PLAN_TB)
        sel = jnp.where((info_ref[0:1, cols] == expert_id) | (info_ref[1:2, cols] == expert_id), 1.0, 0.0)
        selb = sel.astype(BF16)
        rank_scr[:, cols] = _dot(selb, before_ref[...]) + count_scr[:, 0:1]
        count_scr[...] += _dot(selb, jnp.ones((PLAN_TB, LANES), BF16))

    @pl.when(step == n_blocks)
    def _():
        counts = count_scr[...]
        padded = jnp.floor((counts + (SLOT_TM - 1)) * (1.0 / SLOT_TM)) * SLOT_TM
        starts, ends, run = [], [], jnp.zeros((1, LANES), F32)
        for e in range(N_EXPERTS):
            starts.append(run)
            run = run + padded[e:e + 1]
            ends.append(run)
        n = pos_ref.shape[1]
        for k in range(2):
            chosen = info_ref[k:k + 1, :]
            slot = jnp.zeros((1, n), F32)
            for e in range(N_EXPERTS):
                slot = jnp.where(chosen == float(e), starts[e][:, 0:1] + rank_scr[e:e + 1, :], slot)
            pos_ref[k:k + 1, :] = slot.astype(jnp.int32)
        pos_ref[2:8, :] = jnp.zeros((6, n), jnp.int32)
        tile_start = lax.broadcasted_iota(jnp.int32, (1, LANES), 1).astype(F32) * SLOT_TM
        used = jnp.where(tile_start < run, 1.0, 0.0)
        tile_start = jnp.minimum(tile_start, run - SLOT_TM)
        owner = jnp.zeros((1, LANES), F32)
        for e in range(N_EXPERTS - 1):
            owner = owner + jnp.where(ends[e] <= tile_start, 1.0, 0.0)
        tile = tile_start * (1.0 / SLOT_TM)
        tmap_ref[...] = jnp.concatenate([owner, used, tile] + [jnp.zeros((1, LANES), F32)] * 5,
                                        axis=0).astype(jnp.int32)


def _plan(info):
    n = info.shape[1]
    n_blocks = n // PLAN_TB
    t_i = lax.broadcasted_iota(jnp.int32, (PLAN_TB, PLAN_TB), 0)
    t_j = lax.broadcasted_iota(jnp.int32, (PLAN_TB, PLAN_TB), 1)
    before = (t_i < t_j).astype(BF16)
    return pl.pallas_call(
        functools.partial(_plan_kernel, n_blocks=n_blocks),
        out_shape=(jax.ShapeDtypeStruct((8, n), jnp.int32), jax.ShapeDtypeStruct((8, LANES), jnp.int32)),
        grid=(n_blocks + 1,),
        in_specs=[_const_spec((8, n)), _const_spec((PLAN_TB, PLAN_TB))],
        out_specs=(pl.BlockSpec((8, n), lambda i: (0, 0)), pl.BlockSpec((8, LANES), lambda i: (0, 0))),
        scratch_shapes=[pltpu.VMEM((N_EXPERTS, n), F32), pltpu.VMEM((N_EXPERTS, LANES), F32)],
        compiler_params=_cparams(("arbitrary",)),
        name="plan",
    )(info, before)


def _sc_mesh():
    return plsc.VectorSubcoreMesh(core_axis_name="core", subcore_axis_name="subcore")


def _sc_scatter_rows(x, idx, n_out):
    n_idx = idx.shape[0]
    src_blocks = x.shape[0] // SC_WIN

    @pl.kernel(out_type=jax.ShapeDtypeStruct((n_out, SC_ROW), x.dtype), mesh=_sc_mesh(), scratch_types=[])
    def scatter_kernel(x_hbm, i_hbm, o_hbm):
        def body(x_vmem, i_vmem):
            pltpu.sync_copy(x_vmem, o_hbm.at[i_vmem.at[0]])

        pltpu.emit_pipeline(
            body,
            grid=(n_idx // SC_WIN,),
            in_specs=[pl.BlockSpec((SC_WIN, SC_ROW), index_map=lambda i: (i % src_blocks, 0)),
                      pl.BlockSpec((1, SC_WIN), index_map=lambda i: (0, i))],
            out_specs=[],
            core_axis_name=("core", "subcore"),
            dimension_semantics=(pltpu.PARALLEL,),
        )(x_hbm, i_hbm)

    return scatter_kernel(x, idx.reshape(1, n_idx))


def _sc_gather_rows(x, idx):
    n_idx = idx.shape[0]

    @pl.kernel(out_type=jax.ShapeDtypeStruct((n_idx, SC_ROW), x.dtype), mesh=_sc_mesh(), scratch_types=[])
    def gather_kernel(x_hbm, i_hbm, o_hbm):
        def body(i_vmem, o_vmem):
            pltpu.sync_copy(x_hbm.at[i_vmem.at[0]], o_vmem)

        pltpu.emit_pipeline(
            body,
            grid=(n_idx // SC_WIN,),
            in_specs=[pl.BlockSpec((1, SC_WIN), index_map=lambda i: (0, i))],
            out_specs=[pl.BlockSpec((SC_WIN, SC_ROW), index_map=lambda i: (i, 0))],
            core_axis_name=("core", "subcore"),
            dimension_semantics=(pltpu.PARALLEL,),
        )(i_hbm, o_hbm)

    return gather_kernel(x, idx.reshape(1, n_idx))


def _experts_kernel(tmap_ref, xs_ref, wg_ref, wu_ref, wd_ref, ys_ref, wg_scr, wu_scr, wd_scr):
    j = pl.program_id(0)

    @pl.when((j == 0) | (tmap_ref[0, j] != tmap_ref[0, jnp.maximum(j - 1, 0)]))
    def _():
        wg_scr[...] = wg_ref[0].astype(BF16)
        wu_scr[...] = wu_ref[0].astype(BF16)
        wd_scr[...] = wd_ref[0].astype(BF16)

    @pl.when(tmap_ref[1, j] > 0)
    def _():
        x = _unpack_bf16_pairs(xs_ref[0:SLOT_TM, :], xs_ref[SLOT_TM:2 * SLOT_TM, :])
        a = _dot(x, wg_scr[...])
        u = _dot(x, wu_scr[...])
        act = (a * jax.nn.sigmoid(a) * u).astype(BF16)
        y = _dot(act, wd_scr[...]).astype(BF16).astype(F32)
        packed = _pack_bf16_pairs(y)
        ys_ref[0:SLOT_TM, :] = packed[:, :SC_ROW]
        ys_ref[SLOT_TM:2 * SLOT_TM, :] = packed[:, SC_ROW:]


def _experts(xs, tmap, wg, wu, wd):
    n_tiles = xs.shape[0] // (2 * SLOT_TM)
    grid_spec = pltpu.PrefetchScalarGridSpec(
        num_scalar_prefetch=1,
        grid=(n_tiles,),
        in_specs=[
            pl.BlockSpec((2 * SLOT_TM, SC_ROW), lambda j, tm: (tm[2, j], 0)),
            pl.BlockSpec((1, D_MODEL, D_EXPERT), lambda j, tm: (tm[0, j], 0, 0)),
            pl.BlockSpec((1, D_MODEL, D_EXPERT), lambda j, tm: (tm[0, j], 0, 0)),
            pl.BlockSpec((1, D_EXPERT, D_MODEL), lambda j, tm: (tm[0, j], 0, 0)),
        ],
        out_specs=pl.BlockSpec((2 * SLOT_TM, SC_ROW), lambda j, tm: (tm[2, j], 0)),
        scratch_shapes=[pltpu.VMEM((D_MODEL, D_EXPERT), BF16), pltpu.VMEM((D_MODEL, D_EXPERT), BF16),
                        pltpu.VMEM((D_EXPERT, D_MODEL), BF16)],
    )
    return pl.pallas_call(
        _experts_kernel,
        out_shape=jax.ShapeDtypeStruct(xs.shape, jnp.int32),
        grid_spec=grid_spec,
        compiler_params=_cparams(("arbitrary",)),
        name="experts",
    )(tmap, xs, wg, wu, wd)


FIN_TM = 512


def _final_kernel(x1_ref, mod_ref, yg_ref, w_ref, gf_ref, y_ref):
    moe = None
    for k in range(2):
        base = 2 * k * FIN_TM
        yk = _unpack_bf16_pairs(yg_ref[base:base + FIN_TM, :], yg_ref[base + FIN_TM:base + 2 * FIN_TM, :])
        term = w_ref[:, k:k + 1] * yk.astype(F32)
        moe = term if moe is None else moe + term
    x2 = x1_ref[...] + mod_ref[0, 5:6, :] * moe
    y_ref[...] = x2 * _rms_scale(x2) * gf_ref[...]


def _final(x1, mod3, mod_row_fn, yg, w_cols, g_final):
    n = x1.shape[0]
    return pl.pallas_call(
        _final_kernel,
        out_shape=jax.ShapeDtypeStruct((n, D_MODEL), F32),
        grid=(n // FIN_TM,),
        in_specs=[
            pl.BlockSpec((FIN_TM, D_MODEL), lambda i: (i, 0)),
            pl.BlockSpec((1, 6, D_MODEL), lambda i: (mod_row_fn(i * FIN_TM), 0, 0)),
            pl.BlockSpec((4 * FIN_TM, SC_ROW), lambda i: (i, 0)),
            pl.BlockSpec((FIN_TM, 2), lambda i: (i, 0)),
            _const_spec((1, D_MODEL)),
        ],
        out_specs=pl.BlockSpec((FIN_TM, D_MODEL), lambda i: (i, 0)),
        compiler_params=_cparams(("parallel",)),
        name="final",
    )(x1, mod3, yg, w_cols, g_final)


def _half_row(pos, half):
    return (pos // SLOT_TM) * (2 * SLOT_TM) + half * SLOT_TM + pos % SLOT_TM


def _moe(x1, h2p, info, mod3, mod_row_fn, wg, wu, wd, g_final):
    n = x1.shape[0]
    pos, tmap = _plan(info)
    n_slots = 2 * n + N_EXPERTS * SLOT_TM
    pos1, pos2 = pos[0], pos[1]

    def route_order(p):
        return jnp.stack([_half_row(p, 0).reshape(n // MIX_TM, MIX_TM),
                          _half_row(p, 1).reshape(n // MIX_TM, MIX_TM)], axis=1).reshape(-1)

    xs = _sc_scatter_rows(h2p, jnp.concatenate([route_order(pos1), route_order(pos2)]), 2 * n_slots)
    ys = _experts(xs, tmap, wg, wu, wd)

    def fin_order(p):
        return jnp.stack([_half_row(p, 0).reshape(n // FIN_TM, FIN_TM),
                          _half_row(p, 1).reshape(n // FIN_TM, FIN_TM)], axis=1)

    gidx = jnp.stack([fin_order(pos1), fin_order(pos2)], axis=1).reshape(-1)
    yg = _sc_gather_rows(ys, gidx)
    w_cols = info[2:4].T
    return _final(x1, mod3, mod_row_fn, yg, w_cols, g_final)


def _stream(x, mod3, mod_row_fn, seq_len, rows_2d, state, w):
    bsz, t, _ = x.shape
    n = bsz * t
    x2d = x.reshape(n, D_MODEL)
    p3, bcum, imb, amax = _in_projection(x2d, mod3, mod_row_fn, w["g_norm1"], w["wmain"], w["bmain"],
                                         w["colscale"], w["wg_hi"], w["wg_lo"], w["bg"])
    m0 = None if state is None else state[2]
    cols, rows, cs = _gate_prep(bcum, imb, amax, m0)
    ml = _mlstm(p3, cols, rows, cs, w["gn3"], None if state is None else state[:2])
    amat, invc = _pool_constants(seq_len, rows_2d)
    n_seq = max(1, 1024 // seq_len)
    yb = _pool(p3, amat, invc, w["wpg"], w["pscale"], seq_len, n_seq, rows_2d)
    ya = ml if state is not None else ml[0]
    x1, h2p, info = _mix(x2d, mod3, mod_row_fn, ya, yb, p3, w["wa"], w["wb"], w["wo"],
                         w["g_norm2"], w["wr_hi"], w["wr_lo"], w["br"])
    y = _moe(x1, h2p, info, mod3, mod_row_fn, w["wg"], w["wu"], w["wd"], w["g_final"])
    return y.reshape(bsz, t, D_MODEL), ml, cs


def kernel(x_prompt, x_sample, state_C, state_n, state_m, c, c_ctx, w_mod, b_mod, g_norm1, w_in, b_in, gn_gain, w_pool_grp, pool_scale, w_proj_a, w_proj_b, w_out, g_norm2, w_router_grp, b_router_grp, w_router_exp, b_router_exp, w_exp_gate, w_exp_up, w_exp_down, g_final):
    bp, tp, _ = x_prompt.shape
    bs, ts, _ = x_sample.shape
    assert w_mod.shape[0] == 1, "single trunk layer"
    assert tp == CHUNK and ts % CHUNK == 0 and ts == ML_ROWS

    cvec = jnp.zeros((8, D_MODEL), F32).at[0].set(c_ctx).at[1:1 + bs].set(c)
    mod3 = _modulation(cvec, w_mod[0], b_mod[0])

    wi, bi = w_in[0], b_in[0]
    d = D_MODEL
    o_q, o_k, o_v, o_o, o_g, o_p, o_ga, o_gb = 0, d, 2 * d, 3 * d, 4 * d, 4 * d + 32, 4 * d + 32 + 512, 4 * d + 32 + 512 + d
    order = [(o_q, d), (o_k, d), (o_v, d), (o_o, d), (o_ga, d), (o_gb, d), (o_p, 512)]
    wmain = _pack_columns(wi, order)
    bmain = jnp.concatenate([bi[s:s + l] for s, l in order]).reshape(1, D_MAIN)
    colscale = jnp.ones((D_MAIN,), F32).at[SLAB_K * LANES:(SLAB_K + N_HEADS) * LANES].set(HEAD_DIM ** -0.5)
    colscale = colscale.reshape(1, D_MAIN)
    gwt = _take_columns_t(wi, o_g, 32)
    gb = bi[o_g:o_g + 32]
    i_f, f_f, i_b, f_b = (slice(0, 8), slice(8, 16), slice(16, 24), slice(24, 32))
    pad_w = jnp.zeros((LANES - N_HD, d), F32)
    pad_b = jnp.zeros((LANES - N_HD,), F32)
    wg2 = jnp.concatenate([gwt[f_f], gwt[f_b], pad_w, gwt[i_f], gwt[i_b], pad_w], axis=0).T
    bg2 = jnp.concatenate([gb[f_f], gb[f_b], pad_b, gb[i_f], gb[i_b], pad_b]).reshape(1, 2 * LANES)
    wg_hi, wg_lo = _split_bf16(wg2)

    wr = jnp.concatenate([w_router_grp[0], w_router_exp[0].reshape(d, N_EXPERTS),
                          jnp.zeros((d, LANES - ROUTER_ROWS), F32)], axis=1)
    br = jnp.concatenate([b_router_grp[0], b_router_exp[0].reshape(N_EXPERTS),
                          jnp.zeros((LANES - ROUTER_ROWS,), F32)]).reshape(1, LANES)
    wr_hi, wr_lo = _split_bf16(wr)

    w = dict(
        g_norm1=g_norm1[0].reshape(1, d), wmain=wmain, bmain=bmain, colscale=colscale,
        wg_hi=wg_hi, wg_lo=wg_lo, bg=bg2,
        gn3=gn_gain[0].reshape(N_HEADS, 1, HEAD_DIM),
        wpg=w_pool_grp[0].astype(BF16), pscale=pool_scale[0].reshape(N_GROUPS, 1, LANES),
        wa=w_proj_a[0].astype(BF16), wb=w_proj_b[0].astype(BF16), wo=w_out[0].astype(BF16),
        g_norm2=g_norm2[0].reshape(1, d), wr_hi=wr_hi, wr_lo=wr_lo, br=br,
        wg=w_exp_gate[0], wu=w_exp_up[0], wd=w_exp_down[0],
        g_final=g_final.reshape(1, d),
    )

    y_prompt, (_, cst, nst), cs_prompt = _stream(x_prompt, mod3, lambda r: 0, tp, None, None, w)
    n_rep = jnp.broadcast_to(state_n[..., None], state_C.shape)
    m0 = jnp.pad(state_m[:, 0].reshape(bs, N_HD), ((0, 0), (0, LANES - N_HD))).reshape(bs, 1, LANES)
    y_sample, _, _ = _stream(x_sample, mod3, lambda r: 1 + r // ts, ts, ts // GRID_W, (state_C, n_rep, m0), w)

    new_c = cst
    new_n = nst.reshape(bp, 1, 2, N_HEADS, HEAD_DIM)
    new_m = cs_prompt[:, 1, :N_HD].reshape(bp, 1, 2, N_HEADS)
    return (y_prompt, y_sample, new_c, new_n, new_m)
```

```python
import functools

import numpy as np
import jax
import jax.numpy as jnp
from jax import lax
from jax.experimental import pallas as pl
from jax.experimental.pallas import tpu as pltpu
from jax.experimental.pallas import tpu_sc as plsc

F32 = jnp.float32
BF16 = jnp.bfloat16

D_MODEL = 1024
N_HEADS = 8
HEAD_DIM = 128
GRID_W = 64
POOL_WINDOWS = (2, 4, 8, 16)
N_GROUPS = 4
EXPERTS_PER_GROUP = 4
N_EXPERTS = 16
D_EXPERT = 512
EPS = 1e-6
NEG = -1e30

LANES = 128
CHUNK = 256
N_SLABS = 52
SLAB_Q, SLAB_K, SLAB_V, SLAB_O, SLAB_GA, SLAB_GB, SLAB_POOL = 0, 8, 16, 24, 32, 40, 48
D_MAIN = N_SLABS * LANES
VMEM_LIMIT = 56 * 1024 * 1024


def _cparams(sem):
    return pltpu.CompilerParams(dimension_semantics=sem, vmem_limit_bytes=VMEM_LIMIT)


def _const_spec(shape):
    nd = len(shape)
    return pl.BlockSpec(shape, lambda *_: (0,) * nd, pipeline_mode=pl.Buffered(1))


def _split_bf16(x):
    hi = x.astype(BF16)
    lo = (x - hi.astype(F32)).astype(BF16)
    return hi, lo


def _dot(a, b):
    return jnp.dot(a, b, preferred_element_type=F32)


def _dot3(a, b_hi, b_lo):
    a_hi, a_lo = _split_bf16(a)
    return _dot(a_hi, b_hi) + _dot(a_hi, b_lo) + _dot(a_lo, b_hi)


def _rms_scale(x):
    return lax.rsqrt(jnp.mean(x * x, axis=-1, keepdims=True) + EPS)


def _log_sigmoid(x):
    return jnp.minimum(x, 0.0) - jnp.log1p(jnp.exp(-jnp.abs(x)))


def _mod_kernel(c_ref, w_ref, b_ref, o_ref):
    c = c_ref[...]
    s = c * jax.nn.sigmoid(c)
    w_hi, w_lo = _split_bf16(w_ref[...])
    o_ref[...] = _dot3(s, w_hi, w_lo) + b_ref[...]


def _modulation(cvec, w_mod, b_mod):
    tn = 2048
    out = pl.pallas_call(
        _mod_kernel,
        out_shape=jax.ShapeDtypeStruct((8, 6 * D_MODEL), F32),
        grid=(6 * D_MODEL // tn,),
        in_specs=[
            pl.BlockSpec((8, D_MODEL), lambda j: (0, 0)),
            pl.BlockSpec((D_MODEL, tn), lambda j: (0, j)),
            pl.BlockSpec((1, tn), lambda j: (0, j)),
        ],
        out_specs=pl.BlockSpec((8, tn), lambda j: (0, j)),
        compiler_params=_cparams(("parallel",)),
        name="modulation",
    )(cvec, w_mod, b_mod.reshape(1, -1))
    return out.reshape(8, 6, D_MODEL)


PACK_COLS = 512


def _pack_kernel(wt_ref, o_ref):
    o_ref[...] = wt_ref[...].T.astype(BF16)


def _pack_columns(w, order):
    k, n = w.shape
    starts = []
    for src, width in order:
        assert width % PACK_COLS == 0
        starts += [src + j for j in range(0, width, PACK_COLS)]
    n_out = len(starts) * PACK_COLS

    unit = 32
    assert all(s % unit == 0 for s in starts)

    def src_row(j):
        row = jnp.int32(starts[-1] // unit)
        for idx in range(len(starts) - 2, -1, -1):
            row = jnp.where(j == idx, starts[idx] // unit, row)
        return row * unit

    return pl.pallas_call(
        _pack_kernel,
        out_shape=jax.ShapeDtypeStruct((k, n_out), BF16),
        grid=(len(starts),),
        in_specs=[pl.BlockSpec((pl.Element(PACK_COLS), pl.Element(k)), lambda j: (src_row(j), 0))],
        out_specs=pl.BlockSpec((k, PACK_COLS), lambda j: (0, j)),
        compiler_params=_cparams(("parallel",)),
        name="pack_columns",
    )(w.T)


def _copy_kernel(w_ref, o_ref):
    o_ref[...] = w_ref[...]


def _take_columns_t(w, start, width):
    k = w.shape[0]
    return pl.pallas_call(
        _copy_kernel,
        out_shape=jax.ShapeDtypeStruct((width, k), w.dtype),
        grid=(1,),
        in_specs=[pl.BlockSpec((pl.Element(width), pl.Element(k)), lambda j: (start, 0))],
        out_specs=pl.BlockSpec((width, k), lambda j: (0, 0)),
        name="take_columns",
    )(w.T)


IN_TM = 512
IN_TN = 512
LOG2E = 1.4426950408889634
N_HD = 2 * N_HEADS


def _fwd_lanes(shape):
    lane = lax.broadcasted_iota(jnp.int32, shape, len(shape) - 1)
    return (lane & N_HEADS) == 0


def _inproj_kernel(x_ref, mod_ref, g_ref, w_ref, b_ref, cs_ref, wg_hi_ref, wg_lo_ref, bg_ref,
                   p3_ref, bcum_ref, imb_ref, amax_ref):
    x = x_ref[...]
    h = x * _rms_scale(x) * g_ref[...]
    h = h * (1.0 + mod_ref[0, 1:2, :]) + mod_ref[0, 0:1, :]
    hb = h.astype(BF16)
    for n in range(D_MAIN // IN_TN):
        sl = slice(n * IN_TN, (n + 1) * IN_TN)
        acc = (_dot(hb, w_ref[:, sl]) + b_ref[:, sl]) * cs_ref[:, sl]
        accb = acc.astype(BF16)
        for j in range(IN_TN // LANES):
            p3_ref[n * (IN_TN // LANES) + j] = accb[:, j * LANES:(j + 1) * LANES]

    g2 = _dot3(h, wg_hi_ref[...], wg_lo_ref[...]) + bg_ref[...]
    ls = _log_sigmoid(g2[:, :LANES])
    gi = g2[:, LANES:]
    row = lax.broadcasted_iota(jnp.int32, (CHUNK, LANES), 0)
    fwd = _fwd_lanes((CHUNK, LANES))
    for c in range(IN_TM // CHUNK):
        rows = slice(c * CHUNK, (c + 1) * CHUNK)
        cf = ls[rows]
        cb = cf
        s = 1
        while s < CHUNK:
            cf = cf + jnp.where(row >= s, pltpu.roll(cf, s, 0), 0.0)
            cb = cb + jnp.where(row < CHUNK - s, pltpu.roll(cb, CHUNK - s, 0), 0.0)
            s *= 2
        b = jnp.where(fwd, cf, cb)
        imb = gi[rows] - b
        mf = imb
        mb = imb
        s = 1
        while s < CHUNK:
            mf = jnp.maximum(mf, jnp.where(row >= s, pltpu.roll(mf, s, 0), -jnp.inf))
            mb = jnp.maximum(mb, jnp.where(row < CHUNK - s, pltpu.roll(mb, CHUNK - s, 0), -jnp.inf))
            s *= 2
        bcum_ref[rows, :] = b
        imb_ref[rows, :] = imb
        amax_ref[rows, :] = b + jnp.where(fwd, mf, mb)


def _in_projection(x2d, mod3, mod_row_fn, g_norm1, wmain, bmain, colscale, wg_hi, wg_lo, bg):
    n = x2d.shape[0]
    return pl.pallas_call(
        _inproj_kernel,
        out_shape=(jax.ShapeDtypeStruct((N_SLABS, n, LANES), BF16),) + (jax.ShapeDtypeStruct((n, LANES), F32),) * 3,
        grid=(n // IN_TM,),
        in_specs=[
            pl.BlockSpec((IN_TM, D_MODEL), lambda i: (i, 0)),
            pl.BlockSpec((1, 6, D_MODEL), lambda i: (mod_row_fn(i * IN_TM), 0, 0)),
            _const_spec((1, D_MODEL)),
            _const_spec((D_MODEL, D_MAIN)),
            _const_spec((1, D_MAIN)),
            _const_spec((1, D_MAIN)),
            _const_spec((D_MODEL, 2 * LANES)),
            _const_spec((D_MODEL, 2 * LANES)),
            _const_spec((1, 2 * LANES)),
        ],
        out_specs=(pl.BlockSpec((N_SLABS, IN_TM, LANES), lambda i: (0, i, 0)),)
        + (pl.BlockSpec((IN_TM, LANES), lambda i: (i, 0)),) * 3,
        compiler_params=_cparams(("parallel",)),
        name="in_projection",
    )(x2d, mod3, g_norm1, wmain, bmain, colscale, wg_hi, wg_lo, bg)


GP_ROWS = 1024
GP_NCH = GP_ROWS // CHUNK
Q_C2, Q_EM, Q_WI, Q_WC = 0, 1, 2, 3


def _gate_prep_kernel(*refs, carry):
    if carry:
        b_ref, imb_ref, a_ref, m0_ref, cols_ref, rw_ref, cs_ref = refs
    else:
        b_ref, imb_ref, a_ref, cols_ref, rw_ref, cs_ref = refs
    fwd1 = _fwd_lanes((1, LANES))
    lane = lax.broadcasted_iota(jnp.int32, (CHUNK, LANES), 1)
    neg = jnp.full((1, LANES), NEG, F32)

    def at_scan_end(ref, c):
        return jnp.where(fwd1, ref[(c + 1) * CHUNK - 1:(c + 1) * CHUNK, :], ref[c * CHUNK:c * CHUNK + 1, :])

    b_last = [at_scan_end(b_ref, c) for c in range(GP_NCH)]
    a_last = [at_scan_end(a_ref, c) for c in range(GP_NCH)]
    m_start = m0_ref[0] if carry else neg
    m_prev = [[None] * GP_NCH, [None] * GP_NCH]
    m_new = [[None] * GP_NCH, [None] * GP_NCH]
    for d, order in enumerate((range(GP_NCH), range(GP_NCH - 1, -1, -1))):
        m = m_start
        for c in order:
            m_prev[d][c] = m
            m_new[d][c] = jnp.maximum(b_last[c] + m, a_last[c])
            m = m_new[d][c] if carry else neg

    row_src = []
    for c in range(GP_NCH):
        rows = slice(c * CHUNK, (c + 1) * CHUNK)
        mp = jnp.where(fwd1, m_prev[0][c], m_prev[1][c])
        mn = jnp.where(fwd1, m_new[0][c], m_new[1][c])
        b = b_ref[rows, :]
        imb = imb_ref[rows, :]
        inter = b + mp
        m_t = jnp.maximum(inter, a_ref[rows, :])
        c2 = (b - m_t) * LOG2E
        em = jnp.exp(-m_t)
        wc = jnp.exp(b_last[c] - mn + imb)
        packed = jnp.where(lane < N_HD, c2, pltpu.roll(em, N_HD * Q_EM, 1))
        if carry:
            wi = jnp.exp(inter - m_t)
            packed = jnp.where(lane < N_HD * Q_WI, packed, pltpu.roll(wi, N_HD * Q_WI, 1))
        cols_ref[rows, :] = jnp.where(lane < N_HD * Q_WC, packed, pltpu.roll(wc, N_HD * Q_WC, 1))
        row_src.append(jnp.where(lane < N_HD, imb * (-LOG2E), pltpu.roll(wc, N_HD, 1)))
        cs_ref[c, 0:1, :] = jnp.exp(b_last[c] + mp - mn)
        cs_ref[c, 1:2, :] = mn

    rows_t = jnp.concatenate(row_src, axis=0).T
    for r in range(2 * N_HD):
        rw_ref[r] = rows_t[r:r + 1]


def _gate_prep(bcum, imb, amax, m0=None):
    n = bcum.shape[0]
    carry = m0 is not None
    blk = pl.BlockSpec((GP_ROWS, LANES), lambda i: (i, 0))
    in_specs = [blk, blk, blk]
    args = [bcum, imb, amax]
    if carry:
        in_specs.append(pl.BlockSpec((1, 1, LANES), lambda i: (i, 0, 0)))
        args.append(m0)
    return pl.pallas_call(
        functools.partial(_gate_prep_kernel, carry=carry),
        out_shape=(jax.ShapeDtypeStruct((n, LANES), F32), jax.ShapeDtypeStruct((2 * N_HD, 1, n), F32),
                   jax.ShapeDtypeStruct((n // CHUNK, 2, LANES), F32)),
        grid=(n // GP_ROWS,),
        in_specs=in_specs,
        out_specs=(blk, pl.BlockSpec((2 * N_HD, 1, GP_ROWS), lambda i: (0, 0, i)),
                   pl.BlockSpec((GP_NCH, 2, LANES), lambda i: (i, 0, 0))),
        compiler_params=_cparams(("parallel",)),
        name="gate_prep",
    )(*args)


ML_ROWS = 1024
ML_ROWS_FREE = 2048


def _pick_col(blk, lane_iota, idx):
    return jnp.sum(jnp.where(lane_iota == idx, blk, 0.0), axis=1, keepdims=True)


def _mlstm_kernel(*refs, carry, n_chunks):
    if carry:
        (q_ref, k_ref, v_ref, o_ref, cols_ref, r2f_ref, r2b_ref, gn_ref, cs_ref, c0_ref, n0_ref,
         ya_ref, s_scr, h_scr) = refs
    else:
        (q_ref, k_ref, v_ref, o_ref, cols_ref, r2f_ref, r2b_ref, gn_ref, wrf_ref, wrb_ref,
         ya_ref, cst_ref, nst_ref, s_scr, h_scr) = refs
    head = pl.program_id(1)
    lane = lax.broadcasted_iota(jnp.int32, (CHUNK, LANES), 1)
    lane1 = lax.broadcasted_iota(jnp.int32, (1, LANES), 1)
    r_idx = lax.broadcasted_iota(jnp.int32, (CHUNK, CHUNK), 0)
    c_idx = lax.broadcasted_iota(jnp.int32, (CHUNK, CHUNK), 1)
    ones = jnp.ones((CHUNK, HEAD_DIM), BF16)
    tn_dims = (((0,), (0,)), ((), ()))

    for d in range(2):
        mask = (c_idx <= r_idx) if d == 0 else (c_idx >= r_idx)
        r2_ref = r2f_ref if d == 0 else r2b_ref
        base = N_HEADS * d + head
        if carry:
            cn = jnp.concatenate([c0_ref[0, 0, d, 0], n0_ref[0, 0, d, 0]], axis=1)
        order = range(n_chunks) if d == 0 else range(n_chunks - 1, -1, -1)
        for step, c in enumerate(order):
            rows = slice(c * CHUNK, (c + 1) * CHUNK)
            q = q_ref[0, rows, :]
            k = k_ref[0, rows, :]
            v = v_ref[0, rows, :]
            v1 = jnp.concatenate([v, ones], axis=1)
            if d == 0:
                s = lax.dot_general(q, k, (((1,), (1,)), ((), ())), preferred_element_type=F32)
                s_scr[rows, :] = s
            else:
                s = s_scr[rows, :]
            blk = cols_ref[rows, :]
            arg = _pick_col(blk, lane, base + N_HD * Q_C2) - r2_ref[0, :, rows]
            sw = s * jnp.exp2(jnp.where(mask, arg, NEG))
            nd = _dot(sw.astype(BF16), v1)
            if carry:
                nd = nd + _pick_col(blk, lane, base + N_HD * Q_WI) * _dot(q, cn.astype(BF16))
            hdir = nd[:, :HEAD_DIM] / jnp.maximum(jnp.abs(nd[:, HEAD_DIM:]), _pick_col(blk, lane, base + N_HD * Q_EM))
            if d == 0:
                h_scr[rows, :] = hdir
            else:
                h_scr[rows, :] = h_scr[rows, :] + hdir

            if carry and step == n_chunks - 1:
                continue
            w_col = _pick_col(blk, lane, base + N_HD * Q_WC)
            if carry:
                u = lax.dot_general(k, (w_col * v1.astype(F32)).astype(BF16), tn_dims,
                                    preferred_element_type=F32)
                decay = jnp.sum(jnp.where(lane1 == base, cs_ref[c, 0:1, :], 0.0), axis=1, keepdims=True)
                cn = decay * cn + u
            else:
                w_row = (wrf_ref if d == 0 else wrb_ref)[0, :, rows]
                w8 = jnp.broadcast_to(w_row, (8, CHUNK)).astype(BF16)
                cst_ref[c, 0, d, 0] = lax.dot_general(k, (w_col * v.astype(F32)).astype(BF16), tn_dims,
                                                      preferred_element_type=F32)
                nst_ref[c, d, 0] = _dot(w8, k)[0:1, :]

    hh = h_scr[...]
    hn = hh * _rms_scale(hh) * gn_ref[0]
    ya_ref[0] = (hn * jax.nn.sigmoid(o_ref[0].astype(F32))).astype(BF16)


def _mlstm(p3, cols, rows, cs, gn3, state=None):
    n = p3.shape[1]
    carry = state is not None
    rows_step = ML_ROWS if carry else ML_ROWS_FREE
    n_chunks = rows_step // CHUNK
    nblk = n // rows_step

    def slab(base):
        return pl.BlockSpec((1, rows_step, LANES), lambda g, h: (base + h, g, 0))

    def row(base):
        return pl.BlockSpec((1, 1, rows_step), lambda g, h: (base + h, 0, g))

    in_specs = [
        slab(SLAB_Q), slab(SLAB_K), slab(SLAB_V), slab(SLAB_O),
        pl.BlockSpec((rows_step, LANES), lambda g, h: (g, 0)),
        row(0), row(N_HEADS),
        pl.BlockSpec((1, 1, LANES), lambda g, h: (h, 0, 0)),
    ]
    args = [p3, p3, p3, p3, cols, rows, rows, gn3]
    ya_shape = jax.ShapeDtypeStruct((N_HEADS, n, LANES), BF16)
    ya_spec = pl.BlockSpec((1, rows_step, LANES), lambda g, h: (h, g, 0))
    state_spec = pl.BlockSpec((1, 1, 2, 1, HEAD_DIM, HEAD_DIM), lambda g, h: (g, 0, 0, h, 0, 0))
    if carry:
        c0, n0 = state
        in_specs += [pl.BlockSpec((n_chunks, 2, LANES), lambda g, h: (g, 0, 0)), state_spec, state_spec]
        args += [cs, c0, n0]
        out_shape = ya_shape
        out_specs = ya_spec
    else:
        in_specs += [row(N_HD), row(N_HD + N_HEADS)]
        args += [rows, rows]
        nseq = n // CHUNK
        out_shape = (ya_shape,
                     jax.ShapeDtypeStruct((nseq, 1, 2, N_HEADS, HEAD_DIM, HEAD_DIM), F32),
                     jax.ShapeDtypeStruct((nseq, 2, N_HEADS, 1, HEAD_DIM), F32))
        out_specs = (ya_spec,
                     pl.BlockSpec((n_chunks, 1, 2, 1, HEAD_DIM, HEAD_DIM), lambda g, h: (g, 0, 0, h, 0, 0)),
                     pl.BlockSpec((n_chunks, 2, 1, 1, HEAD_DIM), lambda g, h: (g, 0, h, 0, 0)))
    return pl.pallas_call(
        functools.partial(_mlstm_kernel, carry=carry, n_chunks=n_chunks),
        out_shape=out_shape,
        grid=(nblk, N_HEADS),
        in_specs=in_specs,
        out_specs=out_specs,
        scratch_shapes=[pltpu.VMEM((rows_step, CHUNK), F32), pltpu.VMEM((rows_step, HEAD_DIM), F32)],
        compiler_params=_cparams(("parallel", "parallel")),
        name="mlstm_latent" if carry else "mlstm_prompt",
    )(*args)


def _pool_kernel(u_ref, a_ref, ic_ref, wpg_ref, ps_ref, yb_ref, *, seq_len, n_seq, rows_2d):
    piece = seq_len if rows_2d is None else GRID_W
    n_piece = n_seq if rows_2d is None else rows_2d
    for g in range(N_GROUPS):
        u = jnp.concatenate([u_ref[g, i * piece:(i + 1) * piece, :] for i in range(n_piece)], axis=1)
        win = _dot(a_ref[g], u)
        if rows_2d is None:
            inv_cnt = jnp.concatenate([ic_ref[g]] * n_piece, axis=1)
        else:
            run = [jnp.zeros((piece, LANES), F32)]
            for r in range(rows_2d):
                run.append(run[-1] + win[:, r * LANES:(r + 1) * LANES])
            half = POOL_WINDOWS[g] // 2
            win = jnp.concatenate([run[min(r - half + POOL_WINDOWS[g], rows_2d)] - run[max(r - half, 0)]
                                   for r in range(rows_2d)], axis=1)
            inv_cnt = jnp.concatenate([ic_ref[g, r * piece:(r + 1) * piece, :] for r in range(rows_2d)], axis=1)
        p = (win * inv_cnt - u.astype(F32)).astype(BF16)
        p = jnp.concatenate([p[:, i * LANES:(i + 1) * LANES] for i in range(n_piece)], axis=0)
        yb_ref[g] = (_dot(p, wpg_ref[g]) * ps_ref[g]).astype(BF16)


def _pool(p3, amat, invc, wpg, pscale, seq_len, n_seq, rows_2d):
    n = p3.shape[1]
    rows = seq_len * n_seq
    return pl.pallas_call(
        functools.partial(_pool_kernel, seq_len=seq_len, n_seq=n_seq, rows_2d=rows_2d),
        out_shape=jax.ShapeDtypeStruct((N_GROUPS, n, LANES), BF16),
        grid=(n // rows,),
        in_specs=[
            pl.BlockSpec((N_GROUPS, rows, LANES), lambda i: (SLAB_POOL // N_GROUPS, i, 0)),
            _const_spec(amat.shape),
            _const_spec((N_GROUPS, seq_len, LANES)),
            _const_spec((N_GROUPS, LANES, LANES)),
            _const_spec((N_GROUPS, 1, LANES)),
        ],
        out_specs=pl.BlockSpec((N_GROUPS, rows, LANES), lambda i: (0, i, 0)),
        compiler_params=_cparams(("parallel",)),
        name="pool",
    )(p3, amat, invc, wpg, pscale)


def _window_matrix(size, w):
    idx = np.arange(size)
    lo = np.clip(idx - w // 2, 0, size)
    hi = np.clip(idx - w // 2 + w, 0, size)
    s = np.arange(size)[None, :]
    a = ((s >= lo[:, None]) & (s < hi[:, None])).astype(np.float32)
    return a, (hi - lo).astype(np.float32)


def _pool_constants(seq_len, rows_2d):
    mats, invs = [], []
    for w in POOL_WINDOWS:
        if rows_2d is None:
            a, cnt = _window_matrix(seq_len, w)
        else:
            _, cr = _window_matrix(rows_2d, w)
            a, cc = _window_matrix(GRID_W, w)
            cnt = np.kron(cr, cc)
        mats.append(a)
        invs.append(np.repeat((1.0 / cnt)[:, None], LANES, axis=1))
    return (jnp.asarray(np.stack(mats), dtype=BF16), jnp.asarray(np.stack(invs), dtype=F32))


MIX_TM = 512


def _mix_kernel(x_ref, mod_ref, ya_ref, yb_ref, ga_ref, gb_ref, wa_ref, wb_ref, wo_ref,
                g2_ref, wr_hi_ref, wr_lo_ref, br_ref, x1_ref, h2_ref, info_ref):
    ya = jnp.concatenate([ya_ref[i] for i in range(N_HEADS)], axis=1)
    yb = jnp.concatenate([yb_ref[i] for i in range(N_GROUPS)], axis=1)
    ga = jnp.concatenate([ga_ref[i] for i in range(8)], axis=1).astype(F32)
    gb = jnp.concatenate([gb_ref[i] for i in range(8)], axis=1).astype(F32)
    merged = jax.nn.sigmoid(ga) * _dot(ya, wa_ref[...]) + jax.nn.sigmoid(gb) * _dot(yb, wb_ref[...])
    out = _dot(merged.astype(BF16), wo_ref[...])
    x1 = x_ref[...] + mod_ref[0, 2:3, :] * out
    x1_ref[...] = x1

    h2 = x1 * _rms_scale(x1) * g2_ref[...]
    h2 = h2 * (1.0 + mod_ref[0, 4:5, :]) + mod_ref[0, 3:4, :]
    h_hi, h_lo = _split_bf16(h2)
    packed = _pack_bf16_pairs(h_hi.astype(F32))
    h2_ref[0:MIX_TM, :] = packed[:, :SC_ROW]
    h2_ref[MIX_TM:2 * MIX_TM, :] = packed[:, SC_ROW:]
    wr_hi = wr_hi_ref[...]
    logits = _dot(h_hi, wr_hi) + _dot(h_lo, wr_hi) + _dot(h_hi, wr_lo_ref[...]) + br_ref[...]
    first, second, w1, w2 = _route_rows(logits.T)
    zero = jnp.zeros_like(w1)
    info_ref[...] = jnp.concatenate([first, second, w1, w2, zero, zero, zero, zero], axis=0)


def _mix(x2d, mod3, mod_row_fn, ya, yb, p3, wa, wb, wo, g_norm2, wr_hi, wr_lo, br):
    n = x2d.shape[0]
    return pl.pallas_call(
        _mix_kernel,
        out_shape=(jax.ShapeDtypeStruct((n, D_MODEL), F32),
                   jax.ShapeDtypeStruct((2 * n, SC_ROW), jnp.int32), jax.ShapeDtypeStruct((8, n), F32)),
        grid=(n // MIX_TM,),
        in_specs=[
            pl.BlockSpec((MIX_TM, D_MODEL), lambda i: (i, 0)),
            pl.BlockSpec((1, 6, D_MODEL), lambda i: (mod_row_fn(i * MIX_TM), 0, 0)),
            pl.BlockSpec((N_HEADS, MIX_TM, LANES), lambda i: (0, i, 0)),
            pl.BlockSpec((N_GROUPS, MIX_TM, LANES), lambda i: (0, i, 0)),
            pl.BlockSpec((8, MIX_TM, LANES), lambda i: (SLAB_GA // 8, i, 0)),
            pl.BlockSpec((8, MIX_TM, LANES), lambda i: (SLAB_GB // 8, i, 0)),
            _const_spec((D_MODEL, D_MODEL)),
            _const_spec((N_GROUPS * LANES, D_MODEL)),
            _const_spec((D_MODEL, D_MODEL)),
            _const_spec((1, D_MODEL)),
            _const_spec((D_MODEL, LANES)),
            _const_spec((D_MODEL, LANES)),
            _const_spec((1, LANES)),
        ],
        out_specs=(pl.BlockSpec((MIX_TM, D_MODEL), lambda i: (i, 0)),
                   pl.BlockSpec((2 * MIX_TM, SC_ROW), lambda i: (i, 0)),
                   pl.BlockSpec((8, MIX_TM), lambda i: (0, i))),
        compiler_params=_cparams(("parallel",)),
        name="mix",
    )(x2d, mod3, ya, yb, p3, p3, wa, wb, wo, g_norm2, wr_hi, wr_lo, br)


PLAN_TB = 1024
ROUTER_ROWS = 4 + N_EXPERTS
SLOT_TM = 1024
SC_ROW = 256
SC_WIN = 128
HI_MASK = -65536


def _pack_bf16_pairs(x):
    bits = pltpu.bitcast(x, jnp.int32)
    half = D_MODEL // 2
    return jnp.bitwise_or(jnp.bitwise_and(lax.shift_right_logical(bits[:, :half], 16), 0xFFFF),
                          jnp.bitwise_and(bits[:, half:], HI_MASK))


def _unpack_bf16_pairs(lo_words, hi_words):
    def low(w):
        return pltpu.bitcast(lax.shift_left(w, 16), F32)

    def high(w):
        return pltpu.bitcast(jnp.bitwise_and(w, HI_MASK), F32)

    return jnp.concatenate([low(lo_words), low(hi_words), high(lo_words), high(hi_words)], axis=1).astype(BF16)


def _route_rows(lt):
    lg = [lt[i:i + 1] for i in range(N_GROUPS)]
    mx = jnp.maximum(jnp.maximum(lg[0], lg[1]), jnp.maximum(lg[2], lg[3]))
    p_top = 1.0 / (jnp.exp(lg[0] - mx) + jnp.exp(lg[1] - mx) + jnp.exp(lg[2] - mx) + jnp.exp(lg[3] - mx))
    gidx = jnp.zeros(lg[0].shape, jnp.int32)
    best = lg[0]
    for i in range(1, N_GROUPS):
        upd = lg[i] > best
        gidx = jnp.where(upd, i, gidx)
        best = jnp.where(upd, lg[i], best)

    def expert_row(g, e):
        r = N_GROUPS + g * EXPERTS_PER_GROUP + e
        return lt[r:r + 1]

    le = [jnp.where(gidx == 0, expert_row(0, e),
                    jnp.where(gidx == 1, expert_row(1, e),
                              jnp.where(gidx == 2, expert_row(2, e), expert_row(3, e))))
          for e in range(EXPERTS_PER_GROUP)]
    v1 = jnp.maximum(jnp.maximum(le[0], le[1]), jnp.maximum(le[2], le[3]))
    i1 = jnp.full(v1.shape, EXPERTS_PER_GROUP - 1, jnp.int32)
    for e in range(EXPERTS_PER_GROUP - 2, -1, -1):
        i1 = jnp.where(le[e] == v1, e, i1)
    le2 = [jnp.where(i1 == e, -jnp.inf, le[e]) for e in range(EXPERTS_PER_GROUP)]
    v2 = jnp.maximum(jnp.maximum(le2[0], le2[1]), jnp.maximum(le2[2], le2[3]))
    i2 = jnp.full(v2.shape, EXPERTS_PER_GROUP - 1, jnp.int32)
    for e in range(EXPERTS_PER_GROUP - 2, -1, -1):
        i2 = jnp.where(le2[e] == v2, e, i2)
    e2 = jnp.exp(v2 - v1)
    first = (gidx * EXPERTS_PER_GROUP + i1).astype(F32)
    second = (gidx * EXPERTS_PER_GROUP + i2).astype(F32)
    return first, second, p_top / (1.0 + e2), p_top * e2 / (1.0 + e2)


def _plan_kernel(info_ref, before_ref, sidx_ref, gidx_ref, tmap_ref, rank_scr, count_scr, *, n_blocks):
    step = pl.program_id(0)
    expert_id = lax.broadcasted_iota(jnp.int32, (N_EXPERTS, PLAN_TB), 0).astype(F32)

    @pl.when(step == 0)
    def _():
        count_scr[...] = jnp.zeros_like(count_scr)

    @pl.when(step < n_blocks)
    def _():
        cols = pl.ds(pl.multiple_of(step * PLAN_TB, PLAN_TB), PLAN_TB)
        sel = jnp.where((info_ref[0:1, cols] == expert_id) | (info_ref[1:2, cols] == expert_id), 1.0, 0.0)
        selb = sel.astype(BF16)
        rank_scr[:, cols] = _dot(selb, before_ref[...]) + count_scr[:, 0:1]
        count_scr[...] += _dot(selb, jnp.ones((PLAN_TB, LANES), BF16))

    @pl.when(step == n_blocks)
    def _():
        counts = count_scr[...]
        padded = jnp.floor((counts + (SLOT_TM - 1)) * (1.0 / SLOT_TM)) * SLOT_TM
        starts, ends, run = [], [], jnp.zeros((1, LANES), F32)
        for e in range(N_EXPERTS):
            starts.append(run)
            run = run + padded[e:e + 1]
            ends.append(run)
        n = info_ref.shape[1]
        shift = SLOT_TM.bit_length() - 1
        for k in range(2):
            chosen = info_ref[k:k + 1, :]
            slot = jnp.zeros((1, n), F32)
            for e in range(N_EXPERTS):
                slot = jnp.where(chosen == float(e), starts[e][:, 0:1] + rank_scr[e:e + 1, :], slot)
            slot = slot.astype(jnp.int32)
            low = jnp.left_shift(jnp.right_shift(slot, shift), shift + 1) + jnp.bitwise_and(slot, SLOT_TM - 1)
            for half in range(2):
                rows = low + half * SLOT_TM
                for t in range(n // MIX_TM):
                    sidx_ref[k, t, half:half + 1, :] = rows[:, t * MIX_TM:(t + 1) * MIX_TM]
                for t in range(n // FIN_TM):
                    gidx_ref[t, k, half:half + 1, :] = rows[:, t * FIN_TM:(t + 1) * FIN_TM]
        tile_start = lax.broadcasted_iota(jnp.int32, (1, LANES), 1).astype(F32) * SLOT_TM
        used = jnp.where(tile_start < run, 1.0, 0.0)
        tile_start = jnp.minimum(tile_start, run - SLOT_TM)
        owner = jnp.zeros((1, LANES), F32)
        for e in range(N_EXPERTS - 1):
            owner = owner + jnp.where(ends[e] <= tile_start, 1.0, 0.0)
        tile = tile_start * (1.0 / SLOT_TM)
        tmap_ref[...] = jnp.concatenate([owner, used, tile] + [jnp.zeros((1, LANES), F32)] * 5,
                                        axis=0).astype(jnp.int32)


def _plan(info):
    n = info.shape[1]
    n_blocks = n // PLAN_TB
    t_i = lax.broadcasted_iota(jnp.int32, (PLAN_TB, PLAN_TB), 0)
    t_j = lax.broadcasted_iota(jnp.int32, (PLAN_TB, PLAN_TB), 1)
    before = (t_i < t_j).astype(BF16)
    return pl.pallas_call(
        functools.partial(_plan_kernel, n_blocks=n_blocks),
        out_shape=(jax.ShapeDtypeStruct((2, n // MIX_TM, 2, MIX_TM), jnp.int32),
                   jax.ShapeDtypeStruct((n // FIN_TM, 2, 2, FIN_TM), jnp.int32),
                   jax.ShapeDtypeStruct((8, LANES), jnp.int32)),
        grid=(n_blocks + 1,),
        in_specs=[_const_spec((8, n)), _const_spec((PLAN_TB, PLAN_TB))],
        out_specs=(pl.BlockSpec((2, n // MIX_TM, 2, MIX_TM), lambda i: (0, 0, 0, 0)),
                   pl.BlockSpec((n // FIN_TM, 2, 2, FIN_TM), lambda i: (0, 0, 0, 0)),
                   pl.BlockSpec((8, LANES), lambda i: (0, 0))),
        scratch_shapes=[pltpu.VMEM((N_EXPERTS, n), F32), pltpu.VMEM((N_EXPERTS, LANES), F32)],
        compiler_params=_cparams(("arbitrary",)),
        name="plan",
    )(info, before)


def _sc_mesh():
    return plsc.VectorSubcoreMesh(core_axis_name="core", subcore_axis_name="subcore")


def _sc_scatter_rows(x, idx, n_out):
    n_idx = idx.shape[0]
    src_blocks = x.shape[0] // SC_WIN

    @pl.kernel(out_type=jax.ShapeDtypeStruct((n_out, SC_ROW), x.dtype), mesh=_sc_mesh(), scratch_types=[])
    def scatter_kernel(x_hbm, i_hbm, o_hbm):
        def body(x_vmem, i_vmem):
            pltpu.sync_copy(x_vmem, o_hbm.at[i_vmem.at[0]])

        pltpu.emit_pipeline(
            body,
            grid=(n_idx // SC_WIN,),
            in_specs=[pl.BlockSpec((SC_WIN, SC_ROW), index_map=lambda i: (i % src_blocks, 0)),
                      pl.BlockSpec((1, SC_WIN), index_map=lambda i: (0, i))],
            out_specs=[],
            core_axis_name=("core", "subcore"),
            dimension_semantics=(pltpu.PARALLEL,),
        )(x_hbm, i_hbm)

    return scatter_kernel(x, idx.reshape(1, n_idx))


def _sc_gather_rows(x, idx):
    n_idx = idx.shape[0]

    @pl.kernel(out_type=jax.ShapeDtypeStruct((n_idx, SC_ROW), x.dtype), mesh=_sc_mesh(), scratch_types=[])
    def gather_kernel(x_hbm, i_hbm, o_hbm):
        def body(i_vmem, o_vmem):
            pltpu.sync_copy(x_hbm.at[i_vmem.at[0]], o_vmem)

        pltpu.emit_pipeline(
            body,
            grid=(n_idx // SC_WIN,),
            in_specs=[pl.BlockSpec((1, SC_WIN), index_map=lambda i: (0, i))],
            out_specs=[pl.BlockSpec((SC_WIN, SC_ROW), index_map=lambda i: (i, 0))],
            core_axis_name=("core", "subcore"),
            dimension_semantics=(pltpu.PARALLEL,),
        )(i_hbm, o_hbm)

    return gather_kernel(x, idx.reshape(1, n_idx))


def _experts_kernel(tmap_ref, xs_ref, wg_ref, wu_ref, wd_ref, ys_ref, wg_scr, wu_scr, wd_scr):
    j = pl.program_id(0)

    @pl.when((j == 0) | (tmap_ref[0, j] != tmap_ref[0, jnp.maximum(j - 1, 0)]))
    def _():
        wg_scr[...] = wg_ref[0].astype(BF16)
        wu_scr[...] = wu_ref[0].astype(BF16)
        wd_scr[...] = wd_ref[0].astype(BF16)

    @pl.when(tmap_ref[1, j] > 0)
    def _():
        x = _unpack_bf16_pairs(xs_ref[0:SLOT_TM, :], xs_ref[SLOT_TM:2 * SLOT_TM, :])
        a = _dot(x, wg_scr[...])
        u = _dot(x, wu_scr[...])
        act = (a * jax.nn.sigmoid(a) * u).astype(BF16)
        y = _dot(act, wd_scr[...]).astype(BF16).astype(F32)
        packed = _pack_bf16_pairs(y)
        ys_ref[0:SLOT_TM, :] = packed[:, :SC_ROW]
        ys_ref[SLOT_TM:2 * SLOT_TM, :] = packed[:, SC_ROW:]


def _experts(xs, tmap, wg, wu, wd):
    n_tiles = xs.shape[0] // (2 * SLOT_TM)
    grid_spec = pltpu.PrefetchScalarGridSpec(
        num_scalar_prefetch=1,
        grid=(n_tiles,),
        in_specs=[
            pl.BlockSpec((2 * SLOT_TM, SC_ROW), lambda j, tm: (tm[2, j], 0)),
            pl.BlockSpec((1, D_MODEL, D_EXPERT), lambda j, tm: (tm[0, j], 0, 0)),
            pl.BlockSpec((1, D_MODEL, D_EXPERT), lambda j, tm: (tm[0, j], 0, 0)),
            pl.BlockSpec((1, D_EXPERT, D_MODEL), lambda j, tm: (tm[0, j], 0, 0)),
        ],
        out_specs=pl.BlockSpec((2 * SLOT_TM, SC_ROW), lambda j, tm: (tm[2, j], 0)),
        scratch_shapes=[pltpu.VMEM((D_MODEL, D_EXPERT), BF16), pltpu.VMEM((D_MODEL, D_EXPERT), BF16),
                        pltpu.VMEM((D_EXPERT, D_MODEL), BF16)],
    )
    return pl.pallas_call(
        _experts_kernel,
        out_shape=jax.ShapeDtypeStruct(xs.shape, jnp.int32),
        grid_spec=grid_spec,
        compiler_params=_cparams(("arbitrary",)),
        name="experts",
    )(tmap, xs, wg, wu, wd)


FIN_TM = 1024


def _final_kernel(x1_ref, mod_ref, yg_ref, w_ref, gf_ref, y_ref):
    moe = None
    for k in range(2):
        base = 2 * k * FIN_TM
        yk = _unpack_bf16_pairs(yg_ref[base:base + FIN_TM, :], yg_ref[base + FIN_TM:base + 2 * FIN_TM, :])
        term = w_ref[:, k:k + 1] * yk.astype(F32)
        moe = term if moe is None else moe + term
    x2 = x1_ref[...] + mod_ref[0, 5:6, :] * moe
    y_ref[...] = x2 * _rms_scale(x2) * gf_ref[...]


def _final(x1, mod3, mod_row_fn, yg, w_cols, g_final):
    n = x1.shape[0]
    return pl.pallas_call(
        _final_kernel,
        out_shape=jax.ShapeDtypeStruct((n, D_MODEL), F32),
        grid=(n // FIN_TM,),
        in_specs=[
            pl.BlockSpec((FIN_TM, D_MODEL), lambda i: (i, 0)),
            pl.BlockSpec((1, 6, D_MODEL), lambda i: (mod_row_fn(i * FIN_TM), 0, 0)),
            pl.BlockSpec((4 * FIN_TM, SC_ROW), lambda i: (i, 0)),
            pl.BlockSpec((FIN_TM, 2), lambda i: (i, 0)),
            _const_spec((1, D_MODEL)),
        ],
        out_specs=pl.BlockSpec((FIN_TM, D_MODEL), lambda i: (i, 0)),
        compiler_params=_cparams(("parallel",)),
        name="final",
    )(x1, mod3, yg, w_cols, g_final)


def _moe(x1, h2p, info, mod3, mod_row_fn, wg, wu, wd, g_final):
    n = x1.shape[0]
    sidx, gidx, tmap = _plan(info)
    n_slots = 2 * n + N_EXPERTS * SLOT_TM
    xs = _sc_scatter_rows(h2p, sidx.reshape(-1), 2 * n_slots)
    ys = _experts(xs, tmap, wg, wu, wd)
    yg = _sc_gather_rows(ys, gidx.reshape(-1))
    w_cols = info[2:4].T
    return _final(x1, mod3, mod_row_fn, yg, w_cols, g_final)


def _stream(x, mod3, mod_row_fn, seq_len, rows_2d, state, w):
    bsz, t, _ = x.shape
    n = bsz * t
    x2d = x.reshape(n, D_MODEL)
    p3, bcum, imb, amax = _in_projection(x2d, mod3, mod_row_fn, w["g_norm1"], w["wmain"], w["bmain"],
                                         w["colscale"], w["wg_hi"], w["wg_lo"], w["bg"])
    m0 = None if state is None else state[2]
    cols, rows, cs = _gate_prep(bcum, imb, amax, m0)
    ml = _mlstm(p3, cols, rows, cs, w["gn3"], None if state is None else state[:2])
    amat, invc = _pool_constants(seq_len, rows_2d)
    n_seq = max(1, 1024 // seq_len)
    yb = _pool(p3, amat, invc, w["wpg"], w["pscale"], seq_len, n_seq, rows_2d)
    ya = ml if state is not None else ml[0]
    x1, h2p, info = _mix(x2d, mod3, mod_row_fn, ya, yb, p3, w["wa"], w["wb"], w["wo"],
                         w["g_norm2"], w["wr_hi"], w["wr_lo"], w["br"])
    y = _moe(x1, h2p, info, mod3, mod_row_fn, w["wg"], w["wu"], w["wd"], w["g_final"])
    return y.reshape(bsz, t, D_MODEL), ml, cs


def kernel(x_prompt, x_sample, state_C, state_n, state_m, c, c_ctx, w_mod, b_mod, g_norm1, w_in, b_in, gn_gain, w_pool_grp, pool_scale, w_proj_a, w_proj_b, w_out, g_norm2, w_router_grp, b_router_grp, w_router_exp, b_router_exp, w_exp_gate, w_exp_up, w_exp_down, g_final):
    bp, tp, _ = x_prompt.shape
    bs, ts, _ = x_sample.shape
    assert w_mod.shape[0] == 1, "single trunk layer"
    assert tp == CHUNK and ts % CHUNK == 0 and ts == ML_ROWS

    cvec = jnp.zeros((8, D_MODEL), F32).at[0].set(c_ctx).at[1:1 + bs].set(c)
    mod3 = _modulation(cvec, w_mod[0], b_mod[0])

    wi, bi = w_in[0], b_in[0]
    d = D_MODEL
    o_q, o_k, o_v, o_o, o_g, o_p, o_ga, o_gb = 0, d, 2 * d, 3 * d, 4 * d, 4 * d + 32, 4 * d + 32 + 512, 4 * d + 32 + 512 + d
    order = [(o_q, d), (o_k, d), (o_v, d), (o_o, d), (o_ga, d), (o_gb, d), (o_p, 512)]
    wmain = _pack_columns(wi, order)
    bmain = jnp.concatenate([bi[s:s + l] for s, l in order]).reshape(1, D_MAIN)
    colscale = jnp.ones((D_MAIN,), F32).at[SLAB_K * LANES:(SLAB_K + N_HEADS) * LANES].set(HEAD_DIM ** -0.5)
    colscale = colscale.reshape(1, D_MAIN)
    gwt = _take_columns_t(wi, o_g, 32)
    gb = bi[o_g:o_g + 32]
    i_f, f_f, i_b, f_b = (slice(0, 8), slice(8, 16), slice(16, 24), slice(24, 32))
    pad_w = jnp.zeros((LANES - N_HD, d), F32)
    pad_b = jnp.zeros((LANES - N_HD,), F32)
    wg2 = jnp.concatenate([gwt[f_f], gwt[f_b], pad_w, gwt[i_f], gwt[i_b], pad_w], axis=0).T
    bg2 = jnp.concatenate([gb[f_f], gb[f_b], pad_b, gb[i_f], gb[i_b], pad_b]).reshape(1, 2 * LANES)
    wg_hi, wg_lo = _split_bf16(wg2)

    wr = jnp.concatenate([w_router_grp[0], w_router_exp[0].reshape(d, N_EXPERTS),
                          jnp.zeros((d, LANES - ROUTER_ROWS), F32)], axis=1)
    br = jnp.concatenate([b_router_grp[0], b_router_exp[0].reshape(N_EXPERTS),
                          jnp.zeros((LANES - ROUTER_ROWS,), F32)]).reshape(1, LANES)
    wr_hi, wr_lo = _split_bf16(wr)

    w = dict(
        g_norm1=g_norm1[0].reshape(1, d), wmain=wmain, bmain=bmain, colscale=colscale,
        wg_hi=wg_hi, wg_lo=wg_lo, bg=bg2,
        gn3=gn_gain[0].reshape(N_HEADS, 1, HEAD_DIM),
        wpg=w_pool_grp[0].astype(BF16), pscale=pool_scale[0].reshape(N_GROUPS, 1, LANES),
        wa=w_proj_a[0].astype(BF16), wb=w_proj_b[0].astype(BF16), wo=w_out[0].astype(BF16),
        g_norm2=g_norm2[0].reshape(1, d), wr_hi=wr_hi, wr_lo=wr_lo, br=br,
        wg=w_exp_gate[0], wu=w_exp_up[0], wd=w_exp_down[0],
        g_final=g_final.reshape(1, d),
    )

    y_prompt, (_, cst, nst), cs_prompt = _stream(x_prompt, mod3, lambda r: 0, tp, None, None, w)
    n_rep = jnp.broadcast_to(state_n[..., None], state_C.shape)
    m0 = jnp.pad(state_m[:, 0].reshape(bs, N_HD), ((0, 0), (0, LANES - N_HD))).reshape(bs, 1, LANES)
    y_sample, _, _ = _stream(x_sample, mod3, lambda r: 1 + r // ts, ts, ts // GRID_W, (state_C, n_rep, m0), w)

    new_c = cst
    new_n = nst.reshape(bp, 1, 2, N_HEADS, HEAD_DIM)
    new_m = cs_prompt[:, 1, :N_HD].reshape(bp, 1, 2, N_HEADS)
    return (y_prompt, y_sample, new_c, new_n, new_m)
```

```python
import functools

import numpy as np
import jax
import jax.numpy as jnp
from jax import lax
from jax.experimental import pallas as pl
from jax.experimental.pallas import tpu as pltpu
from jax.experimental.pallas import tpu_sc as plsc

F32 = jnp.float32
BF16 = jnp.bfloat16

D_MODEL = 1024
N_HEADS = 8
HEAD_DIM = 128
GRID_W = 64
POOL_WINDOWS = (2, 4, 8, 16)
N_GROUPS = 4
EXPERTS_PER_GROUP = 4
N_EXPERTS = 16
D_EXPERT = 512
EPS = 1e-6
NEG = -1e30

LANES = 128
CHUNK = 256
N_SLABS = 52
SLAB_Q, SLAB_K, SLAB_V, SLAB_O, SLAB_GA, SLAB_GB, SLAB_POOL = 0, 8, 16, 24, 32, 40, 48
D_MAIN = N_SLABS * LANES
VMEM_LIMIT = 56 * 1024 * 1024


def _cparams(sem):
    return pltpu.CompilerParams(dimension_semantics=sem, vmem_limit_bytes=VMEM_LIMIT)


def _const_spec(shape):
    nd = len(shape)
    return pl.BlockSpec(shape, lambda *_: (0,) * nd, pipeline_mode=pl.Buffered(1))


def _split_bf16(x):
    hi = x.astype(BF16)
    lo = (x - hi.astype(F32)).astype(BF16)
    return hi, lo


def _dot(a, b):
    return jnp.dot(a, b, preferred_element_type=F32)


def _dot3(a, b_hi, b_lo):
    a_hi, a_lo = _split_bf16(a)
    return _dot(a_hi, b_hi) + _dot(a_hi, b_lo) + _dot(a_lo, b_hi)


def _dot3_packed(a_hi, a_lo, b_hilo):
    first = _dot(a_hi, b_hilo)
    return first[:, :LANES] + first[:, LANES:] + _dot(a_lo, b_hilo[:, :LANES])


def _rms_scale(x):
    return lax.rsqrt(jnp.mean(x * x, axis=-1, keepdims=True) + EPS)


def _log_sigmoid(x):
    return jnp.minimum(x, 0.0) - jnp.log1p(jnp.exp(-jnp.abs(x)))


def _mod_kernel(c_ref, w_ref, b_ref, o_ref):
    c = c_ref[...]
    s = c * jax.nn.sigmoid(c)
    w_hi, w_lo = _split_bf16(w_ref[...])
    o_ref[...] = _dot3(s, w_hi, w_lo) + b_ref[...]


def _modulation(cvec, w_mod, b_mod):
    tn = 2048
    out = pl.pallas_call(
        _mod_kernel,
        out_shape=jax.ShapeDtypeStruct((8, 6 * D_MODEL), F32),
        grid=(6 * D_MODEL // tn,),
        in_specs=[
            pl.BlockSpec((8, D_MODEL), lambda j: (0, 0)),
            pl.BlockSpec((D_MODEL, tn), lambda j: (0, j)),
            pl.BlockSpec((1, tn), lambda j: (0, j)),
        ],
        out_specs=pl.BlockSpec((8, tn), lambda j: (0, j)),
        compiler_params=_cparams(("parallel",)),
        name="modulation",
    )(cvec, w_mod, b_mod.reshape(1, -1))
    return out.reshape(8, 6, D_MODEL)


PACK_COLS = 512


def _pack_kernel(wt_ref, o_ref):
    o_ref[...] = wt_ref[...].T.astype(BF16)


def _pack_columns(w, order):
    k, n = w.shape
    starts = []
    for src, width in order:
        assert width % PACK_COLS == 0
        starts += [src + j for j in range(0, width, PACK_COLS)]
    n_out = len(starts) * PACK_COLS

    unit = 32
    assert all(s % unit == 0 for s in starts)

    def src_row(j):
        row = jnp.int32(starts[-1] // unit)
        for idx in range(len(starts) - 2, -1, -1):
            row = jnp.where(j == idx, starts[idx] // unit, row)
        return row * unit

    return pl.pallas_call(
        _pack_kernel,
        out_shape=jax.ShapeDtypeStruct((k, n_out), BF16),
        grid=(len(starts),),
        in_specs=[pl.BlockSpec((pl.Element(PACK_COLS), pl.Element(k)), lambda j: (src_row(j), 0))],
        out_specs=pl.BlockSpec((k, PACK_COLS), lambda j: (0, j)),
        compiler_params=_cparams(("parallel",)),
        name="pack_columns",
    )(w.T)


def _copy_kernel(w_ref, o_ref):
    o_ref[...] = w_ref[...]


def _take_columns_t(w, start, width):
    k = w.shape[0]
    return pl.pallas_call(
        _copy_kernel,
        out_shape=jax.ShapeDtypeStruct((width, k), w.dtype),
        grid=(1,),
        in_specs=[pl.BlockSpec((pl.Element(width), pl.Element(k)), lambda j: (start, 0))],
        out_specs=pl.BlockSpec((width, k), lambda j: (0, 0)),
        name="take_columns",
    )(w.T)


IN_TM = 512
IN_TN = 512
LOG2E = 1.4426950408889634
N_HD = 2 * N_HEADS


def _fwd_lanes(shape):
    lane = lax.broadcasted_iota(jnp.int32, shape, len(shape) - 1)
    return (lane & N_HEADS) == 0


def _inproj_kernel(x_ref, mod_ref, g_ref, w_ref, b_ref, cs_ref, wg_ref, bg_ref,
                   p3_ref, bcum_ref, imb_ref, amax_ref):
    x = x_ref[...]
    h = x * _rms_scale(x) * g_ref[...]
    h = h * (1.0 + mod_ref[0, 1:2, :]) + mod_ref[0, 0:1, :]
    hb = h.astype(BF16)
    for n in range(D_MAIN // IN_TN):
        sl = slice(n * IN_TN, (n + 1) * IN_TN)
        acc = (_dot(hb, w_ref[:, sl]) + b_ref[:, sl]) * cs_ref[:, sl]
        accb = acc.astype(BF16)
        for j in range(IN_TN // LANES):
            p3_ref[n * (IN_TN // LANES) + j] = accb[:, j * LANES:(j + 1) * LANES]

    h_lo = (h - hb.astype(F32)).astype(BF16)
    gates = _dot3_packed(hb, h_lo, wg_ref[...]) + bg_ref[...]
    ls = _log_sigmoid(gates)
    gi = pltpu.roll(gates, LANES - N_HD, 1)
    row = lax.broadcasted_iota(jnp.int32, (CHUNK, LANES), 0)
    fwd = _fwd_lanes((CHUNK, LANES))
    for c in range(IN_TM // CHUNK):
        rows = slice(c * CHUNK, (c + 1) * CHUNK)
        cf = ls[rows]
        cb = cf
        s = 1
        while s < CHUNK:
            cf = cf + jnp.where(row >= s, pltpu.roll(cf, s, 0), 0.0)
            cb = cb + jnp.where(row < CHUNK - s, pltpu.roll(cb, CHUNK - s, 0), 0.0)
            s *= 2
        b = jnp.where(fwd, cf, cb)
        imb = gi[rows] - b
        mf = imb
        mb = imb
        s = 1
        while s < CHUNK:
            mf = jnp.maximum(mf, jnp.where(row >= s, pltpu.roll(mf, s, 0), -jnp.inf))
            mb = jnp.maximum(mb, jnp.where(row < CHUNK - s, pltpu.roll(mb, CHUNK - s, 0), -jnp.inf))
            s *= 2
        bcum_ref[rows, :] = b
        imb_ref[rows, :] = imb
        amax_ref[rows, :] = b + jnp.where(fwd, mf, mb)


def _in_projection(x2d, mod3, mod_row_fn, g_norm1, wmain, bmain, colscale, wg, bg):
    n = x2d.shape[0]
    return pl.pallas_call(
        _inproj_kernel,
        out_shape=(jax.ShapeDtypeStruct((N_SLABS, n, LANES), BF16),) + (jax.ShapeDtypeStruct((n, LANES), F32),) * 3,
        grid=(n // IN_TM,),
        in_specs=[
            pl.BlockSpec((IN_TM, D_MODEL), lambda i: (i, 0)),
            pl.BlockSpec((1, 6, D_MODEL), lambda i: (mod_row_fn(i * IN_TM), 0, 0)),
            _const_spec((1, D_MODEL)),
            _const_spec((D_MODEL, D_MAIN)),
            _const_spec((1, D_MAIN)),
            _const_spec((1, D_MAIN)),
            _const_spec((D_MODEL, 2 * LANES)),
            _const_spec((1, LANES)),
        ],
        out_specs=(pl.BlockSpec((N_SLABS, IN_TM, LANES), lambda i: (0, i, 0)),)
        + (pl.BlockSpec((IN_TM, LANES), lambda i: (i, 0)),) * 3,
        compiler_params=_cparams(("parallel",)),
        name="in_projection",
    )(x2d, mod3, g_norm1, wmain, bmain, colscale, wg, bg)


GP_ROWS = 1024
GP_NCH = GP_ROWS // CHUNK
Q_C2, Q_EM, Q_WI, Q_WC = 0, 1, 2, 3


def _gate_prep_kernel(*refs, carry):
    if carry:
        b_ref, imb_ref, a_ref, m0_ref, cols_ref, rw_ref, cs_ref = refs
    else:
        b_ref, imb_ref, a_ref, cols_ref, rw_ref, cs_ref = refs
    fwd1 = _fwd_lanes((1, LANES))
    lane = lax.broadcasted_iota(jnp.int32, (CHUNK, LANES), 1)
    neg = jnp.full((1, LANES), NEG, F32)

    def at_scan_end(ref, c):
        return jnp.where(fwd1, ref[(c + 1) * CHUNK - 1:(c + 1) * CHUNK, :], ref[c * CHUNK:c * CHUNK + 1, :])

    b_last = [at_scan_end(b_ref, c) for c in range(GP_NCH)]
    a_last = [at_scan_end(a_ref, c) for c in range(GP_NCH)]
    m_start = m0_ref[0] if carry else neg
    m_prev = [[None] * GP_NCH, [None] * GP_NCH]
    m_new = [[None] * GP_NCH, [None] * GP_NCH]
    for d, order in enumerate((range(GP_NCH), range(GP_NCH - 1, -1, -1))):
        m = m_start
        for c in order:
            m_prev[d][c] = m
            m_new[d][c] = jnp.maximum(b_last[c] + m, a_last[c])
            m = m_new[d][c] if carry else neg

    row_src = []
    for c in range(GP_NCH):
        rows = slice(c * CHUNK, (c + 1) * CHUNK)
        mp = jnp.where(fwd1, m_prev[0][c], m_prev[1][c])
        mn = jnp.where(fwd1, m_new[0][c], m_new[1][c])
        b = b_ref[rows, :]
        imb = imb_ref[rows, :]
        inter = b + mp
        m_t = jnp.maximum(inter, a_ref[rows, :])
        c2 = (b - m_t) * LOG2E
        em = jnp.exp(-m_t)
        wc = jnp.exp(b_last[c] - mn + imb)
        packed = jnp.where(lane < N_HD, c2, pltpu.roll(em, N_HD * Q_EM, 1))
        if carry:
            wi = jnp.exp(inter - m_t)
            packed = jnp.where(lane < N_HD * Q_WI, packed, pltpu.roll(wi, N_HD * Q_WI, 1))
        cols_ref[rows, :] = jnp.where(lane < N_HD * Q_WC, packed, pltpu.roll(wc, N_HD * Q_WC, 1))
        row_src.append(jnp.where(lane < N_HD, imb * (-LOG2E), pltpu.roll(wc, N_HD, 1)))
        cs_ref[c, 0:1, :] = jnp.exp(b_last[c] + mp - mn)
        cs_ref[c, 1:2, :] = mn

    rows_t = jnp.concatenate(row_src, axis=0).T
    for r in range(2 * N_HD):
        rw_ref[r] = rows_t[r:r + 1]


def _gate_prep(bcum, imb, amax, m0=None):
    n = bcum.shape[0]
    carry = m0 is not None
    blk = pl.BlockSpec((GP_ROWS, LANES), lambda i: (i, 0))
    in_specs = [blk, blk, blk]
    args = [bcum, imb, amax]
    if carry:
        in_specs.append(pl.BlockSpec((1, 1, LANES), lambda i: (i, 0, 0)))
        args.append(m0)
    return pl.pallas_call(
        functools.partial(_gate_prep_kernel, carry=carry),
        out_shape=(jax.ShapeDtypeStruct((n, LANES), F32), jax.ShapeDtypeStruct((2 * N_HD, 1, n), F32),
                   jax.ShapeDtypeStruct((n // CHUNK, 2, LANES), F32)),
        grid=(n // GP_ROWS,),
        in_specs=in_specs,
        out_specs=(blk, pl.BlockSpec((2 * N_HD, 1, GP_ROWS), lambda i: (0, 0, i)),
                   pl.BlockSpec((GP_NCH, 2, LANES), lambda i: (i, 0, 0))),
        compiler_params=_cparams(("parallel",)),
        name="gate_prep",
    )(*args)


ML_ROWS = 1024
ML_ROWS_FREE = 2048


def _pick_col(blk, lane_iota, idx):
    return jnp.sum(jnp.where(lane_iota == idx, blk, 0.0), axis=1, keepdims=True)


def _mlstm_kernel(*refs, carry, n_chunks):
    if carry:
        (q_ref, k_ref, v_ref, o_ref, cols_ref, r2f_ref, r2b_ref, gn_ref, cs_ref, c0_ref, n0_ref,
         ya_ref, s_scr, h_scr) = refs
    else:
        (q_ref, k_ref, v_ref, o_ref, cols_ref, r2f_ref, r2b_ref, gn_ref, wrf_ref, wrb_ref,
         ya_ref, cst_ref, nst_ref, s_scr, h_scr) = refs
    head = pl.program_id(1)
    lane = lax.broadcasted_iota(jnp.int32, (CHUNK, LANES), 1)
    lane1 = lax.broadcasted_iota(jnp.int32, (1, LANES), 1)
    r_idx = lax.broadcasted_iota(jnp.int32, (CHUNK, CHUNK), 0)
    c_idx = lax.broadcasted_iota(jnp.int32, (CHUNK, CHUNK), 1)
    ones = jnp.ones((CHUNK, HEAD_DIM), BF16)
    tn_dims = (((0,), (0,)), ((), ()))

    for d in range(2):
        mask = (c_idx <= r_idx) if d == 0 else (c_idx >= r_idx)
        r2_ref = r2f_ref if d == 0 else r2b_ref
        base = N_HEADS * d + head
        if carry:
            cn = jnp.concatenate([c0_ref[0, 0, d, 0], n0_ref[0, 0, d, 0]], axis=1)
        order = range(n_chunks) if d == 0 else range(n_chunks - 1, -1, -1)
        for step, c in enumerate(order):
            rows = slice(c * CHUNK, (c + 1) * CHUNK)
            q = q_ref[0, rows, :]
            k = k_ref[0, rows, :]
            v = v_ref[0, rows, :]
            v1 = jnp.concatenate([v, ones], axis=1)
            if d == 0:
                s = lax.dot_general(q, k, (((1,), (1,)), ((), ())), preferred_element_type=F32)
                s_scr[rows, :] = s
            else:
                s = s_scr[rows, :]
            blk = cols_ref[rows, :]
            arg = _pick_col(blk, lane, base + N_HD * Q_C2) - r2_ref[0, :, rows]
            sw = s * jnp.exp2(jnp.where(mask, arg, NEG))
            nd = _dot(sw.astype(BF16), v1)
            if carry:
                nd = nd + _pick_col(blk, lane, base + N_HD * Q_WI) * _dot(q, cn.astype(BF16))
            hdir = nd[:, :HEAD_DIM] / jnp.maximum(jnp.abs(nd[:, HEAD_DIM:]), _pick_col(blk, lane, base + N_HD * Q_EM))
            if d == 0:
                h_scr[rows, :] = hdir
            else:
                h_scr[rows, :] = h_scr[rows, :] + hdir

            if carry and step == n_chunks - 1:
                continue
            w_col = _pick_col(blk, lane, base + N_HD * Q_WC)
            if carry:
                u = lax.dot_general(k, (w_col * v1.astype(F32)).astype(BF16), tn_dims,
                                    preferred_element_type=F32)
                decay = jnp.sum(jnp.where(lane1 == base, cs_ref[c, 0:1, :], 0.0), axis=1, keepdims=True)
                cn = decay * cn + u
            else:
                w_row = (wrf_ref if d == 0 else wrb_ref)[0, :, rows]
                w8 = jnp.broadcast_to(w_row, (8, CHUNK)).astype(BF16)
                cst_ref[c, 0, d, 0] = lax.dot_general(k, (w_col * v.astype(F32)).astype(BF16), tn_dims,
                                                      preferred_element_type=F32)
                nst_ref[c, d, 0] = _dot(w8, k)[0:1, :]

    hh = h_scr[...]
    hn = hh * _rms_scale(hh) * gn_ref[0]
    ya_ref[0] = (hn * jax.nn.sigmoid(o_ref[0].astype(F32))).astype(BF16)


def _mlstm(p3, cols, rows, cs, gn3, state=None):
    n = p3.shape[1]
    carry = state is not None
    rows_step = ML_ROWS if carry else ML_ROWS_FREE
    n_chunks = rows_step // CHUNK
    nblk = n // rows_step

    def slab(base):
        return pl.BlockSpec((1, rows_step, LANES), lambda g, h: (base + h, g, 0))

    def row(base):
        return pl.BlockSpec((1, 1, rows_step), lambda g, h: (base + h, 0, g))

    in_specs = [
        slab(SLAB_Q), slab(SLAB_K), slab(SLAB_V), slab(SLAB_O),
        pl.BlockSpec((rows_step, LANES), lambda g, h: (g, 0)),
        row(0), row(N_HEADS),
        pl.BlockSpec((1, 1, LANES), lambda g, h: (h, 0, 0)),
    ]
    args = [p3, p3, p3, p3, cols, rows, rows, gn3]
    ya_shape = jax.ShapeDtypeStruct((N_HEADS, n, LANES), BF16)
    ya_spec = pl.BlockSpec((1, rows_step, LANES), lambda g, h: (h, g, 0))
    state_spec = pl.BlockSpec((1, 1, 2, 1, HEAD_DIM, HEAD_DIM), lambda g, h: (g, 0, 0, h, 0, 0))
    if carry:
        c0, n0 = state
        in_specs += [pl.BlockSpec((n_chunks, 2, LANES), lambda g, h: (g, 0, 0)), state_spec, state_spec]
        args += [cs, c0, n0]
        out_shape = ya_shape
        out_specs = ya_spec
    else:
        in_specs += [row(N_HD), row(N_HD + N_HEADS)]
        args += [rows, rows]
        nseq = n // CHUNK
        out_shape = (ya_shape,
                     jax.ShapeDtypeStruct((nseq, 1, 2, N_HEADS, HEAD_DIM, HEAD_DIM), F32),
                     jax.ShapeDtypeStruct((nseq, 2, N_HEADS, 1, HEAD_DIM), F32))
        out_specs = (ya_spec,
                     pl.BlockSpec((n_chunks, 1, 2, 1, HEAD_DIM, HEAD_DIM), lambda g, h: (g, 0, 0, h, 0, 0)),
                     pl.BlockSpec((n_chunks, 2, 1, 1, HEAD_DIM), lambda g, h: (g, 0, h, 0, 0)))
    return pl.pallas_call(
        functools.partial(_mlstm_kernel, carry=carry, n_chunks=n_chunks),
        out_shape=out_shape,
        grid=(nblk, N_HEADS),
        in_specs=in_specs,
        out_specs=out_specs,
        scratch_shapes=[pltpu.VMEM((rows_step, CHUNK), F32), pltpu.VMEM((rows_step, HEAD_DIM), F32)],
        compiler_params=_cparams(("parallel", "parallel")),
        name="mlstm_latent" if carry else "mlstm_prompt",
    )(*args)


def _pool_kernel(u_ref, a_ref, ic_ref, wpg_ref, ps_ref, yb_ref, *, seq_len, n_seq, rows_2d):
    piece = seq_len if rows_2d is None else GRID_W
    n_piece = n_seq if rows_2d is None else rows_2d
    for g in range(N_GROUPS):
        u = jnp.concatenate([u_ref[g, i * piece:(i + 1) * piece, :] for i in range(n_piece)], axis=1)
        win = _dot(a_ref[g], u)
        if rows_2d is None:
            inv_cnt = jnp.concatenate([ic_ref[g]] * n_piece, axis=1)
        else:
            run = [jnp.zeros((piece, LANES), F32)]
            for r in range(rows_2d):
                run.append(run[-1] + win[:, r * LANES:(r + 1) * LANES])
            half = POOL_WINDOWS[g] // 2
            win = jnp.concatenate([run[min(r - half + POOL_WINDOWS[g], rows_2d)] - run[max(r - half, 0)]
                                   for r in range(rows_2d)], axis=1)
            inv_cnt = jnp.concatenate([ic_ref[g, r * piece:(r + 1) * piece, :] for r in range(rows_2d)], axis=1)
        p = (win * inv_cnt - u.astype(F32)).astype(BF16)
        p = jnp.concatenate([p[:, i * LANES:(i + 1) * LANES] for i in range(n_piece)], axis=0)
        yb_ref[g] = (_dot(p, wpg_ref[g]) * ps_ref[g]).astype(BF16)


def _pool(p3, amat, invc, wpg, pscale, seq_len, n_seq, rows_2d):
    n = p3.shape[1]
    rows = seq_len * n_seq
    return pl.pallas_call(
        functools.partial(_pool_kernel, seq_len=seq_len, n_seq=n_seq, rows_2d=rows_2d),
        out_shape=jax.ShapeDtypeStruct((N_GROUPS, n, LANES), BF16),
        grid=(n // rows,),
        in_specs=[
            pl.BlockSpec((N_GROUPS, rows, LANES), lambda i: (SLAB_POOL // N_GROUPS, i, 0)),
            _const_spec(amat.shape),
            _const_spec((N_GROUPS, seq_len, LANES)),
            _const_spec((N_GROUPS, LANES, LANES)),
            _const_spec((N_GROUPS, 1, LANES)),
        ],
        out_specs=pl.BlockSpec((N_GROUPS, rows, LANES), lambda i: (0, i, 0)),
        compiler_params=_cparams(("parallel",)),
        name="pool",
    )(p3, amat, invc, wpg, pscale)


def _window_matrix(size, w):
    idx = np.arange(size)
    lo = np.clip(idx - w // 2, 0, size)
    hi = np.clip(idx - w // 2 + w, 0, size)
    s = np.arange(size)[None, :]
    a = ((s >= lo[:, None]) & (s < hi[:, None])).astype(np.float32)
    return a, (hi - lo).astype(np.float32)


def _pool_constants(seq_len, rows_2d):
    mats, invs = [], []
    for w in POOL_WINDOWS:
        if rows_2d is None:
            a, cnt = _window_matrix(seq_len, w)
        else:
            _, cr = _window_matrix(rows_2d, w)
            a, cc = _window_matrix(GRID_W, w)
            cnt = np.kron(cr, cc)
        mats.append(a)
        invs.append(np.repeat((1.0 / cnt)[:, None], LANES, axis=1))
    return (jnp.asarray(np.stack(mats), dtype=BF16), jnp.asarray(np.stack(invs), dtype=F32))


MIX_TM = 512


def _mix_kernel(x_ref, mod_ref, ya_ref, yb_ref, ga_ref, gb_ref, wa_ref, wb_ref, wo_ref,
                g2_ref, wr_ref, br_ref, x1_ref, h2_ref, info_ref):
    ya = jnp.concatenate([ya_ref[i] for i in range(N_HEADS)], axis=1)
    yb = jnp.concatenate([yb_ref[i] for i in range(N_GROUPS)], axis=1)
    ga = jnp.concatenate([ga_ref[i] for i in range(8)], axis=1).astype(F32)
    gb = jnp.concatenate([gb_ref[i] for i in range(8)], axis=1).astype(F32)
    merged = jax.nn.sigmoid(ga) * _dot(ya, wa_ref[...]) + jax.nn.sigmoid(gb) * _dot(yb, wb_ref[...])
    out = _dot(merged.astype(BF16), wo_ref[...])
    x1 = x_ref[...] + mod_ref[0, 2:3, :] * out
    x1_ref[...] = x1

    h2 = x1 * _rms_scale(x1) * g2_ref[...]
    h2 = h2 * (1.0 + mod_ref[0, 4:5, :]) + mod_ref[0, 3:4, :]
    h_hi, h_lo = _split_bf16(h2)
    packed = _pack_bf16_pairs(h_hi.astype(F32))
    h2_ref[0:MIX_TM, :] = packed[:, :SC_ROW]
    h2_ref[MIX_TM:2 * MIX_TM, :] = packed[:, SC_ROW:]
    logits = _dot3_packed(h_hi, h_lo, wr_ref[...]) + br_ref[...]
    first, second, w1, w2 = _route_rows(logits.T)
    zero = jnp.zeros_like(w1)
    info_ref[...] = jnp.concatenate([first, second, w1, w2, zero, zero, zero, zero], axis=0)


def _mix(x2d, mod3, mod_row_fn, ya, yb, p3, wa, wb, wo, g_norm2, wr, br):
    n = x2d.shape[0]
    return pl.pallas_call(
        _mix_kernel,
        out_shape=(jax.ShapeDtypeStruct((n, D_MODEL), F32),
                   jax.ShapeDtypeStruct((2 * n, SC_ROW), jnp.int32), jax.ShapeDtypeStruct((8, n), F32)),
        grid=(n // MIX_TM,),
        in_specs=[
            pl.BlockSpec((MIX_TM, D_MODEL), lambda i: (i, 0)),
            pl.BlockSpec((1, 6, D_MODEL), lambda i: (mod_row_fn(i * MIX_TM), 0, 0)),
            pl.BlockSpec((N_HEADS, MIX_TM, LANES), lambda i: (0, i, 0)),
            pl.BlockSpec((N_GROUPS, MIX_TM, LANES), lambda i: (0, i, 0)),
            pl.BlockSpec((8, MIX_TM, LANES), lambda i: (SLAB_GA // 8, i, 0)),
            pl.BlockSpec((8, MIX_TM, LANES), lambda i: (SLAB_GB // 8, i, 0)),
            _const_spec((D_MODEL, D_MODEL)),
            _const_spec((N_GROUPS * LANES, D_MODEL)),
            _const_spec((D_MODEL, D_MODEL)),
            _const_spec((1, D_MODEL)),
            _const_spec((D_MODEL, 2 * LANES)),
            _const_spec((1, LANES)),
        ],
        out_specs=(pl.BlockSpec((MIX_TM, D_MODEL), lambda i: (i, 0)),
                   pl.BlockSpec((2 * MIX_TM, SC_ROW), lambda i: (i, 0)),
                   pl.BlockSpec((8, MIX_TM), lambda i: (0, i))),
        compiler_params=_cparams(("parallel",)),
        name="mix",
    )(x2d, mod3, ya, yb, p3, p3, wa, wb, wo, g_norm2, wr, br)


PLAN_TB = 1024
ROUTER_ROWS = 4 + N_EXPERTS
SLOT_TM = 1024
SC_ROW = 256
SC_WIN = 128
HI_MASK = -65536


def _pack_bf16_pairs(x):
    bits = pltpu.bitcast(x, jnp.int32)
    half = D_MODEL // 2
    return jnp.bitwise_or(jnp.bitwise_and(lax.shift_right_logical(bits[:, :half], 16), 0xFFFF),
                          jnp.bitwise_and(bits[:, half:], HI_MASK))


def _unpack_bf16_pairs(lo_words, hi_words):
    def low(w):
        return pltpu.bitcast(lax.shift_left(w, 16), F32)

    def high(w):
        return pltpu.bitcast(jnp.bitwise_and(w, HI_MASK), F32)

    return jnp.concatenate([low(lo_words), low(hi_words), high(lo_words), high(hi_words)], axis=1).astype(BF16)


def _route_rows(lt):
    lg = [lt[i:i + 1] for i in range(N_GROUPS)]
    mx = jnp.maximum(jnp.maximum(lg[0], lg[1]), jnp.maximum(lg[2], lg[3]))
    p_top = 1.0 / (jnp.exp(lg[0] - mx) + jnp.exp(lg[1] - mx) + jnp.exp(lg[2] - mx) + jnp.exp(lg[3] - mx))
    gidx = jnp.zeros(lg[0].shape, jnp.int32)
    best = lg[0]
    for i in range(1, N_GROUPS):
        upd = lg[i] > best
        gidx = jnp.where(upd, i, gidx)
        best = jnp.where(upd, lg[i], best)

    def expert_row(g, e):
        r = N_GROUPS + g * EXPERTS_PER_GROUP + e
        return lt[r:r + 1]

    le = [jnp.where(gidx == 0, expert_row(0, e),
                    jnp.where(gidx == 1, expert_row(1, e),
                              jnp.where(gidx == 2, expert_row(2, e), expert_row(3, e))))
          for e in range(EXPERTS_PER_GROUP)]
    v1 = jnp.maximum(jnp.maximum(le[0], le[1]), jnp.maximum(le[2], le[3]))
    i1 = jnp.full(v1.shape, EXPERTS_PER_GROUP - 1, jnp.int32)
    for e in range(EXPERTS_PER_GROUP - 2, -1, -1):
        i1 = jnp.where(le[e] == v1, e, i1)
    le2 = [jnp.where(i1 == e, -jnp.inf, le[e]) for e in range(EXPERTS_PER_GROUP)]
    v2 = jnp.maximum(jnp.maximum(le2[0], le2[1]), jnp.maximum(le2[2], le2[3]))
    i2 = jnp.full(v2.shape, EXPERTS_PER_GROUP - 1, jnp.int32)
    for e in range(EXPERTS_PER_GROUP - 2, -1, -1):
        i2 = jnp.where(le2[e] == v2, e, i2)
    e2 = jnp.exp(v2 - v1)
    first = (gidx * EXPERTS_PER_GROUP + i1).astype(F32)
    second = (gidx * EXPERTS_PER_GROUP + i2).astype(F32)
    return first, second, p_top / (1.0 + e2), p_top * e2 / (1.0 + e2)


def _plan_kernel(info_ref, before_ref, sidx_ref, gidx_ref, tmap_ref, rank_scr, count_scr, *, n_blocks):
    step = pl.program_id(0)
    expert_id = lax.broadcasted_iota(jnp.int32, (N_EXPERTS, PLAN_TB), 0).astype(F32)

    @pl.when(step == 0)
    def _():
        count_scr[...] = jnp.zeros_like(count_scr)

    @pl.when(step < n_blocks)
    def _():
        cols = pl.ds(pl.multiple_of(step * PLAN_TB, PLAN_TB), PLAN_TB)
        sel = jnp.where((info_ref[0:1, cols] == expert_id) | (info_ref[1:2, cols] == expert_id), 1.0, 0.0)
        selb = sel.astype(BF16)
        rank_scr[:, cols] = _dot(selb, before_ref[...]) + count_scr[:, 0:1]
        count_scr[...] += _dot(selb, jnp.ones((PLAN_TB, LANES), BF16))

    @pl.when(step == n_blocks)
    def _():
        counts = count_scr[...]
        padded = jnp.floor((counts + (SLOT_TM - 1)) * (1.0 / SLOT_TM)) * SLOT_TM
        starts, ends, run = [], [], jnp.zeros((1, LANES), F32)
        for e in range(N_EXPERTS):
            starts.append(run)
            run = run + padded[e:e + 1]
            ends.append(run)
        n = info_ref.shape[1]
        shift = SLOT_TM.bit_length() - 1
        for k in range(2):
            chosen = info_ref[k:k + 1, :]
            slot = jnp.zeros((1, n), F32)
            for e in range(N_EXPERTS):
                slot = jnp.where(chosen == float(e), starts[e][:, 0:1] + rank_scr[e:e + 1, :], slot)
            slot = slot.astype(jnp.int32)
            low = jnp.left_shift(jnp.right_shift(slot, shift), shift + 1) + jnp.bitwise_and(slot, SLOT_TM - 1)
            for half in range(2):
                rows = low + half * SLOT_TM
                for t in range(n // MIX_TM):
                    sidx_ref[k, t, half:half + 1, :] = rows[:, t * MIX_TM:(t + 1) * MIX_TM]
                for t in range(n // FIN_TM):
                    gidx_ref[t, k, half:half + 1, :] = rows[:, t * FIN_TM:(t + 1) * FIN_TM]
        tile_start = lax.broadcasted_iota(jnp.int32, (1, LANES), 1).astype(F32) * SLOT_TM
        used = jnp.where(tile_start < run, 1.0, 0.0)
        tile_start = jnp.minimum(tile_start, run - SLOT_TM)
        owner = jnp.zeros((1, LANES), F32)
        for e in range(N_EXPERTS - 1):
            owner = owner + jnp.where(ends[e] <= tile_start, 1.0, 0.0)
        tile = tile_start * (1.0 / SLOT_TM)
        tmap_ref[...] = jnp.concatenate([owner, used, tile] + [jnp.zeros((1, LANES), F32)] * 5,
                                        axis=0).astype(jnp.int32)


def _plan(info):
    n = info.shape[1]
    n_blocks = n // PLAN_TB
    t_i = lax.broadcasted_iota(jnp.int32, (PLAN_TB, PLAN_TB), 0)
    t_j = lax.broadcasted_iota(jnp.int32, (PLAN_TB, PLAN_TB), 1)
    before = (t_i < t_j).astype(BF16)
    return pl.pallas_call(
        functools.partial(_plan_kernel, n_blocks=n_blocks),
        out_shape=(jax.ShapeDtypeStruct((2, n // MIX_TM, 2, MIX_TM), jnp.int32),
                   jax.ShapeDtypeStruct((n // FIN_TM, 2, 2, FIN_TM), jnp.int32),
                   jax.ShapeDtypeStruct((8, LANES), jnp.int32)),
        grid=(n_blocks + 1,),
        in_specs=[_const_spec((8, n)), _const_spec((PLAN_TB, PLAN_TB))],
        out_specs=(pl.BlockSpec((2, n // MIX_TM, 2, MIX_TM), lambda i: (0, 0, 0, 0)),
                   pl.BlockSpec((n // FIN_TM, 2, 2, FIN_TM), lambda i: (0, 0, 0, 0)),
                   pl.BlockSpec((8, LANES), lambda i: (0, 0))),
        scratch_shapes=[pltpu.VMEM((N_EXPERTS, n), F32), pltpu.VMEM((N_EXPERTS, LANES), F32)],
        compiler_params=_cparams(("arbitrary",)),
        name="plan",
    )(info, before)


def _sc_mesh():
    return plsc.VectorSubcoreMesh(core_axis_name="core", subcore_axis_name="subcore")


def _sc_scatter_rows(x, idx, n_out):
    n_idx = idx.shape[0]
    src_blocks = x.shape[0] // SC_WIN

    @pl.kernel(out_type=jax.ShapeDtypeStruct((n_out, SC_ROW), x.dtype), mesh=_sc_mesh(), scratch_types=[])
    def scatter_kernel(x_hbm, i_hbm, o_hbm):
        def body(x_vmem, i_vmem):
            pltpu.sync_copy(x_vmem, o_hbm.at[i_vmem.at[0]])

        pltpu.emit_pipeline(
            body,
            grid=(n_idx // SC_WIN,),
            in_specs=[pl.BlockSpec((SC_WIN, SC_ROW), index_map=lambda i: (i % src_blocks, 0)),
                      pl.BlockSpec((1, SC_WIN), index_map=lambda i: (0, i))],
            out_specs=[],
            core_axis_name=("core", "subcore"),
            dimension_semantics=(pltpu.PARALLEL,),
        )(x_hbm, i_hbm)

    return scatter_kernel(x, idx.reshape(1, n_idx))


def _sc_gather_rows(x, idx):
    n_idx = idx.shape[0]

    @pl.kernel(out_type=jax.ShapeDtypeStruct((n_idx, SC_ROW), x.dtype), mesh=_sc_mesh(), scratch_types=[])
    def gather_kernel(x_hbm, i_hbm, o_hbm):
        def body(i_vmem, o_vmem):
            pltpu.sync_copy(x_hbm.at[i_vmem.at[0]], o_vmem)

        pltpu.emit_pipeline(
            body,
            grid=(n_idx // SC_WIN,),
            in_specs=[pl.BlockSpec((1, SC_WIN), index_map=lambda i: (0, i))],
            out_specs=[pl.BlockSpec((SC_WIN, SC_ROW), index_map=lambda i: (i, 0))],
            core_axis_name=("core", "subcore"),
            dimension_semantics=(pltpu.PARALLEL,),
        )(i_hbm, o_hbm)

    return gather_kernel(x, idx.reshape(1, n_idx))


def _experts_kernel(tmap_ref, xs_ref, wg_ref, wu_ref, wd_ref, ys_ref, wg_scr, wu_scr, wd_scr):
    j = pl.program_id(0)

    @pl.when((j == 0) | (tmap_ref[0, j] != tmap_ref[0, jnp.maximum(j - 1, 0)]))
    def _():
        wg_scr[...] = wg_ref[0].astype(BF16)
        wu_scr[...] = wu_ref[0].astype(BF16)
        wd_scr[...] = wd_ref[0].astype(BF16)

    @pl.when(tmap_ref[1, j] > 0)
    def _():
        x = _unpack_bf16_pairs(xs_ref[0:SLOT_TM, :], xs_ref[SLOT_TM:2 * SLOT_TM, :])
        a = _dot(x, wg_scr[...])
        u = _dot(x, wu_scr[...])
        act = (a * jax.nn.sigmoid(a) * u).astype(BF16)
        y = _dot(act, wd_scr[...]).astype(BF16).astype(F32)
        packed = _pack_bf16_pairs(y)
        ys_ref[0:SLOT_TM, :] = packed[:, :SC_ROW]
        ys_ref[SLOT_TM:2 * SLOT_TM, :] = packed[:, SC_ROW:]


def _experts(xs, tmap, wg, wu, wd):
    n_tiles = xs.shape[0] // (2 * SLOT_TM)
    grid_spec = pltpu.PrefetchScalarGridSpec(
        num_scalar_prefetch=1,
        grid=(n_tiles,),
        in_specs=[
            pl.BlockSpec((2 * SLOT_TM, SC_ROW), lambda j, tm: (tm[2, j], 0)),
            pl.BlockSpec((1, D_MODEL, D_EXPERT), lambda j, tm: (tm[0, j], 0, 0)),
            pl.BlockSpec((1, D_MODEL, D_EXPERT), lambda j, tm: (tm[0, j], 0, 0)),
            pl.BlockSpec((1, D_EXPERT, D_MODEL), lambda j, tm: (tm[0, j], 0, 0)),
        ],
        out_specs=pl.BlockSpec((2 * SLOT_TM, SC_ROW), lambda j, tm: (tm[2, j], 0)),
        scratch_shapes=[pltpu.VMEM((D_MODEL, D_EXPERT), BF16), pltpu.VMEM((D_MODEL, D_EXPERT), BF16),
                        pltpu.VMEM((D_EXPERT, D_MODEL), BF16)],
    )
    return pl.pallas_call(
        _experts_kernel,
        out_shape=jax.ShapeDtypeStruct(xs.shape, jnp.int32),
        grid_spec=grid_spec,
        compiler_params=_cparams(("arbitrary",)),
        name="experts",
    )(tmap, xs, wg, wu, wd)


FIN_TM = 1024


def _final_kernel(x1_ref, mod_ref, yg_ref, w_ref, gf_ref, y_ref):
    moe = None
    for k in range(2):
        base = 2 * k * FIN_TM
        yk = _unpack_bf16_pairs(yg_ref[base:base + FIN_TM, :], yg_ref[base + FIN_TM:base + 2 * FIN_TM, :])
        term = w_ref[:, k:k + 1] * yk.astype(F32)
        moe = term if moe is None else moe + term
    x2 = x1_ref[...] + mod_ref[0, 5:6, :] * moe
    y_ref[...] = x2 * _rms_scale(x2) * gf_ref[...]


def _final(x1, mod3, mod_row_fn, yg, w_cols, g_final):
    n = x1.shape[0]
    return pl.pallas_call(
        _final_kernel,
        out_shape=jax.ShapeDtypeStruct((n, D_MODEL), F32),
        grid=(n // FIN_TM,),
        in_specs=[
            pl.BlockSpec((FIN_TM, D_MODEL), lambda i: (i, 0)),
            pl.BlockSpec((1, 6, D_MODEL), lambda i: (mod_row_fn(i * FIN_TM), 0, 0)),
            pl.BlockSpec((4 * FIN_TM, SC_ROW), lambda i: (i, 0)),
            pl.BlockSpec((FIN_TM, 2), lambda i: (i, 0)),
            _const_spec((1, D_MODEL)),
        ],
        out_specs=pl.BlockSpec((FIN_TM, D_MODEL), lambda i: (i, 0)),
        compiler_params=_cparams(("parallel",)),
        name="final",
    )(x1, mod3, yg, w_cols, g_final)


def _moe(x1, h2p, info, mod3, mod_row_fn, wg, wu, wd, g_final):
    n = x1.shape[0]
    sidx, gidx, tmap = _plan(info)
    n_slots = 2 * n + N_EXPERTS * SLOT_TM
    xs = _sc_scatter_rows(h2p, sidx.reshape(-1), 2 * n_slots)
    ys = _experts(xs, tmap, wg, wu, wd)
    yg = _sc_gather_rows(ys, gidx.reshape(-1))
    w_cols = info[2:4].T
    return _final(x1, mod3, mod_row_fn, yg, w_cols, g_final)


def _stream(x, mod3, mod_row_fn, seq_len, rows_2d, state, w):
    bsz, t, _ = x.shape
    n = bsz * t
    x2d = x.reshape(n, D_MODEL)
    p3, bcum, imb, amax = _in_projection(x2d, mod3, mod_row_fn, w["g_norm1"], w["wmain"], w["bmain"],
                                         w["colscale"], w["w_gates"], w["b_gates"])
    m0 = None if state is None else state[2]
    cols, rows, cs = _gate_prep(bcum, imb, amax, m0)
    ml = _mlstm(p3, cols, rows, cs, w["gn3"], None if state is None else state[:2])
    amat, invc = _pool_constants(seq_len, rows_2d)
    n_seq = max(1, 1024 // seq_len)
    yb = _pool(p3, amat, invc, w["wpg"], w["pscale"], seq_len, n_seq, rows_2d)
    ya = ml if state is not None else ml[0]
    x1, h2p, info = _mix(x2d, mod3, mod_row_fn, ya, yb, p3, w["wa"], w["wb"], w["wo"],
                         w["g_norm2"], w["wr"], w["br"])
    y = _moe(x1, h2p, info, mod3, mod_row_fn, w["wg"], w["wu"], w["wd"], w["g_final"])
    return y.reshape(bsz, t, D_MODEL), ml, cs


def kernel(x_prompt, x_sample, state_C, state_n, state_m, c, c_ctx, w_mod, b_mod, g_norm1, w_in, b_in, gn_gain, w_pool_grp, pool_scale, w_proj_a, w_proj_b, w_out, g_norm2, w_router_grp, b_router_grp, w_router_exp, b_router_exp, w_exp_gate, w_exp_up, w_exp_down, g_final):
    bp, tp, _ = x_prompt.shape
    bs, ts, _ = x_sample.shape
    assert w_mod.shape[0] == 1, "single trunk layer"
    assert tp == CHUNK and ts % CHUNK == 0 and ts == ML_ROWS

    cvec = jnp.zeros((8, D_MODEL), F32).at[0].set(c_ctx).at[1:1 + bs].set(c)
    mod3 = _modulation(cvec, w_mod[0], b_mod[0])

    wi, bi = w_in[0], b_in[0]
    d = D_MODEL
    o_q, o_k, o_v, o_o, o_g, o_p, o_ga, o_gb = 0, d, 2 * d, 3 * d, 4 * d, 4 * d + 32, 4 * d + 32 + 512, 4 * d + 32 + 512 + d
    order = [(o_q, d), (o_k, d), (o_v, d), (o_o, d), (o_ga, d), (o_gb, d), (o_p, 512)]
    wmain = _pack_columns(wi, order)
    bmain = jnp.concatenate([bi[s:s + l] for s, l in order]).reshape(1, D_MAIN)
    colscale = jnp.ones((D_MAIN,), F32).at[SLAB_K * LANES:(SLAB_K + N_HEADS) * LANES].set(HEAD_DIM ** -0.5)
    colscale = colscale.reshape(1, D_MAIN)
    gwt = _take_columns_t(wi, o_g, 32)
    gb = bi[o_g:o_g + 32]
    i_f, f_f, i_b, f_b = (slice(0, 8), slice(8, 16), slice(16, 24), slice(24, 32))
    pad_w = jnp.zeros((LANES - 2 * N_HD, d), F32)
    pad_b = jnp.zeros((LANES - 2 * N_HD,), F32)
    wg = jnp.concatenate([gwt[f_f], gwt[f_b], gwt[i_f], gwt[i_b], pad_w], axis=0).T
    bg = jnp.concatenate([gb[f_f], gb[f_b], gb[i_f], gb[i_b], pad_b]).reshape(1, LANES)
    wg = jnp.concatenate(_split_bf16(wg), axis=1)

    wr = jnp.concatenate([w_router_grp[0], w_router_exp[0].reshape(d, N_EXPERTS),
                          jnp.zeros((d, LANES - ROUTER_ROWS), F32)], axis=1)
    br = jnp.concatenate([b_router_grp[0], b_router_exp[0].reshape(N_EXPERTS),
                          jnp.zeros((LANES - ROUTER_ROWS,), F32)]).reshape(1, LANES)
    wr = jnp.concatenate(_split_bf16(wr), axis=1)

    w = dict(
        g_norm1=g_norm1[0].reshape(1, d), wmain=wmain, bmain=bmain, colscale=colscale,
        w_gates=wg, b_gates=bg,
        gn3=gn_gain[0].reshape(N_HEADS, 1, HEAD_DIM),
        wpg=w_pool_grp[0].astype(BF16), pscale=pool_scale[0].reshape(N_GROUPS, 1, LANES),
        wa=w_proj_a[0].astype(BF16), wb=w_proj_b[0].astype(BF16), wo=w_out[0].astype(BF16),
        g_norm2=g_norm2[0].reshape(1, d), wr=wr, br=br,
        wg=w_exp_gate[0], wu=w_exp_up[0], wd=w_exp_down[0],
        g_final=g_final.reshape(1, d),
    )

    y_prompt, (_, cst, nst), cs_prompt = _stream(x_prompt, mod3, lambda r: 0, tp, None, None, w)
    n_rep = jnp.broadcast_to(state_n[..., None], state_C.shape)
    m0 = jnp.pad(state_m[:, 0].reshape(bs, N_HD), ((0, 0), (0, LANES - N_HD))).reshape(bs, 1, LANES)
    y_sample, _, _ = _stream(x_sample, mod3, lambda r: 1 + r // ts, ts, ts // GRID_W, (state_C, n_rep, m0), w)

    new_c = cst
    new_n = nst.reshape(bp, 1, 2, N_HEADS, HEAD_DIM)
    new_m = cs_prompt[:, 1, :N_HD].reshape(bp, 1, 2, N_HEADS)
    return (y_prompt, y_sample, new_c, new_n, new_m)
```

```python
import functools

import numpy as np
import jax
import jax.numpy as jnp
from jax import lax
from jax.experimental import pallas as pl
from jax.experimental.pallas import tpu as pltpu
from jax.experimental.pallas import tpu_sc as plsc

F32 = jnp.float32
BF16 = jnp.bfloat16

D_MODEL = 1024
N_HEADS = 8
HEAD_DIM = 128
GRID_W = 64
POOL_WINDOWS = (2, 4, 8, 16)
N_GROUPS = 4
EXPERTS_PER_GROUP = 4
N_EXPERTS = 16
D_EXPERT = 512
EPS = 1e-6
NEG = -1e30

LANES = 128
CHUNK = 256
N_SLABS = 52
SLAB_Q, SLAB_K, SLAB_V, SLAB_O, SLAB_GA, SLAB_GB, SLAB_POOL = 0, 8, 16, 24, 32, 40, 48
D_MAIN = N_SLABS * LANES
VMEM_LIMIT = 56 * 1024 * 1024


def _cparams(sem):
    return pltpu.CompilerParams(dimension_semantics=sem, vmem_limit_bytes=VMEM_LIMIT)


def _const_spec(shape):
    nd = len(shape)
    return pl.BlockSpec(shape, lambda *_: (0,) * nd, pipeline_mode=pl.Buffered(1))


def _split_bf16(x):
    hi = x.astype(BF16)
    lo = (x - hi.astype(F32)).astype(BF16)
    return hi, lo


def _dot(a, b):
    return jnp.dot(a, b, preferred_element_type=F32)


def _dot3(a, b_hi, b_lo):
    a_hi, a_lo = _split_bf16(a)
    return _dot(a_hi, b_hi) + _dot(a_hi, b_lo) + _dot(a_lo, b_hi)


def _dot3_packed(a_hi, a_lo, b_hilo):
    first = _dot(a_hi, b_hilo)
    return first[:, :LANES] + first[:, LANES:] + _dot(a_lo, b_hilo[:, :LANES])


def _rms_scale(x):
    return lax.rsqrt(jnp.mean(x * x, axis=-1, keepdims=True) + EPS)


def _log_sigmoid(x):
    return jnp.minimum(x, 0.0) - jnp.log1p(jnp.exp(-jnp.abs(x)))


def _mod_kernel(c_ref, w_ref, b_ref, o_ref):
    c = c_ref[...]
    s = c * jax.nn.sigmoid(c)
    w_hi, w_lo = _split_bf16(w_ref[...])
    o_ref[...] = _dot3(s, w_hi, w_lo) + b_ref[...]


def _modulation(cvec, w_mod, b_mod):
    tn = 2048
    out = pl.pallas_call(
        _mod_kernel,
        out_shape=jax.ShapeDtypeStruct((8, 6 * D_MODEL), F32),
        grid=(6 * D_MODEL // tn,),
        in_specs=[
            pl.BlockSpec((8, D_MODEL), lambda j: (0, 0)),
            pl.BlockSpec((D_MODEL, tn), lambda j: (0, j)),
            pl.BlockSpec((1, tn), lambda j: (0, j)),
        ],
        out_specs=pl.BlockSpec((8, tn), lambda j: (0, j)),
        compiler_params=_cparams(("parallel",)),
        name="modulation",
    )(cvec, w_mod, b_mod.reshape(1, -1))
    return out.reshape(8, 6, D_MODEL)


PACK_COLS = 512


def _pack_kernel(wt_ref, o_ref):
    o_ref[...] = wt_ref[...].T.astype(BF16)


def _pack_columns(w, order):
    k, n = w.shape
    starts = []
    for src, width in order:
        assert width % PACK_COLS == 0
        starts += [src + j for j in range(0, width, PACK_COLS)]
    n_out = len(starts) * PACK_COLS

    unit = 32
    assert all(s % unit == 0 for s in starts)

    def src_row(j):
        row = jnp.int32(starts[-1] // unit)
        for idx in range(len(starts) - 2, -1, -1):
            row = jnp.where(j == idx, starts[idx] // unit, row)
        return row * unit

    return pl.pallas_call(
        _pack_kernel,
        out_shape=jax.ShapeDtypeStruct((k, n_out), BF16),
        grid=(len(starts),),
        in_specs=[pl.BlockSpec((pl.Element(PACK_COLS), pl.Element(k)), lambda j: (src_row(j), 0))],
        out_specs=pl.BlockSpec((k, PACK_COLS), lambda j: (0, j)),
        compiler_params=_cparams(("parallel",)),
        name="pack_columns",
    )(w.T)


def _copy_kernel(w_ref, o_ref):
    o_ref[...] = w_ref[...]


def _take_columns_t(w, start, width):
    k = w.shape[0]
    return pl.pallas_call(
        _copy_kernel,
        out_shape=jax.ShapeDtypeStruct((width, k), w.dtype),
        grid=(1,),
        in_specs=[pl.BlockSpec((pl.Element(width), pl.Element(k)), lambda j: (start, 0))],
        out_specs=pl.BlockSpec((width, k), lambda j: (0, 0)),
        name="take_columns",
    )(w.T)


IN_TM = 512
IN_TN = 512
LOG2E = 1.4426950408889634
N_HD = 2 * N_HEADS


def _fwd_lanes(shape):
    lane = lax.broadcasted_iota(jnp.int32, shape, len(shape) - 1)
    return (lane & N_HEADS) == 0


def _inproj_kernel(x_ref, mod_ref, g_ref, w_ref, b_ref, cs_ref, wg_ref, bg_ref,
                   p3_ref, bcum_ref, imb_ref, amax_ref):
    x = x_ref[...]
    h = x * _rms_scale(x) * g_ref[...]
    h = h * (1.0 + mod_ref[0, 1:2, :]) + mod_ref[0, 0:1, :]
    hb = h.astype(BF16)
    for n in range(D_MAIN // IN_TN):
        sl = slice(n * IN_TN, (n + 1) * IN_TN)
        acc = (_dot(hb, w_ref[:, sl]) + b_ref[:, sl]) * cs_ref[:, sl]
        accb = acc.astype(BF16)
        for j in range(IN_TN // LANES):
            p3_ref[n * (IN_TN // LANES) + j] = accb[:, j * LANES:(j + 1) * LANES]

    h_lo = (h - hb.astype(F32)).astype(BF16)
    gates = _dot3_packed(hb, h_lo, wg_ref[...]) + bg_ref[...]
    ls = _log_sigmoid(gates)
    gi = pltpu.roll(gates, LANES - N_HD, 1)
    row = lax.broadcasted_iota(jnp.int32, (CHUNK, LANES), 0)
    fwd = _fwd_lanes((CHUNK, LANES))
    for c in range(IN_TM // CHUNK):
        rows = slice(c * CHUNK, (c + 1) * CHUNK)
        cf = ls[rows]
        cb = cf
        s = 1
        while s < CHUNK:
            cf = cf + jnp.where(row >= s, pltpu.roll(cf, s, 0), 0.0)
            cb = cb + jnp.where(row < CHUNK - s, pltpu.roll(cb, CHUNK - s, 0), 0.0)
            s *= 2
        b = jnp.where(fwd, cf, cb)
        imb = gi[rows] - b
        mf = imb
        mb = imb
        s = 1
        while s < CHUNK:
            mf = jnp.maximum(mf, jnp.where(row >= s, pltpu.roll(mf, s, 0), -jnp.inf))
            mb = jnp.maximum(mb, jnp.where(row < CHUNK - s, pltpu.roll(mb, CHUNK - s, 0), -jnp.inf))
            s *= 2
        bcum_ref[rows, :] = b
        imb_ref[rows, :] = imb
        amax_ref[rows, :] = b + jnp.where(fwd, mf, mb)


def _in_projection(x2d, mod3, mod_row_fn, g_norm1, wmain, bmain, colscale, wg, bg):
    n = x2d.shape[0]
    return pl.pallas_call(
        _inproj_kernel,
        out_shape=(jax.ShapeDtypeStruct((N_SLABS, n, LANES), BF16),) + (jax.ShapeDtypeStruct((n, LANES), F32),) * 3,
        grid=(n // IN_TM,),
        in_specs=[
            pl.BlockSpec((IN_TM, D_MODEL), lambda i: (i, 0)),
            pl.BlockSpec((1, 6, D_MODEL), lambda i: (mod_row_fn(i * IN_TM), 0, 0)),
            _const_spec((1, D_MODEL)),
            _const_spec((D_MODEL, D_MAIN)),
            _const_spec((1, D_MAIN)),
            _const_spec((1, D_MAIN)),
            _const_spec((D_MODEL, 2 * LANES)),
            _const_spec((1, LANES)),
        ],
        out_specs=(pl.BlockSpec((N_SLABS, IN_TM, LANES), lambda i: (0, i, 0)),)
        + (pl.BlockSpec((IN_TM, LANES), lambda i: (i, 0)),) * 3,
        compiler_params=_cparams(("parallel",)),
        name="in_projection",
    )(x2d, mod3, g_norm1, wmain, bmain, colscale, wg, bg)


GP_ROWS = 1024
GP_NCH = GP_ROWS // CHUNK
Q_C2, Q_EM, Q_WI, Q_WC = 0, 1, 2, 3


def _gate_prep_kernel(*refs, carry):
    if carry:
        b_ref, imb_ref, a_ref, m0_ref, cols_ref, rw_ref, cs_ref = refs
    else:
        b_ref, imb_ref, a_ref, cols_ref, rw_ref, cs_ref = refs
    fwd1 = _fwd_lanes((1, LANES))
    lane = lax.broadcasted_iota(jnp.int32, (CHUNK, LANES), 1)
    neg = jnp.full((1, LANES), NEG, F32)

    def at_scan_end(ref, c):
        return jnp.where(fwd1, ref[(c + 1) * CHUNK - 1:(c + 1) * CHUNK, :], ref[c * CHUNK:c * CHUNK + 1, :])

    b_last = [at_scan_end(b_ref, c) for c in range(GP_NCH)]
    a_last = [at_scan_end(a_ref, c) for c in range(GP_NCH)]
    m_start = m0_ref[0] if carry else neg
    m_prev = [[None] * GP_NCH, [None] * GP_NCH]
    m_new = [[None] * GP_NCH, [None] * GP_NCH]
    for d, order in enumerate((range(GP_NCH), range(GP_NCH - 1, -1, -1))):
        m = m_start
        for c in order:
            m_prev[d][c] = m
            m_new[d][c] = jnp.maximum(b_last[c] + m, a_last[c])
            m = m_new[d][c] if carry else neg

    row_src = []
    for c in range(GP_NCH):
        rows = slice(c * CHUNK, (c + 1) * CHUNK)
        mp = jnp.where(fwd1, m_prev[0][c], m_prev[1][c])
        mn = jnp.where(fwd1, m_new[0][c], m_new[1][c])
        b = b_ref[rows, :]
        imb = imb_ref[rows, :]
        inter = b + mp
        m_t = jnp.maximum(inter, a_ref[rows, :])
        c2 = (b - m_t) * LOG2E
        em = jnp.exp(-m_t)
        wc = jnp.exp(b_last[c] - mn + imb)
        packed = jnp.where(lane < N_HD, c2, pltpu.roll(em, N_HD * Q_EM, 1))
        if carry:
            wi = jnp.exp(inter - m_t)
            packed = jnp.where(lane < N_HD * Q_WI, packed, pltpu.roll(wi, N_HD * Q_WI, 1))
        cols_ref[rows, :] = jnp.where(lane < N_HD * Q_WC, packed, pltpu.roll(wc, N_HD * Q_WC, 1))
        row_src.append(jnp.where(lane < N_HD, imb * (-LOG2E), pltpu.roll(wc, N_HD, 1)))
        cs_ref[c, 0:1, :] = jnp.exp(b_last[c] + mp - mn)
        cs_ref[c, 1:2, :] = mn

    rows_t = jnp.concatenate(row_src, axis=0).T
    for r in range(2 * N_HD):
        rw_ref[r] = rows_t[r:r + 1]


def _gate_prep(bcum, imb, amax, m0=None):
    n = bcum.shape[0]
    carry = m0 is not None
    blk = pl.BlockSpec((GP_ROWS, LANES), lambda i: (i, 0))
    in_specs = [blk, blk, blk]
    args = [bcum, imb, amax]
    if carry:
        in_specs.append(pl.BlockSpec((1, 1, LANES), lambda i: (i, 0, 0)))
        args.append(m0)
    return pl.pallas_call(
        functools.partial(_gate_prep_kernel, carry=carry),
        out_shape=(jax.ShapeDtypeStruct((n, LANES), F32), jax.ShapeDtypeStruct((2 * N_HD, 1, n), F32),
                   jax.ShapeDtypeStruct((n // CHUNK, 2, LANES), F32)),
        grid=(n // GP_ROWS,),
        in_specs=in_specs,
        out_specs=(blk, pl.BlockSpec((2 * N_HD, 1, GP_ROWS), lambda i: (0, 0, i)),
                   pl.BlockSpec((GP_NCH, 2, LANES), lambda i: (i, 0, 0))),
        compiler_params=_cparams(("parallel",)),
        name="gate_prep",
    )(*args)


ML_ROWS = 1024
ML_ROWS_FREE = 4096


def _pick_col(blk, lane_iota, idx):
    return jnp.sum(jnp.where(lane_iota == idx, blk, 0.0), axis=1, keepdims=True)


def _mlstm_kernel(*refs, carry, n_chunks):
    if carry:
        (q_ref, k_ref, v_ref, o_ref, cols_ref, r2f_ref, r2b_ref, gn_ref, cs_ref, c0_ref, n0_ref,
         ya_ref, s_scr, h_scr) = refs
    else:
        (q_ref, k_ref, v_ref, o_ref, cols_ref, r2f_ref, r2b_ref, gn_ref, wrf_ref, wrb_ref,
         ya_ref, cst_ref, nst_ref, s_scr, h_scr) = refs
    head = pl.program_id(1)
    lane = lax.broadcasted_iota(jnp.int32, (CHUNK, LANES), 1)
    lane1 = lax.broadcasted_iota(jnp.int32, (1, LANES), 1)
    r_idx = lax.broadcasted_iota(jnp.int32, (CHUNK, CHUNK), 0)
    c_idx = lax.broadcasted_iota(jnp.int32, (CHUNK, CHUNK), 1)
    ones = jnp.ones((CHUNK, HEAD_DIM), BF16)
    tn_dims = (((0,), (0,)), ((), ()))

    for d in range(2):
        mask = (c_idx <= r_idx) if d == 0 else (c_idx >= r_idx)
        r2_ref = r2f_ref if d == 0 else r2b_ref
        base = N_HEADS * d + head
        if carry:
            cn = jnp.concatenate([c0_ref[0, 0, d, 0], n0_ref[0, 0, d, 0]], axis=1)
        order = range(n_chunks) if d == 0 else range(n_chunks - 1, -1, -1)
        for step, c in enumerate(order):
            rows = slice(c * CHUNK, (c + 1) * CHUNK)
            q = q_ref[0, rows, :]
            k = k_ref[0, rows, :]
            v = v_ref[0, rows, :]
            v1 = jnp.concatenate([v, ones], axis=1)
            if d == 0:
                s = lax.dot_general(q, k, (((1,), (1,)), ((), ())), preferred_element_type=F32)
                s_scr[rows, :] = s
            else:
                s = s_scr[rows, :]
            blk = cols_ref[rows, :]
            arg = _pick_col(blk, lane, base + N_HD * Q_C2) - r2_ref[0, :, rows]
            sw = s * jnp.exp2(jnp.where(mask, arg, NEG))
            nd = _dot(sw.astype(BF16), v1)
            if carry:
                nd = nd + _pick_col(blk, lane, base + N_HD * Q_WI) * _dot(q, cn.astype(BF16))
            hdir = nd[:, :HEAD_DIM] / jnp.maximum(jnp.abs(nd[:, HEAD_DIM:]), _pick_col(blk, lane, base + N_HD * Q_EM))
            if d == 0:
                h_scr[rows, :] = hdir
            else:
                h_scr[rows, :] = h_scr[rows, :] + hdir

            if carry and step == n_chunks - 1:
                continue
            w_col = _pick_col(blk, lane, base + N_HD * Q_WC)
            if carry:
                u = lax.dot_general(k, (w_col * v1.astype(F32)).astype(BF16), tn_dims,
                                    preferred_element_type=F32)
                decay = jnp.sum(jnp.where(lane1 == base, cs_ref[c, 0:1, :], 0.0), axis=1, keepdims=True)
                cn = decay * cn + u
            else:
                w_row = (wrf_ref if d == 0 else wrb_ref)[0, :, rows]
                w8 = jnp.broadcast_to(w_row, (8, CHUNK)).astype(BF16)
                cst_ref[c, 0, d, 0] = lax.dot_general(k, (w_col * v.astype(F32)).astype(BF16), tn_dims,
                                                      preferred_element_type=F32)
                nst_ref[c, d, 0] = _dot(w8, k)[0:1, :]

    hh = h_scr[...]
    hn = hh * _rms_scale(hh) * gn_ref[0]
    ya_ref[0] = (hn * jax.nn.sigmoid(o_ref[0].astype(F32))).astype(BF16)


def _mlstm(p3, cols, rows, cs, gn3, state=None):
    n = p3.shape[1]
    carry = state is not None
    rows_step = ML_ROWS if carry else ML_ROWS_FREE
    n_chunks = rows_step // CHUNK
    nblk = n // rows_step

    def slab(base):
        return pl.BlockSpec((1, rows_step, LANES), lambda g, h: (base + h, g, 0))

    def row(base):
        return pl.BlockSpec((1, 1, rows_step), lambda g, h: (base + h, 0, g))

    in_specs = [
        slab(SLAB_Q), slab(SLAB_K), slab(SLAB_V), slab(SLAB_O),
        pl.BlockSpec((rows_step, LANES), lambda g, h: (g, 0)),
        row(0), row(N_HEADS),
        pl.BlockSpec((1, 1, LANES), lambda g, h: (h, 0, 0)),
    ]
    args = [p3, p3, p3, p3, cols, rows, rows, gn3]
    ya_shape = jax.ShapeDtypeStruct((N_HEADS, n, LANES), BF16)
    ya_spec = pl.BlockSpec((1, rows_step, LANES), lambda g, h: (h, g, 0))
    state_spec = pl.BlockSpec((1, 1, 2, 1, HEAD_DIM, HEAD_DIM), lambda g, h: (g, 0, 0, h, 0, 0))
    if carry:
        c0, n0 = state
        in_specs += [pl.BlockSpec((n_chunks, 2, LANES), lambda g, h: (g, 0, 0)), state_spec, state_spec]
        args += [cs, c0, n0]
        out_shape = ya_shape
        out_specs = ya_spec
    else:
        in_specs += [row(N_HD), row(N_HD + N_HEADS)]
        args += [rows, rows]
        nseq = n // CHUNK
        out_shape = (ya_shape,
                     jax.ShapeDtypeStruct((nseq, 1, 2, N_HEADS, HEAD_DIM, HEAD_DIM), F32),
                     jax.ShapeDtypeStruct((nseq, 2, N_HEADS, 1, HEAD_DIM), F32))
        out_specs = (ya_spec,
                     pl.BlockSpec((n_chunks, 1, 2, 1, HEAD_DIM, HEAD_DIM), lambda g, h: (g, 0, 0, h, 0, 0)),
                     pl.BlockSpec((n_chunks, 2, 1, 1, HEAD_DIM), lambda g, h: (g, 0, h, 0, 0)))
    return pl.pallas_call(
        functools.partial(_mlstm_kernel, carry=carry, n_chunks=n_chunks),
        out_shape=out_shape,
        grid=(nblk, N_HEADS),
        in_specs=in_specs,
        out_specs=out_specs,
        scratch_shapes=[pltpu.VMEM((rows_step, CHUNK), F32), pltpu.VMEM((rows_step, HEAD_DIM), F32)],
        compiler_params=_cparams(("parallel", "parallel")),
        name="mlstm_latent" if carry else "mlstm_prompt",
    )(*args)


def _pool_kernel(u_ref, a_ref, ic_ref, wpg_ref, ps_ref, yb_ref, *, seq_len, n_seq, rows_2d):
    piece = seq_len if rows_2d is None else GRID_W
    n_piece = n_seq if rows_2d is None else rows_2d
    for g in range(N_GROUPS):
        u = jnp.concatenate([u_ref[g, i * piece:(i + 1) * piece, :] for i in range(n_piece)], axis=1)
        win = _dot(a_ref[g], u)
        if rows_2d is None:
            inv_cnt = jnp.concatenate([ic_ref[g]] * n_piece, axis=1)
        else:
            run = [jnp.zeros((piece, LANES), F32)]
            for r in range(rows_2d):
                run.append(run[-1] + win[:, r * LANES:(r + 1) * LANES])
            half = POOL_WINDOWS[g] // 2
            win = jnp.concatenate([run[min(r - half + POOL_WINDOWS[g], rows_2d)] - run[max(r - half, 0)]
                                   for r in range(rows_2d)], axis=1)
            inv_cnt = jnp.concatenate([ic_ref[g, r * piece:(r + 1) * piece, :] for r in range(rows_2d)], axis=1)
        p = (win * inv_cnt - u.astype(F32)).astype(BF16)
        p = jnp.concatenate([p[:, i * LANES:(i + 1) * LANES] for i in range(n_piece)], axis=0)
        yb_ref[g] = (_dot(p, wpg_ref[g]) * ps_ref[g]).astype(BF16)


def _pool(p3, amat, invc, wpg, pscale, seq_len, n_seq, rows_2d):
    n = p3.shape[1]
    rows = seq_len * n_seq
    return pl.pallas_call(
        functools.partial(_pool_kernel, seq_len=seq_len, n_seq=n_seq, rows_2d=rows_2d),
        out_shape=jax.ShapeDtypeStruct((N_GROUPS, n, LANES), BF16),
        grid=(n // rows,),
        in_specs=[
            pl.BlockSpec((N_GROUPS, rows, LANES), lambda i: (SLAB_POOL // N_GROUPS, i, 0)),
            _const_spec(amat.shape),
            _const_spec((N_GROUPS, seq_len, LANES)),
            _const_spec((N_GROUPS, LANES, LANES)),
            _const_spec((N_GROUPS, 1, LANES)),
        ],
        out_specs=pl.BlockSpec((N_GROUPS, rows, LANES), lambda i: (0, i, 0)),
        compiler_params=_cparams(("parallel",)),
        name="pool",
    )(p3, amat, invc, wpg, pscale)


def _window_matrix(size, w):
    idx = np.arange(size)
    lo = np.clip(idx - w // 2, 0, size)
    hi = np.clip(idx - w // 2 + w, 0, size)
    s = np.arange(size)[None, :]
    a = ((s >= lo[:, None]) & (s < hi[:, None])).astype(np.float32)
    return a, (hi - lo).astype(np.float32)


def _pool_constants(seq_len, rows_2d):
    mats, invs = [], []
    for w in POOL_WINDOWS:
        if rows_2d is None:
            a, cnt = _window_matrix(seq_len, w)
        else:
            _, cr = _window_matrix(rows_2d, w)
            a, cc = _window_matrix(GRID_W, w)
            cnt = np.kron(cr, cc)
        mats.append(a)
        invs.append(np.repeat((1.0 / cnt)[:, None], LANES, axis=1))
    return (jnp.asarray(np.stack(mats), dtype=BF16), jnp.asarray(np.stack(invs), dtype=F32))


MIX_TM = 1024


def _mix_kernel(x_ref, mod_ref, ya_ref, yb_ref, ga_ref, gb_ref, wa_ref, wb_ref, wo_ref,
                g2_ref, wr_ref, br_ref, x1_ref, h2_ref, info_ref):
    ya = jnp.concatenate([ya_ref[i] for i in range(N_HEADS)], axis=1)
    yb = jnp.concatenate([yb_ref[i] for i in range(N_GROUPS)], axis=1)
    ga = jnp.concatenate([ga_ref[i] for i in range(8)], axis=1).astype(F32)
    gb = jnp.concatenate([gb_ref[i] for i in range(8)], axis=1).astype(F32)
    merged = jax.nn.sigmoid(ga) * _dot(ya, wa_ref[...]) + jax.nn.sigmoid(gb) * _dot(yb, wb_ref[...])
    out = _dot(merged.astype(BF16), wo_ref[...])
    x1 = x_ref[...] + mod_ref[0, 2:3, :] * out
    x1_ref[...] = x1

    h2 = x1 * _rms_scale(x1) * g2_ref[...]
    h2 = h2 * (1.0 + mod_ref[0, 4:5, :]) + mod_ref[0, 3:4, :]
    h_hi, h_lo = _split_bf16(h2)
    packed = _pack_bf16_pairs(h_hi.astype(F32))
    h2_ref[0:MIX_TM, :] = packed[:, :SC_ROW]
    h2_ref[MIX_TM:2 * MIX_TM, :] = packed[:, SC_ROW:]
    logits = _dot3_packed(h_hi, h_lo, wr_ref[...]) + br_ref[...]
    first, second, w1, w2 = _route_rows(logits.T)
    zero = jnp.zeros_like(w1)
    info_ref[...] = jnp.concatenate([first, second, w1, w2, zero, zero, zero, zero], axis=0)


def _mix(x2d, mod3, mod_row_fn, ya, yb, p3, wa, wb, wo, g_norm2, wr, br):
    n = x2d.shape[0]
    return pl.pallas_call(
        _mix_kernel,
        out_shape=(jax.ShapeDtypeStruct((n, D_MODEL), F32),
                   jax.ShapeDtypeStruct((2 * n, SC_ROW), jnp.int32), jax.ShapeDtypeStruct((8, n), F32)),
        grid=(n // MIX_TM,),
        in_specs=[
            pl.BlockSpec((MIX_TM, D_MODEL), lambda i: (i, 0)),
            pl.BlockSpec((1, 6, D_MODEL), lambda i: (mod_row_fn(i * MIX_TM), 0, 0)),
            pl.BlockSpec((N_HEADS, MIX_TM, LANES), lambda i: (0, i, 0)),
            pl.BlockSpec((N_GROUPS, MIX_TM, LANES), lambda i: (0, i, 0)),
            pl.BlockSpec((8, MIX_TM, LANES), lambda i: (SLAB_GA // 8, i, 0)),
            pl.BlockSpec((8, MIX_TM, LANES), lambda i: (SLAB_GB // 8, i, 0)),
            _const_spec((D_MODEL, D_MODEL)),
            _const_spec((N_GROUPS * LANES, D_MODEL)),
            _const_spec((D_MODEL, D_MODEL)),
            _const_spec((1, D_MODEL)),
            _const_spec((D_MODEL, 2 * LANES)),
            _const_spec((1, LANES)),
        ],
        out_specs=(pl.BlockSpec((MIX_TM, D_MODEL), lambda i: (i, 0)),
                   pl.BlockSpec((2 * MIX_TM, SC_ROW), lambda i: (i, 0)),
                   pl.BlockSpec((8, MIX_TM), lambda i: (0, i))),
        compiler_params=_cparams(("parallel",)),
        name="mix",
    )(x2d, mod3, ya, yb, p3, p3, wa, wb, wo, g_norm2, wr, br)


PLAN_TB = 1024
ROUTER_ROWS = 4 + N_EXPERTS
SLOT_TM = 1024
SC_ROW = 256
SC_WIN = 128
HI_MASK = -65536


def _pack_bf16_pairs(x):
    bits = pltpu.bitcast(x, jnp.int32)
    half = D_MODEL // 2
    return jnp.bitwise_or(jnp.bitwise_and(lax.shift_right_logical(bits[:, :half], 16), 0xFFFF),
                          jnp.bitwise_and(bits[:, half:], HI_MASK))


def _unpack_bf16_pairs(lo_words, hi_words):
    def low(w):
        return pltpu.bitcast(lax.shift_left(w, 16), F32)

    def high(w):
        return pltpu.bitcast(jnp.bitwise_and(w, HI_MASK), F32)

    return jnp.concatenate([low(lo_words), low(hi_words), high(lo_words), high(hi_words)], axis=1).astype(BF16)


def _route_rows(lt):
    lg = [lt[i:i + 1] for i in range(N_GROUPS)]
    mx = jnp.maximum(jnp.maximum(lg[0], lg[1]), jnp.maximum(lg[2], lg[3]))
    p_top = 1.0 / (jnp.exp(lg[0] - mx) + jnp.exp(lg[1] - mx) + jnp.exp(lg[2] - mx) + jnp.exp(lg[3] - mx))
    gidx = jnp.zeros(lg[0].shape, jnp.int32)
    best = lg[0]
    for i in range(1, N_GROUPS):
        upd = lg[i] > best
        gidx = jnp.where(upd, i, gidx)
        best = jnp.where(upd, lg[i], best)

    def expert_row(g, e):
        r = N_GROUPS + g * EXPERTS_PER_GROUP + e
        return lt[r:r + 1]

    le = [jnp.where(gidx == 0, expert_row(0, e),
                    jnp.where(gidx == 1, expert_row(1, e),
                              jnp.where(gidx == 2, expert_row(2, e), expert_row(3, e))))
          for e in range(EXPERTS_PER_GROUP)]
    v1 = jnp.maximum(jnp.maximum(le[0], le[1]), jnp.maximum(le[2], le[3]))
    i1 = jnp.full(v1.shape, EXPERTS_PER_GROUP - 1, jnp.int32)
    for e in range(EXPERTS_PER_GROUP - 2, -1, -1):
        i1 = jnp.where(le[e] == v1, e, i1)
    le2 = [jnp.where(i1 == e, -jnp.inf, le[e]) for e in range(EXPERTS_PER_GROUP)]
    v2 = jnp.maximum(jnp.maximum(le2[0], le2[1]), jnp.maximum(le2[2], le2[3]))
    i2 = jnp.full(v2.shape, EXPERTS_PER_GROUP - 1, jnp.int32)
    for e in range(EXPERTS_PER_GROUP - 2, -1, -1):
        i2 = jnp.where(le2[e] == v2, e, i2)
    e2 = jnp.exp(v2 - v1)
    first = (gidx * EXPERTS_PER_GROUP + i1).astype(F32)
    second = (gidx * EXPERTS_PER_GROUP + i2).astype(F32)
    return first, second, p_top / (1.0 + e2), p_top * e2 / (1.0 + e2)


def _plan_kernel(info_ref, before_ref, sidx_ref, gidx_ref, tmap_ref, rank_scr, count_scr, *, n_blocks):
    step = pl.program_id(0)
    expert_id = lax.broadcasted_iota(jnp.int32, (N_EXPERTS, PLAN_TB), 0).astype(F32)

    @pl.when(step == 0)
    def _():
        count_scr[...] = jnp.zeros_like(count_scr)

    @pl.when(step < n_blocks)
    def _():
        cols = pl.ds(pl.multiple_of(step * PLAN_TB, PLAN_TB), PLAN_TB)
        sel = jnp.where((info_ref[0:1, cols] == expert_id) | (info_ref[1:2, cols] == expert_id), 1.0, 0.0)
        selb = sel.astype(BF16)
        rank_scr[:, cols] = _dot(selb, before_ref[...]) + count_scr[:, 0:1]
        count_scr[...] += _dot(selb, jnp.ones((PLAN_TB, LANES), BF16))

    @pl.when(step == n_blocks)
    def _():
        counts = count_scr[...]
        padded = jnp.floor((counts + (SLOT_TM - 1)) * (1.0 / SLOT_TM)) * SLOT_TM
        starts, ends, run = [], [], jnp.zeros((1, LANES), F32)
        for e in range(N_EXPERTS):
            starts.append(run)
            run = run + padded[e:e + 1]
            ends.append(run)
        n = info_ref.shape[1]
        shift = SLOT_TM.bit_length() - 1
        for k in range(2):
            chosen = info_ref[k:k + 1, :]
            slot = jnp.zeros((1, n), F32)
            for e in range(N_EXPERTS):
                slot = jnp.where(chosen == float(e), starts[e][:, 0:1] + rank_scr[e:e + 1, :], slot)
            slot = slot.astype(jnp.int32)
            low = jnp.left_shift(jnp.right_shift(slot, shift), shift + 1) + jnp.bitwise_and(slot, SLOT_TM - 1)
            for half in range(2):
                rows = low + half * SLOT_TM
                for t in range(n // MIX_TM):
                    sidx_ref[k, t, half:half + 1, :] = rows[:, t * MIX_TM:(t + 1) * MIX_TM]
                for t in range(n // FIN_TM):
                    gidx_ref[t, k, half:half + 1, :] = rows[:, t * FIN_TM:(t + 1) * FIN_TM]
        tile_start = lax.broadcasted_iota(jnp.int32, (1, LANES), 1).astype(F32) * SLOT_TM
        used = jnp.where(tile_start < run, 1.0, 0.0)
        tile_start = jnp.minimum(tile_start, run - SLOT_TM)
        owner = jnp.zeros((1, LANES), F32)
        for e in range(N_EXPERTS - 1):
            owner = owner + jnp.where(ends[e] <= tile_start, 1.0, 0.0)
        tile = tile_start * (1.0 / SLOT_TM)
        tmap_ref[...] = jnp.concatenate([owner, used, tile] + [jnp.zeros((1, LANES), F32)] * 5,
                                        axis=0).astype(jnp.int32)


def _plan(info):
    n = info.shape[1]
    n_blocks = n // PLAN_TB
    t_i = lax.broadcasted_iota(jnp.int32, (PLAN_TB, PLAN_TB), 0)
    t_j = lax.broadcasted_iota(jnp.int32, (PLAN_TB, PLAN_TB), 1)
    before = (t_i < t_j).astype(BF16)
    return pl.pallas_call(
        functools.partial(_plan_kernel, n_blocks=n_blocks),
        out_shape=(jax.ShapeDtypeStruct((2, n // MIX_TM, 2, MIX_TM), jnp.int32),
                   jax.ShapeDtypeStruct((n // FIN_TM, 2, 2, FIN_TM), jnp.int32),
                   jax.ShapeDtypeStruct((8, LANES), jnp.int32)),
        grid=(n_blocks + 1,),
        in_specs=[_const_spec((8, n)), _const_spec((PLAN_TB, PLAN_TB))],
        out_specs=(pl.BlockSpec((2, n // MIX_TM, 2, MIX_TM), lambda i: (0, 0, 0, 0)),
                   pl.BlockSpec((n // FIN_TM, 2, 2, FIN_TM), lambda i: (0, 0, 0, 0)),
                   pl.BlockSpec((8, LANES), lambda i: (0, 0))),
        scratch_shapes=[pltpu.VMEM((N_EXPERTS, n), F32), pltpu.VMEM((N_EXPERTS, LANES), F32)],
        compiler_params=_cparams(("arbitrary",)),
        name="plan",
    )(info, before)


def _sc_mesh():
    return plsc.VectorSubcoreMesh(core_axis_name="core", subcore_axis_name="subcore")


def _sc_scatter_rows(x, idx, n_out):
    n_idx = idx.shape[0]
    src_blocks = x.shape[0] // SC_WIN

    @pl.kernel(out_type=jax.ShapeDtypeStruct((n_out, SC_ROW), x.dtype), mesh=_sc_mesh(), scratch_types=[])
    def scatter_kernel(x_hbm, i_hbm, o_hbm):
        def body(x_vmem, i_vmem):
            pltpu.sync_copy(x_vmem, o_hbm.at[i_vmem.at[0]])

        pltpu.emit_pipeline(
            body,
            grid=(n_idx // SC_WIN,),
            in_specs=[pl.BlockSpec((SC_WIN, SC_ROW), index_map=lambda i: (i % src_blocks, 0)),
                      pl.BlockSpec((1, SC_WIN), index_map=lambda i: (0, i))],
            out_specs=[],
            core_axis_name=("core", "subcore"),
            dimension_semantics=(pltpu.PARALLEL,),
        )(x_hbm, i_hbm)

    return scatter_kernel(x, idx.reshape(1, n_idx))


def _sc_gather_rows(x, idx):
    n_idx = idx.shape[0]

    @pl.kernel(out_type=jax.ShapeDtypeStruct((n_idx, SC_ROW), x.dtype), mesh=_sc_mesh(), scratch_types=[])
    def gather_kernel(x_hbm, i_hbm, o_hbm):
        def body(i_vmem, o_vmem):
            pltpu.sync_copy(x_hbm.at[i_vmem.at[0]], o_vmem)

        pltpu.emit_pipeline(
            body,
            grid=(n_idx // SC_WIN,),
            in_specs=[pl.BlockSpec((1, SC_WIN), index_map=lambda i: (0, i))],
            out_specs=[pl.BlockSpec((SC_WIN, SC_ROW), index_map=lambda i: (i, 0))],
            core_axis_name=("core", "subcore"),
            dimension_semantics=(pltpu.PARALLEL,),
        )(i_hbm, o_hbm)

    return gather_kernel(x, idx.reshape(1, n_idx))


def _experts_kernel(tmap_ref, xs_ref, wg_ref, wu_ref, wd_ref, ys_ref, wg_scr, wu_scr, wd_scr):
    j = pl.program_id(0)

    @pl.when((j == 0) | (tmap_ref[0, j] != tmap_ref[0, jnp.maximum(j - 1, 0)]))
    def _():
        wg_scr[...] = wg_ref[0].astype(BF16)
        wu_scr[...] = wu_ref[0].astype(BF16)
        wd_scr[...] = wd_ref[0].astype(BF16)

    @pl.when(tmap_ref[1, j] > 0)
    def _():
        x = _unpack_bf16_pairs(xs_ref[0:SLOT_TM, :], xs_ref[SLOT_TM:2 * SLOT_TM, :])
        a = _dot(x, wg_scr[...])
        u = _dot(x, wu_scr[...])
        act = (a * jax.nn.sigmoid(a) * u).astype(BF16)
        y = _dot(act, wd_scr[...]).astype(BF16).astype(F32)
        packed = _pack_bf16_pairs(y)
        ys_ref[0:SLOT_TM, :] = packed[:, :SC_ROW]
        ys_ref[SLOT_TM:2 * SLOT_TM, :] = packed[:, SC_ROW:]


def _experts(xs, tmap, wg, wu, wd):
    n_tiles = xs.shape[0] // (2 * SLOT_TM)
    grid_spec = pltpu.PrefetchScalarGridSpec(
        num_scalar_prefetch=1,
        grid=(n_tiles,),
        in_specs=[
            pl.BlockSpec((2 * SLOT_TM, SC_ROW), lambda j, tm: (tm[2, j], 0)),
            pl.BlockSpec((1, D_MODEL, D_EXPERT), lambda j, tm: (tm[0, j], 0, 0)),
            pl.BlockSpec((1, D_MODEL, D_EXPERT), lambda j, tm: (tm[0, j], 0, 0)),
            pl.BlockSpec((1, D_EXPERT, D_MODEL), lambda j, tm: (tm[0, j], 0, 0)),
        ],
        out_specs=pl.BlockSpec((2 * SLOT_TM, SC_ROW), lambda j, tm: (tm[2, j], 0)),
        scratch_shapes=[pltpu.VMEM((D_MODEL, D_EXPERT), BF16), pltpu.VMEM((D_MODEL, D_EXPERT), BF16),
                        pltpu.VMEM((D_EXPERT, D_MODEL), BF16)],
    )
    return pl.pallas_call(
        _experts_kernel,
        out_shape=jax.ShapeDtypeStruct(xs.shape, jnp.int32),
        grid_spec=grid_spec,
        compiler_params=_cparams(("arbitrary",)),
        name="experts",
    )(tmap, xs, wg, wu, wd)


FIN_TM = 1024


def _final_kernel(x1_ref, mod_ref, yg_ref, w_ref, gf_ref, y_ref):
    moe = None
    for k in range(2):
        base = 2 * k * FIN_TM
        yk = _unpack_bf16_pairs(yg_ref[base:base + FIN_TM, :], yg_ref[base + FIN_TM:base + 2 * FIN_TM, :])
        term = w_ref[:, k:k + 1] * yk.astype(F32)
        moe = term if moe is None else moe + term
    x2 = x1_ref[...] + mod_ref[0, 5:6, :] * moe
    y_ref[...] = x2 * _rms_scale(x2) * gf_ref[...]


def _final(x1, mod3, mod_row_fn, yg, w_cols, g_final):
    n = x1.shape[0]
    return pl.pallas_call(
        _final_kernel,
        out_shape=jax.ShapeDtypeStruct((n, D_MODEL), F32),
        grid=(n // FIN_TM,),
        in_specs=[
            pl.BlockSpec((FIN_TM, D_MODEL), lambda i: (i, 0)),
            pl.BlockSpec((1, 6, D_MODEL), lambda i: (mod_row_fn(i * FIN_TM), 0, 0)),
            pl.BlockSpec((4 * FIN_TM, SC_ROW), lambda i: (i, 0)),
            pl.BlockSpec((FIN_TM, 2), lambda i: (i, 0)),
            _const_spec((1, D_MODEL)),
        ],
        out_specs=pl.BlockSpec((FIN_TM, D_MODEL), lambda i: (i, 0)),
        compiler_params=_cparams(("parallel",)),
        name="final",
    )(x1, mod3, yg, w_cols, g_final)


def _moe(x1, h2p, info, mod3, mod_row_fn, wg, wu, wd, g_final):
    n = x1.shape[0]
    sidx, gidx, tmap = _plan(info)
    n_slots = 2 * n + N_EXPERTS * SLOT_TM
    xs = _sc_scatter_rows(h2p, sidx.reshape(-1), 2 * n_slots)
    ys = _experts(xs, tmap, wg, wu, wd)
    yg = _sc_gather_rows(ys, gidx.reshape(-1))
    w_cols = info[2:4].T
    return _final(x1, mod3, mod_row_fn, yg, w_cols, g_final)


def _stream(x, mod3, mod_row_fn, seq_len, rows_2d, state, w):
    bsz, t, _ = x.shape
    n = bsz * t
    x2d = x.reshape(n, D_MODEL)
    p3, bcum, imb, amax = _in_projection(x2d, mod3, mod_row_fn, w["g_norm1"], w["wmain"], w["bmain"],
                                         w["colscale"], w["w_gates"], w["b_gates"])
    m0 = None if state is None else state[2]
    cols, rows, cs = _gate_prep(bcum, imb, amax, m0)
    ml = _mlstm(p3, cols, rows, cs, w["gn3"], None if state is None else state[:2])
    amat, invc = _pool_constants(seq_len, rows_2d)
    n_seq = max(1, 1024 // seq_len)
    yb = _pool(p3, amat, invc, w["wpg"], w["pscale"], seq_len, n_seq, rows_2d)
    ya = ml if state is not None else ml[0]
    x1, h2p, info = _mix(x2d, mod3, mod_row_fn, ya, yb, p3, w["wa"], w["wb"], w["wo"],
                         w["g_norm2"], w["wr"], w["br"])
    y = _moe(x1, h2p, info, mod3, mod_row_fn, w["wg"], w["wu"], w["wd"], w["g_final"])
    return y.reshape(bsz, t, D_MODEL), ml, cs


def kernel(x_prompt, x_sample, state_C, state_n, state_m, c, c_ctx, w_mod, b_mod, g_norm1, w_in, b_in, gn_gain, w_pool_grp, pool_scale, w_proj_a, w_proj_b, w_out, g_norm2, w_router_grp, b_router_grp, w_router_exp, b_router_exp, w_exp_gate, w_exp_up, w_exp_down, g_final):
    bp, tp, _ = x_prompt.shape
    bs, ts, _ = x_sample.shape
    assert w_mod.shape[0] == 1, "single trunk layer"
    assert tp == CHUNK and ts % CHUNK == 0 and ts == ML_ROWS

    cvec = jnp.zeros((8, D_MODEL), F32).at[0].set(c_ctx).at[1:1 + bs].set(c)
    mod3 = _modulation(cvec, w_mod[0], b_mod[0])

    wi, bi = w_in[0], b_in[0]
    d = D_MODEL
    o_q, o_k, o_v, o_o, o_g, o_p, o_ga, o_gb = 0, d, 2 * d, 3 * d, 4 * d, 4 * d + 32, 4 * d + 32 + 512, 4 * d + 32 + 512 + d
    order = [(o_q, d), (o_k, d), (o_v, d), (o_o, d), (o_ga, d), (o_gb, d), (o_p, 512)]
    wmain = _pack_columns(wi, order)
    bmain = jnp.concatenate([bi[s:s + l] for s, l in order]).reshape(1, D_MAIN)
    colscale = jnp.ones((D_MAIN,), F32).at[SLAB_K * LANES:(SLAB_K + N_HEADS) * LANES].set(HEAD_DIM ** -0.5)
    colscale = colscale.reshape(1, D_MAIN)
    gwt = _take_columns_t(wi, o_g, 32)
    gb = bi[o_g:o_g + 32]
    i_f, f_f, i_b, f_b = (slice(0, 8), slice(8, 16), slice(16, 24), slice(24, 32))
    pad_w = jnp.zeros((LANES - 2 * N_HD, d), F32)
    pad_b = jnp.zeros((LANES - 2 * N_HD,), F32)
    wg = jnp.concatenate([gwt[f_f], gwt[f_b], gwt[i_f], gwt[i_b], pad_w], axis=0).T
    bg = jnp.concatenate([gb[f_f], gb[f_b], gb[i_f], gb[i_b], pad_b]).reshape(1, LANES)
    wg = jnp.concatenate(_split_bf16(wg), axis=1)

    wr = jnp.concatenate([w_router_grp[0], w_router_exp[0].reshape(d, N_EXPERTS),
                          jnp.zeros((d, LANES - ROUTER_ROWS), F32)], axis=1)
    br = jnp.concatenate([b_router_grp[0], b_router_exp[0].reshape(N_EXPERTS),
                          jnp.zeros((LANES - ROUTER_ROWS,), F32)]).reshape(1, LANES)
    wr = jnp.concatenate(_split_bf16(wr), axis=1)

    w = dict(
        g_norm1=g_norm1[0].reshape(1, d), wmain=wmain, bmain=bmain, colscale=colscale,
        w_gates=wg, b_gates=bg,
        gn3=gn_gain[0].reshape(N_HEADS, 1, HEAD_DIM),
        wpg=w_pool_grp[0].astype(BF16), pscale=pool_scale[0].reshape(N_GROUPS, 1, LANES),
        wa=w_proj_a[0].astype(BF16), wb=w_proj_b[0].astype(BF16), wo=w_out[0].astype(BF16),
        g_norm2=g_norm2[0].reshape(1, d), wr=wr, br=br,
        wg=w_exp_gate[0], wu=w_exp_up[0], wd=w_exp_down[0],
        g_final=g_final.reshape(1, d),
    )

    y_prompt, (_, cst, nst), cs_prompt = _stream(x_prompt, mod3, lambda r: 0, tp, None, None, w)
    n_rep = jnp.broadcast_to(state_n[..., None], state_C.shape)
    m0 = jnp.pad(state_m[:, 0].reshape(bs, N_HD), ((0, 0), (0, LANES - N_HD))).reshape(bs, 1, LANES)
    y_sample, _, _ = _stream(x_sample, mod3, lambda r: 1 + r // ts, ts, ts // GRID_W, (state_C, n_rep, m0), w)

    new_c = cst
    new_n = nst.reshape(bp, 1, 2, N_HEADS, HEAD_DIM)
    new_m = cs_prompt[:, 1, :N_HD].reshape(bp, 1, 2, N_HEADS)
    return (y_prompt, y_sample, new_c, new_n, new_m)
```

```python
import functools

import numpy as np
import jax
import jax.numpy as jnp
from jax import lax
from jax.experimental import pallas as pl
from jax.experimental.pallas import tpu as pltpu
from jax.experimental.pallas import tpu_sc as plsc

F32 = jnp.float32
BF16 = jnp.bfloat16

D_MODEL = 1024
N_HEADS = 8
HEAD_DIM = 128
GRID_W = 64
POOL_WINDOWS = (2, 4, 8, 16)
N_GROUPS = 4
EXPERTS_PER_GROUP = 4
N_EXPERTS = 16
D_EXPERT = 512
EPS = 1e-6
NEG = -1e30

LANES = 128
CHUNK = 256
N_SLABS = 52
SLAB_Q, SLAB_K, SLAB_V, SLAB_O, SLAB_GA, SLAB_GB, SLAB_POOL = 0, 8, 16, 24, 32, 40, 48
D_MAIN = N_SLABS * LANES
VMEM_LIMIT = 56 * 1024 * 1024


def _cparams(sem):
    return pltpu.CompilerParams(dimension_semantics=sem, vmem_limit_bytes=VMEM_LIMIT)


def _const_spec(shape):
    nd = len(shape)
    return pl.BlockSpec(shape, lambda *_: (0,) * nd, pipeline_mode=pl.Buffered(1))


def _split_bf16(x):
    hi = x.astype(BF16)
    lo = (x - hi.astype(F32)).astype(BF16)
    return hi, lo


def _dot(a, b):
    return jnp.dot(a, b, preferred_element_type=F32)


def _dot3(a, b_hi, b_lo):
    a_hi, a_lo = _split_bf16(a)
    return _dot(a_hi, b_hi) + _dot(a_hi, b_lo) + _dot(a_lo, b_hi)


def _dot3_packed(a_hi, a_lo, b_hilo):
    first = _dot(a_hi, b_hilo)
    return first[:, :LANES] + first[:, LANES:] + _dot(a_lo, b_hilo[:, :LANES])


def _rms_scale(x):
    return lax.rsqrt(jnp.mean(x * x, axis=-1, keepdims=True) + EPS)


def _log_sigmoid(x):
    return jnp.minimum(x, 0.0) - jnp.log1p(jnp.exp(-jnp.abs(x)))


def _mod_kernel(c_ref, w_ref, b_ref, o_ref):
    c = c_ref[...]
    s = c * jax.nn.sigmoid(c)
    w_hi, w_lo = _split_bf16(w_ref[...])
    o_ref[...] = _dot3(s, w_hi, w_lo) + b_ref[...]


def _modulation(cvec, w_mod, b_mod):
    tn = 2048
    out = pl.pallas_call(
        _mod_kernel,
        out_shape=jax.ShapeDtypeStruct((8, 6 * D_MODEL), F32),
        grid=(6 * D_MODEL // tn,),
        in_specs=[
            pl.BlockSpec((8, D_MODEL), lambda j: (0, 0)),
            pl.BlockSpec((D_MODEL, tn), lambda j: (0, j)),
            pl.BlockSpec((1, tn), lambda j: (0, j)),
        ],
        out_specs=pl.BlockSpec((8, tn), lambda j: (0, j)),
        compiler_params=_cparams(("parallel",)),
        name="modulation",
    )(cvec, w_mod, b_mod.reshape(1, -1))
    return out.reshape(8, 6, D_MODEL)


PACK_COLS = 512


def _pack_kernel(wt_ref, o_ref):
    o_ref[...] = wt_ref[...].T.astype(BF16)


def _pack_columns(w, order):
    k, n = w.shape
    starts = []
    for src, width in order:
        assert width % PACK_COLS == 0
        starts += [src + j for j in range(0, width, PACK_COLS)]
    n_out = len(starts) * PACK_COLS

    unit = 32
    assert all(s % unit == 0 for s in starts)

    def src_row(j):
        row = jnp.int32(starts[-1] // unit)
        for idx in range(len(starts) - 2, -1, -1):
            row = jnp.where(j == idx, starts[idx] // unit, row)
        return row * unit

    return pl.pallas_call(
        _pack_kernel,
        out_shape=jax.ShapeDtypeStruct((k, n_out), BF16),
        grid=(len(starts),),
        in_specs=[pl.BlockSpec((pl.Element(PACK_COLS), pl.Element(k)), lambda j: (src_row(j), 0))],
        out_specs=pl.BlockSpec((k, PACK_COLS), lambda j: (0, j)),
        compiler_params=_cparams(("parallel",)),
        name="pack_columns",
    )(w.T)


def _copy_kernel(w_ref, o_ref):
    o_ref[...] = w_ref[...]


def _take_columns_t(w, start, width):
    k = w.shape[0]
    return pl.pallas_call(
        _copy_kernel,
        out_shape=jax.ShapeDtypeStruct((width, k), w.dtype),
        grid=(1,),
        in_specs=[pl.BlockSpec((pl.Element(width), pl.Element(k)), lambda j: (start, 0))],
        out_specs=pl.BlockSpec((width, k), lambda j: (0, 0)),
        name="take_columns",
    )(w.T)


IN_TM = 512
IN_TN = 512
LOG2E = 1.4426950408889634
N_HD = 2 * N_HEADS


def _fwd_lanes(shape):
    lane = lax.broadcasted_iota(jnp.int32, shape, len(shape) - 1)
    return (lane & N_HEADS) == 0


def _inproj_kernel(x_ref, mod_ref, g_ref, w_ref, b_ref, cs_ref, wg_ref, bg_ref,
                   p3_ref, bcum_ref, imb_ref, amax_ref):
    x = x_ref[...]
    h = x * _rms_scale(x) * g_ref[...]
    h = h * (1.0 + mod_ref[0, 1:2, :]) + mod_ref[0, 0:1, :]
    hb = h.astype(BF16)
    for n in range(D_MAIN // IN_TN):
        sl = slice(n * IN_TN, (n + 1) * IN_TN)
        acc = (_dot(hb, w_ref[:, sl]) + b_ref[:, sl]) * cs_ref[:, sl]
        accb = acc.astype(BF16)
        for j in range(IN_TN // LANES):
            p3_ref[n * (IN_TN // LANES) + j] = accb[:, j * LANES:(j + 1) * LANES]

    h_lo = (h - hb.astype(F32)).astype(BF16)
    gates = _dot3_packed(hb, h_lo, wg_ref[...]) + bg_ref[...]
    ls = _log_sigmoid(gates)
    gi = pltpu.roll(gates, LANES - N_HD, 1)
    row = lax.broadcasted_iota(jnp.int32, (CHUNK, LANES), 0)
    fwd = _fwd_lanes((CHUNK, LANES))
    for c in range(IN_TM // CHUNK):
        rows = slice(c * CHUNK, (c + 1) * CHUNK)
        cf = ls[rows]
        cb = cf
        s = 1
        while s < CHUNK:
            cf = cf + jnp.where(row >= s, pltpu.roll(cf, s, 0), 0.0)
            cb = cb + jnp.where(row < CHUNK - s, pltpu.roll(cb, CHUNK - s, 0), 0.0)
            s *= 2
        b = jnp.where(fwd, cf, cb)
        imb = gi[rows] - b
        mf = imb
        mb = imb
        s = 1
        while s < CHUNK:
            mf = jnp.maximum(mf, jnp.where(row >= s, pltpu.roll(mf, s, 0), -jnp.inf))
            mb = jnp.maximum(mb, jnp.where(row < CHUNK - s, pltpu.roll(mb, CHUNK - s, 0), -jnp.inf))
            s *= 2
        bcum_ref[rows, :] = b
        imb_ref[rows, :] = imb
        amax_ref[rows, :] = b + jnp.where(fwd, mf, mb)


def _in_projection(x2d, mod3, mod_row_fn, g_norm1, wmain, bmain, colscale, wg, bg):
    n = x2d.shape[0]
    return pl.pallas_call(
        _inproj_kernel,
        out_shape=(jax.ShapeDtypeStruct((N_SLABS, n, LANES), BF16),) + (jax.ShapeDtypeStruct((n, LANES), F32),) * 3,
        grid=(n // IN_TM,),
        in_specs=[
            pl.BlockSpec((IN_TM, D_MODEL), lambda i: (i, 0)),
            pl.BlockSpec((1, 6, D_MODEL), lambda i: (mod_row_fn(i * IN_TM), 0, 0)),
            _const_spec((1, D_MODEL)),
            _const_spec((D_MODEL, D_MAIN)),
            _const_spec((1, D_MAIN)),
            _const_spec((1, D_MAIN)),
            _const_spec((D_MODEL, 2 * LANES)),
            _const_spec((1, LANES)),
        ],
        out_specs=(pl.BlockSpec((N_SLABS, IN_TM, LANES), lambda i: (0, i, 0)),)
        + (pl.BlockSpec((IN_TM, LANES), lambda i: (i, 0)),) * 3,
        compiler_params=_cparams(("parallel",)),
        name="in_projection",
    )(x2d, mod3, g_norm1, wmain, bmain, colscale, wg, bg)


GP_ROWS = 1024
GP_NCH = GP_ROWS // CHUNK
Q_C2, Q_EM, Q_WI, Q_WC = 0, 1, 2, 3


def _gate_prep_kernel(*refs, carry):
    if carry:
        b_ref, imb_ref, a_ref, m0_ref, cols_ref, rw_ref, cs_ref = refs
    else:
        b_ref, imb_ref, a_ref, cols_ref, rw_ref, cs_ref = refs
    fwd1 = _fwd_lanes((1, LANES))
    lane = lax.broadcasted_iota(jnp.int32, (CHUNK, LANES), 1)
    neg = jnp.full((1, LANES), NEG, F32)

    def at_scan_end(ref, c):
        return jnp.where(fwd1, ref[(c + 1) * CHUNK - 1:(c + 1) * CHUNK, :], ref[c * CHUNK:c * CHUNK + 1, :])

    b_last = [at_scan_end(b_ref, c) for c in range(GP_NCH)]
    a_last = [at_scan_end(a_ref, c) for c in range(GP_NCH)]
    m_start = m0_ref[0] if carry else neg
    m_prev = [[None] * GP_NCH, [None] * GP_NCH]
    m_new = [[None] * GP_NCH, [None] * GP_NCH]
    for d, order in enumerate((range(GP_NCH), range(GP_NCH - 1, -1, -1))):
        m = m_start
        for c in order:
            m_prev[d][c] = m
            m_new[d][c] = jnp.maximum(b_last[c] + m, a_last[c])
            m = m_new[d][c] if carry else neg

    row_src = []
    for c in range(GP_NCH):
        rows = slice(c * CHUNK, (c + 1) * CHUNK)
        mp = jnp.where(fwd1, m_prev[0][c], m_prev[1][c])
        mn = jnp.where(fwd1, m_new[0][c], m_new[1][c])
        b = b_ref[rows, :]
        imb = imb_ref[rows, :]
        inter = b + mp
        m_t = jnp.maximum(inter, a_ref[rows, :])
        c2 = (b - m_t) * LOG2E
        em = jnp.exp(-m_t)
        wc = jnp.exp(b_last[c] - mn + imb)
        packed = jnp.where(lane < N_HD, c2, pltpu.roll(em, N_HD * Q_EM, 1))
        if carry:
            wi = jnp.exp(inter - m_t)
            packed = jnp.where(lane < N_HD * Q_WI, packed, pltpu.roll(wi, N_HD * Q_WI, 1))
        cols_ref[rows, :] = jnp.where(lane < N_HD * Q_WC, packed, pltpu.roll(wc, N_HD * Q_WC, 1))
        row_src.append(jnp.where(lane < N_HD, imb * (-LOG2E), pltpu.roll(wc, N_HD, 1)))
        cs_ref[c, 0:1, :] = jnp.exp(b_last[c] + mp - mn)
        cs_ref[c, 1:2, :] = mn

    rows_t = jnp.concatenate(row_src, axis=0).T
    for r in range(2 * N_HD):
        rw_ref[r] = rows_t[r:r + 1]


def _gate_prep(bcum, imb, amax, m0=None):
    n = bcum.shape[0]
    carry = m0 is not None
    blk = pl.BlockSpec((GP_ROWS, LANES), lambda i: (i, 0))
    in_specs = [blk, blk, blk]
    args = [bcum, imb, amax]
    if carry:
        in_specs.append(pl.BlockSpec((1, 1, LANES), lambda i: (i, 0, 0)))
        args.append(m0)
    return pl.pallas_call(
        functools.partial(_gate_prep_kernel, carry=carry),
        out_shape=(jax.ShapeDtypeStruct((n, LANES), F32), jax.ShapeDtypeStruct((2 * N_HD, 1, n), F32),
                   jax.ShapeDtypeStruct((n // CHUNK, 2, LANES), F32)),
        grid=(n // GP_ROWS,),
        in_specs=in_specs,
        out_specs=(blk, pl.BlockSpec((2 * N_HD, 1, GP_ROWS), lambda i: (0, 0, i)),
                   pl.BlockSpec((GP_NCH, 2, LANES), lambda i: (i, 0, 0))),
        compiler_params=_cparams(("parallel",)),
        name="gate_prep",
    )(*args)


ML_ROWS = 1024
ML_ROWS_FREE = 4096


def _pick_col(blk, lane_iota, idx):
    return jnp.sum(jnp.where(lane_iota == idx, blk, 0.0), axis=1, keepdims=True)


def _mlstm_kernel(*refs, carry, n_chunks):
    if carry:
        (q_ref, k_ref, v_ref, o_ref, cols_ref, r2f_ref, r2b_ref, gn_ref, cs_ref, c0_ref, n0_ref,
         ya_ref, s_scr, h_scr) = refs
    else:
        (q_ref, k_ref, v_ref, o_ref, cols_ref, r2f_ref, r2b_ref, gn_ref, wrf_ref, wrb_ref,
         ya_ref, cst_ref, nst_ref, s_scr, h_scr) = refs
    head = pl.program_id(1)
    lane = lax.broadcasted_iota(jnp.int32, (CHUNK, LANES), 1)
    lane1 = lax.broadcasted_iota(jnp.int32, (1, LANES), 1)
    r_idx = lax.broadcasted_iota(jnp.int32, (CHUNK, CHUNK), 0)
    c_idx = lax.broadcasted_iota(jnp.int32, (CHUNK, CHUNK), 1)
    ones = jnp.ones((CHUNK, HEAD_DIM), BF16)
    tn_dims = (((0,), (0,)), ((), ()))

    for d in range(2):
        mask = (c_idx <= r_idx) if d == 0 else (c_idx >= r_idx)
        r2_ref = r2f_ref if d == 0 else r2b_ref
        base = N_HEADS * d + head
        if carry:
            cn = jnp.concatenate([c0_ref[0, 0, d, 0], n0_ref[0, 0, d, 0]], axis=1)
        order = range(n_chunks) if d == 0 else range(n_chunks - 1, -1, -1)
        for step, c in enumerate(order):
            rows = slice(c * CHUNK, (c + 1) * CHUNK)
            q = q_ref[0, rows, :]
            k = k_ref[0, rows, :]
            v = v_ref[0, rows, :]
            v1 = jnp.concatenate([v, ones], axis=1)
            if d == 0:
                s = lax.dot_general(q, k, (((1,), (1,)), ((), ())), preferred_element_type=F32)
                s_scr[rows, :] = s
            else:
                s = s_scr[rows, :]
            blk = cols_ref[rows, :]
            arg = _pick_col(blk, lane, base + N_HD * Q_C2) - r2_ref[0, :, rows]
            sw = s * jnp.exp2(jnp.where(mask, arg, NEG))
            nd = _dot(sw.astype(BF16), v1)
            if carry:
                nd = nd + _pick_col(blk, lane, base + N_HD * Q_WI) * _dot(q, cn.astype(BF16))
            hdir = nd[:, :HEAD_DIM] / jnp.maximum(jnp.abs(nd[:, HEAD_DIM:]), _pick_col(blk, lane, base + N_HD * Q_EM))
            if d == 0:
                h_scr[rows, :] = hdir
            else:
                h_scr[rows, :] = h_scr[rows, :] + hdir

            if carry and step == n_chunks - 1:
                continue
            w_col = _pick_col(blk, lane, base + N_HD * Q_WC)
            if carry:
                u = lax.dot_general(k, (w_col * v1.astype(F32)).astype(BF16), tn_dims,
                                    preferred_element_type=F32)
                decay = jnp.sum(jnp.where(lane1 == base, cs_ref[c, 0:1, :], 0.0), axis=1, keepdims=True)
                cn = decay * cn + u
            else:
                w_row = (wrf_ref if d == 0 else wrb_ref)[0, :, rows]
                w8 = jnp.broadcast_to(w_row, (8, CHUNK)).astype(BF16)
                cst_ref[c, 0, d, 0] = lax.dot_general(k, (w_col * v.astype(F32)).astype(BF16), tn_dims,
                                                      preferred_element_type=F32)
                nst_ref[c, d, 0] = _dot(w8, k)[0:1, :]

    hh = h_scr[...]
    hn = hh * _rms_scale(hh) * gn_ref[0]
    ya_ref[0] = (hn * jax.nn.sigmoid(o_ref[0].astype(F32))).astype(BF16)


def _mlstm(p3, cols, rows, cs, gn3, state=None):
    n = p3.shape[1]
    carry = state is not None
    rows_step = ML_ROWS if carry else ML_ROWS_FREE
    n_chunks = rows_step // CHUNK
    nblk = n // rows_step

    def slab(base):
        return pl.BlockSpec((1, rows_step, LANES), lambda g, h: (base + h, g, 0))

    def row(base):
        return pl.BlockSpec((1, 1, rows_step), lambda g, h: (base + h, 0, g))

    in_specs = [
        slab(SLAB_Q), slab(SLAB_K), slab(SLAB_V), slab(SLAB_O),
        pl.BlockSpec((rows_step, LANES), lambda g, h: (g, 0)),
        row(0), row(N_HEADS),
        pl.BlockSpec((1, 1, LANES), lambda g, h: (h, 0, 0)),
    ]
    args = [p3, p3, p3, p3, cols, rows, rows, gn3]
    ya_shape = jax.ShapeDtypeStruct((N_HEADS, n, LANES), BF16)
    ya_spec = pl.BlockSpec((1, rows_step, LANES), lambda g, h: (h, g, 0))
    state_spec = pl.BlockSpec((1, 1, 2, 1, HEAD_DIM, HEAD_DIM), lambda g, h: (g, 0, 0, h, 0, 0))
    if carry:
        c0, n0 = state
        in_specs += [pl.BlockSpec((n_chunks, 2, LANES), lambda g, h: (g, 0, 0)), state_spec, state_spec]
        args += [cs, c0, n0]
        out_shape = ya_shape
        out_specs = ya_spec
    else:
        in_specs += [row(N_HD), row(N_HD + N_HEADS)]
        args += [rows, rows]
        nseq = n // CHUNK
        out_shape = (ya_shape,
                     jax.ShapeDtypeStruct((nseq, 1, 2, N_HEADS, HEAD_DIM, HEAD_DIM), F32),
                     jax.ShapeDtypeStruct((nseq, 2, N_HEADS, 1, HEAD_DIM), F32))
        out_specs = (ya_spec,
                     pl.BlockSpec((n_chunks, 1, 2, 1, HEAD_DIM, HEAD_DIM), lambda g, h: (g, 0, 0, h, 0, 0)),
                     pl.BlockSpec((n_chunks, 2, 1, 1, HEAD_DIM), lambda g, h: (g, 0, h, 0, 0)))
    return pl.pallas_call(
        functools.partial(_mlstm_kernel, carry=carry, n_chunks=n_chunks),
        out_shape=out_shape,
        grid=(nblk, N_HEADS),
        in_specs=in_specs,
        out_specs=out_specs,
        scratch_shapes=[pltpu.VMEM((rows_step, CHUNK), F32), pltpu.VMEM((rows_step, HEAD_DIM), F32)],
        compiler_params=_cparams(("parallel", "parallel")),
        name="mlstm_latent" if carry else "mlstm_prompt",
    )(*args)


def _pool_kernel(u_ref, a_ref, ic_ref, wpg_ref, ps_ref, yb_ref, *, seq_len, n_seq, rows_2d):
    piece = seq_len if rows_2d is None else GRID_W
    n_piece = n_seq if rows_2d is None else rows_2d
    for g in range(N_GROUPS):
        u = jnp.concatenate([u_ref[g, i * piece:(i + 1) * piece, :] for i in range(n_piece)], axis=1)
        win = _dot(a_ref[g], u)
        if rows_2d is None:
            inv_cnt = jnp.concatenate([ic_ref[g]] * n_piece, axis=1)
        else:
            run = [jnp.zeros((piece, LANES), F32)]
            for r in range(rows_2d):
                run.append(run[-1] + win[:, r * LANES:(r + 1) * LANES])
            half = POOL_WINDOWS[g] // 2
            win = jnp.concatenate([run[min(r - half + POOL_WINDOWS[g], rows_2d)] - run[max(r - half, 0)]
                                   for r in range(rows_2d)], axis=1)
            inv_cnt = jnp.concatenate([ic_ref[g, r * piece:(r + 1) * piece, :] for r in range(rows_2d)], axis=1)
        p = (win * inv_cnt - u.astype(F32)).astype(BF16)
        p = jnp.concatenate([p[:, i * LANES:(i + 1) * LANES] for i in range(n_piece)], axis=0)
        yb_ref[g] = (_dot(p, wpg_ref[g]) * ps_ref[g]).astype(BF16)


def _pool(p3, amat, invc, wpg, pscale, seq_len, n_seq, rows_2d):
    n = p3.shape[1]
    rows = seq_len * n_seq
    return pl.pallas_call(
        functools.partial(_pool_kernel, seq_len=seq_len, n_seq=n_seq, rows_2d=rows_2d),
        out_shape=jax.ShapeDtypeStruct((N_GROUPS, n, LANES), BF16),
        grid=(n // rows,),
        in_specs=[
            pl.BlockSpec((N_GROUPS, rows, LANES), lambda i: (SLAB_POOL // N_GROUPS, i, 0)),
            _const_spec(amat.shape),
            _const_spec((N_GROUPS, seq_len, LANES)),
            _const_spec((N_GROUPS, LANES, LANES)),
            _const_spec((N_GROUPS, 1, LANES)),
        ],
        out_specs=pl.BlockSpec((N_GROUPS, rows, LANES), lambda i: (0, i, 0)),
        compiler_params=_cparams(("parallel",)),
        name="pool",
    )(p3, amat, invc, wpg, pscale)


def _window_matrix(size, w):
    idx = np.arange(size)
    lo = np.clip(idx - w // 2, 0, size)
    hi = np.clip(idx - w // 2 + w, 0, size)
    s = np.arange(size)[None, :]
    a = ((s >= lo[:, None]) & (s < hi[:, None])).astype(np.float32)
    return a, (hi - lo).astype(np.float32)


def _pool_constants(seq_len, rows_2d):
    mats, invs = [], []
    for w in POOL_WINDOWS:
        if rows_2d is None:
            a, cnt = _window_matrix(seq_len, w)
        else:
            _, cr = _window_matrix(rows_2d, w)
            a, cc = _window_matrix(GRID_W, w)
            cnt = np.kron(cr, cc)
        mats.append(a)
        invs.append(np.repeat((1.0 / cnt)[:, None], LANES, axis=1))
    return (jnp.asarray(np.stack(mats), dtype=BF16), jnp.asarray(np.stack(invs), dtype=F32))


MIX_TM = 512


def _mix_kernel(x_ref, mod_ref, ya_ref, yb_ref, ga_ref, gb_ref, wa_ref, wb_ref, wo_ref,
                g2_ref, wr_ref, br_ref, x1_ref, h2_ref, info_ref):
    ya = jnp.concatenate([ya_ref[i] for i in range(N_HEADS)], axis=1)
    yb = jnp.concatenate([yb_ref[i] for i in range(N_GROUPS)], axis=1)
    ga = jnp.concatenate([ga_ref[i] for i in range(8)], axis=1).astype(F32)
    gb = jnp.concatenate([gb_ref[i] for i in range(8)], axis=1).astype(F32)
    merged = jax.nn.sigmoid(ga) * _dot(ya, wa_ref[...]) + jax.nn.sigmoid(gb) * _dot(yb, wb_ref[...])
    out = _dot(merged.astype(BF16), wo_ref[...])
    x1 = x_ref[...] + mod_ref[0, 2:3, :] * out
    x1_ref[...] = x1

    h2 = x1 * _rms_scale(x1) * g2_ref[...]
    h2 = h2 * (1.0 + mod_ref[0, 4:5, :]) + mod_ref[0, 3:4, :]
    h_hi, h_lo = _split_bf16(h2)
    packed = _pack_bf16_pairs(h_hi.astype(F32))
    h2_ref[0:MIX_TM, :] = packed[:, :SC_ROW]
    h2_ref[MIX_TM:2 * MIX_TM, :] = packed[:, SC_ROW:]
    logits = _dot3_packed(h_hi, h_lo, wr_ref[...]) + br_ref[...]
    first, second, w1, w2 = _route_rows(logits.T)
    zero = jnp.zeros_like(w1)
    info_ref[...] = jnp.concatenate([first, second, w1, w2, zero, zero, zero, zero], axis=0)


def _mix(x2d, mod3, mod_row_fn, ya, yb, p3, wa, wb, wo, g_norm2, wr, br):
    n = x2d.shape[0]
    return pl.pallas_call(
        _mix_kernel,
        out_shape=(jax.ShapeDtypeStruct((n, D_MODEL), F32),
                   jax.ShapeDtypeStruct((2 * n, SC_ROW), jnp.int32), jax.ShapeDtypeStruct((8, n), F32)),
        grid=(n // MIX_TM,),
        in_specs=[
            pl.BlockSpec((MIX_TM, D_MODEL), lambda i: (i, 0)),
            pl.BlockSpec((1, 6, D_MODEL), lambda i: (mod_row_fn(i * MIX_TM), 0, 0)),
            pl.BlockSpec((N_HEADS, MIX_TM, LANES), lambda i: (0, i, 0)),
            pl.BlockSpec((N_GROUPS, MIX_TM, LANES), lambda i: (0, i, 0)),
            pl.BlockSpec((8, MIX_TM, LANES), lambda i: (SLAB_GA // 8, i, 0)),
            pl.BlockSpec((8, MIX_TM, LANES), lambda i: (SLAB_GB // 8, i, 0)),
            _const_spec((D_MODEL, D_MODEL)),
            _const_spec((N_GROUPS * LANES, D_MODEL)),
            _const_spec((D_MODEL, D_MODEL)),
            _const_spec((1, D_MODEL)),
            _const_spec((D_MODEL, 2 * LANES)),
            _const_spec((1, LANES)),
        ],
        out_specs=(pl.BlockSpec((MIX_TM, D_MODEL), lambda i: (i, 0)),
                   pl.BlockSpec((2 * MIX_TM, SC_ROW), lambda i: (i, 0)),
                   pl.BlockSpec((8, MIX_TM), lambda i: (0, i))),
        compiler_params=_cparams(("parallel",)),
        name="mix",
    )(x2d, mod3, ya, yb, p3, p3, wa, wb, wo, g_norm2, wr, br)


PLAN_TB = 1024
ROUTER_ROWS = 4 + N_EXPERTS
SLOT_TM = 1024
SC_ROW = 256
SC_WIN = 128
HI_MASK = -65536


def _pack_bf16_pairs(x):
    bits = pltpu.bitcast(x, jnp.int32)
    half = D_MODEL // 2
    return jnp.bitwise_or(jnp.bitwise_and(lax.shift_right_logical(bits[:, :half], 16), 0xFFFF),
                          jnp.bitwise_and(bits[:, half:], HI_MASK))


def _unpack_bf16_pairs(lo_words, hi_words):
    def low(w):
        return pltpu.bitcast(lax.shift_left(w, 16), F32)

    def high(w):
        return pltpu.bitcast(jnp.bitwise_and(w, HI_MASK), F32)

    return jnp.concatenate([low(lo_words), low(hi_words), high(lo_words), high(hi_words)], axis=1).astype(BF16)


def _route_rows(lt):
    lg = [lt[i:i + 1] for i in range(N_GROUPS)]
    mx = jnp.maximum(jnp.maximum(lg[0], lg[1]), jnp.maximum(lg[2], lg[3]))
    p_top = 1.0 / (jnp.exp(lg[0] - mx) + jnp.exp(lg[1] - mx) + jnp.exp(lg[2] - mx) + jnp.exp(lg[3] - mx))
    gidx = jnp.zeros(lg[0].shape, jnp.int32)
    best = lg[0]
    for i in range(1, N_GROUPS):
        upd = lg[i] > best
        gidx = jnp.where(upd, i, gidx)
        best = jnp.where(upd, lg[i], best)

    def expert_row(g, e):
        r = N_GROUPS + g * EXPERTS_PER_GROUP + e
        return lt[r:r + 1]

    le = [jnp.where(gidx == 0, expert_row(0, e),
                    jnp.where(gidx == 1, expert_row(1, e),
                              jnp.where(gidx == 2, expert_row(2, e), expert_row(3, e))))
          for e in range(EXPERTS_PER_GROUP)]
    v1 = jnp.maximum(jnp.maximum(le[0], le[1]), jnp.maximum(le[2], le[3]))
    i1 = jnp.full(v1.shape, EXPERTS_PER_GROUP - 1, jnp.int32)
    for e in range(EXPERTS_PER_GROUP - 2, -1, -1):
        i1 = jnp.where(le[e] == v1, e, i1)
    le2 = [jnp.where(i1 == e, -jnp.inf, le[e]) for e in range(EXPERTS_PER_GROUP)]
    v2 = jnp.maximum(jnp.maximum(le2[0], le2[1]), jnp.maximum(le2[2], le2[3]))
    i2 = jnp.full(v2.shape, EXPERTS_PER_GROUP - 1, jnp.int32)
    for e in range(EXPERTS_PER_GROUP - 2, -1, -1):
        i2 = jnp.where(le2[e] == v2, e, i2)
    e2 = jnp.exp(v2 - v1)
    first = (gidx * EXPERTS_PER_GROUP + i1).astype(F32)
    second = (gidx * EXPERTS_PER_GROUP + i2).astype(F32)
    return first, second, p_top / (1.0 + e2), p_top * e2 / (1.0 + e2)


def _plan_kernel(info_ref, before_ref, sidx_ref, gidx_ref, tmap_ref, rank_scr, count_scr, *, n_blocks):
    step = pl.program_id(0)
    expert_id = lax.broadcasted_iota(jnp.int32, (N_EXPERTS, PLAN_TB), 0).astype(F32)

    @pl.when(step == 0)
    def _():
        count_scr[...] = jnp.zeros_like(count_scr)

    @pl.when(step < n_blocks)
    def _():
        cols = pl.ds(pl.multiple_of(step * PLAN_TB, PLAN_TB), PLAN_TB)
        sel = jnp.where((info_ref[0:1, cols] == expert_id) | (info_ref[1:2, cols] == expert_id), 1.0, 0.0)
        selb = sel.astype(BF16)
        rank_scr[:, cols] = _dot(selb, before_ref[...]) + count_scr[:, 0:1]
        count_scr[...] += _dot(selb, jnp.ones((PLAN_TB, LANES), BF16))

    @pl.when(step == n_blocks)
    def _():
        counts = count_scr[...]
        padded = jnp.floor((counts + (SLOT_TM - 1)) * (1.0 / SLOT_TM)) * SLOT_TM
        starts, ends, run = [], [], jnp.zeros((1, LANES), F32)
        for e in range(N_EXPERTS):
            starts.append(run)
            run = run + padded[e:e + 1]
            ends.append(run)
        n = info_ref.shape[1]
        shift = SLOT_TM.bit_length() - 1
        for k in range(2):
            chosen = info_ref[k:k + 1, :]
            slot = jnp.zeros((1, n), F32)
            for e in range(N_EXPERTS):
                slot = jnp.where(chosen == float(e), starts[e][:, 0:1] + rank_scr[e:e + 1, :], slot)
            slot = slot.astype(jnp.int32)
            low = jnp.left_shift(jnp.right_shift(slot, shift), shift + 1) + jnp.bitwise_and(slot, SLOT_TM - 1)
            for half in range(2):
                rows = low + half * SLOT_TM
                for t in range(n // MIX_TM):
                    sidx_ref[k, t, half:half + 1, :] = rows[:, t * MIX_TM:(t + 1) * MIX_TM]
                for t in range(n // FIN_TM):
                    gidx_ref[t, k, half:half + 1, :] = rows[:, t * FIN_TM:(t + 1) * FIN_TM]
        tile_start = lax.broadcasted_iota(jnp.int32, (1, LANES), 1).astype(F32) * SLOT_TM
        used = jnp.where(tile_start < run, 1.0, 0.0)
        tile_start = jnp.minimum(tile_start, run - SLOT_TM)
        owner = jnp.zeros((1, LANES), F32)
        for e in range(N_EXPERTS - 1):
            owner = owner + jnp.where(ends[e] <= tile_start, 1.0, 0.0)
        tile = tile_start * (1.0 / SLOT_TM)
        tmap_ref[...] = jnp.concatenate([owner, used, tile] + [jnp.zeros((1, LANES), F32)] * 5,
                                        axis=0).astype(jnp.int32)


def _plan(info):
    n = info.shape[1]
    n_blocks = n // PLAN_TB
    t_i = lax.broadcasted_iota(jnp.int32, (PLAN_TB, PLAN_TB), 0)
    t_j = lax.broadcasted_iota(jnp.int32, (PLAN_TB, PLAN_TB), 1)
    before = (t_i < t_j).astype(BF16)
    return pl.pallas_call(
        functools.partial(_plan_kernel, n_blocks=n_blocks),
        out_shape=(jax.ShapeDtypeStruct((2, n // MIX_TM, 2, MIX_TM), jnp.int32),
                   jax.ShapeDtypeStruct((n // FIN_TM, 2, 2, FIN_TM), jnp.int32),
                   jax.ShapeDtypeStruct((8, LANES), jnp.int32)),
        grid=(n_blocks + 1,),
        in_specs=[_const_spec((8, n)), _const_spec((PLAN_TB, PLAN_TB))],
        out_specs=(pl.BlockSpec((2, n // MIX_TM, 2, MIX_TM), lambda i: (0, 0, 0, 0)),
                   pl.BlockSpec((n // FIN_TM, 2, 2, FIN_TM), lambda i: (0, 0, 0, 0)),
                   pl.BlockSpec((8, LANES), lambda i: (0, 0))),
        scratch_shapes=[pltpu.VMEM((N_EXPERTS, n), F32), pltpu.VMEM((N_EXPERTS, LANES), F32)],
        compiler_params=_cparams(("arbitrary",)),
        name="plan",
    )(info, before)


def _sc_mesh():
    return plsc.VectorSubcoreMesh(core_axis_name="core", subcore_axis_name="subcore")


def _sc_scatter_rows(x, idx, n_out):
    n_idx = idx.shape[0]
    src_blocks = x.shape[0] // SC_WIN

    @pl.kernel(out_type=jax.ShapeDtypeStruct((n_out, SC_ROW), x.dtype), mesh=_sc_mesh(), scratch_types=[])
    def scatter_kernel(x_hbm, i_hbm, o_hbm):
        def body(x_vmem, i_vmem):
            pltpu.sync_copy(x_vmem, o_hbm.at[i_vmem.at[0]])

        pltpu.emit_pipeline(
            body,
            grid=(n_idx // SC_WIN,),
            in_specs=[pl.BlockSpec((SC_WIN, SC_ROW), index_map=lambda i: (i % src_blocks, 0)),
                      pl.BlockSpec((1, SC_WIN), index_map=lambda i: (0, i))],
            out_specs=[],
            core_axis_name=("core", "subcore"),
            dimension_semantics=(pltpu.PARALLEL,),
        )(x_hbm, i_hbm)

    return scatter_kernel(x, idx.reshape(1, n_idx))


def _sc_gather_rows(x, idx):
    n_idx = idx.shape[0]

    @pl.kernel(out_type=jax.ShapeDtypeStruct((n_idx, SC_ROW), x.dtype), mesh=_sc_mesh(), scratch_types=[])
    def gather_kernel(x_hbm, i_hbm, o_hbm):
        def body(i_vmem, o_vmem):
            pltpu.sync_copy(x_hbm.at[i_vmem.at[0]], o_vmem)

        pltpu.emit_pipeline(
            body,
            grid=(n_idx // SC_WIN,),
            in_specs=[pl.BlockSpec((1, SC_WIN), index_map=lambda i: (0, i))],
            out_specs=[pl.BlockSpec((SC_WIN, SC_ROW), index_map=lambda i: (i, 0))],
            core_axis_name=("core", "subcore"),
            dimension_semantics=(pltpu.PARALLEL,),
        )(i_hbm, o_hbm)

    return gather_kernel(x, idx.reshape(1, n_idx))


def _experts_kernel(tmap_ref, xs_ref, wg_ref, wu_ref, wd_ref, ys_ref, wg_scr, wu_scr, wd_scr):
    j = pl.program_id(0)

    @pl.when((j == 0) | (tmap_ref[0, j] != tmap_ref[0, jnp.maximum(j - 1, 0)]))
    def _():
        wg_scr[...] = wg_ref[0].astype(BF16)
        wu_scr[...] = wu_ref[0].astype(BF16)
        wd_scr[...] = wd_ref[0].astype(BF16)

    @pl.when(tmap_ref[1, j] > 0)
    def _():
        x = _unpack_bf16_pairs(xs_ref[0:SLOT_TM, :], xs_ref[SLOT_TM:2 * SLOT_TM, :])
        a = _dot(x, wg_scr[...])
        u = _dot(x, wu_scr[...])
        act = (a * jax.nn.sigmoid(a) * u).astype(BF16)
        y = _dot(act, wd_scr[...]).astype(BF16).astype(F32)
        packed = _pack_bf16_pairs(y)
        ys_ref[0:SLOT_TM, :] = packed[:, :SC_ROW]
        ys_ref[SLOT_TM:2 * SLOT_TM, :] = packed[:, SC_ROW:]


def _experts(xs, tmap, wg, wu, wd):
    n_tiles = xs.shape[0] // (2 * SLOT_TM)
    grid_spec = pltpu.PrefetchScalarGridSpec(
        num_scalar_prefetch=1,
        grid=(n_tiles,),
        in_specs=[
            pl.BlockSpec((2 * SLOT_TM, SC_ROW), lambda j, tm: (tm[2, j], 0)),
            pl.BlockSpec((1, D_MODEL, D_EXPERT), lambda j, tm: (tm[0, j], 0, 0)),
            pl.BlockSpec((1, D_MODEL, D_EXPERT), lambda j, tm: (tm[0, j], 0, 0)),
            pl.BlockSpec((1, D_EXPERT, D_MODEL), lambda j, tm: (tm[0, j], 0, 0)),
        ],
        out_specs=pl.BlockSpec((2 * SLOT_TM, SC_ROW), lambda j, tm: (tm[2, j], 0)),
        scratch_shapes=[pltpu.VMEM((D_MODEL, D_EXPERT), BF16), pltpu.VMEM((D_MODEL, D_EXPERT), BF16),
                        pltpu.VMEM((D_EXPERT, D_MODEL), BF16)],
    )
    return pl.pallas_call(
        _experts_kernel,
        out_shape=jax.ShapeDtypeStruct(xs.shape, jnp.int32),
        grid_spec=grid_spec,
        compiler_params=_cparams(("arbitrary",)),
        name="experts",
    )(tmap, xs, wg, wu, wd)


FIN_TM = 1024


def _final_kernel(x1_ref, mod_ref, yg_ref, w_ref, gf_ref, y_ref):
    moe = None
    for k in range(2):
        base = 2 * k * FIN_TM
        yk = _unpack_bf16_pairs(yg_ref[base:base + FIN_TM, :], yg_ref[base + FIN_TM:base + 2 * FIN_TM, :])
        term = w_ref[:, k:k + 1] * yk.astype(F32)
        moe = term if moe is None else moe + term
    x2 = x1_ref[...] + mod_ref[0, 5:6, :] * moe
    y_ref[...] = x2 * _rms_scale(x2) * gf_ref[...]


def _final(x1, mod3, mod_row_fn, yg, w_cols, g_final):
    n = x1.shape[0]
    return pl.pallas_call(
        _final_kernel,
        out_shape=jax.ShapeDtypeStruct((n, D_MODEL), F32),
        grid=(n // FIN_TM,),
        in_specs=[
            pl.BlockSpec((FIN_TM, D_MODEL), lambda i: (i, 0)),
            pl.BlockSpec((1, 6, D_MODEL), lambda i: (mod_row_fn(i * FIN_TM), 0, 0)),
            pl.BlockSpec((4 * FIN_TM, SC_ROW), lambda i: (i, 0)),
            pl.BlockSpec((FIN_TM, 2), lambda i: (i, 0)),
            _const_spec((1, D_MODEL)),
        ],
        out_specs=pl.BlockSpec((FIN_TM, D_MODEL), lambda i: (i, 0)),
        compiler_params=_cparams(("parallel",)),
        name="final",
    )(x1, mod3, yg, w_cols, g_final)


def _moe(x1, h2p, info, mod3, mod_row_fn, wg, wu, wd, g_final):
    n = x1.shape[0]
    sidx, gidx, tmap = _plan(info)
    n_slots = 2 * n + N_EXPERTS * SLOT_TM
    xs = _sc_scatter_rows(h2p, sidx.reshape(-1), 2 * n_slots)
    ys = _experts(xs, tmap, wg, wu, wd)
    yg = _sc_gather_rows(ys, gidx.reshape(-1))
    w_cols = info[2:4].T
    return _final(x1, mod3, mod_row_fn, yg, w_cols, g_final)


def _stream(x, mod3, mod_row_fn, seq_len, rows_2d, state, w):
    bsz, t, _ = x.shape
    n = bsz * t
    x2d = x.reshape(n, D_MODEL)
    p3, bcum, imb, amax = _in_projection(x2d, mod3, mod_row_fn, w["g_norm1"], w["wmain"], w["bmain"],
                                         w["colscale"], w["w_gates"], w["b_gates"])
    m0 = None if state is None else state[2]
    cols, rows, cs = _gate_prep(bcum, imb, amax, m0)
    ml = _mlstm(p3, cols, rows, cs, w["gn3"], None if state is None else state[:2])
    amat, invc = _pool_constants(seq_len, rows_2d)
    n_seq = max(1, 1024 // seq_len)
    yb = _pool(p3, amat, invc, w["wpg"], w["pscale"], seq_len, n_seq, rows_2d)
    ya = ml if state is not None else ml[0]
    x1, h2p, info = _mix(x2d, mod3, mod_row_fn, ya, yb, p3, w["wa"], w["wb"], w["wo"],
                         w["g_norm2"], w["wr"], w["br"])
    y = _moe(x1, h2p, info, mod3, mod_row_fn, w["wg"], w["wu"], w["wd"], w["g_final"])
    return y.reshape(bsz, t, D_MODEL), ml, cs


def kernel(x_prompt, x_sample, state_C, state_n, state_m, c, c_ctx, w_mod, b_mod, g_norm1, w_in, b_in, gn_gain, w_pool_grp, pool_scale, w_proj_a, w_proj_b, w_out, g_norm2, w_router_grp, b_router_grp, w_router_exp, b_router_exp, w_exp_gate, w_exp_up, w_exp_down, g_final):
    bp, tp, _ = x_prompt.shape
    bs, ts, _ = x_sample.shape
    assert w_mod.shape[0] == 1, "single trunk layer"
    assert tp == CHUNK and ts % CHUNK == 0 and ts == ML_ROWS

    cvec = jnp.zeros((8, D_MODEL), F32).at[0].set(c_ctx).at[1:1 + bs].set(c)
    mod3 = _modulation(cvec, w_mod[0], b_mod[0])

    wi, bi = w_in[0], b_in[0]
    d = D_MODEL
    o_q, o_k, o_v, o_o, o_g, o_p, o_ga, o_gb = 0, d, 2 * d, 3 * d, 4 * d, 4 * d + 32, 4 * d + 32 + 512, 4 * d + 32 + 512 + d
    order = [(o_q, d), (o_k, d), (o_v, d), (o_o, d), (o_ga, d), (o_gb, d), (o_p, 512)]
    wmain = _pack_columns(wi, order)
    bmain = jnp.concatenate([bi[s:s + l] for s, l in order]).reshape(1, D_MAIN)
    colscale = jnp.ones((D_MAIN,), F32).at[SLAB_K * LANES:(SLAB_K + N_HEADS) * LANES].set(HEAD_DIM ** -0.5)
    colscale = colscale.reshape(1, D_MAIN)
    gwt = _take_columns_t(wi, o_g, 32)
    gb = bi[o_g:o_g + 32]
    i_f, f_f, i_b, f_b = (slice(0, 8), slice(8, 16), slice(16, 24), slice(24, 32))
    pad_w = jnp.zeros((LANES - 2 * N_HD, d), F32)
    pad_b = jnp.zeros((LANES - 2 * N_HD,), F32)
    wg = jnp.concatenate([gwt[f_f], gwt[f_b], gwt[i_f], gwt[i_b], pad_w], axis=0).T
    bg = jnp.concatenate([gb[f_f], gb[f_b], gb[i_f], gb[i_b], pad_b]).reshape(1, LANES)
    wg = jnp.concatenate(_split_bf16(wg), axis=1)

    wr = jnp.concatenate([w_router_grp[0], w_router_exp[0].reshape(d, N_EXPERTS),
                          jnp.zeros((d, LANES - ROUTER_ROWS), F32)], axis=1)
    br = jnp.concatenate([b_router_grp[0], b_router_exp[0].reshape(N_EXPERTS),
                          jnp.zeros((LANES - ROUTER_ROWS,), F32)]).reshape(1, LANES)
    wr = jnp.concatenate(_split_bf16(wr), axis=1)

    w = dict(
        g_norm1=g_norm1[0].reshape(1, d), wmain=wmain, bmain=bmain, colscale=colscale,
        w_gates=wg, b_gates=bg,
        gn3=gn_gain[0].reshape(N_HEADS, 1, HEAD_DIM),
        wpg=w_pool_grp[0].astype(BF16), pscale=pool_scale[0].reshape(N_GROUPS, 1, LANES),
        wa=w_proj_a[0].astype(BF16), wb=w_proj_b[0].astype(BF16), wo=w_out[0].astype(BF16),
        g_norm2=g_norm2[0].reshape(1, d), wr=wr, br=br,
        wg=w_exp_gate[0], wu=w_exp_up[0], wd=w_exp_down[0],
        g_final=g_final.reshape(1, d),
    )

    y_prompt, (_, cst, nst), cs_prompt = _stream(x_prompt, mod3, lambda r: 0, tp, None, None, w)
    n_rep = jnp.broadcast_to(state_n[..., None], state_C.shape)
    m0 = jnp.pad(state_m[:, 0].reshape(bs, N_HD), ((0, 0), (0, LANES - N_HD))).reshape(bs, 1, LANES)
    y_sample, _, _ = _stream(x_sample, mod3, lambda r: 1 + r // ts, ts, ts // GRID_W, (state_C, n_rep, m0), w)

    new_c = cst
    new_n = nst.reshape(bp, 1, 2, N_HEADS, HEAD_DIM)
    new_m = cs_prompt[:, 1, :N_HD].reshape(bp, 1, 2, N_HEADS)
    return (y_prompt, y_sample, new_c, new_n, new_m)
```

```python
import functools

import numpy as np
import jax
import jax.numpy as jnp
from jax import lax
from jax.experimental import pallas as pl
from jax.experimental.pallas import tpu as pltpu
from jax.experimental.pallas import tpu_sc as plsc

F32 = jnp.float32
BF16 = jnp.bfloat16

D_MODEL = 1024
N_HEADS = 8
HEAD_DIM = 128
GRID_W = 64
POOL_WINDOWS = (2, 4, 8, 16)
N_GROUPS = 4
EXPERTS_PER_GROUP = 4
N_EXPERTS = 16
D_EXPERT = 512
EPS = 1e-6
NEG = -1e30

LANES = 128
CHUNK = 256
N_SLABS = 52
SLAB_Q, SLAB_K, SLAB_V, SLAB_O, SLAB_GA, SLAB_GB, SLAB_POOL = 0, 8, 16, 24, 32, 40, 48
D_MAIN = N_SLABS * LANES
VMEM_LIMIT = 56 * 1024 * 1024


def _cparams(sem):
    return pltpu.CompilerParams(dimension_semantics=sem, vmem_limit_bytes=VMEM_LIMIT)


def _const_spec(shape):
    nd = len(shape)
    return pl.BlockSpec(shape, lambda *_: (0,) * nd, pipeline_mode=pl.Buffered(1))


def _split_bf16(x):
    hi = x.astype(BF16)
    lo = (x - hi.astype(F32)).astype(BF16)
    return hi, lo


def _dot(a, b):
    return jnp.dot(a, b, preferred_element_type=F32)


def _dot3(a, b_hi, b_lo):
    a_hi, a_lo = _split_bf16(a)
    return _dot(a_hi, b_hi) + _dot(a_hi, b_lo) + _dot(a_lo, b_hi)


def _dot3_packed(a_hi, a_lo, b_hilo):
    first = _dot(a_hi, b_hilo)
    return first[:, :LANES] + first[:, LANES:] + _dot(a_lo, b_hilo[:, :LANES])


def _rms_scale(x):
    return lax.rsqrt(jnp.mean(x * x, axis=-1, keepdims=True) + EPS)


def _log_sigmoid(x):
    return jnp.minimum(x, 0.0) - jnp.log1p(jnp.exp(-jnp.abs(x)))


def _mod_kernel(c_ref, w_ref, b_ref, o_ref):
    c = c_ref[...]
    s = c * jax.nn.sigmoid(c)
    w_hi, w_lo = _split_bf16(w_ref[...])
    o_ref[...] = _dot3(s, w_hi, w_lo) + b_ref[...]


def _modulation(cvec, w_mod, b_mod):
    tn = 2048
    out = pl.pallas_call(
        _mod_kernel,
        out_shape=jax.ShapeDtypeStruct((8, 6 * D_MODEL), F32),
        grid=(6 * D_MODEL // tn,),
        in_specs=[
            pl.BlockSpec((8, D_MODEL), lambda j: (0, 0)),
            pl.BlockSpec((D_MODEL, tn), lambda j: (0, j)),
            pl.BlockSpec((1, tn), lambda j: (0, j)),
        ],
        out_specs=pl.BlockSpec((8, tn), lambda j: (0, j)),
        compiler_params=_cparams(("parallel",)),
        name="modulation",
    )(cvec, w_mod, b_mod.reshape(1, -1))
    return out.reshape(8, 6, D_MODEL)


PACK_COLS = 512


def _pack_kernel(wt_ref, o_ref):
    o_ref[...] = wt_ref[...].T.astype(BF16)


def _pack_columns(w, order):
    k, n = w.shape
    starts = []
    for src, width in order:
        assert width % PACK_COLS == 0
        starts += [src + j for j in range(0, width, PACK_COLS)]
    n_out = len(starts) * PACK_COLS

    unit = 32
    assert all(s % unit == 0 for s in starts)

    def src_row(j):
        row = jnp.int32(starts[-1] // unit)
        for idx in range(len(starts) - 2, -1, -1):
            row = jnp.where(j == idx, starts[idx] // unit, row)
        return row * unit

    return pl.pallas_call(
        _pack_kernel,
        out_shape=jax.ShapeDtypeStruct((k, n_out), BF16),
        grid=(len(starts),),
        in_specs=[pl.BlockSpec((pl.Element(PACK_COLS), pl.Element(k)), lambda j: (src_row(j), 0))],
        out_specs=pl.BlockSpec((k, PACK_COLS), lambda j: (0, j)),
        compiler_params=_cparams(("parallel",)),
        name="pack_columns",
    )(w.T)


def _copy_kernel(w_ref, o_ref):
    o_ref[...] = w_ref[...]


def _take_columns_t(w, start, width):
    k = w.shape[0]
    return pl.pallas_call(
        _copy_kernel,
        out_shape=jax.ShapeDtypeStruct((width, k), w.dtype),
        grid=(1,),
        in_specs=[pl.BlockSpec((pl.Element(width), pl.Element(k)), lambda j: (start, 0))],
        out_specs=pl.BlockSpec((width, k), lambda j: (0, 0)),
        name="take_columns",
    )(w.T)


IN_TM = 512
IN_TN = 512
LOG2E = 1.4426950408889634
N_HD = 2 * N_HEADS


def _fwd_lanes(shape):
    lane = lax.broadcasted_iota(jnp.int32, shape, len(shape) - 1)
    return (lane & N_HEADS) == 0


def _inproj_kernel(x_ref, mod_ref, g_ref, w_ref, b_ref, cs_ref, wg_ref, bg_ref,
                   p3_ref, bcum_ref, imb_ref, amax_ref):
    x = x_ref[...]
    h = x * _rms_scale(x) * g_ref[...]
    h = h * (1.0 + mod_ref[0, 1:2, :]) + mod_ref[0, 0:1, :]
    hb = h.astype(BF16)
    for n in range(D_MAIN // IN_TN):
        sl = slice(n * IN_TN, (n + 1) * IN_TN)
        acc = (_dot(hb, w_ref[:, sl]) + b_ref[:, sl]) * cs_ref[:, sl]
        accb = acc.astype(BF16)
        for j in range(IN_TN // LANES):
            p3_ref[n * (IN_TN // LANES) + j] = accb[:, j * LANES:(j + 1) * LANES]

    h_lo = (h - hb.astype(F32)).astype(BF16)
    gates = _dot3_packed(hb, h_lo, wg_ref[...]) + bg_ref[...]
    ls = _log_sigmoid(gates)
    gi = pltpu.roll(gates, LANES - N_HD, 1)
    row = lax.broadcasted_iota(jnp.int32, (CHUNK, LANES), 0)
    fwd = _fwd_lanes((CHUNK, LANES))
    for c in range(IN_TM // CHUNK):
        rows = slice(c * CHUNK, (c + 1) * CHUNK)
        cf = ls[rows]
        cb = cf
        s = 1
        while s < CHUNK:
            cf = cf + jnp.where(row >= s, pltpu.roll(cf, s, 0), 0.0)
            cb = cb + jnp.where(row < CHUNK - s, pltpu.roll(cb, CHUNK - s, 0), 0.0)
            s *= 2
        b = jnp.where(fwd, cf, cb)
        imb = gi[rows] - b
        mf = imb
        mb = imb
        s = 1
        while s < CHUNK:
            mf = jnp.maximum(mf, jnp.where(row >= s, pltpu.roll(mf, s, 0), -jnp.inf))
            mb = jnp.maximum(mb, jnp.where(row < CHUNK - s, pltpu.roll(mb, CHUNK - s, 0), -jnp.inf))
            s *= 2
        bcum_ref[rows, :] = b
        imb_ref[rows, :] = imb
        amax_ref[rows, :] = b + jnp.where(fwd, mf, mb)


def _in_projection(x2d, mod3, mod_row_fn, g_norm1, wmain, bmain, colscale, wg, bg):
    n = x2d.shape[0]
    return pl.pallas_call(
        _inproj_kernel,
        out_shape=(jax.ShapeDtypeStruct((N_SLABS, n, LANES), BF16),) + (jax.ShapeDtypeStruct((n, LANES), F32),) * 3,
        grid=(n // IN_TM,),
        in_specs=[
            pl.BlockSpec((IN_TM, D_MODEL), lambda i: (i, 0)),
            pl.BlockSpec((1, 6, D_MODEL), lambda i: (mod_row_fn(i * IN_TM), 0, 0)),
            _const_spec((1, D_MODEL)),
            _const_spec((D_MODEL, D_MAIN)),
            _const_spec((1, D_MAIN)),
            _const_spec((1, D_MAIN)),
            _const_spec((D_MODEL, 2 * LANES)),
            _const_spec((1, LANES)),
        ],
        out_specs=(pl.BlockSpec((N_SLABS, IN_TM, LANES), lambda i: (0, i, 0)),)
        + (pl.BlockSpec((IN_TM, LANES), lambda i: (i, 0)),) * 3,
        compiler_params=_cparams(("parallel",)),
        name="in_projection",
    )(x2d, mod3, g_norm1, wmain, bmain, colscale, wg, bg)


GP_ROWS = 1024
GP_NCH = GP_ROWS // CHUNK
Q_C2, Q_EM, Q_WI, Q_WC = 0, 1, 2, 3


def _gate_prep_kernel(*refs, carry):
    if carry:
        b_ref, imb_ref, a_ref, m0_ref, cols_ref, rw_ref, cs_ref = refs
    else:
        b_ref, imb_ref, a_ref, cols_ref, rw_ref, cs_ref = refs
    fwd1 = _fwd_lanes((1, LANES))
    lane = lax.broadcasted_iota(jnp.int32, (CHUNK, LANES), 1)
    neg = jnp.full((1, LANES), NEG, F32)

    def at_scan_end(ref, c):
        return jnp.where(fwd1, ref[(c + 1) * CHUNK - 1:(c + 1) * CHUNK, :], ref[c * CHUNK:c * CHUNK + 1, :])

    b_last = [at_scan_end(b_ref, c) for c in range(GP_NCH)]
    a_last = [at_scan_end(a_ref, c) for c in range(GP_NCH)]
    m_start = m0_ref[0] if carry else neg
    m_prev = [[None] * GP_NCH, [None] * GP_NCH]
    m_new = [[None] * GP_NCH, [None] * GP_NCH]
    for d, order in enumerate((range(GP_NCH), range(GP_NCH - 1, -1, -1))):
        m = m_start
        for c in order:
            m_prev[d][c] = m
            m_new[d][c] = jnp.maximum(b_last[c] + m, a_last[c])
            m = m_new[d][c] if carry else neg

    row_src = []
    for c in range(GP_NCH):
        rows = slice(c * CHUNK, (c + 1) * CHUNK)
        mp = jnp.where(fwd1, m_prev[0][c], m_prev[1][c])
        mn = jnp.where(fwd1, m_new[0][c], m_new[1][c])
        b = b_ref[rows, :]
        imb = imb_ref[rows, :]
        inter = b + mp
        m_t = jnp.maximum(inter, a_ref[rows, :])
        c2 = (b - m_t) * LOG2E
        em = jnp.exp(-m_t)
        wc = jnp.exp(b_last[c] - mn + imb)
        packed = jnp.where(lane < N_HD, c2, pltpu.roll(em, N_HD * Q_EM, 1))
        if carry:
            wi = jnp.exp(inter - m_t)
            packed = jnp.where(lane < N_HD * Q_WI, packed, pltpu.roll(wi, N_HD * Q_WI, 1))
        cols_ref[rows, :] = jnp.where(lane < N_HD * Q_WC, packed, pltpu.roll(wc, N_HD * Q_WC, 1))
        row_src.append(jnp.where(lane < N_HD, imb * (-LOG2E), pltpu.roll(wc, N_HD, 1)))
        cs_ref[c, 0:1, :] = jnp.exp(b_last[c] + mp - mn)
        cs_ref[c, 1:2, :] = mn

    rows_t = jnp.concatenate(row_src, axis=0).T
    for r in range(2 * N_HD):
        rw_ref[r] = rows_t[r:r + 1]


def _gate_prep(bcum, imb, amax, m0=None):
    n = bcum.shape[0]
    carry = m0 is not None
    blk = pl.BlockSpec((GP_ROWS, LANES), lambda i: (i, 0))
    in_specs = [blk, blk, blk]
    args = [bcum, imb, amax]
    if carry:
        in_specs.append(pl.BlockSpec((1, 1, LANES), lambda i: (i, 0, 0)))
        args.append(m0)
    return pl.pallas_call(
        functools.partial(_gate_prep_kernel, carry=carry),
        out_shape=(jax.ShapeDtypeStruct((n, LANES), F32), jax.ShapeDtypeStruct((2 * N_HD, 1, n), F32),
                   jax.ShapeDtypeStruct((n // CHUNK, 2, LANES), F32)),
        grid=(n // GP_ROWS,),
        in_specs=in_specs,
        out_specs=(blk, pl.BlockSpec((2 * N_HD, 1, GP_ROWS), lambda i: (0, 0, i)),
                   pl.BlockSpec((GP_NCH, 2, LANES), lambda i: (i, 0, 0))),
        compiler_params=_cparams(("parallel",)),
        name="gate_prep",
    )(*args)


ML_ROWS = 1024
ML_SEQS = 4
ML_ROWS_FREE = 4096


def _pick_col(blk, lane_iota, idx):
    return jnp.sum(jnp.where(lane_iota == idx, blk, 0.0), axis=1, keepdims=True)


def _mlstm_kernel(*refs, carry, n_chunks, n_seqs):
    if carry:
        (q_ref, k_ref, v_ref, o_ref, cols_ref, r2f_ref, r2b_ref, gn_ref, cs_ref, c0_ref, n0_ref,
         ya_ref, s_scr, h_scr) = refs
    else:
        (q_ref, k_ref, v_ref, o_ref, cols_ref, r2f_ref, r2b_ref, gn_ref, wrf_ref, wrb_ref,
         ya_ref, cst_ref, nst_ref, s_scr, h_scr) = refs
    head = pl.program_id(1)
    lane = lax.broadcasted_iota(jnp.int32, (CHUNK, LANES), 1)
    lane1 = lax.broadcasted_iota(jnp.int32, (1, LANES), 1)
    r_idx = lax.broadcasted_iota(jnp.int32, (CHUNK, CHUNK), 0)
    c_idx = lax.broadcasted_iota(jnp.int32, (CHUNK, CHUNK), 1)
    ones = jnp.ones((CHUNK, HEAD_DIM), BF16)
    tn_dims = (((0,), (0,)), ((), ()))

    for d in range(2):
        mask = (c_idx <= r_idx) if d == 0 else (c_idx >= r_idx)
        r2_ref = r2f_ref if d == 0 else r2b_ref
        base = N_HEADS * d + head
        per_seq = n_chunks // n_seqs
        local = range(per_seq) if d == 0 else range(per_seq - 1, -1, -1)
        for sq, step, c in [(sq, step, sq * per_seq + lc) for sq in range(n_seqs) for step, lc in enumerate(local)]:
            if carry and step == 0:
                cn = jnp.concatenate([c0_ref[sq, 0, d, 0], n0_ref[sq, 0, d, 0]], axis=1)
            rows = slice(c * CHUNK, (c + 1) * CHUNK)
            q = q_ref[0, rows, :]
            k = k_ref[0, rows, :]
            v = v_ref[0, rows, :]
            v1 = jnp.concatenate([v, ones], axis=1)
            if d == 0:
                s = lax.dot_general(q, k, (((1,), (1,)), ((), ())), preferred_element_type=F32)
                s_scr[rows, :] = s
            else:
                s = s_scr[rows, :]
            blk = cols_ref[rows, :]
            arg = _pick_col(blk, lane, base + N_HD * Q_C2) - r2_ref[0, :, rows]
            sw = s * jnp.exp2(jnp.where(mask, arg, NEG))
            nd = _dot(sw.astype(BF16), v1)
            if carry:
                nd = nd + _pick_col(blk, lane, base + N_HD * Q_WI) * _dot(q, cn.astype(BF16))
            hdir = nd[:, :HEAD_DIM] / jnp.maximum(jnp.abs(nd[:, HEAD_DIM:]), _pick_col(blk, lane, base + N_HD * Q_EM))
            if d == 0:
                h_scr[rows, :] = hdir
            else:
                h_scr[rows, :] = h_scr[rows, :] + hdir

            if carry and step == per_seq - 1:
                continue
            w_col = _pick_col(blk, lane, base + N_HD * Q_WC)
            if carry:
                u = lax.dot_general(k, (w_col * v1.astype(F32)).astype(BF16), tn_dims,
                                    preferred_element_type=F32)
                decay = jnp.sum(jnp.where(lane1 == base, cs_ref[c, 0:1, :], 0.0), axis=1, keepdims=True)
                cn = decay * cn + u
            else:
                w_row = (wrf_ref if d == 0 else wrb_ref)[0, :, rows]
                w8 = jnp.broadcast_to(w_row, (8, CHUNK)).astype(BF16)
                cst_ref[c, 0, d, 0] = lax.dot_general(k, (w_col * v.astype(F32)).astype(BF16), tn_dims,
                                                      preferred_element_type=F32)
                nst_ref[c, d, 0] = _dot(w8, k)[0:1, :]

    hh = h_scr[...]
    hn = hh * _rms_scale(hh) * gn_ref[0]
    ya_ref[0] = (hn * jax.nn.sigmoid(o_ref[0].astype(F32))).astype(BF16)


def _mlstm(p3, cols, rows, cs, gn3, state=None):
    n = p3.shape[1]
    carry = state is not None
    n_seqs = ML_SEQS if carry else 1
    rows_step = ML_ROWS * ML_SEQS if carry else ML_ROWS_FREE
    n_chunks = rows_step // CHUNK
    nblk = n // rows_step

    def slab(base):
        return pl.BlockSpec((1, rows_step, LANES), lambda g, h: (base + h, g, 0))

    def row(base):
        return pl.BlockSpec((1, 1, rows_step), lambda g, h: (base + h, 0, g))

    in_specs = [
        slab(SLAB_Q), slab(SLAB_K), slab(SLAB_V), slab(SLAB_O),
        pl.BlockSpec((rows_step, LANES), lambda g, h: (g, 0)),
        row(0), row(N_HEADS),
        pl.BlockSpec((1, 1, LANES), lambda g, h: (h, 0, 0)),
    ]
    args = [p3, p3, p3, p3, cols, rows, rows, gn3]
    ya_shape = jax.ShapeDtypeStruct((N_HEADS, n, LANES), BF16)
    ya_spec = pl.BlockSpec((1, rows_step, LANES), lambda g, h: (h, g, 0))
    state_spec = pl.BlockSpec((n_seqs, 1, 2, 1, HEAD_DIM, HEAD_DIM), lambda g, h: (g, 0, 0, h, 0, 0))
    if carry:
        c0, n0 = state
        in_specs += [pl.BlockSpec((n_chunks, 2, LANES), lambda g, h: (g, 0, 0)), state_spec, state_spec]
        args += [cs, c0, n0]
        out_shape = ya_shape
        out_specs = ya_spec
    else:
        in_specs += [row(N_HD), row(N_HD + N_HEADS)]
        args += [rows, rows]
        nseq = n // CHUNK
        out_shape = (ya_shape,
                     jax.ShapeDtypeStruct((nseq, 1, 2, N_HEADS, HEAD_DIM, HEAD_DIM), F32),
                     jax.ShapeDtypeStruct((nseq, 2, N_HEADS, 1, HEAD_DIM), F32))
        out_specs = (ya_spec,
                     pl.BlockSpec((n_chunks, 1, 2, 1, HEAD_DIM, HEAD_DIM), lambda g, h: (g, 0, 0, h, 0, 0)),
                     pl.BlockSpec((n_chunks, 2, 1, 1, HEAD_DIM), lambda g, h: (g, 0, h, 0, 0)))
    return pl.pallas_call(
        functools.partial(_mlstm_kernel, carry=carry, n_chunks=n_chunks, n_seqs=n_seqs),
        out_shape=out_shape,
        grid=(nblk, N_HEADS),
        in_specs=in_specs,
        out_specs=out_specs,
        scratch_shapes=[pltpu.VMEM((rows_step, CHUNK), F32), pltpu.VMEM((rows_step, HEAD_DIM), F32)],
        compiler_params=_cparams(("parallel", "parallel")),
        name="mlstm_latent" if carry else "mlstm_prompt",
    )(*args)


def _pool_kernel(u_ref, a_ref, ic_ref, wpg_ref, ps_ref, yb_ref, *, seq_len, n_seq, rows_2d):
    piece = seq_len if rows_2d is None else GRID_W
    n_piece = n_seq if rows_2d is None else rows_2d
    for g in range(N_GROUPS):
        u = jnp.concatenate([u_ref[g, i * piece:(i + 1) * piece, :] for i in range(n_piece)], axis=1)
        win = _dot(a_ref[g], u)
        if rows_2d is None:
            inv_cnt = jnp.concatenate([ic_ref[g]] * n_piece, axis=1)
        else:
            run = [jnp.zeros((piece, LANES), F32)]
            for r in range(rows_2d):
                run.append(run[-1] + win[:, r * LANES:(r + 1) * LANES])
            half = POOL_WINDOWS[g] // 2
            win = jnp.concatenate([run[min(r - half + POOL_WINDOWS[g], rows_2d)] - run[max(r - half, 0)]
                                   for r in range(rows_2d)], axis=1)
            inv_cnt = jnp.concatenate([ic_ref[g, r * piece:(r + 1) * piece, :] for r in range(rows_2d)], axis=1)
        p = (win * inv_cnt - u.astype(F32)).astype(BF16)
        p = jnp.concatenate([p[:, i * LANES:(i + 1) * LANES] for i in range(n_piece)], axis=0)
        yb_ref[g] = (_dot(p, wpg_ref[g]) * ps_ref[g]).astype(BF16)


def _pool(p3, amat, invc, wpg, pscale, seq_len, n_seq, rows_2d):
    n = p3.shape[1]
    rows = seq_len * n_seq
    return pl.pallas_call(
        functools.partial(_pool_kernel, seq_len=seq_len, n_seq=n_seq, rows_2d=rows_2d),
        out_shape=jax.ShapeDtypeStruct((N_GROUPS, n, LANES), BF16),
        grid=(n // rows,),
        in_specs=[
            pl.BlockSpec((N_GROUPS, rows, LANES), lambda i: (SLAB_POOL // N_GROUPS, i, 0)),
            _const_spec(amat.shape),
            _const_spec((N_GROUPS, seq_len, LANES)),
            _const_spec((N_GROUPS, LANES, LANES)),
            _const_spec((N_GROUPS, 1, LANES)),
        ],
        out_specs=pl.BlockSpec((N_GROUPS, rows, LANES), lambda i: (0, i, 0)),
        compiler_params=_cparams(("parallel",)),
        name="pool",
    )(p3, amat, invc, wpg, pscale)


def _window_matrix(size, w):
    idx = np.arange(size)
    lo = np.clip(idx - w // 2, 0, size)
    hi = np.clip(idx - w // 2 + w, 0, size)
    s = np.arange(size)[None, :]
    a = ((s >= lo[:, None]) & (s < hi[:, None])).astype(np.float32)
    return a, (hi - lo).astype(np.float32)


def _pool_constants(seq_len, rows_2d):
    mats, invs = [], []
    for w in POOL_WINDOWS:
        if rows_2d is None:
            a, cnt = _window_matrix(seq_len, w)
        else:
            _, cr = _window_matrix(rows_2d, w)
            a, cc = _window_matrix(GRID_W, w)
            cnt = np.kron(cr, cc)
        mats.append(a)
        invs.append(np.repeat((1.0 / cnt)[:, None], LANES, axis=1))
    return (jnp.asarray(np.stack(mats), dtype=BF16), jnp.asarray(np.stack(invs), dtype=F32))


MIX_TM = 512


def _mix_kernel(x_ref, mod_ref, ya_ref, yb_ref, ga_ref, gb_ref, wa_ref, wb_ref, wo_ref,
                g2_ref, wr_ref, br_ref, x1_ref, h2_ref, info_ref):
    ya = jnp.concatenate([ya_ref[i] for i in range(N_HEADS)], axis=1)
    yb = jnp.concatenate([yb_ref[i] for i in range(N_GROUPS)], axis=1)
    ga = jnp.concatenate([ga_ref[i] for i in range(8)], axis=1).astype(F32)
    gb = jnp.concatenate([gb_ref[i] for i in range(8)], axis=1).astype(F32)
    merged = jax.nn.sigmoid(ga) * _dot(ya, wa_ref[...]) + jax.nn.sigmoid(gb) * _dot(yb, wb_ref[...])
    out = _dot(merged.astype(BF16), wo_ref[...])
    x1 = x_ref[...] + mod_ref[0, 2:3, :] * out
    x1_ref[...] = x1

    h2 = x1 * _rms_scale(x1) * g2_ref[...]
    h2 = h2 * (1.0 + mod_ref[0, 4:5, :]) + mod_ref[0, 3:4, :]
    h_hi, h_lo = _split_bf16(h2)
    packed = _pack_bf16_pairs(h_hi.astype(F32))
    h2_ref[0:MIX_TM, :] = packed[:, :SC_ROW]
    h2_ref[MIX_TM:2 * MIX_TM, :] = packed[:, SC_ROW:]
    logits = _dot3_packed(h_hi, h_lo, wr_ref[...]) + br_ref[...]
    first, second, w1, w2 = _route_rows(logits.T)
    zero = jnp.zeros_like(w1)
    info_ref[...] = jnp.concatenate([first, second, w1, w2, zero, zero, zero, zero], axis=0)


def _mix(x2d, mod3, mod_row_fn, ya, yb, p3, wa, wb, wo, g_norm2, wr, br):
    n = x2d.shape[0]
    return pl.pallas_call(
        _mix_kernel,
        out_shape=(jax.ShapeDtypeStruct((n, D_MODEL), F32),
                   jax.ShapeDtypeStruct((2 * n, SC_ROW), jnp.int32), jax.ShapeDtypeStruct((8, n), F32)),
        grid=(n // MIX_TM,),
        in_specs=[
            pl.BlockSpec((MIX_TM, D_MODEL), lambda i: (i, 0)),
            pl.BlockSpec((1, 6, D_MODEL), lambda i: (mod_row_fn(i * MIX_TM), 0, 0)),
            pl.BlockSpec((N_HEADS, MIX_TM, LANES), lambda i: (0, i, 0)),
            pl.BlockSpec((N_GROUPS, MIX_TM, LANES), lambda i: (0, i, 0)),
            pl.BlockSpec((8, MIX_TM, LANES), lambda i: (SLAB_GA // 8, i, 0)),
            pl.BlockSpec((8, MIX_TM, LANES), lambda i: (SLAB_GB // 8, i, 0)),
            _const_spec((D_MODEL, D_MODEL)),
            _const_spec((N_GROUPS * LANES, D_MODEL)),
            _const_spec((D_MODEL, D_MODEL)),
            _const_spec((1, D_MODEL)),
            _const_spec((D_MODEL, 2 * LANES)),
            _const_spec((1, LANES)),
        ],
        out_specs=(pl.BlockSpec((MIX_TM, D_MODEL), lambda i: (i, 0)),
                   pl.BlockSpec((2 * MIX_TM, SC_ROW), lambda i: (i, 0)),
                   pl.BlockSpec((8, MIX_TM), lambda i: (0, i))),
        compiler_params=_cparams(("parallel",)),
        name="mix",
    )(x2d, mod3, ya, yb, p3, p3, wa, wb, wo, g_norm2, wr, br)


PLAN_TB = 1024
ROUTER_ROWS = 4 + N_EXPERTS
SLOT_TM = 1024
SC_ROW = 256
SC_WIN = 128
HI_MASK = -65536


def _pack_bf16_pairs(x):
    bits = pltpu.bitcast(x, jnp.int32)
    half = D_MODEL // 2
    return jnp.bitwise_or(jnp.bitwise_and(lax.shift_right_logical(bits[:, :half], 16), 0xFFFF),
                          jnp.bitwise_and(bits[:, half:], HI_MASK))


def _unpack_bf16_pairs(lo_words, hi_words):
    def low(w):
        return pltpu.bitcast(lax.shift_left(w, 16), F32)

    def high(w):
        return pltpu.bitcast(jnp.bitwise_and(w, HI_MASK), F32)

    return jnp.concatenate([low(lo_words), low(hi_words), high(lo_words), high(hi_words)], axis=1).astype(BF16)


def _route_rows(lt):
    lg = [lt[i:i + 1] for i in range(N_GROUPS)]
    mx = jnp.maximum(jnp.maximum(lg[0], lg[1]), jnp.maximum(lg[2], lg[3]))
    p_top = 1.0 / (jnp.exp(lg[0] - mx) + jnp.exp(lg[1] - mx) + jnp.exp(lg[2] - mx) + jnp.exp(lg[3] - mx))
    gidx = jnp.zeros(lg[0].shape, jnp.int32)
    best = lg[0]
    for i in range(1, N_GROUPS):
        upd = lg[i] > best
        gidx = jnp.where(upd, i, gidx)
        best = jnp.where(upd, lg[i], best)

    def expert_row(g, e):
        r = N_GROUPS + g * EXPERTS_PER_GROUP + e
        return lt[r:r + 1]

    le = [jnp.where(gidx == 0, expert_row(0, e),
                    jnp.where(gidx == 1, expert_row(1, e),
                              jnp.where(gidx == 2, expert_row(2, e), expert_row(3, e))))
          for e in range(EXPERTS_PER_GROUP)]
    v1 = jnp.maximum(jnp.maximum(le[0], le[1]), jnp.maximum(le[2], le[3]))
    i1 = jnp.full(v1.shape, EXPERTS_PER_GROUP - 1, jnp.int32)
    for e in range(EXPERTS_PER_GROUP - 2, -1, -1):
        i1 = jnp.where(le[e] == v1, e, i1)
    le2 = [jnp.where(i1 == e, -jnp.inf, le[e]) for e in range(EXPERTS_PER_GROUP)]
    v2 = jnp.maximum(jnp.maximum(le2[0], le2[1]), jnp.maximum(le2[2], le2[3]))
    i2 = jnp.full(v2.shape, EXPERTS_PER_GROUP - 1, jnp.int32)
    for e in range(EXPERTS_PER_GROUP - 2, -1, -1):
        i2 = jnp.where(le2[e] == v2, e, i2)
    e2 = jnp.exp(v2 - v1)
    first = (gidx * EXPERTS_PER_GROUP + i1).astype(F32)
    second = (gidx * EXPERTS_PER_GROUP + i2).astype(F32)
    return first, second, p_top / (1.0 + e2), p_top * e2 / (1.0 + e2)


def _plan_kernel(info_ref, before_ref, sidx_ref, gidx_ref, tmap_ref, rank_scr, count_scr, *, n_blocks):
    step = pl.program_id(0)
    expert_id = lax.broadcasted_iota(jnp.int32, (N_EXPERTS, PLAN_TB), 0).astype(F32)

    @pl.when(step == 0)
    def _():
        count_scr[...] = jnp.zeros_like(count_scr)

    @pl.when(step < n_blocks)
    def _():
        cols = pl.ds(pl.multiple_of(step * PLAN_TB, PLAN_TB), PLAN_TB)
        sel = jnp.where((info_ref[0:1, cols] == expert_id) | (info_ref[1:2, cols] == expert_id), 1.0, 0.0)
        selb = sel.astype(BF16)
        rank_scr[:, cols] = _dot(selb, before_ref[...]) + count_scr[:, 0:1]
        count_scr[...] += _dot(selb, jnp.ones((PLAN_TB, LANES), BF16))

    @pl.when(step == n_blocks)
    def _():
        counts = count_scr[...]
        padded = jnp.floor((counts + (SLOT_TM - 1)) * (1.0 / SLOT_TM)) * SLOT_TM
        starts, ends, run = [], [], jnp.zeros((1, LANES), F32)
        for e in range(N_EXPERTS):
            starts.append(run)
            run = run + padded[e:e + 1]
            ends.append(run)
        n = info_ref.shape[1]
        shift = SLOT_TM.bit_length() - 1
        for k in range(2):
            chosen = info_ref[k:k + 1, :]
            slot = jnp.zeros((1, n), F32)
            for e in range(N_EXPERTS):
                slot = jnp.where(chosen == float(e), starts[e][:, 0:1] + rank_scr[e:e + 1, :], slot)
            slot = slot.astype(jnp.int32)
            low = jnp.left_shift(jnp.right_shift(slot, shift), shift + 1) + jnp.bitwise_and(slot, SLOT_TM - 1)
            for half in range(2):
                rows = low + half * SLOT_TM
                for t in range(n // MIX_TM):
                    sidx_ref[k, t, half:half + 1, :] = rows[:, t * MIX_TM:(t + 1) * MIX_TM]
                for t in range(n // FIN_TM):
                    gidx_ref[t, k, half:half + 1, :] = rows[:, t * FIN_TM:(t + 1) * FIN_TM]
        tile_start = lax.broadcasted_iota(jnp.int32, (1, LANES), 1).astype(F32) * SLOT_TM
        used = jnp.where(tile_start < run, 1.0, 0.0)
        tile_start = jnp.minimum(tile_start, run - SLOT_TM)
        owner = jnp.zeros((1, LANES), F32)
        for e in range(N_EXPERTS - 1):
            owner = owner + jnp.where(ends[e] <= tile_start, 1.0, 0.0)
        tile = tile_start * (1.0 / SLOT_TM)
        tmap_ref[...] = jnp.concatenate([owner, used, tile] + [jnp.zeros((1, LANES), F32)] * 5,
                                        axis=0).astype(jnp.int32)


def _plan(info):
    n = info.shape[1]
    n_blocks = n // PLAN_TB
    t_i = lax.broadcasted_iota(jnp.int32, (PLAN_TB, PLAN_TB), 0)
    t_j = lax.broadcasted_iota(jnp.int32, (PLAN_TB, PLAN_TB), 1)
    before = (t_i < t_j).astype(BF16)
    return pl.pallas_call(
        functools.partial(_plan_kernel, n_blocks=n_blocks),
        out_shape=(jax.ShapeDtypeStruct((2, n // MIX_TM, 2, MIX_TM), jnp.int32),
                   jax.ShapeDtypeStruct((n // FIN_TM, 2, 2, FIN_TM), jnp.int32),
                   jax.ShapeDtypeStruct((8, LANES), jnp.int32)),
        grid=(n_blocks + 1,),
        in_specs=[_const_spec((8, n)), _const_spec((PLAN_TB, PLAN_TB))],
        out_specs=(pl.BlockSpec((2, n // MIX_TM, 2, MIX_TM), lambda i: (0, 0, 0, 0)),
                   pl.BlockSpec((n // FIN_TM, 2, 2, FIN_TM), lambda i: (0, 0, 0, 0)),
                   pl.BlockSpec((8, LANES), lambda i: (0, 0))),
        scratch_shapes=[pltpu.VMEM((N_EXPERTS, n), F32), pltpu.VMEM((N_EXPERTS, LANES), F32)],
        compiler_params=_cparams(("arbitrary",)),
        name="plan",
    )(info, before)


def _sc_mesh():
    return plsc.VectorSubcoreMesh(core_axis_name="core", subcore_axis_name="subcore")


def _sc_scatter_rows(x, idx, n_out):
    n_idx = idx.shape[0]
    src_blocks = x.shape[0] // SC_WIN

    @pl.kernel(out_type=jax.ShapeDtypeStruct((n_out, SC_ROW), x.dtype), mesh=_sc_mesh(), scratch_types=[])
    def scatter_kernel(x_hbm, i_hbm, o_hbm):
        def body(x_vmem, i_vmem):
            pltpu.sync_copy(x_vmem, o_hbm.at[i_vmem.at[0]])

        pltpu.emit_pipeline(
            body,
            grid=(n_idx // SC_WIN,),
            in_specs=[pl.BlockSpec((SC_WIN, SC_ROW), index_map=lambda i: (i % src_blocks, 0)),
                      pl.BlockSpec((1, SC_WIN), index_map=lambda i: (0, i))],
            out_specs=[],
            core_axis_name=("core", "subcore"),
            dimension_semantics=(pltpu.PARALLEL,),
        )(x_hbm, i_hbm)

    return scatter_kernel(x, idx.reshape(1, n_idx))


def _sc_gather_rows(x, idx):
    n_idx = idx.shape[0]

    @pl.kernel(out_type=jax.ShapeDtypeStruct((n_idx, SC_ROW), x.dtype), mesh=_sc_mesh(), scratch_types=[])
    def gather_kernel(x_hbm, i_hbm, o_hbm):
        def body(i_vmem, o_vmem):
            pltpu.sync_copy(x_hbm.at[i_vmem.at[0]], o_vmem)

        pltpu.emit_pipeline(
            body,
            grid=(n_idx // SC_WIN,),
            in_specs=[pl.BlockSpec((1, SC_WIN), index_map=lambda i: (0, i))],
            out_specs=[pl.BlockSpec((SC_WIN, SC_ROW), index_map=lambda i: (i, 0))],
            core_axis_name=("core", "subcore"),
            dimension_semantics=(pltpu.PARALLEL,),
        )(i_hbm, o_hbm)

    return gather_kernel(x, idx.reshape(1, n_idx))


def _experts_kernel(tmap_ref, xs_ref, wg_ref, wu_ref, wd_ref, ys_ref, wg_scr, wu_scr, wd_scr):
    j = pl.program_id(0)

    @pl.when((j == 0) | (tmap_ref[0, j] != tmap_ref[0, jnp.maximum(j - 1, 0)]))
    def _():
        wg_scr[...] = wg_ref[0].astype(BF16)
        wu_scr[...] = wu_ref[0].astype(BF16)
        wd_scr[...] = wd_ref[0].astype(BF16)

    @pl.when(tmap_ref[1, j] > 0)
    def _():
        x = _unpack_bf16_pairs(xs_ref[0:SLOT_TM, :], xs_ref[SLOT_TM:2 * SLOT_TM, :])
        a = _dot(x, wg_scr[...])
        u = _dot(x, wu_scr[...])
        act = (a * jax.nn.sigmoid(a) * u).astype(BF16)
        y = _dot(act, wd_scr[...]).astype(BF16).astype(F32)
        packed = _pack_bf16_pairs(y)
        ys_ref[0:SLOT_TM, :] = packed[:, :SC_ROW]
        ys_ref[SLOT_TM:2 * SLOT_TM, :] = packed[:, SC_ROW:]


def _experts(xs, tmap, wg, wu, wd):
    n_tiles = xs.shape[0] // (2 * SLOT_TM)
    grid_spec = pltpu.PrefetchScalarGridSpec(
        num_scalar_prefetch=1,
        grid=(n_tiles,),
        in_specs=[
            pl.BlockSpec((2 * SLOT_TM, SC_ROW), lambda j, tm: (tm[2, j], 0)),
            pl.BlockSpec((1, D_MODEL, D_EXPERT), lambda j, tm: (tm[0, j], 0, 0)),
            pl.BlockSpec((1, D_MODEL, D_EXPERT), lambda j, tm: (tm[0, j], 0, 0)),
            pl.BlockSpec((1, D_EXPERT, D_MODEL), lambda j, tm: (tm[0, j], 0, 0)),
        ],
        out_specs=pl.BlockSpec((2 * SLOT_TM, SC_ROW), lambda j, tm: (tm[2, j], 0)),
        scratch_shapes=[pltpu.VMEM((D_MODEL, D_EXPERT), BF16), pltpu.VMEM((D_MODEL, D_EXPERT), BF16),
                        pltpu.VMEM((D_EXPERT, D_MODEL), BF16)],
    )
    return pl.pallas_call(
        _experts_kernel,
        out_shape=jax.ShapeDtypeStruct(xs.shape, jnp.int32),
        grid_spec=grid_spec,
        compiler_params=_cparams(("arbitrary",)),
        name="experts",
    )(tmap, xs, wg, wu, wd)


FIN_TM = 1024


def _final_kernel(x1_ref, mod_ref, yg_ref, w_ref, gf_ref, y_ref):
    moe = None
    for k in range(2):
        base = 2 * k * FIN_TM
        yk = _unpack_bf16_pairs(yg_ref[base:base + FIN_TM, :], yg_ref[base + FIN_TM:base + 2 * FIN_TM, :])
        term = w_ref[:, k:k + 1] * yk.astype(F32)
        moe = term if moe is None else moe + term
    x2 = x1_ref[...] + mod_ref[0, 5:6, :] * moe
    y_ref[...] = x2 * _rms_scale(x2) * gf_ref[...]


def _final(x1, mod3, mod_row_fn, yg, w_cols, g_final):
    n = x1.shape[0]
    return pl.pallas_call(
        _final_kernel,
        out_shape=jax.ShapeDtypeStruct((n, D_MODEL), F32),
        grid=(n // FIN_TM,),
        in_specs=[
            pl.BlockSpec((FIN_TM, D_MODEL), lambda i: (i, 0)),
            pl.BlockSpec((1, 6, D_MODEL), lambda i: (mod_row_fn(i * FIN_TM), 0, 0)),
            pl.BlockSpec((4 * FIN_TM, SC_ROW), lambda i: (i, 0)),
            pl.BlockSpec((FIN_TM, 2), lambda i: (i, 0)),
            _const_spec((1, D_MODEL)),
        ],
        out_specs=pl.BlockSpec((FIN_TM, D_MODEL), lambda i: (i, 0)),
        compiler_params=_cparams(("parallel",)),
        name="final",
    )(x1, mod3, yg, w_cols, g_final)


def _moe(x1, h2p, info, mod3, mod_row_fn, wg, wu, wd, g_final):
    n = x1.shape[0]
    sidx, gidx, tmap = _plan(info)
    n_slots = 2 * n + N_EXPERTS * SLOT_TM
    xs = _sc_scatter_rows(h2p, sidx.reshape(-1), 2 * n_slots)
    ys = _experts(xs, tmap, wg, wu, wd)
    yg = _sc_gather_rows(ys, gidx.reshape(-1))
    w_cols = info[2:4].T
    return _final(x1, mod3, mod_row_fn, yg, w_cols, g_final)


def _stream(x, mod3, mod_row_fn, seq_len, rows_2d, state, w):
    bsz, t, _ = x.shape
    n = bsz * t
    x2d = x.reshape(n, D_MODEL)
    p3, bcum, imb, amax = _in_projection(x2d, mod3, mod_row_fn, w["g_norm1"], w["wmain"], w["bmain"],
                                         w["colscale"], w["w_gates"], w["b_gates"])
    m0 = None if state is None else state[2]
    cols, rows, cs = _gate_prep(bcum, imb, amax, m0)
    ml = _mlstm(p3, cols, rows, cs, w["gn3"], None if state is None else state[:2])
    amat, invc = _pool_constants(seq_len, rows_2d)
    n_seq = max(1, 1024 // seq_len)
    yb = _pool(p3, amat, invc, w["wpg"], w["pscale"], seq_len, n_seq, rows_2d)
    ya = ml if state is not None else ml[0]
    x1, h2p, info = _mix(x2d, mod3, mod_row_fn, ya, yb, p3, w["wa"], w["wb"], w["wo"],
                         w["g_norm2"], w["wr"], w["br"])
    y = _moe(x1, h2p, info, mod3, mod_row_fn, w["wg"], w["wu"], w["wd"], w["g_final"])
    return y.reshape(bsz, t, D_MODEL), ml, cs


def kernel(x_prompt, x_sample, state_C, state_n, state_m, c, c_ctx, w_mod, b_mod, g_norm1, w_in, b_in, gn_gain, w_pool_grp, pool_scale, w_proj_a, w_proj_b, w_out, g_norm2, w_router_grp, b_router_grp, w_router_exp, b_router_exp, w_exp_gate, w_exp_up, w_exp_down, g_final):
    bp, tp, _ = x_prompt.shape
    bs, ts, _ = x_sample.shape
    assert w_mod.shape[0] == 1, "single trunk layer"
    assert tp == CHUNK and ts == ML_ROWS and bs % ML_SEQS == 0

    cvec = jnp.zeros((8, D_MODEL), F32).at[0].set(c_ctx).at[1:1 + bs].set(c)
    mod3 = _modulation(cvec, w_mod[0], b_mod[0])

    wi, bi = w_in[0], b_in[0]
    d = D_MODEL
    o_q, o_k, o_v, o_o, o_g, o_p, o_ga, o_gb = 0, d, 2 * d, 3 * d, 4 * d, 4 * d + 32, 4 * d + 32 + 512, 4 * d + 32 + 512 + d
    order = [(o_q, d), (o_k, d), (o_v, d), (o_o, d), (o_ga, d), (o_gb, d), (o_p, 512)]
    wmain = _pack_columns(wi, order)
    bmain = jnp.concatenate([bi[s:s + l] for s, l in order]).reshape(1, D_MAIN)
    colscale = jnp.ones((D_MAIN,), F32).at[SLAB_K * LANES:(SLAB_K + N_HEADS) * LANES].set(HEAD_DIM ** -0.5)
    colscale = colscale.reshape(1, D_MAIN)
    gwt = _take_columns_t(wi, o_g, 32)
    gb = bi[o_g:o_g + 32]
    i_f, f_f, i_b, f_b = (slice(0, 8), slice(8, 16), slice(16, 24), slice(24, 32))
    pad_w = jnp.zeros((LANES - 2 * N_HD, d), F32)
    pad_b = jnp.zeros((LANES - 2 * N_HD,), F32)
    wg = jnp.concatenate([gwt[f_f], gwt[f_b], gwt[i_f], gwt[i_b], pad_w], axis=0).T
    bg = jnp.concatenate([gb[f_f], gb[f_b], gb[i_f], gb[i_b], pad_b]).reshape(1, LANES)
    wg = jnp.concatenate(_split_bf16(wg), axis=1)

    wr = jnp.concatenate([w_router_grp[0], w_router_exp[0].reshape(d, N_EXPERTS),
                          jnp.zeros((d, LANES - ROUTER_ROWS), F32)], axis=1)
    br = jnp.concatenate([b_router_grp[0], b_router_exp[0].reshape(N_EXPERTS),
                          jnp.zeros((LANES - ROUTER_ROWS,), F32)]).reshape(1, LANES)
    wr = jnp.concatenate(_split_bf16(wr), axis=1)

    w = dict(
        g_norm1=g_norm1[0].reshape(1, d), wmain=wmain, bmain=bmain, colscale=colscale,
        w_gates=wg, b_gates=bg,
        gn3=gn_gain[0].reshape(N_HEADS, 1, HEAD_DIM),
        wpg=w_pool_grp[0].astype(BF16), pscale=pool_scale[0].reshape(N_GROUPS, 1, LANES),
        wa=w_proj_a[0].astype(BF16), wb=w_proj_b[0].astype(BF16), wo=w_out[0].astype(BF16),
        g_norm2=g_norm2[0].reshape(1, d), wr=wr, br=br,
        wg=w_exp_gate[0], wu=w_exp_up[0], wd=w_exp_down[0],
        g_final=g_final.reshape(1, d),
    )

    y_prompt, (_, cst, nst), cs_prompt = _stream(x_prompt, mod3, lambda r: 0, tp, None, None, w)
    n_rep = jnp.broadcast_to(state_n[..., None], state_C.shape)
    m0 = jnp.pad(state_m[:, 0].reshape(bs, N_HD), ((0, 0), (0, LANES - N_HD))).reshape(bs, 1, LANES)
    y_sample, _, _ = _stream(x_sample, mod3, lambda r: 1 + r // ts, ts, ts // GRID_W, (state_C, n_rep, m0), w)

    new_c = cst
    new_n = nst.reshape(bp, 1, 2, N_HEADS, HEAD_DIM)
    new_m = cs_prompt[:, 1, :N_HD].reshape(bp, 1, 2, N_HEADS)
    return (y_prompt, y_sample, new_c, new_n, new_m)
```

```python
import functools

import numpy as np
import jax
import jax.numpy as jnp
from jax import lax
from jax.experimental import pallas as pl
from jax.experimental.pallas import tpu as pltpu
from jax.experimental.pallas import tpu_sc as plsc

F32 = jnp.float32
BF16 = jnp.bfloat16

D_MODEL = 1024
N_HEADS = 8
HEAD_DIM = 128
GRID_W = 64
POOL_WINDOWS = (2, 4, 8, 16)
N_GROUPS = 4
EXPERTS_PER_GROUP = 4
N_EXPERTS = 16
D_EXPERT = 512
EPS = 1e-6
NEG = -1e30

LANES = 128
CHUNK = 256
N_SLABS = 52
SLAB_Q, SLAB_K, SLAB_V, SLAB_O, SLAB_GA, SLAB_GB, SLAB_POOL = 0, 8, 16, 24, 32, 40, 48
D_MAIN = N_SLABS * LANES
VMEM_LIMIT = 56 * 1024 * 1024


def _cparams(sem):
    return pltpu.CompilerParams(dimension_semantics=sem, vmem_limit_bytes=VMEM_LIMIT)


def _const_spec(shape):
    nd = len(shape)
    return pl.BlockSpec(shape, lambda *_: (0,) * nd, pipeline_mode=pl.Buffered(1))


def _split_bf16(x):
    hi = x.astype(BF16)
    lo = (x - hi.astype(F32)).astype(BF16)
    return hi, lo


def _dot(a, b):
    return jnp.dot(a, b, preferred_element_type=F32)


def _dot3(a, b_hi, b_lo):
    a_hi, a_lo = _split_bf16(a)
    return _dot(a_hi, b_hi) + _dot(a_hi, b_lo) + _dot(a_lo, b_hi)


def _dot3_packed(a_hi, a_lo, b_hilo):
    first = _dot(a_hi, b_hilo)
    return first[:, :LANES] + first[:, LANES:] + _dot(a_lo, b_hilo[:, :LANES])


def _rms_scale(x):
    return lax.rsqrt(jnp.mean(x * x, axis=-1, keepdims=True) + EPS)


def _log_sigmoid(x):
    return jnp.minimum(x, 0.0) - jnp.log1p(jnp.exp(-jnp.abs(x)))


def _mod_kernel(c_ref, w_ref, b_ref, o_ref):
    c = c_ref[...]
    s = c * jax.nn.sigmoid(c)
    w_hi, w_lo = _split_bf16(w_ref[...])
    o_ref[...] = _dot3(s, w_hi, w_lo) + b_ref[...]


def _modulation(cvec, w_mod, b_mod):
    tn = 2048
    out = pl.pallas_call(
        _mod_kernel,
        out_shape=jax.ShapeDtypeStruct((8, 6 * D_MODEL), F32),
        grid=(6 * D_MODEL // tn,),
        in_specs=[
            pl.BlockSpec((8, D_MODEL), lambda j: (0, 0)),
            pl.BlockSpec((D_MODEL, tn), lambda j: (0, j)),
            pl.BlockSpec((1, tn), lambda j: (0, j)),
        ],
        out_specs=pl.BlockSpec((8, tn), lambda j: (0, j)),
        compiler_params=_cparams(("parallel",)),
        name="modulation",
    )(cvec, w_mod, b_mod.reshape(1, -1))
    return out.reshape(8, 6, D_MODEL)


PACK_COLS = 512


def _pack_kernel(wt_ref, o_ref):
    o_ref[...] = wt_ref[...].T.astype(BF16)


def _pack_columns(w, order):
    k, n = w.shape
    starts = []
    for src, width in order:
        assert width % PACK_COLS == 0
        starts += [src + j for j in range(0, width, PACK_COLS)]
    n_out = len(starts) * PACK_COLS

    unit = 32
    assert all(s % unit == 0 for s in starts)

    def src_row(j):
        row = jnp.int32(starts[-1] // unit)
        for idx in range(len(starts) - 2, -1, -1):
            row = jnp.where(j == idx, starts[idx] // unit, row)
        return row * unit

    return pl.pallas_call(
        _pack_kernel,
        out_shape=jax.ShapeDtypeStruct((k, n_out), BF16),
        grid=(len(starts),),
        in_specs=[pl.BlockSpec((pl.Element(PACK_COLS), pl.Element(k)), lambda j: (src_row(j), 0))],
        out_specs=pl.BlockSpec((k, PACK_COLS), lambda j: (0, j)),
        compiler_params=_cparams(("parallel",)),
        name="pack_columns",
    )(w.T)


def _copy_kernel(w_ref, o_ref):
    o_ref[...] = w_ref[...]


def _take_columns_t(w, start, width):
    k = w.shape[0]
    return pl.pallas_call(
        _copy_kernel,
        out_shape=jax.ShapeDtypeStruct((width, k), w.dtype),
        grid=(1,),
        in_specs=[pl.BlockSpec((pl.Element(width), pl.Element(k)), lambda j: (start, 0))],
        out_specs=pl.BlockSpec((width, k), lambda j: (0, 0)),
        name="take_columns",
    )(w.T)


IN_TM = 512
IN_TN = 512
LOG2E = 1.4426950408889634
N_HD = 2 * N_HEADS


def _fwd_lanes(shape):
    lane = lax.broadcasted_iota(jnp.int32, shape, len(shape) - 1)
    return (lane & N_HEADS) == 0


def _inproj_kernel(x_ref, mod_ref, g_ref, w_ref, b_ref, cs_ref, wg_ref, bg_ref,
                   p3_ref, bcum_ref, imb_ref, amax_ref):
    x = x_ref[...]
    h = x * _rms_scale(x) * g_ref[...]
    h = h * (1.0 + mod_ref[0, 1:2, :]) + mod_ref[0, 0:1, :]
    hb = h.astype(BF16)
    for n in range(D_MAIN // IN_TN):
        sl = slice(n * IN_TN, (n + 1) * IN_TN)
        acc = (_dot(hb, w_ref[:, sl]) + b_ref[:, sl]) * cs_ref[:, sl]
        accb = acc.astype(BF16)
        for j in range(IN_TN // LANES):
            p3_ref[n * (IN_TN // LANES) + j] = accb[:, j * LANES:(j + 1) * LANES]

    h_lo = (h - hb.astype(F32)).astype(BF16)
    gates = _dot3_packed(hb, h_lo, wg_ref[...]) + bg_ref[...]
    ls = _log_sigmoid(gates)
    gi = pltpu.roll(gates, LANES - N_HD, 1)
    row = lax.broadcasted_iota(jnp.int32, (CHUNK, LANES), 0)
    fwd = _fwd_lanes((CHUNK, LANES))
    for c in range(IN_TM // CHUNK):
        rows = slice(c * CHUNK, (c + 1) * CHUNK)
        cf = ls[rows]
        cb = cf
        s = 1
        while s < CHUNK:
            cf = cf + jnp.where(row >= s, pltpu.roll(cf, s, 0), 0.0)
            cb = cb + jnp.where(row < CHUNK - s, pltpu.roll(cb, CHUNK - s, 0), 0.0)
            s *= 2
        b = jnp.where(fwd, cf, cb)
        imb = gi[rows] - b
        mf = imb
        mb = imb
        s = 1
        while s < CHUNK:
            mf = jnp.maximum(mf, jnp.where(row >= s, pltpu.roll(mf, s, 0), -jnp.inf))
            mb = jnp.maximum(mb, jnp.where(row < CHUNK - s, pltpu.roll(mb, CHUNK - s, 0), -jnp.inf))
            s *= 2
        bcum_ref[rows, :] = b
        imb_ref[rows, :] = imb
        amax_ref[rows, :] = b + jnp.where(fwd, mf, mb)


def _in_projection(x2d, mod3, mod_row_fn, g_norm1, wmain, bmain, colscale, wg, bg):
    n = x2d.shape[0]
    return pl.pallas_call(
        _inproj_kernel,
        out_shape=(jax.ShapeDtypeStruct((N_SLABS, n, LANES), BF16),) + (jax.ShapeDtypeStruct((n, LANES), F32),) * 3,
        grid=(n // IN_TM,),
        in_specs=[
            pl.BlockSpec((IN_TM, D_MODEL), lambda i: (i, 0)),
            pl.BlockSpec((1, 6, D_MODEL), lambda i: (mod_row_fn(i * IN_TM), 0, 0)),
            _const_spec((1, D_MODEL)),
            _const_spec((D_MODEL, D_MAIN)),
            _const_spec((1, D_MAIN)),
            _const_spec((1, D_MAIN)),
            _const_spec((D_MODEL, 2 * LANES)),
            _const_spec((1, LANES)),
        ],
        out_specs=(pl.BlockSpec((N_SLABS, IN_TM, LANES), lambda i: (0, i, 0)),)
        + (pl.BlockSpec((IN_TM, LANES), lambda i: (i, 0)),) * 3,
        compiler_params=_cparams(("parallel",)),
        name="in_projection",
    )(x2d, mod3, g_norm1, wmain, bmain, colscale, wg, bg)


GP_ROWS = 1024
GP_NCH = GP_ROWS // CHUNK
Q_C2, Q_EM, Q_WI, Q_WC = 0, 1, 2, 3


def _gate_prep_kernel(*refs, carry):
    if carry:
        b_ref, imb_ref, a_ref, m0_ref, cols_ref, rw_ref, cs_ref = refs
    else:
        b_ref, imb_ref, a_ref, cols_ref, rw_ref, cs_ref = refs
    fwd1 = _fwd_lanes((1, LANES))
    lane = lax.broadcasted_iota(jnp.int32, (CHUNK, LANES), 1)
    neg = jnp.full((1, LANES), NEG, F32)

    def at_scan_end(ref, c):
        return jnp.where(fwd1, ref[(c + 1) * CHUNK - 1:(c + 1) * CHUNK, :], ref[c * CHUNK:c * CHUNK + 1, :])

    b_last = [at_scan_end(b_ref, c) for c in range(GP_NCH)]
    a_last = [at_scan_end(a_ref, c) for c in range(GP_NCH)]
    m_start = m0_ref[0] if carry else neg
    m_prev = [[None] * GP_NCH, [None] * GP_NCH]
    m_new = [[None] * GP_NCH, [None] * GP_NCH]
    for d, order in enumerate((range(GP_NCH), range(GP_NCH - 1, -1, -1))):
        m = m_start
        for c in order:
            m_prev[d][c] = m
            m_new[d][c] = jnp.maximum(b_last[c] + m, a_last[c])
            m = m_new[d][c] if carry else neg

    row_src = []
    for c in range(GP_NCH):
        rows = slice(c * CHUNK, (c + 1) * CHUNK)
        mp = jnp.where(fwd1, m_prev[0][c], m_prev[1][c])
        mn = jnp.where(fwd1, m_new[0][c], m_new[1][c])
        b = b_ref[rows, :]
        imb = imb_ref[rows, :]
        inter = b + mp
        m_t = jnp.maximum(inter, a_ref[rows, :])
        c2 = (b - m_t) * LOG2E
        em = jnp.exp(-m_t)
        wc = jnp.exp(b_last[c] - mn + imb)
        packed = jnp.where(lane < N_HD, c2, pltpu.roll(em, N_HD * Q_EM, 1))
        if carry:
            wi = jnp.exp(inter - m_t)
            packed = jnp.where(lane < N_HD * Q_WI, packed, pltpu.roll(wi, N_HD * Q_WI, 1))
        cols_ref[rows, :] = jnp.where(lane < N_HD * Q_WC, packed, pltpu.roll(wc, N_HD * Q_WC, 1))
        row_src.append(jnp.where(lane < N_HD, imb * (-LOG2E), pltpu.roll(wc, N_HD, 1)))
        cs_ref[c, 0:1, :] = jnp.exp(b_last[c] + mp - mn)
        cs_ref[c, 1:2, :] = mn

    rows_t = jnp.concatenate(row_src, axis=0).T
    for r in range(2 * N_HD):
        rw_ref[r] = rows_t[r:r + 1]


def _gate_prep(bcum, imb, amax, m0=None):
    n = bcum.shape[0]
    carry = m0 is not None
    blk = pl.BlockSpec((GP_ROWS, LANES), lambda i: (i, 0))
    in_specs = [blk, blk, blk]
    args = [bcum, imb, amax]
    if carry:
        in_specs.append(pl.BlockSpec((1, 1, LANES), lambda i: (i, 0, 0)))
        args.append(m0)
    return pl.pallas_call(
        functools.partial(_gate_prep_kernel, carry=carry),
        out_shape=(jax.ShapeDtypeStruct((n, LANES), F32), jax.ShapeDtypeStruct((2 * N_HD, 1, n), F32),
                   jax.ShapeDtypeStruct((n // CHUNK, 2, LANES), F32)),
        grid=(n // GP_ROWS,),
        in_specs=in_specs,
        out_specs=(blk, pl.BlockSpec((2 * N_HD, 1, GP_ROWS), lambda i: (0, 0, i)),
                   pl.BlockSpec((GP_NCH, 2, LANES), lambda i: (i, 0, 0))),
        compiler_params=_cparams(("parallel",)),
        name="gate_prep",
    )(*args)


ML_ROWS = 1024
ML_SEQS = 4
ML_ROWS_FREE = 8192


def _pick_col(blk, lane_iota, idx):
    return jnp.sum(jnp.where(lane_iota == idx, blk, 0.0), axis=1, keepdims=True)


def _mlstm_kernel(*refs, carry, n_chunks, n_seqs):
    if carry:
        (q_ref, k_ref, v_ref, o_ref, cols_ref, r2f_ref, r2b_ref, gn_ref, cs_ref, c0_ref, n0_ref,
         ya_ref, s_scr, h_scr) = refs
    else:
        (q_ref, k_ref, v_ref, o_ref, cols_ref, r2f_ref, r2b_ref, gn_ref, wrf_ref, wrb_ref,
         ya_ref, cst_ref, nst_ref, s_scr, h_scr) = refs
    head = pl.program_id(1)
    lane = lax.broadcasted_iota(jnp.int32, (CHUNK, LANES), 1)
    lane1 = lax.broadcasted_iota(jnp.int32, (1, LANES), 1)
    r_idx = lax.broadcasted_iota(jnp.int32, (CHUNK, CHUNK), 0)
    c_idx = lax.broadcasted_iota(jnp.int32, (CHUNK, CHUNK), 1)
    ones = jnp.ones((CHUNK, HEAD_DIM), BF16)
    tn_dims = (((0,), (0,)), ((), ()))

    for d in range(2):
        mask = (c_idx <= r_idx) if d == 0 else (c_idx >= r_idx)
        r2_ref = r2f_ref if d == 0 else r2b_ref
        base = N_HEADS * d + head
        per_seq = n_chunks // n_seqs
        local = range(per_seq) if d == 0 else range(per_seq - 1, -1, -1)
        for sq, step, c in [(sq, step, sq * per_seq + lc) for sq in range(n_seqs) for step, lc in enumerate(local)]:
            if carry and step == 0:
                cn = jnp.concatenate([c0_ref[sq, 0, d, 0], n0_ref[sq, 0, d, 0]], axis=1)
            rows = slice(c * CHUNK, (c + 1) * CHUNK)
            q = q_ref[0, rows, :]
            k = k_ref[0, rows, :]
            v = v_ref[0, rows, :]
            v1 = jnp.concatenate([v, ones], axis=1)
            if d == 0:
                s = lax.dot_general(q, k, (((1,), (1,)), ((), ())), preferred_element_type=F32)
                s_scr[rows, :] = s
            else:
                s = s_scr[rows, :]
            blk = cols_ref[rows, :]
            arg = _pick_col(blk, lane, base + N_HD * Q_C2) - r2_ref[0, :, rows]
            sw = s * jnp.exp2(jnp.where(mask, arg, NEG))
            nd = _dot(sw.astype(BF16), v1)
            if carry:
                nd = nd + _pick_col(blk, lane, base + N_HD * Q_WI) * _dot(q, cn.astype(BF16))
            hdir = nd[:, :HEAD_DIM] / jnp.maximum(jnp.abs(nd[:, HEAD_DIM:]), _pick_col(blk, lane, base + N_HD * Q_EM))
            if d == 0:
                h_scr[rows, :] = hdir
            else:
                h_scr[rows, :] = h_scr[rows, :] + hdir

            if carry and step == per_seq - 1:
                continue
            w_col = _pick_col(blk, lane, base + N_HD * Q_WC)
            if carry:
                u = lax.dot_general(k, (w_col * v1.astype(F32)).astype(BF16), tn_dims,
                                    preferred_element_type=F32)
                decay = jnp.sum(jnp.where(lane1 == base, cs_ref[c, 0:1, :], 0.0), axis=1, keepdims=True)
                cn = decay * cn + u
            else:
                w_row = (wrf_ref if d == 0 else wrb_ref)[0, :, rows]
                w8 = jnp.broadcast_to(w_row, (8, CHUNK)).astype(BF16)
                cst_ref[c, 0, d, 0] = lax.dot_general(k, (w_col * v.astype(F32)).astype(BF16), tn_dims,
                                                      preferred_element_type=F32)
                nst_ref[c, d, 0] = _dot(w8, k)[0:1, :]

    hh = h_scr[...]
    hn = hh * _rms_scale(hh) * gn_ref[0]
    ya_ref[0] = (hn * jax.nn.sigmoid(o_ref[0].astype(F32))).astype(BF16)


def _mlstm(p3, cols, rows, cs, gn3, state=None):
    n = p3.shape[1]
    carry = state is not None
    n_seqs = ML_SEQS if carry else 1
    rows_step = ML_ROWS * ML_SEQS if carry else ML_ROWS_FREE
    n_chunks = rows_step // CHUNK
    nblk = n // rows_step

    def slab(base):
        return pl.BlockSpec((1, rows_step, LANES), lambda g, h: (base + h, g, 0))

    def row(base):
        return pl.BlockSpec((1, 1, rows_step), lambda g, h: (base + h, 0, g))

    in_specs = [
        slab(SLAB_Q), slab(SLAB_K), slab(SLAB_V), slab(SLAB_O),
        pl.BlockSpec((rows_step, LANES), lambda g, h: (g, 0)),
        row(0), row(N_HEADS),
        pl.BlockSpec((1, 1, LANES), lambda g, h: (h, 0, 0)),
    ]
    args = [p3, p3, p3, p3, cols, rows, rows, gn3]
    ya_shape = jax.ShapeDtypeStruct((N_HEADS, n, LANES), BF16)
    ya_spec = pl.BlockSpec((1, rows_step, LANES), lambda g, h: (h, g, 0))
    state_spec = pl.BlockSpec((n_seqs, 1, 2, 1, HEAD_DIM, HEAD_DIM), lambda g, h: (g, 0, 0, h, 0, 0))
    if carry:
        c0, n0 = state
        in_specs += [pl.BlockSpec((n_chunks, 2, LANES), lambda g, h: (g, 0, 0)), state_spec, state_spec]
        args += [cs, c0, n0]
        out_shape = ya_shape
        out_specs = ya_spec
    else:
        in_specs += [row(N_HD), row(N_HD + N_HEADS)]
        args += [rows, rows]
        nseq = n // CHUNK
        out_shape = (ya_shape,
                     jax.ShapeDtypeStruct((nseq, 1, 2, N_HEADS, HEAD_DIM, HEAD_DIM), F32),
                     jax.ShapeDtypeStruct((nseq, 2, N_HEADS, 1, HEAD_DIM), F32))
        out_specs = (ya_spec,
                     pl.BlockSpec((n_chunks, 1, 2, 1, HEAD_DIM, HEAD_DIM), lambda g, h: (g, 0, 0, h, 0, 0)),
                     pl.BlockSpec((n_chunks, 2, 1, 1, HEAD_DIM), lambda g, h: (g, 0, h, 0, 0)))
    return pl.pallas_call(
        functools.partial(_mlstm_kernel, carry=carry, n_chunks=n_chunks, n_seqs=n_seqs),
        out_shape=out_shape,
        grid=(nblk, N_HEADS),
        in_specs=in_specs,
        out_specs=out_specs,
        scratch_shapes=[pltpu.VMEM((rows_step, CHUNK), F32), pltpu.VMEM((rows_step, HEAD_DIM), F32)],
        compiler_params=_cparams(("parallel", "parallel")),
        name="mlstm_latent" if carry else "mlstm_prompt",
    )(*args)


def _pool_kernel(u_ref, a_ref, ic_ref, wpg_ref, ps_ref, yb_ref, *, seq_len, n_seq, rows_2d):
    piece = seq_len if rows_2d is None else GRID_W
    n_piece = n_seq if rows_2d is None else rows_2d
    for g in range(N_GROUPS):
        u = jnp.concatenate([u_ref[g, i * piece:(i + 1) * piece, :] for i in range(n_piece)], axis=1)
        win = _dot(a_ref[g], u)
        if rows_2d is None:
            inv_cnt = jnp.concatenate([ic_ref[g]] * n_piece, axis=1)
        else:
            run = [jnp.zeros((piece, LANES), F32)]
            for r in range(rows_2d):
                run.append(run[-1] + win[:, r * LANES:(r + 1) * LANES])
            half = POOL_WINDOWS[g] // 2
            win = jnp.concatenate([run[min(r - half + POOL_WINDOWS[g], rows_2d)] - run[max(r - half, 0)]
                                   for r in range(rows_2d)], axis=1)
            inv_cnt = jnp.concatenate([ic_ref[g, r * piece:(r + 1) * piece, :] for r in range(rows_2d)], axis=1)
        p = (win * inv_cnt - u.astype(F32)).astype(BF16)
        p = jnp.concatenate([p[:, i * LANES:(i + 1) * LANES] for i in range(n_piece)], axis=0)
        yb_ref[g] = (_dot(p, wpg_ref[g]) * ps_ref[g]).astype(BF16)


def _pool(p3, amat, invc, wpg, pscale, seq_len, n_seq, rows_2d):
    n = p3.shape[1]
    rows = seq_len * n_seq
    return pl.pallas_call(
        functools.partial(_pool_kernel, seq_len=seq_len, n_seq=n_seq, rows_2d=rows_2d),
        out_shape=jax.ShapeDtypeStruct((N_GROUPS, n, LANES), BF16),
        grid=(n // rows,),
        in_specs=[
            pl.BlockSpec((N_GROUPS, rows, LANES), lambda i: (SLAB_POOL // N_GROUPS, i, 0)),
            _const_spec(amat.shape),
            _const_spec((N_GROUPS, seq_len, LANES)),
            _const_spec((N_GROUPS, LANES, LANES)),
            _const_spec((N_GROUPS, 1, LANES)),
        ],
        out_specs=pl.BlockSpec((N_GROUPS, rows, LANES), lambda i: (0, i, 0)),
        compiler_params=_cparams(("parallel",)),
        name="pool",
    )(p3, amat, invc, wpg, pscale)


def _window_matrix(size, w):
    idx = np.arange(size)
    lo = np.clip(idx - w // 2, 0, size)
    hi = np.clip(idx - w // 2 + w, 0, size)
    s = np.arange(size)[None, :]
    a = ((s >= lo[:, None]) & (s < hi[:, None])).astype(np.float32)
    return a, (hi - lo).astype(np.float32)


def _pool_constants(seq_len, rows_2d):
    mats, invs = [], []
    for w in POOL_WINDOWS:
        if rows_2d is None:
            a, cnt = _window_matrix(seq_len, w)
        else:
            _, cr = _window_matrix(rows_2d, w)
            a, cc = _window_matrix(GRID_W, w)
            cnt = np.kron(cr, cc)
        mats.append(a)
        invs.append(np.repeat((1.0 / cnt)[:, None], LANES, axis=1))
    return (jnp.asarray(np.stack(mats), dtype=BF16), jnp.asarray(np.stack(invs), dtype=F32))


MIX_TM = 512


def _mix_kernel(x_ref, mod_ref, ya_ref, yb_ref, ga_ref, gb_ref, wa_ref, wb_ref, wo_ref,
                g2_ref, wr_ref, br_ref, x1_ref, h2_ref, info_ref):
    ya = jnp.concatenate([ya_ref[i] for i in range(N_HEADS)], axis=1)
    yb = jnp.concatenate([yb_ref[i] for i in range(N_GROUPS)], axis=1)
    ga = jnp.concatenate([ga_ref[i] for i in range(8)], axis=1).astype(F32)
    gb = jnp.concatenate([gb_ref[i] for i in range(8)], axis=1).astype(F32)
    merged = jax.nn.sigmoid(ga) * _dot(ya, wa_ref[...]) + jax.nn.sigmoid(gb) * _dot(yb, wb_ref[...])
    out = _dot(merged.astype(BF16), wo_ref[...])
    x1 = x_ref[...] + mod_ref[0, 2:3, :] * out
    x1_ref[...] = x1

    h2 = x1 * _rms_scale(x1) * g2_ref[...]
    h2 = h2 * (1.0 + mod_ref[0, 4:5, :]) + mod_ref[0, 3:4, :]
    h_hi, h_lo = _split_bf16(h2)
    packed = _pack_bf16_pairs(h_hi.astype(F32))
    h2_ref[0:MIX_TM, :] = packed[:, :SC_ROW]
    h2_ref[MIX_TM:2 * MIX_TM, :] = packed[:, SC_ROW:]
    logits = _dot3_packed(h_hi, h_lo, wr_ref[...]) + br_ref[...]
    first, second, w1, w2 = _route_rows(logits.T)
    zero = jnp.zeros_like(w1)
    info_ref[...] = jnp.concatenate([first, second, w1, w2, zero, zero, zero, zero], axis=0)


def _mix(x2d, mod3, mod_row_fn, ya, yb, p3, wa, wb, wo, g_norm2, wr, br):
    n = x2d.shape[0]
    return pl.pallas_call(
        _mix_kernel,
        out_shape=(jax.ShapeDtypeStruct((n, D_MODEL), F32),
                   jax.ShapeDtypeStruct((2 * n, SC_ROW), jnp.int32), jax.ShapeDtypeStruct((8, n), F32)),
        grid=(n // MIX_TM,),
        in_specs=[
            pl.BlockSpec((MIX_TM, D_MODEL), lambda i: (i, 0)),
            pl.BlockSpec((1, 6, D_MODEL), lambda i: (mod_row_fn(i * MIX_TM), 0, 0)),
            pl.BlockSpec((N_HEADS, MIX_TM, LANES), lambda i: (0, i, 0)),
            pl.BlockSpec((N_GROUPS, MIX_TM, LANES), lambda i: (0, i, 0)),
            pl.BlockSpec((8, MIX_TM, LANES), lambda i: (SLAB_GA // 8, i, 0)),
            pl.BlockSpec((8, MIX_TM, LANES), lambda i: (SLAB_GB // 8, i, 0)),
            _const_spec((D_MODEL, D_MODEL)),
            _const_spec((N_GROUPS * LANES, D_MODEL)),
            _const_spec((D_MODEL, D_MODEL)),
            _const_spec((1, D_MODEL)),
            _const_spec((D_MODEL, 2 * LANES)),
            _const_spec((1, LANES)),
        ],
        out_specs=(pl.BlockSpec((MIX_TM, D_MODEL), lambda i: (i, 0)),
                   pl.BlockSpec((2 * MIX_TM, SC_ROW), lambda i: (i, 0)),
                   pl.BlockSpec((8, MIX_TM), lambda i: (0, i))),
        compiler_params=_cparams(("parallel",)),
        name="mix",
    )(x2d, mod3, ya, yb, p3, p3, wa, wb, wo, g_norm2, wr, br)


PLAN_TB = 1024
ROUTER_ROWS = 4 + N_EXPERTS
SLOT_TM = 1024
SC_ROW = 256
SC_WIN = 128
HI_MASK = -65536


def _pack_bf16_pairs(x):
    bits = pltpu.bitcast(x, jnp.int32)
    half = D_MODEL // 2
    return jnp.bitwise_or(jnp.bitwise_and(lax.shift_right_logical(bits[:, :half], 16), 0xFFFF),
                          jnp.bitwise_and(bits[:, half:], HI_MASK))


def _unpack_bf16_pairs(lo_words, hi_words):
    def low(w):
        return pltpu.bitcast(lax.shift_left(w, 16), F32)

    def high(w):
        return pltpu.bitcast(jnp.bitwise_and(w, HI_MASK), F32)

    return jnp.concatenate([low(lo_words), low(hi_words), high(lo_words), high(hi_words)], axis=1).astype(BF16)


def _route_rows(lt):
    lg = [lt[i:i + 1] for i in range(N_GROUPS)]
    mx = jnp.maximum(jnp.maximum(lg[0], lg[1]), jnp.maximum(lg[2], lg[3]))
    p_top = 1.0 / (jnp.exp(lg[0] - mx) + jnp.exp(lg[1] - mx) + jnp.exp(lg[2] - mx) + jnp.exp(lg[3] - mx))
    gidx = jnp.zeros(lg[0].shape, jnp.int32)
    best = lg[0]
    for i in range(1, N_GROUPS):
        upd = lg[i] > best
        gidx = jnp.where(upd, i, gidx)
        best = jnp.where(upd, lg[i], best)

    def expert_row(g, e):
        r = N_GROUPS + g * EXPERTS_PER_GROUP + e
        return lt[r:r + 1]

    le = [jnp.where(gidx == 0, expert_row(0, e),
                    jnp.where(gidx == 1, expert_row(1, e),
                              jnp.where(gidx == 2, expert_row(2, e), expert_row(3, e))))
          for e in range(EXPERTS_PER_GROUP)]
    v1 = jnp.maximum(jnp.maximum(le[0], le[1]), jnp.maximum(le[2], le[3]))
    i1 = jnp.full(v1.shape, EXPERTS_PER_GROUP - 1, jnp.int32)
    for e in range(EXPERTS_PER_GROUP - 2, -1, -1):
        i1 = jnp.where(le[e] == v1, e, i1)
    le2 = [jnp.where(i1 == e, -jnp.inf, le[e]) for e in range(EXPERTS_PER_GROUP)]
    v2 = jnp.maximum(jnp.maximum(le2[0], le2[1]), jnp.maximum(le2[2], le2[3]))
    i2 = jnp.full(v2.shape, EXPERTS_PER_GROUP - 1, jnp.int32)
    for e in range(EXPERTS_PER_GROUP - 2, -1, -1):
        i2 = jnp.where(le2[e] == v2, e, i2)
    e2 = jnp.exp(v2 - v1)
    first = (gidx * EXPERTS_PER_GROUP + i1).astype(F32)
    second = (gidx * EXPERTS_PER_GROUP + i2).astype(F32)
    return first, second, p_top / (1.0 + e2), p_top * e2 / (1.0 + e2)


def _plan_kernel(info_ref, before_ref, sidx_ref, gidx_ref, tmap_ref, rank_scr, count_scr, *, n_blocks):
    step = pl.program_id(0)
    expert_id = lax.broadcasted_iota(jnp.int32, (N_EXPERTS, PLAN_TB), 0).astype(F32)

    @pl.when(step == 0)
    def _():
        count_scr[...] = jnp.zeros_like(count_scr)

    @pl.when(step < n_blocks)
    def _():
        cols = pl.ds(pl.multiple_of(step * PLAN_TB, PLAN_TB), PLAN_TB)
        sel = jnp.where((info_ref[0:1, cols] == expert_id) | (info_ref[1:2, cols] == expert_id), 1.0, 0.0)
        selb = sel.astype(BF16)
        rank_scr[:, cols] = _dot(selb, before_ref[...]) + count_scr[:, 0:1]
        count_scr[...] += _dot(selb, jnp.ones((PLAN_TB, LANES), BF16))

    @pl.when(step == n_blocks)
    def _():
        counts = count_scr[...]
        padded = jnp.floor((counts + (SLOT_TM - 1)) * (1.0 / SLOT_TM)) * SLOT_TM
        starts, ends, run = [], [], jnp.zeros((1, LANES), F32)
        for e in range(N_EXPERTS):
            starts.append(run)
            run = run + padded[e:e + 1]
            ends.append(run)
        n = info_ref.shape[1]
        shift = SLOT_TM.bit_length() - 1
        for k in range(2):
            chosen = info_ref[k:k + 1, :]
            slot = jnp.zeros((1, n), F32)
            for e in range(N_EXPERTS):
                slot = jnp.where(chosen == float(e), starts[e][:, 0:1] + rank_scr[e:e + 1, :], slot)
            slot = slot.astype(jnp.int32)
            low = jnp.left_shift(jnp.right_shift(slot, shift), shift + 1) + jnp.bitwise_and(slot, SLOT_TM - 1)
            for half in range(2):
                rows = low + half * SLOT_TM
                for t in range(n // MIX_TM):
                    sidx_ref[k, t, half:half + 1, :] = rows[:, t * MIX_TM:(t + 1) * MIX_TM]
                for t in range(n // FIN_TM):
                    gidx_ref[t, k, half:half + 1, :] = rows[:, t * FIN_TM:(t + 1) * FIN_TM]
        tile_start = lax.broadcasted_iota(jnp.int32, (1, LANES), 1).astype(F32) * SLOT_TM
        used = jnp.where(tile_start < run, 1.0, 0.0)
        tile_start = jnp.minimum(tile_start, run - SLOT_TM)
        owner = jnp.zeros((1, LANES), F32)
        for e in range(N_EXPERTS - 1):
            owner = owner + jnp.where(ends[e] <= tile_start, 1.0, 0.0)
        tile = tile_start * (1.0 / SLOT_TM)
        tmap_ref[...] = jnp.concatenate([owner, used, tile] + [jnp.zeros((1, LANES), F32)] * 5,
                                        axis=0).astype(jnp.int32)


def _plan(info):
    n = info.shape[1]
    n_blocks = n // PLAN_TB
    t_i = lax.broadcasted_iota(jnp.int32, (PLAN_TB, PLAN_TB), 0)
    t_j = lax.broadcasted_iota(jnp.int32, (PLAN_TB, PLAN_TB), 1)
    before = (t_i < t_j).astype(BF16)
    return pl.pallas_call(
        functools.partial(_plan_kernel, n_blocks=n_blocks),
        out_shape=(jax.ShapeDtypeStruct((2, n // MIX_TM, 2, MIX_TM), jnp.int32),
                   jax.ShapeDtypeStruct((n // FIN_TM, 2, 2, FIN_TM), jnp.int32),
                   jax.ShapeDtypeStruct((8, LANES), jnp.int32)),
        grid=(n_blocks + 1,),
        in_specs=[_const_spec((8, n)), _const_spec((PLAN_TB, PLAN_TB))],
        out_specs=(pl.BlockSpec((2, n // MIX_TM, 2, MIX_TM), lambda i: (0, 0, 0, 0)),
                   pl.BlockSpec((n // FIN_TM, 2, 2, FIN_TM), lambda i: (0, 0, 0, 0)),
                   pl.BlockSpec((8, LANES), lambda i: (0, 0))),
        scratch_shapes=[pltpu.VMEM((N_EXPERTS, n), F32), pltpu.VMEM((N_EXPERTS, LANES), F32)],
        compiler_params=_cparams(("arbitrary",)),
        name="plan",
    )(info, before)


def _sc_mesh():
    return plsc.VectorSubcoreMesh(core_axis_name="core", subcore_axis_name="subcore")


def _sc_scatter_rows(x, idx, n_out):
    n_idx = idx.shape[0]
    src_blocks = x.shape[0] // SC_WIN

    @pl.kernel(out_type=jax.ShapeDtypeStruct((n_out, SC_ROW), x.dtype), mesh=_sc_mesh(), scratch_types=[])
    def scatter_kernel(x_hbm, i_hbm, o_hbm):
        def body(x_vmem, i_vmem):
            pltpu.sync_copy(x_vmem, o_hbm.at[i_vmem.at[0]])

        pltpu.emit_pipeline(
            body,
            grid=(n_idx // SC_WIN,),
            in_specs=[pl.BlockSpec((SC_WIN, SC_ROW), index_map=lambda i: (i % src_blocks, 0)),
                      pl.BlockSpec((1, SC_WIN), index_map=lambda i: (0, i))],
            out_specs=[],
            core_axis_name=("core", "subcore"),
            dimension_semantics=(pltpu.PARALLEL,),
        )(x_hbm, i_hbm)

    return scatter_kernel(x, idx.reshape(1, n_idx))


def _sc_gather_rows(x, idx):
    n_idx = idx.shape[0]

    @pl.kernel(out_type=jax.ShapeDtypeStruct((n_idx, SC_ROW), x.dtype), mesh=_sc_mesh(), scratch_types=[])
    def gather_kernel(x_hbm, i_hbm, o_hbm):
        def body(i_vmem, o_vmem):
            pltpu.sync_copy(x_hbm.at[i_vmem.at[0]], o_vmem)

        pltpu.emit_pipeline(
            body,
            grid=(n_idx // SC_WIN,),
            in_specs=[pl.BlockSpec((1, SC_WIN), index_map=lambda i: (0, i))],
            out_specs=[pl.BlockSpec((SC_WIN, SC_ROW), index_map=lambda i: (i, 0))],
            core_axis_name=("core", "subcore"),
            dimension_semantics=(pltpu.PARALLEL,),
        )(i_hbm, o_hbm)

    return gather_kernel(x, idx.reshape(1, n_idx))


def _experts_kernel(tmap_ref, xs_ref, wg_ref, wu_ref, wd_ref, ys_ref, wg_scr, wu_scr, wd_scr):
    j = pl.program_id(0)

    @pl.when((j == 0) | (tmap_ref[0, j] != tmap_ref[0, jnp.maximum(j - 1, 0)]))
    def _():
        wg_scr[...] = wg_ref[0].astype(BF16)
        wu_scr[...] = wu_ref[0].astype(BF16)
        wd_scr[...] = wd_ref[0].astype(BF16)

    @pl.when(tmap_ref[1, j] > 0)
    def _():
        x = _unpack_bf16_pairs(xs_ref[0:SLOT_TM, :], xs_ref[SLOT_TM:2 * SLOT_TM, :])
        a = _dot(x, wg_scr[...])
        u = _dot(x, wu_scr[...])
        act = (a * jax.nn.sigmoid(a) * u).astype(BF16)
        y = _dot(act, wd_scr[...]).astype(BF16).astype(F32)
        packed = _pack_bf16_pairs(y)
        ys_ref[0:SLOT_TM, :] = packed[:, :SC_ROW]
        ys_ref[SLOT_TM:2 * SLOT_TM, :] = packed[:, SC_ROW:]


def _experts(xs, tmap, wg, wu, wd):
    n_tiles = xs.shape[0] // (2 * SLOT_TM)
    grid_spec = pltpu.PrefetchScalarGridSpec(
        num_scalar_prefetch=1,
        grid=(n_tiles,),
        in_specs=[
            pl.BlockSpec((2 * SLOT_TM, SC_ROW), lambda j, tm: (tm[2, j], 0)),
            pl.BlockSpec((1, D_MODEL, D_EXPERT), lambda j, tm: (tm[0, j], 0, 0)),
            pl.BlockSpec((1, D_MODEL, D_EXPERT), lambda j, tm: (tm[0, j], 0, 0)),
            pl.BlockSpec((1, D_EXPERT, D_MODEL), lambda j, tm: (tm[0, j], 0, 0)),
        ],
        out_specs=pl.BlockSpec((2 * SLOT_TM, SC_ROW), lambda j, tm: (tm[2, j], 0)),
        scratch_shapes=[pltpu.VMEM((D_MODEL, D_EXPERT), BF16), pltpu.VMEM((D_MODEL, D_EXPERT), BF16),
                        pltpu.VMEM((D_EXPERT, D_MODEL), BF16)],
    )
    return pl.pallas_call(
        _experts_kernel,
        out_shape=jax.ShapeDtypeStruct(xs.shape, jnp.int32),
        grid_spec=grid_spec,
        compiler_params=_cparams(("arbitrary",)),
        name="experts",
    )(tmap, xs, wg, wu, wd)


FIN_TM = 1024


def _final_kernel(x1_ref, mod_ref, yg_ref, w_ref, gf_ref, y_ref):
    moe = None
    for k in range(2):
        base = 2 * k * FIN_TM
        yk = _unpack_bf16_pairs(yg_ref[base:base + FIN_TM, :], yg_ref[base + FIN_TM:base + 2 * FIN_TM, :])
        term = w_ref[:, k:k + 1] * yk.astype(F32)
        moe = term if moe is None else moe + term
    x2 = x1_ref[...] + mod_ref[0, 5:6, :] * moe
    y_ref[...] = x2 * _rms_scale(x2) * gf_ref[...]


def _final(x1, mod3, mod_row_fn, yg, w_cols, g_final):
    n = x1.shape[0]
    return pl.pallas_call(
        _final_kernel,
        out_shape=jax.ShapeDtypeStruct((n, D_MODEL), F32),
        grid=(n // FIN_TM,),
        in_specs=[
            pl.BlockSpec((FIN_TM, D_MODEL), lambda i: (i, 0)),
            pl.BlockSpec((1, 6, D_MODEL), lambda i: (mod_row_fn(i * FIN_TM), 0, 0)),
            pl.BlockSpec((4 * FIN_TM, SC_ROW), lambda i: (i, 0)),
            pl.BlockSpec((FIN_TM, 2), lambda i: (i, 0)),
            _const_spec((1, D_MODEL)),
        ],
        out_specs=pl.BlockSpec((FIN_TM, D_MODEL), lambda i: (i, 0)),
        compiler_params=_cparams(("parallel",)),
        name="final",
    )(x1, mod3, yg, w_cols, g_final)


def _moe(x1, h2p, info, mod3, mod_row_fn, wg, wu, wd, g_final):
    n = x1.shape[0]
    sidx, gidx, tmap = _plan(info)
    n_slots = 2 * n + N_EXPERTS * SLOT_TM
    xs = _sc_scatter_rows(h2p, sidx.reshape(-1), 2 * n_slots)
    ys = _experts(xs, tmap, wg, wu, wd)
    yg = _sc_gather_rows(ys, gidx.reshape(-1))
    w_cols = info[2:4].T
    return _final(x1, mod3, mod_row_fn, yg, w_cols, g_final)


def _stream(x, mod3, mod_row_fn, seq_len, rows_2d, state, w):
    bsz, t, _ = x.shape
    n = bsz * t
    x2d = x.reshape(n, D_MODEL)
    p3, bcum, imb, amax = _in_projection(x2d, mod3, mod_row_fn, w["g_norm1"], w["wmain"], w["bmain"],
                                         w["colscale"], w["w_gates"], w["b_gates"])
    m0 = None if state is None else state[2]
    cols, rows, cs = _gate_prep(bcum, imb, amax, m0)
    ml = _mlstm(p3, cols, rows, cs, w["gn3"], None if state is None else state[:2])
    amat, invc = _pool_constants(seq_len, rows_2d)
    n_seq = max(1, 1024 // seq_len)
    yb = _pool(p3, amat, invc, w["wpg"], w["pscale"], seq_len, n_seq, rows_2d)
    ya = ml if state is not None else ml[0]
    x1, h2p, info = _mix(x2d, mod3, mod_row_fn, ya, yb, p3, w["wa"], w["wb"], w["wo"],
                         w["g_norm2"], w["wr"], w["br"])
    y = _moe(x1, h2p, info, mod3, mod_row_fn, w["wg"], w["wu"], w["wd"], w["g_final"])
    return y.reshape(bsz, t, D_MODEL), ml, cs


def kernel(x_prompt, x_sample, state_C, state_n, state_m, c, c_ctx, w_mod, b_mod, g_norm1, w_in, b_in, gn_gain, w_pool_grp, pool_scale, w_proj_a, w_proj_b, w_out, g_norm2, w_router_grp, b_router_grp, w_router_exp, b_router_exp, w_exp_gate, w_exp_up, w_exp_down, g_final):
    bp, tp, _ = x_prompt.shape
    bs, ts, _ = x_sample.shape
    assert w_mod.shape[0] == 1, "single trunk layer"
    assert tp == CHUNK and ts == ML_ROWS and bs % ML_SEQS == 0

    cvec = jnp.zeros((8, D_MODEL), F32).at[0].set(c_ctx).at[1:1 + bs].set(c)
    mod3 = _modulation(cvec, w_mod[0], b_mod[0])

    wi, bi = w_in[0], b_in[0]
    d = D_MODEL
    o_q, o_k, o_v, o_o, o_g, o_p, o_ga, o_gb = 0, d, 2 * d, 3 * d, 4 * d, 4 * d + 32, 4 * d + 32 + 512, 4 * d + 32 + 512 + d
    order = [(o_q, d), (o_k, d), (o_v, d), (o_o, d), (o_ga, d), (o_gb, d), (o_p, 512)]
    wmain = _pack_columns(wi, order)
    bmain = jnp.concatenate([bi[s:s + l] for s, l in order]).reshape(1, D_MAIN)
    colscale = jnp.ones((D_MAIN,), F32).at[SLAB_K * LANES:(SLAB_K + N_HEADS) * LANES].set(HEAD_DIM ** -0.5)
    colscale = colscale.reshape(1, D_MAIN)
    gwt = _take_columns_t(wi, o_g, 32)
    gb = bi[o_g:o_g + 32]
    i_f, f_f, i_b, f_b = (slice(0, 8), slice(8, 16), slice(16, 24), slice(24, 32))
    pad_w = jnp.zeros((LANES - 2 * N_HD, d), F32)
    pad_b = jnp.zeros((LANES - 2 * N_HD,), F32)
    wg = jnp.concatenate([gwt[f_f], gwt[f_b], gwt[i_f], gwt[i_b], pad_w], axis=0).T
    bg = jnp.concatenate([gb[f_f], gb[f_b], gb[i_f], gb[i_b], pad_b]).reshape(1, LANES)
    wg = jnp.concatenate(_split_bf16(wg), axis=1)

    wr = jnp.concatenate([w_router_grp[0], w_router_exp[0].reshape(d, N_EXPERTS),
                          jnp.zeros((d, LANES - ROUTER_ROWS), F32)], axis=1)
    br = jnp.concatenate([b_router_grp[0], b_router_exp[0].reshape(N_EXPERTS),
                          jnp.zeros((LANES - ROUTER_ROWS,), F32)]).reshape(1, LANES)
    wr = jnp.concatenate(_split_bf16(wr), axis=1)

    w = dict(
        g_norm1=g_norm1[0].reshape(1, d), wmain=wmain, bmain=bmain, colscale=colscale,
        w_gates=wg, b_gates=bg,
        gn3=gn_gain[0].reshape(N_HEADS, 1, HEAD_DIM),
        wpg=w_pool_grp[0].astype(BF16), pscale=pool_scale[0].reshape(N_GROUPS, 1, LANES),
        wa=w_proj_a[0].astype(BF16), wb=w_proj_b[0].astype(BF16), wo=w_out[0].astype(BF16),
        g_norm2=g_norm2[0].reshape(1, d), wr=wr, br=br,
        wg=w_exp_gate[0], wu=w_exp_up[0], wd=w_exp_down[0],
        g_final=g_final.reshape(1, d),
    )

    y_prompt, (_, cst, nst), cs_prompt = _stream(x_prompt, mod3, lambda r: 0, tp, None, None, w)
    n_rep = jnp.broadcast_to(state_n[..., None], state_C.shape)
    m0 = jnp.pad(state_m[:, 0].reshape(bs, N_HD), ((0, 0), (0, LANES - N_HD))).reshape(bs, 1, LANES)
    y_sample, _, _ = _stream(x_sample, mod3, lambda r: 1 + r // ts, ts, ts // GRID_W, (state_C, n_rep, m0), w)

    new_c = cst
    new_n = nst.reshape(bp, 1, 2, N_HEADS, HEAD_DIM)
    new_m = cs_prompt[:, 1, :N_HD].reshape(bp, 1, 2, N_HEADS)
    return (y_prompt, y_sample, new_c, new_n, new_m)
```

```python
import functools

import numpy as np
import jax
import jax.numpy as jnp
from jax import lax
from jax.experimental import pallas as pl
from jax.experimental.pallas import tpu as pltpu
from jax.experimental.pallas import tpu_sc as plsc

F32 = jnp.float32
BF16 = jnp.bfloat16

D_MODEL = 1024
N_HEADS = 8
HEAD_DIM = 128
GRID_W = 64
POOL_WINDOWS = (2, 4, 8, 16)
N_GROUPS = 4
EXPERTS_PER_GROUP = 4
N_EXPERTS = 16
D_EXPERT = 512
EPS = 1e-6
NEG = -1e30

LANES = 128
CHUNK = 256
N_SLABS = 52
SLAB_Q, SLAB_K, SLAB_V, SLAB_O, SLAB_GA, SLAB_GB, SLAB_POOL = 0, 8, 16, 24, 32, 40, 48
D_MAIN = N_SLABS * LANES
VMEM_LIMIT = 56 * 1024 * 1024


def _cparams(sem):
    return pltpu.CompilerParams(dimension_semantics=sem, vmem_limit_bytes=VMEM_LIMIT)


def _const_spec(shape):
    nd = len(shape)
    return pl.BlockSpec(shape, lambda *_: (0,) * nd, pipeline_mode=pl.Buffered(1))


def _split_bf16(x):
    hi = x.astype(BF16)
    lo = (x - hi.astype(F32)).astype(BF16)
    return hi, lo


def _dot(a, b):
    return jnp.dot(a, b, preferred_element_type=F32)


def _dot3(a, b_hi, b_lo):
    a_hi, a_lo = _split_bf16(a)
    return _dot(a_hi, b_hi) + _dot(a_hi, b_lo) + _dot(a_lo, b_hi)


def _dot3_packed(a_hi, a_lo, b_hilo):
    first = _dot(a_hi, b_hilo)
    return first[:, :LANES] + first[:, LANES:] + _dot(a_lo, b_hilo[:, :LANES])


def _rms_scale(x):
    return lax.rsqrt(jnp.mean(x * x, axis=-1, keepdims=True) + EPS)


def _log_sigmoid(x):
    return jnp.minimum(x, 0.0) - jnp.log1p(jnp.exp(-jnp.abs(x)))


def _mod_kernel(c_ref, w_ref, b_ref, o_ref):
    c = c_ref[...]
    s = c * jax.nn.sigmoid(c)
    w_hi, w_lo = _split_bf16(w_ref[...])
    o_ref[...] = _dot3(s, w_hi, w_lo) + b_ref[...]


def _modulation(cvec, w_mod, b_mod):
    tn = 2048
    out = pl.pallas_call(
        _mod_kernel,
        out_shape=jax.ShapeDtypeStruct((8, 6 * D_MODEL), F32),
        grid=(6 * D_MODEL // tn,),
        in_specs=[
            pl.BlockSpec((8, D_MODEL), lambda j: (0, 0)),
            pl.BlockSpec((D_MODEL, tn), lambda j: (0, j)),
            pl.BlockSpec((1, tn), lambda j: (0, j)),
        ],
        out_specs=pl.BlockSpec((8, tn), lambda j: (0, j)),
        compiler_params=_cparams(("parallel",)),
        name="modulation",
    )(cvec, w_mod, b_mod.reshape(1, -1))
    return out.reshape(8, 6, D_MODEL)


PACK_COLS = 512


def _pack_kernel(wt_ref, o_ref):
    o_ref[...] = wt_ref[...].T.astype(BF16)


def _pack_columns(w, order):
    k, n = w.shape
    starts = []
    for src, width in order:
        assert width % PACK_COLS == 0
        starts += [src + j for j in range(0, width, PACK_COLS)]
    n_out = len(starts) * PACK_COLS

    unit = 32
    assert all(s % unit == 0 for s in starts)

    def src_row(j):
        row = jnp.int32(starts[-1] // unit)
        for idx in range(len(starts) - 2, -1, -1):
            row = jnp.where(j == idx, starts[idx] // unit, row)
        return row * unit

    return pl.pallas_call(
        _pack_kernel,
        out_shape=jax.ShapeDtypeStruct((k, n_out), BF16),
        grid=(len(starts),),
        in_specs=[pl.BlockSpec((pl.Element(PACK_COLS), pl.Element(k)), lambda j: (src_row(j), 0))],
        out_specs=pl.BlockSpec((k, PACK_COLS), lambda j: (0, j)),
        compiler_params=_cparams(("parallel",)),
        name="pack_columns",
    )(w.T)


def _copy_kernel(w_ref, o_ref):
    o_ref[...] = w_ref[...]


def _take_columns_t(w, start, width):
    k = w.shape[0]
    return pl.pallas_call(
        _copy_kernel,
        out_shape=jax.ShapeDtypeStruct((width, k), w.dtype),
        grid=(1,),
        in_specs=[pl.BlockSpec((pl.Element(width), pl.Element(k)), lambda j: (start, 0))],
        out_specs=pl.BlockSpec((width, k), lambda j: (0, 0)),
        name="take_columns",
    )(w.T)


IN_TM = 512
IN_TN = 512
LOG2E = 1.4426950408889634
N_HD = 2 * N_HEADS


def _fwd_lanes(shape):
    lane = lax.broadcasted_iota(jnp.int32, shape, len(shape) - 1)
    return (lane & N_HEADS) == 0


def _inproj_kernel(x_ref, mod_ref, g_ref, w_ref, b_ref, cs_ref, wg_ref, bg_ref,
                   p3_ref, bcum_ref, imb_ref, amax_ref):
    x = x_ref[...]
    h = x * _rms_scale(x) * g_ref[...]
    h = h * (1.0 + mod_ref[0, 1:2, :]) + mod_ref[0, 0:1, :]
    hb = h.astype(BF16)
    for n in range(D_MAIN // IN_TN):
        sl = slice(n * IN_TN, (n + 1) * IN_TN)
        acc = (_dot(hb, w_ref[:, sl]) + b_ref[:, sl]) * cs_ref[:, sl]
        accb = acc.astype(BF16)
        for j in range(IN_TN // LANES):
            p3_ref[n * (IN_TN // LANES) + j] = accb[:, j * LANES:(j + 1) * LANES]

    h_lo = (h - hb.astype(F32)).astype(BF16)
    gates = _dot3_packed(hb, h_lo, wg_ref[...]) + bg_ref[...]
    ls = _log_sigmoid(gates)
    gi = pltpu.roll(gates, LANES - N_HD, 1)
    row = lax.broadcasted_iota(jnp.int32, (CHUNK, LANES), 0)
    fwd = _fwd_lanes((CHUNK, LANES))
    for c in range(IN_TM // CHUNK):
        rows = slice(c * CHUNK, (c + 1) * CHUNK)
        cf = ls[rows]
        cb = cf
        s = 1
        while s < CHUNK:
            cf = cf + jnp.where(row >= s, pltpu.roll(cf, s, 0), 0.0)
            cb = cb + jnp.where(row < CHUNK - s, pltpu.roll(cb, CHUNK - s, 0), 0.0)
            s *= 2
        b = jnp.where(fwd, cf, cb)
        imb = gi[rows] - b
        mf = imb
        mb = imb
        s = 1
        while s < CHUNK:
            mf = jnp.maximum(mf, jnp.where(row >= s, pltpu.roll(mf, s, 0), -jnp.inf))
            mb = jnp.maximum(mb, jnp.where(row < CHUNK - s, pltpu.roll(mb, CHUNK - s, 0), -jnp.inf))
            s *= 2
        bcum_ref[rows, :] = b
        imb_ref[rows, :] = imb
        amax_ref[rows, :] = b + jnp.where(fwd, mf, mb)


def _in_projection(x2d, mod3, mod_row_fn, g_norm1, wmain, bmain, colscale, wg, bg):
    n = x2d.shape[0]
    return pl.pallas_call(
        _inproj_kernel,
        out_shape=(jax.ShapeDtypeStruct((N_SLABS, n, LANES), BF16),) + (jax.ShapeDtypeStruct((n, LANES), F32),) * 3,
        grid=(n // IN_TM,),
        in_specs=[
            pl.BlockSpec((IN_TM, D_MODEL), lambda i: (i, 0)),
            pl.BlockSpec((1, 6, D_MODEL), lambda i: (mod_row_fn(i * IN_TM), 0, 0)),
            _const_spec((1, D_MODEL)),
            _const_spec((D_MODEL, D_MAIN)),
            _const_spec((1, D_MAIN)),
            _const_spec((1, D_MAIN)),
            _const_spec((D_MODEL, 2 * LANES)),
            _const_spec((1, LANES)),
        ],
        out_specs=(pl.BlockSpec((N_SLABS, IN_TM, LANES), lambda i: (0, i, 0)),)
        + (pl.BlockSpec((IN_TM, LANES), lambda i: (i, 0)),) * 3,
        compiler_params=_cparams(("parallel",)),
        name="in_projection",
    )(x2d, mod3, g_norm1, wmain, bmain, colscale, wg, bg)


GP_ROWS = 1024
GP_NCH = GP_ROWS // CHUNK
Q_C2, Q_EM, Q_WI, Q_WC = 0, 1, 2, 3


def _gate_prep_kernel(*refs, carry):
    if carry:
        b_ref, imb_ref, a_ref, m0_ref, cols_ref, rw_ref, cs_ref = refs
    else:
        b_ref, imb_ref, a_ref, cols_ref, rw_ref, cs_ref = refs
    fwd1 = _fwd_lanes((1, LANES))
    lane = lax.broadcasted_iota(jnp.int32, (CHUNK, LANES), 1)
    neg = jnp.full((1, LANES), NEG, F32)

    def at_scan_end(ref, c):
        return jnp.where(fwd1, ref[(c + 1) * CHUNK - 1:(c + 1) * CHUNK, :], ref[c * CHUNK:c * CHUNK + 1, :])

    b_last = [at_scan_end(b_ref, c) for c in range(GP_NCH)]
    a_last = [at_scan_end(a_ref, c) for c in range(GP_NCH)]
    m_start = m0_ref[0] if carry else neg
    m_prev = [[None] * GP_NCH, [None] * GP_NCH]
    m_new = [[None] * GP_NCH, [None] * GP_NCH]
    for d, order in enumerate((range(GP_NCH), range(GP_NCH - 1, -1, -1))):
        m = m_start
        for c in order:
            m_prev[d][c] = m
            m_new[d][c] = jnp.maximum(b_last[c] + m, a_last[c])
            m = m_new[d][c] if carry else neg

    row_src = []
    for c in range(GP_NCH):
        rows = slice(c * CHUNK, (c + 1) * CHUNK)
        mp = jnp.where(fwd1, m_prev[0][c], m_prev[1][c])
        mn = jnp.where(fwd1, m_new[0][c], m_new[1][c])
        b = b_ref[rows, :]
        imb = imb_ref[rows, :]
        inter = b + mp
        m_t = jnp.maximum(inter, a_ref[rows, :])
        c2 = (b - m_t) * LOG2E
        em = jnp.exp(-m_t)
        wc = jnp.exp(b_last[c] - mn + imb)
        packed = jnp.where(lane < N_HD, c2, pltpu.roll(em, N_HD * Q_EM, 1))
        if carry:
            wi = jnp.exp(inter - m_t)
            packed = jnp.where(lane < N_HD * Q_WI, packed, pltpu.roll(wi, N_HD * Q_WI, 1))
        cols_ref[rows, :] = jnp.where(lane < N_HD * Q_WC, packed, pltpu.roll(wc, N_HD * Q_WC, 1))
        row_src.append(jnp.where(lane < N_HD, imb * (-LOG2E), pltpu.roll(wc, N_HD, 1)))
        cs_ref[c, 0:1, :] = jnp.exp(b_last[c] + mp - mn)
        cs_ref[c, 1:2, :] = mn

    rows_t = jnp.concatenate(row_src, axis=0).T
    for r in range(2 * N_HD):
        rw_ref[r] = rows_t[r:r + 1]


def _gate_prep(bcum, imb, amax, m0=None):
    n = bcum.shape[0]
    carry = m0 is not None
    blk = pl.BlockSpec((GP_ROWS, LANES), lambda i: (i, 0))
    in_specs = [blk, blk, blk]
    args = [bcum, imb, amax]
    if carry:
        in_specs.append(pl.BlockSpec((1, 1, LANES), lambda i: (i, 0, 0)))
        args.append(m0)
    return pl.pallas_call(
        functools.partial(_gate_prep_kernel, carry=carry),
        out_shape=(jax.ShapeDtypeStruct((n, LANES), F32), jax.ShapeDtypeStruct((2 * N_HD, 1, n), F32),
                   jax.ShapeDtypeStruct((n // CHUNK, 2, LANES), F32)),
        grid=(n // GP_ROWS,),
        in_specs=in_specs,
        out_specs=(blk, pl.BlockSpec((2 * N_HD, 1, GP_ROWS), lambda i: (0, 0, i)),
                   pl.BlockSpec((GP_NCH, 2, LANES), lambda i: (i, 0, 0))),
        compiler_params=_cparams(("parallel",)),
        name="gate_prep",
    )(*args)


ML_ROWS = 1024
ML_SEQS = 4
ML_ROWS_FREE = 4096


def _pick_col(blk, lane_iota, idx):
    return jnp.sum(jnp.where(lane_iota == idx, blk, 0.0), axis=1, keepdims=True)


def _mlstm_kernel(*refs, carry, n_chunks, n_seqs):
    if carry:
        (q_ref, k_ref, v_ref, o_ref, cols_ref, r2f_ref, r2b_ref, gn_ref, cs_ref, c0_ref, n0_ref,
         ya_ref, s_scr, h_scr) = refs
    else:
        (q_ref, k_ref, v_ref, o_ref, cols_ref, r2f_ref, r2b_ref, gn_ref, wrf_ref, wrb_ref,
         ya_ref, cst_ref, nst_ref, s_scr, h_scr) = refs
    head = pl.program_id(1)
    lane = lax.broadcasted_iota(jnp.int32, (CHUNK, LANES), 1)
    lane1 = lax.broadcasted_iota(jnp.int32, (1, LANES), 1)
    r_idx = lax.broadcasted_iota(jnp.int32, (CHUNK, CHUNK), 0)
    c_idx = lax.broadcasted_iota(jnp.int32, (CHUNK, CHUNK), 1)
    ones = jnp.ones((CHUNK, HEAD_DIM), BF16)
    tn_dims = (((0,), (0,)), ((), ()))

    for d in range(2):
        mask = (c_idx <= r_idx) if d == 0 else (c_idx >= r_idx)
        r2_ref = r2f_ref if d == 0 else r2b_ref
        base = N_HEADS * d + head
        per_seq = n_chunks // n_seqs
        local = range(per_seq) if d == 0 else range(per_seq - 1, -1, -1)
        for sq, step, c in [(sq, step, sq * per_seq + lc) for sq in range(n_seqs) for step, lc in enumerate(local)]:
            if carry and step == 0:
                n_rep = jnp.broadcast_to(n0_ref[sq, d, 0], (HEAD_DIM, HEAD_DIM)).T
                cn = jnp.concatenate([c0_ref[sq, 0, d, 0], n_rep], axis=1)
            rows = slice(c * CHUNK, (c + 1) * CHUNK)
            q = q_ref[0, rows, :]
            k = k_ref[0, rows, :]
            v = v_ref[0, rows, :]
            v1 = jnp.concatenate([v, ones], axis=1)
            if d == 0:
                s = lax.dot_general(q, k, (((1,), (1,)), ((), ())), preferred_element_type=F32)
                s_scr[rows, :] = s
            else:
                s = s_scr[rows, :]
            blk = cols_ref[rows, :]
            arg = _pick_col(blk, lane, base + N_HD * Q_C2) - r2_ref[0, :, rows]
            sw = s * jnp.exp2(jnp.where(mask, arg, NEG))
            nd = _dot(sw.astype(BF16), v1)
            if carry:
                nd = nd + _pick_col(blk, lane, base + N_HD * Q_WI) * _dot(q, cn.astype(BF16))
            hdir = nd[:, :HEAD_DIM] / jnp.maximum(jnp.abs(nd[:, HEAD_DIM:]), _pick_col(blk, lane, base + N_HD * Q_EM))
            if d == 0:
                h_scr[rows, :] = hdir
            else:
                h_scr[rows, :] = h_scr[rows, :] + hdir

            if carry and step == per_seq - 1:
                continue
            w_col = _pick_col(blk, lane, base + N_HD * Q_WC)
            if carry:
                u = lax.dot_general(k, (w_col * v1.astype(F32)).astype(BF16), tn_dims,
                                    preferred_element_type=F32)
                decay = jnp.sum(jnp.where(lane1 == base, cs_ref[c, 0:1, :], 0.0), axis=1, keepdims=True)
                cn = decay * cn + u
            else:
                w_row = (wrf_ref if d == 0 else wrb_ref)[0, :, rows]
                w8 = jnp.broadcast_to(w_row, (8, CHUNK)).astype(BF16)
                cst_ref[c, 0, d, 0] = lax.dot_general(k, (w_col * v.astype(F32)).astype(BF16), tn_dims,
                                                      preferred_element_type=F32)
                nst_ref[c, d, 0] = _dot(w8, k)[0:1, :]

    hh = h_scr[...]
    hn = hh * _rms_scale(hh) * gn_ref[0]
    ya_ref[0] = (hn * jax.nn.sigmoid(o_ref[0].astype(F32))).astype(BF16)


def _mlstm(p3, cols, rows, cs, gn3, state=None):
    n = p3.shape[1]
    carry = state is not None
    n_seqs = ML_SEQS if carry else 1
    rows_step = ML_ROWS * ML_SEQS if carry else ML_ROWS_FREE
    n_chunks = rows_step // CHUNK
    nblk = n // rows_step

    def slab(base):
        return pl.BlockSpec((1, rows_step, LANES), lambda g, h: (base + h, g, 0))

    def row(base):
        return pl.BlockSpec((1, 1, rows_step), lambda g, h: (base + h, 0, g))

    in_specs = [
        slab(SLAB_Q), slab(SLAB_K), slab(SLAB_V), slab(SLAB_O),
        pl.BlockSpec((rows_step, LANES), lambda g, h: (g, 0)),
        row(0), row(N_HEADS),
        pl.BlockSpec((1, 1, LANES), lambda g, h: (h, 0, 0)),
    ]
    args = [p3, p3, p3, p3, cols, rows, rows, gn3]
    ya_shape = jax.ShapeDtypeStruct((N_HEADS, n, LANES), BF16)
    ya_spec = pl.BlockSpec((1, rows_step, LANES), lambda g, h: (h, g, 0))
    state_spec = pl.BlockSpec((n_seqs, 1, 2, 1, HEAD_DIM, HEAD_DIM), lambda g, h: (g, 0, 0, h, 0, 0))
    if carry:
        c0, n0 = state
        in_specs += [pl.BlockSpec((n_chunks, 2, LANES), lambda g, h: (g, 0, 0)), state_spec,
                     pl.BlockSpec((n_seqs, 2, 1, 1, HEAD_DIM), lambda g, h: (g, 0, h, 0, 0))]
        args += [cs, c0, n0]
        out_shape = ya_shape
        out_specs = ya_spec
    else:
        in_specs += [row(N_HD), row(N_HD + N_HEADS)]
        args += [rows, rows]
        nseq = n // CHUNK
        out_shape = (ya_shape,
                     jax.ShapeDtypeStruct((nseq, 1, 2, N_HEADS, HEAD_DIM, HEAD_DIM), F32),
                     jax.ShapeDtypeStruct((nseq, 2, N_HEADS, 1, HEAD_DIM), F32))
        out_specs = (ya_spec,
                     pl.BlockSpec((n_chunks, 1, 2, 1, HEAD_DIM, HEAD_DIM), lambda g, h: (g, 0, 0, h, 0, 0)),
                     pl.BlockSpec((n_chunks, 2, 1, 1, HEAD_DIM), lambda g, h: (g, 0, h, 0, 0)))
    return pl.pallas_call(
        functools.partial(_mlstm_kernel, carry=carry, n_chunks=n_chunks, n_seqs=n_seqs),
        out_shape=out_shape,
        grid=(nblk, N_HEADS),
        in_specs=in_specs,
        out_specs=out_specs,
        scratch_shapes=[pltpu.VMEM((rows_step, CHUNK), F32), pltpu.VMEM((rows_step, HEAD_DIM), F32)],
        compiler_params=_cparams(("parallel", "parallel")),
        name="mlstm_latent" if carry else "mlstm_prompt",
    )(*args)


def _pool_kernel(u_ref, a_ref, ic_ref, wpg_ref, ps_ref, yb_ref, *, seq_len, n_seq, rows_2d):
    piece = seq_len if rows_2d is None else GRID_W
    n_piece = n_seq if rows_2d is None else rows_2d
    for g in range(N_GROUPS):
        u = jnp.concatenate([u_ref[g, i * piece:(i + 1) * piece, :] for i in range(n_piece)], axis=1)
        win = _dot(a_ref[g], u)
        if rows_2d is None:
            inv_cnt = jnp.concatenate([ic_ref[g]] * n_piece, axis=1)
        else:
            run = [jnp.zeros((piece, LANES), F32)]
            for r in range(rows_2d):
                run.append(run[-1] + win[:, r * LANES:(r + 1) * LANES])
            half = POOL_WINDOWS[g] // 2
            win = jnp.concatenate([run[min(r - half + POOL_WINDOWS[g], rows_2d)] - run[max(r - half, 0)]
                                   for r in range(rows_2d)], axis=1)
            inv_cnt = jnp.concatenate([ic_ref[g, r * piece:(r + 1) * piece, :] for r in range(rows_2d)], axis=1)
        p = (win * inv_cnt - u.astype(F32)).astype(BF16)
        p = jnp.concatenate([p[:, i * LANES:(i + 1) * LANES] for i in range(n_piece)], axis=0)
        yb_ref[g] = (_dot(p, wpg_ref[g]) * ps_ref[g]).astype(BF16)


def _pool(p3, amat, invc, wpg, pscale, seq_len, n_seq, rows_2d):
    n = p3.shape[1]
    rows = seq_len * n_seq
    return pl.pallas_call(
        functools.partial(_pool_kernel, seq_len=seq_len, n_seq=n_seq, rows_2d=rows_2d),
        out_shape=jax.ShapeDtypeStruct((N_GROUPS, n, LANES), BF16),
        grid=(n // rows,),
        in_specs=[
            pl.BlockSpec((N_GROUPS, rows, LANES), lambda i: (SLAB_POOL // N_GROUPS, i, 0)),
            _const_spec(amat.shape),
            _const_spec((N_GROUPS, seq_len, LANES)),
            _const_spec((N_GROUPS, LANES, LANES)),
            _const_spec((N_GROUPS, 1, LANES)),
        ],
        out_specs=pl.BlockSpec((N_GROUPS, rows, LANES), lambda i: (0, i, 0)),
        compiler_params=_cparams(("parallel",)),
        name="pool",
    )(p3, amat, invc, wpg, pscale)


def _window_matrix(size, w):
    idx = np.arange(size)
    lo = np.clip(idx - w // 2, 0, size)
    hi = np.clip(idx - w // 2 + w, 0, size)
    s = np.arange(size)[None, :]
    a = ((s >= lo[:, None]) & (s < hi[:, None])).astype(np.float32)
    return a, (hi - lo).astype(np.float32)


def _pool_constants(seq_len, rows_2d):
    mats, invs = [], []
    for w in POOL_WINDOWS:
        if rows_2d is None:
            a, cnt = _window_matrix(seq_len, w)
        else:
            _, cr = _window_matrix(rows_2d, w)
            a, cc = _window_matrix(GRID_W, w)
            cnt = np.kron(cr, cc)
        mats.append(a)
        invs.append(np.repeat((1.0 / cnt)[:, None], LANES, axis=1))
    return (jnp.asarray(np.stack(mats), dtype=BF16), jnp.asarray(np.stack(invs), dtype=F32))


MIX_TM = 512


def _mix_kernel(x_ref, mod_ref, ya_ref, yb_ref, ga_ref, gb_ref, wa_ref, wb_ref, wo_ref,
                g2_ref, wr_ref, br_ref, x1_ref, h2_ref, info_ref):
    ya = jnp.concatenate([ya_ref[i] for i in range(N_HEADS)], axis=1)
    yb = jnp.concatenate([yb_ref[i] for i in range(N_GROUPS)], axis=1)
    ga = jnp.concatenate([ga_ref[i] for i in range(8)], axis=1).astype(F32)
    gb = jnp.concatenate([gb_ref[i] for i in range(8)], axis=1).astype(F32)
    merged = jax.nn.sigmoid(ga) * _dot(ya, wa_ref[...]) + jax.nn.sigmoid(gb) * _dot(yb, wb_ref[...])
    out = _dot(merged.astype(BF16), wo_ref[...])
    x1 = x_ref[...] + mod_ref[0, 2:3, :] * out
    x1_ref[...] = x1

    h2 = x1 * _rms_scale(x1) * g2_ref[...]
    h2 = h2 * (1.0 + mod_ref[0, 4:5, :]) + mod_ref[0, 3:4, :]
    h_hi, h_lo = _split_bf16(h2)
    packed = _pack_bf16_pairs(h_hi.astype(F32))
    h2_ref[0:MIX_TM, :] = packed[:, :SC_ROW]
    h2_ref[MIX_TM:2 * MIX_TM, :] = packed[:, SC_ROW:]
    logits = _dot3_packed(h_hi, h_lo, wr_ref[...]) + br_ref[...]
    first, second, w1, w2 = _route_rows(logits.T)
    zero = jnp.zeros_like(w1)
    info_ref[...] = jnp.concatenate([first, second, w1, w2, zero, zero, zero, zero], axis=0)


def _mix(x2d, mod3, mod_row_fn, ya, yb, p3, wa, wb, wo, g_norm2, wr, br):
    n = x2d.shape[0]
    return pl.pallas_call(
        _mix_kernel,
        out_shape=(jax.ShapeDtypeStruct((n, D_MODEL), F32),
                   jax.ShapeDtypeStruct((2 * n, SC_ROW), jnp.int32), jax.ShapeDtypeStruct((8, n), F32)),
        grid=(n // MIX_TM,),
        in_specs=[
            pl.BlockSpec((MIX_TM, D_MODEL), lambda i: (i, 0)),
            pl.BlockSpec((1, 6, D_MODEL), lambda i: (mod_row_fn(i * MIX_TM), 0, 0)),
            pl.BlockSpec((N_HEADS, MIX_TM, LANES), lambda i: (0, i, 0)),
            pl.BlockSpec((N_GROUPS, MIX_TM, LANES), lambda i: (0, i, 0)),
            pl.BlockSpec((8, MIX_TM, LANES), lambda i: (SLAB_GA // 8, i, 0)),
            pl.BlockSpec((8, MIX_TM, LANES), lambda i: (SLAB_GB // 8, i, 0)),
            _const_spec((D_MODEL, D_MODEL)),
            _const_spec((N_GROUPS * LANES, D_MODEL)),
            _const_spec((D_MODEL, D_MODEL)),
            _const_spec((1, D_MODEL)),
            _const_spec((D_MODEL, 2 * LANES)),
            _const_spec((1, LANES)),
        ],
        out_specs=(pl.BlockSpec((MIX_TM, D_MODEL), lambda i: (i, 0)),
                   pl.BlockSpec((2 * MIX_TM, SC_ROW), lambda i: (i, 0)),
                   pl.BlockSpec((8, MIX_TM), lambda i: (0, i))),
        compiler_params=_cparams(("parallel",)),
        name="mix",
    )(x2d, mod3, ya, yb, p3, p3, wa, wb, wo, g_norm2, wr, br)


PLAN_TB = 1024
ROUTER_ROWS = 4 + N_EXPERTS
SLOT_TM = 1024
SC_ROW = 256
SC_WIN = 128
HI_MASK = -65536


def _pack_bf16_pairs(x):
    bits = pltpu.bitcast(x, jnp.int32)
    half = D_MODEL // 2
    return jnp.bitwise_or(jnp.bitwise_and(lax.shift_right_logical(bits[:, :half], 16), 0xFFFF),
                          jnp.bitwise_and(bits[:, half:], HI_MASK))


def _unpack_bf16_pairs(lo_words, hi_words):
    def low(w):
        return pltpu.bitcast(lax.shift_left(w, 16), F32)

    def high(w):
        return pltpu.bitcast(jnp.bitwise_and(w, HI_MASK), F32)

    return jnp.concatenate([low(lo_words), low(hi_words), high(lo_words), high(hi_words)], axis=1).astype(BF16)


def _route_rows(lt):
    lg = [lt[i:i + 1] for i in range(N_GROUPS)]
    mx = jnp.maximum(jnp.maximum(lg[0], lg[1]), jnp.maximum(lg[2], lg[3]))
    p_top = 1.0 / (jnp.exp(lg[0] - mx) + jnp.exp(lg[1] - mx) + jnp.exp(lg[2] - mx) + jnp.exp(lg[3] - mx))
    gidx = jnp.zeros(lg[0].shape, jnp.int32)
    best = lg[0]
    for i in range(1, N_GROUPS):
        upd = lg[i] > best
        gidx = jnp.where(upd, i, gidx)
        best = jnp.where(upd, lg[i], best)

    def expert_row(g, e):
        r = N_GROUPS + g * EXPERTS_PER_GROUP + e
        return lt[r:r + 1]

    le = [jnp.where(gidx == 0, expert_row(0, e),
                    jnp.where(gidx == 1, expert_row(1, e),
                              jnp.where(gidx == 2, expert_row(2, e), expert_row(3, e))))
          for e in range(EXPERTS_PER_GROUP)]
    v1 = jnp.maximum(jnp.maximum(le[0], le[1]), jnp.maximum(le[2], le[3]))
    i1 = jnp.full(v1.shape, EXPERTS_PER_GROUP - 1, jnp.int32)
    for e in range(EXPERTS_PER_GROUP - 2, -1, -1):
        i1 = jnp.where(le[e] == v1, e, i1)
    le2 = [jnp.where(i1 == e, -jnp.inf, le[e]) for e in range(EXPERTS_PER_GROUP)]
    v2 = jnp.maximum(jnp.maximum(le2[0], le2[1]), jnp.maximum(le2[2], le2[3]))
    i2 = jnp.full(v2.shape, EXPERTS_PER_GROUP - 1, jnp.int32)
    for e in range(EXPERTS_PER_GROUP - 2, -1, -1):
        i2 = jnp.where(le2[e] == v2, e, i2)
    e2 = jnp.exp(v2 - v1)
    first = (gidx * EXPERTS_PER_GROUP + i1).astype(F32)
    second = (gidx * EXPERTS_PER_GROUP + i2).astype(F32)
    return first, second, p_top / (1.0 + e2), p_top * e2 / (1.0 + e2)


def _plan_kernel(info_ref, before_ref, sidx_ref, gidx_ref, tmap_ref, rank_scr, count_scr, *, n_blocks):
    step = pl.program_id(0)
    expert_id = lax.broadcasted_iota(jnp.int32, (N_EXPERTS, PLAN_TB), 0).astype(F32)

    @pl.when(step == 0)
    def _():
        count_scr[...] = jnp.zeros_like(count_scr)

    @pl.when(step < n_blocks)
    def _():
        cols = pl.ds(pl.multiple_of(step * PLAN_TB, PLAN_TB), PLAN_TB)
        sel = jnp.where((info_ref[0:1, cols] == expert_id) | (info_ref[1:2, cols] == expert_id), 1.0, 0.0)
        selb = sel.astype(BF16)
        rank_scr[:, cols] = _dot(selb, before_ref[...]) + count_scr[:, 0:1]
        count_scr[...] += _dot(selb, jnp.ones((PLAN_TB, LANES), BF16))

    @pl.when(step == n_blocks)
    def _():
        counts = count_scr[...]
        padded = jnp.floor((counts + (SLOT_TM - 1)) * (1.0 / SLOT_TM)) * SLOT_TM
        starts, ends, run = [], [], jnp.zeros((1, LANES), F32)
        for e in range(N_EXPERTS):
            starts.append(run)
            run = run + padded[e:e + 1]
            ends.append(run)
        n = info_ref.shape[1]
        shift = SLOT_TM.bit_length() - 1
        for k in range(2):
            chosen = info_ref[k:k + 1, :]
            slot = jnp.zeros((1, n), F32)
            for e in range(N_EXPERTS):
                slot = jnp.where(chosen == float(e), starts[e][:, 0:1] + rank_scr[e:e + 1, :], slot)
            slot = slot.astype(jnp.int32)
            low = jnp.left_shift(jnp.right_shift(slot, shift), shift + 1) + jnp.bitwise_and(slot, SLOT_TM - 1)
            for half in range(2):
                rows = low + half * SLOT_TM
                for t in range(n // MIX_TM):
                    sidx_ref[k, t, half:half + 1, :] = rows[:, t * MIX_TM:(t + 1) * MIX_TM]
                for t in range(n // FIN_TM):
                    gidx_ref[t, k, half:half + 1, :] = rows[:, t * FIN_TM:(t + 1) * FIN_TM]
        tile_start = lax.broadcasted_iota(jnp.int32, (1, LANES), 1).astype(F32) * SLOT_TM
        used = jnp.where(tile_start < run, 1.0, 0.0)
        tile_start = jnp.minimum(tile_start, run - SLOT_TM)
        owner = jnp.zeros((1, LANES), F32)
        for e in range(N_EXPERTS - 1):
            owner = owner + jnp.where(ends[e] <= tile_start, 1.0, 0.0)
        tile = tile_start * (1.0 / SLOT_TM)
        tmap_ref[...] = jnp.concatenate([owner, used, tile] + [jnp.zeros((1, LANES), F32)] * 5,
                                        axis=0).astype(jnp.int32)


def _plan(info):
    n = info.shape[1]
    n_blocks = n // PLAN_TB
    t_i = lax.broadcasted_iota(jnp.int32, (PLAN_TB, PLAN_TB), 0)
    t_j = lax.broadcasted_iota(jnp.int32, (PLAN_TB, PLAN_TB), 1)
    before = (t_i < t_j).astype(BF16)
    return pl.pallas_call(
        functools.partial(_plan_kernel, n_blocks=n_blocks),
        out_shape=(jax.ShapeDtypeStruct((2, n // MIX_TM, 2, MIX_TM), jnp.int32),
                   jax.ShapeDtypeStruct((n // FIN_TM, 2, 2, FIN_TM), jnp.int32),
                   jax.ShapeDtypeStruct((8, LANES), jnp.int32)),
        grid=(n_blocks + 1,),
        in_specs=[_const_spec((8, n)), _const_spec((PLAN_TB, PLAN_TB))],
        out_specs=(pl.BlockSpec((2, n // MIX_TM, 2, MIX_TM), lambda i: (0, 0, 0, 0)),
                   pl.BlockSpec((n // FIN_TM, 2, 2, FIN_TM), lambda i: (0, 0, 0, 0)),
                   pl.BlockSpec((8, LANES), lambda i: (0, 0))),
        scratch_shapes=[pltpu.VMEM((N_EXPERTS, n), F32), pltpu.VMEM((N_EXPERTS, LANES), F32)],
        compiler_params=_cparams(("arbitrary",)),
        name="plan",
    )(info, before)


def _sc_mesh():
    return plsc.VectorSubcoreMesh(core_axis_name="core", subcore_axis_name="subcore")


def _sc_scatter_rows(x, idx, n_out):
    n_idx = idx.shape[0]
    src_blocks = x.shape[0] // SC_WIN

    @pl.kernel(out_type=jax.ShapeDtypeStruct((n_out, SC_ROW), x.dtype), mesh=_sc_mesh(), scratch_types=[])
    def scatter_kernel(x_hbm, i_hbm, o_hbm):
        def body(x_vmem, i_vmem):
            pltpu.sync_copy(x_vmem, o_hbm.at[i_vmem.at[0]])

        pltpu.emit_pipeline(
            body,
            grid=(n_idx // SC_WIN,),
            in_specs=[pl.BlockSpec((SC_WIN, SC_ROW), index_map=lambda i: (i % src_blocks, 0)),
                      pl.BlockSpec((1, SC_WIN), index_map=lambda i: (0, i))],
            out_specs=[],
            core_axis_name=("core", "subcore"),
            dimension_semantics=(pltpu.PARALLEL,),
        )(x_hbm, i_hbm)

    return scatter_kernel(x, idx.reshape(1, n_idx))


def _sc_gather_rows(x, idx):
    n_idx = idx.shape[0]

    @pl.kernel(out_type=jax.ShapeDtypeStruct((n_idx, SC_ROW), x.dtype), mesh=_sc_mesh(), scratch_types=[])
    def gather_kernel(x_hbm, i_hbm, o_hbm):
        def body(i_vmem, o_vmem):
            pltpu.sync_copy(x_hbm.at[i_vmem.at[0]], o_vmem)

        pltpu.emit_pipeline(
            body,
            grid=(n_idx // SC_WIN,),
            in_specs=[pl.BlockSpec((1, SC_WIN), index_map=lambda i: (0, i))],
            out_specs=[pl.BlockSpec((SC_WIN, SC_ROW), index_map=lambda i: (i, 0))],
            core_axis_name=("core", "subcore"),
            dimension_semantics=(pltpu.PARALLEL,),
        )(i_hbm, o_hbm)

    return gather_kernel(x, idx.reshape(1, n_idx))


def _experts_kernel(tmap_ref, xs_ref, wg_ref, wu_ref, wd_ref, ys_ref, wg_scr, wu_scr, wd_scr):
    j = pl.program_id(0)

    @pl.when((j == 0) | (tmap_ref[0, j] != tmap_ref[0, jnp.maximum(j - 1, 0)]))
    def _():
        wg_scr[...] = wg_ref[0].astype(BF16)
        wu_scr[...] = wu_ref[0].astype(BF16)
        wd_scr[...] = wd_ref[0].astype(BF16)

    @pl.when(tmap_ref[1, j] > 0)
    def _():
        x = _unpack_bf16_pairs(xs_ref[0:SLOT_TM, :], xs_ref[SLOT_TM:2 * SLOT_TM, :])
        a = _dot(x, wg_scr[...])
        u = _dot(x, wu_scr[...])
        act = (a * jax.nn.sigmoid(a) * u).astype(BF16)
        y = _dot(act, wd_scr[...]).astype(BF16).astype(F32)
        packed = _pack_bf16_pairs(y)
        ys_ref[0:SLOT_TM, :] = packed[:, :SC_ROW]
        ys_ref[SLOT_TM:2 * SLOT_TM, :] = packed[:, SC_ROW:]


def _experts(xs, tmap, wg, wu, wd):
    n_tiles = xs.shape[0] // (2 * SLOT_TM)
    grid_spec = pltpu.PrefetchScalarGridSpec(
        num_scalar_prefetch=1,
        grid=(n_tiles,),
        in_specs=[
            pl.BlockSpec((2 * SLOT_TM, SC_ROW), lambda j, tm: (tm[2, j], 0)),
            pl.BlockSpec((1, D_MODEL, D_EXPERT), lambda j, tm: (tm[0, j], 0, 0)),
            pl.BlockSpec((1, D_MODEL, D_EXPERT), lambda j, tm: (tm[0, j], 0, 0)),
            pl.BlockSpec((1, D_EXPERT, D_MODEL), lambda j, tm: (tm[0, j], 0, 0)),
        ],
        out_specs=pl.BlockSpec((2 * SLOT_TM, SC_ROW), lambda j, tm: (tm[2, j], 0)),
        scratch_shapes=[pltpu.VMEM((D_MODEL, D_EXPERT), BF16), pltpu.VMEM((D_MODEL, D_EXPERT), BF16),
                        pltpu.VMEM((D_EXPERT, D_MODEL), BF16)],
    )
    return pl.pallas_call(
        _experts_kernel,
        out_shape=jax.ShapeDtypeStruct(xs.shape, jnp.int32),
        grid_spec=grid_spec,
        compiler_params=_cparams(("arbitrary",)),
        name="experts",
    )(tmap, xs, wg, wu, wd)


FIN_TM = 1024


def _final_kernel(x1_ref, mod_ref, yg_ref, info_ref, gf_ref, y_ref):
    w_cols = info_ref[...].T
    moe = None
    for k in range(2):
        base = 2 * k * FIN_TM
        yk = _unpack_bf16_pairs(yg_ref[base:base + FIN_TM, :], yg_ref[base + FIN_TM:base + 2 * FIN_TM, :])
        term = w_cols[:, 2 + k:3 + k] * yk.astype(F32)
        moe = term if moe is None else moe + term
    x2 = x1_ref[...] + mod_ref[0, 5:6, :] * moe
    y_ref[...] = x2 * _rms_scale(x2) * gf_ref[...]


def _final(x1, mod3, mod_row_fn, yg, info, g_final):
    n = x1.shape[0]
    return pl.pallas_call(
        _final_kernel,
        out_shape=jax.ShapeDtypeStruct((n, D_MODEL), F32),
        grid=(n // FIN_TM,),
        in_specs=[
            pl.BlockSpec((FIN_TM, D_MODEL), lambda i: (i, 0)),
            pl.BlockSpec((1, 6, D_MODEL), lambda i: (mod_row_fn(i * FIN_TM), 0, 0)),
            pl.BlockSpec((4 * FIN_TM, SC_ROW), lambda i: (i, 0)),
            pl.BlockSpec((8, FIN_TM), lambda i: (0, i)),
            _const_spec((1, D_MODEL)),
        ],
        out_specs=pl.BlockSpec((FIN_TM, D_MODEL), lambda i: (i, 0)),
        compiler_params=_cparams(("parallel",)),
        name="final",
    )(x1, mod3, yg, info, g_final)


def _moe(x1, h2p, info, mod3, mod_row_fn, wg, wu, wd, g_final):
    n = x1.shape[0]
    sidx, gidx, tmap = _plan(info)
    n_slots = 2 * n + N_EXPERTS * SLOT_TM
    xs = _sc_scatter_rows(h2p, sidx.reshape(-1), 2 * n_slots)
    ys = _experts(xs, tmap, wg, wu, wd)
    yg = _sc_gather_rows(ys, gidx.reshape(-1))
    return _final(x1, mod3, mod_row_fn, yg, info, g_final)


def _stream(x, mod3, mod_row_fn, seq_len, rows_2d, state, w):
    bsz, t, _ = x.shape
    n = bsz * t
    x2d = x.reshape(n, D_MODEL)
    p3, bcum, imb, amax = _in_projection(x2d, mod3, mod_row_fn, w["g_norm1"], w["wmain"], w["bmain"],
                                         w["colscale"], w["w_gates"], w["b_gates"])
    m0 = None if state is None else state[2]
    cols, rows, cs = _gate_prep(bcum, imb, amax, m0)
    ml = _mlstm(p3, cols, rows, cs, w["gn3"], None if state is None else state[:2])
    amat, invc = _pool_constants(seq_len, rows_2d)
    n_seq = max(1, 1024 // seq_len)
    yb = _pool(p3, amat, invc, w["wpg"], w["pscale"], seq_len, n_seq, rows_2d)
    ya = ml if state is not None else ml[0]
    x1, h2p, info = _mix(x2d, mod3, mod_row_fn, ya, yb, p3, w["wa"], w["wb"], w["wo"],
                         w["g_norm2"], w["wr"], w["br"])
    y = _moe(x1, h2p, info, mod3, mod_row_fn, w["wg"], w["wu"], w["wd"], w["g_final"])
    return y.reshape(bsz, t, D_MODEL), ml, cs


def kernel(x_prompt, x_sample, state_C, state_n, state_m, c, c_ctx, w_mod, b_mod, g_norm1, w_in, b_in, gn_gain, w_pool_grp, pool_scale, w_proj_a, w_proj_b, w_out, g_norm2, w_router_grp, b_router_grp, w_router_exp, b_router_exp, w_exp_gate, w_exp_up, w_exp_down, g_final):
    bp, tp, _ = x_prompt.shape
    bs, ts, _ = x_sample.shape
    assert w_mod.shape[0] == 1, "single trunk layer"
    assert tp == CHUNK and ts == ML_ROWS and bs % ML_SEQS == 0

    cvec = jnp.zeros((8, D_MODEL), F32).at[0].set(c_ctx).at[1:1 + bs].set(c)
    mod3 = _modulation(cvec, w_mod[0], b_mod[0])

    wi, bi = w_in[0], b_in[0]
    d = D_MODEL
    o_q, o_k, o_v, o_o, o_g, o_p, o_ga, o_gb = 0, d, 2 * d, 3 * d, 4 * d, 4 * d + 32, 4 * d + 32 + 512, 4 * d + 32 + 512 + d
    order = [(o_q, d), (o_k, d), (o_v, d), (o_o, d), (o_ga, d), (o_gb, d), (o_p, 512)]
    wmain = _pack_columns(wi, order)
    bmain = jnp.concatenate([bi[s:s + l] for s, l in order]).reshape(1, D_MAIN)
    colscale = jnp.ones((D_MAIN,), F32).at[SLAB_K * LANES:(SLAB_K + N_HEADS) * LANES].set(HEAD_DIM ** -0.5)
    colscale = colscale.reshape(1, D_MAIN)
    gwt = _take_columns_t(wi, o_g, 32)
    gb = bi[o_g:o_g + 32]
    i_f, f_f, i_b, f_b = (slice(0, 8), slice(8, 16), slice(16, 24), slice(24, 32))
    pad_w = jnp.zeros((LANES - 2 * N_HD, d), F32)
    pad_b = jnp.zeros((LANES - 2 * N_HD,), F32)
    wg = jnp.concatenate([gwt[f_f], gwt[f_b], gwt[i_f], gwt[i_b], pad_w], axis=0).T
    bg = jnp.concatenate([gb[f_f], gb[f_b], gb[i_f], gb[i_b], pad_b]).reshape(1, LANES)
    wg = jnp.concatenate(_split_bf16(wg), axis=1)

    wr = jnp.concatenate([w_router_grp[0], w_router_exp[0].reshape(d, N_EXPERTS),
                          jnp.zeros((d, LANES - ROUTER_ROWS), F32)], axis=1)
    br = jnp.concatenate([b_router_grp[0], b_router_exp[0].reshape(N_EXPERTS),
                          jnp.zeros((LANES - ROUTER_ROWS,), F32)]).reshape(1, LANES)
    wr = jnp.concatenate(_split_bf16(wr), axis=1)

    w = dict(
        g_norm1=g_norm1[0].reshape(1, d), wmain=wmain, bmain=bmain, colscale=colscale,
        w_gates=wg, b_gates=bg,
        gn3=gn_gain[0].reshape(N_HEADS, 1, HEAD_DIM),
        wpg=w_pool_grp[0].astype(BF16), pscale=pool_scale[0].reshape(N_GROUPS, 1, LANES),
        wa=w_proj_a[0].astype(BF16), wb=w_proj_b[0].astype(BF16), wo=w_out[0].astype(BF16),
        g_norm2=g_norm2[0].reshape(1, d), wr=wr, br=br,
        wg=w_exp_gate[0], wu=w_exp_up[0], wd=w_exp_down[0],
        g_final=g_final.reshape(1, d),
    )

    y_prompt, (_, cst, nst), cs_prompt = _stream(x_prompt, mod3, lambda r: 0, tp, None, None, w)
    n0 = state_n[:, 0].reshape(bs, 2, N_HEADS, 1, HEAD_DIM)
    m0 = jnp.pad(state_m[:, 0].reshape(bs, N_HD), ((0, 0), (0, LANES - N_HD))).reshape(bs, 1, LANES)
    y_sample, _, _ = _stream(x_sample, mod3, lambda r: 1 + r // ts, ts, ts // GRID_W, (state_C, n0, m0), w)

    new_c = cst
    new_n = nst.reshape(bp, 1, 2, N_HEADS, HEAD_DIM)
    new_m = cs_prompt[:, 1, :N_HD].reshape(bp, 1, 2, N_HEADS)
    return (y_prompt, y_sample, new_c, new_n, new_m)
```

```python
import functools

import numpy as np
import jax
import jax.numpy as jnp
from jax import lax
from jax.experimental import pallas as pl
from jax.experimental.pallas import tpu as pltpu
from jax.experimental.pallas import tpu_sc as plsc

F32 = jnp.float32
BF16 = jnp.bfloat16

D_MODEL = 1024
N_HEADS = 8
HEAD_DIM = 128
GRID_W = 64
POOL_WINDOWS = (2, 4, 8, 16)
N_GROUPS = 4
EXPERTS_PER_GROUP = 4
N_EXPERTS = 16
D_EXPERT = 512
EPS = 1e-6
NEG = -1e30

LANES = 128
CHUNK = 256
N_SLABS = 52
SLAB_Q, SLAB_K, SLAB_V, SLAB_O, SLAB_GA, SLAB_GB, SLAB_POOL = 0, 8, 16, 24, 32, 40, 48
D_MAIN = N_SLABS * LANES
VMEM_LIMIT = 56 * 1024 * 1024


def _cparams(sem):
    return pltpu.CompilerParams(dimension_semantics=sem, vmem_limit_bytes=VMEM_LIMIT)


def _const_spec(shape):
    nd = len(shape)
    return pl.BlockSpec(shape, lambda *_: (0,) * nd, pipeline_mode=pl.Buffered(1))


def _split_bf16(x):
    hi = x.astype(BF16)
    lo = (x - hi.astype(F32)).astype(BF16)
    return hi, lo


def _dot(a, b):
    return jnp.dot(a, b, preferred_element_type=F32)


def _dot3(a, b_hi, b_lo):
    a_hi, a_lo = _split_bf16(a)
    return _dot(a_hi, b_hi) + _dot(a_hi, b_lo) + _dot(a_lo, b_hi)


def _dot3_packed(a_hi, a_lo, b_hilo):
    first = _dot(a_hi, b_hilo)
    return first[:, :LANES] + first[:, LANES:] + _dot(a_lo, b_hilo[:, :LANES])


def _rms_scale(x):
    return lax.rsqrt(jnp.mean(x * x, axis=-1, keepdims=True) + EPS)


def _log_sigmoid(x):
    return jnp.minimum(x, 0.0) - jnp.log1p(jnp.exp(-jnp.abs(x)))


def _mod_kernel(c_ref, w_ref, b_ref, o_ref):
    c = c_ref[...]
    s = c * jax.nn.sigmoid(c)
    w_hi, w_lo = _split_bf16(w_ref[...])
    o_ref[...] = _dot3(s, w_hi, w_lo) + b_ref[...]


def _modulation(cvec, w_mod, b_mod):
    tn = 2048
    out = pl.pallas_call(
        _mod_kernel,
        out_shape=jax.ShapeDtypeStruct((8, 6 * D_MODEL), F32),
        grid=(6 * D_MODEL // tn,),
        in_specs=[
            pl.BlockSpec((8, D_MODEL), lambda j: (0, 0)),
            pl.BlockSpec((D_MODEL, tn), lambda j: (0, j)),
            pl.BlockSpec((1, tn), lambda j: (0, j)),
        ],
        out_specs=pl.BlockSpec((8, tn), lambda j: (0, j)),
        compiler_params=_cparams(("parallel",)),
        name="modulation",
    )(cvec, w_mod, b_mod.reshape(1, -1))
    return out.reshape(8, 6, D_MODEL)


PACK_COLS = 512


def _pack_kernel(wt_ref, o_ref):
    o_ref[...] = wt_ref[...].T.astype(BF16)


def _pack_columns(w, order):
    k, n = w.shape
    starts = []
    for src, width in order:
        assert width % PACK_COLS == 0
        starts += [src + j for j in range(0, width, PACK_COLS)]
    n_out = len(starts) * PACK_COLS

    unit = 32
    assert all(s % unit == 0 for s in starts)

    def src_row(j):
        row = jnp.int32(starts[-1] // unit)
        for idx in range(len(starts) - 2, -1, -1):
            row = jnp.where(j == idx, starts[idx] // unit, row)
        return row * unit

    return pl.pallas_call(
        _pack_kernel,
        out_shape=jax.ShapeDtypeStruct((k, n_out), BF16),
        grid=(len(starts),),
        in_specs=[pl.BlockSpec((pl.Element(PACK_COLS), pl.Element(k)), lambda j: (src_row(j), 0))],
        out_specs=pl.BlockSpec((k, PACK_COLS), lambda j: (0, j)),
        compiler_params=_cparams(("parallel",)),
        name="pack_columns",
    )(w.T)


def _copy_kernel(w_ref, o_ref):
    o_ref[...] = w_ref[...]


def _take_columns_t(w, start, width):
    k = w.shape[0]
    return pl.pallas_call(
        _copy_kernel,
        out_shape=jax.ShapeDtypeStruct((width, k), w.dtype),
        grid=(1,),
        in_specs=[pl.BlockSpec((pl.Element(width), pl.Element(k)), lambda j: (start, 0))],
        out_specs=pl.BlockSpec((width, k), lambda j: (0, 0)),
        name="take_columns",
    )(w.T)


IN_TM = 512
IN_TN = 512
LOG2E = 1.4426950408889634
N_HD = 2 * N_HEADS


def _fwd_lanes(shape):
    lane = lax.broadcasted_iota(jnp.int32, shape, len(shape) - 1)
    return (lane & N_HEADS) == 0


def _inproj_kernel(x_ref, mod_ref, g_ref, w_ref, b_ref, cs_ref, wg_ref, bg_ref,
                   p3_ref, bcum_ref, imb_ref, amax_ref):
    x = x_ref[...]
    h = x * _rms_scale(x) * g_ref[...]
    h = h * (1.0 + mod_ref[0, 1:2, :]) + mod_ref[0, 0:1, :]
    hb = h.astype(BF16)
    for n in range(D_MAIN // IN_TN):
        sl = slice(n * IN_TN, (n + 1) * IN_TN)
        acc = (_dot(hb, w_ref[:, sl]) + b_ref[:, sl]) * cs_ref[:, sl]
        accb = acc.astype(BF16)
        for j in range(IN_TN // LANES):
            p3_ref[n * (IN_TN // LANES) + j] = accb[:, j * LANES:(j + 1) * LANES]

    h_lo = (h - hb.astype(F32)).astype(BF16)
    gates = _dot3_packed(hb, h_lo, wg_ref[...]) + bg_ref[...]
    ls = _log_sigmoid(gates)
    gi = pltpu.roll(gates, LANES - N_HD, 1)
    row = lax.broadcasted_iota(jnp.int32, (CHUNK, LANES), 0)
    fwd = _fwd_lanes((CHUNK, LANES))
    for c in range(IN_TM // CHUNK):
        rows = slice(c * CHUNK, (c + 1) * CHUNK)
        cf = ls[rows]
        cb = cf
        s = 1
        while s < CHUNK:
            cf = cf + jnp.where(row >= s, pltpu.roll(cf, s, 0), 0.0)
            cb = cb + jnp.where(row < CHUNK - s, pltpu.roll(cb, CHUNK - s, 0), 0.0)
            s *= 2
        b = jnp.where(fwd, cf, cb)
        imb = gi[rows] - b
        mf = imb
        mb = imb
        s = 1
        while s < CHUNK:
            mf = jnp.maximum(mf, jnp.where(row >= s, pltpu.roll(mf, s, 0), -jnp.inf))
            mb = jnp.maximum(mb, jnp.where(row < CHUNK - s, pltpu.roll(mb, CHUNK - s, 0), -jnp.inf))
            s *= 2
        bcum_ref[rows, :] = b
        imb_ref[rows, :] = imb
        amax_ref[rows, :] = b + jnp.where(fwd, mf, mb)


def _in_projection(x2d, mod3, mod_row_fn, g_norm1, wmain, bmain, colscale, wg, bg):
    n = x2d.shape[0]
    return pl.pallas_call(
        _inproj_kernel,
        out_shape=(jax.ShapeDtypeStruct((N_SLABS, n, LANES), BF16),) + (jax.ShapeDtypeStruct((n, LANES), F32),) * 3,
        grid=(n // IN_TM,),
        in_specs=[
            pl.BlockSpec((IN_TM, D_MODEL), lambda i: (i, 0)),
            pl.BlockSpec((1, 6, D_MODEL), lambda i: (mod_row_fn(i * IN_TM), 0, 0)),
            _const_spec((1, D_MODEL)),
            _const_spec((D_MODEL, D_MAIN)),
            _const_spec((1, D_MAIN)),
            _const_spec((1, D_MAIN)),
            _const_spec((D_MODEL, 2 * LANES)),
            _const_spec((1, LANES)),
        ],
        out_specs=(pl.BlockSpec((N_SLABS, IN_TM, LANES), lambda i: (0, i, 0)),)
        + (pl.BlockSpec((IN_TM, LANES), lambda i: (i, 0)),) * 3,
        compiler_params=_cparams(("parallel",)),
        name="in_projection",
    )(x2d, mod3, g_norm1, wmain, bmain, colscale, wg, bg)


GP_ROWS = 1024
GP_NCH = GP_ROWS // CHUNK
Q_C2, Q_EM, Q_WI, Q_WC = 0, 1, 2, 3


def _gate_prep_kernel(*refs, carry):
    if carry:
        b_ref, imb_ref, a_ref, m0_ref, cols_ref, rw_ref, cs_ref = refs
    else:
        b_ref, imb_ref, a_ref, cols_ref, rw_ref, cs_ref = refs
    fwd1 = _fwd_lanes((1, LANES))
    lane = lax.broadcasted_iota(jnp.int32, (CHUNK, LANES), 1)
    neg = jnp.full((1, LANES), NEG, F32)

    def at_scan_end(ref, c):
        return jnp.where(fwd1, ref[(c + 1) * CHUNK - 1:(c + 1) * CHUNK, :], ref[c * CHUNK:c * CHUNK + 1, :])

    b_last = [at_scan_end(b_ref, c) for c in range(GP_NCH)]
    a_last = [at_scan_end(a_ref, c) for c in range(GP_NCH)]
    m_start = m0_ref[0] if carry else neg
    m_prev = [[None] * GP_NCH, [None] * GP_NCH]
    m_new = [[None] * GP_NCH, [None] * GP_NCH]
    for d, order in enumerate((range(GP_NCH), range(GP_NCH - 1, -1, -1))):
        m = m_start
        for c in order:
            m_prev[d][c] = m
            m_new[d][c] = jnp.maximum(b_last[c] + m, a_last[c])
            m = m_new[d][c] if carry else neg

    row_src = []
    for c in range(GP_NCH):
        rows = slice(c * CHUNK, (c + 1) * CHUNK)
        mp = jnp.where(fwd1, m_prev[0][c], m_prev[1][c])
        mn = jnp.where(fwd1, m_new[0][c], m_new[1][c])
        b = b_ref[rows, :]
        imb = imb_ref[rows, :]
        inter = b + mp
        m_t = jnp.maximum(inter, a_ref[rows, :])
        c2 = (b - m_t) * LOG2E
        em = jnp.exp(-m_t)
        wc = jnp.exp(b_last[c] - mn + imb)
        packed = jnp.where(lane < N_HD, c2, pltpu.roll(em, N_HD * Q_EM, 1))
        if carry:
            wi = jnp.exp(inter - m_t)
            packed = jnp.where(lane < N_HD * Q_WI, packed, pltpu.roll(wi, N_HD * Q_WI, 1))
        cols_ref[rows, :] = jnp.where(lane < N_HD * Q_WC, packed, pltpu.roll(wc, N_HD * Q_WC, 1))
        row_src.append(jnp.where(lane < N_HD, imb * (-LOG2E), pltpu.roll(wc, N_HD, 1)))
        cs_ref[c, 0:1, :] = jnp.exp(b_last[c] + mp - mn)
        cs_ref[c, 1:2, :] = mn

    rows_t = jnp.concatenate(row_src, axis=0).T
    for r in range(2 * N_HD):
        rw_ref[r] = rows_t[r:r + 1]


def _gate_prep(bcum, imb, amax, m0=None):
    n = bcum.shape[0]
    carry = m0 is not None
    blk = pl.BlockSpec((GP_ROWS, LANES), lambda i: (i, 0))
    in_specs = [blk, blk, blk]
    args = [bcum, imb, amax]
    if carry:
        in_specs.append(pl.BlockSpec((1, 1, LANES), lambda i: (i, 0, 0)))
        args.append(m0)
    return pl.pallas_call(
        functools.partial(_gate_prep_kernel, carry=carry),
        out_shape=(jax.ShapeDtypeStruct((n, LANES), F32), jax.ShapeDtypeStruct((2 * N_HD, 1, n), F32),
                   jax.ShapeDtypeStruct((n // CHUNK, 2, LANES), F32)),
        grid=(n // GP_ROWS,),
        in_specs=in_specs,
        out_specs=(blk, pl.BlockSpec((2 * N_HD, 1, GP_ROWS), lambda i: (0, 0, i)),
                   pl.BlockSpec((GP_NCH, 2, LANES), lambda i: (i, 0, 0))),
        compiler_params=_cparams(("parallel",)),
        name="gate_prep",
    )(*args)


ML_ROWS = 1024
ML_SEQS = 4
ML_ROWS_FREE = 4096


def _pick_col(blk, lane_iota, idx):
    return jnp.sum(jnp.where(lane_iota == idx, blk, 0.0), axis=1, keepdims=True)


def _mlstm_kernel(*refs, carry, n_chunks, n_seqs):
    if carry:
        (q_ref, k_ref, v_ref, o_ref, cols_ref, r2f_ref, r2b_ref, gn_ref, cs_ref, c0_ref, n0_ref,
         ya_ref, s_scr, h_scr) = refs
    else:
        (q_ref, k_ref, v_ref, o_ref, cols_ref, r2f_ref, r2b_ref, gn_ref, wrf_ref, wrb_ref,
         ya_ref, cst_ref, nst_ref, s_scr, h_scr) = refs
    head = pl.program_id(1)
    lane = lax.broadcasted_iota(jnp.int32, (CHUNK, LANES), 1)
    lane1 = lax.broadcasted_iota(jnp.int32, (1, LANES), 1)
    r_idx = lax.broadcasted_iota(jnp.int32, (CHUNK, CHUNK), 0)
    c_idx = lax.broadcasted_iota(jnp.int32, (CHUNK, CHUNK), 1)
    ones = jnp.ones((CHUNK, HEAD_DIM), BF16)
    tn_dims = (((0,), (0,)), ((), ()))

    for d in range(2):
        mask = (c_idx <= r_idx) if d == 0 else (c_idx >= r_idx)
        r2_ref = r2f_ref if d == 0 else r2b_ref
        base = N_HEADS * d + head
        per_seq = n_chunks // n_seqs
        local = range(per_seq) if d == 0 else range(per_seq - 1, -1, -1)
        for sq, step, c in [(sq, step, sq * per_seq + lc) for sq in range(n_seqs) for step, lc in enumerate(local)]:
            if carry and step == 0:
                n_rep = jnp.broadcast_to(n0_ref[sq, d, 0], (HEAD_DIM, HEAD_DIM)).T
                cn = jnp.concatenate([c0_ref[sq, 0, d, 0], n_rep], axis=1)
            rows = slice(c * CHUNK, (c + 1) * CHUNK)
            q = q_ref[0, rows, :]
            k = k_ref[0, rows, :]
            v = v_ref[0, rows, :]
            v1 = jnp.concatenate([v, ones], axis=1)
            if d == 0:
                s = lax.dot_general(q, k, (((1,), (1,)), ((), ())), preferred_element_type=F32)
                s_scr[rows, :] = s
            else:
                s = s_scr[rows, :]
            blk = cols_ref[rows, :]
            arg = _pick_col(blk, lane, base + N_HD * Q_C2) - r2_ref[0, :, rows]
            sw = s * jnp.exp2(jnp.where(mask, arg, NEG))
            nd = _dot(sw.astype(BF16), v1)
            if carry:
                nd = nd + _pick_col(blk, lane, base + N_HD * Q_WI) * _dot(q, cn.astype(BF16))
            hdir = nd[:, :HEAD_DIM] / jnp.maximum(jnp.abs(nd[:, HEAD_DIM:]), _pick_col(blk, lane, base + N_HD * Q_EM))
            if d == 0:
                h_scr[rows, :] = hdir
            else:
                h_scr[rows, :] = h_scr[rows, :] + hdir

            if carry and step == per_seq - 1:
                continue
            w_col = _pick_col(blk, lane, base + N_HD * Q_WC)
            if carry:
                u = lax.dot_general(k, (w_col * v1.astype(F32)).astype(BF16), tn_dims,
                                    preferred_element_type=F32)
                decay = jnp.sum(jnp.where(lane1 == base, cs_ref[c, 0:1, :], 0.0), axis=1, keepdims=True)
                cn = decay * cn + u
            else:
                w_row = (wrf_ref if d == 0 else wrb_ref)[0, :, rows]
                w8 = jnp.broadcast_to(w_row, (8, CHUNK)).astype(BF16)
                cst_ref[c, 0, d, 0] = lax.dot_general(k, (w_col * v.astype(F32)).astype(BF16), tn_dims,
                                                      preferred_element_type=F32)
                nst_ref[c, d, 0] = _dot(w8, k)[0:1, :]

    hh = h_scr[...]
    hn = hh * _rms_scale(hh) * gn_ref[0]
    ya_ref[0] = (hn * jax.nn.sigmoid(o_ref[0].astype(F32))).astype(BF16)


def _mlstm(p3, cols, rows, cs, gn3, state=None):
    n = p3.shape[1]
    carry = state is not None
    n_seqs = ML_SEQS if carry else 1
    rows_step = ML_ROWS * ML_SEQS if carry else ML_ROWS_FREE
    n_chunks = rows_step // CHUNK
    nblk = n // rows_step

    def slab(base):
        return pl.BlockSpec((1, rows_step, LANES), lambda g, h: (base + h, g, 0))

    def row(base):
        return pl.BlockSpec((1, 1, rows_step), lambda g, h: (base + h, 0, g))

    in_specs = [
        slab(SLAB_Q), slab(SLAB_K), slab(SLAB_V), slab(SLAB_O),
        pl.BlockSpec((rows_step, LANES), lambda g, h: (g, 0)),
        row(0), row(N_HEADS),
        pl.BlockSpec((1, 1, LANES), lambda g, h: (h, 0, 0)),
    ]
    args = [p3, p3, p3, p3, cols, rows, rows, gn3]
    ya_shape = jax.ShapeDtypeStruct((N_HEADS, n, LANES), BF16)
    ya_spec = pl.BlockSpec((1, rows_step, LANES), lambda g, h: (h, g, 0))
    state_spec = pl.BlockSpec((n_seqs, 1, 2, 1, HEAD_DIM, HEAD_DIM), lambda g, h: (g, 0, 0, h, 0, 0))
    if carry:
        c0, n0 = state
        in_specs += [pl.BlockSpec((n_chunks, 2, LANES), lambda g, h: (g, 0, 0)), state_spec,
                     pl.BlockSpec((n_seqs, 2, 1, 1, HEAD_DIM), lambda g, h: (g, 0, h, 0, 0))]
        args += [cs, c0, n0]
        out_shape = ya_shape
        out_specs = ya_spec
    else:
        in_specs += [row(N_HD), row(N_HD + N_HEADS)]
        args += [rows, rows]
        nseq = n // CHUNK
        out_shape = (ya_shape,
                     jax.ShapeDtypeStruct((nseq, 1, 2, N_HEADS, HEAD_DIM, HEAD_DIM), F32),
                     jax.ShapeDtypeStruct((nseq, 2, N_HEADS, 1, HEAD_DIM), F32))
        out_specs = (ya_spec,
                     pl.BlockSpec((n_chunks, 1, 2, 1, HEAD_DIM, HEAD_DIM), lambda g, h: (g, 0, 0, h, 0, 0)),
                     pl.BlockSpec((n_chunks, 2, 1, 1, HEAD_DIM), lambda g, h: (g, 0, h, 0, 0)))
    return pl.pallas_call(
        functools.partial(_mlstm_kernel, carry=carry, n_chunks=n_chunks, n_seqs=n_seqs),
        out_shape=out_shape,
        grid=(nblk, N_HEADS),
        in_specs=in_specs,
        out_specs=out_specs,
        scratch_shapes=[pltpu.VMEM((rows_step, CHUNK), F32), pltpu.VMEM((rows_step, HEAD_DIM), F32)],
        compiler_params=_cparams(("parallel", "parallel")),
        name="mlstm_latent" if carry else "mlstm_prompt",
    )(*args)


def _pool_kernel(u_ref, a_ref, ic_ref, wpg_ref, ps_ref, yb_ref, *, seq_len, n_seq, rows_2d):
    piece = seq_len if rows_2d is None else GRID_W
    n_piece = n_seq if rows_2d is None else rows_2d
    for g in range(N_GROUPS):
        u = jnp.concatenate([u_ref[g, i * piece:(i + 1) * piece, :] for i in range(n_piece)], axis=1)
        win = _dot(a_ref[g], u)
        if rows_2d is None:
            inv_cnt = jnp.concatenate([ic_ref[g]] * n_piece, axis=1)
        else:
            run = [jnp.zeros((piece, LANES), F32)]
            for r in range(rows_2d):
                run.append(run[-1] + win[:, r * LANES:(r + 1) * LANES])
            half = POOL_WINDOWS[g] // 2
            win = jnp.concatenate([run[min(r - half + POOL_WINDOWS[g], rows_2d)] - run[max(r - half, 0)]
                                   for r in range(rows_2d)], axis=1)
            inv_cnt = jnp.concatenate([ic_ref[g, r * piece:(r + 1) * piece, :] for r in range(rows_2d)], axis=1)
        p = (win * inv_cnt - u.astype(F32)).astype(BF16)
        p = jnp.concatenate([p[:, i * LANES:(i + 1) * LANES] for i in range(n_piece)], axis=0)
        yb_ref[g] = (_dot(p, wpg_ref[g]) * ps_ref[g]).astype(BF16)


def _pool(p3, amat, invc, wpg, pscale, seq_len, n_seq, rows_2d):
    n = p3.shape[1]
    rows = seq_len * n_seq
    return pl.pallas_call(
        functools.partial(_pool_kernel, seq_len=seq_len, n_seq=n_seq, rows_2d=rows_2d),
        out_shape=jax.ShapeDtypeStruct((N_GROUPS, n, LANES), BF16),
        grid=(n // rows,),
        in_specs=[
            pl.BlockSpec((N_GROUPS, rows, LANES), lambda i: (SLAB_POOL // N_GROUPS, i, 0)),
            _const_spec(amat.shape),
            _const_spec((N_GROUPS, seq_len, LANES)),
            _const_spec((N_GROUPS, LANES, LANES)),
            _const_spec((N_GROUPS, 1, LANES)),
        ],
        out_specs=pl.BlockSpec((N_GROUPS, rows, LANES), lambda i: (0, i, 0)),
        compiler_params=_cparams(("parallel",)),
        name="pool",
    )(p3, amat, invc, wpg, pscale)


def _window_matrix(size, w):
    idx = np.arange(size)
    lo = np.clip(idx - w // 2, 0, size)
    hi = np.clip(idx - w // 2 + w, 0, size)
    s = np.arange(size)[None, :]
    a = ((s >= lo[:, None]) & (s < hi[:, None])).astype(np.float32)
    return a, (hi - lo).astype(np.float32)


def _pool_constants(seq_len, rows_2d):
    mats, invs = [], []
    for w in POOL_WINDOWS:
        if rows_2d is None:
            a, cnt = _window_matrix(seq_len, w)
        else:
            _, cr = _window_matrix(rows_2d, w)
            a, cc = _window_matrix(GRID_W, w)
            cnt = np.kron(cr, cc)
        mats.append(a)
        invs.append(np.repeat((1.0 / cnt)[:, None], LANES, axis=1))
    return (jnp.asarray(np.stack(mats), dtype=BF16), jnp.asarray(np.stack(invs), dtype=F32))


MIX_TM = 512


def _mix_kernel(x_ref, mod_ref, ya_ref, yb_ref, ga_ref, gb_ref, wa_ref, wb_ref, wo_ref,
                g2_ref, wr_ref, br_ref, x1_ref, h2_ref, info_ref):
    ya = jnp.concatenate([ya_ref[i] for i in range(N_HEADS)], axis=1)
    yb = jnp.concatenate([yb_ref[i] for i in range(N_GROUPS)], axis=1)
    ga = jnp.concatenate([ga_ref[i] for i in range(8)], axis=1).astype(F32)
    gb = jnp.concatenate([gb_ref[i] for i in range(8)], axis=1).astype(F32)
    merged = jax.nn.sigmoid(ga) * _dot(ya, wa_ref[...]) + jax.nn.sigmoid(gb) * _dot(yb, wb_ref[...])
    out = _dot(merged.astype(BF16), wo_ref[...])
    x1 = x_ref[...] + mod_ref[0, 2:3, :] * out
    x1_ref[...] = x1

    h2 = x1 * _rms_scale(x1) * g2_ref[...]
    h2 = h2 * (1.0 + mod_ref[0, 4:5, :]) + mod_ref[0, 3:4, :]
    h_hi, h_lo = _split_bf16(h2)
    packed = _pack_bf16_pairs(h_hi.astype(F32))
    h2_ref[0:MIX_TM, :] = packed[:, :SC_ROW]
    h2_ref[MIX_TM:2 * MIX_TM, :] = packed[:, SC_ROW:]
    logits = _dot3_packed(h_hi, h_lo, wr_ref[...]) + br_ref[...]
    first, second, w1, w2 = _route_rows(logits.T)
    zero = jnp.zeros_like(w1)
    info_ref[...] = jnp.concatenate([first, second, w1, w2, zero, zero, zero, zero], axis=0)


def _mix(x2d, mod3, mod_row_fn, ya, yb, p3, wa, wb, wo, g_norm2, wr, br):
    n = x2d.shape[0]
    return pl.pallas_call(
        _mix_kernel,
        out_shape=(jax.ShapeDtypeStruct((n, D_MODEL), F32),
                   jax.ShapeDtypeStruct((2 * n, SC_ROW), jnp.int32), jax.ShapeDtypeStruct((8, n), F32)),
        grid=(n // MIX_TM,),
        in_specs=[
            pl.BlockSpec((MIX_TM, D_MODEL), lambda i: (i, 0)),
            pl.BlockSpec((1, 6, D_MODEL), lambda i: (mod_row_fn(i * MIX_TM), 0, 0)),
            pl.BlockSpec((N_HEADS, MIX_TM, LANES), lambda i: (0, i, 0)),
            pl.BlockSpec((N_GROUPS, MIX_TM, LANES), lambda i: (0, i, 0)),
            pl.BlockSpec((8, MIX_TM, LANES), lambda i: (SLAB_GA // 8, i, 0)),
            pl.BlockSpec((8, MIX_TM, LANES), lambda i: (SLAB_GB // 8, i, 0)),
            _const_spec((D_MODEL, D_MODEL)),
            _const_spec((N_GROUPS * LANES, D_MODEL)),
            _const_spec((D_MODEL, D_MODEL)),
            _const_spec((1, D_MODEL)),
            _const_spec((D_MODEL, 2 * LANES)),
            _const_spec((1, LANES)),
        ],
        out_specs=(pl.BlockSpec((MIX_TM, D_MODEL), lambda i: (i, 0)),
                   pl.BlockSpec((2 * MIX_TM, SC_ROW), lambda i: (i, 0)),
                   pl.BlockSpec((8, MIX_TM), lambda i: (0, i))),
        compiler_params=_cparams(("parallel",)),
        name="mix",
    )(x2d, mod3, ya, yb, p3, p3, wa, wb, wo, g_norm2, wr, br)


PLAN_TB = 1024
ROUTER_ROWS = 4 + N_EXPERTS
SLOT_TM = 1024
SC_ROW = 256
SC_WIN = 128
HI_MASK = -65536


def _pack_bf16_pairs(x):
    bits = pltpu.bitcast(x, jnp.int32)
    half = D_MODEL // 2
    return jnp.bitwise_or(jnp.bitwise_and(lax.shift_right_logical(bits[:, :half], 16), 0xFFFF),
                          jnp.bitwise_and(bits[:, half:], HI_MASK))


def _unpack_bf16_pairs(lo_words, hi_words):
    def low(w):
        return pltpu.bitcast(lax.shift_left(w, 16), F32)

    def high(w):
        return pltpu.bitcast(jnp.bitwise_and(w, HI_MASK), F32)

    return jnp.concatenate([low(lo_words), low(hi_words), high(lo_words), high(hi_words)], axis=1).astype(BF16)


def _route_rows(lt):
    lg = [lt[i:i + 1] for i in range(N_GROUPS)]
    mx = jnp.maximum(jnp.maximum(lg[0], lg[1]), jnp.maximum(lg[2], lg[3]))
    p_top = 1.0 / (jnp.exp(lg[0] - mx) + jnp.exp(lg[1] - mx) + jnp.exp(lg[2] - mx) + jnp.exp(lg[3] - mx))
    gidx = jnp.zeros(lg[0].shape, jnp.int32)
    best = lg[0]
    for i in range(1, N_GROUPS):
        upd = lg[i] > best
        gidx = jnp.where(upd, i, gidx)
        best = jnp.where(upd, lg[i], best)

    def expert_row(g, e):
        r = N_GROUPS + g * EXPERTS_PER_GROUP + e
        return lt[r:r + 1]

    le = [jnp.where(gidx == 0, expert_row(0, e),
                    jnp.where(gidx == 1, expert_row(1, e),
                              jnp.where(gidx == 2, expert_row(2, e), expert_row(3, e))))
          for e in range(EXPERTS_PER_GROUP)]
    v1 = jnp.maximum(jnp.maximum(le[0], le[1]), jnp.maximum(le[2], le[3]))
    i1 = jnp.full(v1.shape, EXPERTS_PER_GROUP - 1, jnp.int32)
    for e in range(EXPERTS_PER_GROUP - 2, -1, -1):
        i1 = jnp.where(le[e] == v1, e, i1)
    le2 = [jnp.where(i1 == e, -jnp.inf, le[e]) for e in range(EXPERTS_PER_GROUP)]
    v2 = jnp.maximum(jnp.maximum(le2[0], le2[1]), jnp.maximum(le2[2], le2[3]))
    i2 = jnp.full(v2.shape, EXPERTS_PER_GROUP - 1, jnp.int32)
    for e in range(EXPERTS_PER_GROUP - 2, -1, -1):
        i2 = jnp.where(le2[e] == v2, e, i2)
    e2 = jnp.exp(v2 - v1)
    first = (gidx * EXPERTS_PER_GROUP + i1).astype(F32)
    second = (gidx * EXPERTS_PER_GROUP + i2).astype(F32)
    return first, second, p_top / (1.0 + e2), p_top * e2 / (1.0 + e2)


def _plan_kernel(info_ref, before_ref, sidx_ref, gidx_ref, tmap_ref, rank_scr, count_scr, *, n_blocks, min_tiles):
    step = pl.program_id(0)
    expert_id = lax.broadcasted_iota(jnp.int32, (N_EXPERTS, PLAN_TB), 0).astype(F32)

    @pl.when(step == 0)
    def _():
        count_scr[...] = jnp.zeros_like(count_scr)

    @pl.when(step < n_blocks)
    def _():
        cols = pl.ds(pl.multiple_of(step * PLAN_TB, PLAN_TB), PLAN_TB)
        sel = jnp.where((info_ref[0:1, cols] == expert_id) | (info_ref[1:2, cols] == expert_id), 1.0, 0.0)
        selb = sel.astype(BF16)
        rank_scr[:, cols] = _dot(selb, before_ref[...]) + count_scr[:, 0:1]
        count_scr[...] += _dot(selb, jnp.ones((PLAN_TB, LANES), BF16))

    @pl.when(step == n_blocks)
    def _():
        counts = count_scr[...]
        padded = jnp.floor((counts + (SLOT_TM - 1)) * (1.0 / SLOT_TM)) * SLOT_TM
        padded = jnp.maximum(padded, float(min_tiles * SLOT_TM))
        starts, ends, run = [], [], jnp.zeros((1, LANES), F32)
        for e in range(N_EXPERTS):
            starts.append(run)
            run = run + padded[e:e + 1]
            ends.append(run)
        n = info_ref.shape[1]
        shift = SLOT_TM.bit_length() - 1
        for k in range(2):
            chosen = info_ref[k:k + 1, :]
            slot = jnp.zeros((1, n), F32)
            for e in range(N_EXPERTS):
                slot = jnp.where(chosen == float(e), starts[e][:, 0:1] + rank_scr[e:e + 1, :], slot)
            slot = slot.astype(jnp.int32)
            low = jnp.left_shift(jnp.right_shift(slot, shift), shift + 1) + jnp.bitwise_and(slot, SLOT_TM - 1)
            for half in range(2):
                rows = low + half * SLOT_TM
                for t in range(n // MIX_TM):
                    sidx_ref[k, t, half:half + 1, :] = rows[:, t * MIX_TM:(t + 1) * MIX_TM]
                for t in range(n // FIN_TM):
                    gidx_ref[t, k, half:half + 1, :] = rows[:, t * FIN_TM:(t + 1) * FIN_TM]
        tile_start = lax.broadcasted_iota(jnp.int32, (1, LANES), 1).astype(F32) * SLOT_TM
        used = jnp.where(tile_start < run, 1.0, 0.0)
        tile_start = jnp.minimum(tile_start, run - SLOT_TM)
        owner = jnp.zeros((1, LANES), F32)
        for e in range(N_EXPERTS - 1):
            owner = owner + jnp.where(ends[e] <= tile_start, 1.0, 0.0)
        tile = tile_start * (1.0 / SLOT_TM)
        tmap_ref[...] = jnp.concatenate([owner, used, tile] + [jnp.zeros((1, LANES), F32)] * 5,
                                        axis=0).astype(jnp.int32)


def _plan(info, min_tiles):
    n = info.shape[1]
    n_blocks = n // PLAN_TB
    t_i = lax.broadcasted_iota(jnp.int32, (PLAN_TB, PLAN_TB), 0)
    t_j = lax.broadcasted_iota(jnp.int32, (PLAN_TB, PLAN_TB), 1)
    before = (t_i < t_j).astype(BF16)
    return pl.pallas_call(
        functools.partial(_plan_kernel, n_blocks=n_blocks, min_tiles=min_tiles),
        out_shape=(jax.ShapeDtypeStruct((2, n // MIX_TM, 2, MIX_TM), jnp.int32),
                   jax.ShapeDtypeStruct((n // FIN_TM, 2, 2, FIN_TM), jnp.int32),
                   jax.ShapeDtypeStruct((8, LANES), jnp.int32)),
        grid=(n_blocks + 1,),
        in_specs=[_const_spec((8, n)), _const_spec((PLAN_TB, PLAN_TB))],
        out_specs=(pl.BlockSpec((2, n // MIX_TM, 2, MIX_TM), lambda i: (0, 0, 0, 0)),
                   pl.BlockSpec((n // FIN_TM, 2, 2, FIN_TM), lambda i: (0, 0, 0, 0)),
                   pl.BlockSpec((8, LANES), lambda i: (0, 0))),
        scratch_shapes=[pltpu.VMEM((N_EXPERTS, n), F32), pltpu.VMEM((N_EXPERTS, LANES), F32)],
        compiler_params=_cparams(("arbitrary",)),
        name="plan",
    )(info, before)


def _sc_mesh():
    return plsc.VectorSubcoreMesh(core_axis_name="core", subcore_axis_name="subcore")


def _sc_scatter_rows(x, idx, n_out):
    n_idx = idx.shape[0]
    src_blocks = x.shape[0] // SC_WIN

    @pl.kernel(out_type=jax.ShapeDtypeStruct((n_out, SC_ROW), x.dtype), mesh=_sc_mesh(), scratch_types=[])
    def scatter_kernel(x_hbm, i_hbm, o_hbm):
        def body(x_vmem, i_vmem):
            pltpu.sync_copy(x_vmem, o_hbm.at[i_vmem.at[0]])

        pltpu.emit_pipeline(
            body,
            grid=(n_idx // SC_WIN,),
            in_specs=[pl.BlockSpec((SC_WIN, SC_ROW), index_map=lambda i: (i % src_blocks, 0)),
                      pl.BlockSpec((1, SC_WIN), index_map=lambda i: (0, i))],
            out_specs=[],
            core_axis_name=("core", "subcore"),
            dimension_semantics=(pltpu.PARALLEL,),
        )(x_hbm, i_hbm)

    return scatter_kernel(x, idx.reshape(1, n_idx))


def _sc_gather_rows(x, idx):
    n_idx = idx.shape[0]

    @pl.kernel(out_type=jax.ShapeDtypeStruct((n_idx, SC_ROW), x.dtype), mesh=_sc_mesh(), scratch_types=[])
    def gather_kernel(x_hbm, i_hbm, o_hbm):
        def body(i_vmem, o_vmem):
            pltpu.sync_copy(x_hbm.at[i_vmem.at[0]], o_vmem)

        pltpu.emit_pipeline(
            body,
            grid=(n_idx // SC_WIN,),
            in_specs=[pl.BlockSpec((1, SC_WIN), index_map=lambda i: (0, i))],
            out_specs=[pl.BlockSpec((SC_WIN, SC_ROW), index_map=lambda i: (i, 0))],
            core_axis_name=("core", "subcore"),
            dimension_semantics=(pltpu.PARALLEL,),
        )(i_hbm, o_hbm)

    return gather_kernel(x, idx.reshape(1, n_idx))


def _experts_kernel(tmap_ref, xs_ref, wg_ref, wu_ref, wd_ref, ys_ref, *wb_refs, cast_weights):
    j = pl.program_id(0)
    if cast_weights:
        wg_b, wu_b, wd_b = wb_refs

        @pl.when((j == 0) | (tmap_ref[0, j] != tmap_ref[0, jnp.maximum(j - 1, 0)]))
        def _():
            wg_b[0] = wg_ref[0].astype(BF16)
            wu_b[0] = wu_ref[0].astype(BF16)
            wd_b[0] = wd_ref[0].astype(BF16)
    else:
        wg_b, wu_b, wd_b = wg_ref, wu_ref, wd_ref

    @pl.when(tmap_ref[1, j] > 0)
    def _():
        x = _unpack_bf16_pairs(xs_ref[0:SLOT_TM, :], xs_ref[SLOT_TM:2 * SLOT_TM, :])
        a = _dot(x, wg_b[0])
        u = _dot(x, wu_b[0])
        act = (a * jax.nn.sigmoid(a) * u).astype(BF16)
        y = _dot(act, wd_b[0]).astype(BF16).astype(F32)
        packed = _pack_bf16_pairs(y)
        ys_ref[0:SLOT_TM, :] = packed[:, :SC_ROW]
        ys_ref[SLOT_TM:2 * SLOT_TM, :] = packed[:, SC_ROW:]


def _experts(xs, tmap, wg, wu, wd):
    n_tiles = xs.shape[0] // (2 * SLOT_TM)
    cast_weights = wg.dtype != BF16
    tile_spec = pl.BlockSpec((2 * SLOT_TM, SC_ROW), lambda j, tm: (tm[2, j], 0))
    w_specs = [pl.BlockSpec((1, D_MODEL, D_EXPERT), lambda j, tm: (tm[0, j], 0, 0)),
               pl.BlockSpec((1, D_MODEL, D_EXPERT), lambda j, tm: (tm[0, j], 0, 0)),
               pl.BlockSpec((1, D_EXPERT, D_MODEL), lambda j, tm: (tm[0, j], 0, 0))]
    out_shape = [jax.ShapeDtypeStruct(xs.shape, jnp.int32)]
    out_specs = [tile_spec]
    if cast_weights:
        out_shape += [jax.ShapeDtypeStruct(w.shape, BF16) for w in (wg, wu, wd)]
        out_specs += w_specs
    grid_spec = pltpu.PrefetchScalarGridSpec(
        num_scalar_prefetch=1,
        grid=(n_tiles,),
        in_specs=[tile_spec] + w_specs,
        out_specs=out_specs,
    )
    out = pl.pallas_call(
        functools.partial(_experts_kernel, cast_weights=cast_weights),
        out_shape=out_shape,
        grid_spec=grid_spec,
        compiler_params=_cparams(("arbitrary",)),
        name="experts",
    )(tmap, xs, wg, wu, wd)
    return out[0], tuple(out[1:])


FIN_TM = 1024


def _final_kernel(x1_ref, mod_ref, yg_ref, info_ref, gf_ref, y_ref):
    w_cols = info_ref[...].T
    moe = None
    for k in range(2):
        base = 2 * k * FIN_TM
        yk = _unpack_bf16_pairs(yg_ref[base:base + FIN_TM, :], yg_ref[base + FIN_TM:base + 2 * FIN_TM, :])
        term = w_cols[:, 2 + k:3 + k] * yk.astype(F32)
        moe = term if moe is None else moe + term
    x2 = x1_ref[...] + mod_ref[0, 5:6, :] * moe
    y_ref[...] = x2 * _rms_scale(x2) * gf_ref[...]


def _final(x1, mod3, mod_row_fn, yg, info, g_final):
    n = x1.shape[0]
    return pl.pallas_call(
        _final_kernel,
        out_shape=jax.ShapeDtypeStruct((n, D_MODEL), F32),
        grid=(n // FIN_TM,),
        in_specs=[
            pl.BlockSpec((FIN_TM, D_MODEL), lambda i: (i, 0)),
            pl.BlockSpec((1, 6, D_MODEL), lambda i: (mod_row_fn(i * FIN_TM), 0, 0)),
            pl.BlockSpec((4 * FIN_TM, SC_ROW), lambda i: (i, 0)),
            pl.BlockSpec((8, FIN_TM), lambda i: (0, i)),
            _const_spec((1, D_MODEL)),
        ],
        out_specs=pl.BlockSpec((FIN_TM, D_MODEL), lambda i: (i, 0)),
        compiler_params=_cparams(("parallel",)),
        name="final",
    )(x1, mod3, yg, info, g_final)


def _moe(x1, h2p, info, mod3, mod_row_fn, wg, wu, wd, g_final):
    n = x1.shape[0]
    sidx, gidx, tmap = _plan(info, min_tiles=int(wg.dtype != BF16))
    n_slots = 2 * n + N_EXPERTS * SLOT_TM
    xs = _sc_scatter_rows(h2p, sidx.reshape(-1), 2 * n_slots)
    ys, w_bf16 = _experts(xs, tmap, wg, wu, wd)
    yg = _sc_gather_rows(ys, gidx.reshape(-1))
    return _final(x1, mod3, mod_row_fn, yg, info, g_final), w_bf16


def _stream(x, mod3, mod_row_fn, seq_len, rows_2d, state, w):
    bsz, t, _ = x.shape
    n = bsz * t
    x2d = x.reshape(n, D_MODEL)
    p3, bcum, imb, amax = _in_projection(x2d, mod3, mod_row_fn, w["g_norm1"], w["wmain"], w["bmain"],
                                         w["colscale"], w["w_gates"], w["b_gates"])
    m0 = None if state is None else state[2]
    cols, rows, cs = _gate_prep(bcum, imb, amax, m0)
    ml = _mlstm(p3, cols, rows, cs, w["gn3"], None if state is None else state[:2])
    amat, invc = _pool_constants(seq_len, rows_2d)
    n_seq = max(1, 1024 // seq_len)
    yb = _pool(p3, amat, invc, w["wpg"], w["pscale"], seq_len, n_seq, rows_2d)
    ya = ml if state is not None else ml[0]
    x1, h2p, info = _mix(x2d, mod3, mod_row_fn, ya, yb, p3, w["wa"], w["wb"], w["wo"],
                         w["g_norm2"], w["wr"], w["br"])
    y, w_bf16 = _moe(x1, h2p, info, mod3, mod_row_fn, w["wg"], w["wu"], w["wd"], w["g_final"])
    return y.reshape(bsz, t, D_MODEL), ml, cs, w_bf16


def kernel(x_prompt, x_sample, state_C, state_n, state_m, c, c_ctx, w_mod, b_mod, g_norm1, w_in, b_in, gn_gain, w_pool_grp, pool_scale, w_proj_a, w_proj_b, w_out, g_norm2, w_router_grp, b_router_grp, w_router_exp, b_router_exp, w_exp_gate, w_exp_up, w_exp_down, g_final):
    bp, tp, _ = x_prompt.shape
    bs, ts, _ = x_sample.shape
    assert w_mod.shape[0] == 1, "single trunk layer"
    assert tp == CHUNK and ts == ML_ROWS and bs % ML_SEQS == 0

    cvec = jnp.zeros((8, D_MODEL), F32).at[0].set(c_ctx).at[1:1 + bs].set(c)
    mod3 = _modulation(cvec, w_mod[0], b_mod[0])

    wi, bi = w_in[0], b_in[0]
    d = D_MODEL
    o_q, o_k, o_v, o_o, o_g, o_p, o_ga, o_gb = 0, d, 2 * d, 3 * d, 4 * d, 4 * d + 32, 4 * d + 32 + 512, 4 * d + 32 + 512 + d
    order = [(o_q, d), (o_k, d), (o_v, d), (o_o, d), (o_ga, d), (o_gb, d), (o_p, 512)]
    wmain = _pack_columns(wi, order)
    bmain = jnp.concatenate([bi[s:s + l] for s, l in order]).reshape(1, D_MAIN)
    colscale = jnp.ones((D_MAIN,), F32).at[SLAB_K * LANES:(SLAB_K + N_HEADS) * LANES].set(HEAD_DIM ** -0.5)
    colscale = colscale.reshape(1, D_MAIN)
    gwt = _take_columns_t(wi, o_g, 32)
    gb = bi[o_g:o_g + 32]
    i_f, f_f, i_b, f_b = (slice(0, 8), slice(8, 16), slice(16, 24), slice(24, 32))
    pad_w = jnp.zeros((LANES - 2 * N_HD, d), F32)
    pad_b = jnp.zeros((LANES - 2 * N_HD,), F32)
    wg = jnp.concatenate([gwt[f_f], gwt[f_b], gwt[i_f], gwt[i_b], pad_w], axis=0).T
    bg = jnp.concatenate([gb[f_f], gb[f_b], gb[i_f], gb[i_b], pad_b]).reshape(1, LANES)
    wg = jnp.concatenate(_split_bf16(wg), axis=1)

    wr = jnp.concatenate([w_router_grp[0], w_router_exp[0].reshape(d, N_EXPERTS),
                          jnp.zeros((d, LANES - ROUTER_ROWS), F32)], axis=1)
    br = jnp.concatenate([b_router_grp[0], b_router_exp[0].reshape(N_EXPERTS),
                          jnp.zeros((LANES - ROUTER_ROWS,), F32)]).reshape(1, LANES)
    wr = jnp.concatenate(_split_bf16(wr), axis=1)

    w = dict(
        g_norm1=g_norm1[0].reshape(1, d), wmain=wmain, bmain=bmain, colscale=colscale,
        w_gates=wg, b_gates=bg,
        gn3=gn_gain[0].reshape(N_HEADS, 1, HEAD_DIM),
        wpg=w_pool_grp[0].astype(BF16), pscale=pool_scale[0].reshape(N_GROUPS, 1, LANES),
        wa=w_proj_a[0].astype(BF16), wb=w_proj_b[0].astype(BF16), wo=w_out[0].astype(BF16),
        g_norm2=g_norm2[0].reshape(1, d), wr=wr, br=br,
        wg=w_exp_gate[0], wu=w_exp_up[0], wd=w_exp_down[0],
        g_final=g_final.reshape(1, d),
    )

    y_prompt, (_, cst, nst), cs_prompt, w_bf16 = _stream(x_prompt, mod3, lambda r: 0, tp, None, None, w)
    w = dict(w, wg=w_bf16[0], wu=w_bf16[1], wd=w_bf16[2])
    n0 = state_n[:, 0].reshape(bs, 2, N_HEADS, 1, HEAD_DIM)
    m0 = jnp.pad(state_m[:, 0].reshape(bs, N_HD), ((0, 0), (0, LANES - N_HD))).reshape(bs, 1, LANES)
    y_sample, _, _, _ = _stream(x_sample, mod3, lambda r: 1 + r // ts, ts, ts // GRID_W, (state_C, n0, m0), w)

    new_c = cst
    new_n = nst.reshape(bp, 1, 2, N_HEADS, HEAD_DIM)
    new_m = cs_prompt[:, 1, :N_HD].reshape(bp, 1, 2, N_HEADS)
    return (y_prompt, y_sample, new_c, new_n, new_m)
```

```python
import functools

import numpy as np
import jax
import jax.numpy as jnp
from jax import lax
from jax.experimental import pallas as pl
from jax.experimental.pallas import tpu as pltpu
from jax.experimental.pallas import tpu_sc as plsc

F32 = jnp.float32
BF16 = jnp.bfloat16

D_MODEL = 1024
N_HEADS = 8
HEAD_DIM = 128
GRID_W = 64
POOL_WINDOWS = (2, 4, 8, 16)
N_GROUPS = 4
EXPERTS_PER_GROUP = 4
N_EXPERTS = 16
D_EXPERT = 512
EPS = 1e-6
NEG = -1e30

LANES = 128
CHUNK = 256
N_SLABS = 52
SLAB_Q, SLAB_K, SLAB_V, SLAB_O, SLAB_GA, SLAB_GB, SLAB_POOL = 0, 8, 16, 24, 32, 40, 48
D_MAIN = N_SLABS * LANES
VMEM_LIMIT = 56 * 1024 * 1024


def _cparams(sem):
    return pltpu.CompilerParams(dimension_semantics=sem, vmem_limit_bytes=VMEM_LIMIT)


def _const_spec(shape):
    nd = len(shape)
    return pl.BlockSpec(shape, lambda *_: (0,) * nd, pipeline_mode=pl.Buffered(1))


def _split_bf16(x):
    hi = x.astype(BF16)
    lo = (x - hi.astype(F32)).astype(BF16)
    return hi, lo


def _dot(a, b):
    return jnp.dot(a, b, preferred_element_type=F32)


def _dot3(a, b_hi, b_lo):
    a_hi, a_lo = _split_bf16(a)
    return _dot(a_hi, b_hi) + _dot(a_hi, b_lo) + _dot(a_lo, b_hi)


def _dot3_packed(a_hi, a_lo, b_hilo):
    first = _dot(a_hi, b_hilo)
    return first[:, :LANES] + first[:, LANES:] + _dot(a_lo, b_hilo[:, :LANES])


def _rms_scale(x):
    return lax.rsqrt(jnp.mean(x * x, axis=-1, keepdims=True) + EPS)


def _log_sigmoid(x):
    return jnp.minimum(x, 0.0) - jnp.log1p(jnp.exp(-jnp.abs(x)))


def _mod_kernel(c_ref, w_ref, b_ref, o_ref):
    c = c_ref[...]
    s = c * jax.nn.sigmoid(c)
    w_hi, w_lo = _split_bf16(w_ref[...])
    o_ref[...] = _dot3(s, w_hi, w_lo) + b_ref[...]


def _modulation(cvec, w_mod, b_mod):
    tn = 2048
    out = pl.pallas_call(
        _mod_kernel,
        out_shape=jax.ShapeDtypeStruct((8, 6 * D_MODEL), F32),
        grid=(6 * D_MODEL // tn,),
        in_specs=[
            pl.BlockSpec((8, D_MODEL), lambda j: (0, 0)),
            pl.BlockSpec((D_MODEL, tn), lambda j: (0, j)),
            pl.BlockSpec((1, tn), lambda j: (0, j)),
        ],
        out_specs=pl.BlockSpec((8, tn), lambda j: (0, j)),
        compiler_params=_cparams(("parallel",)),
        name="modulation",
    )(cvec, w_mod, b_mod.reshape(1, -1))
    return out.reshape(8, 6, D_MODEL)


PACK_COLS = 512


def _pack_kernel(wt_ref, o_ref):
    o_ref[...] = wt_ref[...].T.astype(BF16)


def _pack_columns(w, order):
    k, n = w.shape
    starts = []
    for src, width in order:
        assert width % PACK_COLS == 0
        starts += [src + j for j in range(0, width, PACK_COLS)]
    n_out = len(starts) * PACK_COLS

    unit = 32
    assert all(s % unit == 0 for s in starts)

    def src_row(j):
        row = jnp.int32(starts[-1] // unit)
        for idx in range(len(starts) - 2, -1, -1):
            row = jnp.where(j == idx, starts[idx] // unit, row)
        return row * unit

    return pl.pallas_call(
        _pack_kernel,
        out_shape=jax.ShapeDtypeStruct((k, n_out), BF16),
        grid=(len(starts),),
        in_specs=[pl.BlockSpec((pl.Element(PACK_COLS), pl.Element(k)), lambda j: (src_row(j), 0))],
        out_specs=pl.BlockSpec((k, PACK_COLS), lambda j: (0, j)),
        compiler_params=_cparams(("parallel",)),
        name="pack_columns",
    )(w.T)


def _copy_kernel(w_ref, o_ref):
    o_ref[...] = w_ref[...]


def _take_columns_t(w, start, width):
    k = w.shape[0]
    return pl.pallas_call(
        _copy_kernel,
        out_shape=jax.ShapeDtypeStruct((width, k), w.dtype),
        grid=(1,),
        in_specs=[pl.BlockSpec((pl.Element(width), pl.Element(k)), lambda j: (start, 0))],
        out_specs=pl.BlockSpec((width, k), lambda j: (0, 0)),
        name="take_columns",
    )(w.T)


IN_TM = 512
IN_TN = 512
LOG2E = 1.4426950408889634
N_HD = 2 * N_HEADS


def _fwd_lanes(shape):
    lane = lax.broadcasted_iota(jnp.int32, shape, len(shape) - 1)
    return (lane & N_HEADS) == 0


def _inproj_kernel(x_ref, mod_ref, g_ref, w_ref, b_ref, cs_ref, wg_ref, bg_ref,
                   p3_ref, bcum_ref, imb_ref, amax_ref):
    x = x_ref[...]
    h = x * _rms_scale(x) * g_ref[...]
    h = h * (1.0 + mod_ref[0, 1:2, :]) + mod_ref[0, 0:1, :]
    hb = h.astype(BF16)
    for n in range(D_MAIN // IN_TN):
        sl = slice(n * IN_TN, (n + 1) * IN_TN)
        acc = (_dot(hb, w_ref[:, sl]) + b_ref[:, sl]) * cs_ref[:, sl]
        accb = acc.astype(BF16)
        for j in range(IN_TN // LANES):
            p3_ref[n * (IN_TN // LANES) + j] = accb[:, j * LANES:(j + 1) * LANES]

    h_lo = (h - hb.astype(F32)).astype(BF16)
    gates = _dot3_packed(hb, h_lo, wg_ref[...]) + bg_ref[...]
    ls = _log_sigmoid(gates)
    gi = pltpu.roll(gates, LANES - N_HD, 1)
    row = lax.broadcasted_iota(jnp.int32, (CHUNK, LANES), 0)
    fwd = _fwd_lanes((CHUNK, LANES))
    for c in range(IN_TM // CHUNK):
        rows = slice(c * CHUNK, (c + 1) * CHUNK)
        cf = ls[rows]
        cb = cf
        s = 1
        while s < CHUNK:
            cf = cf + jnp.where(row >= s, pltpu.roll(cf, s, 0), 0.0)
            cb = cb + jnp.where(row < CHUNK - s, pltpu.roll(cb, CHUNK - s, 0), 0.0)
            s *= 2
        b = jnp.where(fwd, cf, cb)
        imb = gi[rows] - b
        mf = imb
        mb = imb
        s = 1
        while s < CHUNK:
            mf = jnp.maximum(mf, jnp.where(row >= s, pltpu.roll(mf, s, 0), -jnp.inf))
            mb = jnp.maximum(mb, jnp.where(row < CHUNK - s, pltpu.roll(mb, CHUNK - s, 0), -jnp.inf))
            s *= 2
        bcum_ref[rows, :] = b
        imb_ref[rows, :] = imb
        amax_ref[rows, :] = b + jnp.where(fwd, mf, mb)


def _in_projection(x2d, mod3, mod_row_fn, g_norm1, wmain, bmain, colscale, wg, bg):
    n = x2d.shape[0]
    return pl.pallas_call(
        _inproj_kernel,
        out_shape=(jax.ShapeDtypeStruct((N_SLABS, n, LANES), BF16),) + (jax.ShapeDtypeStruct((n, LANES), F32),) * 3,
        grid=(n // IN_TM,),
        in_specs=[
            pl.BlockSpec((IN_TM, D_MODEL), lambda i: (i, 0)),
            pl.BlockSpec((1, 6, D_MODEL), lambda i: (mod_row_fn(i * IN_TM), 0, 0)),
            _const_spec((1, D_MODEL)),
            _const_spec((D_MODEL, D_MAIN)),
            _const_spec((1, D_MAIN)),
            _const_spec((1, D_MAIN)),
            _const_spec((D_MODEL, 2 * LANES)),
            _const_spec((1, LANES)),
        ],
        out_specs=(pl.BlockSpec((N_SLABS, IN_TM, LANES), lambda i: (0, i, 0)),)
        + (pl.BlockSpec((IN_TM, LANES), lambda i: (i, 0)),) * 3,
        compiler_params=_cparams(("parallel",)),
        name="in_projection",
    )(x2d, mod3, g_norm1, wmain, bmain, colscale, wg, bg)


GP_ROWS = 1024
GP_NCH = GP_ROWS // CHUNK
Q_C2, Q_EM, Q_WI, Q_WC = 0, 1, 2, 3


def _gate_prep_kernel(*refs, carry):
    if carry:
        b_ref, imb_ref, a_ref, m0_ref, cols_ref, rw_ref, cs_ref = refs
    else:
        b_ref, imb_ref, a_ref, cols_ref, rw_ref, cs_ref = refs
    fwd1 = _fwd_lanes((1, LANES))
    lane = lax.broadcasted_iota(jnp.int32, (CHUNK, LANES), 1)
    neg = jnp.full((1, LANES), NEG, F32)

    def at_scan_end(ref, c):
        return jnp.where(fwd1, ref[(c + 1) * CHUNK - 1:(c + 1) * CHUNK, :], ref[c * CHUNK:c * CHUNK + 1, :])

    b_last = [at_scan_end(b_ref, c) for c in range(GP_NCH)]
    a_last = [at_scan_end(a_ref, c) for c in range(GP_NCH)]
    m_start = m0_ref[0] if carry else neg
    m_prev = [[None] * GP_NCH, [None] * GP_NCH]
    m_new = [[None] * GP_NCH, [None] * GP_NCH]
    for d, order in enumerate((range(GP_NCH), range(GP_NCH - 1, -1, -1))):
        m = m_start
        for c in order:
            m_prev[d][c] = m
            m_new[d][c] = jnp.maximum(b_last[c] + m, a_last[c])
            m = m_new[d][c] if carry else neg

    row_src = []
    for c in range(GP_NCH):
        rows = slice(c * CHUNK, (c + 1) * CHUNK)
        mp = jnp.where(fwd1, m_prev[0][c], m_prev[1][c])
        mn = jnp.where(fwd1, m_new[0][c], m_new[1][c])
        b = b_ref[rows, :]
        imb = imb_ref[rows, :]
        inter = b + mp
        m_t = jnp.maximum(inter, a_ref[rows, :])
        c2 = (b - m_t) * LOG2E
        em = jnp.exp(-m_t)
        wc = jnp.exp(b_last[c] - mn + imb)
        packed = jnp.where(lane < N_HD, c2, pltpu.roll(em, N_HD * Q_EM, 1))
        if carry:
            wi = jnp.exp(inter - m_t)
            packed = jnp.where(lane < N_HD * Q_WI, packed, pltpu.roll(wi, N_HD * Q_WI, 1))
        cols_ref[rows, :] = jnp.where(lane < N_HD * Q_WC, packed, pltpu.roll(wc, N_HD * Q_WC, 1))
        row_src.append(jnp.where(lane < N_HD, imb * (-LOG2E), pltpu.roll(wc, N_HD, 1)))
        cs_ref[c, 0:1, :] = jnp.exp(b_last[c] + mp - mn)
        cs_ref[c, 1:2, :] = mn

    rows_t = jnp.concatenate(row_src, axis=0).T
    for r in range(2 * N_HD):
        rw_ref[r] = rows_t[r:r + 1]


def _gate_prep(bcum, imb, amax, m0=None):
    n = bcum.shape[0]
    carry = m0 is not None
    blk = pl.BlockSpec((GP_ROWS, LANES), lambda i: (i, 0))
    in_specs = [blk, blk, blk]
    args = [bcum, imb, amax]
    if carry:
        in_specs.append(pl.BlockSpec((1, 1, LANES), lambda i: (i, 0, 0)))
        args.append(m0)
    return pl.pallas_call(
        functools.partial(_gate_prep_kernel, carry=carry),
        out_shape=(jax.ShapeDtypeStruct((n, LANES), F32), jax.ShapeDtypeStruct((2 * N_HD, 1, n), F32),
                   jax.ShapeDtypeStruct((n // CHUNK, 2, LANES), F32)),
        grid=(n // GP_ROWS,),
        in_specs=in_specs,
        out_specs=(blk, pl.BlockSpec((2 * N_HD, 1, GP_ROWS), lambda i: (0, 0, i)),
                   pl.BlockSpec((GP_NCH, 2, LANES), lambda i: (i, 0, 0))),
        compiler_params=_cparams(("parallel",)),
        name="gate_prep",
    )(*args)


ML_ROWS = 1024
ML_SEQS = 4
ML_ROWS_FREE = 4096


def _pick_col(blk, lane_iota, idx):
    return jnp.sum(jnp.where(lane_iota == idx, blk, 0.0), axis=1, keepdims=True)


def _mlstm_kernel(*refs, carry, n_chunks, n_seqs):
    if carry:
        (q_ref, k_ref, v_ref, o_ref, cols_ref, r2f_ref, r2b_ref, gn_ref, cs_ref, c0_ref, n0_ref,
         ya_ref, s_scr, h_scr) = refs
    else:
        (q_ref, k_ref, v_ref, o_ref, cols_ref, r2f_ref, r2b_ref, gn_ref, wrf_ref, wrb_ref,
         ya_ref, cst_ref, nst_ref, s_scr, h_scr) = refs
    head = pl.program_id(1)
    lane = lax.broadcasted_iota(jnp.int32, (CHUNK, LANES), 1)
    lane1 = lax.broadcasted_iota(jnp.int32, (1, LANES), 1)
    r_idx = lax.broadcasted_iota(jnp.int32, (CHUNK, CHUNK), 0)
    c_idx = lax.broadcasted_iota(jnp.int32, (CHUNK, CHUNK), 1)
    ones = jnp.ones((CHUNK, HEAD_DIM), BF16)
    tn_dims = (((0,), (0,)), ((), ()))

    for d in range(2):
        mask = (c_idx <= r_idx) if d == 0 else (c_idx >= r_idx)
        r2_ref = r2f_ref if d == 0 else r2b_ref
        base = N_HEADS * d + head
        per_seq = n_chunks // n_seqs
        local = range(per_seq) if d == 0 else range(per_seq - 1, -1, -1)
        for sq, step, c in [(sq, step, sq * per_seq + lc) for sq in range(n_seqs) for step, lc in enumerate(local)]:
            if carry and step == 0:
                n_rep = jnp.broadcast_to(n0_ref[sq, d, 0], (HEAD_DIM, HEAD_DIM)).T
                cn = jnp.concatenate([c0_ref[sq, 0, d, 0], n_rep], axis=1)
            rows = slice(c * CHUNK, (c + 1) * CHUNK)
            q = q_ref[0, rows, :]
            k = k_ref[0, rows, :]
            v = v_ref[0, rows, :]
            v1 = jnp.concatenate([v, ones], axis=1)
            if d == 0:
                s = lax.dot_general(q, k, (((1,), (1,)), ((), ())), preferred_element_type=F32)
                s_scr[rows, :] = s
            else:
                s = s_scr[rows, :]
            blk = cols_ref[rows, :]
            arg = _pick_col(blk, lane, base + N_HD * Q_C2) - r2_ref[0, :, rows]
            sw = s * jnp.exp2(jnp.where(mask, arg, NEG))
            nd = _dot(sw.astype(BF16), v1)
            if carry:
                nd = nd + _pick_col(blk, lane, base + N_HD * Q_WI) * _dot(q, cn.astype(BF16))
            hdir = nd[:, :HEAD_DIM] / jnp.maximum(jnp.abs(nd[:, HEAD_DIM:]), _pick_col(blk, lane, base + N_HD * Q_EM))
            if d == 0:
                h_scr[rows, :] = hdir
            else:
                h_scr[rows, :] = h_scr[rows, :] + hdir

            if carry and step == per_seq - 1:
                continue
            w_col = _pick_col(blk, lane, base + N_HD * Q_WC)
            if carry:
                u = lax.dot_general(k, (w_col * v1.astype(F32)).astype(BF16), tn_dims,
                                    preferred_element_type=F32)
                decay = jnp.sum(jnp.where(lane1 == base, cs_ref[c, 0:1, :], 0.0), axis=1, keepdims=True)
                cn = decay * cn + u
            else:
                w_row = (wrf_ref if d == 0 else wrb_ref)[0, :, rows]
                w8 = jnp.broadcast_to(w_row, (8, CHUNK)).astype(BF16)
                cst_ref[c, 0, d, 0] = lax.dot_general(k, (w_col * v.astype(F32)).astype(BF16), tn_dims,
                                                      preferred_element_type=F32)
                nst_ref[c, d, 0] = _dot(w8, k)[0:1, :]

    hh = h_scr[...]
    hn = hh * _rms_scale(hh) * gn_ref[0]
    ya_ref[0] = (hn * jax.nn.sigmoid(o_ref[0].astype(F32))).astype(BF16)


def _mlstm(p3, cols, rows, cs, gn3, state=None):
    n = p3.shape[1]
    carry = state is not None
    n_seqs = ML_SEQS if carry else 1
    rows_step = ML_ROWS * ML_SEQS if carry else ML_ROWS_FREE
    n_chunks = rows_step // CHUNK
    nblk = n // rows_step

    def slab(base):
        return pl.BlockSpec((1, rows_step, LANES), lambda g, h: (base + h, g, 0))

    def row(base):
        return pl.BlockSpec((1, 1, rows_step), lambda g, h: (base + h, 0, g))

    in_specs = [
        slab(SLAB_Q), slab(SLAB_K), slab(SLAB_V), slab(SLAB_O),
        pl.BlockSpec((rows_step, LANES), lambda g, h: (g, 0)),
        row(0), row(N_HEADS),
        pl.BlockSpec((1, 1, LANES), lambda g, h: (h, 0, 0)),
    ]
    args = [p3, p3, p3, p3, cols, rows, rows, gn3]
    ya_shape = jax.ShapeDtypeStruct((N_HEADS, n, LANES), BF16)
    ya_spec = pl.BlockSpec((1, rows_step, LANES), lambda g, h: (h, g, 0))
    state_spec = pl.BlockSpec((n_seqs, 1, 2, 1, HEAD_DIM, HEAD_DIM), lambda g, h: (g, 0, 0, h, 0, 0))
    if carry:
        c0, n0 = state
        in_specs += [pl.BlockSpec((n_chunks, 2, LANES), lambda g, h: (g, 0, 0)), state_spec,
                     pl.BlockSpec((n_seqs, 2, 1, 1, HEAD_DIM), lambda g, h: (g, 0, h, 0, 0))]
        args += [cs, c0, n0]
        out_shape = ya_shape
        out_specs = ya_spec
    else:
        in_specs += [row(N_HD), row(N_HD + N_HEADS)]
        args += [rows, rows]
        nseq = n // CHUNK
        out_shape = (ya_shape,
                     jax.ShapeDtypeStruct((nseq, 1, 2, N_HEADS, HEAD_DIM, HEAD_DIM), F32),
                     jax.ShapeDtypeStruct((nseq, 2, N_HEADS, 1, HEAD_DIM), F32))
        out_specs = (ya_spec,
                     pl.BlockSpec((n_chunks, 1, 2, 1, HEAD_DIM, HEAD_DIM), lambda g, h: (g, 0, 0, h, 0, 0)),
                     pl.BlockSpec((n_chunks, 2, 1, 1, HEAD_DIM), lambda g, h: (g, 0, h, 0, 0)))
    return pl.pallas_call(
        functools.partial(_mlstm_kernel, carry=carry, n_chunks=n_chunks, n_seqs=n_seqs),
        out_shape=out_shape,
        grid=(nblk, N_HEADS),
        in_specs=in_specs,
        out_specs=out_specs,
        scratch_shapes=[pltpu.VMEM((rows_step, CHUNK), F32), pltpu.VMEM((rows_step, HEAD_DIM), F32)],
        compiler_params=_cparams(("parallel", "parallel")),
        name="mlstm_latent" if carry else "mlstm_prompt",
    )(*args)


def _pool_kernel(u_ref, a_ref, ic_ref, wpg_ref, ps_ref, yb_ref, *, seq_len, n_seq, rows_2d):
    piece = seq_len if rows_2d is None else GRID_W
    n_piece = n_seq if rows_2d is None else rows_2d
    for g in range(N_GROUPS):
        u = jnp.concatenate([u_ref[g, i * piece:(i + 1) * piece, :] for i in range(n_piece)], axis=1)
        win = _dot(a_ref[g], u)
        if rows_2d is None:
            inv_cnt = jnp.concatenate([ic_ref[g]] * n_piece, axis=1)
        else:
            run = [jnp.zeros((piece, LANES), F32)]
            for r in range(rows_2d):
                run.append(run[-1] + win[:, r * LANES:(r + 1) * LANES])
            half = POOL_WINDOWS[g] // 2
            win = jnp.concatenate([run[min(r - half + POOL_WINDOWS[g], rows_2d)] - run[max(r - half, 0)]
                                   for r in range(rows_2d)], axis=1)
            inv_cnt = jnp.concatenate([ic_ref[g, r * piece:(r + 1) * piece, :] for r in range(rows_2d)], axis=1)
        p = (win * inv_cnt - u.astype(F32)).astype(BF16)
        p = jnp.concatenate([p[:, i * LANES:(i + 1) * LANES] for i in range(n_piece)], axis=0)
        yb_ref[g] = (_dot(p, wpg_ref[g]) * ps_ref[g]).astype(BF16)


def _pool(p3, amat, invc, wpg, pscale, seq_len, n_seq, rows_2d):
    n = p3.shape[1]
    rows = seq_len * n_seq
    return pl.pallas_call(
        functools.partial(_pool_kernel, seq_len=seq_len, n_seq=n_seq, rows_2d=rows_2d),
        out_shape=jax.ShapeDtypeStruct((N_GROUPS, n, LANES), BF16),
        grid=(n // rows,),
        in_specs=[
            pl.BlockSpec((N_GROUPS, rows, LANES), lambda i: (SLAB_POOL // N_GROUPS, i, 0)),
            _const_spec(amat.shape),
            _const_spec((N_GROUPS, seq_len, LANES)),
            _const_spec((N_GROUPS, LANES, LANES)),
            _const_spec((N_GROUPS, 1, LANES)),
        ],
        out_specs=pl.BlockSpec((N_GROUPS, rows, LANES), lambda i: (0, i, 0)),
        compiler_params=_cparams(("parallel",)),
        name="pool",
    )(p3, amat, invc, wpg, pscale)


def _window_matrix(size, w):
    idx = np.arange(size)
    lo = np.clip(idx - w // 2, 0, size)
    hi = np.clip(idx - w // 2 + w, 0, size)
    s = np.arange(size)[None, :]
    a = ((s >= lo[:, None]) & (s < hi[:, None])).astype(np.float32)
    return a, (hi - lo).astype(np.float32)


def _pool_constants(seq_len, rows_2d):
    mats, invs = [], []
    for w in POOL_WINDOWS:
        if rows_2d is None:
            a, cnt = _window_matrix(seq_len, w)
        else:
            _, cr = _window_matrix(rows_2d, w)
            a, cc = _window_matrix(GRID_W, w)
            cnt = np.kron(cr, cc)
        mats.append(a)
        invs.append(np.repeat((1.0 / cnt)[:, None], LANES, axis=1))
    return (jnp.asarray(np.stack(mats), dtype=BF16), jnp.asarray(np.stack(invs), dtype=F32))


MIX_TM = 512


def _mix_kernel(x_ref, mod_ref, ya_ref, yb_ref, ga_ref, gb_ref, wa_ref, wb_ref, wo_ref,
                g2_ref, wr_ref, br_ref, x1_ref, h2_ref, info_ref):
    ya = jnp.concatenate([ya_ref[i] for i in range(N_HEADS)], axis=1)
    yb = jnp.concatenate([yb_ref[i] for i in range(N_GROUPS)], axis=1)
    ga = jnp.concatenate([ga_ref[i] for i in range(8)], axis=1).astype(F32)
    gb = jnp.concatenate([gb_ref[i] for i in range(8)], axis=1).astype(F32)
    merged = jax.nn.sigmoid(ga) * _dot(ya, wa_ref[...]) + jax.nn.sigmoid(gb) * _dot(yb, wb_ref[...])
    out = _dot(merged.astype(BF16), wo_ref[...])
    x1 = x_ref[...] + mod_ref[0, 2:3, :] * out
    x1_ref[...] = x1

    h2 = x1 * _rms_scale(x1) * g2_ref[...]
    h2 = h2 * (1.0 + mod_ref[0, 4:5, :]) + mod_ref[0, 3:4, :]
    h_hi, h_lo = _split_bf16(h2)
    packed = _pack_bf16_pairs(h_hi.astype(F32))
    h2_ref[0:MIX_TM, :] = packed[:, :SC_ROW]
    h2_ref[MIX_TM:2 * MIX_TM, :] = packed[:, SC_ROW:]
    logits = _dot3_packed(h_hi, h_lo, wr_ref[...]) + br_ref[...]
    first, second, w1, w2 = _route_rows(logits.T)
    zero = jnp.zeros_like(w1)
    info_ref[...] = jnp.concatenate([first, second, w1, w2, zero, zero, zero, zero], axis=0)


def _mix(x2d, mod3, mod_row_fn, ya, yb, p3, wa, wb, wo, g_norm2, wr, br):
    n = x2d.shape[0]
    return pl.pallas_call(
        _mix_kernel,
        out_shape=(jax.ShapeDtypeStruct((n, D_MODEL), F32),
                   jax.ShapeDtypeStruct((2 * n, SC_ROW), jnp.int32), jax.ShapeDtypeStruct((8, n), F32)),
        grid=(n // MIX_TM,),
        in_specs=[
            pl.BlockSpec((MIX_TM, D_MODEL), lambda i: (i, 0)),
            pl.BlockSpec((1, 6, D_MODEL), lambda i: (mod_row_fn(i * MIX_TM), 0, 0)),
            pl.BlockSpec((N_HEADS, MIX_TM, LANES), lambda i: (0, i, 0)),
            pl.BlockSpec((N_GROUPS, MIX_TM, LANES), lambda i: (0, i, 0)),
            pl.BlockSpec((8, MIX_TM, LANES), lambda i: (SLAB_GA // 8, i, 0)),
            pl.BlockSpec((8, MIX_TM, LANES), lambda i: (SLAB_GB // 8, i, 0)),
            _const_spec((D_MODEL, D_MODEL)),
            _const_spec((N_GROUPS * LANES, D_MODEL)),
            _const_spec((D_MODEL, D_MODEL)),
            _const_spec((1, D_MODEL)),
            _const_spec((D_MODEL, 2 * LANES)),
            _const_spec((1, LANES)),
        ],
        out_specs=(pl.BlockSpec((MIX_TM, D_MODEL), lambda i: (i, 0)),
                   pl.BlockSpec((2 * MIX_TM, SC_ROW), lambda i: (i, 0)),
                   pl.BlockSpec((8, MIX_TM), lambda i: (0, i))),
        compiler_params=_cparams(("parallel",)),
        name="mix",
    )(x2d, mod3, ya, yb, p3, p3, wa, wb, wo, g_norm2, wr, br)


PLAN_TB = 1024
ROUTER_ROWS = 4 + N_EXPERTS
SLOT_TM = 1024
SC_ROW = 256
SC_WIN = 128
HI_MASK = -65536


def _pack_bf16_pairs(x):
    bits = pltpu.bitcast(x, jnp.int32)
    half = D_MODEL // 2
    return jnp.bitwise_or(jnp.bitwise_and(lax.shift_right_logical(bits[:, :half], 16), 0xFFFF),
                          jnp.bitwise_and(bits[:, half:], HI_MASK))


def _unpack_bf16_pairs(lo_words, hi_words):
    def low(w):
        return pltpu.bitcast(lax.shift_left(w, 16), F32)

    def high(w):
        return pltpu.bitcast(jnp.bitwise_and(w, HI_MASK), F32)

    return jnp.concatenate([low(lo_words), low(hi_words), high(lo_words), high(hi_words)], axis=1).astype(BF16)


def _route_rows(lt):
    lg = [lt[i:i + 1] for i in range(N_GROUPS)]
    mx = jnp.maximum(jnp.maximum(lg[0], lg[1]), jnp.maximum(lg[2], lg[3]))
    p_top = 1.0 / (jnp.exp(lg[0] - mx) + jnp.exp(lg[1] - mx) + jnp.exp(lg[2] - mx) + jnp.exp(lg[3] - mx))
    gidx = jnp.zeros(lg[0].shape, jnp.int32)
    best = lg[0]
    for i in range(1, N_GROUPS):
        upd = lg[i] > best
        gidx = jnp.where(upd, i, gidx)
        best = jnp.where(upd, lg[i], best)

    def expert_row(g, e):
        r = N_GROUPS + g * EXPERTS_PER_GROUP + e
        return lt[r:r + 1]

    le = [jnp.where(gidx == 0, expert_row(0, e),
                    jnp.where(gidx == 1, expert_row(1, e),
                              jnp.where(gidx == 2, expert_row(2, e), expert_row(3, e))))
          for e in range(EXPERTS_PER_GROUP)]
    v1 = jnp.maximum(jnp.maximum(le[0], le[1]), jnp.maximum(le[2], le[3]))
    i1 = jnp.full(v1.shape, EXPERTS_PER_GROUP - 1, jnp.int32)
    for e in range(EXPERTS_PER_GROUP - 2, -1, -1):
        i1 = jnp.where(le[e] == v1, e, i1)
    le2 = [jnp.where(i1 == e, -jnp.inf, le[e]) for e in range(EXPERTS_PER_GROUP)]
    v2 = jnp.maximum(jnp.maximum(le2[0], le2[1]), jnp.maximum(le2[2], le2[3]))
    i2 = jnp.full(v2.shape, EXPERTS_PER_GROUP - 1, jnp.int32)
    for e in range(EXPERTS_PER_GROUP - 2, -1, -1):
        i2 = jnp.where(le2[e] == v2, e, i2)
    e2 = jnp.exp(v2 - v1)
    first = (gidx * EXPERTS_PER_GROUP + i1).astype(F32)
    second = (gidx * EXPERTS_PER_GROUP + i2).astype(F32)
    return first, second, p_top / (1.0 + e2), p_top * e2 / (1.0 + e2)


def _plan_kernel(info_ref, before_ref, sidx_ref, gidx_ref, tmap_ref, rank_scr, count_scr, *, n_blocks, min_tiles,
                 slot_tm):
    step = pl.program_id(0)
    expert_id = lax.broadcasted_iota(jnp.int32, (N_EXPERTS, PLAN_TB), 0).astype(F32)

    @pl.when(step == 0)
    def _():
        count_scr[...] = jnp.zeros_like(count_scr)

    @pl.when(step < n_blocks)
    def _():
        cols = pl.ds(pl.multiple_of(step * PLAN_TB, PLAN_TB), PLAN_TB)
        sel = jnp.where((info_ref[0:1, cols] == expert_id) | (info_ref[1:2, cols] == expert_id), 1.0, 0.0)
        selb = sel.astype(BF16)
        rank_scr[:, cols] = _dot(selb, before_ref[...]) + count_scr[:, 0:1]
        count_scr[...] += _dot(selb, jnp.ones((PLAN_TB, LANES), BF16))

    @pl.when(step == n_blocks)
    def _():
        counts = count_scr[...]
        padded = jnp.floor((counts + (slot_tm - 1)) * (1.0 / slot_tm)) * slot_tm
        padded = jnp.maximum(padded, float(min_tiles * slot_tm))
        starts, ends, run = [], [], jnp.zeros((1, LANES), F32)
        for e in range(N_EXPERTS):
            starts.append(run)
            run = run + padded[e:e + 1]
            ends.append(run)
        n = info_ref.shape[1]
        shift = slot_tm.bit_length() - 1
        for k in range(2):
            chosen = info_ref[k:k + 1, :]
            slot = jnp.zeros((1, n), F32)
            for e in range(N_EXPERTS):
                slot = jnp.where(chosen == float(e), starts[e][:, 0:1] + rank_scr[e:e + 1, :], slot)
            slot = slot.astype(jnp.int32)
            low = jnp.left_shift(jnp.right_shift(slot, shift), shift + 1) + jnp.bitwise_and(slot, slot_tm - 1)
            for half in range(2):
                rows = low + half * slot_tm
                for t in range(n // MIX_TM):
                    sidx_ref[k, t, half:half + 1, :] = rows[:, t * MIX_TM:(t + 1) * MIX_TM]
                for t in range(n // FIN_TM):
                    gidx_ref[t, k, half:half + 1, :] = rows[:, t * FIN_TM:(t + 1) * FIN_TM]
        tile_start = lax.broadcasted_iota(jnp.int32, (1, LANES), 1).astype(F32) * slot_tm
        used = jnp.where(tile_start < run, 1.0, 0.0)
        tile_start = jnp.minimum(tile_start, run - slot_tm)
        owner = jnp.zeros((1, LANES), F32)
        for e in range(N_EXPERTS - 1):
            owner = owner + jnp.where(ends[e] <= tile_start, 1.0, 0.0)
        tile = tile_start * (1.0 / slot_tm)
        tmap_ref[...] = jnp.concatenate([owner, used, tile] + [jnp.zeros((1, LANES), F32)] * 5,
                                        axis=0).astype(jnp.int32)


def _plan(info, min_tiles, slot_tm):
    n = info.shape[1]
    n_blocks = n // PLAN_TB
    t_i = lax.broadcasted_iota(jnp.int32, (PLAN_TB, PLAN_TB), 0)
    t_j = lax.broadcasted_iota(jnp.int32, (PLAN_TB, PLAN_TB), 1)
    before = (t_i < t_j).astype(BF16)
    return pl.pallas_call(
        functools.partial(_plan_kernel, n_blocks=n_blocks, min_tiles=min_tiles, slot_tm=slot_tm),
        out_shape=(jax.ShapeDtypeStruct((2, n // MIX_TM, 2, MIX_TM), jnp.int32),
                   jax.ShapeDtypeStruct((n // FIN_TM, 2, 2, FIN_TM), jnp.int32),
                   jax.ShapeDtypeStruct((8, LANES), jnp.int32)),
        grid=(n_blocks + 1,),
        in_specs=[_const_spec((8, n)), _const_spec((PLAN_TB, PLAN_TB))],
        out_specs=(pl.BlockSpec((2, n // MIX_TM, 2, MIX_TM), lambda i: (0, 0, 0, 0)),
                   pl.BlockSpec((n // FIN_TM, 2, 2, FIN_TM), lambda i: (0, 0, 0, 0)),
                   pl.BlockSpec((8, LANES), lambda i: (0, 0))),
        scratch_shapes=[pltpu.VMEM((N_EXPERTS, n), F32), pltpu.VMEM((N_EXPERTS, LANES), F32)],
        compiler_params=_cparams(("arbitrary",)),
        name="plan",
    )(info, before)


def _sc_mesh():
    return plsc.VectorSubcoreMesh(core_axis_name="core", subcore_axis_name="subcore")


def _sc_scatter_rows(x, idx, n_out):
    n_idx = idx.shape[0]
    src_blocks = x.shape[0] // SC_WIN

    @pl.kernel(out_type=jax.ShapeDtypeStruct((n_out, SC_ROW), x.dtype), mesh=_sc_mesh(), scratch_types=[])
    def scatter_kernel(x_hbm, i_hbm, o_hbm):
        def body(x_vmem, i_vmem):
            pltpu.sync_copy(x_vmem, o_hbm.at[i_vmem.at[0]])

        pltpu.emit_pipeline(
            body,
            grid=(n_idx // SC_WIN,),
            in_specs=[pl.BlockSpec((SC_WIN, SC_ROW), index_map=lambda i: (i % src_blocks, 0)),
                      pl.BlockSpec((1, SC_WIN), index_map=lambda i: (0, i))],
            out_specs=[],
            core_axis_name=("core", "subcore"),
            dimension_semantics=(pltpu.PARALLEL,),
        )(x_hbm, i_hbm)

    return scatter_kernel(x, idx.reshape(1, n_idx))


def _sc_gather_rows(x, idx):
    n_idx = idx.shape[0]

    @pl.kernel(out_type=jax.ShapeDtypeStruct((n_idx, SC_ROW), x.dtype), mesh=_sc_mesh(), scratch_types=[])
    def gather_kernel(x_hbm, i_hbm, o_hbm):
        def body(i_vmem, o_vmem):
            pltpu.sync_copy(x_hbm.at[i_vmem.at[0]], o_vmem)

        pltpu.emit_pipeline(
            body,
            grid=(n_idx // SC_WIN,),
            in_specs=[pl.BlockSpec((1, SC_WIN), index_map=lambda i: (0, i))],
            out_specs=[pl.BlockSpec((SC_WIN, SC_ROW), index_map=lambda i: (i, 0))],
            core_axis_name=("core", "subcore"),
            dimension_semantics=(pltpu.PARALLEL,),
        )(i_hbm, o_hbm)

    return gather_kernel(x, idx.reshape(1, n_idx))


def _experts_kernel(tmap_ref, xs_ref, wg_ref, wu_ref, wd_ref, ys_ref, *wb_refs, cast_weights, slot_tm):
    j = pl.program_id(0)
    if cast_weights:
        wg_b, wu_b, wd_b = wb_refs

        @pl.when((j == 0) | (tmap_ref[0, j] != tmap_ref[0, jnp.maximum(j - 1, 0)]))
        def _():
            wg_b[0] = wg_ref[0].astype(BF16)
            wu_b[0] = wu_ref[0].astype(BF16)
            wd_b[0] = wd_ref[0].astype(BF16)
    else:
        wg_b, wu_b, wd_b = wg_ref, wu_ref, wd_ref

    @pl.when(tmap_ref[1, j] > 0)
    def _():
        x = _unpack_bf16_pairs(xs_ref[0:slot_tm, :], xs_ref[slot_tm:2 * slot_tm, :])
        a = _dot(x, wg_b[0])
        u = _dot(x, wu_b[0])
        act = (a * jax.nn.sigmoid(a) * u).astype(BF16)
        y = _dot(act, wd_b[0]).astype(BF16).astype(F32)
        packed = _pack_bf16_pairs(y)
        ys_ref[0:slot_tm, :] = packed[:, :SC_ROW]
        ys_ref[slot_tm:2 * slot_tm, :] = packed[:, SC_ROW:]


def _experts(xs, tmap, wg, wu, wd, slot_tm):
    n_tiles = xs.shape[0] // (2 * slot_tm)
    cast_weights = wg.dtype != BF16
    tile_spec = pl.BlockSpec((2 * slot_tm, SC_ROW), lambda j, tm: (tm[2, j], 0))
    w_specs = [pl.BlockSpec((1, D_MODEL, D_EXPERT), lambda j, tm: (tm[0, j], 0, 0)),
               pl.BlockSpec((1, D_MODEL, D_EXPERT), lambda j, tm: (tm[0, j], 0, 0)),
               pl.BlockSpec((1, D_EXPERT, D_MODEL), lambda j, tm: (tm[0, j], 0, 0))]
    out_shape = [jax.ShapeDtypeStruct(xs.shape, jnp.int32)]
    out_specs = [tile_spec]
    if cast_weights:
        out_shape += [jax.ShapeDtypeStruct(w.shape, BF16) for w in (wg, wu, wd)]
        out_specs += w_specs
    grid_spec = pltpu.PrefetchScalarGridSpec(
        num_scalar_prefetch=1,
        grid=(n_tiles,),
        in_specs=[tile_spec] + w_specs,
        out_specs=out_specs,
    )
    out = pl.pallas_call(
        functools.partial(_experts_kernel, cast_weights=cast_weights, slot_tm=slot_tm),
        out_shape=out_shape,
        grid_spec=grid_spec,
        compiler_params=_cparams(("arbitrary",)),
        name="experts",
    )(tmap, xs, wg, wu, wd)
    return out[0], tuple(out[1:])


FIN_TM = 1024


def _final_kernel(x1_ref, mod_ref, yg_ref, info_ref, gf_ref, y_ref):
    w_cols = info_ref[...].T
    moe = None
    for k in range(2):
        base = 2 * k * FIN_TM
        yk = _unpack_bf16_pairs(yg_ref[base:base + FIN_TM, :], yg_ref[base + FIN_TM:base + 2 * FIN_TM, :])
        term = w_cols[:, 2 + k:3 + k] * yk.astype(F32)
        moe = term if moe is None else moe + term
    x2 = x1_ref[...] + mod_ref[0, 5:6, :] * moe
    y_ref[...] = x2 * _rms_scale(x2) * gf_ref[...]


def _final(x1, mod3, mod_row_fn, yg, info, g_final):
    n = x1.shape[0]
    return pl.pallas_call(
        _final_kernel,
        out_shape=jax.ShapeDtypeStruct((n, D_MODEL), F32),
        grid=(n // FIN_TM,),
        in_specs=[
            pl.BlockSpec((FIN_TM, D_MODEL), lambda i: (i, 0)),
            pl.BlockSpec((1, 6, D_MODEL), lambda i: (mod_row_fn(i * FIN_TM), 0, 0)),
            pl.BlockSpec((4 * FIN_TM, SC_ROW), lambda i: (i, 0)),
            pl.BlockSpec((8, FIN_TM), lambda i: (0, i)),
            _const_spec((1, D_MODEL)),
        ],
        out_specs=pl.BlockSpec((FIN_TM, D_MODEL), lambda i: (i, 0)),
        compiler_params=_cparams(("parallel",)),
        name="final",
    )(x1, mod3, yg, info, g_final)


def _moe(x1, h2p, info, mod3, mod_row_fn, wg, wu, wd, g_final):
    n = x1.shape[0]
    slot_tm = min(SLOT_TM, 2 * n // N_EXPERTS)
    sidx, gidx, tmap = _plan(info, min_tiles=int(wg.dtype != BF16), slot_tm=slot_tm)
    n_slots = 2 * n + N_EXPERTS * slot_tm
    xs = _sc_scatter_rows(h2p, sidx.reshape(-1), 2 * n_slots)
    ys, w_bf16 = _experts(xs, tmap, wg, wu, wd, slot_tm)
    yg = _sc_gather_rows(ys, gidx.reshape(-1))
    return _final(x1, mod3, mod_row_fn, yg, info, g_final), w_bf16


def _stream(x, mod3, mod_row_fn, seq_len, rows_2d, state, w):
    bsz, t, _ = x.shape
    n = bsz * t
    x2d = x.reshape(n, D_MODEL)
    p3, bcum, imb, amax = _in_projection(x2d, mod3, mod_row_fn, w["g_norm1"], w["wmain"], w["bmain"],
                                         w["colscale"], w["w_gates"], w["b_gates"])
    m0 = None if state is None else state[2]
    cols, rows, cs = _gate_prep(bcum, imb, amax, m0)
    ml = _mlstm(p3, cols, rows, cs, w["gn3"], None if state is None else state[:2])
    amat, invc = _pool_constants(seq_len, rows_2d)
    n_seq = max(1, 1024 // seq_len)
    yb = _pool(p3, amat, invc, w["wpg"], w["pscale"], seq_len, n_seq, rows_2d)
    ya = ml if state is not None else ml[0]
    x1, h2p, info = _mix(x2d, mod3, mod_row_fn, ya, yb, p3, w["wa"], w["wb"], w["wo"],
                         w["g_norm2"], w["wr"], w["br"])
    y, w_bf16 = _moe(x1, h2p, info, mod3, mod_row_fn, w["wg"], w["wu"], w["wd"], w["g_final"])
    return y.reshape(bsz, t, D_MODEL), ml, cs, w_bf16


def kernel(x_prompt, x_sample, state_C, state_n, state_m, c, c_ctx, w_mod, b_mod, g_norm1, w_in, b_in, gn_gain, w_pool_grp, pool_scale, w_proj_a, w_proj_b, w_out, g_norm2, w_router_grp, b_router_grp, w_router_exp, b_router_exp, w_exp_gate, w_exp_up, w_exp_down, g_final):
    bp, tp, _ = x_prompt.shape
    bs, ts, _ = x_sample.shape
    assert w_mod.shape[0] == 1, "single trunk layer"
    assert tp == CHUNK and ts == ML_ROWS and bs % ML_SEQS == 0

    cvec = jnp.zeros((8, D_MODEL), F32).at[0].set(c_ctx).at[1:1 + bs].set(c)
    mod3 = _modulation(cvec, w_mod[0], b_mod[0])

    wi, bi = w_in[0], b_in[0]
    d = D_MODEL
    o_q, o_k, o_v, o_o, o_g, o_p, o_ga, o_gb = 0, d, 2 * d, 3 * d, 4 * d, 4 * d + 32, 4 * d + 32 + 512, 4 * d + 32 + 512 + d
    order = [(o_q, d), (o_k, d), (o_v, d), (o_o, d), (o_ga, d), (o_gb, d), (o_p, 512)]
    wmain = _pack_columns(wi, order)
    bmain = jnp.concatenate([bi[s:s + l] for s, l in order]).reshape(1, D_MAIN)
    colscale = jnp.ones((D_MAIN,), F32).at[SLAB_K * LANES:(SLAB_K + N_HEADS) * LANES].set(HEAD_DIM ** -0.5)
    colscale = colscale.reshape(1, D_MAIN)
    gwt = _take_columns_t(wi, o_g, 32)
    gb = bi[o_g:o_g + 32]
    i_f, f_f, i_b, f_b = (slice(0, 8), slice(8, 16), slice(16, 24), slice(24, 32))
    pad_w = jnp.zeros((LANES - 2 * N_HD, d), F32)
    pad_b = jnp.zeros((LANES - 2 * N_HD,), F32)
    wg = jnp.concatenate([gwt[f_f], gwt[f_b], gwt[i_f], gwt[i_b], pad_w], axis=0).T
    bg = jnp.concatenate([gb[f_f], gb[f_b], gb[i_f], gb[i_b], pad_b]).reshape(1, LANES)
    wg = jnp.concatenate(_split_bf16(wg), axis=1)

    wr = jnp.concatenate([w_router_grp[0], w_router_exp[0].reshape(d, N_EXPERTS),
                          jnp.zeros((d, LANES - ROUTER_ROWS), F32)], axis=1)
    br = jnp.concatenate([b_router_grp[0], b_router_exp[0].reshape(N_EXPERTS),
                          jnp.zeros((LANES - ROUTER_ROWS,), F32)]).reshape(1, LANES)
    wr = jnp.concatenate(_split_bf16(wr), axis=1)

    w = dict(
        g_norm1=g_norm1[0].reshape(1, d), wmain=wmain, bmain=bmain, colscale=colscale,
        w_gates=wg, b_gates=bg,
        gn3=gn_gain[0].reshape(N_HEADS, 1, HEAD_DIM),
        wpg=w_pool_grp[0].astype(BF16), pscale=pool_scale[0].reshape(N_GROUPS, 1, LANES),
        wa=w_proj_a[0].astype(BF16), wb=w_proj_b[0].astype(BF16), wo=w_out[0].astype(BF16),
        g_norm2=g_norm2[0].reshape(1, d), wr=wr, br=br,
        wg=w_exp_gate[0], wu=w_exp_up[0], wd=w_exp_down[0],
        g_final=g_final.reshape(1, d),
    )

    y_prompt, (_, cst, nst), cs_prompt, w_bf16 = _stream(x_prompt, mod3, lambda r: 0, tp, None, None, w)
    w = dict(w, wg=w_bf16[0], wu=w_bf16[1], wd=w_bf16[2])
    n0 = state_n[:, 0].reshape(bs, 2, N_HEADS, 1, HEAD_DIM)
    m0 = jnp.pad(state_m[:, 0].reshape(bs, N_HD), ((0, 0), (0, LANES - N_HD))).reshape(bs, 1, LANES)
    y_sample, _, _, _ = _stream(x_sample, mod3, lambda r: 1 + r // ts, ts, ts // GRID_W, (state_C, n0, m0), w)

    new_c = cst
    new_n = nst.reshape(bp, 1, 2, N_HEADS, HEAD_DIM)
    new_m = cs_prompt[:, 1, :N_HD].reshape(bp, 1, 2, N_HEADS)
    return (y_prompt, y_sample, new_c, new_n, new_m)
```

```python
import functools

import numpy as np
import jax
import jax.numpy as jnp
from jax import lax
from jax.experimental import pallas as pl
from jax.experimental.pallas import tpu as pltpu
from jax.experimental.pallas import tpu_sc as plsc

F32 = jnp.float32
BF16 = jnp.bfloat16

D_MODEL = 1024
N_HEADS = 8
HEAD_DIM = 128
GRID_W = 64
POOL_WINDOWS = (2, 4, 8, 16)
N_GROUPS = 4
EXPERTS_PER_GROUP = 4
N_EXPERTS = 16
D_EXPERT = 512
EPS = 1e-6
NEG = -1e30

LANES = 128
CHUNK = 256
N_SLABS = 52
SLAB_Q, SLAB_K, SLAB_V, SLAB_O, SLAB_GA, SLAB_GB, SLAB_POOL = 0, 8, 16, 24, 32, 40, 48
D_MAIN = N_SLABS * LANES
VMEM_LIMIT = 56 * 1024 * 1024


def _cparams(sem):
    return pltpu.CompilerParams(dimension_semantics=sem, vmem_limit_bytes=VMEM_LIMIT)


def _const_spec(shape):
    nd = len(shape)
    return pl.BlockSpec(shape, lambda *_: (0,) * nd, pipeline_mode=pl.Buffered(1))


def _split_bf16(x):
    hi = x.astype(BF16)
    lo = (x - hi.astype(F32)).astype(BF16)
    return hi, lo


def _dot(a, b):
    return jnp.dot(a, b, preferred_element_type=F32)


def _dot3(a, b_hi, b_lo):
    a_hi, a_lo = _split_bf16(a)
    return _dot(a_hi, b_hi) + _dot(a_hi, b_lo) + _dot(a_lo, b_hi)


def _dot3_packed(a_hi, a_lo, b_hilo):
    first = _dot(a_hi, b_hilo)
    return first[:, :LANES] + first[:, LANES:] + _dot(a_lo, b_hilo[:, :LANES])


def _rms_scale(x):
    return lax.rsqrt(jnp.mean(x * x, axis=-1, keepdims=True) + EPS)


def _log_sigmoid(x):
    return jnp.minimum(x, 0.0) - jnp.log1p(jnp.exp(-jnp.abs(x)))


def _mod_kernel(c_ref, w_ref, b_ref, o_ref):
    c = c_ref[...]
    s = c * jax.nn.sigmoid(c)
    w_hi, w_lo = _split_bf16(w_ref[...])
    o_ref[...] = _dot3(s, w_hi, w_lo) + b_ref[...]


def _modulation(cvec, w_mod, b_mod):
    tn = 2048
    out = pl.pallas_call(
        _mod_kernel,
        out_shape=jax.ShapeDtypeStruct((8, 6 * D_MODEL), F32),
        grid=(6 * D_MODEL // tn,),
        in_specs=[
            pl.BlockSpec((8, D_MODEL), lambda j: (0, 0)),
            pl.BlockSpec((D_MODEL, tn), lambda j: (0, j)),
            pl.BlockSpec((1, tn), lambda j: (0, j)),
        ],
        out_specs=pl.BlockSpec((8, tn), lambda j: (0, j)),
        compiler_params=_cparams(("parallel",)),
        name="modulation",
    )(cvec, w_mod, b_mod.reshape(1, -1))
    return out.reshape(8, 6, D_MODEL)


PACK_COLS = 512


def _pack_kernel(wt_ref, o_ref):
    o_ref[...] = wt_ref[...].T.astype(BF16)


def _pack_columns(w, order):
    k, n = w.shape
    starts = []
    for src, width in order:
        assert width % PACK_COLS == 0
        starts += [src + j for j in range(0, width, PACK_COLS)]
    n_out = len(starts) * PACK_COLS

    unit = 32
    assert all(s % unit == 0 for s in starts)

    def src_row(j):
        row = jnp.int32(starts[-1] // unit)
        for idx in range(len(starts) - 2, -1, -1):
            row = jnp.where(j == idx, starts[idx] // unit, row)
        return row * unit

    return pl.pallas_call(
        _pack_kernel,
        out_shape=jax.ShapeDtypeStruct((k, n_out), BF16),
        grid=(len(starts),),
        in_specs=[pl.BlockSpec((pl.Element(PACK_COLS), pl.Element(k)), lambda j: (src_row(j), 0))],
        out_specs=pl.BlockSpec((k, PACK_COLS), lambda j: (0, j)),
        compiler_params=_cparams(("parallel",)),
        name="pack_columns",
    )(w.T)


def _copy_kernel(w_ref, o_ref):
    o_ref[...] = w_ref[...]


def _take_columns_t(w, start, width):
    k = w.shape[0]
    return pl.pallas_call(
        _copy_kernel,
        out_shape=jax.ShapeDtypeStruct((width, k), w.dtype),
        grid=(1,),
        in_specs=[pl.BlockSpec((pl.Element(width), pl.Element(k)), lambda j: (start, 0))],
        out_specs=pl.BlockSpec((width, k), lambda j: (0, 0)),
        name="take_columns",
    )(w.T)


IN_TM = 512
IN_TN = 512
LOG2E = 1.4426950408889634
N_HD = 2 * N_HEADS


def _fwd_lanes(shape):
    lane = lax.broadcasted_iota(jnp.int32, shape, len(shape) - 1)
    return (lane & N_HEADS) == 0


def _inproj_kernel(x_ref, mod_ref, g_ref, w_ref, b_ref, cs_ref, wg_ref, bg_ref,
                   p3_ref, bcum_ref, imb_ref, amax_ref):
    x = x_ref[...]
    h = x * _rms_scale(x) * g_ref[...]
    h = h * (1.0 + mod_ref[0, 1:2, :]) + mod_ref[0, 0:1, :]
    hb = h.astype(BF16)
    for n in range(D_MAIN // IN_TN):
        sl = slice(n * IN_TN, (n + 1) * IN_TN)
        acc = (_dot(hb, w_ref[:, sl]) + b_ref[:, sl]) * cs_ref[:, sl]
        accb = acc.astype(BF16)
        for j in range(IN_TN // LANES):
            p3_ref[n * (IN_TN // LANES) + j] = accb[:, j * LANES:(j + 1) * LANES]

    h_lo = (h - hb.astype(F32)).astype(BF16)
    gates = _dot3_packed(hb, h_lo, wg_ref[...]) + bg_ref[...]
    ls = _log_sigmoid(gates)
    gi = pltpu.roll(gates, LANES - N_HD, 1)
    row = lax.broadcasted_iota(jnp.int32, (CHUNK, LANES), 0)
    fwd = _fwd_lanes((CHUNK, LANES))
    for c in range(IN_TM // CHUNK):
        rows = slice(c * CHUNK, (c + 1) * CHUNK)
        cf = ls[rows]
        cb = cf
        s = 1
        while s < CHUNK:
            cf = cf + jnp.where(row >= s, pltpu.roll(cf, s, 0), 0.0)
            cb = cb + jnp.where(row < CHUNK - s, pltpu.roll(cb, CHUNK - s, 0), 0.0)
            s *= 2
        b = jnp.where(fwd, cf, cb)
        imb = gi[rows] - b
        mf = imb
        mb = imb
        s = 1
        while s < CHUNK:
            mf = jnp.maximum(mf, jnp.where(row >= s, pltpu.roll(mf, s, 0), -jnp.inf))
            mb = jnp.maximum(mb, jnp.where(row < CHUNK - s, pltpu.roll(mb, CHUNK - s, 0), -jnp.inf))
            s *= 2
        bcum_ref[rows, :] = b
        imb_ref[rows, :] = imb
        amax_ref[rows, :] = b + jnp.where(fwd, mf, mb)


def _in_projection(x2d, mod3, mod_row_fn, g_norm1, wmain, bmain, colscale, wg, bg):
    n = x2d.shape[0]
    return pl.pallas_call(
        _inproj_kernel,
        out_shape=(jax.ShapeDtypeStruct((N_SLABS, n, LANES), BF16),) + (jax.ShapeDtypeStruct((n, LANES), F32),) * 3,
        grid=(n // IN_TM,),
        in_specs=[
            pl.BlockSpec((IN_TM, D_MODEL), lambda i: (i, 0)),
            pl.BlockSpec((1, 6, D_MODEL), lambda i: (mod_row_fn(i * IN_TM), 0, 0)),
            _const_spec((1, D_MODEL)),
            _const_spec((D_MODEL, D_MAIN)),
            _const_spec((1, D_MAIN)),
            _const_spec((1, D_MAIN)),
            _const_spec((D_MODEL, 2 * LANES)),
            _const_spec((1, LANES)),
        ],
        out_specs=(pl.BlockSpec((N_SLABS, IN_TM, LANES), lambda i: (0, i, 0)),)
        + (pl.BlockSpec((IN_TM, LANES), lambda i: (i, 0)),) * 3,
        compiler_params=_cparams(("parallel",)),
        name="in_projection",
    )(x2d, mod3, g_norm1, wmain, bmain, colscale, wg, bg)


GP_ROWS = 1024
GP_NCH = GP_ROWS // CHUNK
Q_C2, Q_EM, Q_WI, Q_WC = 0, 1, 2, 3


def _gate_prep_kernel(*refs, carry):
    if carry:
        b_ref, imb_ref, a_ref, m0_ref, cols_ref, rw_ref, cs_ref = refs
    else:
        b_ref, imb_ref, a_ref, cols_ref, rw_ref, cs_ref = refs
    fwd1 = _fwd_lanes((1, LANES))
    lane = lax.broadcasted_iota(jnp.int32, (CHUNK, LANES), 1)
    neg = jnp.full((1, LANES), NEG, F32)

    def at_scan_end(ref, c):
        return jnp.where(fwd1, ref[(c + 1) * CHUNK - 1:(c + 1) * CHUNK, :], ref[c * CHUNK:c * CHUNK + 1, :])

    b_last = [at_scan_end(b_ref, c) for c in range(GP_NCH)]
    a_last = [at_scan_end(a_ref, c) for c in range(GP_NCH)]
    m_start = m0_ref[0] if carry else neg
    m_prev = [[None] * GP_NCH, [None] * GP_NCH]
    m_new = [[None] * GP_NCH, [None] * GP_NCH]
    for d, order in enumerate((range(GP_NCH), range(GP_NCH - 1, -1, -1))):
        m = m_start
        for c in order:
            m_prev[d][c] = m
            m_new[d][c] = jnp.maximum(b_last[c] + m, a_last[c])
            m = m_new[d][c] if carry else neg

    row_src = []
    for c in range(GP_NCH):
        rows = slice(c * CHUNK, (c + 1) * CHUNK)
        mp = jnp.where(fwd1, m_prev[0][c], m_prev[1][c])
        mn = jnp.where(fwd1, m_new[0][c], m_new[1][c])
        b = b_ref[rows, :]
        imb = imb_ref[rows, :]
        inter = b + mp
        m_t = jnp.maximum(inter, a_ref[rows, :])
        c2 = (b - m_t) * LOG2E
        em = jnp.exp(-m_t)
        wc = jnp.exp(b_last[c] - mn + imb)
        packed = jnp.where(lane < N_HD, c2, pltpu.roll(em, N_HD * Q_EM, 1))
        if carry:
            wi = jnp.exp(inter - m_t)
            packed = jnp.where(lane < N_HD * Q_WI, packed, pltpu.roll(wi, N_HD * Q_WI, 1))
        cols_ref[rows, :] = jnp.where(lane < N_HD * Q_WC, packed, pltpu.roll(wc, N_HD * Q_WC, 1))
        row_src.append(jnp.where(lane < N_HD, imb * (-LOG2E), pltpu.roll(wc, N_HD, 1)))
        cs_ref[c, 0:1, :] = jnp.exp(b_last[c] + mp - mn)
        cs_ref[c, 1:2, :] = mn

    rows_t = jnp.concatenate(row_src, axis=0).T
    for r in range(2 * N_HD):
        rw_ref[r] = rows_t[r:r + 1]


def _gate_prep(bcum, imb, amax, m0=None):
    n = bcum.shape[0]
    carry = m0 is not None
    blk = pl.BlockSpec((GP_ROWS, LANES), lambda i: (i, 0))
    in_specs = [blk, blk, blk]
    args = [bcum, imb, amax]
    if carry:
        in_specs.append(pl.BlockSpec((1, 1, LANES), lambda i: (i, 0, 0)))
        args.append(m0)
    return pl.pallas_call(
        functools.partial(_gate_prep_kernel, carry=carry),
        out_shape=(jax.ShapeDtypeStruct((n, LANES), F32), jax.ShapeDtypeStruct((2 * N_HD, 1, n), F32),
                   jax.ShapeDtypeStruct((n // CHUNK, 2, LANES), F32)),
        grid=(n // GP_ROWS,),
        in_specs=in_specs,
        out_specs=(blk, pl.BlockSpec((2 * N_HD, 1, GP_ROWS), lambda i: (0, 0, i)),
                   pl.BlockSpec((GP_NCH, 2, LANES), lambda i: (i, 0, 0))),
        compiler_params=_cparams(("parallel",)),
        name="gate_prep",
    )(*args)


ML_ROWS = 1024
ML_SEQS = 4
ML_ROWS_FREE = 4096


def _pick_col(blk, lane_iota, idx):
    return jnp.sum(jnp.where(lane_iota == idx, blk, 0.0), axis=1, keepdims=True)


def _mlstm_kernel(*refs, carry, n_chunks, n_seqs):
    if carry:
        (q_ref, k_ref, v_ref, o_ref, cols_ref, r2f_ref, r2b_ref, gn_ref, cs_ref, c0_ref, n0_ref,
         ya_ref, s_scr, h_scr) = refs
    else:
        (q_ref, k_ref, v_ref, o_ref, cols_ref, r2f_ref, r2b_ref, gn_ref, wrf_ref, wrb_ref,
         ya_ref, cst_ref, nst_ref, s_scr, h_scr) = refs
    head = pl.program_id(1)
    lane = lax.broadcasted_iota(jnp.int32, (CHUNK, LANES), 1)
    lane1 = lax.broadcasted_iota(jnp.int32, (1, LANES), 1)
    r_idx = lax.broadcasted_iota(jnp.int32, (CHUNK, CHUNK), 0)
    c_idx = lax.broadcasted_iota(jnp.int32, (CHUNK, CHUNK), 1)
    ones = jnp.ones((CHUNK, HEAD_DIM), BF16)
    tn_dims = (((0,), (0,)), ((), ()))

    for d in range(2):
        mask = (c_idx <= r_idx) if d == 0 else (c_idx >= r_idx)
        r2_ref = r2f_ref if d == 0 else r2b_ref
        base = N_HEADS * d + head
        per_seq = n_chunks // n_seqs
        local = range(per_seq) if d == 0 else range(per_seq - 1, -1, -1)
        for sq, step, c in [(sq, step, sq * per_seq + lc) for sq in range(n_seqs) for step, lc in enumerate(local)]:
            if carry and step == 0:
                n_rep = jnp.broadcast_to(n0_ref[sq, d, 0], (HEAD_DIM, HEAD_DIM)).T
                cn = jnp.concatenate([c0_ref[sq, 0, d, 0], n_rep], axis=1)
            rows = slice(c * CHUNK, (c + 1) * CHUNK)
            q = q_ref[0, rows, :]
            k = k_ref[0, rows, :]
            v = v_ref[0, rows, :]
            v1 = jnp.concatenate([v, ones], axis=1)
            if d == 0:
                s = lax.dot_general(q, k, (((1,), (1,)), ((), ())), preferred_element_type=F32)
                s_scr[rows, :] = s
            else:
                s = s_scr[rows, :]
            blk = cols_ref[rows, :]
            arg = _pick_col(blk, lane, base + N_HD * Q_C2) - r2_ref[0, :, rows]
            sw = s * jnp.exp2(jnp.where(mask, arg, NEG))
            nd = _dot(sw.astype(BF16), v1)
            if carry:
                nd = nd + _pick_col(blk, lane, base + N_HD * Q_WI) * _dot(q, cn.astype(BF16))
            hdir = nd[:, :HEAD_DIM] / jnp.maximum(jnp.abs(nd[:, HEAD_DIM:]), _pick_col(blk, lane, base + N_HD * Q_EM))
            if d == 0:
                h_scr[rows, :] = hdir
            else:
                h_scr[rows, :] = h_scr[rows, :] + hdir

            if carry and step == per_seq - 1:
                continue
            w_col = _pick_col(blk, lane, base + N_HD * Q_WC)
            if carry:
                u = lax.dot_general(k, (w_col * v1.astype(F32)).astype(BF16), tn_dims,
                                    preferred_element_type=F32)
                decay = jnp.sum(jnp.where(lane1 == base, cs_ref[c, 0:1, :], 0.0), axis=1, keepdims=True)
                cn = decay * cn + u
            else:
                w_row = (wrf_ref if d == 0 else wrb_ref)[0, :, rows]
                w8 = jnp.broadcast_to(w_row, (8, CHUNK)).astype(BF16)
                cst_ref[c, 0, d, 0] = lax.dot_general(k, (w_col * v.astype(F32)).astype(BF16), tn_dims,
                                                      preferred_element_type=F32)
                nst_ref[c, d, 0] = _dot(w8, k)[0:1, :]

    hh = h_scr[...]
    hn = hh * _rms_scale(hh) * gn_ref[0]
    ya_ref[0] = (hn * jax.nn.sigmoid(o_ref[0].astype(F32))).astype(BF16)


def _mlstm(p3, cols, rows, cs, gn3, state=None):
    n = p3.shape[1]
    carry = state is not None
    n_seqs = ML_SEQS if carry else 1
    rows_step = ML_ROWS * ML_SEQS if carry else ML_ROWS_FREE
    n_chunks = rows_step // CHUNK
    nblk = n // rows_step

    def slab(base):
        return pl.BlockSpec((1, rows_step, LANES), lambda g, h: (base + h, g, 0))

    def row(base):
        return pl.BlockSpec((1, 1, rows_step), lambda g, h: (base + h, 0, g))

    in_specs = [
        slab(SLAB_Q), slab(SLAB_K), slab(SLAB_V), slab(SLAB_O),
        pl.BlockSpec((rows_step, LANES), lambda g, h: (g, 0)),
        row(0), row(N_HEADS),
        pl.BlockSpec((1, 1, LANES), lambda g, h: (h, 0, 0)),
    ]
    args = [p3, p3, p3, p3, cols, rows, rows, gn3]
    ya_shape = jax.ShapeDtypeStruct((N_HEADS, n, LANES), BF16)
    ya_spec = pl.BlockSpec((1, rows_step, LANES), lambda g, h: (h, g, 0))
    state_spec = pl.BlockSpec((n_seqs, 1, 2, 1, HEAD_DIM, HEAD_DIM), lambda g, h: (g, 0, 0, h, 0, 0))
    if carry:
        c0, n0 = state
        in_specs += [pl.BlockSpec((n_chunks, 2, LANES), lambda g, h: (g, 0, 0)), state_spec,
                     pl.BlockSpec((n_seqs, 2, 1, 1, HEAD_DIM), lambda g, h: (g, 0, h, 0, 0))]
        args += [cs, c0, n0]
        out_shape = ya_shape
        out_specs = ya_spec
    else:
        in_specs += [row(N_HD), row(N_HD + N_HEADS)]
        args += [rows, rows]
        nseq = n // CHUNK
        out_shape = (ya_shape,
                     jax.ShapeDtypeStruct((nseq, 1, 2, N_HEADS, HEAD_DIM, HEAD_DIM), F32),
                     jax.ShapeDtypeStruct((nseq, 2, N_HEADS, 1, HEAD_DIM), F32))
        out_specs = (ya_spec,
                     pl.BlockSpec((n_chunks, 1, 2, 1, HEAD_DIM, HEAD_DIM), lambda g, h: (g, 0, 0, h, 0, 0)),
                     pl.BlockSpec((n_chunks, 2, 1, 1, HEAD_DIM), lambda g, h: (g, 0, h, 0, 0)))
    return pl.pallas_call(
        functools.partial(_mlstm_kernel, carry=carry, n_chunks=n_chunks, n_seqs=n_seqs),
        out_shape=out_shape,
        grid=(nblk, N_HEADS),
        in_specs=in_specs,
        out_specs=out_specs,
        scratch_shapes=[pltpu.VMEM((rows_step, CHUNK), F32), pltpu.VMEM((rows_step, HEAD_DIM), F32)],
        compiler_params=_cparams(("parallel", "parallel")),
        name="mlstm_latent" if carry else "mlstm_prompt",
    )(*args)


def _pool_kernel(u_ref, a_ref, ic_ref, wpg_ref, ps_ref, yb_ref, *, seq_len, n_seq, rows_2d):
    piece = seq_len if rows_2d is None else GRID_W
    n_piece = n_seq if rows_2d is None else rows_2d
    for g in range(N_GROUPS):
        u = jnp.concatenate([u_ref[g, i * piece:(i + 1) * piece, :] for i in range(n_piece)], axis=1)
        win = _dot(a_ref[g], u)
        if rows_2d is None:
            inv_cnt = jnp.concatenate([ic_ref[g]] * n_piece, axis=1)
        else:
            run = [jnp.zeros((piece, LANES), F32)]
            for r in range(rows_2d):
                run.append(run[-1] + win[:, r * LANES:(r + 1) * LANES])
            half = POOL_WINDOWS[g] // 2
            win = jnp.concatenate([run[min(r - half + POOL_WINDOWS[g], rows_2d)] - run[max(r - half, 0)]
                                   for r in range(rows_2d)], axis=1)
            inv_cnt = jnp.concatenate([ic_ref[g, r * piece:(r + 1) * piece, :] for r in range(rows_2d)], axis=1)
        p = (win * inv_cnt - u.astype(F32)).astype(BF16)
        p = jnp.concatenate([p[:, i * LANES:(i + 1) * LANES] for i in range(n_piece)], axis=0)
        yb_ref[g] = (_dot(p, wpg_ref[g]) * ps_ref[g]).astype(BF16)


def _pool(p3, amat, invc, wpg, pscale, seq_len, n_seq, rows_2d):
    n = p3.shape[1]
    rows = seq_len * n_seq
    return pl.pallas_call(
        functools.partial(_pool_kernel, seq_len=seq_len, n_seq=n_seq, rows_2d=rows_2d),
        out_shape=jax.ShapeDtypeStruct((N_GROUPS, n, LANES), BF16),
        grid=(n // rows,),
        in_specs=[
            pl.BlockSpec((N_GROUPS, rows, LANES), lambda i: (SLAB_POOL // N_GROUPS, i, 0)),
            _const_spec(amat.shape),
            _const_spec((N_GROUPS, seq_len, LANES)),
            _const_spec((N_GROUPS, LANES, LANES)),
            _const_spec((N_GROUPS, 1, LANES)),
        ],
        out_specs=pl.BlockSpec((N_GROUPS, rows, LANES), lambda i: (0, i, 0)),
        compiler_params=_cparams(("parallel",)),
        name="pool",
    )(p3, amat, invc, wpg, pscale)


def _window_matrix(size, w):
    idx = np.arange(size)
    lo = np.clip(idx - w // 2, 0, size)
    hi = np.clip(idx - w // 2 + w, 0, size)
    s = np.arange(size)[None, :]
    a = ((s >= lo[:, None]) & (s < hi[:, None])).astype(np.float32)
    return a, (hi - lo).astype(np.float32)


def _pool_constants(seq_len, rows_2d):
    mats, invs = [], []
    for w in POOL_WINDOWS:
        if rows_2d is None:
            a, cnt = _window_matrix(seq_len, w)
        else:
            _, cr = _window_matrix(rows_2d, w)
            a, cc = _window_matrix(GRID_W, w)
            cnt = np.kron(cr, cc)
        mats.append(a)
        invs.append(np.repeat((1.0 / cnt)[:, None], LANES, axis=1))
    return (jnp.asarray(np.stack(mats), dtype=BF16), jnp.asarray(np.stack(invs), dtype=F32))


MIX_TM = 512


def _mix_kernel(x_ref, mod_ref, ya_ref, yb_ref, ga_ref, gb_ref, wa_ref, wb_ref, wo_ref,
                g2_ref, wr_ref, br_ref, x1_ref, h2_ref, info_ref):
    ya = jnp.concatenate([ya_ref[i] for i in range(N_HEADS)], axis=1)
    yb = jnp.concatenate([yb_ref[i] for i in range(N_GROUPS)], axis=1)
    ga = jnp.concatenate([ga_ref[i] for i in range(8)], axis=1).astype(F32)
    gb = jnp.concatenate([gb_ref[i] for i in range(8)], axis=1).astype(F32)
    merged = jax.nn.sigmoid(ga) * _dot(ya, wa_ref[...]) + jax.nn.sigmoid(gb) * _dot(yb, wb_ref[...])
    out = _dot(merged.astype(BF16), wo_ref[...])
    x1 = x_ref[...] + mod_ref[0, 2:3, :] * out
    x1_ref[...] = x1

    h2 = x1 * _rms_scale(x1) * g2_ref[...]
    h2 = h2 * (1.0 + mod_ref[0, 4:5, :]) + mod_ref[0, 3:4, :]
    h_hi, h_lo = _split_bf16(h2)
    packed = _pack_bf16_pairs(h_hi.astype(F32))
    h2_ref[0:MIX_TM, :] = packed[:, :SC_ROW]
    h2_ref[MIX_TM:2 * MIX_TM, :] = packed[:, SC_ROW:]
    logits = _dot3_packed(h_hi, h_lo, wr_ref[...]) + br_ref[...]
    first, second, w1, w2 = _route_rows(logits.T)
    zero = jnp.zeros_like(w1)
    info_ref[...] = jnp.concatenate([first, second, w1, w2, zero, zero, zero, zero], axis=0)


def _mix(x2d, mod3, mod_row_fn, ya, yb, p3, wa, wb, wo, g_norm2, wr, br):
    n = x2d.shape[0]
    return pl.pallas_call(
        _mix_kernel,
        out_shape=(jax.ShapeDtypeStruct((n, D_MODEL), F32),
                   jax.ShapeDtypeStruct((2 * n, SC_ROW), jnp.int32), jax.ShapeDtypeStruct((8, n), F32)),
        grid=(n // MIX_TM,),
        in_specs=[
            pl.BlockSpec((MIX_TM, D_MODEL), lambda i: (i, 0)),
            pl.BlockSpec((1, 6, D_MODEL), lambda i: (mod_row_fn(i * MIX_TM), 0, 0)),
            pl.BlockSpec((N_HEADS, MIX_TM, LANES), lambda i: (0, i, 0)),
            pl.BlockSpec((N_GROUPS, MIX_TM, LANES), lambda i: (0, i, 0)),
            pl.BlockSpec((8, MIX_TM, LANES), lambda i: (SLAB_GA // 8, i, 0)),
            pl.BlockSpec((8, MIX_TM, LANES), lambda i: (SLAB_GB // 8, i, 0)),
            _const_spec((D_MODEL, D_MODEL)),
            _const_spec((N_GROUPS * LANES, D_MODEL)),
            _const_spec((D_MODEL, D_MODEL)),
            _const_spec((1, D_MODEL)),
            _const_spec((D_MODEL, 2 * LANES)),
            _const_spec((1, LANES)),
        ],
        out_specs=(pl.BlockSpec((MIX_TM, D_MODEL), lambda i: (i, 0)),
                   pl.BlockSpec((2 * MIX_TM, SC_ROW), lambda i: (i, 0)),
                   pl.BlockSpec((8, MIX_TM), lambda i: (0, i))),
        compiler_params=_cparams(("parallel",)),
        name="mix",
    )(x2d, mod3, ya, yb, p3, p3, wa, wb, wo, g_norm2, wr, br)


PLAN_TB = 1024
ROUTER_ROWS = 4 + N_EXPERTS
SLOT_TM = 1024
SC_ROW = 256
SC_WIN = 128
HI_MASK = -65536


def _pack_bf16_pairs(x):
    bits = pltpu.bitcast(x, jnp.int32)
    half = D_MODEL // 2
    return jnp.bitwise_or(jnp.bitwise_and(lax.shift_right_logical(bits[:, :half], 16), 0xFFFF),
                          jnp.bitwise_and(bits[:, half:], HI_MASK))


def _unpack_bf16_pairs(lo_words, hi_words):
    def low(w):
        return pltpu.bitcast(lax.shift_left(w, 16), F32)

    def high(w):
        return pltpu.bitcast(jnp.bitwise_and(w, HI_MASK), F32)

    return jnp.concatenate([low(lo_words), low(hi_words), high(lo_words), high(hi_words)], axis=1).astype(BF16)


def _route_rows(lt):
    lg = [lt[i:i + 1] for i in range(N_GROUPS)]
    mx = jnp.maximum(jnp.maximum(lg[0], lg[1]), jnp.maximum(lg[2], lg[3]))
    p_top = 1.0 / (jnp.exp(lg[0] - mx) + jnp.exp(lg[1] - mx) + jnp.exp(lg[2] - mx) + jnp.exp(lg[3] - mx))
    gidx = jnp.zeros(lg[0].shape, jnp.int32)
    best = lg[0]
    for i in range(1, N_GROUPS):
        upd = lg[i] > best
        gidx = jnp.where(upd, i, gidx)
        best = jnp.where(upd, lg[i], best)

    def expert_row(g, e):
        r = N_GROUPS + g * EXPERTS_PER_GROUP + e
        return lt[r:r + 1]

    le = [jnp.where(gidx == 0, expert_row(0, e),
                    jnp.where(gidx == 1, expert_row(1, e),
                              jnp.where(gidx == 2, expert_row(2, e), expert_row(3, e))))
          for e in range(EXPERTS_PER_GROUP)]
    v1 = jnp.maximum(jnp.maximum(le[0], le[1]), jnp.maximum(le[2], le[3]))
    i1 = jnp.full(v1.shape, EXPERTS_PER_GROUP - 1, jnp.int32)
    for e in range(EXPERTS_PER_GROUP - 2, -1, -1):
        i1 = jnp.where(le[e] == v1, e, i1)
    le2 = [jnp.where(i1 == e, -jnp.inf, le[e]) for e in range(EXPERTS_PER_GROUP)]
    v2 = jnp.maximum(jnp.maximum(le2[0], le2[1]), jnp.maximum(le2[2], le2[3]))
    i2 = jnp.full(v2.shape, EXPERTS_PER_GROUP - 1, jnp.int32)
    for e in range(EXPERTS_PER_GROUP - 2, -1, -1):
        i2 = jnp.where(le2[e] == v2, e, i2)
    e2 = jnp.exp(v2 - v1)
    first = (gidx * EXPERTS_PER_GROUP + i1).astype(F32)
    second = (gidx * EXPERTS_PER_GROUP + i2).astype(F32)
    return first, second, p_top / (1.0 + e2), p_top * e2 / (1.0 + e2)


def _plan_kernel(info_ref, before_ref, sidx_ref, gidx_ref, tmap_ref, rank_scr, count_scr, *, n_blocks, min_tiles,
                 slot_tm):
    step = pl.program_id(0)
    expert_id = lax.broadcasted_iota(jnp.int32, (N_EXPERTS, PLAN_TB), 0).astype(F32)

    @pl.when(step == 0)
    def _():
        count_scr[...] = jnp.zeros_like(count_scr)

    @pl.when(step < n_blocks)
    def _():
        cols = pl.ds(pl.multiple_of(step * PLAN_TB, PLAN_TB), PLAN_TB)
        sel = jnp.where((info_ref[0:1, cols] == expert_id) | (info_ref[1:2, cols] == expert_id), 1.0, 0.0)
        selb = sel.astype(BF16)
        rank_scr[:, cols] = _dot(selb, before_ref[...]) + count_scr[:, 0:1]
        count_scr[...] += _dot(selb, jnp.ones((PLAN_TB, LANES), BF16))

    @pl.when(step == n_blocks)
    def _():
        counts = count_scr[...]
        padded = jnp.floor((counts + (slot_tm - 1)) * (1.0 / slot_tm)) * slot_tm
        padded = jnp.maximum(padded, float(min_tiles * slot_tm))
        starts, ends, run = [], [], jnp.zeros((1, LANES), F32)
        for e in range(N_EXPERTS):
            starts.append(run)
            run = run + padded[e:e + 1]
            ends.append(run)
        n = info_ref.shape[1]
        shift = slot_tm.bit_length() - 1
        for k in range(2):
            chosen = info_ref[k:k + 1, :]
            slot = jnp.zeros((1, n), F32)
            for e in range(N_EXPERTS):
                slot = jnp.where(chosen == float(e), starts[e][:, 0:1] + rank_scr[e:e + 1, :], slot)
            slot = slot.astype(jnp.int32)
            low = jnp.left_shift(jnp.right_shift(slot, shift), shift + 1) + jnp.bitwise_and(slot, slot_tm - 1)
            for half in range(2):
                rows = low + half * slot_tm
                for c in range(n // LANES):
                    piece = rows[:, c * LANES:(c + 1) * LANES]
                    t, j = divmod(c, MIX_TM // LANES)
                    r = ((k * (n // MIX_TM) + t) * 2 + half) * (MIX_TM // LANES) + j
                    sidx_ref[r:r + 1, :] = piece
                    t, j = divmod(c, FIN_TM // LANES)
                    r = ((t * 2 + k) * 2 + half) * (FIN_TM // LANES) + j
                    gidx_ref[r:r + 1, :] = piece
        tile_start = lax.broadcasted_iota(jnp.int32, (1, LANES), 1).astype(F32) * slot_tm
        used = jnp.where(tile_start < run, 1.0, 0.0)
        tile_start = jnp.minimum(tile_start, run - slot_tm)
        owner = jnp.zeros((1, LANES), F32)
        for e in range(N_EXPERTS - 1):
            owner = owner + jnp.where(ends[e] <= tile_start, 1.0, 0.0)
        tile = tile_start * (1.0 / slot_tm)
        tmap_ref[...] = jnp.concatenate([owner, used, tile] + [jnp.zeros((1, LANES), F32)] * 5,
                                        axis=0).astype(jnp.int32)


def _plan(info, min_tiles, slot_tm):
    n = info.shape[1]
    n_blocks = n // PLAN_TB
    t_i = lax.broadcasted_iota(jnp.int32, (PLAN_TB, PLAN_TB), 0)
    t_j = lax.broadcasted_iota(jnp.int32, (PLAN_TB, PLAN_TB), 1)
    before = (t_i < t_j).astype(BF16)
    return pl.pallas_call(
        functools.partial(_plan_kernel, n_blocks=n_blocks, min_tiles=min_tiles, slot_tm=slot_tm),
        out_shape=(jax.ShapeDtypeStruct((4 * n // LANES, LANES), jnp.int32),
                   jax.ShapeDtypeStruct((4 * n // LANES, LANES), jnp.int32),
                   jax.ShapeDtypeStruct((8, LANES), jnp.int32)),
        grid=(n_blocks + 1,),
        in_specs=[_const_spec((8, n)), _const_spec((PLAN_TB, PLAN_TB))],
        out_specs=(pl.BlockSpec((4 * n // LANES, LANES), lambda i: (0, 0)),
                   pl.BlockSpec((4 * n // LANES, LANES), lambda i: (0, 0)),
                   pl.BlockSpec((8, LANES), lambda i: (0, 0))),
        scratch_shapes=[pltpu.VMEM((N_EXPERTS, n), F32), pltpu.VMEM((N_EXPERTS, LANES), F32)],
        compiler_params=_cparams(("arbitrary",)),
        name="plan",
    )(info, before)


def _sc_mesh():
    return plsc.VectorSubcoreMesh(core_axis_name="core", subcore_axis_name="subcore")


def _sc_scatter_rows(x, idx, n_out):
    n_idx = idx.shape[0]
    src_blocks = x.shape[0] // SC_WIN

    @pl.kernel(out_type=jax.ShapeDtypeStruct((n_out, SC_ROW), x.dtype), mesh=_sc_mesh(), scratch_types=[])
    def scatter_kernel(x_hbm, i_hbm, o_hbm):
        def body(x_vmem, i_vmem):
            pltpu.sync_copy(x_vmem, o_hbm.at[i_vmem.at[0]])

        pltpu.emit_pipeline(
            body,
            grid=(n_idx // SC_WIN,),
            in_specs=[pl.BlockSpec((SC_WIN, SC_ROW), index_map=lambda i: (i % src_blocks, 0)),
                      pl.BlockSpec((1, SC_WIN), index_map=lambda i: (0, i))],
            out_specs=[],
            core_axis_name=("core", "subcore"),
            dimension_semantics=(pltpu.PARALLEL,),
        )(x_hbm, i_hbm)

    return scatter_kernel(x, idx.reshape(1, n_idx))


def _sc_gather_rows(x, idx):
    n_idx = idx.shape[0]

    @pl.kernel(out_type=jax.ShapeDtypeStruct((n_idx, SC_ROW), x.dtype), mesh=_sc_mesh(), scratch_types=[])
    def gather_kernel(x_hbm, i_hbm, o_hbm):
        def body(i_vmem, o_vmem):
            pltpu.sync_copy(x_hbm.at[i_vmem.at[0]], o_vmem)

        pltpu.emit_pipeline(
            body,
            grid=(n_idx // SC_WIN,),
            in_specs=[pl.BlockSpec((1, SC_WIN), index_map=lambda i: (0, i))],
            out_specs=[pl.BlockSpec((SC_WIN, SC_ROW), index_map=lambda i: (i, 0))],
            core_axis_name=("core", "subcore"),
            dimension_semantics=(pltpu.PARALLEL,),
        )(i_hbm, o_hbm)

    return gather_kernel(x, idx.reshape(1, n_idx))


def _experts_kernel(tmap_ref, xs_ref, wg_ref, wu_ref, wd_ref, ys_ref, *wb_refs, cast_weights, slot_tm):
    j = pl.program_id(0)
    if cast_weights:
        wg_b, wu_b, wd_b = wb_refs

        @pl.when((j == 0) | (tmap_ref[0, j] != tmap_ref[0, jnp.maximum(j - 1, 0)]))
        def _():
            wg_b[0] = wg_ref[0].astype(BF16)
            wu_b[0] = wu_ref[0].astype(BF16)
            wd_b[0] = wd_ref[0].astype(BF16)
    else:
        wg_b, wu_b, wd_b = wg_ref, wu_ref, wd_ref

    @pl.when(tmap_ref[1, j] > 0)
    def _():
        x = _unpack_bf16_pairs(xs_ref[0:slot_tm, :], xs_ref[slot_tm:2 * slot_tm, :])
        a = _dot(x, wg_b[0])
        u = _dot(x, wu_b[0])
        act = (a * jax.nn.sigmoid(a) * u).astype(BF16)
        y = _dot(act, wd_b[0]).astype(BF16).astype(F32)
        packed = _pack_bf16_pairs(y)
        ys_ref[0:slot_tm, :] = packed[:, :SC_ROW]
        ys_ref[slot_tm:2 * slot_tm, :] = packed[:, SC_ROW:]


def _experts(xs, tmap, wg, wu, wd, slot_tm):
    n_tiles = xs.shape[0] // (2 * slot_tm)
    cast_weights = wg.dtype != BF16
    tile_spec = pl.BlockSpec((2 * slot_tm, SC_ROW), lambda j, tm: (tm[2, j], 0))
    w_specs = [pl.BlockSpec((1, D_MODEL, D_EXPERT), lambda j, tm: (tm[0, j], 0, 0)),
               pl.BlockSpec((1, D_MODEL, D_EXPERT), lambda j, tm: (tm[0, j], 0, 0)),
               pl.BlockSpec((1, D_EXPERT, D_MODEL), lambda j, tm: (tm[0, j], 0, 0))]
    out_shape = [jax.ShapeDtypeStruct(xs.shape, jnp.int32)]
    out_specs = [tile_spec]
    if cast_weights:
        out_shape += [jax.ShapeDtypeStruct(w.shape, BF16) for w in (wg, wu, wd)]
        out_specs += w_specs
    grid_spec = pltpu.PrefetchScalarGridSpec(
        num_scalar_prefetch=1,
        grid=(n_tiles,),
        in_specs=[tile_spec] + w_specs,
        out_specs=out_specs,
    )
    out = pl.pallas_call(
        functools.partial(_experts_kernel, cast_weights=cast_weights, slot_tm=slot_tm),
        out_shape=out_shape,
        grid_spec=grid_spec,
        compiler_params=_cparams(("arbitrary",)),
        name="experts",
    )(tmap, xs, wg, wu, wd)
    return out[0], tuple(out[1:])


FIN_TM = 1024


def _final_kernel(x1_ref, mod_ref, yg_ref, info_ref, gf_ref, y_ref):
    w_cols = info_ref[...].T
    moe = None
    for k in range(2):
        base = 2 * k * FIN_TM
        yk = _unpack_bf16_pairs(yg_ref[base:base + FIN_TM, :], yg_ref[base + FIN_TM:base + 2 * FIN_TM, :])
        term = w_cols[:, 2 + k:3 + k] * yk.astype(F32)
        moe = term if moe is None else moe + term
    x2 = x1_ref[...] + mod_ref[0, 5:6, :] * moe
    y_ref[...] = x2 * _rms_scale(x2) * gf_ref[...]


def _final(x1, mod3, mod_row_fn, yg, info, g_final):
    n = x1.shape[0]
    return pl.pallas_call(
        _final_kernel,
        out_shape=jax.ShapeDtypeStruct((n, D_MODEL), F32),
        grid=(n // FIN_TM,),
        in_specs=[
            pl.BlockSpec((FIN_TM, D_MODEL), lambda i: (i, 0)),
            pl.BlockSpec((1, 6, D_MODEL), lambda i: (mod_row_fn(i * FIN_TM), 0, 0)),
            pl.BlockSpec((4 * FIN_TM, SC_ROW), lambda i: (i, 0)),
            pl.BlockSpec((8, FIN_TM), lambda i: (0, i)),
            _const_spec((1, D_MODEL)),
        ],
        out_specs=pl.BlockSpec((FIN_TM, D_MODEL), lambda i: (i, 0)),
        compiler_params=_cparams(("parallel",)),
        name="final",
    )(x1, mod3, yg, info, g_final)


def _moe(x1, h2p, info, mod3, mod_row_fn, wg, wu, wd, g_final):
    n = x1.shape[0]
    slot_tm = min(SLOT_TM, 2 * n // N_EXPERTS)
    sidx, gidx, tmap = _plan(info, min_tiles=int(wg.dtype != BF16), slot_tm=slot_tm)
    n_slots = 2 * n + N_EXPERTS * slot_tm
    xs = _sc_scatter_rows(h2p, sidx.reshape(-1), 2 * n_slots)
    ys, w_bf16 = _experts(xs, tmap, wg, wu, wd, slot_tm)
    yg = _sc_gather_rows(ys, gidx.reshape(-1))
    return _final(x1, mod3, mod_row_fn, yg, info, g_final), w_bf16


def _stream(x, mod3, mod_row_fn, seq_len, rows_2d, state, w):
    bsz, t, _ = x.shape
    n = bsz * t
    x2d = x.reshape(n, D_MODEL)
    p3, bcum, imb, amax = _in_projection(x2d, mod3, mod_row_fn, w["g_norm1"], w["wmain"], w["bmain"],
                                         w["colscale"], w["w_gates"], w["b_gates"])
    m0 = None if state is None else state[2]
    cols, rows, cs = _gate_prep(bcum, imb, amax, m0)
    ml = _mlstm(p3, cols, rows, cs, w["gn3"], None if state is None else state[:2])
    amat, invc = _pool_constants(seq_len, rows_2d)
    n_seq = max(1, 1024 // seq_len)
    yb = _pool(p3, amat, invc, w["wpg"], w["pscale"], seq_len, n_seq, rows_2d)
    ya = ml if state is not None else ml[0]
    x1, h2p, info = _mix(x2d, mod3, mod_row_fn, ya, yb, p3, w["wa"], w["wb"], w["wo"],
                         w["g_norm2"], w["wr"], w["br"])
    y, w_bf16 = _moe(x1, h2p, info, mod3, mod_row_fn, w["wg"], w["wu"], w["wd"], w["g_final"])
    return y.reshape(bsz, t, D_MODEL), ml, cs, w_bf16


def kernel(x_prompt, x_sample, state_C, state_n, state_m, c, c_ctx, w_mod, b_mod, g_norm1, w_in, b_in, gn_gain, w_pool_grp, pool_scale, w_proj_a, w_proj_b, w_out, g_norm2, w_router_grp, b_router_grp, w_router_exp, b_router_exp, w_exp_gate, w_exp_up, w_exp_down, g_final):
    bp, tp, _ = x_prompt.shape
    bs, ts, _ = x_sample.shape
    assert w_mod.shape[0] == 1, "single trunk layer"
    assert tp == CHUNK and ts == ML_ROWS and bs % ML_SEQS == 0

    cvec = jnp.zeros((8, D_MODEL), F32).at[0].set(c_ctx).at[1:1 + bs].set(c)
    mod3 = _modulation(cvec, w_mod[0], b_mod[0])

    wi, bi = w_in[0], b_in[0]
    d = D_MODEL
    o_q, o_k, o_v, o_o, o_g, o_p, o_ga, o_gb = 0, d, 2 * d, 3 * d, 4 * d, 4 * d + 32, 4 * d + 32 + 512, 4 * d + 32 + 512 + d
    order = [(o_q, d), (o_k, d), (o_v, d), (o_o, d), (o_ga, d), (o_gb, d), (o_p, 512)]
    wmain = _pack_columns(wi, order)
    bmain = jnp.concatenate([bi[s:s + l] for s, l in order]).reshape(1, D_MAIN)
    colscale = jnp.ones((D_MAIN,), F32).at[SLAB_K * LANES:(SLAB_K + N_HEADS) * LANES].set(HEAD_DIM ** -0.5)
    colscale = colscale.reshape(1, D_MAIN)
    gwt = _take_columns_t(wi, o_g, 32)
    gb = bi[o_g:o_g + 32]
    i_f, f_f, i_b, f_b = (slice(0, 8), slice(8, 16), slice(16, 24), slice(24, 32))
    pad_w = jnp.zeros((LANES - 2 * N_HD, d), F32)
    pad_b = jnp.zeros((LANES - 2 * N_HD,), F32)
    wg = jnp.concatenate([gwt[f_f], gwt[f_b], gwt[i_f], gwt[i_b], pad_w], axis=0).T
    bg = jnp.concatenate([gb[f_f], gb[f_b], gb[i_f], gb[i_b], pad_b]).reshape(1, LANES)
    wg = jnp.concatenate(_split_bf16(wg), axis=1)

    wr = jnp.concatenate([w_router_grp[0], w_router_exp[0].reshape(d, N_EXPERTS),
                          jnp.zeros((d, LANES - ROUTER_ROWS), F32)], axis=1)
    br = jnp.concatenate([b_router_grp[0], b_router_exp[0].reshape(N_EXPERTS),
                          jnp.zeros((LANES - ROUTER_ROWS,), F32)]).reshape(1, LANES)
    wr = jnp.concatenate(_split_bf16(wr), axis=1)

    w = dict(
        g_norm1=g_norm1[0].reshape(1, d), wmain=wmain, bmain=bmain, colscale=colscale,
        w_gates=wg, b_gates=bg,
        gn3=gn_gain[0].reshape(N_HEADS, 1, HEAD_DIM),
        wpg=w_pool_grp[0].astype(BF16), pscale=pool_scale[0].reshape(N_GROUPS, 1, LANES),
        wa=w_proj_a[0].astype(BF16), wb=w_proj_b[0].astype(BF16), wo=w_out[0].astype(BF16),
        g_norm2=g_norm2[0].reshape(1, d), wr=wr, br=br,
        wg=w_exp_gate[0], wu=w_exp_up[0], wd=w_exp_down[0],
        g_final=g_final.reshape(1, d),
    )

    y_prompt, (_, cst, nst), cs_prompt, w_bf16 = _stream(x_prompt, mod3, lambda r: 0, tp, None, None, w)
    w = dict(w, wg=w_bf16[0], wu=w_bf16[1], wd=w_bf16[2])
    n0 = state_n[:, 0].reshape(bs, 2, N_HEADS, 1, HEAD_DIM)
    m0 = jnp.pad(state_m[:, 0].reshape(bs, N_HD), ((0, 0), (0, LANES - N_HD))).reshape(bs, 1, LANES)
    y_sample, _, _, _ = _stream(x_sample, mod3, lambda r: 1 + r // ts, ts, ts // GRID_W, (state_C, n0, m0), w)

    new_c = cst
    new_n = nst.reshape(bp, 1, 2, N_HEADS, HEAD_DIM)
    new_m = cs_prompt[:, 1, :N_HD].reshape(bp, 1, 2, N_HEADS)
    return (y_prompt, y_sample, new_c, new_n, new_m)
```

```python
import functools

import numpy as np
import jax
import jax.numpy as jnp
from jax import lax
from jax.experimental import pallas as pl
from jax.experimental.pallas import tpu as pltpu
from jax.experimental.pallas import tpu_sc as plsc

F32 = jnp.float32
BF16 = jnp.bfloat16

D_MODEL = 1024
N_HEADS = 8
HEAD_DIM = 128
GRID_W = 64
POOL_WINDOWS = (2, 4, 8, 16)
N_GROUPS = 4
EXPERTS_PER_GROUP = 4
N_EXPERTS = 16
D_EXPERT = 512
EPS = 1e-6
NEG = -1e30

LANES = 128
CHUNK = 256
N_SLABS = 52
SLAB_Q, SLAB_K, SLAB_V, SLAB_O, SLAB_GA, SLAB_GB, SLAB_POOL = 0, 8, 16, 24, 32, 40, 48
D_MAIN = N_SLABS * LANES
VMEM_LIMIT = 56 * 1024 * 1024


def _cparams(sem):
    return pltpu.CompilerParams(dimension_semantics=sem, vmem_limit_bytes=VMEM_LIMIT)


def _const_spec(shape):
    nd = len(shape)
    return pl.BlockSpec(shape, lambda *_: (0,) * nd, pipeline_mode=pl.Buffered(1))


def _split_bf16(x):
    hi = x.astype(BF16)
    lo = (x - hi.astype(F32)).astype(BF16)
    return hi, lo


def _dot(a, b):
    return jnp.dot(a, b, preferred_element_type=F32)


def _dot3(a, b_hi, b_lo):
    a_hi, a_lo = _split_bf16(a)
    return _dot(a_hi, b_hi) + _dot(a_hi, b_lo) + _dot(a_lo, b_hi)


def _dot3_packed(a_hi, a_lo, b_hilo):
    first = _dot(a_hi, b_hilo)
    return first[:, :LANES] + first[:, LANES:] + _dot(a_lo, b_hilo[:, :LANES])


def _rms_scale(x):
    return lax.rsqrt(jnp.mean(x * x, axis=-1, keepdims=True) + EPS)


def _log_sigmoid(x):
    return jnp.minimum(x, 0.0) - jnp.log1p(jnp.exp(-jnp.abs(x)))


def _mod_kernel(c_ref, w_ref, b_ref, o_ref):
    c = c_ref[...]
    s = c * jax.nn.sigmoid(c)
    w_hi, w_lo = _split_bf16(w_ref[...])
    o_ref[...] = _dot3(s, w_hi, w_lo) + b_ref[...]


def _modulation(cvec, w_mod, b_mod):
    tn = 2048
    out = pl.pallas_call(
        _mod_kernel,
        out_shape=jax.ShapeDtypeStruct((8, 6 * D_MODEL), F32),
        grid=(6 * D_MODEL // tn,),
        in_specs=[
            pl.BlockSpec((8, D_MODEL), lambda j: (0, 0)),
            pl.BlockSpec((D_MODEL, tn), lambda j: (0, j)),
            pl.BlockSpec((1, tn), lambda j: (0, j)),
        ],
        out_specs=pl.BlockSpec((8, tn), lambda j: (0, j)),
        compiler_params=_cparams(("parallel",)),
        name="modulation",
    )(cvec, w_mod, b_mod.reshape(1, -1))
    return out.reshape(8, 6, D_MODEL)


PACK_COLS = 512


def _pack_kernel(wt_ref, o_ref):
    o_ref[...] = wt_ref[...].T.astype(BF16)


def _pack_columns(w, order):
    k, n = w.shape
    starts = []
    for src, width in order:
        assert width % PACK_COLS == 0
        starts += [src + j for j in range(0, width, PACK_COLS)]
    n_out = len(starts) * PACK_COLS

    unit = 32
    assert all(s % unit == 0 for s in starts)

    def src_row(j):
        row = jnp.int32(starts[-1] // unit)
        for idx in range(len(starts) - 2, -1, -1):
            row = jnp.where(j == idx, starts[idx] // unit, row)
        return row * unit

    return pl.pallas_call(
        _pack_kernel,
        out_shape=jax.ShapeDtypeStruct((k, n_out), BF16),
        grid=(len(starts),),
        in_specs=[pl.BlockSpec((pl.Element(PACK_COLS), pl.Element(k)), lambda j: (src_row(j), 0))],
        out_specs=pl.BlockSpec((k, PACK_COLS), lambda j: (0, j)),
        compiler_params=_cparams(("parallel",)),
        name="pack_columns",
    )(w.T)


def _gate_weights_kernel(wt_ref, o_ref):
    g = wt_ref[...]
    pad = jnp.zeros((LANES - g.shape[0], g.shape[1]), F32)
    w = jnp.concatenate([g[8:16], g[24:32], g[0:8], g[16:24], pad], axis=0).T
    o_ref[...] = jnp.concatenate(_split_bf16(w), axis=1)


def _gate_weights(w, start):
    k = w.shape[0]
    return pl.pallas_call(
        _gate_weights_kernel,
        out_shape=jax.ShapeDtypeStruct((k, 2 * LANES), BF16),
        grid=(1,),
        in_specs=[pl.BlockSpec((pl.Element(4 * N_HEADS), pl.Element(k)), lambda j: (start, 0))],
        out_specs=pl.BlockSpec((k, 2 * LANES), lambda j: (0, 0)),
        name="gate_weights",
    )(w.T)


IN_TM = 512
IN_TN = 512
LOG2E = 1.4426950408889634
N_HD = 2 * N_HEADS


def _fwd_lanes(shape):
    lane = lax.broadcasted_iota(jnp.int32, shape, len(shape) - 1)
    return (lane & N_HEADS) == 0


def _inproj_kernel(x_ref, mod_ref, g_ref, w_ref, b_ref, cs_ref, wg_ref, bg_ref,
                   p3_ref, bcum_ref, imb_ref, amax_ref):
    x = x_ref[...]
    h = x * _rms_scale(x) * g_ref[...]
    h = h * (1.0 + mod_ref[0, 1:2, :]) + mod_ref[0, 0:1, :]
    hb = h.astype(BF16)
    for n in range(D_MAIN // IN_TN):
        sl = slice(n * IN_TN, (n + 1) * IN_TN)
        acc = (_dot(hb, w_ref[:, sl]) + b_ref[:, sl]) * cs_ref[:, sl]
        accb = acc.astype(BF16)
        for j in range(IN_TN // LANES):
            p3_ref[n * (IN_TN // LANES) + j] = accb[:, j * LANES:(j + 1) * LANES]

    h_lo = (h - hb.astype(F32)).astype(BF16)
    gates = _dot3_packed(hb, h_lo, wg_ref[...]) + bg_ref[...]
    ls = _log_sigmoid(gates)
    gi = pltpu.roll(gates, LANES - N_HD, 1)
    row = lax.broadcasted_iota(jnp.int32, (CHUNK, LANES), 0)
    fwd = _fwd_lanes((CHUNK, LANES))
    for c in range(IN_TM // CHUNK):
        rows = slice(c * CHUNK, (c + 1) * CHUNK)
        cf = ls[rows]
        cb = cf
        s = 1
        while s < CHUNK:
            cf = cf + jnp.where(row >= s, pltpu.roll(cf, s, 0), 0.0)
            cb = cb + jnp.where(row < CHUNK - s, pltpu.roll(cb, CHUNK - s, 0), 0.0)
            s *= 2
        b = jnp.where(fwd, cf, cb)
        imb = gi[rows] - b
        mf = imb
        mb = imb
        s = 1
        while s < CHUNK:
            mf = jnp.maximum(mf, jnp.where(row >= s, pltpu.roll(mf, s, 0), -jnp.inf))
            mb = jnp.maximum(mb, jnp.where(row < CHUNK - s, pltpu.roll(mb, CHUNK - s, 0), -jnp.inf))
            s *= 2
        bcum_ref[rows, :] = b
        imb_ref[rows, :] = imb
        amax_ref[rows, :] = b + jnp.where(fwd, mf, mb)


def _in_projection(x2d, mod3, mod_row_fn, g_norm1, wmain, bmain, colscale, wg, bg):
    n = x2d.shape[0]
    return pl.pallas_call(
        _inproj_kernel,
        out_shape=(jax.ShapeDtypeStruct((N_SLABS, n, LANES), BF16),) + (jax.ShapeDtypeStruct((n, LANES), F32),) * 3,
        grid=(n // IN_TM,),
        in_specs=[
            pl.BlockSpec((IN_TM, D_MODEL), lambda i: (i, 0)),
            pl.BlockSpec((1, 6, D_MODEL), lambda i: (mod_row_fn(i * IN_TM), 0, 0)),
            _const_spec((1, D_MODEL)),
            _const_spec((D_MODEL, D_MAIN)),
            _const_spec((1, D_MAIN)),
            _const_spec((1, D_MAIN)),
            _const_spec((D_MODEL, 2 * LANES)),
            _const_spec((1, LANES)),
        ],
        out_specs=(pl.BlockSpec((N_SLABS, IN_TM, LANES), lambda i: (0, i, 0)),)
        + (pl.BlockSpec((IN_TM, LANES), lambda i: (i, 0)),) * 3,
        compiler_params=_cparams(("parallel",)),
        name="in_projection",
    )(x2d, mod3, g_norm1, wmain, bmain, colscale, wg, bg)


GP_ROWS = 1024
GP_NCH = GP_ROWS // CHUNK
Q_C2, Q_EM, Q_WI, Q_WC = 0, 1, 2, 3


def _gate_prep_kernel(*refs, carry):
    if carry:
        b_ref, imb_ref, a_ref, m0_ref, cols_ref, rw_ref, cs_ref = refs
    else:
        b_ref, imb_ref, a_ref, cols_ref, rw_ref, cs_ref = refs
    fwd1 = _fwd_lanes((1, LANES))
    lane = lax.broadcasted_iota(jnp.int32, (CHUNK, LANES), 1)
    neg = jnp.full((1, LANES), NEG, F32)

    def at_scan_end(ref, c):
        return jnp.where(fwd1, ref[(c + 1) * CHUNK - 1:(c + 1) * CHUNK, :], ref[c * CHUNK:c * CHUNK + 1, :])

    b_last = [at_scan_end(b_ref, c) for c in range(GP_NCH)]
    a_last = [at_scan_end(a_ref, c) for c in range(GP_NCH)]
    m_start = m0_ref[0] if carry else neg
    m_prev = [[None] * GP_NCH, [None] * GP_NCH]
    m_new = [[None] * GP_NCH, [None] * GP_NCH]
    for d, order in enumerate((range(GP_NCH), range(GP_NCH - 1, -1, -1))):
        m = m_start
        for c in order:
            m_prev[d][c] = m
            m_new[d][c] = jnp.maximum(b_last[c] + m, a_last[c])
            m = m_new[d][c] if carry else neg

    row_src = []
    for c in range(GP_NCH):
        rows = slice(c * CHUNK, (c + 1) * CHUNK)
        mp = jnp.where(fwd1, m_prev[0][c], m_prev[1][c])
        mn = jnp.where(fwd1, m_new[0][c], m_new[1][c])
        b = b_ref[rows, :]
        imb = imb_ref[rows, :]
        inter = b + mp
        m_t = jnp.maximum(inter, a_ref[rows, :])
        c2 = (b - m_t) * LOG2E
        em = jnp.exp(-m_t)
        wc = jnp.exp(b_last[c] - mn + imb)
        packed = jnp.where(lane < N_HD, c2, pltpu.roll(em, N_HD * Q_EM, 1))
        if carry:
            wi = jnp.exp(inter - m_t)
            packed = jnp.where(lane < N_HD * Q_WI, packed, pltpu.roll(wi, N_HD * Q_WI, 1))
        cols_ref[rows, :] = jnp.where(lane < N_HD * Q_WC, packed, pltpu.roll(wc, N_HD * Q_WC, 1))
        row_src.append(jnp.where(lane < N_HD, imb * (-LOG2E), pltpu.roll(wc, N_HD, 1)))
        cs_ref[c, 0:1, :] = jnp.exp(b_last[c] + mp - mn)
        cs_ref[c, 1:2, :] = mn

    rows_t = jnp.concatenate(row_src, axis=0).T
    for r in range(2 * N_HD):
        rw_ref[r] = rows_t[r:r + 1]


def _gate_prep(bcum, imb, amax, m0=None):
    n = bcum.shape[0]
    carry = m0 is not None
    blk = pl.BlockSpec((GP_ROWS, LANES), lambda i: (i, 0))
    in_specs = [blk, blk, blk]
    args = [bcum, imb, amax]
    if carry:
        in_specs.append(pl.BlockSpec((1, 1, LANES), lambda i: (i, 0, 0)))
        args.append(m0)
    return pl.pallas_call(
        functools.partial(_gate_prep_kernel, carry=carry),
        out_shape=(jax.ShapeDtypeStruct((n, LANES), F32), jax.ShapeDtypeStruct((2 * N_HD, 1, n), F32),
                   jax.ShapeDtypeStruct((n // CHUNK, 2, LANES), F32)),
        grid=(n // GP_ROWS,),
        in_specs=in_specs,
        out_specs=(blk, pl.BlockSpec((2 * N_HD, 1, GP_ROWS), lambda i: (0, 0, i)),
                   pl.BlockSpec((GP_NCH, 2, LANES), lambda i: (i, 0, 0))),
        compiler_params=_cparams(("parallel",)),
        name="gate_prep",
    )(*args)


ML_ROWS = 1024
ML_SEQS = 4
ML_ROWS_FREE = 4096


def _pick_col(blk, lane_iota, idx):
    return jnp.sum(jnp.where(lane_iota == idx, blk, 0.0), axis=1, keepdims=True)


def _mlstm_kernel(*refs, carry, n_chunks, n_seqs):
    if carry:
        (q_ref, k_ref, v_ref, o_ref, cols_ref, r2f_ref, r2b_ref, gn_ref, cs_ref, c0_ref, n0_ref,
         ya_ref, s_scr, h_scr) = refs
    else:
        (q_ref, k_ref, v_ref, o_ref, cols_ref, r2f_ref, r2b_ref, gn_ref, wrf_ref, wrb_ref,
         ya_ref, cst_ref, nst_ref, s_scr, h_scr) = refs
    head = pl.program_id(1)
    lane = lax.broadcasted_iota(jnp.int32, (CHUNK, LANES), 1)
    lane1 = lax.broadcasted_iota(jnp.int32, (1, LANES), 1)
    r_idx = lax.broadcasted_iota(jnp.int32, (CHUNK, CHUNK), 0)
    c_idx = lax.broadcasted_iota(jnp.int32, (CHUNK, CHUNK), 1)
    ones = jnp.ones((CHUNK, HEAD_DIM), BF16)
    tn_dims = (((0,), (0,)), ((), ()))

    for d in range(2):
        mask = (c_idx <= r_idx) if d == 0 else (c_idx >= r_idx)
        r2_ref = r2f_ref if d == 0 else r2b_ref
        base = N_HEADS * d + head
        per_seq = n_chunks // n_seqs
        local = range(per_seq) if d == 0 else range(per_seq - 1, -1, -1)
        for sq, step, c in [(sq, step, sq * per_seq + lc) for sq in range(n_seqs) for step, lc in enumerate(local)]:
            if carry and step == 0:
                n_rep = jnp.broadcast_to(n0_ref[sq, d, 0], (HEAD_DIM, HEAD_DIM)).T
                cn = jnp.concatenate([c0_ref[sq, 0, d, 0], n_rep], axis=1)
            rows = slice(c * CHUNK, (c + 1) * CHUNK)
            q = q_ref[0, rows, :]
            k = k_ref[0, rows, :]
            v = v_ref[0, rows, :]
            v1 = jnp.concatenate([v, ones], axis=1)
            if d == 0:
                s = lax.dot_general(q, k, (((1,), (1,)), ((), ())), preferred_element_type=F32)
                s_scr[rows, :] = s
            else:
                s = s_scr[rows, :]
            blk = cols_ref[rows, :]
            arg = _pick_col(blk, lane, base + N_HD * Q_C2) - r2_ref[0, :, rows]
            sw = s * jnp.exp2(jnp.where(mask, arg, NEG))
            nd = _dot(sw.astype(BF16), v1)
            if carry:
                nd = nd + _pick_col(blk, lane, base + N_HD * Q_WI) * _dot(q, cn.astype(BF16))
            hdir = nd[:, :HEAD_DIM] / jnp.maximum(jnp.abs(nd[:, HEAD_DIM:]), _pick_col(blk, lane, base + N_HD * Q_EM))
            if d == 0:
                h_scr[rows, :] = hdir
            else:
                h_scr[rows, :] = h_scr[rows, :] + hdir

            if carry and step == per_seq - 1:
                continue
            w_col = _pick_col(blk, lane, base + N_HD * Q_WC)
            if carry:
                u = lax.dot_general(k, (w_col * v1.astype(F32)).astype(BF16), tn_dims,
                                    preferred_element_type=F32)
                decay = jnp.sum(jnp.where(lane1 == base, cs_ref[c, 0:1, :], 0.0), axis=1, keepdims=True)
                cn = decay * cn + u
            else:
                w_row = (wrf_ref if d == 0 else wrb_ref)[0, :, rows]
                w8 = jnp.broadcast_to(w_row, (8, CHUNK)).astype(BF16)
                cst_ref[c, 0, d, 0] = lax.dot_general(k, (w_col * v.astype(F32)).astype(BF16), tn_dims,
                                                      preferred_element_type=F32)
                nst_ref[c, d, 0] = _dot(w8, k)[0:1, :]

    hh = h_scr[...]
    hn = hh * _rms_scale(hh) * gn_ref[0]
    ya_ref[0] = (hn * jax.nn.sigmoid(o_ref[0].astype(F32))).astype(BF16)


def _mlstm(p3, cols, rows, cs, gn3, state=None):
    n = p3.shape[1]
    carry = state is not None
    n_seqs = ML_SEQS if carry else 1
    rows_step = ML_ROWS * ML_SEQS if carry else ML_ROWS_FREE
    n_chunks = rows_step // CHUNK
    nblk = n // rows_step

    def slab(base):
        return pl.BlockSpec((1, rows_step, LANES), lambda g, h: (base + h, g, 0))

    def row(base):
        return pl.BlockSpec((1, 1, rows_step), lambda g, h: (base + h, 0, g))

    in_specs = [
        slab(SLAB_Q), slab(SLAB_K), slab(SLAB_V), slab(SLAB_O),
        pl.BlockSpec((rows_step, LANES), lambda g, h: (g, 0)),
        row(0), row(N_HEADS),
        pl.BlockSpec((1, 1, LANES), lambda g, h: (h, 0, 0)),
    ]
    args = [p3, p3, p3, p3, cols, rows, rows, gn3]
    ya_shape = jax.ShapeDtypeStruct((N_HEADS, n, LANES), BF16)
    ya_spec = pl.BlockSpec((1, rows_step, LANES), lambda g, h: (h, g, 0))
    state_spec = pl.BlockSpec((n_seqs, 1, 2, 1, HEAD_DIM, HEAD_DIM), lambda g, h: (g, 0, 0, h, 0, 0))
    if carry:
        c0, n0 = state
        in_specs += [pl.BlockSpec((n_chunks, 2, LANES), lambda g, h: (g, 0, 0)), state_spec,
                     pl.BlockSpec((n_seqs, 2, 1, 1, HEAD_DIM), lambda g, h: (g, 0, h, 0, 0))]
        args += [cs, c0, n0]
        out_shape = ya_shape
        out_specs = ya_spec
    else:
        in_specs += [row(N_HD), row(N_HD + N_HEADS)]
        args += [rows, rows]
        nseq = n // CHUNK
        out_shape = (ya_shape,
                     jax.ShapeDtypeStruct((nseq, 1, 2, N_HEADS, HEAD_DIM, HEAD_DIM), F32),
                     jax.ShapeDtypeStruct((nseq, 2, N_HEADS, 1, HEAD_DIM), F32))
        out_specs = (ya_spec,
                     pl.BlockSpec((n_chunks, 1, 2, 1, HEAD_DIM, HEAD_DIM), lambda g, h: (g, 0, 0, h, 0, 0)),
                     pl.BlockSpec((n_chunks, 2, 1, 1, HEAD_DIM), lambda g, h: (g, 0, h, 0, 0)))
    return pl.pallas_call(
        functools.partial(_mlstm_kernel, carry=carry, n_chunks=n_chunks, n_seqs=n_seqs),
        out_shape=out_shape,
        grid=(nblk, N_HEADS),
        in_specs=in_specs,
        out_specs=out_specs,
        scratch_shapes=[pltpu.VMEM((rows_step, CHUNK), F32), pltpu.VMEM((rows_step, HEAD_DIM), F32)],
        compiler_params=_cparams(("parallel", "parallel")),
        name="mlstm_latent" if carry else "mlstm_prompt",
    )(*args)


def _pool_kernel(u_ref, a_ref, ic_ref, wpg_ref, ps_ref, yb_ref, *, seq_len, n_seq, rows_2d):
    piece = seq_len if rows_2d is None else GRID_W
    n_piece = n_seq if rows_2d is None else rows_2d
    for g in range(N_GROUPS):
        u = jnp.concatenate([u_ref[g, i * piece:(i + 1) * piece, :] for i in range(n_piece)], axis=1)
        win = _dot(a_ref[g], u)
        if rows_2d is None:
            inv_cnt = jnp.concatenate([ic_ref[g]] * n_piece, axis=1)
        else:
            run = [jnp.zeros((piece, LANES), F32)]
            for r in range(rows_2d):
                run.append(run[-1] + win[:, r * LANES:(r + 1) * LANES])
            half = POOL_WINDOWS[g] // 2
            win = jnp.concatenate([run[min(r - half + POOL_WINDOWS[g], rows_2d)] - run[max(r - half, 0)]
                                   for r in range(rows_2d)], axis=1)
            inv_cnt = jnp.concatenate([ic_ref[g, r * piece:(r + 1) * piece, :] for r in range(rows_2d)], axis=1)
        p = (win * inv_cnt - u.astype(F32)).astype(BF16)
        p = jnp.concatenate([p[:, i * LANES:(i + 1) * LANES] for i in range(n_piece)], axis=0)
        yb_ref[g] = (_dot(p, wpg_ref[g]) * ps_ref[g]).astype(BF16)


def _pool(p3, amat, invc, wpg, pscale, seq_len, n_seq, rows_2d):
    n = p3.shape[1]
    rows = seq_len * n_seq
    return pl.pallas_call(
        functools.partial(_pool_kernel, seq_len=seq_len, n_seq=n_seq, rows_2d=rows_2d),
        out_shape=jax.ShapeDtypeStruct((N_GROUPS, n, LANES), BF16),
        grid=(n // rows,),
        in_specs=[
            pl.BlockSpec((N_GROUPS, rows, LANES), lambda i: (SLAB_POOL // N_GROUPS, i, 0)),
            _const_spec(amat.shape),
            _const_spec((N_GROUPS, seq_len, LANES)),
            _const_spec((N_GROUPS, LANES, LANES)),
            _const_spec((N_GROUPS, 1, LANES)),
        ],
        out_specs=pl.BlockSpec((N_GROUPS, rows, LANES), lambda i: (0, i, 0)),
        compiler_params=_cparams(("parallel",)),
        name="pool",
    )(p3, amat, invc, wpg, pscale)


def _window_matrix(size, w):
    idx = np.arange(size)
    lo = np.clip(idx - w // 2, 0, size)
    hi = np.clip(idx - w // 2 + w, 0, size)
    s = np.arange(size)[None, :]
    a = ((s >= lo[:, None]) & (s < hi[:, None])).astype(np.float32)
    return a, (hi - lo).astype(np.float32)


def _pool_constants(seq_len, rows_2d):
    mats, invs = [], []
    for w in POOL_WINDOWS:
        if rows_2d is None:
            a, cnt = _window_matrix(seq_len, w)
        else:
            _, cr = _window_matrix(rows_2d, w)
            a, cc = _window_matrix(GRID_W, w)
            cnt = np.kron(cr, cc)
        mats.append(a)
        invs.append(np.repeat((1.0 / cnt)[:, None], LANES, axis=1))
    return (jnp.asarray(np.stack(mats), dtype=BF16), jnp.asarray(np.stack(invs), dtype=F32))


MIX_TM = 512


def _mix_kernel(x_ref, mod_ref, ya_ref, yb_ref, ga_ref, gb_ref, wa_ref, wb_ref, wo_ref,
                g2_ref, wr_ref, br_ref, x1_ref, h2_ref, info_ref):
    ya = jnp.concatenate([ya_ref[i] for i in range(N_HEADS)], axis=1)
    yb = jnp.concatenate([yb_ref[i] for i in range(N_GROUPS)], axis=1)
    ga = jnp.concatenate([ga_ref[i] for i in range(8)], axis=1).astype(F32)
    gb = jnp.concatenate([gb_ref[i] for i in range(8)], axis=1).astype(F32)
    merged = jax.nn.sigmoid(ga) * _dot(ya, wa_ref[...]) + jax.nn.sigmoid(gb) * _dot(yb, wb_ref[...])
    out = _dot(merged.astype(BF16), wo_ref[...])
    x1 = x_ref[...] + mod_ref[0, 2:3, :] * out
    x1_ref[...] = x1

    h2 = x1 * _rms_scale(x1) * g2_ref[...]
    h2 = h2 * (1.0 + mod_ref[0, 4:5, :]) + mod_ref[0, 3:4, :]
    h_hi, h_lo = _split_bf16(h2)
    packed = _pack_bf16_pairs(h_hi.astype(F32))
    h2_ref[0:MIX_TM, :] = packed[:, :SC_ROW]
    h2_ref[MIX_TM:2 * MIX_TM, :] = packed[:, SC_ROW:]
    logits = _dot3_packed(h_hi, h_lo, wr_ref[...]) + br_ref[...]
    first, second, w1, w2 = _route_rows(logits.T)
    zero = jnp.zeros_like(w1)
    info_ref[...] = jnp.concatenate([first, second, w1, w2, zero, zero, zero, zero], axis=0)


def _mix(x2d, mod3, mod_row_fn, ya, yb, p3, wa, wb, wo, g_norm2, wr, br):
    n = x2d.shape[0]
    return pl.pallas_call(
        _mix_kernel,
        out_shape=(jax.ShapeDtypeStruct((n, D_MODEL), F32),
                   jax.ShapeDtypeStruct((2 * n, SC_ROW), jnp.int32), jax.ShapeDtypeStruct((8, n), F32)),
        grid=(n // MIX_TM,),
        in_specs=[
            pl.BlockSpec((MIX_TM, D_MODEL), lambda i: (i, 0)),
            pl.BlockSpec((1, 6, D_MODEL), lambda i: (mod_row_fn(i * MIX_TM), 0, 0)),
            pl.BlockSpec((N_HEADS, MIX_TM, LANES), lambda i: (0, i, 0)),
            pl.BlockSpec((N_GROUPS, MIX_TM, LANES), lambda i: (0, i, 0)),
            pl.BlockSpec((8, MIX_TM, LANES), lambda i: (SLAB_GA // 8, i, 0)),
            pl.BlockSpec((8, MIX_TM, LANES), lambda i: (SLAB_GB // 8, i, 0)),
            _const_spec((D_MODEL, D_MODEL)),
            _const_spec((N_GROUPS * LANES, D_MODEL)),
            _const_spec((D_MODEL, D_MODEL)),
            _const_spec((1, D_MODEL)),
            _const_spec((D_MODEL, 2 * LANES)),
            _const_spec((1, LANES)),
        ],
        out_specs=(pl.BlockSpec((MIX_TM, D_MODEL), lambda i: (i, 0)),
                   pl.BlockSpec((2 * MIX_TM, SC_ROW), lambda i: (i, 0)),
                   pl.BlockSpec((8, MIX_TM), lambda i: (0, i))),
        compiler_params=_cparams(("parallel",)),
        name="mix",
    )(x2d, mod3, ya, yb, p3, p3, wa, wb, wo, g_norm2, wr, br)


PLAN_TB = 1024
ROUTER_ROWS = 4 + N_EXPERTS
SLOT_TM = 1024
SC_ROW = 256
SC_WIN = 128
HI_MASK = -65536


def _pack_bf16_pairs(x):
    bits = pltpu.bitcast(x, jnp.int32)
    half = D_MODEL // 2
    return jnp.bitwise_or(jnp.bitwise_and(lax.shift_right_logical(bits[:, :half], 16), 0xFFFF),
                          jnp.bitwise_and(bits[:, half:], HI_MASK))


def _unpack_bf16_pairs(lo_words, hi_words):
    def low(w):
        return pltpu.bitcast(lax.shift_left(w, 16), F32)

    def high(w):
        return pltpu.bitcast(jnp.bitwise_and(w, HI_MASK), F32)

    return jnp.concatenate([low(lo_words), low(hi_words), high(lo_words), high(hi_words)], axis=1).astype(BF16)


def _route_rows(lt):
    lg = [lt[i:i + 1] for i in range(N_GROUPS)]
    mx = jnp.maximum(jnp.maximum(lg[0], lg[1]), jnp.maximum(lg[2], lg[3]))
    p_top = 1.0 / (jnp.exp(lg[0] - mx) + jnp.exp(lg[1] - mx) + jnp.exp(lg[2] - mx) + jnp.exp(lg[3] - mx))
    gidx = jnp.zeros(lg[0].shape, jnp.int32)
    best = lg[0]
    for i in range(1, N_GROUPS):
        upd = lg[i] > best
        gidx = jnp.where(upd, i, gidx)
        best = jnp.where(upd, lg[i], best)

    def expert_row(g, e):
        r = N_GROUPS + g * EXPERTS_PER_GROUP + e
        return lt[r:r + 1]

    le = [jnp.where(gidx == 0, expert_row(0, e),
                    jnp.where(gidx == 1, expert_row(1, e),
                              jnp.where(gidx == 2, expert_row(2, e), expert_row(3, e))))
          for e in range(EXPERTS_PER_GROUP)]
    v1 = jnp.maximum(jnp.maximum(le[0], le[1]), jnp.maximum(le[2], le[3]))
    i1 = jnp.full(v1.shape, EXPERTS_PER_GROUP - 1, jnp.int32)
    for e in range(EXPERTS_PER_GROUP - 2, -1, -1):
        i1 = jnp.where(le[e] == v1, e, i1)
    le2 = [jnp.where(i1 == e, -jnp.inf, le[e]) for e in range(EXPERTS_PER_GROUP)]
    v2 = jnp.maximum(jnp.maximum(le2[0], le2[1]), jnp.maximum(le2[2], le2[3]))
    i2 = jnp.full(v2.shape, EXPERTS_PER_GROUP - 1, jnp.int32)
    for e in range(EXPERTS_PER_GROUP - 2, -1, -1):
        i2 = jnp.where(le2[e] == v2, e, i2)
    e2 = jnp.exp(v2 - v1)
    first = (gidx * EXPERTS_PER_GROUP + i1).astype(F32)
    second = (gidx * EXPERTS_PER_GROUP + i2).astype(F32)
    return first, second, p_top / (1.0 + e2), p_top * e2 / (1.0 + e2)


def _plan_kernel(info_ref, before_ref, sidx_ref, gidx_ref, tmap_ref, rank_scr, count_scr, *, n_blocks, min_tiles,
                 slot_tm):
    step = pl.program_id(0)
    expert_id = lax.broadcasted_iota(jnp.int32, (N_EXPERTS, PLAN_TB), 0).astype(F32)

    @pl.when(step == 0)
    def _():
        count_scr[...] = jnp.zeros_like(count_scr)

    @pl.when(step < n_blocks)
    def _():
        cols = pl.ds(pl.multiple_of(step * PLAN_TB, PLAN_TB), PLAN_TB)
        sel = jnp.where((info_ref[0:1, cols] == expert_id) | (info_ref[1:2, cols] == expert_id), 1.0, 0.0)
        selb = sel.astype(BF16)
        rank_scr[:, cols] = _dot(selb, before_ref[...]) + count_scr[:, 0:1]
        count_scr[...] += _dot(selb, jnp.ones((PLAN_TB, LANES), BF16))

    @pl.when(step == n_blocks)
    def _():
        counts = count_scr[...]
        padded = jnp.floor((counts + (slot_tm - 1)) * (1.0 / slot_tm)) * slot_tm
        padded = jnp.maximum(padded, float(min_tiles * slot_tm))
        starts, ends, run = [], [], jnp.zeros((1, LANES), F32)
        for e in range(N_EXPERTS):
            starts.append(run)
            run = run + padded[e:e + 1]
            ends.append(run)
        n = info_ref.shape[1]
        shift = slot_tm.bit_length() - 1
        for k in range(2):
            chosen = info_ref[k:k + 1, :]
            slot = jnp.zeros((1, n), F32)
            for e in range(N_EXPERTS):
                slot = jnp.where(chosen == float(e), starts[e][:, 0:1] + rank_scr[e:e + 1, :], slot)
            slot = slot.astype(jnp.int32)
            low = jnp.left_shift(jnp.right_shift(slot, shift), shift + 1) + jnp.bitwise_and(slot, slot_tm - 1)
            for half in range(2):
                rows = low + half * slot_tm
                for c in range(n // LANES):
                    piece = rows[:, c * LANES:(c + 1) * LANES]
                    t, j = divmod(c, MIX_TM // LANES)
                    r = ((k * (n // MIX_TM) + t) * 2 + half) * (MIX_TM // LANES) + j
                    sidx_ref[r:r + 1, :] = piece
                    t, j = divmod(c, FIN_TM // LANES)
                    r = ((t * 2 + k) * 2 + half) * (FIN_TM // LANES) + j
                    gidx_ref[r:r + 1, :] = piece
        tile_start = lax.broadcasted_iota(jnp.int32, (1, LANES), 1).astype(F32) * slot_tm
        used = jnp.where(tile_start < run, 1.0, 0.0)
        tile_start = jnp.minimum(tile_start, run - slot_tm)
        owner = jnp.zeros((1, LANES), F32)
        for e in range(N_EXPERTS - 1):
            owner = owner + jnp.where(ends[e] <= tile_start, 1.0, 0.0)
        tile = tile_start * (1.0 / slot_tm)
        tmap_ref[...] = jnp.concatenate([owner, used, tile] + [jnp.zeros((1, LANES), F32)] * 5,
                                        axis=0).astype(jnp.int32)


def _plan(info, min_tiles, slot_tm):
    n = info.shape[1]
    n_blocks = n // PLAN_TB
    t_i = lax.broadcasted_iota(jnp.int32, (PLAN_TB, PLAN_TB), 0)
    t_j = lax.broadcasted_iota(jnp.int32, (PLAN_TB, PLAN_TB), 1)
    before = (t_i < t_j).astype(BF16)
    return pl.pallas_call(
        functools.partial(_plan_kernel, n_blocks=n_blocks, min_tiles=min_tiles, slot_tm=slot_tm),
        out_shape=(jax.ShapeDtypeStruct((4 * n // LANES, LANES), jnp.int32),
                   jax.ShapeDtypeStruct((4 * n // LANES, LANES), jnp.int32),
                   jax.ShapeDtypeStruct((8, LANES), jnp.int32)),
        grid=(n_blocks + 1,),
        in_specs=[_const_spec((8, n)), _const_spec((PLAN_TB, PLAN_TB))],
        out_specs=(pl.BlockSpec((4 * n // LANES, LANES), lambda i: (0, 0)),
                   pl.BlockSpec((4 * n // LANES, LANES), lambda i: (0, 0)),
                   pl.BlockSpec((8, LANES), lambda i: (0, 0))),
        scratch_shapes=[pltpu.VMEM((N_EXPERTS, n), F32), pltpu.VMEM((N_EXPERTS, LANES), F32)],
        compiler_params=_cparams(("arbitrary",)),
        name="plan",
    )(info, before)


def _sc_mesh():
    return plsc.VectorSubcoreMesh(core_axis_name="core", subcore_axis_name="subcore")


def _sc_scatter_rows(x, idx, n_out):
    n_idx = idx.shape[0]
    src_blocks = x.shape[0] // SC_WIN

    @pl.kernel(out_type=jax.ShapeDtypeStruct((n_out, SC_ROW), x.dtype), mesh=_sc_mesh(), scratch_types=[])
    def scatter_kernel(x_hbm, i_hbm, o_hbm):
        def body(x_vmem, i_vmem):
            pltpu.sync_copy(x_vmem, o_hbm.at[i_vmem.at[0]])

        pltpu.emit_pipeline(
            body,
            grid=(n_idx // SC_WIN,),
            in_specs=[pl.BlockSpec((SC_WIN, SC_ROW), index_map=lambda i: (i % src_blocks, 0)),
                      pl.BlockSpec((1, SC_WIN), index_map=lambda i: (0, i))],
            out_specs=[],
            core_axis_name=("core", "subcore"),
            dimension_semantics=(pltpu.PARALLEL,),
        )(x_hbm, i_hbm)

    return scatter_kernel(x, idx.reshape(1, n_idx))


def _sc_gather_rows(x, idx):
    n_idx = idx.shape[0]

    @pl.kernel(out_type=jax.ShapeDtypeStruct((n_idx, SC_ROW), x.dtype), mesh=_sc_mesh(), scratch_types=[])
    def gather_kernel(x_hbm, i_hbm, o_hbm):
        def body(i_vmem, o_vmem):
            pltpu.sync_copy(x_hbm.at[i_vmem.at[0]], o_vmem)

        pltpu.emit_pipeline(
            body,
            grid=(n_idx // SC_WIN,),
            in_specs=[pl.BlockSpec((1, SC_WIN), index_map=lambda i: (0, i))],
            out_specs=[pl.BlockSpec((SC_WIN, SC_ROW), index_map=lambda i: (i, 0))],
            core_axis_name=("core", "subcore"),
            dimension_semantics=(pltpu.PARALLEL,),
        )(i_hbm, o_hbm)

    return gather_kernel(x, idx.reshape(1, n_idx))


def _experts_kernel(tmap_ref, xs_ref, wg_ref, wu_ref, wd_ref, ys_ref, *wb_refs, cast_weights, slot_tm):
    j = pl.program_id(0)
    if cast_weights:
        wg_b, wu_b, wd_b = wb_refs

        @pl.when((j == 0) | (tmap_ref[0, j] != tmap_ref[0, jnp.maximum(j - 1, 0)]))
        def _():
            wg_b[0] = wg_ref[0].astype(BF16)
            wu_b[0] = wu_ref[0].astype(BF16)
            wd_b[0] = wd_ref[0].astype(BF16)
    else:
        wg_b, wu_b, wd_b = wg_ref, wu_ref, wd_ref

    @pl.when(tmap_ref[1, j] > 0)
    def _():
        x = _unpack_bf16_pairs(xs_ref[0:slot_tm, :], xs_ref[slot_tm:2 * slot_tm, :])
        a = _dot(x, wg_b[0])
        u = _dot(x, wu_b[0])
        act = (a * jax.nn.sigmoid(a) * u).astype(BF16)
        y = _dot(act, wd_b[0]).astype(BF16).astype(F32)
        packed = _pack_bf16_pairs(y)
        ys_ref[0:slot_tm, :] = packed[:, :SC_ROW]
        ys_ref[slot_tm:2 * slot_tm, :] = packed[:, SC_ROW:]


def _experts(xs, tmap, wg, wu, wd, slot_tm):
    n_tiles = xs.shape[0] // (2 * slot_tm)
    cast_weights = wg.dtype != BF16
    tile_spec = pl.BlockSpec((2 * slot_tm, SC_ROW), lambda j, tm: (tm[2, j], 0))
    w_specs = [pl.BlockSpec((1, D_MODEL, D_EXPERT), lambda j, tm: (tm[0, j], 0, 0)),
               pl.BlockSpec((1, D_MODEL, D_EXPERT), lambda j, tm: (tm[0, j], 0, 0)),
               pl.BlockSpec((1, D_EXPERT, D_MODEL), lambda j, tm: (tm[0, j], 0, 0))]
    out_shape = [jax.ShapeDtypeStruct(xs.shape, jnp.int32)]
    out_specs = [tile_spec]
    if cast_weights:
        out_shape += [jax.ShapeDtypeStruct(w.shape, BF16) for w in (wg, wu, wd)]
        out_specs += w_specs
    grid_spec = pltpu.PrefetchScalarGridSpec(
        num_scalar_prefetch=1,
        grid=(n_tiles,),
        in_specs=[tile_spec] + w_specs,
        out_specs=out_specs,
    )
    out = pl.pallas_call(
        functools.partial(_experts_kernel, cast_weights=cast_weights, slot_tm=slot_tm),
        out_shape=out_shape,
        grid_spec=grid_spec,
        compiler_params=_cparams(("arbitrary",)),
        name="experts",
    )(tmap, xs, wg, wu, wd)
    return out[0], tuple(out[1:])


FIN_TM = 1024


def _final_kernel(x1_ref, mod_ref, yg_ref, info_ref, gf_ref, y_ref):
    w_cols = info_ref[...].T
    moe = None
    for k in range(2):
        base = 2 * k * FIN_TM
        yk = _unpack_bf16_pairs(yg_ref[base:base + FIN_TM, :], yg_ref[base + FIN_TM:base + 2 * FIN_TM, :])
        term = w_cols[:, 2 + k:3 + k] * yk.astype(F32)
        moe = term if moe is None else moe + term
    x2 = x1_ref[...] + mod_ref[0, 5:6, :] * moe
    y_ref[...] = x2 * _rms_scale(x2) * gf_ref[...]


def _final(x1, mod3, mod_row_fn, yg, info, g_final):
    n = x1.shape[0]
    return pl.pallas_call(
        _final_kernel,
        out_shape=jax.ShapeDtypeStruct((n, D_MODEL), F32),
        grid=(n // FIN_TM,),
        in_specs=[
            pl.BlockSpec((FIN_TM, D_MODEL), lambda i: (i, 0)),
            pl.BlockSpec((1, 6, D_MODEL), lambda i: (mod_row_fn(i * FIN_TM), 0, 0)),
            pl.BlockSpec((4 * FIN_TM, SC_ROW), lambda i: (i, 0)),
            pl.BlockSpec((8, FIN_TM), lambda i: (0, i)),
            _const_spec((1, D_MODEL)),
        ],
        out_specs=pl.BlockSpec((FIN_TM, D_MODEL), lambda i: (i, 0)),
        compiler_params=_cparams(("parallel",)),
        name="final",
    )(x1, mod3, yg, info, g_final)


def _moe(x1, h2p, info, mod3, mod_row_fn, wg, wu, wd, g_final):
    n = x1.shape[0]
    slot_tm = min(SLOT_TM, 2 * n // N_EXPERTS)
    sidx, gidx, tmap = _plan(info, min_tiles=int(wg.dtype != BF16), slot_tm=slot_tm)
    n_slots = 2 * n + N_EXPERTS * slot_tm
    xs = _sc_scatter_rows(h2p, sidx.reshape(-1), 2 * n_slots)
    ys, w_bf16 = _experts(xs, tmap, wg, wu, wd, slot_tm)
    yg = _sc_gather_rows(ys, gidx.reshape(-1))
    return _final(x1, mod3, mod_row_fn, yg, info, g_final), w_bf16


def _stream(x, mod3, mod_row_fn, seq_len, rows_2d, state, w):
    bsz, t, _ = x.shape
    n = bsz * t
    x2d = x.reshape(n, D_MODEL)
    p3, bcum, imb, amax = _in_projection(x2d, mod3, mod_row_fn, w["g_norm1"], w["wmain"], w["bmain"],
                                         w["colscale"], w["w_gates"], w["b_gates"])
    m0 = None if state is None else state[2]
    cols, rows, cs = _gate_prep(bcum, imb, amax, m0)
    ml = _mlstm(p3, cols, rows, cs, w["gn3"], None if state is None else state[:2])
    amat, invc = _pool_constants(seq_len, rows_2d)
    n_seq = max(1, 1024 // seq_len)
    yb = _pool(p3, amat, invc, w["wpg"], w["pscale"], seq_len, n_seq, rows_2d)
    ya = ml if state is not None else ml[0]
    x1, h2p, info = _mix(x2d, mod3, mod_row_fn, ya, yb, p3, w["wa"], w["wb"], w["wo"],
                         w["g_norm2"], w["wr"], w["br"])
    y, w_bf16 = _moe(x1, h2p, info, mod3, mod_row_fn, w["wg"], w["wu"], w["wd"], w["g_final"])
    return y.reshape(bsz, t, D_MODEL), ml, cs, w_bf16


def kernel(x_prompt, x_sample, state_C, state_n, state_m, c, c_ctx, w_mod, b_mod, g_norm1, w_in, b_in, gn_gain, w_pool_grp, pool_scale, w_proj_a, w_proj_b, w_out, g_norm2, w_router_grp, b_router_grp, w_router_exp, b_router_exp, w_exp_gate, w_exp_up, w_exp_down, g_final):
    bp, tp, _ = x_prompt.shape
    bs, ts, _ = x_sample.shape
    assert w_mod.shape[0] == 1, "single trunk layer"
    assert tp == CHUNK and ts == ML_ROWS and bs % ML_SEQS == 0

    cvec = jnp.zeros((8, D_MODEL), F32).at[0].set(c_ctx).at[1:1 + bs].set(c)
    mod3 = _modulation(cvec, w_mod[0], b_mod[0])

    wi, bi = w_in[0], b_in[0]
    d = D_MODEL
    o_q, o_k, o_v, o_o, o_g, o_p, o_ga, o_gb = 0, d, 2 * d, 3 * d, 4 * d, 4 * d + 32, 4 * d + 32 + 512, 4 * d + 32 + 512 + d
    order = [(o_q, d), (o_k, d), (o_v, d), (o_o, d), (o_ga, d), (o_gb, d), (o_p, 512)]
    wmain = _pack_columns(wi, order)
    bmain = jnp.concatenate([bi[s:s + l] for s, l in order]).reshape(1, D_MAIN)
    colscale = jnp.ones((D_MAIN,), F32).at[SLAB_K * LANES:(SLAB_K + N_HEADS) * LANES].set(HEAD_DIM ** -0.5)
    colscale = colscale.reshape(1, D_MAIN)
    wg = _gate_weights(wi, o_g)
    gb = bi[o_g:o_g + 32]
    i_f, f_f, i_b, f_b = (slice(0, 8), slice(8, 16), slice(16, 24), slice(24, 32))
    pad_b = jnp.zeros((LANES - 2 * N_HD,), F32)
    bg = jnp.concatenate([gb[f_f], gb[f_b], gb[i_f], gb[i_b], pad_b]).reshape(1, LANES)

    wr = jnp.concatenate([w_router_grp[0], w_router_exp[0].reshape(d, N_EXPERTS),
                          jnp.zeros((d, LANES - ROUTER_ROWS), F32)], axis=1)
    br = jnp.concatenate([b_router_grp[0], b_router_exp[0].reshape(N_EXPERTS),
                          jnp.zeros((LANES - ROUTER_ROWS,), F32)]).reshape(1, LANES)
    wr = jnp.concatenate(_split_bf16(wr), axis=1)

    w = dict(
        g_norm1=g_norm1[0].reshape(1, d), wmain=wmain, bmain=bmain, colscale=colscale,
        w_gates=wg, b_gates=bg,
        gn3=gn_gain[0].reshape(N_HEADS, 1, HEAD_DIM),
        wpg=w_pool_grp[0].astype(BF16), pscale=pool_scale[0].reshape(N_GROUPS, 1, LANES),
        wa=w_proj_a[0].astype(BF16), wb=w_proj_b[0].astype(BF16), wo=w_out[0].astype(BF16),
        g_norm2=g_norm2[0].reshape(1, d), wr=wr, br=br,
        wg=w_exp_gate[0], wu=w_exp_up[0], wd=w_exp_down[0],
        g_final=g_final.reshape(1, d),
    )

    y_prompt, (_, cst, nst), cs_prompt, w_bf16 = _stream(x_prompt, mod3, lambda r: 0, tp, None, None, w)
    w = dict(w, wg=w_bf16[0], wu=w_bf16[1], wd=w_bf16[2])
    n0 = state_n[:, 0].reshape(bs, 2, N_HEADS, 1, HEAD_DIM)
    m0 = jnp.pad(state_m[:, 0].reshape(bs, N_HD), ((0, 0), (0, LANES - N_HD))).reshape(bs, 1, LANES)
    y_sample, _, _, _ = _stream(x_sample, mod3, lambda r: 1 + r // ts, ts, ts // GRID_W, (state_C, n0, m0), w)

    new_c = cst
    new_n = nst.reshape(bp, 1, 2, N_HEADS, HEAD_DIM)
    new_m = cs_prompt[:, 1, :N_HD].reshape(bp, 1, 2, N_HEADS)
    return (y_prompt, y_sample, new_c, new_n, new_m)
```

```python
import functools

import numpy as np
import jax
import jax.numpy as jnp
from jax import lax
from jax.experimental import pallas as pl
from jax.experimental.pallas import tpu as pltpu
from jax.experimental.pallas import tpu_sc as plsc

F32 = jnp.float32
BF16 = jnp.bfloat16

D_MODEL = 1024
N_HEADS = 8
HEAD_DIM = 128
GRID_W = 64
POOL_WINDOWS = (2, 4, 8, 16)
N_GROUPS = 4
EXPERTS_PER_GROUP = 4
N_EXPERTS = 16
D_EXPERT = 512
EPS = 1e-6
NEG = -1e30

LANES = 128
CHUNK = 256
N_SLABS = 52
SLAB_Q, SLAB_K, SLAB_V, SLAB_O, SLAB_GA, SLAB_GB, SLAB_POOL = 0, 8, 16, 24, 32, 40, 48
D_MAIN = N_SLABS * LANES
VMEM_LIMIT = 56 * 1024 * 1024


def _cparams(sem):
    return pltpu.CompilerParams(dimension_semantics=sem, vmem_limit_bytes=VMEM_LIMIT)


def _const_spec(shape):
    nd = len(shape)
    return pl.BlockSpec(shape, lambda *_: (0,) * nd, pipeline_mode=pl.Buffered(1))


def _split_bf16(x):
    hi = x.astype(BF16)
    lo = (x - hi.astype(F32)).astype(BF16)
    return hi, lo


def _dot(a, b):
    return jnp.dot(a, b, preferred_element_type=F32)


def _dot3(a, b_hi, b_lo):
    a_hi, a_lo = _split_bf16(a)
    return _dot(a_hi, b_hi) + _dot(a_hi, b_lo) + _dot(a_lo, b_hi)


def _dot3_packed(a_hi, a_lo, b_hilo):
    first = _dot(a_hi, b_hilo)
    return first[:, :LANES] + first[:, LANES:] + _dot(a_lo, b_hilo[:, :LANES])


def _rms_scale(x):
    return lax.rsqrt(jnp.mean(x * x, axis=-1, keepdims=True) + EPS)


def _log_sigmoid(x):
    return jnp.minimum(x, 0.0) - jnp.log1p(jnp.exp(-jnp.abs(x)))


def _mod_kernel(c_ref, w_ref, b_ref, o_ref):
    c = c_ref[...]
    s = c * jax.nn.sigmoid(c)
    w_hi, w_lo = _split_bf16(w_ref[...])
    o_ref[...] = _dot3(s, w_hi, w_lo) + b_ref[...]


def _modulation(cvec, w_mod, b_mod):
    tn = 2048
    out = pl.pallas_call(
        _mod_kernel,
        out_shape=jax.ShapeDtypeStruct((8, 6 * D_MODEL), F32),
        grid=(6 * D_MODEL // tn,),
        in_specs=[
            pl.BlockSpec((8, D_MODEL), lambda j: (0, 0)),
            pl.BlockSpec((D_MODEL, tn), lambda j: (0, j)),
            pl.BlockSpec((1, tn), lambda j: (0, j)),
        ],
        out_specs=pl.BlockSpec((8, tn), lambda j: (0, j)),
        compiler_params=_cparams(("parallel",)),
        name="modulation",
    )(cvec, w_mod, b_mod.reshape(1, -1))
    return out.reshape(8, 6, D_MODEL)


PACK_COLS = 512


def _pack_kernel(wt_ref, o_ref):
    o_ref[...] = wt_ref[...].T.astype(BF16)


def _pack_columns(w, order):
    k, n = w.shape
    starts = []
    for src, width in order:
        assert width % PACK_COLS == 0
        starts += [src + j for j in range(0, width, PACK_COLS)]
    n_out = len(starts) * PACK_COLS

    unit = 32
    assert all(s % unit == 0 for s in starts)

    def src_row(j):
        row = jnp.int32(starts[-1] // unit)
        for idx in range(len(starts) - 2, -1, -1):
            row = jnp.where(j == idx, starts[idx] // unit, row)
        return row * unit

    return pl.pallas_call(
        _pack_kernel,
        out_shape=jax.ShapeDtypeStruct((k, n_out), BF16),
        grid=(len(starts),),
        in_specs=[pl.BlockSpec((pl.Element(PACK_COLS), pl.Element(k)), lambda j: (src_row(j), 0))],
        out_specs=pl.BlockSpec((k, PACK_COLS), lambda j: (0, j)),
        compiler_params=_cparams(("parallel",)),
        name="pack_columns",
    )(w.T)


def _gate_weights_kernel(wt_ref, o_ref):
    g = wt_ref[...]
    pad = jnp.zeros((LANES - g.shape[0], g.shape[1]), F32)
    w = jnp.concatenate([g[8:16], g[24:32], g[0:8], g[16:24], pad], axis=0).T
    o_ref[...] = jnp.concatenate(_split_bf16(w), axis=1)


def _gate_weights(w, start):
    k = w.shape[0]
    return pl.pallas_call(
        _gate_weights_kernel,
        out_shape=jax.ShapeDtypeStruct((k, 2 * LANES), BF16),
        grid=(1,),
        in_specs=[pl.BlockSpec((pl.Element(4 * N_HEADS), pl.Element(k)), lambda j: (start, 0))],
        out_specs=pl.BlockSpec((k, 2 * LANES), lambda j: (0, 0)),
        name="gate_weights",
    )(w.T)


IN_TM = 512
IN_TN = 512
LOG2E = 1.4426950408889634
N_HD = 2 * N_HEADS


def _fwd_lanes(shape):
    lane = lax.broadcasted_iota(jnp.int32, shape, len(shape) - 1)
    return (lane & N_HEADS) == 0


def _inproj_kernel(x_ref, mod_ref, g_ref, w_ref, b_ref, cs_ref, wg_ref, bg_ref,
                   p3_ref, bcum_ref, imb_ref, amax_ref, *, bias_starts):
    x = x_ref[...]
    h = x * _rms_scale(x) * g_ref[...]
    h = h * (1.0 + mod_ref[0, 1:2, :]) + mod_ref[0, 0:1, :]
    hb = h.astype(BF16)
    for n in range(D_MAIN // IN_TN):
        sl = slice(n * IN_TN, (n + 1) * IN_TN)
        bias = b_ref[:, bias_starts[n]:bias_starts[n] + IN_TN]
        acc = (_dot(hb, w_ref[:, sl]) + bias) * cs_ref[:, sl]
        accb = acc.astype(BF16)
        for j in range(IN_TN // LANES):
            p3_ref[n * (IN_TN // LANES) + j] = accb[:, j * LANES:(j + 1) * LANES]

    h_lo = (h - hb.astype(F32)).astype(BF16)
    gates = _dot3_packed(hb, h_lo, wg_ref[...]) + bg_ref[...]
    ls = _log_sigmoid(gates)
    gi = pltpu.roll(gates, LANES - N_HD, 1)
    row = lax.broadcasted_iota(jnp.int32, (CHUNK, LANES), 0)
    fwd = _fwd_lanes((CHUNK, LANES))
    for c in range(IN_TM // CHUNK):
        rows = slice(c * CHUNK, (c + 1) * CHUNK)
        cf = ls[rows]
        cb = cf
        s = 1
        while s < CHUNK:
            cf = cf + jnp.where(row >= s, pltpu.roll(cf, s, 0), 0.0)
            cb = cb + jnp.where(row < CHUNK - s, pltpu.roll(cb, CHUNK - s, 0), 0.0)
            s *= 2
        b = jnp.where(fwd, cf, cb)
        imb = gi[rows] - b
        mf = imb
        mb = imb
        s = 1
        while s < CHUNK:
            mf = jnp.maximum(mf, jnp.where(row >= s, pltpu.roll(mf, s, 0), -jnp.inf))
            mb = jnp.maximum(mb, jnp.where(row < CHUNK - s, pltpu.roll(mb, CHUNK - s, 0), -jnp.inf))
            s *= 2
        bcum_ref[rows, :] = b
        imb_ref[rows, :] = imb
        amax_ref[rows, :] = b + jnp.where(fwd, mf, mb)


def _in_projection(x2d, mod3, mod_row_fn, g_norm1, wmain, b_all, bias_starts, colscale, wg, bg):
    n = x2d.shape[0]
    assert len(bias_starts) == D_MAIN // IN_TN
    return pl.pallas_call(
        functools.partial(_inproj_kernel, bias_starts=bias_starts),
        out_shape=(jax.ShapeDtypeStruct((N_SLABS, n, LANES), BF16),) + (jax.ShapeDtypeStruct((n, LANES), F32),) * 3,
        grid=(n // IN_TM,),
        in_specs=[
            pl.BlockSpec((IN_TM, D_MODEL), lambda i: (i, 0)),
            pl.BlockSpec((1, 6, D_MODEL), lambda i: (mod_row_fn(i * IN_TM), 0, 0)),
            _const_spec((1, D_MODEL)),
            _const_spec((D_MODEL, D_MAIN)),
            _const_spec(b_all.shape),
            _const_spec((1, D_MAIN)),
            _const_spec((D_MODEL, 2 * LANES)),
            _const_spec((1, LANES)),
        ],
        out_specs=(pl.BlockSpec((N_SLABS, IN_TM, LANES), lambda i: (0, i, 0)),)
        + (pl.BlockSpec((IN_TM, LANES), lambda i: (i, 0)),) * 3,
        compiler_params=_cparams(("parallel",)),
        name="in_projection",
    )(x2d, mod3, g_norm1, wmain, b_all, colscale, wg, bg)


GP_ROWS = 1024
GP_NCH = GP_ROWS // CHUNK
Q_C2, Q_EM, Q_WI, Q_WC = 0, 1, 2, 3


def _gate_prep_kernel(*refs, carry):
    if carry:
        b_ref, imb_ref, a_ref, m0_ref, cols_ref, rw_ref, cs_ref = refs
    else:
        b_ref, imb_ref, a_ref, cols_ref, rw_ref, cs_ref = refs
    fwd1 = _fwd_lanes((1, LANES))
    lane = lax.broadcasted_iota(jnp.int32, (CHUNK, LANES), 1)
    neg = jnp.full((1, LANES), NEG, F32)

    def at_scan_end(ref, c):
        return jnp.where(fwd1, ref[(c + 1) * CHUNK - 1:(c + 1) * CHUNK, :], ref[c * CHUNK:c * CHUNK + 1, :])

    b_last = [at_scan_end(b_ref, c) for c in range(GP_NCH)]
    a_last = [at_scan_end(a_ref, c) for c in range(GP_NCH)]
    m_start = m0_ref[0] if carry else neg
    m_prev = [[None] * GP_NCH, [None] * GP_NCH]
    m_new = [[None] * GP_NCH, [None] * GP_NCH]
    for d, order in enumerate((range(GP_NCH), range(GP_NCH - 1, -1, -1))):
        m = m_start
        for c in order:
            m_prev[d][c] = m
            m_new[d][c] = jnp.maximum(b_last[c] + m, a_last[c])
            m = m_new[d][c] if carry else neg

    row_src = []
    for c in range(GP_NCH):
        rows = slice(c * CHUNK, (c + 1) * CHUNK)
        mp = jnp.where(fwd1, m_prev[0][c], m_prev[1][c])
        mn = jnp.where(fwd1, m_new[0][c], m_new[1][c])
        b = b_ref[rows, :]
        imb = imb_ref[rows, :]
        inter = b + mp
        m_t = jnp.maximum(inter, a_ref[rows, :])
        c2 = (b - m_t) * LOG2E
        em = jnp.exp(-m_t)
        wc = jnp.exp(b_last[c] - mn + imb)
        packed = jnp.where(lane < N_HD, c2, pltpu.roll(em, N_HD * Q_EM, 1))
        if carry:
            wi = jnp.exp(inter - m_t)
            packed = jnp.where(lane < N_HD * Q_WI, packed, pltpu.roll(wi, N_HD * Q_WI, 1))
        cols_ref[rows, :] = jnp.where(lane < N_HD * Q_WC, packed, pltpu.roll(wc, N_HD * Q_WC, 1))
        row_src.append(jnp.where(lane < N_HD, imb * (-LOG2E), pltpu.roll(wc, N_HD, 1)))
        cs_ref[c, 0:1, :] = jnp.exp(b_last[c] + mp - mn)
        cs_ref[c, 1:2, :] = mn

    rows_t = jnp.concatenate(row_src, axis=0).T
    for r in range(2 * N_HD):
        rw_ref[r] = rows_t[r:r + 1]


def _gate_prep(bcum, imb, amax, m0=None):
    n = bcum.shape[0]
    carry = m0 is not None
    blk = pl.BlockSpec((GP_ROWS, LANES), lambda i: (i, 0))
    in_specs = [blk, blk, blk]
    args = [bcum, imb, amax]
    if carry:
        in_specs.append(pl.BlockSpec((1, 1, LANES), lambda i: (i, 0, 0)))
        args.append(m0)
    return pl.pallas_call(
        functools.partial(_gate_prep_kernel, carry=carry),
        out_shape=(jax.ShapeDtypeStruct((n, LANES), F32), jax.ShapeDtypeStruct((2 * N_HD, 1, n), F32),
                   jax.ShapeDtypeStruct((n // CHUNK, 2, LANES), F32)),
        grid=(n // GP_ROWS,),
        in_specs=in_specs,
        out_specs=(blk, pl.BlockSpec((2 * N_HD, 1, GP_ROWS), lambda i: (0, 0, i)),
                   pl.BlockSpec((GP_NCH, 2, LANES), lambda i: (i, 0, 0))),
        compiler_params=_cparams(("parallel",)),
        name="gate_prep",
    )(*args)


ML_ROWS = 1024
ML_SEQS = 4
ML_ROWS_FREE = 4096


def _pick_col(blk, lane_iota, idx):
    return jnp.sum(jnp.where(lane_iota == idx, blk, 0.0), axis=1, keepdims=True)


def _mlstm_kernel(*refs, carry, n_chunks, n_seqs):
    if carry:
        (q_ref, k_ref, v_ref, o_ref, cols_ref, r2f_ref, r2b_ref, gn_ref, cs_ref, c0_ref, n0_ref,
         ya_ref, s_scr, h_scr) = refs
    else:
        (q_ref, k_ref, v_ref, o_ref, cols_ref, r2f_ref, r2b_ref, gn_ref, wrf_ref, wrb_ref,
         ya_ref, cst_ref, nst_ref, s_scr, h_scr) = refs
    head = pl.program_id(1)
    lane = lax.broadcasted_iota(jnp.int32, (CHUNK, LANES), 1)
    lane1 = lax.broadcasted_iota(jnp.int32, (1, LANES), 1)
    r_idx = lax.broadcasted_iota(jnp.int32, (CHUNK, CHUNK), 0)
    c_idx = lax.broadcasted_iota(jnp.int32, (CHUNK, CHUNK), 1)
    ones = jnp.ones((CHUNK, HEAD_DIM), BF16)
    tn_dims = (((0,), (0,)), ((), ()))

    for d in range(2):
        mask = (c_idx <= r_idx) if d == 0 else (c_idx >= r_idx)
        r2_ref = r2f_ref if d == 0 else r2b_ref
        base = N_HEADS * d + head
        per_seq = n_chunks // n_seqs
        local = range(per_seq) if d == 0 else range(per_seq - 1, -1, -1)
        for sq, step, c in [(sq, step, sq * per_seq + lc) for sq in range(n_seqs) for step, lc in enumerate(local)]:
            if carry and step == 0:
                n_rep = jnp.broadcast_to(n0_ref[sq, d, 0], (HEAD_DIM, HEAD_DIM)).T
                cn = jnp.concatenate([c0_ref[sq, 0, d, 0], n_rep], axis=1)
            rows = slice(c * CHUNK, (c + 1) * CHUNK)
            q = q_ref[0, rows, :]
            k = k_ref[0, rows, :]
            v = v_ref[0, rows, :]
            v1 = jnp.concatenate([v, ones], axis=1)
            if d == 0:
                s = lax.dot_general(q, k, (((1,), (1,)), ((), ())), preferred_element_type=F32)
                s_scr[rows, :] = s
            else:
                s = s_scr[rows, :]
            blk = cols_ref[rows, :]
            arg = _pick_col(blk, lane, base + N_HD * Q_C2) - r2_ref[0, :, rows]
            sw = s * jnp.exp2(jnp.where(mask, arg, NEG))
            nd = _dot(sw.astype(BF16), v1)
            if carry:
                nd = nd + _pick_col(blk, lane, base + N_HD * Q_WI) * _dot(q, cn.astype(BF16))
            hdir = nd[:, :HEAD_DIM] / jnp.maximum(jnp.abs(nd[:, HEAD_DIM:]), _pick_col(blk, lane, base + N_HD * Q_EM))
            if d == 0:
                h_scr[rows, :] = hdir
            else:
                h_scr[rows, :] = h_scr[rows, :] + hdir

            if carry and step == per_seq - 1:
                continue
            w_col = _pick_col(blk, lane, base + N_HD * Q_WC)
            if carry:
                u = lax.dot_general(k, (w_col * v1.astype(F32)).astype(BF16), tn_dims,
                                    preferred_element_type=F32)
                decay = jnp.sum(jnp.where(lane1 == base, cs_ref[c, 0:1, :], 0.0), axis=1, keepdims=True)
                cn = decay * cn + u
            else:
                w_row = (wrf_ref if d == 0 else wrb_ref)[0, :, rows]
                w8 = jnp.broadcast_to(w_row, (8, CHUNK)).astype(BF16)
                cst_ref[c, 0, d, 0] = lax.dot_general(k, (w_col * v.astype(F32)).astype(BF16), tn_dims,
                                                      preferred_element_type=F32)
                nst_ref[c, d, 0] = _dot(w8, k)[0:1, :]

    hh = h_scr[...]
    hn = hh * _rms_scale(hh) * gn_ref[0]
    ya_ref[0] = (hn * jax.nn.sigmoid(o_ref[0].astype(F32))).astype(BF16)


def _mlstm(p3, cols, rows, cs, gn3, state=None):
    n = p3.shape[1]
    carry = state is not None
    n_seqs = ML_SEQS if carry else 1
    rows_step = ML_ROWS * ML_SEQS if carry else ML_ROWS_FREE
    n_chunks = rows_step // CHUNK
    nblk = n // rows_step

    def slab(base):
        return pl.BlockSpec((1, rows_step, LANES), lambda g, h: (base + h, g, 0))

    def row(base):
        return pl.BlockSpec((1, 1, rows_step), lambda g, h: (base + h, 0, g))

    in_specs = [
        slab(SLAB_Q), slab(SLAB_K), slab(SLAB_V), slab(SLAB_O),
        pl.BlockSpec((rows_step, LANES), lambda g, h: (g, 0)),
        row(0), row(N_HEADS),
        pl.BlockSpec((1, 1, LANES), lambda g, h: (h, 0, 0)),
    ]
    args = [p3, p3, p3, p3, cols, rows, rows, gn3]
    ya_shape = jax.ShapeDtypeStruct((N_HEADS, n, LANES), BF16)
    ya_spec = pl.BlockSpec((1, rows_step, LANES), lambda g, h: (h, g, 0))
    state_spec = pl.BlockSpec((n_seqs, 1, 2, 1, HEAD_DIM, HEAD_DIM), lambda g, h: (g, 0, 0, h, 0, 0))
    if carry:
        c0, n0 = state
        in_specs += [pl.BlockSpec((n_chunks, 2, LANES), lambda g, h: (g, 0, 0)), state_spec,
                     pl.BlockSpec((n_seqs, 2, 1, 1, HEAD_DIM), lambda g, h: (g, 0, h, 0, 0))]
        args += [cs, c0, n0]
        out_shape = ya_shape
        out_specs = ya_spec
    else:
        in_specs += [row(N_HD), row(N_HD + N_HEADS)]
        args += [rows, rows]
        nseq = n // CHUNK
        out_shape = (ya_shape,
                     jax.ShapeDtypeStruct((nseq, 1, 2, N_HEADS, HEAD_DIM, HEAD_DIM), F32),
                     jax.ShapeDtypeStruct((nseq, 2, N_HEADS, 1, HEAD_DIM), F32))
        out_specs = (ya_spec,
                     pl.BlockSpec((n_chunks, 1, 2, 1, HEAD_DIM, HEAD_DIM), lambda g, h: (g, 0, 0, h, 0, 0)),
                     pl.BlockSpec((n_chunks, 2, 1, 1, HEAD_DIM), lambda g, h: (g, 0, h, 0, 0)))
    return pl.pallas_call(
        functools.partial(_mlstm_kernel, carry=carry, n_chunks=n_chunks, n_seqs=n_seqs),
        out_shape=out_shape,
        grid=(nblk, N_HEADS),
        in_specs=in_specs,
        out_specs=out_specs,
        scratch_shapes=[pltpu.VMEM((rows_step, CHUNK), F32), pltpu.VMEM((rows_step, HEAD_DIM), F32)],
        compiler_params=_cparams(("parallel", "parallel")),
        name="mlstm_latent" if carry else "mlstm_prompt",
    )(*args)


def _pool_kernel(u_ref, a_ref, ic_ref, wpg_ref, ps_ref, yb_ref, *, seq_len, n_seq, rows_2d):
    piece = seq_len if rows_2d is None else GRID_W
    n_piece = n_seq if rows_2d is None else rows_2d
    for g in range(N_GROUPS):
        u = jnp.concatenate([u_ref[g, i * piece:(i + 1) * piece, :] for i in range(n_piece)], axis=1)
        win = _dot(a_ref[g], u)
        if rows_2d is None:
            inv_cnt = jnp.concatenate([ic_ref[g]] * n_piece, axis=1)
        else:
            run = [jnp.zeros((piece, LANES), F32)]
            for r in range(rows_2d):
                run.append(run[-1] + win[:, r * LANES:(r + 1) * LANES])
            half = POOL_WINDOWS[g] // 2
            win = jnp.concatenate([run[min(r - half + POOL_WINDOWS[g], rows_2d)] - run[max(r - half, 0)]
                                   for r in range(rows_2d)], axis=1)
            inv_cnt = jnp.concatenate([ic_ref[g, r * piece:(r + 1) * piece, :] for r in range(rows_2d)], axis=1)
        p = (win * inv_cnt - u.astype(F32)).astype(BF16)
        p = jnp.concatenate([p[:, i * LANES:(i + 1) * LANES] for i in range(n_piece)], axis=0)
        yb_ref[g] = (_dot(p, wpg_ref[g]) * ps_ref[g]).astype(BF16)


def _pool(p3, amat, invc, wpg, pscale, seq_len, n_seq, rows_2d):
    n = p3.shape[1]
    rows = seq_len * n_seq
    return pl.pallas_call(
        functools.partial(_pool_kernel, seq_len=seq_len, n_seq=n_seq, rows_2d=rows_2d),
        out_shape=jax.ShapeDtypeStruct((N_GROUPS, n, LANES), BF16),
        grid=(n // rows,),
        in_specs=[
            pl.BlockSpec((N_GROUPS, rows, LANES), lambda i: (SLAB_POOL // N_GROUPS, i, 0)),
            _const_spec(amat.shape),
            _const_spec((N_GROUPS, seq_len, LANES)),
            _const_spec((N_GROUPS, LANES, LANES)),
            _const_spec((N_GROUPS, 1, LANES)),
        ],
        out_specs=pl.BlockSpec((N_GROUPS, rows, LANES), lambda i: (0, i, 0)),
        compiler_params=_cparams(("parallel",)),
        name="pool",
    )(p3, amat, invc, wpg, pscale)


def _window_matrix(size, w):
    idx = np.arange(size)
    lo = np.clip(idx - w // 2, 0, size)
    hi = np.clip(idx - w // 2 + w, 0, size)
    s = np.arange(size)[None, :]
    a = ((s >= lo[:, None]) & (s < hi[:, None])).astype(np.float32)
    return a, (hi - lo).astype(np.float32)


def _pool_constants(seq_len, rows_2d):
    mats, invs = [], []
    for w in POOL_WINDOWS:
        if rows_2d is None:
            a, cnt = _window_matrix(seq_len, w)
        else:
            _, cr = _window_matrix(rows_2d, w)
            a, cc = _window_matrix(GRID_W, w)
            cnt = np.kron(cr, cc)
        mats.append(a)
        invs.append(np.repeat((1.0 / cnt)[:, None], LANES, axis=1))
    return (jnp.asarray(np.stack(mats), dtype=BF16), jnp.asarray(np.stack(invs), dtype=F32))


MIX_TM = 512


def _mix_kernel(x_ref, mod_ref, ya_ref, yb_ref, ga_ref, gb_ref, wa_ref, wb_ref, wo_ref,
                g2_ref, wr_ref, br_ref, x1_ref, h2_ref, info_ref):
    ya = jnp.concatenate([ya_ref[i] for i in range(N_HEADS)], axis=1)
    yb = jnp.concatenate([yb_ref[i] for i in range(N_GROUPS)], axis=1)
    ga = jnp.concatenate([ga_ref[i] for i in range(8)], axis=1).astype(F32)
    gb = jnp.concatenate([gb_ref[i] for i in range(8)], axis=1).astype(F32)
    merged = jax.nn.sigmoid(ga) * _dot(ya, wa_ref[...]) + jax.nn.sigmoid(gb) * _dot(yb, wb_ref[...])
    out = _dot(merged.astype(BF16), wo_ref[...])
    x1 = x_ref[...] + mod_ref[0, 2:3, :] * out
    x1_ref[...] = x1

    h2 = x1 * _rms_scale(x1) * g2_ref[...]
    h2 = h2 * (1.0 + mod_ref[0, 4:5, :]) + mod_ref[0, 3:4, :]
    h_hi, h_lo = _split_bf16(h2)
    packed = _pack_bf16_pairs(h_hi.astype(F32))
    h2_ref[0:MIX_TM, :] = packed[:, :SC_ROW]
    h2_ref[MIX_TM:2 * MIX_TM, :] = packed[:, SC_ROW:]
    logits = _dot3_packed(h_hi, h_lo, wr_ref[...]) + br_ref[...]
    first, second, w1, w2 = _route_rows(logits.T)
    zero = jnp.zeros_like(w1)
    info_ref[...] = jnp.concatenate([first, second, w1, w2, zero, zero, zero, zero], axis=0)


def _mix(x2d, mod3, mod_row_fn, ya, yb, p3, wa, wb, wo, g_norm2, wr, br):
    n = x2d.shape[0]
    return pl.pallas_call(
        _mix_kernel,
        out_shape=(jax.ShapeDtypeStruct((n, D_MODEL), F32),
                   jax.ShapeDtypeStruct((2 * n, SC_ROW), jnp.int32), jax.ShapeDtypeStruct((8, n), F32)),
        grid=(n // MIX_TM,),
        in_specs=[
            pl.BlockSpec((MIX_TM, D_MODEL), lambda i: (i, 0)),
            pl.BlockSpec((1, 6, D_MODEL), lambda i: (mod_row_fn(i * MIX_TM), 0, 0)),
            pl.BlockSpec((N_HEADS, MIX_TM, LANES), lambda i: (0, i, 0)),
            pl.BlockSpec((N_GROUPS, MIX_TM, LANES), lambda i: (0, i, 0)),
            pl.BlockSpec((8, MIX_TM, LANES), lambda i: (SLAB_GA // 8, i, 0)),
            pl.BlockSpec((8, MIX_TM, LANES), lambda i: (SLAB_GB // 8, i, 0)),
            _const_spec((D_MODEL, D_MODEL)),
            _const_spec((N_GROUPS * LANES, D_MODEL)),
            _const_spec((D_MODEL, D_MODEL)),
            _const_spec((1, D_MODEL)),
            _const_spec((D_MODEL, 2 * LANES)),
            _const_spec((1, LANES)),
        ],
        out_specs=(pl.BlockSpec((MIX_TM, D_MODEL), lambda i: (i, 0)),
                   pl.BlockSpec((2 * MIX_TM, SC_ROW), lambda i: (i, 0)),
                   pl.BlockSpec((8, MIX_TM), lambda i: (0, i))),
        compiler_params=_cparams(("parallel",)),
        name="mix",
    )(x2d, mod3, ya, yb, p3, p3, wa, wb, wo, g_norm2, wr, br)


PLAN_TB = 1024
ROUTER_ROWS = 4 + N_EXPERTS
SLOT_TM = 1024
SC_ROW = 256
SC_WIN = 128
HI_MASK = -65536


def _pack_bf16_pairs(x):
    bits = pltpu.bitcast(x, jnp.int32)
    half = D_MODEL // 2
    return jnp.bitwise_or(jnp.bitwise_and(lax.shift_right_logical(bits[:, :half], 16), 0xFFFF),
                          jnp.bitwise_and(bits[:, half:], HI_MASK))


def _unpack_bf16_pairs(lo_words, hi_words):
    def low(w):
        return pltpu.bitcast(lax.shift_left(w, 16), F32)

    def high(w):
        return pltpu.bitcast(jnp.bitwise_and(w, HI_MASK), F32)

    return jnp.concatenate([low(lo_words), low(hi_words), high(lo_words), high(hi_words)], axis=1).astype(BF16)


def _route_rows(lt):
    lg = [lt[i:i + 1] for i in range(N_GROUPS)]
    mx = jnp.maximum(jnp.maximum(lg[0], lg[1]), jnp.maximum(lg[2], lg[3]))
    p_top = 1.0 / (jnp.exp(lg[0] - mx) + jnp.exp(lg[1] - mx) + jnp.exp(lg[2] - mx) + jnp.exp(lg[3] - mx))
    gidx = jnp.zeros(lg[0].shape, jnp.int32)
    best = lg[0]
    for i in range(1, N_GROUPS):
        upd = lg[i] > best
        gidx = jnp.where(upd, i, gidx)
        best = jnp.where(upd, lg[i], best)

    def expert_row(g, e):
        r = N_GROUPS + g * EXPERTS_PER_GROUP + e
        return lt[r:r + 1]

    le = [jnp.where(gidx == 0, expert_row(0, e),
                    jnp.where(gidx == 1, expert_row(1, e),
                              jnp.where(gidx == 2, expert_row(2, e), expert_row(3, e))))
          for e in range(EXPERTS_PER_GROUP)]
    v1 = jnp.maximum(jnp.maximum(le[0], le[1]), jnp.maximum(le[2], le[3]))
    i1 = jnp.full(v1.shape, EXPERTS_PER_GROUP - 1, jnp.int32)
    for e in range(EXPERTS_PER_GROUP - 2, -1, -1):
        i1 = jnp.where(le[e] == v1, e, i1)
    le2 = [jnp.where(i1 == e, -jnp.inf, le[e]) for e in range(EXPERTS_PER_GROUP)]
    v2 = jnp.maximum(jnp.maximum(le2[0], le2[1]), jnp.maximum(le2[2], le2[3]))
    i2 = jnp.full(v2.shape, EXPERTS_PER_GROUP - 1, jnp.int32)
    for e in range(EXPERTS_PER_GROUP - 2, -1, -1):
        i2 = jnp.where(le2[e] == v2, e, i2)
    e2 = jnp.exp(v2 - v1)
    first = (gidx * EXPERTS_PER_GROUP + i1).astype(F32)
    second = (gidx * EXPERTS_PER_GROUP + i2).astype(F32)
    return first, second, p_top / (1.0 + e2), p_top * e2 / (1.0 + e2)


def _plan_kernel(info_ref, before_ref, sidx_ref, gidx_ref, tmap_ref, rank_scr, count_scr, *, n_blocks, min_tiles,
                 slot_tm):
    step = pl.program_id(0)
    expert_id = lax.broadcasted_iota(jnp.int32, (N_EXPERTS, PLAN_TB), 0).astype(F32)

    @pl.when(step == 0)
    def _():
        count_scr[...] = jnp.zeros_like(count_scr)

    @pl.when(step < n_blocks)
    def _():
        cols = pl.ds(pl.multiple_of(step * PLAN_TB, PLAN_TB), PLAN_TB)
        sel = jnp.where((info_ref[0:1, cols] == expert_id) | (info_ref[1:2, cols] == expert_id), 1.0, 0.0)
        selb = sel.astype(BF16)
        rank_scr[:, cols] = _dot(selb, before_ref[...]) + count_scr[:, 0:1]
        count_scr[...] += _dot(selb, jnp.ones((PLAN_TB, LANES), BF16))

    @pl.when(step == n_blocks)
    def _():
        counts = count_scr[...]
        padded = jnp.floor((counts + (slot_tm - 1)) * (1.0 / slot_tm)) * slot_tm
        padded = jnp.maximum(padded, float(min_tiles * slot_tm))
        starts, ends, run = [], [], jnp.zeros((1, LANES), F32)
        for e in range(N_EXPERTS):
            starts.append(run)
            run = run + padded[e:e + 1]
            ends.append(run)
        n = info_ref.shape[1]
        shift = slot_tm.bit_length() - 1
        for k in range(2):
            chosen = info_ref[k:k + 1, :]
            slot = jnp.zeros((1, n), F32)
            for e in range(N_EXPERTS):
                slot = jnp.where(chosen == float(e), starts[e][:, 0:1] + rank_scr[e:e + 1, :], slot)
            slot = slot.astype(jnp.int32)
            low = jnp.left_shift(jnp.right_shift(slot, shift), shift + 1) + jnp.bitwise_and(slot, slot_tm - 1)
            for half in range(2):
                rows = low + half * slot_tm
                for c in range(n // LANES):
                    piece = rows[:, c * LANES:(c + 1) * LANES]
                    t, j = divmod(c, MIX_TM // LANES)
                    r = ((k * (n // MIX_TM) + t) * 2 + half) * (MIX_TM // LANES) + j
                    sidx_ref[r:r + 1, :] = piece
                    t, j = divmod(c, FIN_TM // LANES)
                    r = ((t * 2 + k) * 2 + half) * (FIN_TM // LANES) + j
                    gidx_ref[r:r + 1, :] = piece
        tile_start = lax.broadcasted_iota(jnp.int32, (1, LANES), 1).astype(F32) * slot_tm
        used = jnp.where(tile_start < run, 1.0, 0.0)
        tile_start = jnp.minimum(tile_start, run - slot_tm)
        owner = jnp.zeros((1, LANES), F32)
        for e in range(N_EXPERTS - 1):
            owner = owner + jnp.where(ends[e] <= tile_start, 1.0, 0.0)
        tile = tile_start * (1.0 / slot_tm)
        tmap_ref[...] = jnp.concatenate([owner, used, tile] + [jnp.zeros((1, LANES), F32)] * 5,
                                        axis=0).astype(jnp.int32)


def _plan(info, min_tiles, slot_tm):
    n = info.shape[1]
    n_blocks = n // PLAN_TB
    t_i = lax.broadcasted_iota(jnp.int32, (PLAN_TB, PLAN_TB), 0)
    t_j = lax.broadcasted_iota(jnp.int32, (PLAN_TB, PLAN_TB), 1)
    before = (t_i < t_j).astype(BF16)
    return pl.pallas_call(
        functools.partial(_plan_kernel, n_blocks=n_blocks, min_tiles=min_tiles, slot_tm=slot_tm),
        out_shape=(jax.ShapeDtypeStruct((4 * n // LANES, LANES), jnp.int32),
                   jax.ShapeDtypeStruct((4 * n // LANES, LANES), jnp.int32),
                   jax.ShapeDtypeStruct((8, LANES), jnp.int32)),
        grid=(n_blocks + 1,),
        in_specs=[_const_spec((8, n)), _const_spec((PLAN_TB, PLAN_TB))],
        out_specs=(pl.BlockSpec((4 * n // LANES, LANES), lambda i: (0, 0)),
                   pl.BlockSpec((4 * n // LANES, LANES), lambda i: (0, 0)),
                   pl.BlockSpec((8, LANES), lambda i: (0, 0))),
        scratch_shapes=[pltpu.VMEM((N_EXPERTS, n), F32), pltpu.VMEM((N_EXPERTS, LANES), F32)],
        compiler_params=_cparams(("arbitrary",)),
        name="plan",
    )(info, before)


def _sc_mesh():
    return plsc.VectorSubcoreMesh(core_axis_name="core", subcore_axis_name="subcore")


def _sc_scatter_rows(x, idx, n_out):
    n_idx = idx.shape[0]
    src_blocks = x.shape[0] // SC_WIN

    @pl.kernel(out_type=jax.ShapeDtypeStruct((n_out, SC_ROW), x.dtype), mesh=_sc_mesh(), scratch_types=[])
    def scatter_kernel(x_hbm, i_hbm, o_hbm):
        def body(x_vmem, i_vmem):
            pltpu.sync_copy(x_vmem, o_hbm.at[i_vmem.at[0]])

        pltpu.emit_pipeline(
            body,
            grid=(n_idx // SC_WIN,),
            in_specs=[pl.BlockSpec((SC_WIN, SC_ROW), index_map=lambda i: (i % src_blocks, 0)),
                      pl.BlockSpec((1, SC_WIN), index_map=lambda i: (0, i))],
            out_specs=[],
            core_axis_name=("core", "subcore"),
            dimension_semantics=(pltpu.PARALLEL,),
        )(x_hbm, i_hbm)

    return scatter_kernel(x, idx.reshape(1, n_idx))


def _sc_gather_rows(x, idx):
    n_idx = idx.shape[0]

    @pl.kernel(out_type=jax.ShapeDtypeStruct((n_idx, SC_ROW), x.dtype), mesh=_sc_mesh(), scratch_types=[])
    def gather_kernel(x_hbm, i_hbm, o_hbm):
        def body(i_vmem, o_vmem):
            pltpu.sync_copy(x_hbm.at[i_vmem.at[0]], o_vmem)

        pltpu.emit_pipeline(
            body,
            grid=(n_idx // SC_WIN,),
            in_specs=[pl.BlockSpec((1, SC_WIN), index_map=lambda i: (0, i))],
            out_specs=[pl.BlockSpec((SC_WIN, SC_ROW), index_map=lambda i: (i, 0))],
            core_axis_name=("core", "subcore"),
            dimension_semantics=(pltpu.PARALLEL,),
        )(i_hbm, o_hbm)

    return gather_kernel(x, idx.reshape(1, n_idx))


def _experts_kernel(tmap_ref, xs_ref, wg_ref, wu_ref, wd_ref, ys_ref, *wb_refs, cast_weights, slot_tm):
    j = pl.program_id(0)
    if cast_weights:
        wg_b, wu_b, wd_b = wb_refs

        @pl.when((j == 0) | (tmap_ref[0, j] != tmap_ref[0, jnp.maximum(j - 1, 0)]))
        def _():
            wg_b[0] = wg_ref[0].astype(BF16)
            wu_b[0] = wu_ref[0].astype(BF16)
            wd_b[0] = wd_ref[0].astype(BF16)
    else:
        wg_b, wu_b, wd_b = wg_ref, wu_ref, wd_ref

    @pl.when(tmap_ref[1, j] > 0)
    def _():
        x = _unpack_bf16_pairs(xs_ref[0:slot_tm, :], xs_ref[slot_tm:2 * slot_tm, :])
        a = _dot(x, wg_b[0])
        u = _dot(x, wu_b[0])
        act = (a * jax.nn.sigmoid(a) * u).astype(BF16)
        y = _dot(act, wd_b[0]).astype(BF16).astype(F32)
        packed = _pack_bf16_pairs(y)
        ys_ref[0:slot_tm, :] = packed[:, :SC_ROW]
        ys_ref[slot_tm:2 * slot_tm, :] = packed[:, SC_ROW:]


def _experts(xs, tmap, wg, wu, wd, slot_tm):
    n_tiles = xs.shape[0] // (2 * slot_tm)
    cast_weights = wg.dtype != BF16
    tile_spec = pl.BlockSpec((2 * slot_tm, SC_ROW), lambda j, tm: (tm[2, j], 0))
    w_specs = [pl.BlockSpec((1, D_MODEL, D_EXPERT), lambda j, tm: (tm[0, j], 0, 0)),
               pl.BlockSpec((1, D_MODEL, D_EXPERT), lambda j, tm: (tm[0, j], 0, 0)),
               pl.BlockSpec((1, D_EXPERT, D_MODEL), lambda j, tm: (tm[0, j], 0, 0))]
    out_shape = [jax.ShapeDtypeStruct(xs.shape, jnp.int32)]
    out_specs = [tile_spec]
    if cast_weights:
        out_shape += [jax.ShapeDtypeStruct(w.shape, BF16) for w in (wg, wu, wd)]
        out_specs += w_specs
    grid_spec = pltpu.PrefetchScalarGridSpec(
        num_scalar_prefetch=1,
        grid=(n_tiles,),
        in_specs=[tile_spec] + w_specs,
        out_specs=out_specs,
    )
    out = pl.pallas_call(
        functools.partial(_experts_kernel, cast_weights=cast_weights, slot_tm=slot_tm),
        out_shape=out_shape,
        grid_spec=grid_spec,
        compiler_params=_cparams(("arbitrary",)),
        name="experts",
    )(tmap, xs, wg, wu, wd)
    return out[0], tuple(out[1:])


FIN_TM = 1024


def _final_kernel(x1_ref, mod_ref, yg_ref, info_ref, gf_ref, y_ref):
    w_cols = info_ref[...].T
    moe = None
    for k in range(2):
        base = 2 * k * FIN_TM
        yk = _unpack_bf16_pairs(yg_ref[base:base + FIN_TM, :], yg_ref[base + FIN_TM:base + 2 * FIN_TM, :])
        term = w_cols[:, 2 + k:3 + k] * yk.astype(F32)
        moe = term if moe is None else moe + term
    x2 = x1_ref[...] + mod_ref[0, 5:6, :] * moe
    y_ref[...] = x2 * _rms_scale(x2) * gf_ref[...]


def _final(x1, mod3, mod_row_fn, yg, info, g_final):
    n = x1.shape[0]
    return pl.pallas_call(
        _final_kernel,
        out_shape=jax.ShapeDtypeStruct((n, D_MODEL), F32),
        grid=(n // FIN_TM,),
        in_specs=[
            pl.BlockSpec((FIN_TM, D_MODEL), lambda i: (i, 0)),
            pl.BlockSpec((1, 6, D_MODEL), lambda i: (mod_row_fn(i * FIN_TM), 0, 0)),
            pl.BlockSpec((4 * FIN_TM, SC_ROW), lambda i: (i, 0)),
            pl.BlockSpec((8, FIN_TM), lambda i: (0, i)),
            _const_spec((1, D_MODEL)),
        ],
        out_specs=pl.BlockSpec((FIN_TM, D_MODEL), lambda i: (i, 0)),
        compiler_params=_cparams(("parallel",)),
        name="final",
    )(x1, mod3, yg, info, g_final)


def _moe(x1, h2p, info, mod3, mod_row_fn, wg, wu, wd, g_final):
    n = x1.shape[0]
    slot_tm = min(SLOT_TM, 2 * n // N_EXPERTS)
    sidx, gidx, tmap = _plan(info, min_tiles=int(wg.dtype != BF16), slot_tm=slot_tm)
    n_slots = 2 * n + N_EXPERTS * slot_tm
    xs = _sc_scatter_rows(h2p, sidx.reshape(-1), 2 * n_slots)
    ys, w_bf16 = _experts(xs, tmap, wg, wu, wd, slot_tm)
    yg = _sc_gather_rows(ys, gidx.reshape(-1))
    return _final(x1, mod3, mod_row_fn, yg, info, g_final), w_bf16


def _stream(x, mod3, mod_row_fn, seq_len, rows_2d, state, w):
    bsz, t, _ = x.shape
    n = bsz * t
    x2d = x.reshape(n, D_MODEL)
    p3, bcum, imb, amax = _in_projection(x2d, mod3, mod_row_fn, w["g_norm1"], w["wmain"], w["b_in"], w["bias_starts"],
                                         w["colscale"], w["w_gates"], w["b_gates"])
    m0 = None if state is None else state[2]
    cols, rows, cs = _gate_prep(bcum, imb, amax, m0)
    ml = _mlstm(p3, cols, rows, cs, w["gn3"], None if state is None else state[:2])
    amat, invc = _pool_constants(seq_len, rows_2d)
    n_seq = max(1, 1024 // seq_len)
    yb = _pool(p3, amat, invc, w["wpg"], w["pscale"], seq_len, n_seq, rows_2d)
    ya = ml if state is not None else ml[0]
    x1, h2p, info = _mix(x2d, mod3, mod_row_fn, ya, yb, p3, w["wa"], w["wb"], w["wo"],
                         w["g_norm2"], w["wr"], w["br"])
    y, w_bf16 = _moe(x1, h2p, info, mod3, mod_row_fn, w["wg"], w["wu"], w["wd"], w["g_final"])
    return y.reshape(bsz, t, D_MODEL), ml, cs, w_bf16


def kernel(x_prompt, x_sample, state_C, state_n, state_m, c, c_ctx, w_mod, b_mod, g_norm1, w_in, b_in, gn_gain, w_pool_grp, pool_scale, w_proj_a, w_proj_b, w_out, g_norm2, w_router_grp, b_router_grp, w_router_exp, b_router_exp, w_exp_gate, w_exp_up, w_exp_down, g_final):
    bp, tp, _ = x_prompt.shape
    bs, ts, _ = x_sample.shape
    assert w_mod.shape[0] == 1, "single trunk layer"
    assert tp == CHUNK and ts == ML_ROWS and bs % ML_SEQS == 0

    cvec = jnp.zeros((8, D_MODEL), F32).at[0].set(c_ctx).at[1:1 + bs].set(c)
    mod3 = _modulation(cvec, w_mod[0], b_mod[0])

    wi, bi = w_in[0], b_in[0]
    d = D_MODEL
    o_q, o_k, o_v, o_o, o_g, o_p, o_ga, o_gb = 0, d, 2 * d, 3 * d, 4 * d, 4 * d + 32, 4 * d + 32 + 512, 4 * d + 32 + 512 + d
    order = [(o_q, d), (o_k, d), (o_v, d), (o_o, d), (o_ga, d), (o_gb, d), (o_p, 512)]
    wmain = _pack_columns(wi, order)
    bias_starts = tuple(s + j for s, l in order for j in range(0, l, IN_TN))
    colscale = jnp.ones((D_MAIN,), F32).at[SLAB_K * LANES:(SLAB_K + N_HEADS) * LANES].set(HEAD_DIM ** -0.5)
    colscale = colscale.reshape(1, D_MAIN)
    wg = _gate_weights(wi, o_g)
    gb = bi[o_g:o_g + 32]
    i_f, f_f, i_b, f_b = (slice(0, 8), slice(8, 16), slice(16, 24), slice(24, 32))
    pad_b = jnp.zeros((LANES - 2 * N_HD,), F32)
    bg = jnp.concatenate([gb[f_f], gb[f_b], gb[i_f], gb[i_b], pad_b]).reshape(1, LANES)

    wr = jnp.concatenate([w_router_grp[0], w_router_exp[0].reshape(d, N_EXPERTS),
                          jnp.zeros((d, LANES - ROUTER_ROWS), F32)], axis=1)
    br = jnp.concatenate([b_router_grp[0], b_router_exp[0].reshape(N_EXPERTS),
                          jnp.zeros((LANES - ROUTER_ROWS,), F32)]).reshape(1, LANES)
    wr = jnp.concatenate(_split_bf16(wr), axis=1)

    w = dict(
        g_norm1=g_norm1[0].reshape(1, d), wmain=wmain, b_in=b_in, bias_starts=bias_starts, colscale=colscale,
        w_gates=wg, b_gates=bg,
        gn3=gn_gain[0].reshape(N_HEADS, 1, HEAD_DIM),
        wpg=w_pool_grp[0].astype(BF16), pscale=pool_scale[0].reshape(N_GROUPS, 1, LANES),
        wa=w_proj_a[0].astype(BF16), wb=w_proj_b[0].astype(BF16), wo=w_out[0].astype(BF16),
        g_norm2=g_norm2[0].reshape(1, d), wr=wr, br=br,
        wg=w_exp_gate[0], wu=w_exp_up[0], wd=w_exp_down[0],
        g_final=g_final.reshape(1, d),
    )

    y_prompt, (_, cst, nst), cs_prompt, w_bf16 = _stream(x_prompt, mod3, lambda r: 0, tp, None, None, w)
    w = dict(w, wg=w_bf16[0], wu=w_bf16[1], wd=w_bf16[2])
    n0 = state_n[:, 0].reshape(bs, 2, N_HEADS, 1, HEAD_DIM)
    m0 = jnp.pad(state_m[:, 0].reshape(bs, N_HD), ((0, 0), (0, LANES - N_HD))).reshape(bs, 1, LANES)
    y_sample, _, _, _ = _stream(x_sample, mod3, lambda r: 1 + r // ts, ts, ts // GRID_W, (state_C, n0, m0), w)

    new_c = cst
    new_n = nst.reshape(bp, 1, 2, N_HEADS, HEAD_DIM)
    new_m = cs_prompt[:, 1, :N_HD].reshape(bp, 1, 2, N_HEADS)
    return (y_prompt, y_sample, new_c, new_n, new_m)
```

```python
import functools

import numpy as np
import jax
import jax.numpy as jnp
from jax import lax
from jax.experimental import pallas as pl
from jax.experimental.pallas import tpu as pltpu
from jax.experimental.pallas import tpu_sc as plsc

F32 = jnp.float32
BF16 = jnp.bfloat16

D_MODEL = 1024
N_HEADS = 8
HEAD_DIM = 128
GRID_W = 64
POOL_WINDOWS = (2, 4, 8, 16)
N_GROUPS = 4
EXPERTS_PER_GROUP = 4
N_EXPERTS = 16
D_EXPERT = 512
EPS = 1e-6
NEG = -1e30

LANES = 128
CHUNK = 256
N_SLABS = 52
SLAB_Q, SLAB_K, SLAB_V, SLAB_O, SLAB_GA, SLAB_GB, SLAB_POOL = 0, 8, 16, 24, 32, 40, 48
D_MAIN = N_SLABS * LANES
VMEM_LIMIT = 56 * 1024 * 1024


def _cparams(sem):
    return pltpu.CompilerParams(dimension_semantics=sem, vmem_limit_bytes=VMEM_LIMIT)


def _const_spec(shape):
    nd = len(shape)
    return pl.BlockSpec(shape, lambda *_: (0,) * nd, pipeline_mode=pl.Buffered(1))


def _split_bf16(x):
    hi = x.astype(BF16)
    lo = (x - hi.astype(F32)).astype(BF16)
    return hi, lo


def _dot(a, b):
    return jnp.dot(a, b, preferred_element_type=F32)


def _dot3(a, b_hi, b_lo):
    a_hi, a_lo = _split_bf16(a)
    return _dot(a_hi, b_hi) + _dot(a_hi, b_lo) + _dot(a_lo, b_hi)


def _dot3_packed(a_hi, a_lo, b_hilo):
    first = _dot(a_hi, b_hilo)
    return first[:, :LANES] + first[:, LANES:] + _dot(a_lo, b_hilo[:, :LANES])


def _rms_scale(x):
    return lax.rsqrt(jnp.mean(x * x, axis=-1, keepdims=True) + EPS)


def _log_sigmoid(x):
    return jnp.minimum(x, 0.0) - jnp.log1p(jnp.exp(-jnp.abs(x)))


def _mod_kernel(c_ref, w_ref, b_ref, o_ref):
    c = c_ref[...]
    s = c * jax.nn.sigmoid(c)
    w_hi, w_lo = _split_bf16(w_ref[...])
    o_ref[...] = _dot3(s, w_hi, w_lo) + b_ref[...]


def _modulation(cvec, w_mod, b_mod):
    tn = 2048
    out = pl.pallas_call(
        _mod_kernel,
        out_shape=jax.ShapeDtypeStruct((8, 6 * D_MODEL), F32),
        grid=(6 * D_MODEL // tn,),
        in_specs=[
            pl.BlockSpec((8, D_MODEL), lambda j: (0, 0)),
            pl.BlockSpec((D_MODEL, tn), lambda j: (0, j)),
            pl.BlockSpec((1, tn), lambda j: (0, j)),
        ],
        out_specs=pl.BlockSpec((8, tn), lambda j: (0, j)),
        compiler_params=_cparams(("parallel",)),
        name="modulation",
    )(cvec, w_mod, b_mod.reshape(1, -1))
    return out.reshape(8, 6, D_MODEL)


PACK_COLS = 512


def _pack_kernel(wt_ref, o_ref):
    o_ref[...] = wt_ref[...].T.astype(BF16)


def _pack_columns(w, order):
    k, n = w.shape
    starts = []
    for src, width in order:
        assert width % PACK_COLS == 0
        starts += [src + j for j in range(0, width, PACK_COLS)]
    n_out = len(starts) * PACK_COLS

    unit = 32
    assert all(s % unit == 0 for s in starts)

    def src_row(j):
        row = jnp.int32(starts[-1] // unit)
        for idx in range(len(starts) - 2, -1, -1):
            row = jnp.where(j == idx, starts[idx] // unit, row)
        return row * unit

    return pl.pallas_call(
        _pack_kernel,
        out_shape=jax.ShapeDtypeStruct((k, n_out), BF16),
        grid=(len(starts),),
        in_specs=[pl.BlockSpec((pl.Element(PACK_COLS), pl.Element(k)), lambda j: (src_row(j), 0))],
        out_specs=pl.BlockSpec((k, PACK_COLS), lambda j: (0, j)),
        compiler_params=_cparams(("parallel",)),
        name="pack_columns",
    )(w.T)


def _gate_weights_kernel(wt_ref, o_ref):
    g = wt_ref[...]
    pad = jnp.zeros((LANES - g.shape[0], g.shape[1]), F32)
    w = jnp.concatenate([g[8:16], g[24:32], g[0:8], g[16:24], pad], axis=0).T
    o_ref[...] = jnp.concatenate(_split_bf16(w), axis=1)


def _gate_weights(w, start):
    k = w.shape[0]
    return pl.pallas_call(
        _gate_weights_kernel,
        out_shape=jax.ShapeDtypeStruct((k, 2 * LANES), BF16),
        grid=(1,),
        in_specs=[pl.BlockSpec((pl.Element(4 * N_HEADS), pl.Element(k)), lambda j: (start, 0))],
        out_specs=pl.BlockSpec((k, 2 * LANES), lambda j: (0, 0)),
        name="gate_weights",
    )(w.T)


IN_TM = 512
IN_TN = 512
LOG2E = 1.4426950408889634
N_HD = 2 * N_HEADS


def _fwd_lanes(shape):
    lane = lax.broadcasted_iota(jnp.int32, shape, len(shape) - 1)
    return (lane & N_HEADS) == 0


def _inproj_kernel(x_ref, mod_ref, g_ref, w_ref, b_ref, cs_ref, wg_ref, bg_ref,
                   p3_ref, bcum_ref, imb_ref, amax_ref, *, bias_starts):
    x = x_ref[...]
    h = x * _rms_scale(x) * g_ref[...]
    h = h * (1.0 + mod_ref[0, 1:2, :]) + mod_ref[0, 0:1, :]
    hb = h.astype(BF16)
    for n in range(D_MAIN // IN_TN):
        sl = slice(n * IN_TN, (n + 1) * IN_TN)
        bias = b_ref[:, bias_starts[n]:bias_starts[n] + IN_TN]
        acc = (_dot(hb, w_ref[:, sl]) + bias) * cs_ref[:, sl]
        accb = acc.astype(BF16)
        for j in range(IN_TN // LANES):
            p3_ref[n * (IN_TN // LANES) + j] = accb[:, j * LANES:(j + 1) * LANES]

    h_lo = (h - hb.astype(F32)).astype(BF16)
    gates = _dot3_packed(hb, h_lo, wg_ref[...]) + bg_ref[...]
    ls = _log_sigmoid(gates)
    gi = pltpu.roll(gates, LANES - N_HD, 1)
    row = lax.broadcasted_iota(jnp.int32, (CHUNK, LANES), 0)
    fwd = _fwd_lanes((CHUNK, LANES))
    for c in range(IN_TM // CHUNK):
        rows = slice(c * CHUNK, (c + 1) * CHUNK)
        cf = ls[rows]
        cb = cf
        s = 1
        while s < CHUNK:
            cf = cf + jnp.where(row >= s, pltpu.roll(cf, s, 0), 0.0)
            cb = cb + jnp.where(row < CHUNK - s, pltpu.roll(cb, CHUNK - s, 0), 0.0)
            s *= 2
        b = jnp.where(fwd, cf, cb)
        imb = gi[rows] - b
        mf = imb
        mb = imb
        s = 1
        while s < CHUNK:
            mf = jnp.maximum(mf, jnp.where(row >= s, pltpu.roll(mf, s, 0), -jnp.inf))
            mb = jnp.maximum(mb, jnp.where(row < CHUNK - s, pltpu.roll(mb, CHUNK - s, 0), -jnp.inf))
            s *= 2
        bcum_ref[rows, :] = b
        imb_ref[rows, :] = imb
        amax_ref[rows, :] = b + jnp.where(fwd, mf, mb)


def _in_projection(x2d, mod3, mod_row_fn, g_norm1, wmain, b_all, bias_starts, colscale, wg, bg):
    n = x2d.shape[0]
    assert len(bias_starts) == D_MAIN // IN_TN
    return pl.pallas_call(
        functools.partial(_inproj_kernel, bias_starts=bias_starts),
        out_shape=(jax.ShapeDtypeStruct((N_SLABS, n, LANES), BF16),) + (jax.ShapeDtypeStruct((n, LANES), F32),) * 3,
        grid=(n // IN_TM,),
        in_specs=[
            pl.BlockSpec((IN_TM, D_MODEL), lambda i: (i, 0)),
            pl.BlockSpec((1, 6, D_MODEL), lambda i: (mod_row_fn(i * IN_TM), 0, 0)),
            _const_spec((1, D_MODEL)),
            _const_spec((D_MODEL, D_MAIN)),
            _const_spec(b_all.shape),
            _const_spec((1, D_MAIN)),
            _const_spec((D_MODEL, 2 * LANES)),
            _const_spec((1, LANES)),
        ],
        out_specs=(pl.BlockSpec((N_SLABS, IN_TM, LANES), lambda i: (0, i, 0)),)
        + (pl.BlockSpec((IN_TM, LANES), lambda i: (i, 0)),) * 3,
        compiler_params=_cparams(("parallel",)),
        name="in_projection",
    )(x2d, mod3, g_norm1, wmain, b_all, colscale, wg, bg)


GP_ROWS = 1024
GP_NCH = GP_ROWS // CHUNK
Q_C2, Q_EM, Q_WI, Q_WC = 0, 1, 2, 3


def _gate_prep_kernel(*refs, carry):
    if carry:
        b_ref, imb_ref, a_ref, m0_ref, cols_ref, rw_ref, cs_ref = refs
    else:
        b_ref, imb_ref, a_ref, cols_ref, rw_ref, cs_ref = refs
    fwd1 = _fwd_lanes((1, LANES))
    lane = lax.broadcasted_iota(jnp.int32, (CHUNK, LANES), 1)
    neg = jnp.full((1, LANES), NEG, F32)

    def at_scan_end(ref, c):
        return jnp.where(fwd1, ref[(c + 1) * CHUNK - 1:(c + 1) * CHUNK, :], ref[c * CHUNK:c * CHUNK + 1, :])

    b_last = [at_scan_end(b_ref, c) for c in range(GP_NCH)]
    a_last = [at_scan_end(a_ref, c) for c in range(GP_NCH)]
    m_start = m0_ref[0] if carry else neg
    m_prev = [[None] * GP_NCH, [None] * GP_NCH]
    m_new = [[None] * GP_NCH, [None] * GP_NCH]
    for d, order in enumerate((range(GP_NCH), range(GP_NCH - 1, -1, -1))):
        m = m_start
        for c in order:
            m_prev[d][c] = m
            m_new[d][c] = jnp.maximum(b_last[c] + m, a_last[c])
            m = m_new[d][c] if carry else neg

    row_src = []
    for c in range(GP_NCH):
        rows = slice(c * CHUNK, (c + 1) * CHUNK)
        mp = jnp.where(fwd1, m_prev[0][c], m_prev[1][c])
        mn = jnp.where(fwd1, m_new[0][c], m_new[1][c])
        b = b_ref[rows, :]
        imb = imb_ref[rows, :]
        inter = b + mp
        m_t = jnp.maximum(inter, a_ref[rows, :])
        c2 = (b - m_t) * LOG2E
        em = jnp.exp(-m_t)
        wc = jnp.exp(b_last[c] - mn + imb)
        packed = jnp.where(lane < N_HD, c2, pltpu.roll(em, N_HD * Q_EM, 1))
        if carry:
            wi = jnp.exp(inter - m_t)
            packed = jnp.where(lane < N_HD * Q_WI, packed, pltpu.roll(wi, N_HD * Q_WI, 1))
        cols_ref[rows, :] = jnp.where(lane < N_HD * Q_WC, packed, pltpu.roll(wc, N_HD * Q_WC, 1))
        row_src.append(jnp.where(lane < N_HD, imb * (-LOG2E), pltpu.roll(wc, N_HD, 1)))
        cs_ref[c, 0:1, :] = jnp.exp(b_last[c] + mp - mn)
        cs_ref[c, 1:2, :] = mn

    rows_t = jnp.concatenate(row_src, axis=0).T
    for r in range(2 * N_HD):
        rw_ref[r] = rows_t[r:r + 1]


def _gate_prep(bcum, imb, amax, m0=None):
    n = bcum.shape[0]
    carry = m0 is not None
    blk = pl.BlockSpec((GP_ROWS, LANES), lambda i: (i, 0))
    in_specs = [blk, blk, blk]
    args = [bcum, imb, amax]
    if carry:
        in_specs.append(pl.BlockSpec((1, 1, LANES), lambda i: (i, 0, 0)))
        args.append(m0)
    return pl.pallas_call(
        functools.partial(_gate_prep_kernel, carry=carry),
        out_shape=(jax.ShapeDtypeStruct((n, LANES), F32), jax.ShapeDtypeStruct((2 * N_HD, 1, n), F32),
                   jax.ShapeDtypeStruct((n // CHUNK, 2, LANES), F32)),
        grid=(n // GP_ROWS,),
        in_specs=in_specs,
        out_specs=(blk, pl.BlockSpec((2 * N_HD, 1, GP_ROWS), lambda i: (0, 0, i)),
                   pl.BlockSpec((GP_NCH, 2, LANES), lambda i: (i, 0, 0))),
        compiler_params=_cparams(("parallel",)),
        name="gate_prep",
    )(*args)


ML_ROWS = 1024
ML_SEQS = 4
ML_ROWS_FREE = 4096


def _pick_col(blk, lane_iota, idx):
    return jnp.sum(jnp.where(lane_iota == idx, blk, 0.0), axis=1, keepdims=True)


def _mlstm_kernel(*refs, carry, n_chunks, n_seqs):
    if carry:
        (q_ref, k_ref, v_ref, o_ref, cols_ref, r2f_ref, r2b_ref, gn_ref, cs_ref, c0_ref, n0_ref,
         ya_ref, s_scr, h_scr) = refs
    else:
        (q_ref, k_ref, v_ref, o_ref, cols_ref, r2f_ref, r2b_ref, gn_ref, wrf_ref, wrb_ref,
         ya_ref, cst_ref, nst_ref, s_scr, h_scr) = refs
    head = pl.program_id(1)
    lane = lax.broadcasted_iota(jnp.int32, (CHUNK, LANES), 1)
    lane1 = lax.broadcasted_iota(jnp.int32, (1, LANES), 1)
    r_idx = lax.broadcasted_iota(jnp.int32, (CHUNK, CHUNK), 0)
    c_idx = lax.broadcasted_iota(jnp.int32, (CHUNK, CHUNK), 1)
    ones = jnp.ones((CHUNK, HEAD_DIM), BF16)
    tn_dims = (((0,), (0,)), ((), ()))

    for d in range(2):
        mask = (c_idx <= r_idx) if d == 0 else (c_idx >= r_idx)
        r2_ref = r2f_ref if d == 0 else r2b_ref
        base = N_HEADS * d + head
        per_seq = n_chunks // n_seqs
        local = range(per_seq) if d == 0 else range(per_seq - 1, -1, -1)
        for sq, step, c in [(sq, step, sq * per_seq + lc) for sq in range(n_seqs) for step, lc in enumerate(local)]:
            if carry and step == 0:
                n_rep = jnp.broadcast_to(n0_ref[sq, d, 0], (HEAD_DIM, HEAD_DIM)).T
                cn = jnp.concatenate([c0_ref[sq, 0, d, 0], n_rep], axis=1)
            rows = slice(c * CHUNK, (c + 1) * CHUNK)
            q = q_ref[0, rows, :]
            k = k_ref[0, rows, :]
            v = v_ref[0, rows, :]
            v1 = jnp.concatenate([v, ones], axis=1)
            if d == 0:
                s = lax.dot_general(q, k, (((1,), (1,)), ((), ())), preferred_element_type=F32)
                s_scr[rows, :] = s
            else:
                s = s_scr[rows, :]
            blk = cols_ref[rows, :]
            arg = _pick_col(blk, lane, base + N_HD * Q_C2) - r2_ref[0, :, rows]
            sw = s * jnp.exp2(jnp.where(mask, arg, NEG))
            nd = _dot(sw.astype(BF16), v1)
            if carry:
                nd = nd + _pick_col(blk, lane, base + N_HD * Q_WI) * _dot(q, cn.astype(BF16))
            hdir = nd[:, :HEAD_DIM] / jnp.maximum(jnp.abs(nd[:, HEAD_DIM:]), _pick_col(blk, lane, base + N_HD * Q_EM))
            if d == 0:
                h_scr[rows, :] = hdir
            else:
                h_scr[rows, :] = h_scr[rows, :] + hdir

            if carry and step == per_seq - 1:
                continue
            w_col = _pick_col(blk, lane, base + N_HD * Q_WC)
            if carry:
                u = lax.dot_general(k, (w_col * v1.astype(F32)).astype(BF16), tn_dims,
                                    preferred_element_type=F32)
                decay = jnp.sum(jnp.where(lane1 == base, cs_ref[c, 0:1, :], 0.0), axis=1, keepdims=True)
                cn = decay * cn + u
            else:
                w_row = (wrf_ref if d == 0 else wrb_ref)[0, :, rows]
                w8 = jnp.broadcast_to(w_row, (8, CHUNK)).astype(BF16)
                cst_ref[c, 0, d, 0] = lax.dot_general(k, (w_col * v.astype(F32)).astype(BF16), tn_dims,
                                                      preferred_element_type=F32)
                nst_ref[c, d, 0] = _dot(w8, k)[0:1, :]

    hh = h_scr[...]
    hn = hh * _rms_scale(hh) * gn_ref[0]
    ya_ref[0] = (hn * jax.nn.sigmoid(o_ref[0].astype(F32))).astype(BF16)


def _mlstm(p3, cols, rows, cs, gn3, state=None):
    n = p3.shape[1]
    carry = state is not None
    n_seqs = ML_SEQS if carry else 1
    rows_step = ML_ROWS * ML_SEQS if carry else ML_ROWS_FREE
    n_chunks = rows_step // CHUNK
    nblk = n // rows_step

    def slab(base):
        return pl.BlockSpec((1, rows_step, LANES), lambda g, h: (base + h, g, 0))

    def row(base):
        return pl.BlockSpec((1, 1, rows_step), lambda g, h: (base + h, 0, g))

    in_specs = [
        slab(SLAB_Q), slab(SLAB_K), slab(SLAB_V), slab(SLAB_O),
        pl.BlockSpec((rows_step, LANES), lambda g, h: (g, 0)),
        row(0), row(N_HEADS),
        pl.BlockSpec((1, 1, LANES), lambda g, h: (h, 0, 0)),
    ]
    args = [p3, p3, p3, p3, cols, rows, rows, gn3]
    ya_shape = jax.ShapeDtypeStruct((N_HEADS, n, LANES), BF16)
    ya_spec = pl.BlockSpec((1, rows_step, LANES), lambda g, h: (h, g, 0))
    state_spec = pl.BlockSpec((n_seqs, 1, 2, 1, HEAD_DIM, HEAD_DIM), lambda g, h: (g, 0, 0, h, 0, 0))
    if carry:
        c0, n0 = state
        in_specs += [pl.BlockSpec((n_chunks, 2, LANES), lambda g, h: (g, 0, 0)), state_spec,
                     pl.BlockSpec((n_seqs, 2, 1, 1, HEAD_DIM), lambda g, h: (g, 0, h, 0, 0))]
        args += [cs, c0, n0]
        out_shape = ya_shape
        out_specs = ya_spec
    else:
        in_specs += [row(N_HD), row(N_HD + N_HEADS)]
        args += [rows, rows]
        nseq = n // CHUNK
        out_shape = (ya_shape,
                     jax.ShapeDtypeStruct((nseq, 1, 2, N_HEADS, HEAD_DIM, HEAD_DIM), F32),
                     jax.ShapeDtypeStruct((nseq, 2, N_HEADS, 1, HEAD_DIM), F32))
        out_specs = (ya_spec,
                     pl.BlockSpec((n_chunks, 1, 2, 1, HEAD_DIM, HEAD_DIM), lambda g, h: (g, 0, 0, h, 0, 0)),
                     pl.BlockSpec((n_chunks, 2, 1, 1, HEAD_DIM), lambda g, h: (g, 0, h, 0, 0)))
    return pl.pallas_call(
        functools.partial(_mlstm_kernel, carry=carry, n_chunks=n_chunks, n_seqs=n_seqs),
        out_shape=out_shape,
        grid=(nblk, N_HEADS),
        in_specs=in_specs,
        out_specs=out_specs,
        scratch_shapes=[pltpu.VMEM((rows_step, CHUNK), F32), pltpu.VMEM((rows_step, HEAD_DIM), F32)],
        compiler_params=_cparams(("parallel", "parallel")),
        name="mlstm_latent" if carry else "mlstm_prompt",
    )(*args)


def _pool_kernel(u_ref, a_ref, ic_ref, wpg_ref, ps_ref, yb_ref, *, seq_len, n_seq, rows_2d):
    piece = seq_len if rows_2d is None else GRID_W
    n_piece = n_seq if rows_2d is None else rows_2d
    for g in range(N_GROUPS):
        u = jnp.concatenate([u_ref[g, i * piece:(i + 1) * piece, :] for i in range(n_piece)], axis=1)
        win = _dot(a_ref[g], u)
        if rows_2d is None:
            inv_cnt = jnp.concatenate([ic_ref[g]] * n_piece, axis=1)
        else:
            run = [jnp.zeros((piece, LANES), F32)]
            for r in range(rows_2d):
                run.append(run[-1] + win[:, r * LANES:(r + 1) * LANES])
            half = POOL_WINDOWS[g] // 2
            win = jnp.concatenate([run[min(r - half + POOL_WINDOWS[g], rows_2d)] - run[max(r - half, 0)]
                                   for r in range(rows_2d)], axis=1)
            inv_cnt = jnp.concatenate([ic_ref[g, r * piece:(r + 1) * piece, :] for r in range(rows_2d)], axis=1)
        p = (win * inv_cnt - u.astype(F32)).astype(BF16)
        p = jnp.concatenate([p[:, i * LANES:(i + 1) * LANES] for i in range(n_piece)], axis=0)
        yb_ref[g] = (_dot(p, wpg_ref[g]) * ps_ref[g]).astype(BF16)


def _pool(p3, amat, invc, wpg, pscale, seq_len, n_seq, rows_2d):
    n = p3.shape[1]
    rows = seq_len * n_seq
    return pl.pallas_call(
        functools.partial(_pool_kernel, seq_len=seq_len, n_seq=n_seq, rows_2d=rows_2d),
        out_shape=jax.ShapeDtypeStruct((N_GROUPS, n, LANES), BF16),
        grid=(n // rows,),
        in_specs=[
            pl.BlockSpec((N_GROUPS, rows, LANES), lambda i: (SLAB_POOL // N_GROUPS, i, 0)),
            _const_spec(amat.shape),
            _const_spec((N_GROUPS, seq_len, LANES)),
            _const_spec((N_GROUPS, LANES, LANES)),
            _const_spec((N_GROUPS, 1, LANES)),
        ],
        out_specs=pl.BlockSpec((N_GROUPS, rows, LANES), lambda i: (0, i, 0)),
        compiler_params=_cparams(("parallel",)),
        name="pool",
    )(p3, amat, invc, wpg, pscale)


def _window_matrix(size, w):
    idx = np.arange(size)
    lo = np.clip(idx - w // 2, 0, size)
    hi = np.clip(idx - w // 2 + w, 0, size)
    s = np.arange(size)[None, :]
    a = ((s >= lo[:, None]) & (s < hi[:, None])).astype(np.float32)
    return a, (hi - lo).astype(np.float32)


def _pool_constants(seq_len, rows_2d):
    mats, invs = [], []
    for w in POOL_WINDOWS:
        if rows_2d is None:
            a, cnt = _window_matrix(seq_len, w)
        else:
            _, cr = _window_matrix(rows_2d, w)
            a, cc = _window_matrix(GRID_W, w)
            cnt = np.kron(cr, cc)
        mats.append(a)
        invs.append(np.repeat((1.0 / cnt)[:, None], LANES, axis=1))
    return (jnp.asarray(np.stack(mats), dtype=BF16), jnp.asarray(np.stack(invs), dtype=F32))


MIX_TM = 512


def _mix_kernel(x_ref, mod_ref, ya_ref, yb_ref, ga_ref, gb_ref, wa_ref, wb_ref, wo_ref,
                g2_ref, wr_ref, br_ref, x1_ref, h2_ref, info_ref):
    ya = jnp.concatenate([ya_ref[i] for i in range(N_HEADS)], axis=1)
    yb = jnp.concatenate([yb_ref[i] for i in range(N_GROUPS)], axis=1)
    ga = jnp.concatenate([ga_ref[i] for i in range(8)], axis=1).astype(F32)
    gb = jnp.concatenate([gb_ref[i] for i in range(8)], axis=1).astype(F32)
    merged = jax.nn.sigmoid(ga) * _dot(ya, wa_ref[...]) + jax.nn.sigmoid(gb) * _dot(yb, wb_ref[...])
    out = _dot(merged.astype(BF16), wo_ref[...])
    x1 = x_ref[...] + mod_ref[0, 2:3, :] * out
    x1_ref[...] = x1

    h2 = x1 * _rms_scale(x1) * g2_ref[...]
    h2 = h2 * (1.0 + mod_ref[0, 4:5, :]) + mod_ref[0, 3:4, :]
    h_hi, h_lo = _split_bf16(h2)
    packed = _pack_bf16_pairs(h_hi.astype(F32))
    h2_ref[0:MIX_TM, :] = packed[:, :SC_ROW]
    h2_ref[MIX_TM:2 * MIX_TM, :] = packed[:, SC_ROW:]
    logits = _dot3_packed(h_hi, h_lo, wr_ref[...]) + br_ref[...]
    first, second, w1, w2 = _route_rows(logits.T)
    zero = jnp.zeros_like(w1)
    info_ref[...] = jnp.concatenate([first, second, w1, w2, zero, zero, zero, zero], axis=0)


def _mix(x2d, mod3, mod_row_fn, ya, yb, p3, wa, wb, wo, g_norm2, wr, br):
    n = x2d.shape[0]
    return pl.pallas_call(
        _mix_kernel,
        out_shape=(jax.ShapeDtypeStruct((n, D_MODEL), F32),
                   jax.ShapeDtypeStruct((2 * n, SC_ROW), jnp.int32), jax.ShapeDtypeStruct((8, n), F32)),
        grid=(n // MIX_TM,),
        in_specs=[
            pl.BlockSpec((MIX_TM, D_MODEL), lambda i: (i, 0)),
            pl.BlockSpec((1, 6, D_MODEL), lambda i: (mod_row_fn(i * MIX_TM), 0, 0)),
            pl.BlockSpec((N_HEADS, MIX_TM, LANES), lambda i: (0, i, 0)),
            pl.BlockSpec((N_GROUPS, MIX_TM, LANES), lambda i: (0, i, 0)),
            pl.BlockSpec((8, MIX_TM, LANES), lambda i: (SLAB_GA // 8, i, 0)),
            pl.BlockSpec((8, MIX_TM, LANES), lambda i: (SLAB_GB // 8, i, 0)),
            _const_spec((D_MODEL, D_MODEL)),
            _const_spec((N_GROUPS * LANES, D_MODEL)),
            _const_spec((D_MODEL, D_MODEL)),
            _const_spec((1, D_MODEL)),
            _const_spec((D_MODEL, 2 * LANES)),
            _const_spec((1, LANES)),
        ],
        out_specs=(pl.BlockSpec((MIX_TM, D_MODEL), lambda i: (i, 0)),
                   pl.BlockSpec((2 * MIX_TM, SC_ROW), lambda i: (i, 0)),
                   pl.BlockSpec((8, MIX_TM), lambda i: (0, i))),
        compiler_params=_cparams(("parallel",)),
        name="mix",
    )(x2d, mod3, ya, yb, p3, p3, wa, wb, wo, g_norm2, wr, br)


PLAN_TB = 1024
ROUTER_ROWS = 4 + N_EXPERTS
SLOT_TM = 1024
SC_ROW = 256
SC_WIN = 128
HI_MASK = -65536


def _pack_bf16_pairs(x):
    bits = pltpu.bitcast(x, jnp.int32)
    half = D_MODEL // 2
    return jnp.bitwise_or(jnp.bitwise_and(lax.shift_right_logical(bits[:, :half], 16), 0xFFFF),
                          jnp.bitwise_and(bits[:, half:], HI_MASK))


def _unpack_bf16_pairs(lo_words, hi_words):
    def low(w):
        return pltpu.bitcast(lax.shift_left(w, 16), F32)

    def high(w):
        return pltpu.bitcast(jnp.bitwise_and(w, HI_MASK), F32)

    return jnp.concatenate([low(lo_words), low(hi_words), high(lo_words), high(hi_words)], axis=1).astype(BF16)


def _route_rows(lt):
    lg = [lt[i:i + 1] for i in range(N_GROUPS)]
    mx = jnp.maximum(jnp.maximum(lg[0], lg[1]), jnp.maximum(lg[2], lg[3]))
    p_top = 1.0 / (jnp.exp(lg[0] - mx) + jnp.exp(lg[1] - mx) + jnp.exp(lg[2] - mx) + jnp.exp(lg[3] - mx))
    gidx = jnp.zeros(lg[0].shape, jnp.int32)
    best = lg[0]
    for i in range(1, N_GROUPS):
        upd = lg[i] > best
        gidx = jnp.where(upd, i, gidx)
        best = jnp.where(upd, lg[i], best)

    def expert_row(g, e):
        r = N_GROUPS + g * EXPERTS_PER_GROUP + e
        return lt[r:r + 1]

    le = [jnp.where(gidx == 0, expert_row(0, e),
                    jnp.where(gidx == 1, expert_row(1, e),
                              jnp.where(gidx == 2, expert_row(2, e), expert_row(3, e))))
          for e in range(EXPERTS_PER_GROUP)]
    v1 = jnp.maximum(jnp.maximum(le[0], le[1]), jnp.maximum(le[2], le[3]))
    i1 = jnp.full(v1.shape, EXPERTS_PER_GROUP - 1, jnp.int32)
    for e in range(EXPERTS_PER_GROUP - 2, -1, -1):
        i1 = jnp.where(le[e] == v1, e, i1)
    le2 = [jnp.where(i1 == e, -jnp.inf, le[e]) for e in range(EXPERTS_PER_GROUP)]
    v2 = jnp.maximum(jnp.maximum(le2[0], le2[1]), jnp.maximum(le2[2], le2[3]))
    i2 = jnp.full(v2.shape, EXPERTS_PER_GROUP - 1, jnp.int32)
    for e in range(EXPERTS_PER_GROUP - 2, -1, -1):
        i2 = jnp.where(le2[e] == v2, e, i2)
    e2 = jnp.exp(v2 - v1)
    first = (gidx * EXPERTS_PER_GROUP + i1).astype(F32)
    second = (gidx * EXPERTS_PER_GROUP + i2).astype(F32)
    return first, second, p_top / (1.0 + e2), p_top * e2 / (1.0 + e2)


def _plan_kernel(info_ref, before_ref, sidx_ref, gidx_ref, tmap_ref, rank_scr, count_scr, *, n_blocks, min_tiles,
                 slot_tm):
    step = pl.program_id(0)
    expert_id = lax.broadcasted_iota(jnp.int32, (N_EXPERTS, PLAN_TB), 0).astype(F32)

    @pl.when(step == 0)
    def _():
        count_scr[...] = jnp.zeros_like(count_scr)

    @pl.when(step < n_blocks)
    def _():
        cols = pl.ds(pl.multiple_of(step * PLAN_TB, PLAN_TB), PLAN_TB)
        sel = jnp.where((info_ref[0:1, cols] == expert_id) | (info_ref[1:2, cols] == expert_id), 1.0, 0.0)
        selb = sel.astype(BF16)
        rank_scr[:, cols] = _dot(selb, before_ref[...]) + count_scr[:, 0:1]
        count_scr[...] += _dot(selb, jnp.ones((PLAN_TB, LANES), BF16))

    @pl.when(step == n_blocks)
    def _():
        counts = count_scr[...]
        padded = jnp.floor((counts + (slot_tm - 1)) * (1.0 / slot_tm)) * slot_tm
        padded = jnp.maximum(padded, float(min_tiles * slot_tm))
        starts, ends, run = [], [], jnp.zeros((1, LANES), F32)
        for e in range(N_EXPERTS):
            starts.append(run)
            run = run + padded[e:e + 1]
            ends.append(run)
        n = info_ref.shape[1]
        shift = slot_tm.bit_length() - 1
        for k in range(2):
            chosen = info_ref[k:k + 1, :]
            slot = jnp.zeros((1, n), F32)
            for e in range(N_EXPERTS):
                slot = jnp.where(chosen == float(e), starts[e][:, 0:1] + rank_scr[e:e + 1, :], slot)
            slot = slot.astype(jnp.int32)
            low = jnp.left_shift(jnp.right_shift(slot, shift), shift + 1) + jnp.bitwise_and(slot, slot_tm - 1)
            for half in range(2):
                rows = low + half * slot_tm
                for c in range(n // LANES):
                    piece = rows[:, c * LANES:(c + 1) * LANES]
                    t, j = divmod(c, MIX_TM // LANES)
                    r = ((k * (n // MIX_TM) + t) * 2 + half) * (MIX_TM // LANES) + j
                    sidx_ref[r:r + 1, :] = piece
                    t, j = divmod(c, FIN_TM // LANES)
                    r = ((t * 2 + k) * 2 + half) * (FIN_TM // LANES) + j
                    gidx_ref[r:r + 1, :] = piece
        tile_start = lax.broadcasted_iota(jnp.int32, (1, LANES), 1).astype(F32) * slot_tm
        used = jnp.where(tile_start < run, 1.0, 0.0)
        tile_start = jnp.minimum(tile_start, run - slot_tm)
        owner = jnp.zeros((1, LANES), F32)
        for e in range(N_EXPERTS - 1):
            owner = owner + jnp.where(ends[e] <= tile_start, 1.0, 0.0)
        tile = tile_start * (1.0 / slot_tm)
        tmap_ref[...] = jnp.concatenate([owner, used, tile] + [jnp.zeros((1, LANES), F32)] * 5,
                                        axis=0).astype(jnp.int32)


def _plan(info, min_tiles, slot_tm):
    n = info.shape[1]
    n_blocks = n // PLAN_TB
    t_i = lax.broadcasted_iota(jnp.int32, (PLAN_TB, PLAN_TB), 0)
    t_j = lax.broadcasted_iota(jnp.int32, (PLAN_TB, PLAN_TB), 1)
    before = (t_i < t_j).astype(BF16)
    return pl.pallas_call(
        functools.partial(_plan_kernel, n_blocks=n_blocks, min_tiles=min_tiles, slot_tm=slot_tm),
        out_shape=(jax.ShapeDtypeStruct((4 * n // LANES, LANES), jnp.int32),
                   jax.ShapeDtypeStruct((4 * n // LANES, LANES), jnp.int32),
                   jax.ShapeDtypeStruct((8, LANES), jnp.int32)),
        grid=(n_blocks + 1,),
        in_specs=[_const_spec((8, n)), _const_spec((PLAN_TB, PLAN_TB))],
        out_specs=(pl.BlockSpec((4 * n // LANES, LANES), lambda i: (0, 0)),
                   pl.BlockSpec((4 * n // LANES, LANES), lambda i: (0, 0)),
                   pl.BlockSpec((8, LANES), lambda i: (0, 0))),
        scratch_shapes=[pltpu.VMEM((N_EXPERTS, n), F32), pltpu.VMEM((N_EXPERTS, LANES), F32)],
        compiler_params=_cparams(("arbitrary",)),
        name="plan",
    )(info, before)


def _sc_mesh():
    return plsc.VectorSubcoreMesh(core_axis_name="core", subcore_axis_name="subcore")


def _sc_scatter_rows(x, idx, n_out):
    n_idx = idx.shape[0]
    src_blocks = x.shape[0] // SC_WIN

    @pl.kernel(out_type=jax.ShapeDtypeStruct((n_out, SC_ROW), x.dtype), mesh=_sc_mesh(), scratch_types=[])
    def scatter_kernel(x_hbm, i_hbm, o_hbm):
        def body(x_vmem, i_vmem):
            pltpu.sync_copy(x_vmem, o_hbm.at[i_vmem.at[0]])

        pltpu.emit_pipeline(
            body,
            grid=(n_idx // SC_WIN,),
            in_specs=[pl.BlockSpec((SC_WIN, SC_ROW), index_map=lambda i: (i % src_blocks, 0)),
                      pl.BlockSpec((1, SC_WIN), index_map=lambda i: (0, i))],
            out_specs=[],
            core_axis_name=("core", "subcore"),
            dimension_semantics=(pltpu.PARALLEL,),
        )(x_hbm, i_hbm)

    return scatter_kernel(x, idx.reshape(1, n_idx))


def _sc_gather_rows(x, idx):
    n_idx = idx.shape[0]

    @pl.kernel(out_type=jax.ShapeDtypeStruct((n_idx, SC_ROW), x.dtype), mesh=_sc_mesh(), scratch_types=[])
    def gather_kernel(x_hbm, i_hbm, o_hbm):
        def body(i_vmem, o_vmem):
            pltpu.sync_copy(x_hbm.at[i_vmem.at[0]], o_vmem)

        pltpu.emit_pipeline(
            body,
            grid=(n_idx // SC_WIN,),
            in_specs=[pl.BlockSpec((1, SC_WIN), index_map=lambda i: (0, i))],
            out_specs=[pl.BlockSpec((SC_WIN, SC_ROW), index_map=lambda i: (i, 0))],
            core_axis_name=("core", "subcore"),
            dimension_semantics=(pltpu.PARALLEL,),
        )(i_hbm, o_hbm)

    return gather_kernel(x, idx.reshape(1, n_idx))


def _experts_kernel(tmap_ref, xs_ref, wg_ref, wu_ref, wd_ref, ys_ref, *wb_refs, cast_weights, slot_tm, step=None):
    j = pl.program_id(0) if step is None else step
    if cast_weights:
        wg_b, wu_b, wd_b = wb_refs

        @pl.when((j == 0) | (tmap_ref[0, j] != tmap_ref[0, jnp.maximum(j - 1, 0)]))
        def _():
            wg_b[0] = wg_ref[0].astype(BF16)
            wu_b[0] = wu_ref[0].astype(BF16)
            wd_b[0] = wd_ref[0].astype(BF16)
    else:
        wg_b, wu_b, wd_b = wg_ref, wu_ref, wd_ref

    @pl.when(tmap_ref[1, j] > 0)
    def _():
        x = _unpack_bf16_pairs(xs_ref[0:slot_tm, :], xs_ref[slot_tm:2 * slot_tm, :])
        a = _dot(x, wg_b[0])
        u = _dot(x, wu_b[0])
        act = (a * jax.nn.sigmoid(a) * u).astype(BF16)
        y = _dot(act, wd_b[0]).astype(BF16).astype(F32)
        packed = _pack_bf16_pairs(y)
        ys_ref[0:slot_tm, :] = packed[:, :SC_ROW]
        ys_ref[slot_tm:2 * slot_tm, :] = packed[:, SC_ROW:]


def _experts_deep(xs, tmap, wg, wu, wd, slot_tm, n_tiles):
    def outer(tmap_ref, xs_hbm, wg_hbm, wu_hbm, wd_hbm, ys_hbm, step_ref):
        step_ref[0] = 0

        def body(xs_ref, wg_ref, wu_ref, wd_ref, ys_ref):
            j = step_ref[0]
            step_ref[0] = j + 1
            _experts_kernel(tmap_ref, xs_ref, wg_ref, wu_ref, wd_ref, ys_ref, cast_weights=False, slot_tm=slot_tm,
                            step=j)

        tile = (2 * slot_tm, SC_ROW)
        pltpu.emit_pipeline(
            body,
            grid=(n_tiles,),
            in_specs=[pl.BlockSpec(tile, lambda j: (tmap_ref[2, j], 0), pipeline_mode=pl.Buffered(3)),
                      pl.BlockSpec((1, D_MODEL, D_EXPERT), lambda j: (tmap_ref[0, j], 0, 0)),
                      pl.BlockSpec((1, D_MODEL, D_EXPERT), lambda j: (tmap_ref[0, j], 0, 0)),
                      pl.BlockSpec((1, D_EXPERT, D_MODEL), lambda j: (tmap_ref[0, j], 0, 0))],
            out_specs=[pl.BlockSpec(tile, lambda j: (tmap_ref[2, j], 0))],
        )(xs_hbm, wg_hbm, wu_hbm, wd_hbm, ys_hbm)

    any_spec = pl.BlockSpec(memory_space=pl.ANY)
    return pl.pallas_call(
        outer,
        out_shape=jax.ShapeDtypeStruct(xs.shape, jnp.int32),
        grid_spec=pltpu.PrefetchScalarGridSpec(
            num_scalar_prefetch=1,
            grid=(1,),
            in_specs=[any_spec] * 4,
            out_specs=any_spec,
            scratch_shapes=[pltpu.SMEM((1,), jnp.int32)],
        ),
        compiler_params=_cparams(("arbitrary",)),
        name="experts_deep",
    )(tmap, xs, wg, wu, wd)


def _experts(xs, tmap, wg, wu, wd, slot_tm):
    n_tiles = xs.shape[0] // (2 * slot_tm)
    cast_weights = wg.dtype != BF16
    tile_spec = pl.BlockSpec((2 * slot_tm, SC_ROW), lambda j, tm: (tm[2, j], 0))
    w_specs = [pl.BlockSpec((1, D_MODEL, D_EXPERT), lambda j, tm: (tm[0, j], 0, 0)),
               pl.BlockSpec((1, D_MODEL, D_EXPERT), lambda j, tm: (tm[0, j], 0, 0)),
               pl.BlockSpec((1, D_EXPERT, D_MODEL), lambda j, tm: (tm[0, j], 0, 0))]
    if not cast_weights:
        return _experts_deep(xs, tmap, wg, wu, wd, slot_tm, n_tiles), ()
    out_shape = [jax.ShapeDtypeStruct(xs.shape, jnp.int32)]
    out_specs = [tile_spec]
    if cast_weights:
        out_shape += [jax.ShapeDtypeStruct(w.shape, BF16) for w in (wg, wu, wd)]
        out_specs += w_specs
    grid_spec = pltpu.PrefetchScalarGridSpec(
        num_scalar_prefetch=1,
        grid=(n_tiles,),
        in_specs=[tile_spec] + w_specs,
        out_specs=out_specs,
    )
    out = pl.pallas_call(
        functools.partial(_experts_kernel, cast_weights=cast_weights, slot_tm=slot_tm),
        out_shape=out_shape,
        grid_spec=grid_spec,
        compiler_params=_cparams(("arbitrary",)),
        name="experts",
    )(tmap, xs, wg, wu, wd)
    return out[0], tuple(out[1:])


FIN_TM = 1024


def _final_kernel(x1_ref, mod_ref, yg_ref, info_ref, gf_ref, y_ref):
    w_cols = info_ref[...].T
    moe = None
    for k in range(2):
        base = 2 * k * FIN_TM
        yk = _unpack_bf16_pairs(yg_ref[base:base + FIN_TM, :], yg_ref[base + FIN_TM:base + 2 * FIN_TM, :])
        term = w_cols[:, 2 + k:3 + k] * yk.astype(F32)
        moe = term if moe is None else moe + term
    x2 = x1_ref[...] + mod_ref[0, 5:6, :] * moe
    y_ref[...] = x2 * _rms_scale(x2) * gf_ref[...]


def _final(x1, mod3, mod_row_fn, yg, info, g_final):
    n = x1.shape[0]
    return pl.pallas_call(
        _final_kernel,
        out_shape=jax.ShapeDtypeStruct((n, D_MODEL), F32),
        grid=(n // FIN_TM,),
        in_specs=[
            pl.BlockSpec((FIN_TM, D_MODEL), lambda i: (i, 0)),
            pl.BlockSpec((1, 6, D_MODEL), lambda i: (mod_row_fn(i * FIN_TM), 0, 0)),
            pl.BlockSpec((4 * FIN_TM, SC_ROW), lambda i: (i, 0)),
            pl.BlockSpec((8, FIN_TM), lambda i: (0, i)),
            _const_spec((1, D_MODEL)),
        ],
        out_specs=pl.BlockSpec((FIN_TM, D_MODEL), lambda i: (i, 0)),
        compiler_params=_cparams(("parallel",)),
        name="final",
    )(x1, mod3, yg, info, g_final)


def _moe(x1, h2p, info, mod3, mod_row_fn, wg, wu, wd, g_final):
    n = x1.shape[0]
    slot_tm = min(SLOT_TM, 2 * n // N_EXPERTS)
    sidx, gidx, tmap = _plan(info, min_tiles=int(wg.dtype != BF16), slot_tm=slot_tm)
    n_slots = 2 * n + N_EXPERTS * slot_tm
    xs = _sc_scatter_rows(h2p, sidx.reshape(-1), 2 * n_slots)
    ys, w_bf16 = _experts(xs, tmap, wg, wu, wd, slot_tm)
    yg = _sc_gather_rows(ys, gidx.reshape(-1))
    return _final(x1, mod3, mod_row_fn, yg, info, g_final), w_bf16


def _stream(x, mod3, mod_row_fn, seq_len, rows_2d, state, w):
    bsz, t, _ = x.shape
    n = bsz * t
    x2d = x.reshape(n, D_MODEL)
    p3, bcum, imb, amax = _in_projection(x2d, mod3, mod_row_fn, w["g_norm1"], w["wmain"], w["b_in"], w["bias_starts"],
                                         w["colscale"], w["w_gates"], w["b_gates"])
    m0 = None if state is None else state[2]
    cols, rows, cs = _gate_prep(bcum, imb, amax, m0)
    ml = _mlstm(p3, cols, rows, cs, w["gn3"], None if state is None else state[:2])
    amat, invc = _pool_constants(seq_len, rows_2d)
    n_seq = max(1, 1024 // seq_len)
    yb = _pool(p3, amat, invc, w["wpg"], w["pscale"], seq_len, n_seq, rows_2d)
    ya = ml if state is not None else ml[0]
    x1, h2p, info = _mix(x2d, mod3, mod_row_fn, ya, yb, p3, w["wa"], w["wb"], w["wo"],
                         w["g_norm2"], w["wr"], w["br"])
    y, w_bf16 = _moe(x1, h2p, info, mod3, mod_row_fn, w["wg"], w["wu"], w["wd"], w["g_final"])
    return y.reshape(bsz, t, D_MODEL), ml, cs, w_bf16


def kernel(x_prompt, x_sample, state_C, state_n, state_m, c, c_ctx, w_mod, b_mod, g_norm1, w_in, b_in, gn_gain, w_pool_grp, pool_scale, w_proj_a, w_proj_b, w_out, g_norm2, w_router_grp, b_router_grp, w_router_exp, b_router_exp, w_exp_gate, w_exp_up, w_exp_down, g_final):
    bp, tp, _ = x_prompt.shape
    bs, ts, _ = x_sample.shape
    assert w_mod.shape[0] == 1, "single trunk layer"
    assert tp == CHUNK and ts == ML_ROWS and bs % ML_SEQS == 0

    cvec = jnp.zeros((8, D_MODEL), F32).at[0].set(c_ctx).at[1:1 + bs].set(c)
    mod3 = _modulation(cvec, w_mod[0], b_mod[0])

    wi, bi = w_in[0], b_in[0]
    d = D_MODEL
    o_q, o_k, o_v, o_o, o_g, o_p, o_ga, o_gb = 0, d, 2 * d, 3 * d, 4 * d, 4 * d + 32, 4 * d + 32 + 512, 4 * d + 32 + 512 + d
    order = [(o_q, d), (o_k, d), (o_v, d), (o_o, d), (o_ga, d), (o_gb, d), (o_p, 512)]
    wmain = _pack_columns(wi, order)
    bias_starts = tuple(s + j for s, l in order for j in range(0, l, IN_TN))
    colscale = jnp.ones((D_MAIN,), F32).at[SLAB_K * LANES:(SLAB_K + N_HEADS) * LANES].set(HEAD_DIM ** -0.5)
    colscale = colscale.reshape(1, D_MAIN)
    wg = _gate_weights(wi, o_g)
    gb = bi[o_g:o_g + 32]
    i_f, f_f, i_b, f_b = (slice(0, 8), slice(8, 16), slice(16, 24), slice(24, 32))
    pad_b = jnp.zeros((LANES - 2 * N_HD,), F32)
    bg = jnp.concatenate([gb[f_f], gb[f_b], gb[i_f], gb[i_b], pad_b]).reshape(1, LANES)

    wr = jnp.concatenate([w_router_grp[0], w_router_exp[0].reshape(d, N_EXPERTS),
                          jnp.zeros((d, LANES - ROUTER_ROWS), F32)], axis=1)
    br = jnp.concatenate([b_router_grp[0], b_router_exp[0].reshape(N_EXPERTS),
                          jnp.zeros((LANES - ROUTER_ROWS,), F32)]).reshape(1, LANES)
    wr = jnp.concatenate(_split_bf16(wr), axis=1)

    w = dict(
        g_norm1=g_norm1[0].reshape(1, d), wmain=wmain, b_in=b_in, bias_starts=bias_starts, colscale=colscale,
        w_gates=wg, b_gates=bg,
        gn3=gn_gain[0].reshape(N_HEADS, 1, HEAD_DIM),
        wpg=w_pool_grp[0].astype(BF16), pscale=pool_scale[0].reshape(N_GROUPS, 1, LANES),
        wa=w_proj_a[0].astype(BF16), wb=w_proj_b[0].astype(BF16), wo=w_out[0].astype(BF16),
        g_norm2=g_norm2[0].reshape(1, d), wr=wr, br=br,
        wg=w_exp_gate[0], wu=w_exp_up[0], wd=w_exp_down[0],
        g_final=g_final.reshape(1, d),
    )

    y_prompt, (_, cst, nst), cs_prompt, w_bf16 = _stream(x_prompt, mod3, lambda r: 0, tp, None, None, w)
    w = dict(w, wg=w_bf16[0], wu=w_bf16[1], wd=w_bf16[2])
    n0 = state_n[:, 0].reshape(bs, 2, N_HEADS, 1, HEAD_DIM)
    m0 = jnp.pad(state_m[:, 0].reshape(bs, N_HD), ((0, 0), (0, LANES - N_HD))).reshape(bs, 1, LANES)
    y_sample, _, _, _ = _stream(x_sample, mod3, lambda r: 1 + r // ts, ts, ts // GRID_W, (state_C, n0, m0), w)

    new_c = cst
    new_n = nst.reshape(bp, 1, 2, N_HEADS, HEAD_DIM)
    new_m = cs_prompt[:, 1, :N_HD].reshape(bp, 1, 2, N_HEADS)
    return (y_prompt, y_sample, new_c, new_n, new_m)
```
